```python
import jax, jax.numpy as jnp
from jax import lax
import numpy as np

D_MODEL = 1024
BATCH = 8
SEQ = 16384
DEPTH = 4

HEAD_DIM = 64
A_Q_HEADS = 8
A_KV_HEADS = 2
A_WINDOW = 128
B_GROUPS = ((128, 1), (512, 4), (2048, 16))
B_HEADS_PER_GROUP = 4
B_HEADS = B_HEADS_PER_GROUP * len(B_GROUPS)
N_ATTN_HEADS = A_Q_HEADS + B_HEADS
BLOCK = 128
A_Q_W = A_Q_HEADS * HEAD_DIM
A_KV_W = A_KV_HEADS * HEAD_DIM
B_W = B_HEADS * HEAD_DIM
B_OUT_W = B_HEADS_PER_GROUP * HEAD_DIM
IN_SPLITS = (A_Q_W, A_KV_W, A_KV_W, B_W, B_W, B_W, D_MODEL, D_MODEL)
IN_W = sum(IN_SPLITS)
D_FF = ((8 * D_MODEL + 3 * 256 - 1) // (3 * 256)) * 256
DN_ALPHA = (2 * DEPTH) ** 0.25
DN_BETA = (8 * DEPTH) ** -0.25
LN_EPS = 1e-5
NEG_INF = -1e30

kernel_name = "hybrid_swa_sink_dilated_gated_deepnorm"


def layer_norm(x, g, b):
    xf = x.astype(jnp.float32)
    mu = xf.mean(-1, keepdims=True)
    var = jnp.square(xf - mu).mean(-1, keepdims=True)
    y = (xf - mu) * lax.rsqrt(var + LN_EPS)
    return (y * g.astype(jnp.float32) + b.astype(jnp.float32)).astype(x.dtype)


def alibi_slopes(n):
    return jnp.exp2(-8.0 * jnp.arange(1, n + 1, dtype=jnp.float32) / n)


def banded_attention(q, k, v, slopes, max_dist, stride, sinks=None):
    bt, L, H, dh = q.shape
    hkv = k.shape[2]
    G = H // hkv
    nb = -(-L // BLOCK)
    Lp = nb * BLOCK
    q = jnp.pad(q, ((0, 0), (0, Lp - L), (0, 0), (0, 0)))
    kv_pad = ((0, 0), (BLOCK, Lp - L), (0, 0), (0, 0))
    k = jnp.pad(k, kv_pad).reshape(bt, nb + 1, BLOCK, hkv, dh)
    v = jnp.pad(v, kv_pad).reshape(bt, nb + 1, BLOCK, hkv, dh)
    kw = jnp.concatenate([k[:, :-1], k[:, 1:]], axis=2)
    vw = jnp.concatenate([v[:, :-1], v[:, 1:]], axis=2)
    qb = q.reshape(bt, nb, BLOCK, hkv, G, dh)
    s = jnp.einsum('bnqhgd,bnshd->bnhgqs', qb, kw,
                   preferred_element_type=jnp.float32) * (dh ** -0.5)
    qi = jnp.arange(BLOCK)[:, None]
    sj = jnp.arange(2 * BLOCK)[None, :]
    dist = qi + BLOCK - sj
    kpos = jnp.arange(nb)[:, None] * BLOCK + jnp.arange(2 * BLOCK)[None, :] - BLOCK
    valid = ((dist >= 0) & (dist <= max_dist))[None] & (kpos >= 0)[:, None, :]
    bias = -(slopes.astype(jnp.float32).reshape(hkv, G, 1, 1)
             * (dist * stride).astype(jnp.float32))
    s = jnp.where(valid[None, :, None, None], s + bias, NEG_INF)
    m = s.max(-1)
    if sinks is not None:
        sink = sinks.astype(jnp.float32).reshape(1, 1, hkv, G, 1)
        m = jnp.maximum(m, sink)
    e = jnp.exp(s - m[..., None])
    den = e.sum(-1)
    if sinks is not None:
        den = den + jnp.exp(sink - m)
    lse = m + jnp.log(den)
    p = (e / den[..., None]).astype(v.dtype)
    o = jnp.einsum('bnhgqs,bnshd->bnqhgd', p, vw).reshape(bt, Lp, H, dh)[:, :L]
    lse = lse.transpose(0, 1, 4, 2, 3).reshape(bt, Lp, H)[:, :L]
    return o, lse


def dilated_group(q, k, v, slopes, window, dilation):
    b, S, h, dh = q.shape
    n = S // dilation

    def fold(t):
        return t.reshape(b, n, dilation, h, dh).transpose(0, 2, 1, 3, 4).reshape(b * dilation, n, h, dh)

    o, lse = banded_attention(fold(q), fold(k), fold(v), slopes, window // dilation, dilation)
    o = o.reshape(b, dilation, n, h, dh).transpose(0, 2, 1, 3, 4).reshape(b, S, h, dh)
    lse = lse.reshape(b, dilation, n, h).transpose(0, 2, 1, 3).reshape(b, S, h)
    return o, lse


def token_mixer(u, w_in, sinks, w_a, w_b, w_o):
    b, S, _ = u.shape
    idx = list(np.cumsum(IN_SPLITS)[:-1])
    qa, ka, va, qb, kb, vb, ga, gb = jnp.split(u @ w_in, idx, axis=-1)
    slopes = alibi_slopes(N_ATTN_HEADS)
    ya, _ = banded_attention(qa.reshape(b, S, A_Q_HEADS, HEAD_DIM),
                             ka.reshape(b, S, A_KV_HEADS, HEAD_DIM),
                             va.reshape(b, S, A_KV_HEADS, HEAD_DIM),
                             slopes[:A_Q_HEADS], A_WINDOW - 1, 1, sinks)
    ya = ya.reshape(b, S, A_Q_W)
    gshape = (b, S, len(B_GROUPS), B_HEADS_PER_GROUP, HEAD_DIM)
    qb, kb, vb = qb.reshape(gshape), kb.reshape(gshape), vb.reshape(gshape)
    outs, lses = [], []
    for g, (window, dilation) in enumerate(B_GROUPS):
        lo = A_Q_HEADS + g * B_HEADS_PER_GROUP
        o, l = dilated_group(qb[:, :, g], kb[:, :, g], vb[:, :, g],
                             slopes[lo:lo + B_HEADS_PER_GROUP], window, dilation)
        outs.append(o)
        lses.append(l)
    wts = jax.nn.softmax(jnp.stack(lses), axis=0)
    yb = (jnp.stack(outs) * wts[..., None].astype(u.dtype)).sum(0).reshape(b, S, B_OUT_W)
    merged = jax.nn.sigmoid(ga) * (ya @ w_a) + jax.nn.sigmoid(gb) * (yb @ w_b)
    return merged @ w_o


def swiglu(u, w_gate, w_up, w_down):
    return (jax.nn.silu(u @ w_gate) * (u @ w_up)) @ w_down


def _fwd_setup_inputs(seed: int = 0) -> dict:
    key = jax.random.key(seed)
    ks = jax.random.split(key, 20)
    nrm = lambda k, shape, s: jax.random.normal(k, shape, jnp.float32) * s
    L, D = DEPTH, D_MODEL
    return {
        "x": nrm(ks[0], (BATCH, SEQ, D), 1.0),
        "c": nrm(ks[1], (BATCH, D), 1.0),
        "w_ada": nrm(ks[2], (L, D, 6 * D), 0.5 * D ** -0.5),
        "b_ada": nrm(ks[3], (L, 6 * D), 0.02),
        "w_in": nrm(ks[4], (L, D, IN_W), D ** -0.5),
        "sinks": nrm(ks[5], (L, A_Q_HEADS), 0.5),
        "w_a": nrm(ks[6], (L, A_Q_W, D), A_Q_W ** -0.5),
        "w_b": nrm(ks[7], (L, B_OUT_W, D), B_OUT_W ** -0.5),
        "w_o": nrm(ks[8], (L, D, D), DN_BETA * D ** -0.5),
        "ln1_g": 1.0 + nrm(ks[9], (L, D), 0.02),
        "ln1_b": nrm(ks[10], (L, D), 0.02),
        "w_gate": nrm(ks[11], (L, D, D_FF), D ** -0.5),
        "w_up": nrm(ks[12], (L, D, D_FF), D ** -0.5),
        "w_down": nrm(ks[13], (L, D_FF, D), DN_BETA * D_FF ** -0.5),
        "ln2_g": 1.0 + nrm(ks[14], (L, D), 0.02),
        "ln2_b": nrm(ks[15], (L, D), 0.02),
    }


def _fwd_reference(x, c, w_ada, b_ada, w_in, sinks, w_a, w_b, w_o, ln1_g, ln1_b,
              w_gate, w_up, w_down, ln2_g, ln2_b):
    sc = jax.nn.silu(c)
    for l in range(DEPTH):
        mod = (sc @ w_ada[l] + b_ada[l])[:, None, :]
        sh1, s1, g1, sh2, s2, g2 = jnp.split(mod, 6, axis=-1)
        u = x * (1 + s1) + sh1
        x = layer_norm(DN_ALPHA * x + g1 * token_mixer(u, w_in[l], sinks[l], w_a[l], w_b[l], w_o[l]),
                       ln1_g[l], ln1_b[l])
        u = x * (1 + s2) + sh2
        x = layer_norm(DN_ALPHA * x + g2 * swiglu(u, w_gate[l], w_up[l], w_down[l]),
                       ln2_g[l], ln2_b[l])
    return x


import jax as _jax
import jax.numpy as _jnp

TWIN_FORMAT = 'train_step'
FWD_PARAMS = ['x', 'c', 'w_ada', 'b_ada', 'w_in', 'sinks', 'w_a', 'w_b', 'w_o', 'ln1_g', 'ln1_b', 'w_gate', 'w_up', 'w_down', 'ln2_g', 'ln2_b']
TWIN_WEIGHTS = ['w_ada', 'b_ada', 'w_in', 'sinks', 'w_a', 'w_b', 'w_o', 'ln1_g', 'ln1_b', 'w_gate', 'w_up', 'w_down', 'ln2_g', 'ln2_b']
TWIN_DIFF_INPUT = 'x'
TWIN_INPUTS = ['x', 'c', 'w_ada', 'b_ada', 'w_in', 'sinks', 'w_a', 'w_b', 'w_o', 'ln1_g', 'ln1_b', 'w_gate', 'w_up', 'w_down', 'ln2_g', 'ln2_b', 'loss_target', 'm_w_ada', 'm_b_ada', 'm_w_in', 'm_sinks', 'm_w_a', 'm_w_b', 'm_w_o', 'm_ln1_g', 'm_ln1_b', 'm_w_gate', 'm_w_up', 'm_w_down', 'm_ln2_g', 'm_ln2_b', 'v_w_ada', 'v_b_ada', 'v_w_in', 'v_sinks', 'v_w_a', 'v_w_b', 'v_w_o', 'v_ln1_g', 'v_ln1_b', 'v_w_gate', 'v_w_up', 'v_w_down', 'v_ln2_g', 'v_ln2_b']
TWIN_OUTPUTS = ['loss', 'grad_x', 'grad_w_ada', 'grad_b_ada', 'grad_w_in', 'grad_sinks', 'grad_w_a', 'grad_w_b', 'grad_w_o', 'grad_ln1_g', 'grad_ln1_b', 'grad_w_gate', 'grad_w_up', 'grad_w_down', 'grad_ln2_g', 'grad_ln2_b', 'delta_w_ada', 'delta_b_ada', 'delta_w_in', 'delta_sinks', 'delta_w_a', 'delta_w_b', 'delta_w_o', 'delta_ln1_g', 'delta_ln1_b', 'delta_w_gate', 'delta_w_up', 'delta_w_down', 'delta_ln2_g', 'delta_ln2_b', 'new_m_w_ada', 'new_m_b_ada', 'new_m_w_in', 'new_m_sinks', 'new_m_w_a', 'new_m_w_b', 'new_m_w_o', 'new_m_ln1_g', 'new_m_ln1_b', 'new_m_w_gate', 'new_m_w_up', 'new_m_w_down', 'new_m_ln2_g', 'new_m_ln2_b', 'new_v_w_ada', 'new_v_b_ada', 'new_v_w_in', 'new_v_sinks', 'new_v_w_a', 'new_v_w_b', 'new_v_w_o', 'new_v_ln1_g', 'new_v_ln1_b', 'new_v_w_gate', 'new_v_w_up', 'new_v_w_down', 'new_v_ln2_g', 'new_v_ln2_b']
TWIN_LEAF_KINDS = {'loss': 'loss', 'grad_x': 'grad_x', 'grad_w_ada': 'grad_w', 'grad_b_ada': 'grad_w', 'grad_w_in': 'grad_w', 'grad_sinks': 'grad_w', 'grad_w_a': 'grad_w', 'grad_w_b': 'grad_w', 'grad_w_o': 'grad_w', 'grad_ln1_g': 'grad_w', 'grad_ln1_b': 'grad_w', 'grad_w_gate': 'grad_w', 'grad_w_up': 'grad_w', 'grad_w_down': 'grad_w', 'grad_ln2_g': 'grad_w', 'grad_ln2_b': 'grad_w', 'delta_w_ada': 'delta_w', 'delta_b_ada': 'delta_w', 'delta_w_in': 'delta_w', 'delta_sinks': 'delta_w', 'delta_w_a': 'delta_w', 'delta_w_b': 'delta_w', 'delta_w_o': 'delta_w', 'delta_ln1_g': 'delta_w', 'delta_ln1_b': 'delta_w', 'delta_w_gate': 'delta_w', 'delta_w_up': 'delta_w', 'delta_w_down': 'delta_w', 'delta_ln2_g': 'delta_w', 'delta_ln2_b': 'delta_w', 'new_m_w_ada': 'new_m', 'new_m_b_ada': 'new_m', 'new_m_w_in': 'new_m', 'new_m_sinks': 'new_m', 'new_m_w_a': 'new_m', 'new_m_w_b': 'new_m', 'new_m_w_o': 'new_m', 'new_m_ln1_g': 'new_m', 'new_m_ln1_b': 'new_m', 'new_m_w_gate': 'new_m', 'new_m_w_up': 'new_m', 'new_m_w_down': 'new_m', 'new_m_ln2_g': 'new_m', 'new_m_ln2_b': 'new_m', 'new_v_w_ada': 'new_v', 'new_v_b_ada': 'new_v', 'new_v_w_in': 'new_v', 'new_v_sinks': 'new_v', 'new_v_w_a': 'new_v', 'new_v_w_b': 'new_v', 'new_v_w_o': 'new_v', 'new_v_ln1_g': 'new_v', 'new_v_ln1_b': 'new_v', 'new_v_w_gate': 'new_v', 'new_v_w_up': 'new_v', 'new_v_w_down': 'new_v', 'new_v_ln2_g': 'new_v', 'new_v_ln2_b': 'new_v'}


def _forward(args):
    return _fwd_reference(*[args[k] for k in FWD_PARAMS])


def _output_shape():
    def fwd():
        inp = _fwd_setup_inputs(0)
        return _fwd_reference(*[inp[k] for k in FWD_PARAMS])
    out = _jax.eval_shape(fwd)
    return out.shape, out.dtype

N_MICROBATCH = 1
ADAM_LR = 0.001
ADAM_B1 = 0.9
ADAM_B2 = 0.999
ADAM_EPS = 1e-08
ADAM_WD = 0.01
ADAM_STEP = 10
PER_EXAMPLE_BATCH_AXIS = {'x': 0, 'c': 0, 'loss_target': 0}
SHARED_INPUTS = []
_WEIGHT_DTYPES = {'w_ada': _jnp.float32, 'b_ada': _jnp.float32, 'w_in': _jnp.float32, 'sinks': _jnp.float32, 'w_a': _jnp.float32, 'w_b': _jnp.float32, 'w_o': _jnp.float32, 'ln1_g': _jnp.float32, 'ln1_b': _jnp.float32, 'w_gate': _jnp.float32, 'w_up': _jnp.float32, 'w_down': _jnp.float32, 'ln2_g': _jnp.float32, 'ln2_b': _jnp.float32}
MOMENT_SCALE = {'w_ada': 2.085004e-02, 'b_ada': 3.555114e-02, 'w_in': 6.024140e-03, 'sinks': 1.571225e-02, 'w_a': 8.127218e-03, 'w_b': 5.073127e-03, 'w_o': 2.146616e-02, 'ln1_g': 4.496814e+00, 'ln1_b': 1.734307e+00, 'w_gate': 1.211996e-02, 'w_up': 1.177132e-02, 'w_down': 4.642372e-02, 'ln2_g': 6.451326e+01, 'ln2_b': 3.242216e+00}


def _to_microbatches(a, axis):
    t = _jnp.moveaxis(a, axis, 0)
    t = t.reshape((N_MICROBATCH, t.shape[0] // N_MICROBATCH) + t.shape[1:])
    return _jnp.moveaxis(t, 1, axis + 1)


def setup_inputs(seed: int = 0) -> dict:
    inp = _fwd_setup_inputs(seed)
    key = _jax.random.fold_in(_jax.random.key(seed), 7919)
    shape, _ = _output_shape()
    out = dict(inp)
    out["loss_target"] = _jax.random.normal(_jax.random.fold_in(key, 0), shape, _jnp.float32)
    for i, name in enumerate(TWIN_WEIGHTS):
        w = inp[name].astype(_jnp.float32)
        if MOMENT_SCALE is None:
            s = _jnp.sqrt(_jnp.mean(_jnp.square(w)) + 1e-30)
        else:
            s = MOMENT_SCALE[name]
        km, kv = _jax.random.split(_jax.random.fold_in(key, i + 1))
        out[name] = w
        out["m_" + name] = s * _jax.random.normal(km, w.shape, _jnp.float32)
        out["v_" + name] = (s * s) * _jax.random.uniform(kv, w.shape, _jnp.float32, 0.5, 1.5)
    if N_MICROBATCH > 1:
        for name, axis in PER_EXAMPLE_BATCH_AXIS.items():
            out[name] = _to_microbatches(out[name], axis)
    return {'x': out['x'], 'c': out['c'], 'w_ada': out['w_ada'], 'b_ada': out['b_ada'], 'w_in': out['w_in'], 'sinks': out['sinks'], 'w_a': out['w_a'], 'w_b': out['w_b'], 'w_o': out['w_o'], 'ln1_g': out['ln1_g'], 'ln1_b': out['ln1_b'], 'w_gate': out['w_gate'], 'w_up': out['w_up'], 'w_down': out['w_down'], 'ln2_g': out['ln2_g'], 'ln2_b': out['ln2_b'], 'loss_target': out['loss_target'], 'm_w_ada': out['m_w_ada'], 'm_b_ada': out['m_b_ada'], 'm_w_in': out['m_w_in'], 'm_sinks': out['m_sinks'], 'm_w_a': out['m_w_a'], 'm_w_b': out['m_w_b'], 'm_w_o': out['m_w_o'], 'm_ln1_g': out['m_ln1_g'], 'm_ln1_b': out['m_ln1_b'], 'm_w_gate': out['m_w_gate'], 'm_w_up': out['m_w_up'], 'm_w_down': out['m_w_down'], 'm_ln2_g': out['m_ln2_g'], 'm_ln2_b': out['m_ln2_b'], 'v_w_ada': out['v_w_ada'], 'v_b_ada': out['v_b_ada'], 'v_w_in': out['v_w_in'], 'v_sinks': out['v_sinks'], 'v_w_a': out['v_w_a'], 'v_w_b': out['v_w_b'], 'v_w_o': out['v_w_o'], 'v_ln1_g': out['v_ln1_g'], 'v_ln1_b': out['v_ln1_b'], 'v_w_gate': out['v_w_gate'], 'v_w_up': out['v_w_up'], 'v_w_down': out['v_w_down'], 'v_ln2_g': out['v_ln2_g'], 'v_ln2_b': out['v_ln2_b']}


def _loss(weights, diff, rest, loss_target):
    with _jax.named_scope("forward"):
        args = {**rest, TWIN_DIFF_INPUT: diff, **{k: w.astype(_WEIGHT_DTYPES[k]) for k, w in weights.items()}}
        y = _forward(args)
    with _jax.named_scope("loss_head"):
        err = _jnp.square(y.astype(_jnp.float32) - loss_target)
        return 0.5 * _jnp.sum(_jnp.mean(err, axis=-1)) if err.ndim else 0.5 * err


def _adamw(w, g, m, v):
    m = ADAM_B1 * m + (1.0 - ADAM_B1) * g
    v = ADAM_B2 * v + (1.0 - ADAM_B2) * _jnp.square(g)
    m_hat = m / (1.0 - ADAM_B1 ** ADAM_STEP)
    v_hat = v / (1.0 - ADAM_B2 ** ADAM_STEP)
    delta = -ADAM_LR * (m_hat / (_jnp.sqrt(v_hat) + ADAM_EPS) + ADAM_WD * w)
    return delta, m, v


def reference(x, c, w_ada, b_ada, w_in, sinks, w_a, w_b, w_o, ln1_g, ln1_b, w_gate, w_up, w_down, ln2_g, ln2_b, loss_target, m_w_ada, m_b_ada, m_w_in, m_sinks, m_w_a, m_w_b, m_w_o, m_ln1_g, m_ln1_b, m_w_gate, m_w_up, m_w_down, m_ln2_g, m_ln2_b, v_w_ada, v_b_ada, v_w_in, v_sinks, v_w_a, v_w_b, v_w_o, v_ln1_g, v_ln1_b, v_w_gate, v_w_up, v_w_down, v_ln2_g, v_ln2_b):
    given = dict(x=x, c=c, w_ada=w_ada, b_ada=b_ada, w_in=w_in, sinks=sinks, w_a=w_a, w_b=w_b, w_o=w_o, ln1_g=ln1_g, ln1_b=ln1_b, w_gate=w_gate, w_up=w_up, w_down=w_down, ln2_g=ln2_g, ln2_b=ln2_b, loss_target=loss_target, m_w_ada=m_w_ada, m_b_ada=m_b_ada, m_w_in=m_w_in, m_sinks=m_sinks, m_w_a=m_w_a, m_w_b=m_w_b, m_w_o=m_w_o, m_ln1_g=m_ln1_g, m_ln1_b=m_ln1_b, m_w_gate=m_w_gate, m_w_up=m_w_up, m_w_down=m_w_down, m_ln2_g=m_ln2_g, m_ln2_b=m_ln2_b, v_w_ada=v_w_ada, v_b_ada=v_b_ada, v_w_in=v_w_in, v_sinks=v_sinks, v_w_a=v_w_a, v_w_b=v_w_b, v_w_o=v_w_o, v_ln1_g=v_ln1_g, v_ln1_b=v_ln1_b, v_w_gate=v_w_gate, v_w_up=v_w_up, v_w_down=v_w_down, v_ln2_g=v_ln2_g, v_ln2_b=v_ln2_b)
    weights = {n: given[n] for n in TWIN_WEIGHTS}
    shared = {n: given[n] for n in SHARED_INPUTS}
    per_example = {n: given[n] for n in ['x', 'c']}
    grad_fn = _jax.value_and_grad(_loss, argnums=(0, 1))

    def one_microbatch(ex, loss_target):
        ex = dict(ex)
        diff = ex.pop(TWIN_DIFF_INPUT)
        return grad_fn(weights, diff, {**shared, **ex}, loss_target)

    if N_MICROBATCH == 1:
        loss, (grad_w, grad_x) = one_microbatch(per_example, given["loss_target"])
    else:
        def body(carry, xs):
            loss_sum, grad_sum = carry
            l_k, (gw_k, gx_k) = one_microbatch(xs[0], xs[1])
            with _jax.named_scope("update"):
                return (loss_sum + l_k, _jax.tree.map(_jnp.add, grad_sum, gw_k)), gx_k

        init = (_jnp.zeros((), _jnp.float32), _jax.tree.map(_jnp.zeros_like, weights))
        (loss, grad_w), grad_x = _jax.lax.scan(body, init, (per_example, given["loss_target"]))
    with _jax.named_scope("update"):
        delta_w, new_m, new_v = {}, {}, {}
        for n in TWIN_WEIGHTS:
            delta_w[n], new_m[n], new_v[n] = _adamw(weights[n], grad_w[n], given["m_" + n], given["v_" + n])
    return (loss, grad_x, *[grad_w[n] for n in TWIN_WEIGHTS], *[delta_w[n] for n in TWIN_WEIGHTS],
            *[new_m[n] for n in TWIN_WEIGHTS], *[new_v[n] for n in TWIN_WEIGHTS])
```

```python
import functools

import jax
import jax.numpy as jnp
from jax import lax
from jax.experimental import pallas as pl
from jax.experimental.pallas import tpu as pltpu

F32 = jnp.float32
BF16 = jnp.bfloat16

D_MODEL = 1024
HEAD_DIM = 64
A_Q_HEADS = 8
A_KV_HEADS = 2
A_WINDOW = 128
B_GROUPS = ((128, 1), (512, 4), (2048, 16))
B_HEADS_PER_GROUP = 4
N_ATTN_HEADS = A_Q_HEADS + B_HEADS_PER_GROUP * len(B_GROUPS)
BLOCK = 128
A_W = (A_Q_HEADS + 2 * A_KV_HEADS) * HEAD_DIM
B_GW = 3 * B_HEADS_PER_GROUP * HEAD_DIM
B_OUT_W = B_HEADS_PER_GROUP * HEAD_DIM
IN_W = 5120
D_FF = 2816
DN_ALPHA = 8.0 ** 0.25
LN_EPS = 1e-5
NEG_INF = -1e30
ADAM_LR, ADAM_B1, ADAM_B2, ADAM_EPS, ADAM_WD, ADAM_STEP = 0.001, 0.9, 0.999, 1e-08, 0.01, 10

N_DEV = 8
MESH = pl.DeviceIdType.MESH
VMEM_LIMIT = 56 * 1024 * 1024
ROW_TILE = 512


def _cp(*sem):
    return pltpu.CompilerParams(dimension_semantics=sem, vmem_limit_bytes=VMEM_LIMIT)


def _row_tile(t):
    return min(ROW_TILE, t)


def _slope(head):
    return 2.0 ** (-8.0 * (head + 1) / N_ATTN_HEADS)


def _sigmoid(x):
    return 1.0 / (1.0 + jnp.exp(-x))


def _dot(a, b):
    return jnp.dot(a, b, preferred_element_type=F32)


def _dot_nt(a, b):
    return lax.dot_general(a, b, (((1,), (1,)), ((), ())), preferred_element_type=F32)


def _dot_tn(a, b):
    return lax.dot_general(a, b, (((0,), (0,)), ((), ())), preferred_element_type=F32)


def _me():
    return lax.axis_index("x"), lax.axis_index("y"), lax.axis_index("c")


def _flip(v, bit):
    return 1 - v if bit else v


def _peer(k):
    x, y, c = _me()
    return (_flip(x, k & 4), _flip(y, k & 2), _flip(c, k & 1))


def _peer_index(k):
    px, py, pc = _peer(k)
    return 4 * px + 2 * py + pc


def all_gather_small(v, name):
    r, c = v.shape

    def body(v_ref, out_ref, send_sems, recv_sems):
        me = _peer_index(0)
        out_ref[me] = v_ref[...]
        copies = []
        for k in range(1, N_DEV):
            cp = pltpu.make_async_remote_copy(
                src_ref=v_ref, dst_ref=out_ref.at[me],
                send_sem=send_sems.at[k - 1], recv_sem=recv_sems.at[k - 1],
                device_id=_peer(k), device_id_type=MESH)
            cp.start()
            copies.append(cp)
        for k in range(1, N_DEV):
            pltpu.make_async_remote_copy(
                src_ref=v_ref, dst_ref=out_ref.at[_peer_index(k)],
                send_sem=send_sems.at[k - 1], recv_sem=recv_sems.at[k - 1],
                device_id=_peer(k), device_id_type=MESH).wait_recv()
        for cp in copies:
            cp.wait_send()

    return pl.pallas_call(
        body, name=name,
        out_shape=jax.ShapeDtypeStruct((N_DEV, r, c), v.dtype),
        in_specs=[pl.BlockSpec(memory_space=pltpu.VMEM)],
        out_specs=pl.BlockSpec(memory_space=pltpu.VMEM),
        scratch_shapes=[pltpu.SemaphoreType.DMA((N_DEV - 1,)), pltpu.SemaphoreType.DMA((N_DEV - 1,))],
        compiler_params=pltpu.CompilerParams(vmem_limit_bytes=VMEM_LIMIT),
    )(v)


def all_gather_big(v, name):
    r, c = v.shape

    def body(v_ref, out_ref, send_sems, recv_sems, local_sem):
        me = _peer_index(0)
        mine = pltpu.make_async_copy(v_ref, out_ref.at[me], local_sem)
        mine.start()
        copies = []
        for k in range(1, N_DEV):
            cp = pltpu.make_async_remote_copy(
                src_ref=v_ref, dst_ref=out_ref.at[me],
                send_sem=send_sems.at[k - 1], recv_sem=recv_sems.at[k - 1],
                device_id=_peer(k), device_id_type=MESH)
            cp.start()
            copies.append(cp)
        for k in range(1, N_DEV):
            pltpu.make_async_remote_copy(
                src_ref=v_ref, dst_ref=out_ref.at[_peer_index(k)],
                send_sem=send_sems.at[k - 1], recv_sem=recv_sems.at[k - 1],
                device_id=_peer(k), device_id_type=MESH).wait_recv()
        for cp in copies:
            cp.wait_send()
        mine.wait()

    return pl.pallas_call(
        body, name=name,
        out_shape=jax.ShapeDtypeStruct((N_DEV, r, c), v.dtype),
        in_specs=[pl.BlockSpec(memory_space=pl.ANY)],
        out_specs=pl.BlockSpec(memory_space=pl.ANY),
        scratch_shapes=[pltpu.SemaphoreType.DMA((N_DEV - 1,)), pltpu.SemaphoreType.DMA((N_DEV - 1,)),
                        pltpu.SemaphoreType.DMA],
    )(v)


def exchange_big(g, name):
    _, r, c = g.shape

    def body(g_ref, out_ref, send_sems, recv_sems, local_sem):
        me = _peer_index(0)
        mine = pltpu.make_async_copy(g_ref.at[me], out_ref.at[me], local_sem)
        mine.start()
        copies = []
        for k in range(1, N_DEV):
            cp = pltpu.make_async_remote_copy(
                src_ref=g_ref.at[_peer_index(k)], dst_ref=out_ref.at[me],
                send_sem=send_sems.at[k - 1], recv_sem=recv_sems.at[k - 1],
                device_id=_peer(k), device_id_type=MESH)
            cp.start()
            copies.append(cp)
        for k in range(1, N_DEV):
            pltpu.make_async_remote_copy(
                src_ref=g_ref.at[me], dst_ref=out_ref.at[_peer_index(k)],
                send_sem=send_sems.at[k - 1], recv_sem=recv_sems.at[k - 1],
                device_id=_peer(k), device_id_type=MESH).wait_recv()
        for cp in copies:
            cp.wait_send()
        mine.wait()

    return pl.pallas_call(
        body, name=name,
        out_shape=jax.ShapeDtypeStruct(g.shape, g.dtype),
        in_specs=[pl.BlockSpec(memory_space=pl.ANY)],
        out_specs=pl.BlockSpec(memory_space=pl.ANY),
        scratch_shapes=[pltpu.SemaphoreType.DMA((N_DEV - 1,)), pltpu.SemaphoreType.DMA((N_DEV - 1,)),
                        pltpu.SemaphoreType.DMA],
    )(g)


def mod_partial(c_all, w_ada, b_loc):
    nl, dm, wc = w_ada.shape

    def body(c_ref, w_ref, b_ref, o_ref, sc_ref):
        cc = c_ref[...]
        sc = cc * _sigmoid(cc)
        sc_ref[...] = sc
        o_ref[...] = jnp.dot(sc, w_ref[...], preferred_element_type=F32,
                             precision=lax.Precision.HIGHEST) + b_ref[...]

    return pl.pallas_call(
        body, name="mod_partial", grid=(nl,),
        out_shape=[jax.ShapeDtypeStruct((nl, N_DEV, wc), F32), jax.ShapeDtypeStruct((N_DEV, dm), F32)],
        in_specs=[pl.BlockSpec((N_DEV, dm), lambda l: (0, 0)),
                  pl.BlockSpec((None, dm, wc), lambda l: (l, 0, 0)),
                  pl.BlockSpec((None, 1, wc), lambda l: (l, 0, 0))],
        out_specs=[pl.BlockSpec((None, N_DEV, wc), lambda l: (l, 0, 0)),
                   pl.BlockSpec((N_DEV, dm), lambda l: (0, 0))],
        compiler_params=_cp("arbitrary"),
    )(c_all, w_ada, b_loc)


def modmm(x, s, sh, w, name):
    t, dm = x.shape
    n = w.shape[1]
    tm = _row_tile(t)
    ch = 512

    def body(x_ref, s_ref, sh_ref, w_ref, u_ref, o_ref):
        u = (x_ref[...] * (1.0 + s_ref[...]) + sh_ref[...]).astype(BF16)
        u_ref[...] = u
        for c0 in range(0, n, ch):
            o_ref[:, c0:c0 + ch] = _dot(u, w_ref[:, c0:c0 + ch]).astype(BF16)

    vec = pl.BlockSpec((1, dm), lambda i: (0, 0))
    return pl.pallas_call(
        body, name=name, grid=(t // tm,),
        out_shape=[jax.ShapeDtypeStruct((t, dm), BF16), jax.ShapeDtypeStruct((t, n), BF16)],
        in_specs=[pl.BlockSpec((tm, dm), lambda i: (i, 0)), vec, vec,
                  pl.BlockSpec((dm, n), lambda i: (0, 0))],
        out_specs=[pl.BlockSpec((tm, dm), lambda i: (i, 0)), pl.BlockSpec((tm, n), lambda i: (i, 0))],
        compiler_params=_cp("parallel"),
    )(x, s, sh, w)


def _ln_store(y, xres_ref, g_ref, lg_ref, lb_ref, y_ref, xo_ref, zh_ref, rs_ref):
    y_ref[...] = y.astype(BF16)
    z = DN_ALPHA * xres_ref[...] + g_ref[...] * y
    mu = jnp.mean(z, axis=1, keepdims=True)
    zc = z - mu
    var = jnp.mean(zc * zc, axis=1, keepdims=True)
    rstd = lax.rsqrt(var + LN_EPS)
    zhat = zc * rstd
    zh_ref[...] = zhat
    xo_ref[...] = zhat * lg_ref[...] + lb_ref[...]
    rs_ref[...] = jnp.broadcast_to(rstd, rs_ref.shape)


def _ln_out_shapes(t, dm):
    return [jax.ShapeDtypeStruct((t, dm), BF16), jax.ShapeDtypeStruct((t, dm), F32),
            jax.ShapeDtypeStruct((t, dm), F32), jax.ShapeDtypeStruct((t, 128), F32)]


def _ln_out_specs(tm, dm):
    row = pl.BlockSpec((tm, dm), lambda i: (i, 0))
    return [row, row, row, pl.BlockSpec((tm, 128), lambda i: (i, 0))]


def proj_ln(a, w, xres, gate, lg, lb, name):
    t, k = a.shape
    dm = w.shape[1]
    tm = _row_tile(t)

    def body(a_ref, w_ref, xres_ref, g_ref, lg_ref, lb_ref, y_ref, xo_ref, zh_ref, rs_ref):
        y = _dot(a_ref[...], w_ref[...])
        _ln_store(y, xres_ref, g_ref, lg_ref, lb_ref, y_ref, xo_ref, zh_ref, rs_ref)

    vec = pl.BlockSpec((1, dm), lambda i: (0, 0))
    return pl.pallas_call(
        body, name=name, grid=(t // tm,),
        out_shape=_ln_out_shapes(t, dm),
        in_specs=[pl.BlockSpec((tm, k), lambda i: (i, 0)), pl.BlockSpec((k, dm), lambda i: (0, 0)),
                  pl.BlockSpec((tm, dm), lambda i: (i, 0)), vec, vec, vec],
        out_specs=_ln_out_specs(tm, dm),
        compiler_params=_cp("parallel"),
    )(a, w, xres, gate, lg, lb)


def swiglu_proj_ln(ab, w, xres, gate, lg, lb, name):
    t = ab.shape[0]
    f, dm = w.shape
    tm = _row_tile(t)

    def body(a_ref, b_ref, w_ref, xres_ref, g_ref, lg_ref, lb_ref, h_ref, y_ref, xo_ref, zh_ref, rs_ref):
        a = a_ref[...].astype(F32)
        h = (a * _sigmoid(a) * b_ref[...].astype(F32)).astype(BF16)
        h_ref[...] = h
        y = _dot(h, w_ref[...])
        _ln_store(y, xres_ref, g_ref, lg_ref, lb_ref, y_ref, xo_ref, zh_ref, rs_ref)

    vec = pl.BlockSpec((1, dm), lambda i: (0, 0))
    return pl.pallas_call(
        body, name=name, grid=(t // tm,),
        out_shape=[jax.ShapeDtypeStruct((t, f), BF16)] + _ln_out_shapes(t, dm),
        in_specs=[pl.BlockSpec((tm, f), lambda i: (i, 0)), pl.BlockSpec((tm, f), lambda i: (i, 1)),
                  pl.BlockSpec((f, dm), lambda i: (0, 0)),
                  pl.BlockSpec((tm, dm), lambda i: (i, 0)), vec, vec, vec],
        out_specs=[pl.BlockSpec((tm, f), lambda i: (i, 0))] + _ln_out_specs(tm, dm),
        compiler_params=_cp("parallel"),
    )(ab, ab, w, xres, gate, lg, lb)


class AttnCfg:
    def __init__(self, dil, heads, kv_heads, col0, max_dist, head0, sinks):
        self.dil, self.heads, self.kv_heads = dil, heads, kv_heads
        self.col0 = col0
        self.max_dist, self.head0, self.sinks = max_dist, head0, sinks
        self.wq = heads * HEAD_DIM
        self.wk = kv_heads * HEAD_DIM
        self.wout = self.wq + 2 * self.wk


ATTN_A = AttnCfg(1, A_Q_HEADS, A_KV_HEADS, 0, A_WINDOW - 1, 0, True)
ATTN_B = [AttnCfg(dil, B_HEADS_PER_GROUP, B_HEADS_PER_GROUP, A_W + g * B_GW, win // dil,
                  A_Q_HEADS + g * B_HEADS_PER_GROUP, False)
          for g, (win, dil) in enumerate(B_GROUPS)]


def _band(i, max_dist):
    qi = lax.broadcasted_iota(jnp.int32, (BLOCK, 2 * BLOCK), 0)
    sj = lax.broadcasted_iota(jnp.int32, (BLOCK, 2 * BLOCK), 1)
    dist = qi + BLOCK - sj
    valid = (dist >= 0) & (dist <= max_dist)
    first_key = jnp.where(i > 0, 0, BLOCK)
    valid_first = valid & (sj >= first_key)
    return dist, valid, valid_first


def _attn_geometry(cfg, t):
    n = t // cfg.dil
    tq = min(512, n)
    return n, tq, tq // BLOCK, n // tq


def attn_fwd(qkvg, cfg, sinks, name):
    t, wtot = qkvg.shape
    d = cfg.dil
    n, tq, nsub, nqb = _attn_geometry(cfg, t)
    wq, wk = cfg.wq, cfg.wk
    grp = cfg.heads // cfg.kv_heads
    view = qkvg.reshape(n, d * wtot)
    qc, kc, vc = cfg.col0, cfg.col0 + wq, cfg.col0 + wq + wk

    def body(sink_ref, q_ref, kc_ref, kp_ref, vc_ref, vp_ref, o_ref, l_ref, kf, vf):
        i = pl.program_id(1)
        kf[0:BLOCK, :] = kp_ref[...]
        kf[BLOCK:, :] = kc_ref[...]
        vf[0:BLOCK, :] = vp_ref[...]
        vf[BLOCK:, :] = vc_ref[...]
        dist, valid, valid_first = _band(i, cfg.max_dist)
        distf = dist.astype(F32)
        for h in range(cfg.heads):
            kv = h // grp
            bias = distf * (-(_slope(cfg.head0 + h) * d))
            hs = slice(h * HEAD_DIM, (h + 1) * HEAD_DIM)
            ks = slice(kv * HEAD_DIM, (kv + 1) * HEAD_DIM)
            for a in range(nsub):
                rows = slice(a * BLOCK, (a + 1) * BLOCK)
                win = slice(a * BLOCK, (a + 2) * BLOCK)
                s = _dot_nt(q_ref[rows, hs], kf[win, ks]) * (HEAD_DIM ** -0.5)
                s = jnp.where(valid_first if a == 0 else valid, s + bias, NEG_INF)
                m = jnp.max(s, axis=1, keepdims=True)
                if cfg.sinks:
                    m = jnp.maximum(m, sink_ref[h])
                e = jnp.exp(s - m)
                den = jnp.sum(e, axis=1, keepdims=True)
                if cfg.sinks:
                    den = den + jnp.exp(sink_ref[h] - m)
                p = (e / den).astype(BF16)
                o_ref[rows, hs] = _dot(p, vf[win, ks]).astype(BF16)
                l_ref[rows, hs] = jnp.broadcast_to(m + jnp.log(den), (BLOCK, HEAD_DIM))

    prev = lambda r, i: jnp.maximum(i * nsub - 1, 0)
    return pl.pallas_call(
        body, name=name, grid=(d, nqb),
        out_shape=[jax.ShapeDtypeStruct((n, d * wq), BF16), jax.ShapeDtypeStruct((n, d * wq), F32)],
        in_specs=[pl.BlockSpec(memory_space=pltpu.SMEM),
                  pl.BlockSpec((tq, wq), lambda r, i: (i, (r * wtot + qc) // wq)),
                  pl.BlockSpec((tq, wk), lambda r, i: (i, (r * wtot + kc) // wk)),
                  pl.BlockSpec((BLOCK, wk), lambda r, i: (prev(r, i), (r * wtot + kc) // wk)),
                  pl.BlockSpec((tq, wk), lambda r, i: (i, (r * wtot + vc) // wk)),
                  pl.BlockSpec((BLOCK, wk), lambda r, i: (prev(r, i), (r * wtot + vc) // wk))],
        out_specs=[pl.BlockSpec((tq, wq), lambda r, i: (i, r)), pl.BlockSpec((tq, wq), lambda r, i: (i, r))],
        scratch_shapes=[pltpu.VMEM((tq + BLOCK, wk), BF16), pltpu.VMEM((tq + BLOCK, wk), BF16)],
        compiler_params=_cp("parallel", "parallel"),
    )(sinks, view, view, view, view, view)


def mix_merge(ya, o_g, l_g, qkvg, w_a, w_b, name):
    t = ya.shape[0]
    dm = w_a.shape[1]
    tm = _row_tile(t)
    gcol = (A_W + len(B_GROUPS) * B_GW) // dm

    def body(ya_ref, o0, o1, o2, l0, l1, l2, ga_ref, gb_ref, wa_ref, wb_ref, yb_ref, mg_ref):
        ls = [l0[...], l1[...], l2[...]]
        m = jnp.maximum(jnp.maximum(ls[0], ls[1]), ls[2])
        es = [jnp.exp(l - m) for l in ls]
        inv = 1.0 / (es[0] + es[1] + es[2])
        yb = sum(o[...].astype(F32) * (e * inv) for o, e in zip((o0, o1, o2), es)).astype(BF16)
        yb_ref[...] = yb
        pa = _dot(ya_ref[...], wa_ref[...])
        pb = _dot(yb, wb_ref[...])
        mg = _sigmoid(ga_ref[...].astype(F32)) * pa + _sigmoid(gb_ref[...].astype(F32)) * pb
        mg_ref[...] = mg.astype(BF16)

    wide = lambda w: pl.BlockSpec((tm, w), lambda i: (i, 0))
    return pl.pallas_call(
        body, name=name, grid=(t // tm,),
        out_shape=[jax.ShapeDtypeStruct((t, B_OUT_W), BF16), jax.ShapeDtypeStruct((t, dm), BF16)],
        in_specs=[wide(ya.shape[1])] + [wide(B_OUT_W)] * 6
                 + [pl.BlockSpec((tm, dm), lambda i: (i, gcol)), pl.BlockSpec((tm, dm), lambda i: (i, gcol + 1)),
                    pl.BlockSpec(w_a.shape, lambda i: (0, 0)), pl.BlockSpec(w_b.shape, lambda i: (0, 0))],
        out_specs=[wide(B_OUT_W), wide(dm)],
        compiler_params=_cp("parallel"),
    )(ya, *o_g, *l_g, qkvg, qkvg, w_a, w_b)


def loss_head(y, target):
    t, dm = y.shape
    tm = _row_tile(t)

    def body(y_ref, t_ref, dy_ref, loss_ref):
        @pl.when(pl.program_id(0) == 0)
        def _():
            loss_ref[...] = jnp.zeros_like(loss_ref)
        err = y_ref[...] - t_ref[...]
        dy_ref[...] = err * (1.0 / dm)
        per_row = jnp.sum(err * err, axis=1, keepdims=True) * (1.0 / dm)
        loss_ref[...] += 0.5 * jnp.sum(per_row, axis=0, keepdims=True)

    row = pl.BlockSpec((tm, dm), lambda i: (i, 0))
    return pl.pallas_call(
        body, name="loss_head", grid=(t // tm,),
        out_shape=[jax.ShapeDtypeStruct((t, dm), F32), jax.ShapeDtypeStruct((8, 128), F32)],
        in_specs=[row, row],
        out_specs=[row, pl.BlockSpec((8, 128), lambda i: (0, 0))],
        compiler_params=_cp("arbitrary"),
    )(y, target)


def _fold_rows(v):
    tm, c = v.shape
    return jnp.sum(v.reshape(tm // 8, 8, c), axis=0)


def _finish_sums(refs, nsteps):
    @pl.when(pl.program_id(0) == nsteps - 1)
    def _():
        for r in refs:
            r[...] = jnp.broadcast_to(jnp.sum(r[...], axis=0, keepdims=True), r.shape)


def ln_bwd(dxo, zhat, rstd, ysub, lg, gate, name):
    t, dm = dxo.shape
    tm = _row_tile(t)

    def body(dxo_ref, zh_ref, rs_ref, y_ref, lg_ref, g_ref, dz_ref, dy_ref, sg_ref, sb_ref, sgate_ref):
        @pl.when(pl.program_id(0) == 0)
        def _():
            for r in (sg_ref, sb_ref, sgate_ref):
                r[...] = jnp.zeros_like(r)
        dxo_v = dxo_ref[...]
        zh = zh_ref[...]
        dxh = dxo_v * lg_ref[...]
        m1 = jnp.mean(dxh, axis=1, keepdims=True)
        m2 = jnp.mean(dxh * zh, axis=1, keepdims=True)
        dz = rs_ref[:, 0:1] * (dxh - m1 - zh * m2)
        dz_ref[...] = dz
        dy_ref[...] = (g_ref[...] * dz).astype(BF16)
        sg_ref[...] += _fold_rows(dxo_v * zh)
        sb_ref[...] += _fold_rows(dxo_v)
        sgate_ref[...] += _fold_rows(dz * y_ref[...].astype(F32))
        _finish_sums((sg_ref, sb_ref, sgate_ref), t // tm)

    row = pl.BlockSpec((tm, dm), lambda i: (i, 0))
    vec = pl.BlockSpec((1, dm), lambda i: (0, 0))
    acc = pl.BlockSpec((8, dm), lambda i: (0, 0))
    return pl.pallas_call(
        body, name=name, grid=(t // tm,),
        out_shape=[jax.ShapeDtypeStruct((t, dm), F32), jax.ShapeDtypeStruct((t, dm), BF16)]
                  + [jax.ShapeDtypeStruct((8, dm), F32)] * 3,
        in_specs=[row, row, pl.BlockSpec((tm, 128), lambda i: (i, 0)), row, vec, vec],
        out_specs=[row, row, acc, acc, acc],
        compiler_params=_cp("arbitrary"),
    )(dxo, zhat, rstd, ysub, lg, gate)


def dgrad_mod(parts, dz, xin, s, name):
    t, dm = dz.shape
    tm = _row_tile(t)
    npart = len(parts)

    def body(*refs):
        g_refs, w_refs = refs[:npart], refs[npart:2 * npart]
        dz_ref, x_ref, s_ref, dx_ref, ss_ref, ssh_ref = refs[2 * npart:]

        @pl.when(pl.program_id(0) == 0)
        def _():
            ss_ref[...] = jnp.zeros_like(ss_ref)
            ssh_ref[...] = jnp.zeros_like(ssh_ref)
        du = _dot(g_refs[0][...], w_refs[0][...])
        for g_ref, w_ref in zip(g_refs[1:], w_refs[1:]):
            du = du + _dot(g_ref[...], w_ref[...])
        dx_ref[...] = DN_ALPHA * dz_ref[...] + du * (1.0 + s_ref[...])
        ss_ref[...] += _fold_rows(du * x_ref[...])
        ssh_ref[...] += _fold_rows(du)
        _finish_sums((ss_ref, ssh_ref), t // tm)

    row = pl.BlockSpec((tm, dm), lambda i: (i, 0))
    acc = pl.BlockSpec((8, dm), lambda i: (0, 0))
    g_specs = [pl.BlockSpec((tm, gw), functools.partial(lambda i, c: (i, c), c=gc)) for _, gw, gc, _, _ in parts]
    w_specs = [pl.BlockSpec((gw, dm), functools.partial(lambda i, r: (r, 0), r=wr)) for _, gw, _, _, wr in parts]
    return pl.pallas_call(
        body, name=name, grid=(t // tm,),
        out_shape=[jax.ShapeDtypeStruct((t, dm), F32)] + [jax.ShapeDtypeStruct((8, dm), F32)] * 2,
        in_specs=g_specs + w_specs + [row, row, pl.BlockSpec((1, dm), lambda i: (0, 0))],
        out_specs=[row, acc, acc],
        compiler_params=_cp("arbitrary"),
    )(*[p[0] for p in parts], *[p[3] for p in parts], dz, xin, s)


def wgrad(a, b, bw, bcol, name):
    t, k = a.shape
    tt = _row_tile(t)
    tn = 512 if bw % 512 == 0 else (256 if bw % 256 == 0 else 128)
    nj = bw // tn

    def body(a_ref, b_ref, o_ref):
        @pl.when(pl.program_id(1) == 0)
        def _():
            o_ref[...] = jnp.zeros_like(o_ref)
        o_ref[...] += _dot_tn(a_ref[...], b_ref[...])

    return pl.pallas_call(
        body, name=name, grid=(nj, t // tt),
        out_shape=jax.ShapeDtypeStruct((k, bw), F32),
        in_specs=[pl.BlockSpec((tt, k), lambda j, s: (s, 0)),
                  pl.BlockSpec((tt, tn), lambda j, s: (s, bcol * nj + j))],
        out_specs=pl.BlockSpec((k, tn), lambda j, s: (0, j)),
        compiler_params=_cp("parallel", "arbitrary"),
    )(a, b)


def dswiglu(dy, wdt, ab, name):
    t, dm = dy.shape
    f = wdt.shape[1]
    tm = _row_tile(t)

    def body(dy_ref, w_ref, a_ref, b_ref, o_ref):
        dh = _dot(dy_ref[...], w_ref[...])
        a = a_ref[...].astype(F32)
        sg = _sigmoid(a)
        o_ref[:, 0:f] = (dh * b_ref[...].astype(F32) * (sg * (1.0 + a * (1.0 - sg)))).astype(BF16)
        o_ref[:, f:] = (dh * (a * sg)).astype(BF16)

    return pl.pallas_call(
        body, name=name, grid=(t // tm,),
        out_shape=jax.ShapeDtypeStruct((t, 2 * f), BF16),
        in_specs=[pl.BlockSpec((tm, dm), lambda i: (i, 0)), pl.BlockSpec((dm, f), lambda i: (0, 0)),
                  pl.BlockSpec((tm, f), lambda i: (i, 0)), pl.BlockSpec((tm, f), lambda i: (i, 1))],
        out_specs=pl.BlockSpec((tm, 2 * f), lambda i: (i, 0)),
        compiler_params=_cp("parallel"),
    )(dy, wdt, ab, ab)


def dmerge(do, wot, ya, yb, w_a, w_b, wat, wbt, qkvg, name):
    t, dm = do.shape
    tm = _row_tile(t)
    gcol = (A_W + len(B_GROUPS) * B_GW) // dm

    def body(do_ref, wot_ref, ya_ref, yb_ref, wa_ref, wb_ref, wat_ref, wbt_ref, g_ref,
             dp_ref, dya_ref, dyb_ref, dg_ref, dm_scr):
        j = pl.program_id(1)

        @pl.when(j == 0)
        def _():
            dm_scr[...] = _dot(do_ref[...], wot_ref[...])

        def branch(y_ref, w_ref, wt_ref, dy_ref):
            p = _dot(y_ref[...], w_ref[...])
            sg = _sigmoid(g_ref[...].astype(F32))
            dmg = dm_scr[...]
            dp = (dmg * sg).astype(BF16)
            dp_ref[...] = dp
            dg_ref[...] = (dmg * p * (sg * (1.0 - sg))).astype(BF16)
            dy_ref[...] = _dot(dp, wt_ref[...]).astype(dy_ref.dtype)

        pl.when(j == 0)(lambda: branch(ya_ref, wa_ref, wat_ref, dya_ref))
        pl.when(j == 1)(lambda: branch(yb_ref, wb_ref, wbt_ref, dyb_ref))

    full = lambda arr: pl.BlockSpec(arr.shape, lambda i, j: (0, 0))
    rowc = lambda w: pl.BlockSpec((tm, w), lambda i, j: (i, 0))
    return pl.pallas_call(
        body, name=name, grid=(t // tm, 2),
        out_shape=[jax.ShapeDtypeStruct((t, 2 * dm), BF16), jax.ShapeDtypeStruct((t, ya.shape[1]), BF16),
                   jax.ShapeDtypeStruct((t, yb.shape[1]), F32), jax.ShapeDtypeStruct((t, 2 * dm), BF16)],
        in_specs=[rowc(dm), full(wot), rowc(ya.shape[1]), rowc(yb.shape[1]), full(w_a), full(w_b),
                  full(wat), full(wbt), pl.BlockSpec((tm, dm), lambda i, j: (i, gcol + j))],
        out_specs=[pl.BlockSpec((tm, dm), lambda i, j: (i, j)), rowc(ya.shape[1]), rowc(yb.shape[1]),
                   pl.BlockSpec((tm, dm), lambda i, j: (i, j))],
        scratch_shapes=[pltpu.VMEM((tm, dm), F32)],
        compiler_params=_cp("parallel", "arbitrary"),
    )(do, wot, ya, yb, w_a, w_b, wat, wbt, qkvg)


def mix_bwd(dyb, o_g, l_g, name):
    t, w = dyb.shape
    tm = _row_tile(t)
    nh = w // HEAD_DIM

    def body(dyb_ref, o0, o1, o2, l0, l1, l2, do0, do1, do2, dl0, dl1, dl2):
        ls = [l0[...], l1[...], l2[...]]
        m = jnp.maximum(jnp.maximum(ls[0], ls[1]), ls[2])
        es = [jnp.exp(l - m) for l in ls]
        inv = 1.0 / (es[0] + es[1] + es[2])
        wts = [e * inv for e in es]
        dyb_v = dyb_ref[...]
        dws = []
        for o_ref, do_ref, wt in zip((o0, o1, o2), (do0, do1, do2), wts):
            do_ref[...] = (dyb_v * wt).astype(BF16)
            prod = dyb_v * o_ref[...].astype(F32)
            for h in range(nh):
                hs = slice(h * HEAD_DIM, (h + 1) * HEAD_DIM)
                dws.append(jnp.broadcast_to(jnp.sum(prod[:, hs], axis=1, keepdims=True), (tm, HEAD_DIM)))
        for h in range(nh):
            hs = slice(h * HEAD_DIM, (h + 1) * HEAD_DIM)
            mean = sum(wts[g][:, hs] * dws[g * nh + h] for g in range(3))
            for g, dl_ref in enumerate((dl0, dl1, dl2)):
                dl_ref[:, hs] = wts[g][:, hs] * (dws[g * nh + h] - mean)

    row = pl.BlockSpec((tm, w), lambda i: (i, 0))
    return pl.pallas_call(
        body, name=name, grid=(t // tm,),
        out_shape=[jax.ShapeDtypeStruct((t, w), BF16)] * 3 + [jax.ShapeDtypeStruct((t, w), F32)] * 3,
        in_specs=[row] * 7, out_specs=[row] * 6,
        compiler_params=_cp("parallel"),
    )(dyb, *o_g, *l_g)


def attn_bwd(qkvg, o, lse, do, dlse, cfg, sinks, name):
    t, wtot = qkvg.shape
    d = cfg.dil
    n, tq, nsub, nqb = _attn_geometry(cfg, t)
    wq, wk, wout = cfg.wq, cfg.wk, cfg.wout
    grp = cfg.heads // cfg.kv_heads
    view = qkvg.reshape(n, d * wtot)
    qc, kc, vc = cfg.col0, cfg.col0 + wq, cfg.col0 + wq + wk
    has_dl = dlse is not None
    scale = HEAD_DIM ** -0.5

    def body(*refs):
        sink_ref, q_ref, qn_ref, kc_ref, kp_ref, vc_ref, vp_ref = refs[:7]
        o_ref, on_ref, do_ref, don_ref, l_ref, ln_ref = refs[7:13]
        rest = refs[13:]
        if has_dl:
            dl_ref, dln_ref = rest[:2]
            rest = rest[2:]
        out_ref = rest[0]
        rest = rest[1:]
        if cfg.sinks:
            dsink_ref = rest[0]
            rest = rest[1:]
        kf, vf, dk_acc, dv_acc = rest
        r, i = pl.program_id(0), pl.program_id(1)
        kf[0:BLOCK, :] = kp_ref[...]
        kf[BLOCK:, :] = kc_ref[...]
        vf[0:BLOCK, :] = vp_ref[...]
        vf[BLOCK:, :] = vc_ref[...]
        dist, valid, valid_first = _band(i, cfg.max_dist)
        distf = dist.astype(F32)
        next_dist = jnp.where(i < nqb - 1, cfg.max_dist, -1)
        valid_next = (dist[:, 0:BLOCK] >= 0) & (dist[:, 0:BLOCK] <= next_dist)
        if cfg.sinks:
            @pl.when((r == 0) & (i == 0))
            def _():
                dsink_ref[...] = jnp.zeros_like(dsink_ref)

        def stats(o_r, do_r, l_r, dl_r, rows, hs):
            do_v = do_r[rows, hs]
            delta = jnp.sum(do_v.astype(F32) * o_r[rows, hs].astype(F32), axis=1, keepdims=True)
            lse_v = jnp.max(l_r[rows, hs], axis=1, keepdims=True)
            shift = -delta
            if has_dl:
                shift = shift + jnp.max(dl_r[rows, hs], axis=1, keepdims=True)
            return do_v, delta, lse_v, shift

        for kv in range(cfg.kv_heads):
            ks = slice(kv * HEAD_DIM, (kv + 1) * HEAD_DIM)
            dk_acc[...] = jnp.zeros_like(dk_acc)
            dv_acc[...] = jnp.zeros_like(dv_acc)
            for g in range(grp):
                h = kv * grp + g
                hs = slice(h * HEAD_DIM, (h + 1) * HEAD_DIM)
                bias = distf * (-(_slope(cfg.head0 + h) * d))
                for a in range(nsub):
                    rows = slice(a * BLOCK, (a + 1) * BLOCK)
                    win = slice(a * BLOCK, (a + 2) * BLOCK)
                    q = q_ref[rows, hs]
                    k = kf[win, ks]
                    do_v, delta, lse_v, shift = stats(o_ref, do_ref, l_ref, dl_ref if has_dl else None, rows, hs)
                    s = _dot_nt(q, k) * scale
                    s = jnp.where(valid_first if a == 0 else valid, s + bias, NEG_INF)
                    p = jnp.exp(s - lse_v)
                    ds = p * (_dot_nt(do_v, vf[win, ks]) + shift)
                    dsb, pb = ds.astype(BF16), p.astype(BF16)
                    out_ref[rows, hs] = (_dot(dsb, k) * scale).astype(BF16)
                    if a == 0:
                        dk_acc[0:BLOCK, :] += _dot_tn(dsb[:, BLOCK:], q) * scale
                        dv_acc[0:BLOCK, :] += _dot_tn(pb[:, BLOCK:], do_v)
                    else:
                        krows = slice((a - 1) * BLOCK, (a + 1) * BLOCK)
                        dk_acc[krows, :] += _dot_tn(dsb, q) * scale
                        dv_acc[krows, :] += _dot_tn(pb, do_v)
                    if cfg.sinks:
                        psink = jnp.exp(sink_ref[h] - lse_v)
                        tot = jnp.sum(psink * (-delta), axis=0, keepdims=True)
                        dsink_ref[h:h + 1, :] += jnp.broadcast_to(tot, (1, 128))
                rows = slice(0, BLOCK)
                last = slice((nsub - 1) * BLOCK, nsub * BLOCK)
                klast = slice(nsub * BLOCK, (nsub + 1) * BLOCK)
                q = qn_ref[rows, hs]
                k = kf[klast, ks]
                do_v, delta, lse_v, shift = stats(on_ref, don_ref, ln_ref, dln_ref if has_dl else None, rows, hs)
                s = _dot_nt(q, k) * scale
                s = jnp.where(valid_next, s + bias[:, 0:BLOCK], NEG_INF)
                p = jnp.exp(s - lse_v)
                ds = p * (_dot_nt(do_v, vf[klast, ks]) + shift)
                dk_acc[last, :] += _dot_tn(ds.astype(BF16), q) * scale
                dv_acc[last, :] += _dot_tn(p.astype(BF16), do_v)
            out_ref[:, wq + kv * HEAD_DIM: wq + (kv + 1) * HEAD_DIM] = dk_acc[...].astype(BF16)
            out_ref[:, wq + wk + kv * HEAD_DIM: wq + wk + (kv + 1) * HEAD_DIM] = dv_acc[...].astype(BF16)

    prev = lambda r, i: jnp.maximum(i * nsub - 1, 0)
    nxt = lambda r, i: jnp.minimum((i + 1) * nsub, n // BLOCK - 1)
    cur_w = lambda w, c: pl.BlockSpec((tq, w), lambda r, i: (i, (r * wtot + c) // w))
    prev_w = lambda w, c: pl.BlockSpec((BLOCK, w), lambda r, i: (prev(r, i), (r * wtot + c) // w))
    o_cur = pl.BlockSpec((tq, wq), lambda r, i: (i, r))
    o_nxt = pl.BlockSpec((BLOCK, wq), lambda r, i: (nxt(r, i), r))
    in_specs = [pl.BlockSpec(memory_space=pltpu.SMEM),
                cur_w(wq, qc), pl.BlockSpec((BLOCK, wq), lambda r, i: (nxt(r, i), (r * wtot + qc) // wq)),
                cur_w(wk, kc), prev_w(wk, kc), cur_w(wk, vc), prev_w(wk, vc),
                o_cur, o_nxt, o_cur, o_nxt, o_cur, o_nxt]
    args = [sinks, view, view, view, view, view, view, o, o, do, do, lse, lse]
    if has_dl:
        in_specs += [o_cur, o_nxt]
        args += [dlse, dlse]
    out_shape = [jax.ShapeDtypeStruct((n, d * wout), BF16)]
    out_specs = [pl.BlockSpec((tq, wout), lambda r, i: (i, r))]
    if cfg.sinks:
        out_shape.append(jax.ShapeDtypeStruct((8, 128), F32))
        out_specs.append(pl.BlockSpec((8, 128), lambda r, i: (0, 0)))
    return pl.pallas_call(
        body, name=name, grid=(d, nqb), out_shape=out_shape, in_specs=in_specs, out_specs=out_specs,
        scratch_shapes=[pltpu.VMEM((tq + BLOCK, wk), BF16), pltpu.VMEM((tq + BLOCK, wk), BF16),
                        pltpu.VMEM((tq, HEAD_DIM), F32), pltpu.VMEM((tq, HEAD_DIM), F32)],
        compiler_params=_cp("arbitrary", "arbitrary"),
    )(*args)


def _adamw(g, w, m, v):
    m = ADAM_B1 * m + (1.0 - ADAM_B1) * g
    v = ADAM_B2 * v + (1.0 - ADAM_B2) * (g * g)
    m_hat = m / (1.0 - ADAM_B1 ** ADAM_STEP)
    v_hat = v / (1.0 - ADAM_B2 ** ADAM_STEP)
    delta = -ADAM_LR * (m_hat / (jnp.sqrt(v_hat) + ADAM_EPS) + ADAM_WD * w)
    return delta, m, v


def adam_reduce(parts, w, m, v, name):
    r, c = w.shape
    tr = next(cand for cand in (256, 128, 64, 32, 16, 8) if r % cand == 0) if r > 256 else r

    def body(p_ref, w_ref, m_ref, v_ref, g_ref, d_ref, mo_ref, vo_ref):
        g = p_ref[0].astype(F32)
        for j in range(1, N_DEV):
            g = g + p_ref[j].astype(F32)
        g_ref[...] = g
        d_ref[...], mo_ref[...], vo_ref[...] = _adamw(g, w_ref[...], m_ref[...], v_ref[...])

    row = pl.BlockSpec((tr, c), lambda i: (i, 0))
    return pl.pallas_call(
        body, name=name, grid=(r // tr,),
        out_shape=[jax.ShapeDtypeStruct((r, c), F32)] * 4,
        in_specs=[pl.BlockSpec((N_DEV, tr, c), lambda i: (0, i, 0)), row, row, row],
        out_specs=[row] * 4,
        compiler_params=_cp("parallel"),
    )(parts, w, m, v)


def adam_w_ada(sct, dm_loc, w, m, v):
    nl, dm, wc = w.shape
    tr = 512

    def body(s_ref, d_ref, w_ref, m_ref, v_ref, g_ref, dl_ref, mo_ref, vo_ref):
        g = jnp.dot(s_ref[...], d_ref[...], preferred_element_type=F32, precision=lax.Precision.HIGHEST)
        g_ref[...] = g
        dl_ref[...], mo_ref[...], vo_ref[...] = _adamw(g, w_ref[...], m_ref[...], v_ref[...])

    blk = pl.BlockSpec((None, tr, wc), lambda l, i: (l, i, 0))
    return pl.pallas_call(
        body, name="adam_w_ada", grid=(nl, dm // tr),
        out_shape=[jax.ShapeDtypeStruct(w.shape, F32)] * 4,
        in_specs=[pl.BlockSpec((tr, 128), lambda l, i: (i, 0)),
                  pl.BlockSpec((None, 128, wc), lambda l, i: (l, 0, 0)), blk, blk, blk],
        out_specs=[blk] * 4,
        compiler_params=_cp("parallel", "parallel"),
    )(sct, dm_loc, w, m, v)


BIG = (("w_in", 1), ("w_a", 1), ("w_b", 1), ("w_o", 0), ("w_gate", 1), ("w_up", 1), ("w_down", 0))
PACK_C = 1024


def _pack_local(arrs):
    flat = jnp.concatenate([a.reshape(-1) for a in arrs])
    return flat.reshape(-1, PACK_C)


def _unpack_local(flat, shapes):
    out, off = [], 0
    flat = flat.reshape(-1)
    for shp in shapes:
        n = shp[0] * shp[1] * shp[2]
        out.append(flat[off:off + n].reshape(shp))
        off += n
    return out


def _full_from_gathered(slab, shp, axis):
    nl, r, c = shp
    s = slab.reshape(N_DEV, nl, r, c)
    if axis == 1:
        return s.transpose(1, 2, 0, 3).reshape(nl, r, N_DEV * c)
    return s.transpose(1, 0, 2, 3).reshape(nl, N_DEV * r, c)


def _shards_from_full(full, axis):
    nl, r, c = full.shape
    if axis == 1:
        s = full.reshape(nl, r, N_DEV, c // N_DEV).transpose(2, 0, 1, 3)
    else:
        s = full.reshape(nl, N_DEV, r // N_DEV, c).transpose(1, 0, 2, 3)
    return s.reshape(N_DEV, -1)


def _perm_in(w):
    hb = len(B_GROUPS) * B_OUT_W
    qb, kb, vb = A_W, A_W + hb, A_W + 2 * hb
    cols = [w[..., 0:A_W]]
    for g in range(len(B_GROUPS)):
        for base in (qb, kb, vb):
            cols.append(w[..., base + g * B_OUT_W: base + (g + 1) * B_OUT_W])
    cols.append(w[..., A_W + 3 * hb:])
    return jnp.concatenate(cols, axis=-1)


def _unperm_in(g):
    hb = len(B_GROUPS) * B_OUT_W
    grp = lambda gi, part: g[..., A_W + gi * B_GW + part * B_OUT_W: A_W + gi * B_GW + (part + 1) * B_OUT_W]
    cols = [g[..., 0:A_W]]
    for part in range(3):
        for gi in range(len(B_GROUPS)):
            cols.append(grp(gi, part))
    cols.append(g[..., A_W + 3 * hb:])
    return jnp.concatenate(cols, axis=-1)


def kernel(x, c, w_ada, b_ada, w_in, sinks, w_a, w_b, w_o, ln1_g, ln1_b, w_gate, w_up, w_down, ln2_g, ln2_b, loss_target, m_w_ada, m_b_ada, m_w_in, m_sinks, m_w_a, m_w_b, m_w_o, m_ln1_g, m_ln1_b, m_w_gate, m_w_up, m_w_down, m_ln2_g, m_ln2_b, v_w_ada, v_b_ada, v_w_in, v_sinks, v_w_a, v_w_b, v_w_o, v_ln1_g, v_ln1_b, v_w_gate, v_w_up, v_w_down, v_ln2_g, v_ln2_b):
    given = dict(locals())
    nl = w_in.shape[0]
    t, dm = x.shape[1], x.shape[2]
    me = 4 * lax.axis_index("x") + 2 * lax.axis_index("y") + lax.axis_index("c")
    x0 = x.reshape(t, dm)
    target = loss_target.reshape(t, dm)

    big_shapes = [given[nm].shape for nm, _ in BIG]
    packed = _pack_local([given[nm].astype(BF16) for nm, _ in BIG])
    gathered = all_gather_big(packed, "gather_weights").reshape(N_DEV, -1)
    full, off = {}, 0
    for (nm, axis), shp in zip(BIG, big_shapes):
        n = shp[0] * shp[1] * shp[2]
        full[nm] = _full_from_gathered(gathered[:, off:off + n], shp, axis)
        off += n
    w_in_p = _perm_in(full["w_in"])
    w_in_t = w_in_p.transpose(0, 2, 1)
    w_ffn = jnp.concatenate([full["w_gate"], full["w_up"]], axis=-1)
    w_ffn_t = w_ffn.transpose(0, 2, 1)
    w_a_t, w_b_t = full["w_a"].transpose(0, 2, 1), full["w_b"].transpose(0, 2, 1)
    w_o_t, w_d_t = full["w_o"].transpose(0, 2, 1), full["w_down"].transpose(0, 2, 1)

    wc = w_ada.shape[2]
    c_all = all_gather_small(jnp.broadcast_to(c, (8, dm)), "gather_c")[:, 0, :]
    b_loc = lax.dynamic_slice_in_dim(b_ada, me * wc, wc, axis=1).reshape(nl, 1, wc)
    mp, sc_all = mod_partial(c_all, w_ada, b_loc)
    mp_all = all_gather_small(mp.reshape(nl * N_DEV, wc), "gather_mod").reshape(N_DEV, nl, N_DEV, wc)
    mod = lax.dynamic_index_in_dim(mp_all, me, axis=2, keepdims=False)
    mod = mod.transpose(1, 0, 2).reshape(nl, 6, 1, dm)

    vec = lambda a, l: a[l].reshape(1, dm)
    sink_pad = jnp.pad(sinks, ((0, 0), (0, 0)))

    saved = []
    xl = x0
    for l in range(nl):
        sh1, s1, g1, sh2, s2, g2 = [mod[l, j] for j in range(6)]
        u1, qkvg = modmm(xl, s1, sh1, w_in_p[l], "in_proj")
        ya, lse_a = attn_fwd(qkvg, ATTN_A, sink_pad[l], "attn_a_fwd")
        o_g, l_g = [], []
        for g, cfg in enumerate(ATTN_B):
            o, ls = attn_fwd(qkvg, cfg, sink_pad[l], "attn_b%d_fwd" % g)
            o_g.append(o.reshape(t, B_OUT_W))
            l_g.append(ls.reshape(t, B_OUT_W))
        yb, merged = mix_merge(ya, o_g, l_g, qkvg, full["w_a"][l], full["w_b"][l], "mix_merge")
        y1, x1, zh1, rs1 = proj_ln(merged, full["w_o"][l], xl, g1, vec(ln1_g, l), vec(ln1_b, l), "out_proj_ln")
        u2, ab = modmm(x1, s2, sh2, w_ffn[l], "ffn_up")
        h, y2, x2, zh2, rs2 = swiglu_proj_ln(ab, full["w_down"][l], x1, g2, vec(ln2_g, l), vec(ln2_b, l),
                                             "ffn_down_ln")
        saved.append(dict(xin=xl, u1=u1, qkvg=qkvg, ya=ya, lse_a=lse_a, o_g=o_g, l_g=l_g, yb=yb, merged=merged,
                          y1=y1, x1=x1, zh1=zh1, rs1=rs1, u2=u2, ab=ab, h=h, y2=y2, zh2=zh2, rs2=rs2))
        xl = x2

    dx, loss_part = loss_head(xl, target)

    grads = {nm: [None] * nl for nm, _ in BIG}
    small = {k: [None] * nl for k in ("dmod", "ln1_g", "ln1_b", "ln2_g", "ln2_b", "sinks")}
    for l in reversed(range(nl)):
        sv = saved[l]
        sh1, s1, g1, sh2, s2, g2 = [mod[l, j] for j in range(6)]
        dz2, dy2, sg, sb, sgate2 = ln_bwd(dx, sv["zh2"], sv["rs2"], sv["y2"], vec(ln2_g, l), g2, "ln_bwd")
        small["ln2_g"][l], small["ln2_b"][l] = sg[0], sb[0]
        grads["w_down"][l] = wgrad(sv["h"], dy2, dm, 0, "wgrad_down")
        dab = dswiglu(dy2, w_d_t[l], sv["ab"], "dswiglu")
        wg_up = wgrad(sv["u2"], dab, 2 * D_FF, 0, "wgrad_ffn_up")
        grads["w_gate"][l], grads["w_up"][l] = wg_up[:, :D_FF], wg_up[:, D_FF:]
        dx1, ss2, ssh2 = dgrad_mod([(dab, 2 * D_FF, 0, w_ffn_t[l], 0)], dz2, sv["x1"], s2, "dgrad_ffn")
        dz1, do1, sg, sb, sgate1 = ln_bwd(dx1, sv["zh1"], sv["rs1"], sv["y1"], vec(ln1_g, l), g1, "ln_bwd")
        small["ln1_g"][l], small["ln1_b"][l] = sg[0], sb[0]
        grads["w_o"][l] = wgrad(sv["merged"], do1, dm, 0, "wgrad_o")
        dpab, dya, dyb, dgab = dmerge(do1, w_o_t[l], sv["ya"], sv["yb"], full["w_a"][l], full["w_b"][l],
                                      w_a_t[l], w_b_t[l], sv["qkvg"], "dmerge")
        grads["w_a"][l] = wgrad(sv["ya"], dpab, dm, 0, "wgrad_a")
        grads["w_b"][l] = wgrad(sv["yb"], dpab, dm, 1, "wgrad_b")
        do_g, dl_g = (lambda r: (r[:3], r[3:]))(mix_bwd(dyb, sv["o_g"], sv["l_g"], "mix_bwd"))
        d_a, dsink = attn_bwd(sv["qkvg"], sv["ya"], sv["lse_a"], dya, None, ATTN_A, sink_pad[l], "attn_a_bwd")
        small["sinks"][l] = dsink[:, 0]
        d_b = []
        for g, cfg in enumerate(ATTN_B):
            n_g = t // cfg.dil
            view = lambda a: a.reshape(n_g, cfg.dil * B_OUT_W)
            (db,) = attn_bwd(sv["qkvg"], view(sv["o_g"][g]), view(sv["l_g"][g]), view(do_g[g]), view(dl_g[g]),
                             cfg, sink_pad[l], "attn_b%d_bwd" % g)
            d_b.append(db.reshape(t, B_GW))
        parts = [(d_a, A_W, 0, w_in_t[l], 0)]
        parts += [(d_b[g], B_GW, 0, w_in_t[l], 1 + g) for g in range(len(ATTN_B))]
        gate_row0 = (A_W + len(ATTN_B) * B_GW) // dm
        parts += [(dgab, dm, j, w_in_t[l], gate_row0 + j) for j in range(2)]
        dx, ss1, ssh1 = dgrad_mod(parts, dz1, sv["xin"], s1, "dgrad_in")
        g_in = [wgrad(sv["u1"], d_a, A_W, 0, "wgrad_in_a")]
        g_in += [wgrad(sv["u1"], d_b[g], B_GW, 0, "wgrad_in_b") for g in range(len(ATTN_B))]
        g_in.append(wgrad(sv["u1"], dgab, 2 * dm, 0, "wgrad_in_gate"))
        grads["w_in"][l] = _unperm_in(jnp.concatenate(g_in, axis=-1))
        small["dmod"][l] = jnp.stack([ssh1[0], ss1[0], sgate1[0], ssh2[0], ss2[0], sgate2[0]])
    grad_x = dx.reshape(x.shape)

    gp = jnp.concatenate([_shards_from_full(jnp.stack(grads[nm]), axis) for nm, axis in BIG], axis=1)
    gp = gp.astype(BF16).reshape(N_DEV, -1, PACK_C)
    recv = exchange_big(gp, "scatter_grads")
    pk = lambda pre: _pack_local([given[pre + nm] for nm, _ in BIG])
    outs = adam_reduce(recv, pk(""), pk("m_"), pk("v_"), "adam_big")
    big_out = [dict(zip([nm for nm, _ in BIG], _unpack_local(o, big_shapes))) for o in outs]

    rows = jnp.concatenate(
        [jnp.stack(small["dmod"]).reshape(nl * 6, dm)]
        + [jnp.stack(small[k]) for k in ("ln1_g", "ln1_b", "ln2_g", "ln2_b")]
        + [jnp.pad(jnp.stack(small["sinks"]).reshape(1, -1), ((0, 0), (0, dm - nl * A_Q_HEADS))),
           jnp.broadcast_to(loss_part[0:1, 0:1], (1, dm))])
    n_rows = rows.shape[0]
    rows = jnp.pad(rows, ((0, -n_rows % 8), (0, 0)))
    rows_all = all_gather_small(rows, "gather_small_grads")

    def pack_small(pre):
        parts = [given[pre + "b_ada"].reshape(nl * 6, dm)]
        parts += [given[pre + k] for k in ("ln1_g", "ln1_b", "ln2_g", "ln2_b")]
        parts.append(jnp.pad(given[pre + "sinks"].reshape(1, -1), ((0, 0), (0, dm - nl * A_Q_HEADS))))
        p = jnp.concatenate(parts)
        return jnp.pad(p, ((0, rows.shape[0] - p.shape[0]), (0, 0)))

    souts = adam_reduce(rows_all, pack_small(""), pack_small("m_"), pack_small("v_"), "adam_small")

    def unpack_small(o):
        r = {"b_ada": o[0:nl * 6].reshape(nl, 6 * dm)}
        for j, k in enumerate(("ln1_g", "ln1_b", "ln2_g", "ln2_b")):
            r[k] = o[nl * 6 + j * nl: nl * 6 + (j + 1) * nl]
        r["sinks"] = o[nl * 10, 0:nl * A_Q_HEADS].reshape(nl, A_Q_HEADS)
        return r

    small_out = [unpack_small(o) for o in souts]
    loss = souts[0][nl * 10 + 1, 0]

    dmod_all = rows_all[:, 0:nl * 6].reshape(N_DEV, nl, 6 * dm)
    dm_loc = lax.dynamic_slice_in_dim(dmod_all, me * wc, wc, axis=2).transpose(1, 0, 2)
    dm_loc = jnp.pad(dm_loc, ((0, 0), (0, 128 - N_DEV), (0, 0)))
    sct = jnp.pad(sc_all.T, ((0, 0), (0, 128 - N_DEV)))
    ada_out = adam_w_ada(sct, dm_loc, w_ada, m_w_ada, v_w_ada)

    names = ["w_ada", "b_ada", "w_in", "sinks", "w_a", "w_b", "w_o", "ln1_g", "ln1_b",
             "w_gate", "w_up", "w_down", "ln2_g", "ln2_b"]

    def pick(kind, nm):
        if nm == "w_ada":
            return ada_out[kind]
        if nm in small_out[kind]:
            return small_out[kind][nm]
        return big_out[kind][nm]

    result = [loss, grad_x]
    for kind in range(4):
        result += [pick(kind, nm) for nm in names]
    return tuple(result)
```

```python
import functools

import jax
import jax.numpy as jnp
from jax import lax
from jax.experimental import pallas as pl
from jax.experimental.pallas import tpu as pltpu

F32 = jnp.float32
BF16 = jnp.bfloat16

D_MODEL = 1024
HEAD_DIM = 64
A_Q_HEADS = 8
A_KV_HEADS = 2
A_WINDOW = 128
B_GROUPS = ((128, 1), (512, 4), (2048, 16))
N_GROUPS = len(B_GROUPS)
B_HEADS_PER_GROUP = 4
N_ATTN_HEADS = A_Q_HEADS + B_HEADS_PER_GROUP * N_GROUPS
BLOCK = 128
A_W = (A_Q_HEADS + 2 * A_KV_HEADS) * HEAD_DIM
B_OUT_W = B_HEADS_PER_GROUP * HEAD_DIM
B_GW = 3 * B_OUT_W
B_ALL = N_GROUPS * B_OUT_W
GATE_COL = A_W + 3 * B_ALL
D_FF = 2816
DN_ALPHA = 8.0 ** 0.25
LN_EPS = 1e-5
NEG_INF = -1e30
ADAM_LR, ADAM_B1, ADAM_B2, ADAM_EPS, ADAM_WD, ADAM_STEP = 0.001, 0.9, 0.999, 1e-08, 0.01, 10

N_DEV = 8
MESH = pl.DeviceIdType.MESH
VMEM_LIMIT = 56 * 1024 * 1024
ROW_TILE = 512
LANES = 128
BF16_ROWS = 16


def _cp(*sem):
    return pltpu.CompilerParams(dimension_semantics=sem, vmem_limit_bytes=VMEM_LIMIT)


def _row_tile(t):
    return min(ROW_TILE, t)


def _slope(head):
    return 2.0 ** (-8.0 * (head + 1) / N_ATTN_HEADS)


def _sigmoid(x):
    return 1.0 / (1.0 + jnp.exp(-x))


def _dot(a, b):
    return jnp.dot(a, b, preferred_element_type=F32)


def _dot_nt(a, b):
    return lax.dot_general(a, b, (((1,), (1,)), ((), ())), preferred_element_type=F32)


def _dot_tn(a, b):
    return lax.dot_general(a, b, (((0,), (0,)), ((), ())), preferred_element_type=F32)


def _fold_scratch(tm, w):
    return [pltpu.VMEM((tm, LANES), F32)] * (w // LANES)


def _fold_to(dst_ref, val, scrs, d, col0=0):
    tm, w = val.shape
    if d == 1:
        dst_ref[0, :, col0:col0 + w] = val.astype(dst_ref.dtype)
        return
    for cb in range(w // LANES):
        scrs[cb][...] = val[:, cb * LANES:(cb + 1) * LANES]
    for r in range(d):
        for cb in range(w // LANES):
            piece = scrs[cb][pl.ds(r, tm // d, stride=d), :]
            dst_ref[r, :, col0 + cb * LANES:col0 + (cb + 1) * LANES] = piece.astype(dst_ref.dtype)


def _unfold_rows(rows_of, scrs, d, n, w):
    for r in range(d):
        for cb in range(w // LANES):
            scrs[cb][pl.ds(r, n, stride=d), :] = rows_of(r, slice(cb * LANES, (cb + 1) * LANES)).astype(F32)
    return jnp.concatenate([scrs[cb][...] for cb in range(w // LANES)], axis=1)


def _unfold_from(src_ref, scrs, d):
    if d == 1:
        return src_ref[0].astype(F32)
    _, n, w = src_ref.shape
    return _unfold_rows(lambda r, cols: src_ref[r, :, cols], scrs, d, n, w)


def _folded_spec(d, tm, w):
    return pl.BlockSpec((d, tm // d, w), lambda i: (0, i, 0))


def _me():
    return lax.axis_index("x"), lax.axis_index("y"), lax.axis_index("c")


def _flip(v, bit):
    return 1 - v if bit else v


def _peer(k):
    x, y, c = _me()
    return (_flip(x, k & 4), _flip(y, k & 2), _flip(c, k & 1))


def _peer_index(k):
    px, py, pc = _peer(k)
    return 4 * px + 2 * py + pc


def all_gather_small(v, name):
    r, c = v.shape

    def body(v_ref, out_ref, send_sems, recv_sems):
        me = _peer_index(0)
        out_ref[me] = v_ref[...]
        copies = []
        for k in range(1, N_DEV):
            cp = pltpu.make_async_remote_copy(
                src_ref=v_ref, dst_ref=out_ref.at[me],
                send_sem=send_sems.at[k - 1], recv_sem=recv_sems.at[k - 1],
                device_id=_peer(k), device_id_type=MESH)
            cp.start()
            copies.append(cp)
        for k in range(1, N_DEV):
            pltpu.make_async_remote_copy(
                src_ref=v_ref, dst_ref=out_ref.at[_peer_index(k)],
                send_sem=send_sems.at[k - 1], recv_sem=recv_sems.at[k - 1],
                device_id=_peer(k), device_id_type=MESH).wait_recv()
        for cp in copies:
            cp.wait_send()

    return pl.pallas_call(
        body, name=name,
        out_shape=jax.ShapeDtypeStruct((N_DEV, r, c), v.dtype),
        in_specs=[pl.BlockSpec(memory_space=pltpu.VMEM)],
        out_specs=pl.BlockSpec(memory_space=pltpu.VMEM),
        scratch_shapes=[pltpu.SemaphoreType.DMA((N_DEV - 1,)), pltpu.SemaphoreType.DMA((N_DEV - 1,))],
        compiler_params=pltpu.CompilerParams(vmem_limit_bytes=VMEM_LIMIT),
    )(v)


class Piece:
    def __init__(self, name, buf, axis, base, size):
        self.name, self.buf, self.axis, self.base, self.size = name, buf, axis, base, size

    def window(self, ref, j):
        start = self.base + j * self.size
        if self.axis == 1:
            return ref.at[:, :, pl.ds(pl.multiple_of(start, LANES), self.size)]
        return ref.at[:, pl.ds(pl.multiple_of(start, BF16_ROWS), self.size), :]


def _sem_scratch(n_pieces):
    return [pltpu.SemaphoreType.DMA((n_pieces, N_DEV - 1)), pltpu.SemaphoreType.DMA((n_pieces, N_DEV - 1)),
            pltpu.SemaphoreType.DMA((n_pieces,))]


def gather_weights(shards, pieces, buf_shapes, name):
    ns, nb = len(pieces), len(buf_shapes)
    buf_of = {nm: i for i, nm in enumerate(buf_shapes)}

    def body(*refs):
        srcs, bufs = refs[:ns], refs[ns:ns + nb]
        send_sems, recv_sems, local_sems = refs[ns + nb:]
        me = _peer_index(0)
        local, remote = [], []
        for pi, pc in enumerate(pieces):
            mine = pc.window(bufs[buf_of[pc.buf]], me)
            cp = pltpu.make_async_copy(srcs[pi], mine, local_sems.at[pi])
            cp.start()
            local.append(cp)
            for k in range(1, N_DEV):
                cp = pltpu.make_async_remote_copy(
                    src_ref=srcs[pi], dst_ref=mine,
                    send_sem=send_sems.at[pi, k - 1], recv_sem=recv_sems.at[pi, k - 1],
                    device_id=_peer(k), device_id_type=MESH)
                cp.start()
                remote.append(cp)
        for pi, pc in enumerate(pieces):
            for k in range(1, N_DEV):
                pltpu.make_async_remote_copy(
                    src_ref=srcs[pi], dst_ref=pc.window(bufs[buf_of[pc.buf]], _peer_index(k)),
                    send_sem=send_sems.at[pi, k - 1], recv_sem=recv_sems.at[pi, k - 1],
                    device_id=_peer(k), device_id_type=MESH).wait_recv()
        for cp in remote:
            cp.wait_send()
        for cp in local:
            cp.wait()

    return pl.pallas_call(
        body, name=name,
        out_shape=[jax.ShapeDtypeStruct(shp, BF16) for shp in buf_shapes.values()],
        in_specs=[pl.BlockSpec(memory_space=pl.ANY)] * ns,
        out_specs=[pl.BlockSpec(memory_space=pl.ANY)] * nb,
        scratch_shapes=_sem_scratch(ns),
    )(*shards)


def scatter_grads(bufs, pieces, shard_shapes, name):
    ns, nb = len(pieces), len(bufs)
    buf_of = {nm: i for i, nm in enumerate(bufs)}

    def body(*refs):
        srcs, outs = refs[:nb], refs[nb:nb + ns]
        send_sems, recv_sems, local_sems = refs[nb + ns:]
        me = _peer_index(0)
        local, remote = [], []
        for pi, pc in enumerate(pieces):
            src = srcs[buf_of[pc.buf]]
            cp = pltpu.make_async_copy(pc.window(src, me), outs[pi].at[me], local_sems.at[pi])
            cp.start()
            local.append(cp)
            for k in range(1, N_DEV):
                cp = pltpu.make_async_remote_copy(
                    src_ref=pc.window(src, _peer_index(k)), dst_ref=outs[pi].at[me],
                    send_sem=send_sems.at[pi, k - 1], recv_sem=recv_sems.at[pi, k - 1],
                    device_id=_peer(k), device_id_type=MESH)
                cp.start()
                remote.append(cp)
        for pi, pc in enumerate(pieces):
            src = srcs[buf_of[pc.buf]]
            for k in range(1, N_DEV):
                pltpu.make_async_remote_copy(
                    src_ref=pc.window(src, me), dst_ref=outs[pi].at[_peer_index(k)],
                    send_sem=send_sems.at[pi, k - 1], recv_sem=recv_sems.at[pi, k - 1],
                    device_id=_peer(k), device_id_type=MESH).wait_recv()
        for cp in remote:
            cp.wait_send()
        for cp in local:
            cp.wait()

    return pl.pallas_call(
        body, name=name,
        out_shape=[jax.ShapeDtypeStruct((N_DEV,) + tuple(shp), BF16) for shp in shard_shapes],
        in_specs=[pl.BlockSpec(memory_space=pl.ANY)] * nb,
        out_specs=[pl.BlockSpec(memory_space=pl.ANY)] * ns,
        scratch_shapes=_sem_scratch(ns),
    )(*bufs.values())


def mod_partial(c_all, w_ada, b_loc):
    nl, dm, wc = w_ada.shape

    def body(c_ref, w_ref, b_ref, o_ref, sc_ref):
        cc = c_ref[...]
        sc = cc * _sigmoid(cc)
        sc_ref[...] = sc
        o_ref[...] = jnp.dot(sc, w_ref[...], preferred_element_type=F32,
                             precision=lax.Precision.HIGHEST) + b_ref[...]

    return pl.pallas_call(
        body, name="mod_partial", grid=(nl,),
        out_shape=[jax.ShapeDtypeStruct((nl, N_DEV, wc), F32), jax.ShapeDtypeStruct((N_DEV, dm), F32)],
        in_specs=[pl.BlockSpec((N_DEV, dm), lambda l: (0, 0)),
                  pl.BlockSpec((None, dm, wc), lambda l: (l, 0, 0)),
                  pl.BlockSpec((None, 1, wc), lambda l: (l, 0, 0))],
        out_specs=[pl.BlockSpec((None, N_DEV, wc), lambda l: (l, 0, 0)),
                   pl.BlockSpec((N_DEV, dm), lambda l: (0, 0))],
        compiler_params=_cp("arbitrary"),
    )(c_all, w_ada, b_loc)


def in_proj(x, s, sh, w, name):
    t, dm = x.shape
    n = w.shape[1]
    tm = _row_tile(t)
    ch = B_OUT_W
    dils = [dil for _, dil in B_GROUPS if dil > 1]

    def body(x_ref, s_ref, sh_ref, w_ref, u_ref, *rest):
        uf_refs, o_ref, qf_refs = rest[:len(dils)], rest[len(dils)], rest[len(dils) + 1:len(dils) * 2 + 1]
        scrs = rest[len(dils) * 2 + 1:]
        uf = x_ref[...] * (1.0 + s_ref[...]) + sh_ref[...]
        u = uf.astype(BF16)
        u_ref[...] = u
        for d, uf_ref in zip(dils, uf_refs):
            _fold_to(uf_ref, uf, scrs, d)
        for c0 in range(0, n, ch):
            res = _dot(u, w_ref[:, c0:c0 + ch])
            o_ref[:, c0:c0 + ch] = res.astype(BF16)
            if A_W <= c0 < GATE_COL:
                part, g = divmod((c0 - A_W) // ch, N_GROUPS)
                d = B_GROUPS[g][1]
                if d > 1:
                    _fold_to(qf_refs[dils.index(d)], res, scrs, d, part * ch)

    vec = pl.BlockSpec((1, dm), lambda i: (0, 0))
    row = lambda w_: pl.BlockSpec((tm, w_), lambda i: (i, 0))
    return pl.pallas_call(
        body, name=name, grid=(t // tm,),
        out_shape=[jax.ShapeDtypeStruct((t, dm), BF16)]
                  + [jax.ShapeDtypeStruct((d, t // d, dm), BF16) for d in dils]
                  + [jax.ShapeDtypeStruct((t, n), BF16)]
                  + [jax.ShapeDtypeStruct((d, t // d, B_GW), BF16) for d in dils],
        in_specs=[row(dm), vec, vec, pl.BlockSpec((dm, n), lambda i: (0, 0))],
        out_specs=[row(dm)] + [_folded_spec(d, tm, dm) for d in dils] + [row(n)]
                  + [_folded_spec(d, tm, B_GW) for d in dils],
        scratch_shapes=_fold_scratch(tm, dm),
        compiler_params=_cp("parallel"),
    )(x, s, sh, w)


def modmm(x, s, sh, w, name):
    t, dm = x.shape
    n = w.shape[1]
    tm = _row_tile(t)
    ch = 512

    def body(x_ref, s_ref, sh_ref, w_ref, u_ref, o_ref):
        u = (x_ref[...] * (1.0 + s_ref[...]) + sh_ref[...]).astype(BF16)
        u_ref[...] = u
        for c0 in range(0, n, ch):
            o_ref[:, c0:c0 + ch] = _dot(u, w_ref[:, c0:c0 + ch]).astype(BF16)

    vec = pl.BlockSpec((1, dm), lambda i: (0, 0))
    return pl.pallas_call(
        body, name=name, grid=(t // tm,),
        out_shape=[jax.ShapeDtypeStruct((t, dm), BF16), jax.ShapeDtypeStruct((t, n), BF16)],
        in_specs=[pl.BlockSpec((tm, dm), lambda i: (i, 0)), vec, vec,
                  pl.BlockSpec((dm, n), lambda i: (0, 0))],
        out_specs=[pl.BlockSpec((tm, dm), lambda i: (i, 0)), pl.BlockSpec((tm, n), lambda i: (i, 0))],
        compiler_params=_cp("parallel"),
    )(x, s, sh, w)


def _ln_store(y, xres_ref, g_ref, lg_ref, lb_ref, y_ref, xo_ref, zh_ref, rs_ref):
    y_ref[...] = y.astype(BF16)
    z = DN_ALPHA * xres_ref[...] + g_ref[...] * y
    mu = jnp.mean(z, axis=1, keepdims=True)
    zc = z - mu
    var = jnp.mean(zc * zc, axis=1, keepdims=True)
    rstd = lax.rsqrt(var + LN_EPS)
    zhat = zc * rstd
    zh_ref[...] = zhat
    xo_ref[...] = zhat * lg_ref[...] + lb_ref[...]
    rs_ref[...] = jnp.broadcast_to(rstd, rs_ref.shape)


def _ln_out_shapes(t, dm):
    return [jax.ShapeDtypeStruct((t, dm), BF16), jax.ShapeDtypeStruct((t, dm), F32),
            jax.ShapeDtypeStruct((t, dm), F32), jax.ShapeDtypeStruct((t, LANES), F32)]


def _ln_out_specs(tm, dm):
    row = pl.BlockSpec((tm, dm), lambda i: (i, 0))
    return [row, row, row, pl.BlockSpec((tm, LANES), lambda i: (i, 0))]


def proj_ln(a, w, xres, gate, lg, lb, name):
    t, k = a.shape
    dm = w.shape[1]
    tm = _row_tile(t)

    def body(a_ref, w_ref, xres_ref, g_ref, lg_ref, lb_ref, y_ref, xo_ref, zh_ref, rs_ref):
        y = _dot(a_ref[...], w_ref[...])
        _ln_store(y, xres_ref, g_ref, lg_ref, lb_ref, y_ref, xo_ref, zh_ref, rs_ref)

    vec = pl.BlockSpec((1, dm), lambda i: (0, 0))
    return pl.pallas_call(
        body, name=name, grid=(t // tm,),
        out_shape=_ln_out_shapes(t, dm),
        in_specs=[pl.BlockSpec((tm, k), lambda i: (i, 0)), pl.BlockSpec((k, dm), lambda i: (0, 0)),
                  pl.BlockSpec((tm, dm), lambda i: (i, 0)), vec, vec, vec],
        out_specs=_ln_out_specs(tm, dm),
        compiler_params=_cp("parallel"),
    )(a, w, xres, gate, lg, lb)


def swiglu_proj_ln(ab, w, xres, gate, lg, lb, name):
    t = ab.shape[0]
    f, dm = w.shape
    tm = _row_tile(t)

    def body(a_ref, b_ref, w_ref, xres_ref, g_ref, lg_ref, lb_ref, h_ref, y_ref, xo_ref, zh_ref, rs_ref):
        a = a_ref[...].astype(F32)
        h = (a * _sigmoid(a) * b_ref[...].astype(F32)).astype(BF16)
        h_ref[...] = h
        y = _dot(h, w_ref[...])
        _ln_store(y, xres_ref, g_ref, lg_ref, lb_ref, y_ref, xo_ref, zh_ref, rs_ref)

    vec = pl.BlockSpec((1, dm), lambda i: (0, 0))
    return pl.pallas_call(
        body, name=name, grid=(t // tm,),
        out_shape=[jax.ShapeDtypeStruct((t, f), BF16)] + _ln_out_shapes(t, dm),
        in_specs=[pl.BlockSpec((tm, f), lambda i: (i, 0)), pl.BlockSpec((tm, f), lambda i: (i, 1)),
                  pl.BlockSpec((f, dm), lambda i: (0, 0)),
                  pl.BlockSpec((tm, dm), lambda i: (i, 0)), vec, vec, vec],
        out_specs=[pl.BlockSpec((tm, f), lambda i: (i, 0))] + _ln_out_specs(tm, dm),
        compiler_params=_cp("parallel"),
    )(ab, ab, w, xres, gate, lg, lb)


class AttnCfg:
    def __init__(self, dil, heads, kv_heads, qc, kc, vc, max_dist, head0, sinks):
        self.dil, self.heads, self.kv_heads = dil, heads, kv_heads
        self.qc, self.kc, self.vc = qc, kc, vc
        self.max_dist, self.head0, self.sinks = max_dist, head0, sinks
        self.wq = heads * HEAD_DIM
        self.wk = kv_heads * HEAD_DIM
        self.wout = self.wq + 2 * self.wk


ATTN_A = AttnCfg(1, A_Q_HEADS, A_KV_HEADS, 0, A_Q_HEADS * HEAD_DIM, (A_Q_HEADS + A_KV_HEADS) * HEAD_DIM,
                 A_WINDOW - 1, 0, True)


def _attn_b_cfg(g):
    win, dil = B_GROUPS[g]
    cols = ((A_W + g * B_OUT_W, A_W + B_ALL + g * B_OUT_W, A_W + 2 * B_ALL + g * B_OUT_W) if dil == 1
            else (0, B_OUT_W, 2 * B_OUT_W))
    return AttnCfg(dil, B_HEADS_PER_GROUP, B_HEADS_PER_GROUP, *cols, win // dil,
                   A_Q_HEADS + g * B_HEADS_PER_GROUP, False)


ATTN_B = [_attn_b_cfg(g) for g in range(N_GROUPS)]


def _band(i, max_dist):
    qi = lax.broadcasted_iota(jnp.int32, (BLOCK, 2 * BLOCK), 0)
    sj = lax.broadcasted_iota(jnp.int32, (BLOCK, 2 * BLOCK), 1)
    dist = qi + BLOCK - sj
    valid = (dist >= 0) & (dist <= max_dist)
    first_key = jnp.where(i > 0, 0, BLOCK)
    valid_first = valid & (sj >= first_key)
    return dist, valid, valid_first


def _attn_geometry(cfg, n):
    tq = min(512, n)
    return tq, tq // BLOCK, n // tq


def attn_fwd(qkv, cfg, sinks, name):
    d, n, _ = qkv.shape
    tq, nsub, nqb = _attn_geometry(cfg, n)
    wq, wk = cfg.wq, cfg.wk
    grp = cfg.heads // cfg.kv_heads

    def body(sink_ref, q_ref, kc_ref, kp_ref, vc_ref, vp_ref, o_ref, l_ref, kf, vf):
        i = pl.program_id(1)
        kf[0:BLOCK, :] = kp_ref[...]
        kf[BLOCK:, :] = kc_ref[...]
        vf[0:BLOCK, :] = vp_ref[...]
        vf[BLOCK:, :] = vc_ref[...]
        dist, valid, valid_first = _band(i, cfg.max_dist)
        distf = dist.astype(F32)
        for h in range(cfg.heads):
            kv = h // grp
            bias = distf * (-(_slope(cfg.head0 + h) * d))
            hs = slice(h * HEAD_DIM, (h + 1) * HEAD_DIM)
            ks = slice(kv * HEAD_DIM, (kv + 1) * HEAD_DIM)
            for a in range(nsub):
                rows = slice(a * BLOCK, (a + 1) * BLOCK)
                win = slice(a * BLOCK, (a + 2) * BLOCK)
                s = _dot_nt(q_ref[rows, hs], kf[win, ks]) * (HEAD_DIM ** -0.5)
                s = jnp.where(valid_first if a == 0 else valid, s + bias, NEG_INF)
                m = jnp.max(s, axis=1, keepdims=True)
                if cfg.sinks:
                    m = jnp.maximum(m, sink_ref[h])
                e = jnp.exp(s - m)
                den = jnp.sum(e, axis=1, keepdims=True)
                if cfg.sinks:
                    den = den + jnp.exp(sink_ref[h] - m)
                p = (e / den).astype(BF16)
                o_ref[rows, hs] = _dot(p, vf[win, ks]).astype(BF16)
                l_ref[rows, hs] = jnp.broadcast_to(m + jnp.log(den), (BLOCK, HEAD_DIM))

    prev = lambda i: jnp.maximum(i * nsub - 1, 0)
    cur = lambda w, c: pl.BlockSpec((None, tq, w), lambda r, i: (r, i, c // w))
    prv = lambda w, c: pl.BlockSpec((None, BLOCK, w), lambda r, i: (r, prev(i), c // w))
    out = pl.BlockSpec((None, tq, wq), lambda r, i: (r, i, 0))
    return pl.pallas_call(
        body, name=name, grid=(d, nqb),
        out_shape=[jax.ShapeDtypeStruct((d, n, wq), BF16), jax.ShapeDtypeStruct((d, n, wq), F32)],
        in_specs=[pl.BlockSpec(memory_space=pltpu.SMEM),
                  cur(wq, cfg.qc), cur(wk, cfg.kc), prv(wk, cfg.kc), cur(wk, cfg.vc), prv(wk, cfg.vc)],
        out_specs=[out, out],
        scratch_shapes=[pltpu.VMEM((tq + BLOCK, wk), BF16), pltpu.VMEM((tq + BLOCK, wk), BF16)],
        compiler_params=_cp("parallel", "parallel"),
    )(sinks, qkv, qkv, qkv, qkv, qkv)


def mix_merge(ya, o_g, l_g, proj, w_a, w_b, name):
    t = ya.shape[0]
    dm = w_a.shape[1]
    tm = _row_tile(t)
    gcol = GATE_COL // dm
    dils = [o.shape[0] for o in o_g]

    def body(ya_ref, o0, o1, o2, l0, l1, l2, ga_ref, gb_ref, wa_ref, wb_ref, yb_ref, mg_ref, *scrs):
        ls = [_unfold_from(l, scrs, d) for l, d in zip((l0, l1, l2), dils)]
        m = jnp.maximum(jnp.maximum(ls[0], ls[1]), ls[2])
        es = [jnp.exp(l - m) for l in ls]
        inv = 1.0 / (es[0] + es[1] + es[2])
        yb = sum(_unfold_from(o, scrs, d) * (e * inv) for o, e, d in zip((o0, o1, o2), es, dils)).astype(BF16)
        yb_ref[...] = yb
        pa = _dot(ya_ref[...], wa_ref[...])
        pb = _dot(yb, wb_ref[...])
        mg = _sigmoid(ga_ref[...].astype(F32)) * pa + _sigmoid(gb_ref[...].astype(F32)) * pb
        mg_ref[...] = mg.astype(BF16)

    wide = lambda w: pl.BlockSpec((tm, w), lambda i: (i, 0))
    folded = [_folded_spec(d, tm, B_OUT_W) for d in dils]
    return pl.pallas_call(
        body, name=name, grid=(t // tm,),
        out_shape=[jax.ShapeDtypeStruct((t, B_OUT_W), BF16), jax.ShapeDtypeStruct((t, dm), BF16)],
        in_specs=[wide(ya.shape[1])] + folded + folded
                 + [pl.BlockSpec((tm, dm), lambda i: (i, gcol)), pl.BlockSpec((tm, dm), lambda i: (i, gcol + 1)),
                    pl.BlockSpec(w_a.shape, lambda i: (0, 0)), pl.BlockSpec(w_b.shape, lambda i: (0, 0))],
        out_specs=[wide(B_OUT_W), wide(dm)],
        scratch_shapes=_fold_scratch(tm, B_OUT_W),
        compiler_params=_cp("parallel"),
    )(ya, *o_g, *l_g, proj, proj, w_a, w_b)


def loss_head(y, target):
    t, dm = y.shape
    tm = _row_tile(t)

    def body(y_ref, t_ref, dy_ref, loss_ref):
        @pl.when(pl.program_id(0) == 0)
        def _():
            loss_ref[...] = jnp.zeros_like(loss_ref)
        err = y_ref[...] - t_ref[...]
        dy_ref[...] = err * (1.0 / dm)
        per_row = jnp.sum(err * err, axis=1, keepdims=True) * (1.0 / dm)
        loss_ref[...] += 0.5 * jnp.sum(per_row, axis=0, keepdims=True)

    row = pl.BlockSpec((tm, dm), lambda i: (i, 0))
    return pl.pallas_call(
        body, name="loss_head", grid=(t // tm,),
        out_shape=[jax.ShapeDtypeStruct((t, dm), F32), jax.ShapeDtypeStruct((8, LANES), F32)],
        in_specs=[row, row],
        out_specs=[row, pl.BlockSpec((8, LANES), lambda i: (0, 0))],
        compiler_params=_cp("arbitrary"),
    )(y, target)


def _fold_rows(v):
    tm, c = v.shape
    return jnp.sum(v.reshape(tm // 8, 8, c), axis=0)


def _finish_sums(refs, nsteps):
    @pl.when(pl.program_id(0) == nsteps - 1)
    def _():
        for r in refs:
            r[...] = jnp.broadcast_to(jnp.sum(r[...], axis=0, keepdims=True), r.shape)


def ln_bwd(dxo, zhat, rstd, ysub, lg, gate, name):
    t, dm = dxo.shape
    tm = _row_tile(t)

    def body(dxo_ref, zh_ref, rs_ref, y_ref, lg_ref, g_ref, dz_ref, dy_ref, sg_ref, sb_ref, sgate_ref):
        @pl.when(pl.program_id(0) == 0)
        def _():
            for r in (sg_ref, sb_ref, sgate_ref):
                r[...] = jnp.zeros_like(r)
        dxo_v = dxo_ref[...]
        zh = zh_ref[...]
        dxh = dxo_v * lg_ref[...]
        m1 = jnp.mean(dxh, axis=1, keepdims=True)
        m2 = jnp.mean(dxh * zh, axis=1, keepdims=True)
        dz = rs_ref[:, 0:1] * (dxh - m1 - zh * m2)
        dz_ref[...] = dz
        dy_ref[...] = (g_ref[...] * dz).astype(BF16)
        sg_ref[...] += _fold_rows(dxo_v * zh)
        sb_ref[...] += _fold_rows(dxo_v)
        sgate_ref[...] += _fold_rows(dz * y_ref[...].astype(F32))
        _finish_sums((sg_ref, sb_ref, sgate_ref), t // tm)

    row = pl.BlockSpec((tm, dm), lambda i: (i, 0))
    vec = pl.BlockSpec((1, dm), lambda i: (0, 0))
    acc = pl.BlockSpec((8, dm), lambda i: (0, 0))
    return pl.pallas_call(
        body, name=name, grid=(t // tm,),
        out_shape=[jax.ShapeDtypeStruct((t, dm), F32), jax.ShapeDtypeStruct((t, dm), BF16)]
                  + [jax.ShapeDtypeStruct((8, dm), F32)] * 3,
        in_specs=[row, row, pl.BlockSpec((tm, LANES), lambda i: (i, 0)), row, vec, vec],
        out_specs=[row, row, acc, acc, acc],
        compiler_params=_cp("arbitrary"),
    )(dxo, zhat, rstd, ysub, lg, gate)


def _mod_bwd_store(du, dz_ref, x_ref, s_ref, dx_ref, ss_ref, ssh_ref, nsteps):
    @pl.when(pl.program_id(0) == 0)
    def _():
        ss_ref[...] = jnp.zeros_like(ss_ref)
        ssh_ref[...] = jnp.zeros_like(ssh_ref)
    dx_ref[...] = DN_ALPHA * dz_ref[...] + du * (1.0 + s_ref[...])
    ss_ref[...] += _fold_rows(du * x_ref[...])
    ssh_ref[...] += _fold_rows(du)
    _finish_sums((ss_ref, ssh_ref), nsteps)


def dgrad_ffn(g, wt, dz, xin, s, name):
    t, dm = dz.shape
    k = g.shape[1]
    tm = _row_tile(t)

    def body(g_ref, w_ref, dz_ref, x_ref, s_ref, dx_ref, ss_ref, ssh_ref):
        du = _dot(g_ref[...], w_ref[...])
        _mod_bwd_store(du, dz_ref, x_ref, s_ref, dx_ref, ss_ref, ssh_ref, t // tm)

    row = pl.BlockSpec((tm, dm), lambda i: (i, 0))
    acc = pl.BlockSpec((8, dm), lambda i: (0, 0))
    return pl.pallas_call(
        body, name=name, grid=(t // tm,),
        out_shape=[jax.ShapeDtypeStruct((t, dm), F32)] + [jax.ShapeDtypeStruct((8, dm), F32)] * 2,
        in_specs=[pl.BlockSpec((tm, k), lambda i: (i, 0)), pl.BlockSpec((k, dm), lambda i: (0, 0)),
                  row, row, pl.BlockSpec((1, dm), lambda i: (0, 0))],
        out_specs=[row, acc, acc],
        compiler_params=_cp("arbitrary"),
    )(g, wt, dz, xin, s)


def dgrad_in(d_a, d_b, dgab, wt, dz, xin, s, name):
    t, dm = dz.shape
    tm = _row_tile(t)
    dils = [a.shape[0] for a in d_b]

    def body(da_ref, b0, b1, b2, dg_ref, w_ref, dz_ref, x_ref, s_ref, dx_ref, ss_ref, ssh_ref, *scrs):
        du = _dot(da_ref[0], w_ref[0:A_W, :])
        for g, (b_ref, d) in enumerate(zip((b0, b1, b2), dils)):
            v = b_ref[...].reshape(tm, B_GW)
            part = None
            for p in range(3):
                r0 = A_W + p * B_ALL + g * B_OUT_W
                term = _dot(v[:, p * B_OUT_W:(p + 1) * B_OUT_W], w_ref[r0:r0 + B_OUT_W, :])
                part = term if part is None else part + term
            if d == 1:
                du = du + part
            else:
                n = tm // d
                du = du + _unfold_rows(lambda r, cols: part[r * n:(r + 1) * n, cols], scrs, d, n, dm)
        for j in range(2):
            du = du + _dot(dg_ref[:, j * dm:(j + 1) * dm], w_ref[GATE_COL + j * dm:GATE_COL + (j + 1) * dm, :])
        _mod_bwd_store(du, dz_ref, x_ref, s_ref, dx_ref, ss_ref, ssh_ref, t // tm)

    row = pl.BlockSpec((tm, dm), lambda i: (i, 0))
    acc = pl.BlockSpec((8, dm), lambda i: (0, 0))
    return pl.pallas_call(
        body, name=name, grid=(t // tm,),
        out_shape=[jax.ShapeDtypeStruct((t, dm), F32)] + [jax.ShapeDtypeStruct((8, dm), F32)] * 2,
        in_specs=[_folded_spec(1, tm, A_W)] + [_folded_spec(d, tm, B_GW) for d in dils]
                 + [pl.BlockSpec((tm, 2 * dm), lambda i: (i, 0)), pl.BlockSpec(wt.shape, lambda i: (0, 0)),
                    row, row, pl.BlockSpec((1, dm), lambda i: (0, 0))],
        out_specs=[row, acc, acc],
        scratch_shapes=_fold_scratch(tm, dm),
        compiler_params=_cp("arbitrary"),
    )(d_a, *d_b, dgab, wt, dz, xin, s)


def wgrad(a, b, buf, l, tn, nj, b0, o0, om, name):
    t, k = a.shape
    tt = _row_tile(t)
    nsteps = t // tt

    def body(a_ref, b_ref, buf_ref, o_ref, acc):
        s = pl.program_id(1)

        @pl.when(s == 0)
        def _():
            acc[...] = jnp.zeros_like(acc)
        acc[...] += _dot_tn(a_ref[...], b_ref[...])

        @pl.when(s == nsteps - 1)
        def _():
            o_ref[...] = acc[...].astype(BF16)

    return pl.pallas_call(
        body, name=name, grid=(nj, nsteps),
        out_shape=jax.ShapeDtypeStruct(buf.shape, buf.dtype),
        in_specs=[pl.BlockSpec((tt, k), lambda j, s: (s, 0)),
                  pl.BlockSpec((tt, tn), lambda j, s: (s, b0 + j)),
                  pl.BlockSpec(memory_space=pl.ANY)],
        out_specs=pl.BlockSpec((None, k, tn), lambda j, s: (l, 0, o0 + om * j)),
        scratch_shapes=[pltpu.VMEM((k, tn), F32)],
        input_output_aliases={2: 0},
        compiler_params=_cp("parallel", "arbitrary"),
    )(a, b, buf)


def dswiglu(dy, wdt, ab, name):
    t, dm = dy.shape
    f = wdt.shape[1]
    tm = _row_tile(t)

    def body(dy_ref, w_ref, a_ref, b_ref, o_ref):
        dh = _dot(dy_ref[...], w_ref[...])
        a = a_ref[...].astype(F32)
        sg = _sigmoid(a)
        o_ref[:, 0:f] = (dh * b_ref[...].astype(F32) * (sg * (1.0 + a * (1.0 - sg)))).astype(BF16)
        o_ref[:, f:] = (dh * (a * sg)).astype(BF16)

    return pl.pallas_call(
        body, name=name, grid=(t // tm,),
        out_shape=jax.ShapeDtypeStruct((t, 2 * f), BF16),
        in_specs=[pl.BlockSpec((tm, dm), lambda i: (i, 0)), pl.BlockSpec((dm, f), lambda i: (0, 0)),
                  pl.BlockSpec((tm, f), lambda i: (i, 0)), pl.BlockSpec((tm, f), lambda i: (i, 1))],
        out_specs=pl.BlockSpec((tm, 2 * f), lambda i: (i, 0)),
        compiler_params=_cp("parallel"),
    )(dy, wdt, ab, ab)


def dmerge(do, wot, ya, yb, w_a, w_b, wat, wbt, proj, name):
    t, dm = do.shape
    tm = _row_tile(t)
    gcol = GATE_COL // dm

    def body(do_ref, wot_ref, ya_ref, yb_ref, wa_ref, wb_ref, wat_ref, wbt_ref, g_ref,
             dp_ref, dya_ref, dyb_ref, dg_ref, dm_scr):
        j = pl.program_id(1)

        @pl.when(j == 0)
        def _():
            dm_scr[...] = _dot(do_ref[...], wot_ref[...])

        def branch(y_ref, w_ref, wt_ref, dy_ref):
            p = _dot(y_ref[...], w_ref[...])
            sg = _sigmoid(g_ref[...].astype(F32))
            dmg = dm_scr[...]
            dp = (dmg * sg).astype(BF16)
            dp_ref[...] = dp
            dg_ref[...] = (dmg * p * (sg * (1.0 - sg))).astype(BF16)
            dy_ref[...] = _dot(dp, wt_ref[...]).astype(dy_ref.dtype)

        pl.when(j == 0)(lambda: branch(ya_ref, wa_ref, wat_ref, dya_ref))
        pl.when(j == 1)(lambda: branch(yb_ref, wb_ref, wbt_ref, dyb_ref))

    full = lambda arr: pl.BlockSpec(arr.shape, lambda i, j: (0, 0))
    rowc = lambda w: pl.BlockSpec((tm, w), lambda i, j: (i, 0))
    return pl.pallas_call(
        body, name=name, grid=(t // tm, 2),
        out_shape=[jax.ShapeDtypeStruct((t, 2 * dm), BF16), jax.ShapeDtypeStruct((t, ya.shape[1]), BF16),
                   jax.ShapeDtypeStruct((t, yb.shape[1]), F32), jax.ShapeDtypeStruct((t, 2 * dm), BF16)],
        in_specs=[rowc(dm), full(wot), rowc(ya.shape[1]), rowc(yb.shape[1]), full(w_a), full(w_b),
                  full(wat), full(wbt), pl.BlockSpec((tm, dm), lambda i, j: (i, gcol + j))],
        out_specs=[pl.BlockSpec((tm, dm), lambda i, j: (i, j)), rowc(ya.shape[1]), rowc(yb.shape[1]),
                   pl.BlockSpec((tm, dm), lambda i, j: (i, j))],
        scratch_shapes=[pltpu.VMEM((tm, dm), F32)],
        compiler_params=_cp("parallel", "arbitrary"),
    )(do, wot, ya, yb, w_a, w_b, wat, wbt, proj)


def mix_bwd(dyb, o_g, l_g, name):
    t, w = dyb.shape
    tm = _row_tile(t)
    nh = w // HEAD_DIM
    dils = [o.shape[0] for o in o_g]

    def body(dyb_ref, o0, o1, o2, l0, l1, l2, do0, do1, do2, dl0, dl1, dl2, *scr):
        ls = [_unfold_from(l, scr, d) for l, d in zip((l0, l1, l2), dils)]
        m = jnp.maximum(jnp.maximum(ls[0], ls[1]), ls[2])
        es = [jnp.exp(l - m) for l in ls]
        inv = 1.0 / (es[0] + es[1] + es[2])
        wts = [e * inv for e in es]
        dyb_v = dyb_ref[...]
        dws = []
        for o_ref, do_ref, wt, d in zip((o0, o1, o2), (do0, do1, do2), wts, dils):
            prod = dyb_v * _unfold_from(o_ref, scr, d)
            _fold_to(do_ref, dyb_v * wt, scr, d)
            for h in range(nh):
                hs = slice(h * HEAD_DIM, (h + 1) * HEAD_DIM)
                dws.append(jnp.broadcast_to(jnp.sum(prod[:, hs], axis=1, keepdims=True), (tm, HEAD_DIM)))
        for g, (dl_ref, d) in enumerate(zip((dl0, dl1, dl2), dils)):
            cols = []
            for h in range(nh):
                hs = slice(h * HEAD_DIM, (h + 1) * HEAD_DIM)
                mean = sum(wts[g2][:, hs] * dws[g2 * nh + h] for g2 in range(N_GROUPS))
                cols.append(wts[g][:, hs] * (dws[g * nh + h] - mean))
            _fold_to(dl_ref, jnp.concatenate(cols, axis=1), scr, d)

    folded = [_folded_spec(d, tm, w) for d in dils]
    return pl.pallas_call(
        body, name=name, grid=(t // tm,),
        out_shape=[jax.ShapeDtypeStruct(o.shape, BF16) for o in o_g]
                  + [jax.ShapeDtypeStruct(o.shape, F32) for o in o_g],
        in_specs=[pl.BlockSpec((tm, w), lambda i: (i, 0))] + folded + folded,
        out_specs=folded + folded,
        scratch_shapes=_fold_scratch(tm, w),
        compiler_params=_cp("parallel"),
    )(dyb, *o_g, *l_g)


def attn_bwd(qkv, o, lse, do, dlse, cfg, sinks, name):
    d, n, _ = qkv.shape
    tq, nsub, nqb = _attn_geometry(cfg, n)
    wq, wk, wout = cfg.wq, cfg.wk, cfg.wout
    grp = cfg.heads // cfg.kv_heads
    has_dl = dlse is not None
    scale = HEAD_DIM ** -0.5

    def body(*refs):
        sink_ref, q_ref, qn_ref, kc_ref, kp_ref, vc_ref, vp_ref = refs[:7]
        o_ref, on_ref, do_ref, don_ref, l_ref, ln_ref = refs[7:13]
        rest = refs[13:]
        dl_ref = dln_ref = None
        if has_dl:
            dl_ref, dln_ref = rest[:2]
            rest = rest[2:]
        out_ref = rest[0]
        rest = rest[1:]
        if cfg.sinks:
            dsink_ref = rest[0]
            rest = rest[1:]
        kf, vf, dk_acc, dv_acc = rest
        r, i = pl.program_id(0), pl.program_id(1)
        kf[0:BLOCK, :] = kp_ref[...]
        kf[BLOCK:, :] = kc_ref[...]
        vf[0:BLOCK, :] = vp_ref[...]
        vf[BLOCK:, :] = vc_ref[...]
        dist, valid, valid_first = _band(i, cfg.max_dist)
        distf = dist.astype(F32)
        next_dist = jnp.where(i < nqb - 1, cfg.max_dist, -1)
        valid_next = (dist[:, 0:BLOCK] >= 0) & (dist[:, 0:BLOCK] <= next_dist)
        if cfg.sinks:
            @pl.when((r == 0) & (i == 0))
            def _():
                dsink_ref[...] = jnp.zeros_like(dsink_ref)

        def stats(o_r, do_r, l_r, dl_r, rows, hs):
            do_v = do_r[rows, hs]
            delta = jnp.sum(do_v.astype(F32) * o_r[rows, hs].astype(F32), axis=1, keepdims=True)
            lse_v = jnp.max(l_r[rows, hs], axis=1, keepdims=True)
            shift = -delta
            if has_dl:
                shift = shift + jnp.max(dl_r[rows, hs], axis=1, keepdims=True)
            return do_v, delta, lse_v, shift

        for kv in range(cfg.kv_heads):
            ks = slice(kv * HEAD_DIM, (kv + 1) * HEAD_DIM)
            dk_acc[...] = jnp.zeros_like(dk_acc)
            dv_acc[...] = jnp.zeros_like(dv_acc)
            for g in range(grp):
                h = kv * grp + g
                hs = slice(h * HEAD_DIM, (h + 1) * HEAD_DIM)
                bias = distf * (-(_slope(cfg.head0 + h) * d))
                for a in range(nsub):
                    rows = slice(a * BLOCK, (a + 1) * BLOCK)
                    win = slice(a * BLOCK, (a + 2) * BLOCK)
                    q = q_ref[rows, hs]
                    k = kf[win, ks]
                    do_v, delta, lse_v, shift = stats(o_ref, do_ref, l_ref, dl_ref, rows, hs)
                    s = _dot_nt(q, k) * scale
                    s = jnp.where(valid_first if a == 0 else valid, s + bias, NEG_INF)
                    p = jnp.exp(s - lse_v)
                    ds = p * (_dot_nt(do_v, vf[win, ks]) + shift)
                    dsb, pb = ds.astype(BF16), p.astype(BF16)
                    out_ref[rows, hs] = (_dot(dsb, k) * scale).astype(BF16)
                    if a == 0:
                        dk_acc[0:BLOCK, :] += _dot_tn(dsb[:, BLOCK:], q) * scale
                        dv_acc[0:BLOCK, :] += _dot_tn(pb[:, BLOCK:], do_v)
                    else:
                        krows = slice((a - 1) * BLOCK, (a + 1) * BLOCK)
                        dk_acc[krows, :] += _dot_tn(dsb, q) * scale
                        dv_acc[krows, :] += _dot_tn(pb, do_v)
                    if cfg.sinks:
                        psink = jnp.exp(sink_ref[h] - lse_v)
                        tot = jnp.sum(psink * (-delta), axis=0, keepdims=True)
                        dsink_ref[h:h + 1, :] += jnp.broadcast_to(tot, (1, LANES))
                rows = slice(0, BLOCK)
                last = slice((nsub - 1) * BLOCK, nsub * BLOCK)
                klast = slice(nsub * BLOCK, (nsub + 1) * BLOCK)
                q = qn_ref[rows, hs]
                k = kf[klast, ks]
                do_v, delta, lse_v, shift = stats(on_ref, don_ref, ln_ref, dln_ref, rows, hs)
                s = _dot_nt(q, k) * scale
                s = jnp.where(valid_next, s + bias[:, 0:BLOCK], NEG_INF)
                p = jnp.exp(s - lse_v)
                ds = p * (_dot_nt(do_v, vf[klast, ks]) + shift)
                dk_acc[last, :] += _dot_tn(ds.astype(BF16), q) * scale
                dv_acc[last, :] += _dot_tn(p.astype(BF16), do_v)
            out_ref[:, wq + kv * HEAD_DIM: wq + (kv + 1) * HEAD_DIM] = dk_acc[...].astype(BF16)
            out_ref[:, wq + wk + kv * HEAD_DIM: wq + wk + (kv + 1) * HEAD_DIM] = dv_acc[...].astype(BF16)

    prev = lambda i: jnp.maximum(i * nsub - 1, 0)
    nxt = lambda i: jnp.minimum((i + 1) * nsub, n // BLOCK - 1)
    cur = lambda w, c: pl.BlockSpec((None, tq, w), lambda r, i: (r, i, c // w))
    prv = lambda w, c: pl.BlockSpec((None, BLOCK, w), lambda r, i: (r, prev(i), c // w))
    o_cur = pl.BlockSpec((None, tq, wq), lambda r, i: (r, i, 0))
    o_nxt = pl.BlockSpec((None, BLOCK, wq), lambda r, i: (r, nxt(i), 0))
    in_specs = [pl.BlockSpec(memory_space=pltpu.SMEM),
                cur(wq, cfg.qc), pl.BlockSpec((None, BLOCK, wq), lambda r, i: (r, nxt(i), cfg.qc // wq)),
                cur(wk, cfg.kc), prv(wk, cfg.kc), cur(wk, cfg.vc), prv(wk, cfg.vc),
                o_cur, o_nxt, o_cur, o_nxt, o_cur, o_nxt]
    args = [sinks, qkv, qkv, qkv, qkv, qkv, qkv, o, o, do, do, lse, lse]
    if has_dl:
        in_specs += [o_cur, o_nxt]
        args += [dlse, dlse]
    out_shape = [jax.ShapeDtypeStruct((d, n, wout), BF16)]
    out_specs = [pl.BlockSpec((None, tq, wout), lambda r, i: (r, i, 0))]
    if cfg.sinks:
        out_shape.append(jax.ShapeDtypeStruct((8, LANES), F32))
        out_specs.append(pl.BlockSpec((8, LANES), lambda r, i: (0, 0)))
    return pl.pallas_call(
        body, name=name, grid=(d, nqb), out_shape=out_shape, in_specs=in_specs, out_specs=out_specs,
        scratch_shapes=[pltpu.VMEM((tq + BLOCK, wk), BF16), pltpu.VMEM((tq + BLOCK, wk), BF16),
                        pltpu.VMEM((tq, HEAD_DIM), F32), pltpu.VMEM((tq, HEAD_DIM), F32)],
        compiler_params=_cp("arbitrary", "arbitrary"),
    )(*args)


def _adamw(g, w, m, v):
    m = ADAM_B1 * m + (1.0 - ADAM_B1) * g
    v = ADAM_B2 * v + (1.0 - ADAM_B2) * (g * g)
    m_hat = m / (1.0 - ADAM_B1 ** ADAM_STEP)
    v_hat = v / (1.0 - ADAM_B2 ** ADAM_STEP)
    delta = -ADAM_LR * (m_hat / (jnp.sqrt(v_hat) + ADAM_EPS) + ADAM_WD * w)
    return delta, m, v


def adam_reduce(parts, w, m, v, name):
    r, c = w.shape
    tr = next(cand for cand in (256, 128, 64, 32, 16, 8) if r % cand == 0) if r > 256 else r

    def body(p_ref, w_ref, m_ref, v_ref, g_ref, d_ref, mo_ref, vo_ref):
        g = p_ref[0].astype(F32)
        for j in range(1, N_DEV):
            g = g + p_ref[j].astype(F32)
        g_ref[...] = g
        d_ref[...], mo_ref[...], vo_ref[...] = _adamw(g, w_ref[...], m_ref[...], v_ref[...])

    row = pl.BlockSpec((tr, c), lambda i: (i, 0))
    return pl.pallas_call(
        body, name=name, grid=(r // tr,),
        out_shape=[jax.ShapeDtypeStruct((r, c), F32)] * 4,
        in_specs=[pl.BlockSpec((N_DEV, tr, c), lambda i: (0, i, 0)), row, row, row],
        out_specs=[row] * 4,
        compiler_params=_cp("parallel"),
    )(parts, w, m, v)


def adam_w_ada(sct, dm_loc, w, m, v):
    nl, dm, wc = w.shape
    tr = 512

    def body(s_ref, d_ref, w_ref, m_ref, v_ref, g_ref, dl_ref, mo_ref, vo_ref):
        g = jnp.dot(s_ref[...], d_ref[...], preferred_element_type=F32, precision=lax.Precision.HIGHEST)
        g_ref[...] = g
        dl_ref[...], mo_ref[...], vo_ref[...] = _adamw(g, w_ref[...], m_ref[...], v_ref[...])

    blk = pl.BlockSpec((None, tr, wc), lambda l, i: (l, i, 0))
    return pl.pallas_call(
        body, name="adam_w_ada", grid=(nl, dm // tr),
        out_shape=[jax.ShapeDtypeStruct(w.shape, F32)] * 4,
        in_specs=[pl.BlockSpec((tr, LANES), lambda l, i: (i, 0)),
                  pl.BlockSpec((None, LANES, wc), lambda l, i: (l, 0, 0)), blk, blk, blk],
        out_specs=[blk] * 4,
        compiler_params=_cp("parallel", "parallel"),
    )(sct, dm_loc, w, m, v)


TRANSPOSED = ("w_gate", "w_up")


def _pieces(dm):
    ncol = lambda n: n // N_DEV
    return [Piece("w_in", "w_in", 1, 0, ncol(GATE_COL + 2 * dm)),
            Piece("w_a", "w_a", 1, 0, ncol(dm)),
            Piece("w_b", "w_b", 1, 0, ncol(dm)),
            Piece("w_o", "w_o", 0, 0, ncol(dm)),
            Piece("w_gate", "w_ffn_t", 0, 0, ncol(D_FF)),
            Piece("w_up", "w_ffn_t", 0, D_FF, ncol(D_FF)),
            Piece("w_down", "w_down", 0, 0, ncol(D_FF))]


def kernel(x, c, w_ada, b_ada, w_in, sinks, w_a, w_b, w_o, ln1_g, ln1_b, w_gate, w_up, w_down, ln2_g, ln2_b, loss_target, m_w_ada, m_b_ada, m_w_in, m_sinks, m_w_a, m_w_b, m_w_o, m_ln1_g, m_ln1_b, m_w_gate, m_w_up, m_w_down, m_ln2_g, m_ln2_b, v_w_ada, v_b_ada, v_w_in, v_sinks, v_w_a, v_w_b, v_w_o, v_ln1_g, v_ln1_b, v_w_gate, v_w_up, v_w_down, v_ln2_g, v_ln2_b):
    given = dict(locals())
    nl = w_in.shape[0]
    t, dm = x.shape[1], x.shape[2]
    me = 4 * lax.axis_index("x") + 2 * lax.axis_index("y") + lax.axis_index("c")
    x0 = x.reshape(t, dm)
    target = loss_target.reshape(t, dm)

    pieces = _pieces(dm)
    local = lambda nm, pre="": (given[pre + nm].transpose(0, 2, 1) if nm in TRANSPOSED else given[pre + nm])
    shards = [local(pc.name).astype(BF16) for pc in pieces]
    buf_shapes = {"w_in": (nl, dm, GATE_COL + 2 * dm), "w_a": (nl, A_Q_HEADS * HEAD_DIM, dm),
                  "w_b": (nl, B_OUT_W, dm), "w_o": (nl, dm, dm), "w_ffn_t": (nl, 2 * D_FF, dm),
                  "w_down": (nl, D_FF, dm)}
    full = dict(zip(buf_shapes, gather_weights(shards, pieces, buf_shapes, "gather_weights")))
    tr = lambda a: a.transpose(0, 2, 1)
    w_in_t, w_a_t, w_b_t, w_o_t, w_d_t = (tr(full[k]) for k in ("w_in", "w_a", "w_b", "w_o", "w_down"))
    w_ffn = tr(full["w_ffn_t"])

    wc = w_ada.shape[2]
    c_all = all_gather_small(jnp.broadcast_to(c, (8, dm)), "gather_c")[:, 0, :]
    b_loc = lax.dynamic_slice_in_dim(b_ada, me * wc, wc, axis=1).reshape(nl, 1, wc)
    mp, sc_all = mod_partial(c_all, w_ada, b_loc)
    mp_all = all_gather_small(mp.reshape(nl * N_DEV, wc), "gather_mod").reshape(N_DEV, nl, N_DEV, wc)
    mod = lax.dynamic_index_in_dim(mp_all, me, axis=2, keepdims=False)
    mod = mod.transpose(1, 0, 2).reshape(nl, 6, 1, dm)

    vec = lambda a, l: a[l].reshape(1, dm)

    saved = []
    xl = x0
    for l in range(nl):
        sh1, s1, g1, sh2, s2, g2 = [mod[l, j] for j in range(6)]
        u1, u1_f4, u1_f16, proj, qkv_f4, qkv_f16 = in_proj(xl, s1, sh1, full["w_in"][l], "in_proj")
        proj3 = proj.reshape(1, t, proj.shape[1])
        qkv_b = [proj3, qkv_f4, qkv_f16]
        ya, lse_a = attn_fwd(proj3, ATTN_A, sinks[l], "attn_a_fwd")
        o_g, l_g = [], []
        for g, cfg in enumerate(ATTN_B):
            o, ls = attn_fwd(qkv_b[g], cfg, sinks[l], "attn_b%d_fwd" % g)
            o_g.append(o)
            l_g.append(ls)
        yb, merged = mix_merge(ya[0], o_g, l_g, proj, full["w_a"][l], full["w_b"][l], "mix_merge")
        y1, x1, zh1, rs1 = proj_ln(merged, full["w_o"][l], xl, g1, vec(ln1_g, l), vec(ln1_b, l), "out_proj_ln")
        u2, ab = modmm(x1, s2, sh2, w_ffn[l], "ffn_up")
        h, y2, x2, zh2, rs2 = swiglu_proj_ln(ab, full["w_down"][l], x1, g2, vec(ln2_g, l), vec(ln2_b, l),
                                             "ffn_down_ln")
        saved.append(dict(xin=xl, u1=[u1, u1_f4.reshape(t, dm), u1_f16.reshape(t, dm)], proj=proj, qkv_b=qkv_b,
                          ya=ya, lse_a=lse_a, o_g=o_g, l_g=l_g, yb=yb, merged=merged,
                          y1=y1, x1=x1, zh1=zh1, rs1=rs1, u2=u2, ab=ab, h=h, y2=y2, zh2=zh2, rs2=rs2))
        xl = x2

    dx, loss_part = loss_head(xl, target)

    gbuf = {nm: lax.empty(shp, BF16) for nm, shp in buf_shapes.items()}
    small = {k: [None] * nl for k in ("dmod", "ln1_g", "ln1_b", "ln2_g", "ln2_b", "sinks")}
    for l in reversed(range(nl)):
        sv = saved[l]
        sh1, s1, g1, sh2, s2, g2 = [mod[l, j] for j in range(6)]
        dz2, dy2, sg, sb, sgate2 = ln_bwd(dx, sv["zh2"], sv["rs2"], sv["y2"], vec(ln2_g, l), g2, "ln_bwd")
        small["ln2_g"][l], small["ln2_b"][l] = sg[0], sb[0]
        gbuf["w_down"] = wgrad(sv["h"], dy2, gbuf["w_down"], l, 512, dm // 512, 0, 0, 1, "wgrad_down")
        dab = dswiglu(dy2, w_d_t[l], sv["ab"], "dswiglu")
        gbuf["w_ffn_t"] = wgrad(dab, sv["u2"], gbuf["w_ffn_t"], l, 512, dm // 512, 0, 0, 1, "wgrad_ffn_up")
        dx1, ss2, ssh2 = dgrad_ffn(dab, full["w_ffn_t"][l], dz2, sv["x1"], s2, "dgrad_ffn")
        dz1, do1, sg, sb, sgate1 = ln_bwd(dx1, sv["zh1"], sv["rs1"], sv["y1"], vec(ln1_g, l), g1, "ln_bwd")
        small["ln1_g"][l], small["ln1_b"][l] = sg[0], sb[0]
        gbuf["w_o"] = wgrad(sv["merged"], do1, gbuf["w_o"], l, 512, dm // 512, 0, 0, 1, "wgrad_o")
        dpab, dya, dyb, dgab = dmerge(do1, w_o_t[l], sv["ya"][0], sv["yb"], full["w_a"][l], full["w_b"][l],
                                      w_a_t[l], w_b_t[l], sv["proj"], "dmerge")
        gbuf["w_a"] = wgrad(sv["ya"][0], dpab, gbuf["w_a"], l, 512, dm // 512, 0, 0, 1, "wgrad_a")
        gbuf["w_b"] = wgrad(sv["yb"], dpab, gbuf["w_b"], l, 512, dm // 512, dm // 512, 0, 1, "wgrad_b")
        mixed = mix_bwd(dyb, sv["o_g"], sv["l_g"], "mix_bwd")
        do_g, dl_g = mixed[:N_GROUPS], mixed[N_GROUPS:]
        d_a, dsink = attn_bwd(sv["qkv_b"][0], sv["ya"], sv["lse_a"], dya.reshape(1, t, -1), None, ATTN_A,
                              sinks[l], "attn_a_bwd")
        small["sinks"][l] = dsink[:, 0]
        d_b = [attn_bwd(sv["qkv_b"][g], sv["o_g"][g], sv["l_g"][g], do_g[g], dl_g[g], cfg, sinks[l],
                        "attn_b%d_bwd" % g)[0] for g, cfg in enumerate(ATTN_B)]
        dx, ss1, ssh1 = dgrad_in(d_a, d_b, dgab, w_in_t[l], dz1, sv["xin"], s1, "dgrad_in")
        gw = wgrad(sv["u1"][0], d_a.reshape(t, A_W), gbuf["w_in"], l, A_W, 1, 0, 0, 1, "wgrad_in_a")
        for g in range(N_GROUPS):
            gw = wgrad(sv["u1"][g], d_b[g].reshape(t, B_GW), gw, l, B_OUT_W, 3, 0, A_W // B_OUT_W + g, N_GROUPS,
                       "wgrad_in_b%d" % g)
        gbuf["w_in"] = wgrad(sv["u1"][0], dgab, gw, l, 512, 2 * dm // 512, 0, GATE_COL // 512, 1, "wgrad_in_gate")
        small["dmod"][l] = jnp.stack([ssh1[0], ss1[0], sgate1[0], ssh2[0], ss2[0], sgate2[0]])
    grad_x = dx.reshape(x.shape)

    shard_shapes = [s.shape for s in shards]
    recv = scatter_grads(gbuf, pieces, shard_shapes, "scatter_grads")
    big_out = {}
    for pc, parts, shp in zip(pieces, recv, shard_shapes):
        flat = lambda a: a.reshape(shp[0] * shp[1], shp[2])
        outs = adam_reduce(parts.reshape(N_DEV, shp[0] * shp[1], shp[2]), flat(local(pc.name)),
                           flat(local(pc.name, "m_")), flat(local(pc.name, "v_")), "adam_" + pc.name)
        outs = [o.reshape(shp) for o in outs]
        big_out[pc.name] = [o.transpose(0, 2, 1) for o in outs] if pc.name in TRANSPOSED else outs

    rows = jnp.concatenate(
        [jnp.stack(small["dmod"]).reshape(nl * 6, dm)]
        + [jnp.stack(small[k]) for k in ("ln1_g", "ln1_b", "ln2_g", "ln2_b")]
        + [jnp.pad(jnp.stack(small["sinks"]).reshape(1, -1), ((0, 0), (0, dm - nl * A_Q_HEADS))),
           jnp.broadcast_to(loss_part[0:1, 0:1], (1, dm))])
    n_rows = rows.shape[0]
    rows = jnp.pad(rows, ((0, -n_rows % 8), (0, 0)))
    rows_all = all_gather_small(rows, "gather_small_grads")

    def pack_small(pre):
        parts = [given[pre + "b_ada"].reshape(nl * 6, dm)]
        parts += [given[pre + k] for k in ("ln1_g", "ln1_b", "ln2_g", "ln2_b")]
        parts.append(jnp.pad(given[pre + "sinks"].reshape(1, -1), ((0, 0), (0, dm - nl * A_Q_HEADS))))
        p = jnp.concatenate(parts)
        return jnp.pad(p, ((0, rows.shape[0] - p.shape[0]), (0, 0)))

    souts = adam_reduce(rows_all, pack_small(""), pack_small("m_"), pack_small("v_"), "adam_small")

    def unpack_small(o):
        r = {"b_ada": o[0:nl * 6].reshape(nl, 6 * dm)}
        for j, k in enumerate(("ln1_g", "ln1_b", "ln2_g", "ln2_b")):
            r[k] = o[nl * 6 + j * nl: nl * 6 + (j + 1) * nl]
        r["sinks"] = o[nl * 10, 0:nl * A_Q_HEADS].reshape(nl, A_Q_HEADS)
        return r

    small_out = [unpack_small(o) for o in souts]
    loss = souts[0][nl * 10 + 1, 0]

    dmod_all = rows_all[:, 0:nl * 6].reshape(N_DEV, nl, 6 * dm)
    dm_loc = lax.dynamic_slice_in_dim(dmod_all, me * wc, wc, axis=2).transpose(1, 0, 2)
    dm_loc = jnp.pad(dm_loc, ((0, 0), (0, LANES - N_DEV), (0, 0)))
    sct = jnp.pad(sc_all.T, ((0, 0), (0, LANES - N_DEV)))
    ada_out = adam_w_ada(sct, dm_loc, w_ada, m_w_ada, v_w_ada)

    names = ["w_ada", "b_ada", "w_in", "sinks", "w_a", "w_b", "w_o", "ln1_g", "ln1_b",
             "w_gate", "w_up", "w_down", "ln2_g", "ln2_b"]

    def pick(kind, nm):
        if nm == "w_ada":
            return ada_out[kind]
        if nm in small_out[kind]:
            return small_out[kind][nm]
        return big_out[nm][kind]

    result = [loss, grad_x]
    for kind in range(4):
        result += [pick(kind, nm) for nm in names]
    return tuple(result)
```

```python
import functools

import jax
import jax.numpy as jnp
from jax import lax
from jax.experimental import pallas as pl
from jax.experimental.pallas import tpu as pltpu

F32 = jnp.float32
BF16 = jnp.bfloat16

D_MODEL = 1024
HEAD_DIM = 64
A_Q_HEADS = 8
A_KV_HEADS = 2
A_WINDOW = 128
B_GROUPS = ((128, 1), (512, 4), (2048, 16))
N_GROUPS = len(B_GROUPS)
B_HEADS_PER_GROUP = 4
N_ATTN_HEADS = A_Q_HEADS + B_HEADS_PER_GROUP * N_GROUPS
BLOCK = 128
A_W = (A_Q_HEADS + 2 * A_KV_HEADS) * HEAD_DIM
B_OUT_W = B_HEADS_PER_GROUP * HEAD_DIM
B_GW = 3 * B_OUT_W
B_ALL = N_GROUPS * B_OUT_W
GATE_COL = A_W + 3 * B_ALL
D_FF = 2816
DN_ALPHA = 8.0 ** 0.25
LN_EPS = 1e-5
NEG_INF = -1e30
ADAM_LR, ADAM_B1, ADAM_B2, ADAM_EPS, ADAM_WD, ADAM_STEP = 0.001, 0.9, 0.999, 1e-08, 0.01, 10

N_DEV = 8
MESH = pl.DeviceIdType.MESH
VMEM_LIMIT = 56 * 1024 * 1024
ROW_TILE = 512
WGRAD_TILE_ELEMS = 2 * 1024 * 1024
LANES = 128
BF16_ROWS = 16


def _cp(*sem):
    return pltpu.CompilerParams(dimension_semantics=sem, vmem_limit_bytes=VMEM_LIMIT)


def _row_tile(t):
    return min(ROW_TILE, t)


def _slope(head):
    return 2.0 ** (-8.0 * (head + 1) / N_ATTN_HEADS)


def _sigmoid(x):
    return 1.0 / (1.0 + jnp.exp(-x))


def _dot(a, b):
    return jnp.dot(a, b, preferred_element_type=F32)


def _dot_nt(a, b):
    return lax.dot_general(a, b, (((1,), (1,)), ((), ())), preferred_element_type=F32)


def _dot_tn(a, b):
    return lax.dot_general(a, b, (((0,), (0,)), ((), ())), preferred_element_type=F32)


def _fold_scratch(tm, w):
    return [pltpu.VMEM((tm, LANES), F32)] * (w // LANES)


def _fold_to(dst_ref, val, scrs, d, col0=0):
    tm, w = val.shape
    if d == 1:
        dst_ref[0, :, col0:col0 + w] = val.astype(dst_ref.dtype)
        return
    for cb in range(w // LANES):
        scrs[cb][...] = val[:, cb * LANES:(cb + 1) * LANES]
    for r in range(d):
        for cb in range(w // LANES):
            piece = scrs[cb][pl.ds(r, tm // d, stride=d), :]
            dst_ref[r, :, col0 + cb * LANES:col0 + (cb + 1) * LANES] = piece.astype(dst_ref.dtype)


def _unfold_rows(rows_of, scrs, d, n, w):
    for r in range(d):
        for cb in range(w // LANES):
            scrs[cb][pl.ds(r, n, stride=d), :] = rows_of(r, slice(cb * LANES, (cb + 1) * LANES)).astype(F32)
    return jnp.concatenate([scrs[cb][...] for cb in range(w // LANES)], axis=1)


def _unfold_from(src_ref, scrs, d):
    if d == 1:
        return src_ref[0].astype(F32)
    _, n, w = src_ref.shape
    return _unfold_rows(lambda r, cols: src_ref[r, :, cols], scrs, d, n, w)


def _folded_spec(d, tm, w):
    return pl.BlockSpec((d, tm // d, w), lambda i: (0, i, 0))


def _me():
    return lax.axis_index("x"), lax.axis_index("y"), lax.axis_index("c")


def _flip(v, bit):
    return 1 - v if bit else v


def _peer(k):
    x, y, c = _me()
    return (_flip(x, k & 4), _flip(y, k & 2), _flip(c, k & 1))


def _peer_index(k):
    px, py, pc = _peer(k)
    return 4 * px + 2 * py + pc


def all_gather_small(v, name):
    r, c = v.shape

    def body(v_ref, out_ref, send_sems, recv_sems):
        me = _peer_index(0)
        out_ref[me] = v_ref[...]
        copies = []
        for k in range(1, N_DEV):
            cp = pltpu.make_async_remote_copy(
                src_ref=v_ref, dst_ref=out_ref.at[me],
                send_sem=send_sems.at[k - 1], recv_sem=recv_sems.at[k - 1],
                device_id=_peer(k), device_id_type=MESH)
            cp.start()
            copies.append(cp)
        for k in range(1, N_DEV):
            pltpu.make_async_remote_copy(
                src_ref=v_ref, dst_ref=out_ref.at[_peer_index(k)],
                send_sem=send_sems.at[k - 1], recv_sem=recv_sems.at[k - 1],
                device_id=_peer(k), device_id_type=MESH).wait_recv()
        for cp in copies:
            cp.wait_send()

    return pl.pallas_call(
        body, name=name,
        out_shape=jax.ShapeDtypeStruct((N_DEV, r, c), v.dtype),
        in_specs=[pl.BlockSpec(memory_space=pltpu.VMEM)],
        out_specs=pl.BlockSpec(memory_space=pltpu.VMEM),
        scratch_shapes=[pltpu.SemaphoreType.DMA((N_DEV - 1,)), pltpu.SemaphoreType.DMA((N_DEV - 1,))],
        compiler_params=pltpu.CompilerParams(vmem_limit_bytes=VMEM_LIMIT),
    )(v)


class Piece:
    def __init__(self, name, buf, axis, base, size):
        self.name, self.buf, self.axis, self.base, self.size = name, buf, axis, base, size

    def window(self, ref, j):
        start = self.base + j * self.size
        if self.axis == 1:
            return ref.at[:, pl.ds(pl.multiple_of(start, LANES), self.size)]
        return ref.at[pl.ds(pl.multiple_of(start, BF16_ROWS), self.size), :]


class Exchange:
    def __init__(self, kind, pieces, ins, out_shapes, bufs):
        self.kind, self.pieces, self.ins, self.out_shapes = kind, pieces, list(ins), list(out_shapes)
        self.buf_of = {nm: i for i, nm in enumerate(bufs)}
        self.n_in, self.n_out = len(self.ins), len(self.out_shapes)
        n = len(pieces)
        self.scratch = [pltpu.SemaphoreType.DMA((n, N_DEV - 1)), pltpu.SemaphoreType.DMA((n, N_DEV - 1)),
                        pltpu.SemaphoreType.DMA((n,))]
        self.in_specs = [pl.BlockSpec(memory_space=pl.ANY)] * self.n_in
        self.out_specs = [pl.BlockSpec(memory_space=pl.ANY)] * self.n_out
        self.out_shape = [jax.ShapeDtypeStruct(s, BF16) for s in self.out_shapes]

    def _ends(self, pi, ins, outs, to):
        pc = self.pieces[pi]
        if self.kind == "gather":
            return ins[pi], pc.window(outs[self.buf_of[pc.buf]], _peer_index(0))
        return pc.window(ins[self.buf_of[pc.buf]], to), outs[pi].at[_peer_index(0)]

    def _landing(self, pi, outs, frm):
        pc = self.pieces[pi]
        if self.kind == "gather":
            return pc.window(outs[self.buf_of[pc.buf]], frm)
        return outs[pi].at[frm]

    def _remote(self, pi, k, src, dst, sems):
        return pltpu.make_async_remote_copy(
            src_ref=src, dst_ref=dst, send_sem=sems[0].at[pi, k - 1], recv_sem=sems[1].at[pi, k - 1],
            device_id=_peer(k), device_id_type=MESH)

    def _local(self, pi, ins, outs, sems):
        return pltpu.make_async_copy(*self._ends(pi, ins, outs, _peer_index(0)), sems[2].at[pi])

    def start(self, ins, outs, sems):
        for pi in range(len(self.pieces)):
            self._local(pi, ins, outs, sems).start()
            for k in range(1, N_DEV):
                self._remote(pi, k, *self._ends(pi, ins, outs, _peer_index(k)), sems).start()

    def finish(self, ins, outs, sems):
        for pi in range(len(self.pieces)):
            src_like = self._ends(pi, ins, outs, _peer_index(0))[0]
            for k in range(1, N_DEV):
                self._remote(pi, k, src_like, self._landing(pi, outs, _peer_index(k)), sems).wait_recv()
        for pi in range(len(self.pieces)):
            for k in range(1, N_DEV):
                self._remote(pi, k, *self._ends(pi, ins, outs, _peer_index(k)), sems).wait_send()
            self._local(pi, ins, outs, sems).wait()


def _hosted(ex, refs, n_in, n_out, first, last):
    if ex is None:
        return refs
    ins, rest = refs[:n_in], refs[n_in:]
    ex_ins, rest = rest[:ex.n_in], rest[ex.n_in:]
    outs, rest = rest[:n_out], rest[n_out:]
    ex_outs, rest = rest[:ex.n_out], rest[ex.n_out:]
    scr, sems = rest[:len(rest) - 3], rest[len(rest) - 3:]
    pl.when(first)(lambda: ex.start(ex_ins, ex_outs, sems))
    pl.when(last)(lambda: ex.finish(ex_ins, ex_outs, sems))
    return tuple(ins) + tuple(outs) + tuple(scr)


def _host_call(body, ex, *, name, grid, out_shape, in_specs, out_specs, scratch_shapes=(), sem=None, args):
    n_out = len(out_shape)
    if ex is not None:
        out_shape = list(out_shape) + ex.out_shape
        in_specs = list(in_specs) + ex.in_specs
        out_specs = list(out_specs) + ex.out_specs
        scratch_shapes = list(scratch_shapes) + ex.scratch
        args = list(args) + ex.ins
    res = pl.pallas_call(body, name=name, grid=grid, out_shape=out_shape, in_specs=in_specs, out_specs=out_specs,
                         scratch_shapes=scratch_shapes, compiler_params=_cp(*sem))(*args)
    return res[:n_out], res[n_out:]


def run_exchange(ex, name):
    def body(*refs):
        ins, outs, sems = refs[:ex.n_in], refs[ex.n_in:ex.n_in + ex.n_out], refs[ex.n_in + ex.n_out:]
        ex.start(ins, outs, sems)
        ex.finish(ins, outs, sems)

    return pl.pallas_call(body, name=name, out_shape=ex.out_shape, in_specs=ex.in_specs, out_specs=ex.out_specs,
                          scratch_shapes=ex.scratch)(*ex.ins)


def mod_partial(c_all, w_ada, b_loc):
    nl, dm, wc = w_ada.shape

    def body(c_ref, w_ref, b_ref, o_ref, sc_ref):
        cc = c_ref[...]
        sc = cc * _sigmoid(cc)
        sc_ref[...] = sc
        o_ref[...] = jnp.dot(sc, w_ref[...], preferred_element_type=F32,
                             precision=lax.Precision.HIGHEST) + b_ref[...]

    return pl.pallas_call(
        body, name="mod_partial", grid=(nl,),
        out_shape=[jax.ShapeDtypeStruct((nl, N_DEV, wc), F32), jax.ShapeDtypeStruct((N_DEV, dm), F32)],
        in_specs=[pl.BlockSpec((N_DEV, dm), lambda l: (0, 0)),
                  pl.BlockSpec((None, dm, wc), lambda l: (l, 0, 0)),
                  pl.BlockSpec((None, 1, wc), lambda l: (l, 0, 0))],
        out_specs=[pl.BlockSpec((None, N_DEV, wc), lambda l: (l, 0, 0)),
                   pl.BlockSpec((N_DEV, dm), lambda l: (0, 0))],
        compiler_params=_cp("arbitrary"),
    )(c_all, w_ada, b_loc)


def in_proj(x, s, sh, w, name, ex=None):
    t, dm = x.shape
    n = w.shape[1]
    tm = _row_tile(t)
    nsteps = t // tm
    ch = B_OUT_W
    dils = [dil for _, dil in B_GROUPS if dil > 1]

    def body(*refs):
        i = pl.program_id(0)
        x_ref, s_ref, sh_ref, w_ref, u_ref, *rest = _hosted(ex, refs, 4, 2 + 2 * len(dils), i == 0, i == nsteps - 1)
        uf_refs, o_ref, qf_refs = rest[:len(dils)], rest[len(dils)], rest[len(dils) + 1:len(dils) * 2 + 1]
        scrs = rest[len(dils) * 2 + 1:]
        uf = x_ref[...] * (1.0 + s_ref[...]) + sh_ref[...]
        u = uf.astype(BF16)
        u_ref[...] = u
        for d, uf_ref in zip(dils, uf_refs):
            _fold_to(uf_ref, uf, scrs, d)
        for c0 in range(0, n, ch):
            res = _dot(u, w_ref[:, c0:c0 + ch])
            o_ref[:, c0:c0 + ch] = res.astype(BF16)
            if A_W <= c0 < GATE_COL:
                part, g = divmod((c0 - A_W) // ch, N_GROUPS)
                d = B_GROUPS[g][1]
                if d > 1:
                    _fold_to(qf_refs[dils.index(d)], res, scrs, d, part * ch)

    vec = pl.BlockSpec((1, dm), lambda i: (0, 0))
    row = lambda w_: pl.BlockSpec((tm, w_), lambda i: (i, 0))
    return _host_call(
        body, ex, name=name, grid=(nsteps,),
        out_shape=[jax.ShapeDtypeStruct((t, dm), BF16)]
                  + [jax.ShapeDtypeStruct((d, t // d, dm), BF16) for d in dils]
                  + [jax.ShapeDtypeStruct((t, n), BF16)]
                  + [jax.ShapeDtypeStruct((d, t // d, B_GW), BF16) for d in dils],
        in_specs=[row(dm), vec, vec, pl.BlockSpec((dm, n), lambda i: (0, 0))],
        out_specs=[row(dm)] + [_folded_spec(d, tm, dm) for d in dils] + [row(n)]
                  + [_folded_spec(d, tm, B_GW) for d in dils],
        scratch_shapes=_fold_scratch(tm, dm), sem=("arbitrary",), args=[x, s, sh, w])


def modmm(x, s, sh, w, name, ex=None):
    t, dm = x.shape
    n = w.shape[1]
    tm = _row_tile(t)
    nsteps = t // tm
    ch = 512

    def body(*refs):
        i = pl.program_id(0)
        x_ref, s_ref, sh_ref, w_ref, u_ref, o_ref = _hosted(ex, refs, 4, 2, i == 0, i == nsteps - 1)
        u = (x_ref[...] * (1.0 + s_ref[...]) + sh_ref[...]).astype(BF16)
        u_ref[...] = u
        for c0 in range(0, n, ch):
            o_ref[:, c0:c0 + ch] = _dot(u, w_ref[:, c0:c0 + ch]).astype(BF16)

    vec = pl.BlockSpec((1, dm), lambda i: (0, 0))
    return _host_call(
        body, ex, name=name, grid=(nsteps,),
        out_shape=[jax.ShapeDtypeStruct((t, dm), BF16), jax.ShapeDtypeStruct((t, n), BF16)],
        in_specs=[pl.BlockSpec((tm, dm), lambda i: (i, 0)), vec, vec,
                  pl.BlockSpec((dm, n), lambda i: (0, 0))],
        out_specs=[pl.BlockSpec((tm, dm), lambda i: (i, 0)), pl.BlockSpec((tm, n), lambda i: (i, 0))],
        sem=("arbitrary",), args=[x, s, sh, w])


def _ln_store(y, xres_ref, g_ref, lg_ref, lb_ref, y_ref, xo_ref, zh_ref, rs_ref):
    y_ref[...] = y.astype(BF16)
    z = DN_ALPHA * xres_ref[...] + g_ref[...] * y
    mu = jnp.mean(z, axis=1, keepdims=True)
    zc = z - mu
    var = jnp.mean(zc * zc, axis=1, keepdims=True)
    rstd = lax.rsqrt(var + LN_EPS)
    zhat = zc * rstd
    zh_ref[...] = zhat
    xo_ref[...] = zhat * lg_ref[...] + lb_ref[...]
    rs_ref[...] = jnp.broadcast_to(rstd, rs_ref.shape)


def _ln_out_shapes(t, dm):
    return [jax.ShapeDtypeStruct((t, dm), BF16), jax.ShapeDtypeStruct((t, dm), F32),
            jax.ShapeDtypeStruct((t, dm), F32), jax.ShapeDtypeStruct((t, LANES), F32)]


def _ln_out_specs(tm, dm):
    row = pl.BlockSpec((tm, dm), lambda i: (i, 0))
    return [row, row, row, pl.BlockSpec((tm, LANES), lambda i: (i, 0))]


def proj_ln(a, w, xres, gate, lg, lb, name):
    t, k = a.shape
    dm = w.shape[1]
    tm = _row_tile(t)

    def body(a_ref, w_ref, xres_ref, g_ref, lg_ref, lb_ref, y_ref, xo_ref, zh_ref, rs_ref):
        y = _dot(a_ref[...], w_ref[...])
        _ln_store(y, xres_ref, g_ref, lg_ref, lb_ref, y_ref, xo_ref, zh_ref, rs_ref)

    vec = pl.BlockSpec((1, dm), lambda i: (0, 0))
    return pl.pallas_call(
        body, name=name, grid=(t // tm,),
        out_shape=_ln_out_shapes(t, dm),
        in_specs=[pl.BlockSpec((tm, k), lambda i: (i, 0)), pl.BlockSpec((k, dm), lambda i: (0, 0)),
                  pl.BlockSpec((tm, dm), lambda i: (i, 0)), vec, vec, vec],
        out_specs=_ln_out_specs(tm, dm),
        compiler_params=_cp("parallel"),
    )(a, w, xres, gate, lg, lb)


def swiglu_proj_ln(ab, w, xres, gate, lg, lb, name):
    t = ab.shape[0]
    f, dm = w.shape
    tm = _row_tile(t)

    def body(a_ref, b_ref, w_ref, xres_ref, g_ref, lg_ref, lb_ref, h_ref, y_ref, xo_ref, zh_ref, rs_ref):
        a = a_ref[...].astype(F32)
        h = (a * _sigmoid(a) * b_ref[...].astype(F32)).astype(BF16)
        h_ref[...] = h
        y = _dot(h, w_ref[...])
        _ln_store(y, xres_ref, g_ref, lg_ref, lb_ref, y_ref, xo_ref, zh_ref, rs_ref)

    vec = pl.BlockSpec((1, dm), lambda i: (0, 0))
    return pl.pallas_call(
        body, name=name, grid=(t // tm,),
        out_shape=[jax.ShapeDtypeStruct((t, f), BF16)] + _ln_out_shapes(t, dm),
        in_specs=[pl.BlockSpec((tm, f), lambda i: (i, 0)), pl.BlockSpec((tm, f), lambda i: (i, 1)),
                  pl.BlockSpec((f, dm), lambda i: (0, 0)),
                  pl.BlockSpec((tm, dm), lambda i: (i, 0)), vec, vec, vec],
        out_specs=[pl.BlockSpec((tm, f), lambda i: (i, 0))] + _ln_out_specs(tm, dm),
        compiler_params=_cp("parallel"),
    )(ab, ab, w, xres, gate, lg, lb)


class AttnCfg:
    def __init__(self, dil, heads, kv_heads, qc, kc, vc, max_dist, head0, sinks):
        self.dil, self.heads, self.kv_heads = dil, heads, kv_heads
        self.qc, self.kc, self.vc = qc, kc, vc
        self.max_dist, self.head0, self.sinks = max_dist, head0, sinks
        self.wq = heads * HEAD_DIM
        self.wk = kv_heads * HEAD_DIM
        self.wout = self.wq + 2 * self.wk


ATTN_A = AttnCfg(1, A_Q_HEADS, A_KV_HEADS, 0, A_Q_HEADS * HEAD_DIM, (A_Q_HEADS + A_KV_HEADS) * HEAD_DIM,
                 A_WINDOW - 1, 0, True)


def _attn_b_cfg(g):
    win, dil = B_GROUPS[g]
    cols = ((A_W + g * B_OUT_W, A_W + B_ALL + g * B_OUT_W, A_W + 2 * B_ALL + g * B_OUT_W) if dil == 1
            else (0, B_OUT_W, 2 * B_OUT_W))
    return AttnCfg(dil, B_HEADS_PER_GROUP, B_HEADS_PER_GROUP, *cols, win // dil,
                   A_Q_HEADS + g * B_HEADS_PER_GROUP, False)


ATTN_B = [_attn_b_cfg(g) for g in range(N_GROUPS)]


def _band(i, max_dist):
    qi = lax.broadcasted_iota(jnp.int32, (BLOCK, 2 * BLOCK), 0)
    sj = lax.broadcasted_iota(jnp.int32, (BLOCK, 2 * BLOCK), 1)
    dist = qi + BLOCK - sj
    valid = (dist >= 0) & (dist <= max_dist)
    first_key = jnp.where(i > 0, 0, BLOCK)
    valid_first = valid & (sj >= first_key)
    return dist, valid, valid_first


def _attn_geometry(cfg, n):
    tq = min(512, n)
    return tq, tq // BLOCK, n // tq


def attn_fwd(qkv, cfg, sinks, name):
    d, n, _ = qkv.shape
    tq, nsub, nqb = _attn_geometry(cfg, n)
    wq, wk = cfg.wq, cfg.wk
    grp = cfg.heads // cfg.kv_heads

    def body(sink_ref, q_ref, kc_ref, kp_ref, vc_ref, vp_ref, o_ref, l_ref, kf, vf):
        i = pl.program_id(1)
        kf[0:BLOCK, :] = kp_ref[...]
        kf[BLOCK:, :] = kc_ref[...]
        vf[0:BLOCK, :] = vp_ref[...]
        vf[BLOCK:, :] = vc_ref[...]
        dist, valid, valid_first = _band(i, cfg.max_dist)
        distf = dist.astype(F32)
        for h in range(cfg.heads):
            kv = h // grp
            bias = distf * (-(_slope(cfg.head0 + h) * d))
            hs = slice(h * HEAD_DIM, (h + 1) * HEAD_DIM)
            ks = slice(kv * HEAD_DIM, (kv + 1) * HEAD_DIM)
            rows = [slice(a * BLOCK, (a + 1) * BLOCK) for a in range(nsub)]
            wins = [slice(a * BLOCK, (a + 2) * BLOCK) for a in range(nsub)]
            ss = [_dot_nt(q_ref[rows[a], hs], kf[wins[a], ks]) * (HEAD_DIM ** -0.5) for a in range(nsub)]
            ps = []
            for a in range(nsub):
                s = jnp.where(valid_first if a == 0 else valid, ss[a] + bias, NEG_INF)
                m = jnp.max(s, axis=1, keepdims=True)
                if cfg.sinks:
                    m = jnp.maximum(m, sink_ref[h])
                e = jnp.exp(s - m)
                den = jnp.sum(e, axis=1, keepdims=True)
                if cfg.sinks:
                    den = den + jnp.exp(sink_ref[h] - m)
                ps.append((e / den).astype(BF16))
                l_ref[rows[a], hs] = jnp.broadcast_to(m + jnp.log(den), (BLOCK, HEAD_DIM))
            for a in range(nsub):
                o_ref[rows[a], hs] = _dot(ps[a], vf[wins[a], ks]).astype(BF16)

    prev = lambda i: jnp.maximum(i * nsub - 1, 0)
    cur = lambda w, c: pl.BlockSpec((None, tq, w), lambda r, i: (r, i, c // w))
    prv = lambda w, c: pl.BlockSpec((None, BLOCK, w), lambda r, i: (r, prev(i), c // w))
    out = pl.BlockSpec((None, tq, wq), lambda r, i: (r, i, 0))
    return pl.pallas_call(
        body, name=name, grid=(d, nqb),
        out_shape=[jax.ShapeDtypeStruct((d, n, wq), BF16), jax.ShapeDtypeStruct((d, n, wq), F32)],
        in_specs=[pl.BlockSpec(memory_space=pltpu.SMEM),
                  cur(wq, cfg.qc), cur(wk, cfg.kc), prv(wk, cfg.kc), cur(wk, cfg.vc), prv(wk, cfg.vc)],
        out_specs=[out, out],
        scratch_shapes=[pltpu.VMEM((tq + BLOCK, wk), BF16), pltpu.VMEM((tq + BLOCK, wk), BF16)],
        compiler_params=_cp("parallel", "parallel"),
    )(sinks, qkv, qkv, qkv, qkv, qkv)


def mix_merge(ya, o_g, l_g, proj, w_a, w_b, name):
    t = ya.shape[0]
    dm = w_a.shape[1]
    tm = _row_tile(t)
    gcol = GATE_COL // dm
    dils = [o.shape[0] for o in o_g]

    def body(ya_ref, o0, o1, o2, l0, l1, l2, ga_ref, gb_ref, wa_ref, wb_ref, yb_ref, mg_ref, *scrs):
        ls = [_unfold_from(l, scrs, d) for l, d in zip((l0, l1, l2), dils)]
        m = jnp.maximum(jnp.maximum(ls[0], ls[1]), ls[2])
        es = [jnp.exp(l - m) for l in ls]
        inv = 1.0 / (es[0] + es[1] + es[2])
        yb = sum(_unfold_from(o, scrs, d) * (e * inv) for o, e, d in zip((o0, o1, o2), es, dils)).astype(BF16)
        yb_ref[...] = yb
        pa = _dot(ya_ref[...], wa_ref[...])
        pb = _dot(yb, wb_ref[...])
        mg = _sigmoid(ga_ref[...].astype(F32)) * pa + _sigmoid(gb_ref[...].astype(F32)) * pb
        mg_ref[...] = mg.astype(BF16)

    wide = lambda w: pl.BlockSpec((tm, w), lambda i: (i, 0))
    folded = [_folded_spec(d, tm, B_OUT_W) for d in dils]
    return pl.pallas_call(
        body, name=name, grid=(t // tm,),
        out_shape=[jax.ShapeDtypeStruct((t, B_OUT_W), BF16), jax.ShapeDtypeStruct((t, dm), BF16)],
        in_specs=[wide(ya.shape[1])] + folded + folded
                 + [pl.BlockSpec((tm, dm), lambda i: (i, gcol)), pl.BlockSpec((tm, dm), lambda i: (i, gcol + 1)),
                    pl.BlockSpec(w_a.shape, lambda i: (0, 0)), pl.BlockSpec(w_b.shape, lambda i: (0, 0))],
        out_specs=[wide(B_OUT_W), wide(dm)],
        scratch_shapes=_fold_scratch(tm, B_OUT_W),
        compiler_params=_cp("parallel"),
    )(ya, *o_g, *l_g, proj, proj, w_a, w_b)


def loss_head(y, target):
    t, dm = y.shape
    tm = _row_tile(t)

    def body(y_ref, t_ref, dy_ref, loss_ref):
        @pl.when(pl.program_id(0) == 0)
        def _():
            loss_ref[...] = jnp.zeros_like(loss_ref)
        err = y_ref[...] - t_ref[...]
        dy_ref[...] = err * (1.0 / dm)
        per_row = jnp.sum(err * err, axis=1, keepdims=True) * (1.0 / dm)
        loss_ref[...] += 0.5 * jnp.sum(per_row, axis=0, keepdims=True)

    row = pl.BlockSpec((tm, dm), lambda i: (i, 0))
    return pl.pallas_call(
        body, name="loss_head", grid=(t // tm,),
        out_shape=[jax.ShapeDtypeStruct((t, dm), F32), jax.ShapeDtypeStruct((8, LANES), F32)],
        in_specs=[row, row],
        out_specs=[row, pl.BlockSpec((8, LANES), lambda i: (0, 0))],
        compiler_params=_cp("arbitrary"),
    )(y, target)


def _fold_rows(v):
    tm, c = v.shape
    return jnp.sum(v.reshape(tm // 8, 8, c), axis=0)


def _finish_sums(refs, nsteps):
    @pl.when(pl.program_id(0) == nsteps - 1)
    def _():
        for r in refs:
            r[...] = jnp.broadcast_to(jnp.sum(r[...], axis=0, keepdims=True), r.shape)


def ln_bwd(dxo, zhat, rstd, ysub, lg, gate, name):
    t, dm = dxo.shape
    tm = _row_tile(t)

    def body(dxo_ref, zh_ref, rs_ref, y_ref, lg_ref, g_ref, dz_ref, dy_ref, sg_ref, sb_ref, sgate_ref):
        @pl.when(pl.program_id(0) == 0)
        def _():
            for r in (sg_ref, sb_ref, sgate_ref):
                r[...] = jnp.zeros_like(r)
        dxo_v = dxo_ref[...]
        zh = zh_ref[...]
        dxh = dxo_v * lg_ref[...]
        m1 = jnp.mean(dxh, axis=1, keepdims=True)
        m2 = jnp.mean(dxh * zh, axis=1, keepdims=True)
        dz = rs_ref[:, 0:1] * (dxh - m1 - zh * m2)
        dz_ref[...] = dz
        dy_ref[...] = (g_ref[...] * dz).astype(BF16)
        sg_ref[...] += _fold_rows(dxo_v * zh)
        sb_ref[...] += _fold_rows(dxo_v)
        sgate_ref[...] += _fold_rows(dz * y_ref[...].astype(F32))
        _finish_sums((sg_ref, sb_ref, sgate_ref), t // tm)

    row = pl.BlockSpec((tm, dm), lambda i: (i, 0))
    vec = pl.BlockSpec((1, dm), lambda i: (0, 0))
    acc = pl.BlockSpec((8, dm), lambda i: (0, 0))
    return pl.pallas_call(
        body, name=name, grid=(t // tm,),
        out_shape=[jax.ShapeDtypeStruct((t, dm), F32), jax.ShapeDtypeStruct((t, dm), BF16)]
                  + [jax.ShapeDtypeStruct((8, dm), F32)] * 3,
        in_specs=[row, row, pl.BlockSpec((tm, LANES), lambda i: (i, 0)), row, vec, vec],
        out_specs=[row, row, acc, acc, acc],
        compiler_params=_cp("arbitrary"),
    )(dxo, zhat, rstd, ysub, lg, gate)


def _mod_bwd_store(du, dz_ref, x_ref, s_ref, dx_ref, ss_ref, ssh_ref, nsteps):
    @pl.when(pl.program_id(0) == 0)
    def _():
        ss_ref[...] = jnp.zeros_like(ss_ref)
        ssh_ref[...] = jnp.zeros_like(ssh_ref)
    dx_ref[...] = DN_ALPHA * dz_ref[...] + du * (1.0 + s_ref[...])
    ss_ref[...] += _fold_rows(du * x_ref[...])
    ssh_ref[...] += _fold_rows(du)
    _finish_sums((ss_ref, ssh_ref), nsteps)


def dgrad_ffn(g, wt, dz, xin, s, name, ex=None):
    t, dm = dz.shape
    k = g.shape[1]
    tm = _row_tile(t)
    nsteps = t // tm

    def body(*refs):
        i = pl.program_id(0)
        g_ref, w_ref, dz_ref, x_ref, s_ref, dx_ref, ss_ref, ssh_ref = _hosted(ex, refs, 5, 3, i == 0, i == nsteps - 1)
        du = _dot(g_ref[...], w_ref[...])
        _mod_bwd_store(du, dz_ref, x_ref, s_ref, dx_ref, ss_ref, ssh_ref, nsteps)

    row = pl.BlockSpec((tm, dm), lambda i: (i, 0))
    acc = pl.BlockSpec((8, dm), lambda i: (0, 0))
    return _host_call(
        body, ex, name=name, grid=(nsteps,),
        out_shape=[jax.ShapeDtypeStruct((t, dm), F32)] + [jax.ShapeDtypeStruct((8, dm), F32)] * 2,
        in_specs=[pl.BlockSpec((tm, k), lambda i: (i, 0)), pl.BlockSpec((k, dm), lambda i: (0, 0)),
                  row, row, pl.BlockSpec((1, dm), lambda i: (0, 0))],
        out_specs=[row, acc, acc], sem=("arbitrary",), args=[g, wt, dz, xin, s])


def dgrad_in(d_a, d_b, dgab, wt, dz, xin, s, name):
    t, dm = dz.shape
    tm = _row_tile(t)
    dils = [a.shape[0] for a in d_b]

    def body(da_ref, b0, b1, b2, dg_ref, w_ref, dz_ref, x_ref, s_ref, dx_ref, ss_ref, ssh_ref, *scrs):
        du = _dot(da_ref[0], w_ref[0:A_W, :])
        for g, (b_ref, d) in enumerate(zip((b0, b1, b2), dils)):
            v = b_ref[...].reshape(tm, B_GW)
            part = None
            for p in range(3):
                r0 = A_W + p * B_ALL + g * B_OUT_W
                term = _dot(v[:, p * B_OUT_W:(p + 1) * B_OUT_W], w_ref[r0:r0 + B_OUT_W, :])
                part = term if part is None else part + term
            if d == 1:
                du = du + part
            else:
                n = tm // d
                du = du + _unfold_rows(lambda r, cols: part[r * n:(r + 1) * n, cols], scrs, d, n, dm)
        for j in range(2):
            du = du + _dot(dg_ref[:, j * dm:(j + 1) * dm], w_ref[GATE_COL + j * dm:GATE_COL + (j + 1) * dm, :])
        _mod_bwd_store(du, dz_ref, x_ref, s_ref, dx_ref, ss_ref, ssh_ref, t // tm)

    row = pl.BlockSpec((tm, dm), lambda i: (i, 0))
    acc = pl.BlockSpec((8, dm), lambda i: (0, 0))
    return pl.pallas_call(
        body, name=name, grid=(t // tm,),
        out_shape=[jax.ShapeDtypeStruct((t, dm), F32)] + [jax.ShapeDtypeStruct((8, dm), F32)] * 2,
        in_specs=[_folded_spec(1, tm, A_W)] + [_folded_spec(d, tm, B_GW) for d in dils]
                 + [pl.BlockSpec((tm, 2 * dm), lambda i: (i, 0)), pl.BlockSpec(wt.shape, lambda i: (0, 0)),
                    row, row, pl.BlockSpec((1, dm), lambda i: (0, 0))],
        out_specs=[row, acc, acc],
        scratch_shapes=_fold_scratch(tm, dm),
        compiler_params=_cp("arbitrary"),
    )(d_a, *d_b, dgab, wt, dz, xin, s)


def wgrad(a, b, buf, tn, nj, b0, o0, om, name):
    t, k = a.shape
    tt = ROW_TILE
    while tt * 2 * k <= WGRAD_TILE_ELEMS and tt * 2 <= t:
        tt *= 2
    nsteps = t // tt

    def body(a_ref, b_ref, buf_ref, o_ref, acc):
        s = pl.program_id(1)

        @pl.when(s == 0)
        def _():
            acc[...] = jnp.zeros_like(acc)
        acc[...] += _dot_tn(a_ref[...], b_ref[...])

        @pl.when(s == nsteps - 1)
        def _():
            o_ref[...] = acc[...].astype(BF16)

    return pl.pallas_call(
        body, name=name, grid=(nj, nsteps),
        out_shape=jax.ShapeDtypeStruct(buf.shape, buf.dtype),
        in_specs=[pl.BlockSpec((tt, k), lambda j, s: (s, 0)),
                  pl.BlockSpec((tt, tn), lambda j, s: (s, b0 + j)),
                  pl.BlockSpec(memory_space=pl.ANY)],
        out_specs=pl.BlockSpec((k, tn), lambda j, s: (0, o0 + om * j)),
        scratch_shapes=[pltpu.VMEM((k, tn), F32)],
        input_output_aliases={2: 0},
        compiler_params=_cp("parallel", "arbitrary"),
    )(a, b, buf)


def dswiglu(dy, wdt, ab, name, ex=None):
    t, dm = dy.shape
    f = wdt.shape[1]
    tm = _row_tile(t)
    nsteps = t // tm

    def body(*refs):
        i = pl.program_id(0)
        dy_ref, w_ref, a_ref, b_ref, o_ref = _hosted(ex, refs, 4, 1, i == 0, i == nsteps - 1)
        dh = _dot(dy_ref[...], w_ref[...])
        a = a_ref[...].astype(F32)
        sg = _sigmoid(a)
        o_ref[:, 0:f] = (dh * b_ref[...].astype(F32) * (sg * (1.0 + a * (1.0 - sg)))).astype(BF16)
        o_ref[:, f:] = (dh * (a * sg)).astype(BF16)

    return _host_call(
        body, ex, name=name, grid=(nsteps,),
        out_shape=[jax.ShapeDtypeStruct((t, 2 * f), BF16)],
        in_specs=[pl.BlockSpec((tm, dm), lambda i: (i, 0)), pl.BlockSpec((dm, f), lambda i: (0, 0)),
                  pl.BlockSpec((tm, f), lambda i: (i, 0)), pl.BlockSpec((tm, f), lambda i: (i, 1))],
        out_specs=[pl.BlockSpec((tm, 2 * f), lambda i: (i, 0))], sem=("arbitrary",), args=[dy, wdt, ab, ab])


def dmerge(do, wot, ya, yb, w_a, w_b, wat, wbt, proj, name):
    t, dm = do.shape
    tm = _row_tile(t)
    gcol = GATE_COL // dm

    def body(do_ref, wot_ref, ya_ref, yb_ref, wa_ref, wb_ref, wat_ref, wbt_ref, g_ref,
             dp_ref, dya_ref, dyb_ref, dg_ref, dm_scr):
        j = pl.program_id(1)

        @pl.when(j == 0)
        def _():
            dm_scr[...] = _dot(do_ref[...], wot_ref[...])

        def branch(y_ref, w_ref, wt_ref, dy_ref):
            p = _dot(y_ref[...], w_ref[...])
            sg = _sigmoid(g_ref[...].astype(F32))
            dmg = dm_scr[...]
            dp = (dmg * sg).astype(BF16)
            dp_ref[...] = dp
            dg_ref[...] = (dmg * p * (sg * (1.0 - sg))).astype(BF16)
            dy_ref[...] = _dot(dp, wt_ref[...]).astype(dy_ref.dtype)

        pl.when(j == 0)(lambda: branch(ya_ref, wa_ref, wat_ref, dya_ref))
        pl.when(j == 1)(lambda: branch(yb_ref, wb_ref, wbt_ref, dyb_ref))

    full = lambda arr: pl.BlockSpec(arr.shape, lambda i, j: (0, 0))
    rowc = lambda w: pl.BlockSpec((tm, w), lambda i, j: (i, 0))
    return pl.pallas_call(
        body, name=name, grid=(t // tm, 2),
        out_shape=[jax.ShapeDtypeStruct((t, 2 * dm), BF16), jax.ShapeDtypeStruct((t, ya.shape[1]), BF16),
                   jax.ShapeDtypeStruct((t, yb.shape[1]), F32), jax.ShapeDtypeStruct((t, 2 * dm), BF16)],
        in_specs=[rowc(dm), full(wot), rowc(ya.shape[1]), rowc(yb.shape[1]), full(w_a), full(w_b),
                  full(wat), full(wbt), pl.BlockSpec((tm, dm), lambda i, j: (i, gcol + j))],
        out_specs=[pl.BlockSpec((tm, dm), lambda i, j: (i, j)), rowc(ya.shape[1]), rowc(yb.shape[1]),
                   pl.BlockSpec((tm, dm), lambda i, j: (i, j))],
        scratch_shapes=[pltpu.VMEM((tm, dm), F32)],
        compiler_params=_cp("parallel", "arbitrary"),
    )(do, wot, ya, yb, w_a, w_b, wat, wbt, proj)


def mix_bwd(dyb, o_g, l_g, name):
    t, w = dyb.shape
    tm = _row_tile(t)
    nh = w // HEAD_DIM
    dils = [o.shape[0] for o in o_g]

    def body(dyb_ref, o0, o1, o2, l0, l1, l2, do0, do1, do2, dl0, dl1, dl2, *scr):
        ls = [_unfold_from(l, scr, d) for l, d in zip((l0, l1, l2), dils)]
        m = jnp.maximum(jnp.maximum(ls[0], ls[1]), ls[2])
        es = [jnp.exp(l - m) for l in ls]
        inv = 1.0 / (es[0] + es[1] + es[2])
        wts = [e * inv for e in es]
        dyb_v = dyb_ref[...]
        dws = []
        for o_ref, do_ref, wt, d in zip((o0, o1, o2), (do0, do1, do2), wts, dils):
            prod = dyb_v * _unfold_from(o_ref, scr, d)
            _fold_to(do_ref, dyb_v * wt, scr, d)
            for h in range(nh):
                hs = slice(h * HEAD_DIM, (h + 1) * HEAD_DIM)
                dws.append(jnp.broadcast_to(jnp.sum(prod[:, hs], axis=1, keepdims=True), (tm, HEAD_DIM)))
        for g, (dl_ref, d) in enumerate(zip((dl0, dl1, dl2), dils)):
            cols = []
            for h in range(nh):
                hs = slice(h * HEAD_DIM, (h + 1) * HEAD_DIM)
                mean = sum(wts[g2][:, hs] * dws[g2 * nh + h] for g2 in range(N_GROUPS))
                cols.append(wts[g][:, hs] * (dws[g * nh + h] - mean))
            _fold_to(dl_ref, jnp.concatenate(cols, axis=1), scr, d)

    folded = [_folded_spec(d, tm, w) for d in dils]
    return pl.pallas_call(
        body, name=name, grid=(t // tm,),
        out_shape=[jax.ShapeDtypeStruct(o.shape, BF16) for o in o_g]
                  + [jax.ShapeDtypeStruct(o.shape, F32) for o in o_g],
        in_specs=[pl.BlockSpec((tm, w), lambda i: (i, 0))] + folded + folded,
        out_specs=folded + folded,
        scratch_shapes=_fold_scratch(tm, w),
        compiler_params=_cp("parallel"),
    )(dyb, *o_g, *l_g)


def attn_bwd(qkv, o, lse, do, dlse, cfg, sinks, name):
    d, n, _ = qkv.shape
    tq, nsub, nqb = _attn_geometry(cfg, n)
    wq, wk, wout = cfg.wq, cfg.wk, cfg.wout
    grp = cfg.heads // cfg.kv_heads
    has_dl = dlse is not None
    scale = HEAD_DIM ** -0.5

    def body(*refs):
        sink_ref, q_ref, qn_ref, kc_ref, kp_ref, vc_ref, vp_ref = refs[:7]
        o_ref, on_ref, do_ref, don_ref, l_ref, ln_ref = refs[7:13]
        rest = refs[13:]
        dl_ref = dln_ref = None
        if has_dl:
            dl_ref, dln_ref = rest[:2]
            rest = rest[2:]
        out_ref = rest[0]
        rest = rest[1:]
        if cfg.sinks:
            dsink_ref = rest[0]
            rest = rest[1:]
        kf, vf, dk_acc, dv_acc = rest
        r, i = pl.program_id(0), pl.program_id(1)
        kf[0:BLOCK, :] = kp_ref[...]
        kf[BLOCK:, :] = kc_ref[...]
        vf[0:BLOCK, :] = vp_ref[...]
        vf[BLOCK:, :] = vc_ref[...]
        dist, valid, valid_first = _band(i, cfg.max_dist)
        distf = dist.astype(F32)
        next_dist = jnp.where(i < nqb - 1, cfg.max_dist, -1)
        valid_next = (dist[:, 0:BLOCK] >= 0) & (dist[:, 0:BLOCK] <= next_dist)
        if cfg.sinks:
            @pl.when((r == 0) & (i == 0))
            def _():
                dsink_ref[...] = jnp.zeros_like(dsink_ref)

        def stats(o_r, do_r, l_r, dl_r, rows, hs):
            do_v = do_r[rows, hs]
            delta = jnp.sum(do_v.astype(F32) * o_r[rows, hs].astype(F32), axis=1, keepdims=True)
            lse_v = jnp.max(l_r[rows, hs], axis=1, keepdims=True)
            shift = -delta
            if has_dl:
                shift = shift + jnp.max(dl_r[rows, hs], axis=1, keepdims=True)
            return do_v, delta, lse_v, shift

        for kv in range(cfg.kv_heads):
            ks = slice(kv * HEAD_DIM, (kv + 1) * HEAD_DIM)
            dk_acc[...] = jnp.zeros_like(dk_acc)
            dv_acc[...] = jnp.zeros_like(dv_acc)
            for g in range(grp):
                h = kv * grp + g
                hs = slice(h * HEAD_DIM, (h + 1) * HEAD_DIM)
                bias = distf * (-(_slope(cfg.head0 + h) * d))
                tiles = []
                for a in range(nsub + 1):
                    if a < nsub:
                        rows = slice(a * BLOCK, (a + 1) * BLOCK)
                        win = slice(a * BLOCK, (a + 2) * BLOCK)
                        q = q_ref[rows, hs]
                        st = stats(o_ref, do_ref, l_ref, dl_ref, rows, hs)
                        mask, bias_a = (valid_first if a == 0 else valid), bias
                    else:
                        win = slice(nsub * BLOCK, (nsub + 1) * BLOCK)
                        q = qn_ref[0:BLOCK, hs]
                        st = stats(on_ref, don_ref, ln_ref, dln_ref, slice(0, BLOCK), hs)
                        mask, bias_a = valid_next, bias[:, 0:BLOCK]
                    k = kf[win, ks]
                    s = _dot_nt(q, k) * scale
                    dp = _dot_nt(st[0], vf[win, ks])
                    tiles.append((q, k, st, s, dp, mask, bias_a))
                grads = []
                for q, k, (do_v, delta, lse_v, shift), s, dp, mask, bias_a in tiles:
                    p = jnp.exp(jnp.where(mask, s + bias_a, NEG_INF) - lse_v)
                    grads.append(((p * (dp + shift)).astype(BF16), p.astype(BF16)))
                for a, ((q, k, (do_v, delta, lse_v, shift), _, _, _, _), (dsb, pb)) in enumerate(zip(tiles, grads)):
                    if a < nsub:
                        out_ref[a * BLOCK:(a + 1) * BLOCK, hs] = (_dot(dsb, k) * scale).astype(BF16)
                    if a == 0:
                        krows = slice(0, BLOCK)
                        dsb, pb = dsb[:, BLOCK:], pb[:, BLOCK:]
                    elif a < nsub:
                        krows = slice((a - 1) * BLOCK, (a + 1) * BLOCK)
                    else:
                        krows = slice((nsub - 1) * BLOCK, nsub * BLOCK)
                    dk_acc[krows, :] += _dot_tn(dsb, q) * scale
                    dv_acc[krows, :] += _dot_tn(pb, do_v)
                    if cfg.sinks and a < nsub:
                        psink = jnp.exp(sink_ref[h] - lse_v)
                        tot = jnp.sum(psink * (-delta), axis=0, keepdims=True)
                        dsink_ref[h:h + 1, :] += jnp.broadcast_to(tot, (1, LANES))
            out_ref[:, wq + kv * HEAD_DIM: wq + (kv + 1) * HEAD_DIM] = dk_acc[...].astype(BF16)
            out_ref[:, wq + wk + kv * HEAD_DIM: wq + wk + (kv + 1) * HEAD_DIM] = dv_acc[...].astype(BF16)

    prev = lambda i: jnp.maximum(i * nsub - 1, 0)
    nxt = lambda i: jnp.minimum((i + 1) * nsub, n // BLOCK - 1)
    cur = lambda w, c: pl.BlockSpec((None, tq, w), lambda r, i: (r, i, c // w))
    prv = lambda w, c: pl.BlockSpec((None, BLOCK, w), lambda r, i: (r, prev(i), c // w))
    o_cur = pl.BlockSpec((None, tq, wq), lambda r, i: (r, i, 0))
    o_nxt = pl.BlockSpec((None, BLOCK, wq), lambda r, i: (r, nxt(i), 0))
    in_specs = [pl.BlockSpec(memory_space=pltpu.SMEM),
                cur(wq, cfg.qc), pl.BlockSpec((None, BLOCK, wq), lambda r, i: (r, nxt(i), cfg.qc // wq)),
                cur(wk, cfg.kc), prv(wk, cfg.kc), cur(wk, cfg.vc), prv(wk, cfg.vc),
                o_cur, o_nxt, o_cur, o_nxt, o_cur, o_nxt]
    args = [sinks, qkv, qkv, qkv, qkv, qkv, qkv, o, o, do, do, lse, lse]
    if has_dl:
        in_specs += [o_cur, o_nxt]
        args += [dlse, dlse]
    out_shape = [jax.ShapeDtypeStruct((d, n, wout), BF16)]
    out_specs = [pl.BlockSpec((None, tq, wout), lambda r, i: (r, i, 0))]
    if cfg.sinks:
        out_shape.append(jax.ShapeDtypeStruct((8, LANES), F32))
        out_specs.append(pl.BlockSpec((8, LANES), lambda r, i: (0, 0)))
    return pl.pallas_call(
        body, name=name, grid=(d, nqb), out_shape=out_shape, in_specs=in_specs, out_specs=out_specs,
        scratch_shapes=[pltpu.VMEM((tq + BLOCK, wk), BF16), pltpu.VMEM((tq + BLOCK, wk), BF16),
                        pltpu.VMEM((tq, HEAD_DIM), F32), pltpu.VMEM((tq, HEAD_DIM), F32)],
        compiler_params=_cp("arbitrary", "arbitrary"),
    )(*args)


def _adamw(g, w, m, v):
    m = ADAM_B1 * m + (1.0 - ADAM_B1) * g
    v = ADAM_B2 * v + (1.0 - ADAM_B2) * (g * g)
    m_hat = m / (1.0 - ADAM_B1 ** ADAM_STEP)
    v_hat = v / (1.0 - ADAM_B2 ** ADAM_STEP)
    delta = -ADAM_LR * (m_hat / (jnp.sqrt(v_hat) + ADAM_EPS) + ADAM_WD * w)
    return delta, m, v


def adam_reduce(parts, w, m, v, name):
    r, c = w.shape
    tr = next(cand for cand in (256, 128, 64, 32, 16, 8) if r % cand == 0) if r > 256 else r

    def body(p_ref, w_ref, m_ref, v_ref, g_ref, d_ref, mo_ref, vo_ref):
        g = p_ref[0].astype(F32)
        for j in range(1, N_DEV):
            g = g + p_ref[j].astype(F32)
        g_ref[...] = g
        d_ref[...], mo_ref[...], vo_ref[...] = _adamw(g, w_ref[...], m_ref[...], v_ref[...])

    row = pl.BlockSpec((tr, c), lambda i: (i, 0))
    return pl.pallas_call(
        body, name=name, grid=(r // tr,),
        out_shape=[jax.ShapeDtypeStruct((r, c), F32)] * 4,
        in_specs=[pl.BlockSpec((N_DEV, tr, c), lambda i: (0, i, 0)), row, row, row],
        out_specs=[row] * 4,
        compiler_params=_cp("parallel"),
    )(parts, w, m, v)


def adam_layers(parts, w, m, v, name):
    nl, r, c = w.shape
    tr = next(cand for cand in (256, 128, 64, 32, 16, 8) if r % cand == 0)
    steps = r // tr

    def body(*refs):
        p_refs = refs[:nl]
        w_ref, m_ref, v_ref, g_ref, d_ref, mo_ref, vo_ref = refs[nl:]
        for k in range(nl):
            @pl.when(pl.program_id(0) == k)
            def _():
                g = p_refs[k][0].astype(F32)
                for j in range(1, N_DEV):
                    g = g + p_refs[k][j].astype(F32)
                g_ref[...] = g
                d_ref[...], mo_ref[...], vo_ref[...] = _adamw(g, w_ref[...], m_ref[...], v_ref[...])

    def part_spec(k):
        return pl.BlockSpec((N_DEV, tr, c), lambda l, i: (0, jnp.clip(i + (l - k) * steps, 0, steps - 1), 0))

    blk = pl.BlockSpec((None, tr, c), lambda l, i: (l, i, 0))
    return pl.pallas_call(
        body, name=name, grid=(nl, steps),
        out_shape=[jax.ShapeDtypeStruct((nl, r, c), F32)] * 4,
        in_specs=[part_spec(k) for k in range(nl)] + [blk, blk, blk],
        out_specs=[blk] * 4,
        compiler_params=_cp("arbitrary", "arbitrary"),
    )(*parts, w, m, v)


def adam_w_ada(sct, dm_loc, w, m, v):
    nl, dm, wc = w.shape
    tr = 512

    def body(s_ref, d_ref, w_ref, m_ref, v_ref, g_ref, dl_ref, mo_ref, vo_ref):
        g = jnp.dot(s_ref[...], d_ref[...], preferred_element_type=F32, precision=lax.Precision.HIGHEST)
        g_ref[...] = g
        dl_ref[...], mo_ref[...], vo_ref[...] = _adamw(g, w_ref[...], m_ref[...], v_ref[...])

    blk = pl.BlockSpec((None, tr, wc), lambda l, i: (l, i, 0))
    return pl.pallas_call(
        body, name="adam_w_ada", grid=(nl, dm // tr),
        out_shape=[jax.ShapeDtypeStruct(w.shape, F32)] * 4,
        in_specs=[pl.BlockSpec((tr, LANES), lambda l, i: (i, 0)),
                  pl.BlockSpec((None, LANES, wc), lambda l, i: (l, 0, 0)), blk, blk, blk],
        out_specs=[blk] * 4,
        compiler_params=_cp("parallel", "parallel"),
    )(sct, dm_loc, w, m, v)


TRANSPOSED = ("w_gate", "w_up")


def _pieces(dm):
    ncol = lambda n: n // N_DEV
    mixer = ([Piece("w_in", "w_in", 1, 0, ncol(GATE_COL + 2 * dm)),
              Piece("w_a", "w_a", 1, 0, ncol(dm)),
              Piece("w_b", "w_b", 1, 0, ncol(dm)),
              Piece("w_o", "w_o", 0, 0, ncol(dm))],
             {"w_in": (dm, GATE_COL + 2 * dm), "w_a": (A_Q_HEADS * HEAD_DIM, dm), "w_b": (B_OUT_W, dm),
              "w_o": (dm, dm)})
    ffn = ([Piece("w_gate", "w_ffn_t", 0, 0, ncol(D_FF)),
            Piece("w_up", "w_ffn_t", 0, D_FF, ncol(D_FF)),
            Piece("w_down", "w_down", 0, 0, ncol(D_FF))],
           {"w_ffn_t": (2 * D_FF, dm), "w_down": (D_FF, dm)})
    return mixer, ffn


def kernel(x, c, w_ada, b_ada, w_in, sinks, w_a, w_b, w_o, ln1_g, ln1_b, w_gate, w_up, w_down, ln2_g, ln2_b, loss_target, m_w_ada, m_b_ada, m_w_in, m_sinks, m_w_a, m_w_b, m_w_o, m_ln1_g, m_ln1_b, m_w_gate, m_w_up, m_w_down, m_ln2_g, m_ln2_b, v_w_ada, v_b_ada, v_w_in, v_sinks, v_w_a, v_w_b, v_w_o, v_ln1_g, v_ln1_b, v_w_gate, v_w_up, v_w_down, v_ln2_g, v_ln2_b):
    given = dict(locals())
    nl = w_in.shape[0]
    t, dm = x.shape[1], x.shape[2]
    me = 4 * lax.axis_index("x") + 2 * lax.axis_index("y") + lax.axis_index("c")
    x0 = x.reshape(t, dm)
    target = loss_target.reshape(t, dm)

    groups = dict(zip(("mixer", "ffn"), _pieces(dm)))
    local = lambda nm, pre="": (given[pre + nm].transpose(0, 2, 1) if nm in TRANSPOSED else given[pre + nm])
    shards = {pc.name: local(pc.name).astype(BF16) for pcs, _ in groups.values() for pc in pcs}

    def gather(group, l):
        pcs, bufs = groups[group]
        return Exchange("gather", pcs, [shards[pc.name][l] for pc in pcs], bufs.values(), bufs)

    def scatter(group, gbuf):
        pcs, bufs = groups[group]
        return Exchange("scatter", pcs, [gbuf[nm] for nm in bufs],
                        [(N_DEV,) + shards[pc.name].shape[1:] for pc in pcs], bufs)

    full = [dict() for _ in range(nl)]
    full[0].update(zip(groups["mixer"][1], run_exchange(gather("mixer", 0), "gather_mixer")))

    wc = w_ada.shape[2]
    c_all = all_gather_small(jnp.broadcast_to(c, (8, dm)), "gather_c")[:, 0, :]
    b_loc = lax.dynamic_slice_in_dim(b_ada, me * wc, wc, axis=1).reshape(nl, 1, wc)
    mp, sc_all = mod_partial(c_all, w_ada, b_loc)
    mp_all = all_gather_small(mp.reshape(nl * N_DEV, wc), "gather_mod").reshape(N_DEV, nl, N_DEV, wc)
    mod = lax.dynamic_index_in_dim(mp_all, me, axis=2, keepdims=False)
    mod = mod.transpose(1, 0, 2).reshape(nl, 6, 1, dm)

    vec = lambda a, l: a[l].reshape(1, dm)

    saved = []
    xl = x0
    for l in range(nl):
        sh1, s1, g1, sh2, s2, g2 = [mod[l, j] for j in range(6)]
        w = full[l]
        (u1, u1_f4, u1_f16, proj, qkv_f4, qkv_f16), got = in_proj(xl, s1, sh1, w["w_in"], "in_proj",
                                                                   gather("ffn", l))
        w.update(zip(groups["ffn"][1], got))
        proj3 = proj.reshape(1, t, proj.shape[1])
        qkv_b = [proj3, qkv_f4, qkv_f16]
        ya, lse_a = attn_fwd(proj3, ATTN_A, sinks[l], "attn_a_fwd")
        o_g, l_g = [], []
        for g, cfg in enumerate(ATTN_B):
            o, ls = attn_fwd(qkv_b[g], cfg, sinks[l], "attn_b%d_fwd" % g)
            o_g.append(o)
            l_g.append(ls)
        yb, merged = mix_merge(ya[0], o_g, l_g, proj, w["w_a"], w["w_b"], "mix_merge")
        y1, x1, zh1, rs1 = proj_ln(merged, w["w_o"], xl, g1, vec(ln1_g, l), vec(ln1_b, l), "out_proj_ln")
        (u2, ab), got = modmm(x1, s2, sh2, w["w_ffn_t"].T, "ffn_up", gather("mixer", l + 1) if l + 1 < nl else None)
        if l + 1 < nl:
            full[l + 1].update(zip(groups["mixer"][1], got))
        h, y2, x2, zh2, rs2 = swiglu_proj_ln(ab, w["w_down"], x1, g2, vec(ln2_g, l), vec(ln2_b, l), "ffn_down_ln")
        saved.append(dict(xin=xl, u1=[u1, u1_f4.reshape(t, dm), u1_f16.reshape(t, dm)], proj=proj, qkv_b=qkv_b,
                          ya=ya, lse_a=lse_a, o_g=o_g, l_g=l_g, yb=yb, merged=merged,
                          y1=y1, x1=x1, zh1=zh1, rs1=rs1, u2=u2, ab=ab, h=h, y2=y2, zh2=zh2, rs2=rs2))
        xl = x2

    dx, loss_part = loss_head(xl, target)

    small = {k: [None] * nl for k in ("dmod", "ln1_g", "ln1_b", "ln2_g", "ln2_b", "sinks")}
    recv = {nm: [None] * nl for grp in groups.values() for nm in (pc.name for pc in grp[0])}
    pending = None

    def keep(group, l, got):
        for pc, arr in zip(groups[group][0], got):
            recv[pc.name][l] = arr

    for l in reversed(range(nl)):
        sv, w = saved[l], full[l]
        sh1, s1, g1, sh2, s2, g2 = [mod[l, j] for j in range(6)]
        fresh = lambda nm: lax.empty({**groups["mixer"][1], **groups["ffn"][1]}[nm], BF16)
        gbuf = {}
        dz2, dy2, sg, sb, sgate2 = ln_bwd(dx, sv["zh2"], sv["rs2"], sv["y2"], vec(ln2_g, l), g2, "ln_bwd")
        small["ln2_g"][l], small["ln2_b"][l] = sg[0], sb[0]
        gbuf["w_down"] = wgrad(sv["h"], dy2, fresh("w_down"), 512, dm // 512, 0, 0, 1, "wgrad_down")
        (dab,), got = dswiglu(dy2, w["w_down"].T, sv["ab"], "dswiglu", pending)
        if pending is not None:
            keep("mixer", l + 1, got)
        gbuf["w_ffn_t"] = wgrad(dab, sv["u2"], fresh("w_ffn_t"), 512, dm // 512, 0, 0, 1, "wgrad_ffn_up")
        (dx1, ss2, ssh2), got = dgrad_ffn(dab, w["w_ffn_t"], dz2, sv["x1"], s2, "dgrad_ffn", scatter("ffn", gbuf))
        keep("ffn", l, got)
        dz1, do1, sg, sb, sgate1 = ln_bwd(dx1, sv["zh1"], sv["rs1"], sv["y1"], vec(ln1_g, l), g1, "ln_bwd")
        small["ln1_g"][l], small["ln1_b"][l] = sg[0], sb[0]
        gbuf["w_o"] = wgrad(sv["merged"], do1, fresh("w_o"), 512, dm // 512, 0, 0, 1, "wgrad_o")
        dpab, dya, dyb, dgab = dmerge(do1, w["w_o"].T, sv["ya"][0], sv["yb"], w["w_a"], w["w_b"],
                                      w["w_a"].T, w["w_b"].T, sv["proj"], "dmerge")
        gbuf["w_a"] = wgrad(sv["ya"][0], dpab, fresh("w_a"), 512, dm // 512, 0, 0, 1, "wgrad_a")
        gbuf["w_b"] = wgrad(sv["yb"], dpab, fresh("w_b"), 512, dm // 512, dm // 512, 0, 1, "wgrad_b")
        mixed = mix_bwd(dyb, sv["o_g"], sv["l_g"], "mix_bwd")
        do_g, dl_g = mixed[:N_GROUPS], mixed[N_GROUPS:]
        d_a, dsink = attn_bwd(sv["qkv_b"][0], sv["ya"], sv["lse_a"], dya.reshape(1, t, -1), None, ATTN_A,
                              sinks[l], "attn_a_bwd")
        small["sinks"][l] = dsink[:, 0]
        d_b = [attn_bwd(sv["qkv_b"][g], sv["o_g"][g], sv["l_g"][g], do_g[g], dl_g[g], cfg, sinks[l],
                        "attn_b%d_bwd" % g)[0] for g, cfg in enumerate(ATTN_B)]
        dx, ss1, ssh1 = dgrad_in(d_a, d_b, dgab, w["w_in"].T, dz1, sv["xin"], s1, "dgrad_in")
        gw = wgrad(sv["u1"][0], d_a.reshape(t, A_W), fresh("w_in"), A_W, 1, 0, 0, 1, "wgrad_in_a")
        for g in range(N_GROUPS):
            gw = wgrad(sv["u1"][g], d_b[g].reshape(t, B_GW), gw, B_OUT_W, 3, 0, A_W // B_OUT_W + g, N_GROUPS,
                       "wgrad_in_b%d" % g)
        gbuf["w_in"] = wgrad(sv["u1"][0], dgab, gw, 512, 2 * dm // 512, 0, GATE_COL // 512, 1, "wgrad_in_gate")
        pending = scatter("mixer", gbuf)
        small["dmod"][l] = jnp.stack([ssh1[0], ss1[0], sgate1[0], ssh2[0], ss2[0], sgate2[0]])
    keep("mixer", 0, run_exchange(pending, "scatter_mixer"))
    grad_x = dx.reshape(x.shape)

    big_out = {}
    for nm, parts in recv.items():
        outs = adam_layers(parts, local(nm), local(nm, "m_"), local(nm, "v_"), "adam_" + nm)
        big_out[nm] = [o.transpose(0, 2, 1) for o in outs] if nm in TRANSPOSED else outs

    rows = jnp.concatenate(
        [jnp.stack(small["dmod"]).reshape(nl * 6, dm)]
        + [jnp.stack(small[k]) for k in ("ln1_g", "ln1_b", "ln2_g", "ln2_b")]
        + [jnp.pad(jnp.stack(small["sinks"]).reshape(1, -1), ((0, 0), (0, dm - nl * A_Q_HEADS))),
           jnp.broadcast_to(loss_part[0:1, 0:1], (1, dm))])
    n_rows = rows.shape[0]
    rows = jnp.pad(rows, ((0, -n_rows % 8), (0, 0)))
    rows_all = all_gather_small(rows, "gather_small_grads")

    def pack_small(pre):
        parts = [given[pre + "b_ada"].reshape(nl * 6, dm)]
        parts += [given[pre + k] for k in ("ln1_g", "ln1_b", "ln2_g", "ln2_b")]
        parts.append(jnp.pad(given[pre + "sinks"].reshape(1, -1), ((0, 0), (0, dm - nl * A_Q_HEADS))))
        p = jnp.concatenate(parts)
        return jnp.pad(p, ((0, rows.shape[0] - p.shape[0]), (0, 0)))

    souts = adam_reduce(rows_all, pack_small(""), pack_small("m_"), pack_small("v_"), "adam_small")

    def unpack_small(o):
        r = {"b_ada": o[0:nl * 6].reshape(nl, 6 * dm)}
        for j, k in enumerate(("ln1_g", "ln1_b", "ln2_g", "ln2_b")):
            r[k] = o[nl * 6 + j * nl: nl * 6 + (j + 1) * nl]
        r["sinks"] = o[nl * 10, 0:nl * A_Q_HEADS].reshape(nl, A_Q_HEADS)
        return r

    small_out = [unpack_small(o) for o in souts]
    loss = souts[0][nl * 10 + 1, 0]

    dmod_all = rows_all[:, 0:nl * 6].reshape(N_DEV, nl, 6 * dm)
    dm_loc = lax.dynamic_slice_in_dim(dmod_all, me * wc, wc, axis=2).transpose(1, 0, 2)
    dm_loc = jnp.pad(dm_loc, ((0, 0), (0, LANES - N_DEV), (0, 0)))
    sct = jnp.pad(sc_all.T, ((0, 0), (0, LANES - N_DEV)))
    ada_out = adam_w_ada(sct, dm_loc, w_ada, m_w_ada, v_w_ada)

    names = ["w_ada", "b_ada", "w_in", "sinks", "w_a", "w_b", "w_o", "ln1_g", "ln1_b",
             "w_gate", "w_up", "w_down", "ln2_g", "ln2_b"]

    def pick(kind, nm):
        if nm == "w_ada":
            return ada_out[kind]
        if nm in small_out[kind]:
            return small_out[kind][nm]
        return big_out[nm][kind]

    result = [loss, grad_x]
    for kind in range(4):
        result += [pick(kind, nm) for nm in names]
    return tuple(result)
```

```python
import functools

import jax
import jax.numpy as jnp
from jax import lax
from jax.experimental import pallas as pl
from jax.experimental.pallas import tpu as pltpu

F32 = jnp.float32
BF16 = jnp.bfloat16

D_MODEL = 1024
HEAD_DIM = 64
A_Q_HEADS = 8
A_KV_HEADS = 2
A_WINDOW = 128
B_GROUPS = ((128, 1), (512, 4), (2048, 16))
N_GROUPS = len(B_GROUPS)
B_HEADS_PER_GROUP = 4
N_ATTN_HEADS = A_Q_HEADS + B_HEADS_PER_GROUP * N_GROUPS
BLOCK = 128
A_W = (A_Q_HEADS + 2 * A_KV_HEADS) * HEAD_DIM
B_OUT_W = B_HEADS_PER_GROUP * HEAD_DIM
B_GW = 3 * B_OUT_W
B_ALL = N_GROUPS * B_OUT_W
GATE_COL = A_W + 3 * B_ALL
D_FF = 2816
DN_ALPHA = 8.0 ** 0.25
LN_EPS = 1e-5
NEG_INF = -1e30
ADAM_LR, ADAM_B1, ADAM_B2, ADAM_EPS, ADAM_WD, ADAM_STEP = 0.001, 0.9, 0.999, 1e-08, 0.01, 10

N_DEV = 8
MESH = pl.DeviceIdType.MESH
VMEM_LIMIT = 56 * 1024 * 1024
ROW_TILE = 512
WGRAD_TILE_ELEMS = 2 * 1024 * 1024
LANES = 128
BF16_ROWS = 16


def _cp(*sem):
    return pltpu.CompilerParams(dimension_semantics=sem, vmem_limit_bytes=VMEM_LIMIT)


def _row_tile(t):
    return min(ROW_TILE, t)


def _slope(head):
    return 2.0 ** (-8.0 * (head + 1) / N_ATTN_HEADS)


def _sigmoid(x):
    return 1.0 / (1.0 + jnp.exp(-x))


def _dot(a, b):
    return jnp.dot(a, b, preferred_element_type=F32)


def _dot_nt(a, b):
    return lax.dot_general(a, b, (((1,), (1,)), ((), ())), preferred_element_type=F32)


def _dot_tn(a, b):
    return lax.dot_general(a, b, (((0,), (0,)), ((), ())), preferred_element_type=F32)


def _fold_scratch(tm, w):
    return [pltpu.VMEM((tm, LANES), F32)] * (w // LANES)


def _fold_to(dst_ref, val, scrs, d, col0=0):
    tm, w = val.shape
    if d == 1:
        dst_ref[0, :, col0:col0 + w] = val.astype(dst_ref.dtype)
        return
    for cb in range(w // LANES):
        scrs[cb][...] = val[:, cb * LANES:(cb + 1) * LANES]
    for r in range(d):
        for cb in range(w // LANES):
            piece = scrs[cb][pl.ds(r, tm // d, stride=d), :]
            dst_ref[r, :, col0 + cb * LANES:col0 + (cb + 1) * LANES] = piece.astype(dst_ref.dtype)


def _unfold_rows(rows_of, scrs, d, n, w):
    for r in range(d):
        for cb in range(w // LANES):
            scrs[cb][pl.ds(r, n, stride=d), :] = rows_of(r, slice(cb * LANES, (cb + 1) * LANES)).astype(F32)
    return jnp.concatenate([scrs[cb][...] for cb in range(w // LANES)], axis=1)


def _unfold_from(src_ref, scrs, d):
    if d == 1:
        return src_ref[0].astype(F32)
    _, n, w = src_ref.shape
    return _unfold_rows(lambda r, cols: src_ref[r, :, cols], scrs, d, n, w)


def _folded_spec(d, tm, w):
    return pl.BlockSpec((d, tm // d, w), lambda i: (0, i, 0))


def _me():
    return lax.axis_index("x"), lax.axis_index("y"), lax.axis_index("c")


def _flip(v, bit):
    return 1 - v if bit else v


def _peer(k):
    x, y, c = _me()
    return (_flip(x, k & 4), _flip(y, k & 2), _flip(c, k & 1))


def _peer_index(k):
    px, py, pc = _peer(k)
    return 4 * px + 2 * py + pc


def all_gather_small(v, name):
    r, c = v.shape

    def body(v_ref, out_ref, send_sems, recv_sems):
        me = _peer_index(0)
        out_ref[me] = v_ref[...]
        copies = []
        for k in range(1, N_DEV):
            cp = pltpu.make_async_remote_copy(
                src_ref=v_ref, dst_ref=out_ref.at[me],
                send_sem=send_sems.at[k - 1], recv_sem=recv_sems.at[k - 1],
                device_id=_peer(k), device_id_type=MESH)
            cp.start()
            copies.append(cp)
        for k in range(1, N_DEV):
            pltpu.make_async_remote_copy(
                src_ref=v_ref, dst_ref=out_ref.at[_peer_index(k)],
                send_sem=send_sems.at[k - 1], recv_sem=recv_sems.at[k - 1],
                device_id=_peer(k), device_id_type=MESH).wait_recv()
        for cp in copies:
            cp.wait_send()

    return pl.pallas_call(
        body, name=name,
        out_shape=jax.ShapeDtypeStruct((N_DEV, r, c), v.dtype),
        in_specs=[pl.BlockSpec(memory_space=pltpu.VMEM)],
        out_specs=pl.BlockSpec(memory_space=pltpu.VMEM),
        scratch_shapes=[pltpu.SemaphoreType.DMA((N_DEV - 1,)), pltpu.SemaphoreType.DMA((N_DEV - 1,))],
        compiler_params=pltpu.CompilerParams(vmem_limit_bytes=VMEM_LIMIT),
    )(v)


class Piece:
    def __init__(self, name, buf, axis, base, size):
        self.name, self.buf, self.axis, self.base, self.size = name, buf, axis, base, size

    def window(self, ref, j):
        start = self.base + j * self.size
        if self.axis == 1:
            return ref.at[:, pl.ds(pl.multiple_of(start, LANES), self.size)]
        return ref.at[pl.ds(pl.multiple_of(start, BF16_ROWS), self.size), :]


class Exchange:
    def __init__(self, kind, pieces, ins, out_shapes, bufs):
        self.kind, self.pieces, self.ins, self.out_shapes = kind, pieces, list(ins), list(out_shapes)
        self.buf_of = {nm: i for i, nm in enumerate(bufs)}
        self.n_in, self.n_out = len(self.ins), len(self.out_shapes)
        n = len(pieces)
        self.scratch = [pltpu.SemaphoreType.DMA((n, N_DEV - 1)), pltpu.SemaphoreType.DMA((n, N_DEV - 1)),
                        pltpu.SemaphoreType.DMA((n,))]
        self.in_specs = [pl.BlockSpec(memory_space=pl.ANY)] * self.n_in
        self.out_specs = [pl.BlockSpec(memory_space=pl.ANY)] * self.n_out
        self.out_shape = [jax.ShapeDtypeStruct(s, BF16) for s in self.out_shapes]

    def _ends(self, pi, ins, outs, to):
        pc = self.pieces[pi]
        if self.kind == "gather":
            return ins[pi], pc.window(outs[self.buf_of[pc.buf]], _peer_index(0))
        return pc.window(ins[self.buf_of[pc.buf]], to), outs[pi].at[_peer_index(0)]

    def _landing(self, pi, outs, frm):
        pc = self.pieces[pi]
        if self.kind == "gather":
            return pc.window(outs[self.buf_of[pc.buf]], frm)
        return outs[pi].at[frm]

    def _remote(self, pi, k, src, dst, sems):
        return pltpu.make_async_remote_copy(
            src_ref=src, dst_ref=dst, send_sem=sems[0].at[pi, k - 1], recv_sem=sems[1].at[pi, k - 1],
            device_id=_peer(k), device_id_type=MESH)

    def _local(self, pi, ins, outs, sems):
        return pltpu.make_async_copy(*self._ends(pi, ins, outs, _peer_index(0)), sems[2].at[pi])

    def start(self, ins, outs, sems):
        for pi in range(len(self.pieces)):
            self._local(pi, ins, outs, sems).start()
            for k in range(1, N_DEV):
                self._remote(pi, k, *self._ends(pi, ins, outs, _peer_index(k)), sems).start()

    def finish(self, ins, outs, sems):
        for pi in range(len(self.pieces)):
            src_like = self._ends(pi, ins, outs, _peer_index(0))[0]
            for k in range(1, N_DEV):
                self._remote(pi, k, src_like, self._landing(pi, outs, _peer_index(k)), sems).wait_recv()
        for pi in range(len(self.pieces)):
            for k in range(1, N_DEV):
                self._remote(pi, k, *self._ends(pi, ins, outs, _peer_index(k)), sems).wait_send()
            self._local(pi, ins, outs, sems).wait()


def _hosted(ex, refs, n_in, n_out, first, last):
    if ex is None:
        return refs
    ins, rest = refs[:n_in], refs[n_in:]
    ex_ins, rest = rest[:ex.n_in], rest[ex.n_in:]
    outs, rest = rest[:n_out], rest[n_out:]
    ex_outs, rest = rest[:ex.n_out], rest[ex.n_out:]
    scr, sems = rest[:len(rest) - 3], rest[len(rest) - 3:]
    pl.when(first)(lambda: ex.start(ex_ins, ex_outs, sems))
    pl.when(last)(lambda: ex.finish(ex_ins, ex_outs, sems))
    return tuple(ins) + tuple(outs) + tuple(scr)


def _host_call(body, ex, *, name, grid, out_shape, in_specs, out_specs, scratch_shapes=(), sem=None, args):
    n_out = len(out_shape)
    if ex is not None:
        out_shape = list(out_shape) + ex.out_shape
        in_specs = list(in_specs) + ex.in_specs
        out_specs = list(out_specs) + ex.out_specs
        scratch_shapes = list(scratch_shapes) + ex.scratch
        args = list(args) + ex.ins
    res = pl.pallas_call(body, name=name, grid=grid, out_shape=out_shape, in_specs=in_specs, out_specs=out_specs,
                         scratch_shapes=scratch_shapes, compiler_params=_cp(*sem))(*args)
    return res[:n_out], res[n_out:]


def run_exchange(ex, name):
    def body(*refs):
        ins, outs, sems = refs[:ex.n_in], refs[ex.n_in:ex.n_in + ex.n_out], refs[ex.n_in + ex.n_out:]
        ex.start(ins, outs, sems)
        ex.finish(ins, outs, sems)

    return pl.pallas_call(body, name=name, out_shape=ex.out_shape, in_specs=ex.in_specs, out_specs=ex.out_specs,
                          scratch_shapes=ex.scratch)(*ex.ins)


def mod_partial(c_all, w_ada, b_loc):
    nl, dm, wc = w_ada.shape

    def body(c_ref, w_ref, b_ref, o_ref, sc_ref):
        cc = c_ref[...]
        sc = cc * _sigmoid(cc)
        sc_ref[...] = sc
        o_ref[...] = jnp.dot(sc, w_ref[...], preferred_element_type=F32,
                             precision=lax.Precision.HIGHEST) + b_ref[...]

    return pl.pallas_call(
        body, name="mod_partial", grid=(nl,),
        out_shape=[jax.ShapeDtypeStruct((nl, N_DEV, wc), F32), jax.ShapeDtypeStruct((N_DEV, dm), F32)],
        in_specs=[pl.BlockSpec((N_DEV, dm), lambda l: (0, 0)),
                  pl.BlockSpec((None, dm, wc), lambda l: (l, 0, 0)),
                  pl.BlockSpec((None, 1, wc), lambda l: (l, 0, 0))],
        out_specs=[pl.BlockSpec((None, N_DEV, wc), lambda l: (l, 0, 0)),
                   pl.BlockSpec((N_DEV, dm), lambda l: (0, 0))],
        compiler_params=_cp("arbitrary"),
    )(c_all, w_ada, b_loc)


def in_proj(x, s, sh, w, name, ex=None):
    t, dm = x.shape
    n = w.shape[1]
    tm = _row_tile(t)
    nsteps = t // tm
    ch = B_OUT_W
    dils = [dil for _, dil in B_GROUPS if dil > 1]

    def body(*refs):
        i = pl.program_id(0)
        x_ref, s_ref, sh_ref, w_ref, u_ref, *rest = _hosted(ex, refs, 4, 2 + 2 * len(dils), i == 0, i == nsteps - 1)
        uf_refs, o_ref, qf_refs = rest[:len(dils)], rest[len(dils)], rest[len(dils) + 1:len(dils) * 2 + 1]
        scrs = rest[len(dils) * 2 + 1:]
        uf = x_ref[...] * (1.0 + s_ref[...]) + sh_ref[...]
        u = uf.astype(BF16)
        u_ref[...] = u
        for d, uf_ref in zip(dils, uf_refs):
            _fold_to(uf_ref, uf, scrs, d)
        for c0 in range(0, n, ch):
            res = _dot(u, w_ref[:, c0:c0 + ch])
            o_ref[:, c0:c0 + ch] = res.astype(BF16)
            if A_W <= c0 < GATE_COL:
                part, g = divmod((c0 - A_W) // ch, N_GROUPS)
                d = B_GROUPS[g][1]
                if d > 1:
                    _fold_to(qf_refs[dils.index(d)], res, scrs, d, part * ch)

    vec = pl.BlockSpec((1, dm), lambda i: (0, 0))
    row = lambda w_: pl.BlockSpec((tm, w_), lambda i: (i, 0))
    return _host_call(
        body, ex, name=name, grid=(nsteps,),
        out_shape=[jax.ShapeDtypeStruct((t, dm), BF16)]
                  + [jax.ShapeDtypeStruct((d, t // d, dm), BF16) for d in dils]
                  + [jax.ShapeDtypeStruct((t, n), BF16)]
                  + [jax.ShapeDtypeStruct((d, t // d, B_GW), BF16) for d in dils],
        in_specs=[row(dm), vec, vec, pl.BlockSpec((dm, n), lambda i: (0, 0))],
        out_specs=[row(dm)] + [_folded_spec(d, tm, dm) for d in dils] + [row(n)]
                  + [_folded_spec(d, tm, B_GW) for d in dils],
        scratch_shapes=_fold_scratch(tm, dm), sem=("arbitrary",), args=[x, s, sh, w])


def modmm(x, s, sh, w, name, ex=None):
    t, dm = x.shape
    n = w.shape[1]
    tm = _row_tile(t)
    nsteps = t // tm
    ch = 512

    def body(*refs):
        i = pl.program_id(0)
        x_ref, s_ref, sh_ref, w_ref, u_ref, o_ref = _hosted(ex, refs, 4, 2, i == 0, i == nsteps - 1)
        u = (x_ref[...] * (1.0 + s_ref[...]) + sh_ref[...]).astype(BF16)
        u_ref[...] = u
        for c0 in range(0, n, ch):
            o_ref[:, c0:c0 + ch] = _dot(u, w_ref[:, c0:c0 + ch]).astype(BF16)

    vec = pl.BlockSpec((1, dm), lambda i: (0, 0))
    return _host_call(
        body, ex, name=name, grid=(nsteps,),
        out_shape=[jax.ShapeDtypeStruct((t, dm), BF16), jax.ShapeDtypeStruct((t, n), BF16)],
        in_specs=[pl.BlockSpec((tm, dm), lambda i: (i, 0)), vec, vec,
                  pl.BlockSpec((dm, n), lambda i: (0, 0))],
        out_specs=[pl.BlockSpec((tm, dm), lambda i: (i, 0)), pl.BlockSpec((tm, n), lambda i: (i, 0))],
        sem=("arbitrary",), args=[x, s, sh, w])


def _ln_store(y, xres_ref, g_ref, lg_ref, lb_ref, y_ref, xo_ref, zh_ref, rs_ref):
    y_ref[...] = y.astype(BF16)
    z = DN_ALPHA * xres_ref[...] + g_ref[...] * y
    mu = jnp.mean(z, axis=1, keepdims=True)
    zc = z - mu
    var = jnp.mean(zc * zc, axis=1, keepdims=True)
    rstd = lax.rsqrt(var + LN_EPS)
    zhat = zc * rstd
    zh_ref[...] = zhat
    xo_ref[...] = zhat * lg_ref[...] + lb_ref[...]
    rs_ref[...] = jnp.broadcast_to(rstd, rs_ref.shape)


def _ln_out_shapes(t, dm):
    return [jax.ShapeDtypeStruct((t, dm), BF16), jax.ShapeDtypeStruct((t, dm), F32),
            jax.ShapeDtypeStruct((t, dm), F32), jax.ShapeDtypeStruct((t, LANES), F32)]


def _ln_out_specs(tm, dm):
    row = pl.BlockSpec((tm, dm), lambda i: (i, 0))
    return [row, row, row, pl.BlockSpec((tm, LANES), lambda i: (i, 0))]


def proj_ln(a, w, xres, gate, lg, lb, name):
    t, k = a.shape
    dm = w.shape[1]
    tm = _row_tile(t)

    def body(a_ref, w_ref, xres_ref, g_ref, lg_ref, lb_ref, y_ref, xo_ref, zh_ref, rs_ref):
        y = _dot(a_ref[...], w_ref[...])
        _ln_store(y, xres_ref, g_ref, lg_ref, lb_ref, y_ref, xo_ref, zh_ref, rs_ref)

    vec = pl.BlockSpec((1, dm), lambda i: (0, 0))
    return pl.pallas_call(
        body, name=name, grid=(t // tm,),
        out_shape=_ln_out_shapes(t, dm),
        in_specs=[pl.BlockSpec((tm, k), lambda i: (i, 0)), pl.BlockSpec((k, dm), lambda i: (0, 0)),
                  pl.BlockSpec((tm, dm), lambda i: (i, 0)), vec, vec, vec],
        out_specs=_ln_out_specs(tm, dm),
        compiler_params=_cp("parallel"),
    )(a, w, xres, gate, lg, lb)


def swiglu_proj_ln(ab, w, xres, gate, lg, lb, name):
    t = ab.shape[0]
    f, dm = w.shape
    tm = _row_tile(t)

    def body(a_ref, b_ref, w_ref, xres_ref, g_ref, lg_ref, lb_ref, h_ref, y_ref, xo_ref, zh_ref, rs_ref):
        a = a_ref[...].astype(F32)
        h = (a * _sigmoid(a) * b_ref[...].astype(F32)).astype(BF16)
        h_ref[...] = h
        y = _dot(h, w_ref[...])
        _ln_store(y, xres_ref, g_ref, lg_ref, lb_ref, y_ref, xo_ref, zh_ref, rs_ref)

    vec = pl.BlockSpec((1, dm), lambda i: (0, 0))
    return pl.pallas_call(
        body, name=name, grid=(t // tm,),
        out_shape=[jax.ShapeDtypeStruct((t, f), BF16)] + _ln_out_shapes(t, dm),
        in_specs=[pl.BlockSpec((tm, f), lambda i: (i, 0)), pl.BlockSpec((tm, f), lambda i: (i, 1)),
                  pl.BlockSpec((f, dm), lambda i: (0, 0)),
                  pl.BlockSpec((tm, dm), lambda i: (i, 0)), vec, vec, vec],
        out_specs=[pl.BlockSpec((tm, f), lambda i: (i, 0))] + _ln_out_specs(tm, dm),
        compiler_params=_cp("parallel"),
    )(ab, ab, w, xres, gate, lg, lb)


class AttnCfg:
    def __init__(self, dil, heads, kv_heads, qc, kc, vc, max_dist, head0, sinks):
        self.dil, self.heads, self.kv_heads = dil, heads, kv_heads
        self.qc, self.kc, self.vc = qc, kc, vc
        self.max_dist, self.head0, self.sinks = max_dist, head0, sinks
        self.wq = heads * HEAD_DIM
        self.wk = kv_heads * HEAD_DIM
        self.wout = self.wq + 2 * self.wk


ATTN_A = AttnCfg(1, A_Q_HEADS, A_KV_HEADS, 0, A_Q_HEADS * HEAD_DIM, (A_Q_HEADS + A_KV_HEADS) * HEAD_DIM,
                 A_WINDOW - 1, 0, True)


def _attn_b_cfg(g):
    win, dil = B_GROUPS[g]
    cols = ((A_W + g * B_OUT_W, A_W + B_ALL + g * B_OUT_W, A_W + 2 * B_ALL + g * B_OUT_W) if dil == 1
            else (0, B_OUT_W, 2 * B_OUT_W))
    return AttnCfg(dil, B_HEADS_PER_GROUP, B_HEADS_PER_GROUP, *cols, win // dil,
                   A_Q_HEADS + g * B_HEADS_PER_GROUP, False)


ATTN_B = [_attn_b_cfg(g) for g in range(N_GROUPS)]


SCALE = HEAD_DIM ** -0.5


def _head(h):
    return slice(h * HEAD_DIM, (h + 1) * HEAD_DIM)


def _stack(parts):
    return parts[0] if len(parts) == 1 else jnp.concatenate(parts, axis=0)


def _masked_bias(mask, distf, cfg, h, d):
    return jnp.where(mask, distf * (-(_slope(cfg.head0 + h) * d)), NEG_INF)


def _band(i, max_dist):
    qi = lax.broadcasted_iota(jnp.int32, (BLOCK, 2 * BLOCK), 0)
    sj = lax.broadcasted_iota(jnp.int32, (BLOCK, 2 * BLOCK), 1)
    dist = qi + BLOCK - sj
    valid = (dist >= 0) & (dist <= max_dist)
    first_key = jnp.where(i > 0, 0, BLOCK)
    valid_first = valid & (sj >= first_key)
    return dist, valid, valid_first


def _attn_geometry(cfg, n):
    tq = min(512, n)
    return tq, tq // BLOCK, n // tq


def attn_fwd(qkv, cfg, sinks, name):
    d, n, _ = qkv.shape
    tq, nsub, nqb = _attn_geometry(cfg, n)
    wq, wk = cfg.wq, cfg.wk
    grp = cfg.heads // cfg.kv_heads

    def body(sink_ref, q_ref, kc_ref, kp_ref, vc_ref, vp_ref, o_ref, l_ref, kf, vf):
        i = pl.program_id(1)
        kf[0:BLOCK, :] = kp_ref[...]
        kf[BLOCK:, :] = kc_ref[...]
        vf[0:BLOCK, :] = vp_ref[...]
        vf[BLOCK:, :] = vc_ref[...]
        dist, valid, valid_first = _band(i, cfg.max_dist)
        distf = dist.astype(F32)
        rows = [slice(a * BLOCK, (a + 1) * BLOCK) for a in range(nsub)]
        wins = [slice(a * BLOCK, (a + 2) * BLOCK) for a in range(nsub)]
        for head in range(cfg.heads):
            heads = [head]
            ks = _head(head // grp)
            b_reg = _stack([_masked_bias(valid, distf, cfg, h, d) for h in heads])
            b_first = _stack([_masked_bias(valid_first, distf, cfg, h, d) for h in heads])
            if cfg.sinks:
                sink = _stack([jnp.full((BLOCK, 1), sink_ref[h], F32) for h in heads])
            ss = [_dot_nt(_stack([q_ref[rows[a], _head(h)] for h in heads]) * SCALE, kf[wins[a], ks])
                  + (b_first if a == 0 else b_reg) for a in range(nsub)]
            es, invs = [], []
            for a in range(nsub):
                m = jnp.max(ss[a], axis=1, keepdims=True)
                if cfg.sinks:
                    m = jnp.maximum(m, sink)
                e = jnp.exp(ss[a] - m)
                den = jnp.sum(e, axis=1, keepdims=True)
                if cfg.sinks:
                    den = den + jnp.exp(sink - m)
                es.append(e.astype(BF16))
                invs.append(1.0 / den)
                lse = m + jnp.log(den)
                for g, h in enumerate(heads):
                    l_ref[rows[a], _head(h)] = jnp.broadcast_to(lse[g * BLOCK:(g + 1) * BLOCK], (BLOCK, HEAD_DIM))
            for a in range(nsub):
                o = _dot(es[a], vf[wins[a], ks]) * invs[a]
                for g, h in enumerate(heads):
                    o_ref[rows[a], _head(h)] = o[g * BLOCK:(g + 1) * BLOCK].astype(BF16)

    prev = lambda i: jnp.maximum(i * nsub - 1, 0)
    cur = lambda w, c: pl.BlockSpec((None, tq, w), lambda r, i: (r, i, c // w))
    prv = lambda w, c: pl.BlockSpec((None, BLOCK, w), lambda r, i: (r, prev(i), c // w))
    out = pl.BlockSpec((None, tq, wq), lambda r, i: (r, i, 0))
    return pl.pallas_call(
        body, name=name, grid=(d, nqb),
        out_shape=[jax.ShapeDtypeStruct((d, n, wq), BF16), jax.ShapeDtypeStruct((d, n, wq), F32)],
        in_specs=[pl.BlockSpec(memory_space=pltpu.SMEM),
                  cur(wq, cfg.qc), cur(wk, cfg.kc), prv(wk, cfg.kc), cur(wk, cfg.vc), prv(wk, cfg.vc)],
        out_specs=[out, out],
        scratch_shapes=[pltpu.VMEM((tq + BLOCK, wk), BF16), pltpu.VMEM((tq + BLOCK, wk), BF16)],
        compiler_params=_cp("parallel", "parallel"),
    )(sinks, qkv, qkv, qkv, qkv, qkv)


def mix_merge(ya, o_g, l_g, proj, w_a, w_b, name):
    t = ya.shape[0]
    dm = w_a.shape[1]
    tm = _row_tile(t)
    gcol = GATE_COL // dm
    dils = [o.shape[0] for o in o_g]

    def body(ya_ref, o0, o1, o2, l0, l1, l2, ga_ref, gb_ref, wa_ref, wb_ref, yb_ref, mg_ref, *scrs):
        ls = [_unfold_from(l, scrs, d) for l, d in zip((l0, l1, l2), dils)]
        m = jnp.maximum(jnp.maximum(ls[0], ls[1]), ls[2])
        es = [jnp.exp(l - m) for l in ls]
        inv = 1.0 / (es[0] + es[1] + es[2])
        yb = sum(_unfold_from(o, scrs, d) * (e * inv) for o, e, d in zip((o0, o1, o2), es, dils)).astype(BF16)
        yb_ref[...] = yb
        pa = _dot(ya_ref[...], wa_ref[...])
        pb = _dot(yb, wb_ref[...])
        mg = _sigmoid(ga_ref[...].astype(F32)) * pa + _sigmoid(gb_ref[...].astype(F32)) * pb
        mg_ref[...] = mg.astype(BF16)

    wide = lambda w: pl.BlockSpec((tm, w), lambda i: (i, 0))
    folded = [_folded_spec(d, tm, B_OUT_W) for d in dils]
    return pl.pallas_call(
        body, name=name, grid=(t // tm,),
        out_shape=[jax.ShapeDtypeStruct((t, B_OUT_W), BF16), jax.ShapeDtypeStruct((t, dm), BF16)],
        in_specs=[wide(ya.shape[1])] + folded + folded
                 + [pl.BlockSpec((tm, dm), lambda i: (i, gcol)), pl.BlockSpec((tm, dm), lambda i: (i, gcol + 1)),
                    pl.BlockSpec(w_a.shape, lambda i: (0, 0)), pl.BlockSpec(w_b.shape, lambda i: (0, 0))],
        out_specs=[wide(B_OUT_W), wide(dm)],
        scratch_shapes=_fold_scratch(tm, B_OUT_W),
        compiler_params=_cp("parallel"),
    )(ya, *o_g, *l_g, proj, proj, w_a, w_b)


def loss_head(y, target):
    t, dm = y.shape
    tm = _row_tile(t)

    def body(y_ref, t_ref, dy_ref, loss_ref):
        @pl.when(pl.program_id(0) == 0)
        def _():
            loss_ref[...] = jnp.zeros_like(loss_ref)
        err = y_ref[...] - t_ref[...]
        dy_ref[...] = err * (1.0 / dm)
        per_row = jnp.sum(err * err, axis=1, keepdims=True) * (1.0 / dm)
        loss_ref[...] += 0.5 * jnp.sum(per_row, axis=0, keepdims=True)

    row = pl.BlockSpec((tm, dm), lambda i: (i, 0))
    return pl.pallas_call(
        body, name="loss_head", grid=(t // tm,),
        out_shape=[jax.ShapeDtypeStruct((t, dm), F32), jax.ShapeDtypeStruct((8, LANES), F32)],
        in_specs=[row, row],
        out_specs=[row, pl.BlockSpec((8, LANES), lambda i: (0, 0))],
        compiler_params=_cp("arbitrary"),
    )(y, target)


def _fold_rows(v):
    tm, c = v.shape
    return jnp.sum(v.reshape(tm // 8, 8, c), axis=0)


def _finish_sums(refs, nsteps):
    @pl.when(pl.program_id(0) == nsteps - 1)
    def _():
        for r in refs:
            r[...] = jnp.broadcast_to(jnp.sum(r[...], axis=0, keepdims=True), r.shape)


def ln_bwd(dxo, zhat, rstd, ysub, lg, gate, name):
    t, dm = dxo.shape
    tm = _row_tile(t)

    def body(dxo_ref, zh_ref, rs_ref, y_ref, lg_ref, g_ref, dz_ref, dy_ref, sg_ref, sb_ref, sgate_ref):
        @pl.when(pl.program_id(0) == 0)
        def _():
            for r in (sg_ref, sb_ref, sgate_ref):
                r[...] = jnp.zeros_like(r)
        dxo_v = dxo_ref[...]
        zh = zh_ref[...]
        dxh = dxo_v * lg_ref[...]
        m1 = jnp.mean(dxh, axis=1, keepdims=True)
        m2 = jnp.mean(dxh * zh, axis=1, keepdims=True)
        dz = rs_ref[:, 0:1] * (dxh - m1 - zh * m2)
        dz_ref[...] = dz
        dy_ref[...] = (g_ref[...] * dz).astype(BF16)
        sg_ref[...] += _fold_rows(dxo_v * zh)
        sb_ref[...] += _fold_rows(dxo_v)
        sgate_ref[...] += _fold_rows(dz * y_ref[...].astype(F32))
        _finish_sums((sg_ref, sb_ref, sgate_ref), t // tm)

    row = pl.BlockSpec((tm, dm), lambda i: (i, 0))
    vec = pl.BlockSpec((1, dm), lambda i: (0, 0))
    acc = pl.BlockSpec((8, dm), lambda i: (0, 0))
    return pl.pallas_call(
        body, name=name, grid=(t // tm,),
        out_shape=[jax.ShapeDtypeStruct((t, dm), F32), jax.ShapeDtypeStruct((t, dm), BF16)]
                  + [jax.ShapeDtypeStruct((8, dm), F32)] * 3,
        in_specs=[row, row, pl.BlockSpec((tm, LANES), lambda i: (i, 0)), row, vec, vec],
        out_specs=[row, row, acc, acc, acc],
        compiler_params=_cp("arbitrary"),
    )(dxo, zhat, rstd, ysub, lg, gate)


def _mod_bwd_store(du, dz_ref, x_ref, s_ref, dx_ref, ss_ref, ssh_ref, nsteps):
    @pl.when(pl.program_id(0) == 0)
    def _():
        ss_ref[...] = jnp.zeros_like(ss_ref)
        ssh_ref[...] = jnp.zeros_like(ssh_ref)
    dx_ref[...] = DN_ALPHA * dz_ref[...] + du * (1.0 + s_ref[...])
    ss_ref[...] += _fold_rows(du * x_ref[...])
    ssh_ref[...] += _fold_rows(du)
    _finish_sums((ss_ref, ssh_ref), nsteps)


def dgrad_ffn(g, wt, dz, xin, s, name, ex=None):
    t, dm = dz.shape
    k = g.shape[1]
    tm = _row_tile(t)
    nsteps = t // tm

    def body(*refs):
        i = pl.program_id(0)
        g_ref, w_ref, dz_ref, x_ref, s_ref, dx_ref, ss_ref, ssh_ref = _hosted(ex, refs, 5, 3, i == 0, i == nsteps - 1)
        du = _dot(g_ref[...], w_ref[...])
        _mod_bwd_store(du, dz_ref, x_ref, s_ref, dx_ref, ss_ref, ssh_ref, nsteps)

    row = pl.BlockSpec((tm, dm), lambda i: (i, 0))
    acc = pl.BlockSpec((8, dm), lambda i: (0, 0))
    return _host_call(
        body, ex, name=name, grid=(nsteps,),
        out_shape=[jax.ShapeDtypeStruct((t, dm), F32)] + [jax.ShapeDtypeStruct((8, dm), F32)] * 2,
        in_specs=[pl.BlockSpec((tm, k), lambda i: (i, 0)), pl.BlockSpec((k, dm), lambda i: (0, 0)),
                  row, row, pl.BlockSpec((1, dm), lambda i: (0, 0))],
        out_specs=[row, acc, acc], sem=("arbitrary",), args=[g, wt, dz, xin, s])


def dgrad_in(d_a, d_b, dgab, wt, dz, xin, s, name, ex=None):
    t, dm = dz.shape
    tm = _row_tile(t)
    nsteps = t // tm
    dils = [a.shape[0] for a in d_b]

    def body(*refs):
        i = pl.program_id(0)
        (da_ref, b0, b1, b2, dg_ref, w_ref, dz_ref, x_ref, s_ref, dx_ref, ss_ref, ssh_ref,
         *scrs) = _hosted(ex, refs, 9, 3, i == 0, i == nsteps - 1)
        du = _dot(da_ref[0], w_ref[0:A_W, :])
        for g, (b_ref, d) in enumerate(zip((b0, b1, b2), dils)):
            v = b_ref[...].reshape(tm, B_GW)
            part = None
            for p in range(3):
                r0 = A_W + p * B_ALL + g * B_OUT_W
                term = _dot(v[:, p * B_OUT_W:(p + 1) * B_OUT_W], w_ref[r0:r0 + B_OUT_W, :])
                part = term if part is None else part + term
            if d == 1:
                du = du + part
            else:
                n = tm // d
                du = du + _unfold_rows(lambda r, cols: part[r * n:(r + 1) * n, cols], scrs, d, n, dm)
        for j in range(2):
            du = du + _dot(dg_ref[:, j * dm:(j + 1) * dm], w_ref[GATE_COL + j * dm:GATE_COL + (j + 1) * dm, :])
        _mod_bwd_store(du, dz_ref, x_ref, s_ref, dx_ref, ss_ref, ssh_ref, nsteps)

    row = pl.BlockSpec((tm, dm), lambda i: (i, 0))
    acc = pl.BlockSpec((8, dm), lambda i: (0, 0))
    return _host_call(
        body, ex, name=name, grid=(nsteps,),
        out_shape=[jax.ShapeDtypeStruct((t, dm), F32)] + [jax.ShapeDtypeStruct((8, dm), F32)] * 2,
        in_specs=[_folded_spec(1, tm, A_W)] + [_folded_spec(d, tm, B_GW) for d in dils]
                 + [pl.BlockSpec((tm, 2 * dm), lambda i: (i, 0)), pl.BlockSpec(wt.shape, lambda i: (0, 0)),
                    row, row, pl.BlockSpec((1, dm), lambda i: (0, 0))],
        out_specs=[row, acc, acc],
        scratch_shapes=_fold_scratch(tm, dm), sem=("arbitrary",), args=[d_a, *d_b, dgab, wt, dz, xin, s])


def wgrad(a, b, buf, tn, nj, b0, o0, om, name):
    t, k = a.shape
    tt = ROW_TILE
    while tt * 2 * k <= WGRAD_TILE_ELEMS and tt * 2 <= t:
        tt *= 2
    nsteps = t // tt

    def body(a_ref, b_ref, buf_ref, o_ref, acc):
        s = pl.program_id(1)

        @pl.when(s == 0)
        def _():
            acc[...] = jnp.zeros_like(acc)
        acc[...] += _dot_tn(a_ref[...], b_ref[...])

        @pl.when(s == nsteps - 1)
        def _():
            o_ref[...] = acc[...].astype(BF16)

    return pl.pallas_call(
        body, name=name, grid=(nj, nsteps),
        out_shape=jax.ShapeDtypeStruct(buf.shape, buf.dtype),
        in_specs=[pl.BlockSpec((tt, k), lambda j, s: (s, 0)),
                  pl.BlockSpec((tt, tn), lambda j, s: (s, b0 + j)),
                  pl.BlockSpec(memory_space=pl.ANY)],
        out_specs=pl.BlockSpec((k, tn), lambda j, s: (0, o0 + om * j)),
        scratch_shapes=[pltpu.VMEM((k, tn), F32)],
        input_output_aliases={2: 0},
        compiler_params=_cp("parallel", "arbitrary"),
    )(a, b, buf)


def dswiglu(dy, wdt, ab, name, ex=None):
    t, dm = dy.shape
    f = wdt.shape[1]
    tm = _row_tile(t)
    nsteps = t // tm

    def body(*refs):
        i = pl.program_id(0)
        dy_ref, w_ref, a_ref, b_ref, o_ref = _hosted(ex, refs, 4, 1, i == 0, i == nsteps - 1)
        dh = _dot(dy_ref[...], w_ref[...])
        a = a_ref[...].astype(F32)
        sg = _sigmoid(a)
        o_ref[:, 0:f] = (dh * b_ref[...].astype(F32) * (sg * (1.0 + a * (1.0 - sg)))).astype(BF16)
        o_ref[:, f:] = (dh * (a * sg)).astype(BF16)

    return _host_call(
        body, ex, name=name, grid=(nsteps,),
        out_shape=[jax.ShapeDtypeStruct((t, 2 * f), BF16)],
        in_specs=[pl.BlockSpec((tm, dm), lambda i: (i, 0)), pl.BlockSpec((dm, f), lambda i: (0, 0)),
                  pl.BlockSpec((tm, f), lambda i: (i, 0)), pl.BlockSpec((tm, f), lambda i: (i, 1))],
        out_specs=[pl.BlockSpec((tm, 2 * f), lambda i: (i, 0))], sem=("arbitrary",), args=[dy, wdt, ab, ab])


def dmerge(do, wot, ya, yb, w_a, w_b, wat, wbt, proj, name):
    t, dm = do.shape
    tm = _row_tile(t)
    gcol = GATE_COL // dm

    def body(do_ref, wot_ref, ya_ref, yb_ref, wa_ref, wb_ref, wat_ref, wbt_ref, g_ref,
             dp_ref, dya_ref, dyb_ref, dg_ref, dm_scr):
        j = pl.program_id(1)

        @pl.when(j == 0)
        def _():
            dm_scr[...] = _dot(do_ref[...], wot_ref[...])

        def branch(y_ref, w_ref, wt_ref, dy_ref):
            p = _dot(y_ref[...], w_ref[...])
            sg = _sigmoid(g_ref[...].astype(F32))
            dmg = dm_scr[...]
            dp = (dmg * sg).astype(BF16)
            dp_ref[...] = dp
            dg_ref[...] = (dmg * p * (sg * (1.0 - sg))).astype(BF16)
            dy_ref[...] = _dot(dp, wt_ref[...]).astype(dy_ref.dtype)

        pl.when(j == 0)(lambda: branch(ya_ref, wa_ref, wat_ref, dya_ref))
        pl.when(j == 1)(lambda: branch(yb_ref, wb_ref, wbt_ref, dyb_ref))

    full = lambda arr: pl.BlockSpec(arr.shape, lambda i, j: (0, 0))
    rowc = lambda w: pl.BlockSpec((tm, w), lambda i, j: (i, 0))
    return pl.pallas_call(
        body, name=name, grid=(t // tm, 2),
        out_shape=[jax.ShapeDtypeStruct((t, 2 * dm), BF16), jax.ShapeDtypeStruct((t, ya.shape[1]), BF16),
                   jax.ShapeDtypeStruct((t, yb.shape[1]), F32), jax.ShapeDtypeStruct((t, 2 * dm), BF16)],
        in_specs=[rowc(dm), full(wot), rowc(ya.shape[1]), rowc(yb.shape[1]), full(w_a), full(w_b),
                  full(wat), full(wbt), pl.BlockSpec((tm, dm), lambda i, j: (i, gcol + j))],
        out_specs=[pl.BlockSpec((tm, dm), lambda i, j: (i, j)), rowc(ya.shape[1]), rowc(yb.shape[1]),
                   pl.BlockSpec((tm, dm), lambda i, j: (i, j))],
        scratch_shapes=[pltpu.VMEM((tm, dm), F32)],
        compiler_params=_cp("parallel", "arbitrary"),
    )(do, wot, ya, yb, w_a, w_b, wat, wbt, proj)


def mix_bwd(dyb, o_g, l_g, name):
    t, w = dyb.shape
    tm = _row_tile(t)
    nh = w // HEAD_DIM
    dils = [o.shape[0] for o in o_g]

    def body(dyb_ref, o0, o1, o2, l0, l1, l2, do0, do1, do2, dl0, dl1, dl2, *scr):
        ls = [_unfold_from(l, scr, d) for l, d in zip((l0, l1, l2), dils)]
        m = jnp.maximum(jnp.maximum(ls[0], ls[1]), ls[2])
        es = [jnp.exp(l - m) for l in ls]
        inv = 1.0 / (es[0] + es[1] + es[2])
        wts = [e * inv for e in es]
        dyb_v = dyb_ref[...]
        dws = []
        for o_ref, do_ref, wt, d in zip((o0, o1, o2), (do0, do1, do2), wts, dils):
            prod = dyb_v * _unfold_from(o_ref, scr, d)
            _fold_to(do_ref, dyb_v * wt, scr, d)
            for h in range(nh):
                hs = slice(h * HEAD_DIM, (h + 1) * HEAD_DIM)
                dws.append(jnp.broadcast_to(jnp.sum(prod[:, hs], axis=1, keepdims=True), (tm, HEAD_DIM)))
        for g, (dl_ref, d) in enumerate(zip((dl0, dl1, dl2), dils)):
            cols = []
            for h in range(nh):
                hs = slice(h * HEAD_DIM, (h + 1) * HEAD_DIM)
                mean = sum(wts[g2][:, hs] * dws[g2 * nh + h] for g2 in range(N_GROUPS))
                cols.append(wts[g][:, hs] * (dws[g * nh + h] - mean))
            _fold_to(dl_ref, jnp.concatenate(cols, axis=1), scr, d)

    folded = [_folded_spec(d, tm, w) for d in dils]
    return pl.pallas_call(
        body, name=name, grid=(t // tm,),
        out_shape=[jax.ShapeDtypeStruct(o.shape, BF16) for o in o_g]
                  + [jax.ShapeDtypeStruct(o.shape, F32) for o in o_g],
        in_specs=[pl.BlockSpec((tm, w), lambda i: (i, 0))] + folded + folded,
        out_specs=folded + folded,
        scratch_shapes=_fold_scratch(tm, w),
        compiler_params=_cp("parallel"),
    )(dyb, *o_g, *l_g)


def attn_bwd(qkv, o, lse, do, dlse, cfg, sinks, name):
    d, n, _ = qkv.shape
    tq, nsub, nqb = _attn_geometry(cfg, n)
    wq, wk, wout = cfg.wq, cfg.wk, cfg.wout
    grp = cfg.heads // cfg.kv_heads
    has_dl = dlse is not None
    scale = HEAD_DIM ** -0.5

    def body(*refs):
        sink_ref, q_ref, qn_ref, kc_ref, kp_ref, vc_ref, vp_ref = refs[:7]
        o_ref, on_ref, do_ref, don_ref, l_ref, ln_ref = refs[7:13]
        rest = refs[13:]
        dl_ref = dln_ref = None
        if has_dl:
            dl_ref, dln_ref = rest[:2]
            rest = rest[2:]
        out_ref = rest[0]
        rest = rest[1:]
        if cfg.sinks:
            dsink_ref = rest[0]
            rest = rest[1:]
        kf, vf, dk_acc, dv_acc = rest
        r, i = pl.program_id(0), pl.program_id(1)
        kf[0:BLOCK, :] = kp_ref[...]
        kf[BLOCK:, :] = kc_ref[...]
        vf[0:BLOCK, :] = vp_ref[...]
        vf[BLOCK:, :] = vc_ref[...]
        dist, valid, valid_first = _band(i, cfg.max_dist)
        distf = dist.astype(F32)
        next_dist = jnp.where(i < nqb - 1, cfg.max_dist, -1)
        valid_next = (dist[:, 0:BLOCK] >= 0) & (dist[:, 0:BLOCK] <= next_dist)
        if cfg.sinks:
            @pl.when((r == 0) & (i == 0))
            def _():
                dsink_ref[...] = jnp.zeros_like(dsink_ref)

        def stacked(ref, rows, heads):
            return _stack([ref[rows, _head(h)] for h in heads])

        def per_row(ref, rows, heads):
            return _stack([jnp.max(ref[rows, _head(h)], axis=1, keepdims=True) for h in heads])

        for kv in range(cfg.kv_heads):
            heads = [kv * grp + g for g in range(grp)]
            ks = slice(kv * HEAD_DIM, (kv + 1) * HEAD_DIM)
            dk_acc[...] = jnp.zeros_like(dk_acc)
            dv_acc[...] = jnp.zeros_like(dv_acc)
            biases = [_stack([_masked_bias(m, dd, cfg, h, d) for h in heads])
                      for m, dd in ((valid_first, distf), (valid, distf), (valid_next, distf[:, 0:BLOCK]))]
            tiles = []
            for a in range(nsub + 1):
                if a < nsub:
                    rows, win = slice(a * BLOCK, (a + 1) * BLOCK), slice(a * BLOCK, (a + 2) * BLOCK)
                    src = (q_ref, o_ref, do_ref, l_ref, dl_ref)
                else:
                    rows, win = slice(0, BLOCK), slice(nsub * BLOCK, (nsub + 1) * BLOCK)
                    src = (qn_ref, on_ref, don_ref, ln_ref, dln_ref)
                qs = stacked(src[0], rows, heads) * SCALE
                do_v = stacked(src[2], rows, heads)
                delta = jnp.sum(do_v.astype(F32) * stacked(src[1], rows, heads).astype(F32), axis=1, keepdims=True)
                lse_v = per_row(src[3], rows, heads)
                shift = (per_row(src[4], rows, heads) - delta) if has_dl else -delta
                k = kf[win, ks]
                s = _dot_nt(qs, k) + biases[0 if a == 0 else (1 if a < nsub else 2)]
                dp = _dot_nt(do_v, vf[win, ks])
                tiles.append((qs, do_v, k, delta, lse_v, shift, s, dp))
            grads = []
            for qs, do_v, k, delta, lse_v, shift, s, dp in tiles:
                p = jnp.exp(s - lse_v)
                grads.append(((p * (dp + shift)).astype(BF16), p.astype(BF16)))
            for a, ((qs, do_v, k, delta, lse_v, shift, _, _), (dsb, pb)) in enumerate(zip(tiles, grads)):
                if a < nsub:
                    dq = _dot(dsb, k) * SCALE
                    for g, h in enumerate(heads):
                        out_ref[a * BLOCK:(a + 1) * BLOCK, _head(h)] = dq[g * BLOCK:(g + 1) * BLOCK].astype(BF16)
                if a == 0:
                    kcols = slice(0, BLOCK)
                    dsb, pb = dsb[:, BLOCK:], pb[:, BLOCK:]
                elif a < nsub:
                    kcols = slice((a - 1) * BLOCK, (a + 1) * BLOCK)
                else:
                    kcols = slice((nsub - 1) * BLOCK, nsub * BLOCK)
                dk_acc[:, kcols] += _dot_tn(qs, dsb)
                dv_acc[:, kcols] += _dot_tn(do_v, pb)
                if cfg.sinks and a < nsub:
                    for g, h in enumerate(heads):
                        part = slice(g * BLOCK, (g + 1) * BLOCK)
                        psink = jnp.exp(sink_ref[h] - lse_v[part])
                        tot = jnp.sum(psink * (-delta[part]), axis=0, keepdims=True)
                        dsink_ref[h:h + 1, :] += jnp.broadcast_to(tot, (1, LANES))
            out_ref[:, wq + kv * HEAD_DIM: wq + (kv + 1) * HEAD_DIM] = dk_acc[...].T.astype(BF16)
            out_ref[:, wq + wk + kv * HEAD_DIM: wq + wk + (kv + 1) * HEAD_DIM] = dv_acc[...].T.astype(BF16)

    prev = lambda i: jnp.maximum(i * nsub - 1, 0)
    nxt = lambda i: jnp.minimum((i + 1) * nsub, n // BLOCK - 1)
    cur = lambda w, c: pl.BlockSpec((None, tq, w), lambda r, i: (r, i, c // w))
    prv = lambda w, c: pl.BlockSpec((None, BLOCK, w), lambda r, i: (r, prev(i), c // w))
    o_cur = pl.BlockSpec((None, tq, wq), lambda r, i: (r, i, 0))
    o_nxt = pl.BlockSpec((None, BLOCK, wq), lambda r, i: (r, nxt(i), 0))
    in_specs = [pl.BlockSpec(memory_space=pltpu.SMEM),
                cur(wq, cfg.qc), pl.BlockSpec((None, BLOCK, wq), lambda r, i: (r, nxt(i), cfg.qc // wq)),
                cur(wk, cfg.kc), prv(wk, cfg.kc), cur(wk, cfg.vc), prv(wk, cfg.vc),
                o_cur, o_nxt, o_cur, o_nxt, o_cur, o_nxt]
    args = [sinks, qkv, qkv, qkv, qkv, qkv, qkv, o, o, do, do, lse, lse]
    if has_dl:
        in_specs += [o_cur, o_nxt]
        args += [dlse, dlse]
    out_shape = [jax.ShapeDtypeStruct((d, n, wout), BF16)]
    out_specs = [pl.BlockSpec((None, tq, wout), lambda r, i: (r, i, 0))]
    if cfg.sinks:
        out_shape.append(jax.ShapeDtypeStruct((8, LANES), F32))
        out_specs.append(pl.BlockSpec((8, LANES), lambda r, i: (0, 0)))
    return pl.pallas_call(
        body, name=name, grid=(d, nqb), out_shape=out_shape, in_specs=in_specs, out_specs=out_specs,
        scratch_shapes=[pltpu.VMEM((tq + BLOCK, wk), BF16), pltpu.VMEM((tq + BLOCK, wk), BF16),
                        pltpu.VMEM((HEAD_DIM, tq), F32), pltpu.VMEM((HEAD_DIM, tq), F32)],
        compiler_params=_cp("arbitrary", "arbitrary"),
    )(*args)


def _adamw(g, w, m, v):
    m = ADAM_B1 * m + (1.0 - ADAM_B1) * g
    v = ADAM_B2 * v + (1.0 - ADAM_B2) * (g * g)
    m_hat = m / (1.0 - ADAM_B1 ** ADAM_STEP)
    v_hat = v / (1.0 - ADAM_B2 ** ADAM_STEP)
    delta = -ADAM_LR * (m_hat / (jnp.sqrt(v_hat) + ADAM_EPS) + ADAM_WD * w)
    return delta, m, v


def adam_reduce(parts, w, m, v, name):
    r, c = w.shape
    tr = next(cand for cand in (256, 128, 64, 32, 16, 8) if r % cand == 0) if r > 256 else r

    def body(p_ref, w_ref, m_ref, v_ref, g_ref, d_ref, mo_ref, vo_ref):
        g = p_ref[0].astype(F32)
        for j in range(1, N_DEV):
            g = g + p_ref[j].astype(F32)
        g_ref[...] = g
        d_ref[...], mo_ref[...], vo_ref[...] = _adamw(g, w_ref[...], m_ref[...], v_ref[...])

    row = pl.BlockSpec((tr, c), lambda i: (i, 0))
    return pl.pallas_call(
        body, name=name, grid=(r // tr,),
        out_shape=[jax.ShapeDtypeStruct((r, c), F32)] * 4,
        in_specs=[pl.BlockSpec((N_DEV, tr, c), lambda i: (0, i, 0)), row, row, row],
        out_specs=[row] * 4,
        compiler_params=_cp("parallel"),
    )(parts, w, m, v)


def adam_layers(parts, w, m, v, name):
    nl, r, c = w.shape
    tr = next(cand for cand in (256, 128, 64, 32, 16, 8) if r % cand == 0)
    steps = r // tr

    def body(*refs):
        p_refs = refs[:nl]
        w_ref, m_ref, v_ref, g_ref, d_ref, mo_ref, vo_ref = refs[nl:]
        for k in range(nl):
            @pl.when(pl.program_id(0) == k)
            def _():
                g = p_refs[k][0].astype(F32)
                for j in range(1, N_DEV):
                    g = g + p_refs[k][j].astype(F32)
                g_ref[...] = g
                d_ref[...], mo_ref[...], vo_ref[...] = _adamw(g, w_ref[...], m_ref[...], v_ref[...])

    def part_spec(k):
        return pl.BlockSpec((N_DEV, tr, c), lambda l, i: (0, jnp.clip(i + (l - k) * steps, 0, steps - 1), 0))

    blk = pl.BlockSpec((None, tr, c), lambda l, i: (l, i, 0))
    return pl.pallas_call(
        body, name=name, grid=(nl, steps),
        out_shape=[jax.ShapeDtypeStruct((nl, r, c), F32)] * 4,
        in_specs=[part_spec(k) for k in range(nl)] + [blk, blk, blk],
        out_specs=[blk] * 4,
        compiler_params=_cp("arbitrary", "arbitrary"),
    )(*parts, w, m, v)


def adam_w_ada(sct, dm_loc, w, m, v):
    nl, dm, wc = w.shape
    tr = 512

    def body(s_ref, d_ref, w_ref, m_ref, v_ref, g_ref, dl_ref, mo_ref, vo_ref):
        g = jnp.dot(s_ref[...], d_ref[...], preferred_element_type=F32, precision=lax.Precision.HIGHEST)
        g_ref[...] = g
        dl_ref[...], mo_ref[...], vo_ref[...] = _adamw(g, w_ref[...], m_ref[...], v_ref[...])

    blk = pl.BlockSpec((None, tr, wc), lambda l, i: (l, i, 0))
    return pl.pallas_call(
        body, name="adam_w_ada", grid=(nl, dm // tr),
        out_shape=[jax.ShapeDtypeStruct(w.shape, F32)] * 4,
        in_specs=[pl.BlockSpec((tr, LANES), lambda l, i: (i, 0)),
                  pl.BlockSpec((None, LANES, wc), lambda l, i: (l, 0, 0)), blk, blk, blk],
        out_specs=[blk] * 4,
        compiler_params=_cp("parallel", "parallel"),
    )(sct, dm_loc, w, m, v)


TRANSPOSED = ("w_gate", "w_up")


def _pieces(dm):
    ncol = lambda n: n // N_DEV
    mixer = ([Piece("w_in", "w_in", 1, 0, ncol(GATE_COL + 2 * dm)),
              Piece("w_a", "w_a", 1, 0, ncol(dm)),
              Piece("w_b", "w_b", 1, 0, ncol(dm)),
              Piece("w_o", "w_o", 0, 0, ncol(dm))],
             {"w_in": (dm, GATE_COL + 2 * dm), "w_a": (A_Q_HEADS * HEAD_DIM, dm), "w_b": (B_OUT_W, dm),
              "w_o": (dm, dm)})
    ffn = ([Piece("w_gate", "w_ffn_t", 0, 0, ncol(D_FF)),
            Piece("w_up", "w_ffn_t", 0, D_FF, ncol(D_FF)),
            Piece("w_down", "w_down", 0, 0, ncol(D_FF))],
           {"w_ffn_t": (2 * D_FF, dm), "w_down": (D_FF, dm)})
    return mixer, ffn


def kernel(x, c, w_ada, b_ada, w_in, sinks, w_a, w_b, w_o, ln1_g, ln1_b, w_gate, w_up, w_down, ln2_g, ln2_b, loss_target, m_w_ada, m_b_ada, m_w_in, m_sinks, m_w_a, m_w_b, m_w_o, m_ln1_g, m_ln1_b, m_w_gate, m_w_up, m_w_down, m_ln2_g, m_ln2_b, v_w_ada, v_b_ada, v_w_in, v_sinks, v_w_a, v_w_b, v_w_o, v_ln1_g, v_ln1_b, v_w_gate, v_w_up, v_w_down, v_ln2_g, v_ln2_b):
    given = dict(locals())
    nl = w_in.shape[0]
    t, dm = x.shape[1], x.shape[2]
    me = 4 * lax.axis_index("x") + 2 * lax.axis_index("y") + lax.axis_index("c")
    x0 = x.reshape(t, dm)
    target = loss_target.reshape(t, dm)

    groups = dict(zip(("mixer", "ffn"), _pieces(dm)))
    local = lambda nm, pre="": (given[pre + nm].transpose(0, 2, 1) if nm in TRANSPOSED else given[pre + nm])
    shards = {pc.name: local(pc.name).astype(BF16) for pcs, _ in groups.values() for pc in pcs}

    def gather(group, l):
        pcs, bufs = groups[group]
        return Exchange("gather", pcs, [shards[pc.name][l] for pc in pcs], bufs.values(), bufs)

    def scatter(group, gbuf):
        pcs, bufs = groups[group]
        return Exchange("scatter", pcs, [gbuf[nm] for nm in bufs],
                        [(N_DEV,) + shards[pc.name].shape[1:] for pc in pcs], bufs)

    full = [dict() for _ in range(nl)]
    full[0].update(zip(groups["mixer"][1], run_exchange(gather("mixer", 0), "gather_mixer")))

    wc = w_ada.shape[2]
    c_all = all_gather_small(jnp.broadcast_to(c, (8, dm)), "gather_c")[:, 0, :]
    b_loc = lax.dynamic_slice_in_dim(b_ada, me * wc, wc, axis=1).reshape(nl, 1, wc)
    mp, sc_all = mod_partial(c_all, w_ada, b_loc)
    mp_all = all_gather_small(mp.reshape(nl * N_DEV, wc), "gather_mod").reshape(N_DEV, nl, N_DEV, wc)
    mod = lax.dynamic_index_in_dim(mp_all, me, axis=2, keepdims=False)
    mod = mod.transpose(1, 0, 2).reshape(nl, 6, 1, dm)

    vec = lambda a, l: a[l].reshape(1, dm)

    saved = []
    xl = x0
    for l in range(nl):
        sh1, s1, g1, sh2, s2, g2 = [mod[l, j] for j in range(6)]
        w = full[l]
        (u1, u1_f4, u1_f16, proj, qkv_f4, qkv_f16), got = in_proj(xl, s1, sh1, w["w_in"], "in_proj",
                                                                   gather("ffn", l))
        w.update(zip(groups["ffn"][1], got))
        proj3 = proj.reshape(1, t, proj.shape[1])
        qkv_b = [proj3, qkv_f4, qkv_f16]
        ya, lse_a = attn_fwd(proj3, ATTN_A, sinks[l], "attn_a_fwd")
        o_g, l_g = [], []
        for g, cfg in enumerate(ATTN_B):
            o, ls = attn_fwd(qkv_b[g], cfg, sinks[l], "attn_b%d_fwd" % g)
            o_g.append(o)
            l_g.append(ls)
        yb, merged = mix_merge(ya[0], o_g, l_g, proj, w["w_a"], w["w_b"], "mix_merge")
        y1, x1, zh1, rs1 = proj_ln(merged, w["w_o"], xl, g1, vec(ln1_g, l), vec(ln1_b, l), "out_proj_ln")
        (u2, ab), got = modmm(x1, s2, sh2, w["w_ffn_t"].T, "ffn_up", gather("mixer", l + 1) if l + 1 < nl else None)
        if l + 1 < nl:
            full[l + 1].update(zip(groups["mixer"][1], got))
        h, y2, x2, zh2, rs2 = swiglu_proj_ln(ab, w["w_down"], x1, g2, vec(ln2_g, l), vec(ln2_b, l), "ffn_down_ln")
        saved.append(dict(xin=xl, u1=[u1, u1_f4.reshape(t, dm), u1_f16.reshape(t, dm)], proj=proj, qkv_b=qkv_b,
                          ya=ya, lse_a=lse_a, o_g=o_g, l_g=l_g, yb=yb, merged=merged,
                          y1=y1, x1=x1, zh1=zh1, rs1=rs1, u2=u2, ab=ab, h=h, y2=y2, zh2=zh2, rs2=rs2))
        xl = x2

    dx, loss_part = loss_head(xl, target)

    small = {k: [None] * nl for k in ("dmod", "ln1_g", "ln1_b", "ln2_g", "ln2_b", "sinks")}
    recv = {nm: [None] * nl for grp in groups.values() for nm in (pc.name for pc in grp[0])}

    def keep(group, l, got):
        for pc, arr in zip(groups[group][0], got):
            recv[pc.name][l] = arr

    for l in reversed(range(nl)):
        sv, w = saved[l], full[l]
        sh1, s1, g1, sh2, s2, g2 = [mod[l, j] for j in range(6)]
        fresh = lambda nm: lax.empty({**groups["mixer"][1], **groups["ffn"][1]}[nm], BF16)
        gbuf = {}
        dz2, dy2, sg, sb, sgate2 = ln_bwd(dx, sv["zh2"], sv["rs2"], sv["y2"], vec(ln2_g, l), g2, "ln_bwd")
        small["ln2_g"][l], small["ln2_b"][l] = sg[0], sb[0]
        gbuf["w_down"] = wgrad(sv["h"], dy2, fresh("w_down"), 512, dm // 512, 0, 0, 1, "wgrad_down")
        (dab,), _ = dswiglu(dy2, w["w_down"].T, sv["ab"], "dswiglu")
        gbuf["w_ffn_t"] = wgrad(dab, sv["u2"], fresh("w_ffn_t"), 512, dm // 512, 0, 0, 1, "wgrad_ffn_up")
        (dx1, ss2, ssh2), got = dgrad_ffn(dab, w["w_ffn_t"], dz2, sv["x1"], s2, "dgrad_ffn", scatter("ffn", gbuf))
        keep("ffn", l, got)
        dz1, do1, sg, sb, sgate1 = ln_bwd(dx1, sv["zh1"], sv["rs1"], sv["y1"], vec(ln1_g, l), g1, "ln_bwd")
        small["ln1_g"][l], small["ln1_b"][l] = sg[0], sb[0]
        gbuf["w_o"] = wgrad(sv["merged"], do1, fresh("w_o"), 512, dm // 512, 0, 0, 1, "wgrad_o")
        dpab, dya, dyb, dgab = dmerge(do1, w["w_o"].T, sv["ya"][0], sv["yb"], w["w_a"], w["w_b"],
                                      w["w_a"].T, w["w_b"].T, sv["proj"], "dmerge")
        gbuf["w_a"] = wgrad(sv["ya"][0], dpab, fresh("w_a"), 512, dm // 512, 0, 0, 1, "wgrad_a")
        gbuf["w_b"] = wgrad(sv["yb"], dpab, fresh("w_b"), 512, dm // 512, dm // 512, 0, 1, "wgrad_b")
        mixed = mix_bwd(dyb, sv["o_g"], sv["l_g"], "mix_bwd")
        do_g, dl_g = mixed[:N_GROUPS], mixed[N_GROUPS:]
        d_a, dsink = attn_bwd(sv["qkv_b"][0], sv["ya"], sv["lse_a"], dya.reshape(1, t, -1), None, ATTN_A,
                              sinks[l], "attn_a_bwd")
        small["sinks"][l] = dsink[:, 0]
        d_b = [attn_bwd(sv["qkv_b"][g], sv["o_g"][g], sv["l_g"][g], do_g[g], dl_g[g], cfg, sinks[l],
                        "attn_b%d_bwd" % g)[0] for g, cfg in enumerate(ATTN_B)]
        gw = wgrad(sv["u1"][0], d_a.reshape(t, A_W), fresh("w_in"), A_W, 1, 0, 0, 1, "wgrad_in_a")
        for g in range(N_GROUPS):
            gw = wgrad(sv["u1"][g], d_b[g].reshape(t, B_GW), gw, B_OUT_W, 3, 0, A_W // B_OUT_W + g, N_GROUPS,
                       "wgrad_in_b%d" % g)
        gbuf["w_in"] = wgrad(sv["u1"][0], dgab, gw, 512, 2 * dm // 512, 0, GATE_COL // 512, 1, "wgrad_in_gate")
        (dx, ss1, ssh1), got = dgrad_in(d_a, d_b, dgab, w["w_in"].T, dz1, sv["xin"], s1, "dgrad_in",
                                        scatter("mixer", gbuf))
        keep("mixer", l, got)
        small["dmod"][l] = jnp.stack([ssh1[0], ss1[0], sgate1[0], ssh2[0], ss2[0], sgate2[0]])
    grad_x = dx.reshape(x.shape)

    big_out = {}
    for nm, parts in recv.items():
        outs = adam_layers(parts, local(nm), local(nm, "m_"), local(nm, "v_"), "adam_" + nm)
        big_out[nm] = [o.transpose(0, 2, 1) for o in outs] if nm in TRANSPOSED else outs

    rows = jnp.concatenate(
        [jnp.stack(small["dmod"]).reshape(nl * 6, dm)]
        + [jnp.stack(small[k]) for k in ("ln1_g", "ln1_b", "ln2_g", "ln2_b")]
        + [jnp.pad(jnp.stack(small["sinks"]).reshape(1, -1), ((0, 0), (0, dm - nl * A_Q_HEADS))),
           jnp.broadcast_to(loss_part[0:1, 0:1], (1, dm))])
    n_rows = rows.shape[0]
    rows = jnp.pad(rows, ((0, -n_rows % 8), (0, 0)))
    rows_all = all_gather_small(rows, "gather_small_grads")

    def pack_small(pre):
        parts = [given[pre + "b_ada"].reshape(nl * 6, dm)]
        parts += [given[pre + k] for k in ("ln1_g", "ln1_b", "ln2_g", "ln2_b")]
        parts.append(jnp.pad(given[pre + "sinks"].reshape(1, -1), ((0, 0), (0, dm - nl * A_Q_HEADS))))
        p = jnp.concatenate(parts)
        return jnp.pad(p, ((0, rows.shape[0] - p.shape[0]), (0, 0)))

    souts = adam_reduce(rows_all, pack_small(""), pack_small("m_"), pack_small("v_"), "adam_small")

    def unpack_small(o):
        r = {"b_ada": o[0:nl * 6].reshape(nl, 6 * dm)}
        for j, k in enumerate(("ln1_g", "ln1_b", "ln2_g", "ln2_b")):
            r[k] = o[nl * 6 + j * nl: nl * 6 + (j + 1) * nl]
        r["sinks"] = o[nl * 10, 0:nl * A_Q_HEADS].reshape(nl, A_Q_HEADS)
        return r

    small_out = [unpack_small(o) for o in souts]
    loss = souts[0][nl * 10 + 1, 0]

    dmod_all = rows_all[:, 0:nl * 6].reshape(N_DEV, nl, 6 * dm)
    dm_loc = lax.dynamic_slice_in_dim(dmod_all, me * wc, wc, axis=2).transpose(1, 0, 2)
    dm_loc = jnp.pad(dm_loc, ((0, 0), (0, LANES - N_DEV), (0, 0)))
    sct = jnp.pad(sc_all.T, ((0, 0), (0, LANES - N_DEV)))
    ada_out = adam_w_ada(sct, dm_loc, w_ada, m_w_ada, v_w_ada)

    names = ["w_ada", "b_ada", "w_in", "sinks", "w_a", "w_b", "w_o", "ln1_g", "ln1_b",
             "w_gate", "w_up", "w_down", "ln2_g", "ln2_b"]

    def pick(kind, nm):
        if nm == "w_ada":
            return ada_out[kind]
        if nm in small_out[kind]:
            return small_out[kind][nm]
        return big_out[nm][kind]

    result = [loss, grad_x]
    for kind in range(4):
        result += [pick(kind, nm) for nm in names]
    return tuple(result)
```

```python
import functools

import jax
import jax.numpy as jnp
from jax import lax
from jax.experimental import pallas as pl
from jax.experimental.pallas import tpu as pltpu

F32 = jnp.float32
BF16 = jnp.bfloat16

D_MODEL = 1024
HEAD_DIM = 64
A_Q_HEADS = 8
A_KV_HEADS = 2
A_WINDOW = 128
B_GROUPS = ((128, 1), (512, 4), (2048, 16))
N_GROUPS = len(B_GROUPS)
B_HEADS_PER_GROUP = 4
N_ATTN_HEADS = A_Q_HEADS + B_HEADS_PER_GROUP * N_GROUPS
BLOCK = 128
A_W = (A_Q_HEADS + 2 * A_KV_HEADS) * HEAD_DIM
B_OUT_W = B_HEADS_PER_GROUP * HEAD_DIM
B_GW = 3 * B_OUT_W
B_ALL = N_GROUPS * B_OUT_W
GATE_COL = A_W + 3 * B_ALL
D_FF = 2816
FF_CHUNK = 256
DGRAD_CHUNK = 256
DN_ALPHA = 8.0 ** 0.25
LN_EPS = 1e-5
NEG_INF = -1e30
ADAM_LR, ADAM_B1, ADAM_B2, ADAM_EPS, ADAM_WD, ADAM_STEP = 0.001, 0.9, 0.999, 1e-08, 0.01, 10

N_DEV = 8
MESH = pl.DeviceIdType.MESH
VMEM_LIMIT = 56 * 1024 * 1024
ROW_TILE = 512
WGRAD_TILE_ELEMS = 2 * 1024 * 1024
LANES = 128
BF16_ROWS = 16


def _cp(*sem):
    return pltpu.CompilerParams(dimension_semantics=sem, vmem_limit_bytes=VMEM_LIMIT)


def _row_tile(t):
    return min(ROW_TILE, t)


def _slope(head):
    return 2.0 ** (-8.0 * (head + 1) / N_ATTN_HEADS)


def _sigmoid(x):
    return 1.0 / (1.0 + jnp.exp(-x))


def _dot(a, b):
    return jnp.dot(a, b, preferred_element_type=F32)


def _dot_nt(a, b):
    return lax.dot_general(a, b, (((1,), (1,)), ((), ())), preferred_element_type=F32)


def _dot_tn(a, b):
    return lax.dot_general(a, b, (((0,), (0,)), ((), ())), preferred_element_type=F32)


def _fold_scratch(tm, w):
    return [pltpu.VMEM((tm, LANES), F32)] * (w // LANES)


def _fold_to(dst_ref, val, scrs, d, col0=0):
    tm, w = val.shape
    if d == 1:
        dst_ref[0, :, col0:col0 + w] = val.astype(dst_ref.dtype)
        return
    for cb in range(w // LANES):
        scrs[cb][...] = val[:, cb * LANES:(cb + 1) * LANES]
    for r in range(d):
        for cb in range(w // LANES):
            piece = scrs[cb][pl.ds(r, tm // d, stride=d), :]
            dst_ref[r, :, col0 + cb * LANES:col0 + (cb + 1) * LANES] = piece.astype(dst_ref.dtype)


def _unfold_rows(rows_of, scrs, d, n, w):
    for r in range(d):
        for cb in range(w // LANES):
            scrs[cb][pl.ds(r, n, stride=d), :] = rows_of(r, slice(cb * LANES, (cb + 1) * LANES)).astype(F32)
    return jnp.concatenate([scrs[cb][0:d * n, :] for cb in range(w // LANES)], axis=1)


def _unfold_from(src_ref, scrs, d):
    if d == 1:
        return src_ref[0].astype(F32)
    _, n, w = src_ref.shape
    return _unfold_rows(lambda r, cols: src_ref[r, :, cols], scrs, d, n, w)


def _folded_spec(d, tm, w):
    return pl.BlockSpec((d, tm // d, w), lambda i: (0, i, 0))


def _me():
    return lax.axis_index("x"), lax.axis_index("y"), lax.axis_index("c")


def _flip(v, bit):
    return 1 - v if bit else v


def _peer(k):
    x, y, c = _me()
    return (_flip(x, k & 4), _flip(y, k & 2), _flip(c, k & 1))


def _peer_index(k):
    px, py, pc = _peer(k)
    return 4 * px + 2 * py + pc


def all_gather_small(v, name):
    r, c = v.shape

    def body(v_ref, out_ref, send_sems, recv_sems):
        me = _peer_index(0)
        out_ref[me] = v_ref[...]
        copies = []
        for k in range(1, N_DEV):
            cp = pltpu.make_async_remote_copy(
                src_ref=v_ref, dst_ref=out_ref.at[me],
                send_sem=send_sems.at[k - 1], recv_sem=recv_sems.at[k - 1],
                device_id=_peer(k), device_id_type=MESH)
            cp.start()
            copies.append(cp)
        for k in range(1, N_DEV):
            pltpu.make_async_remote_copy(
                src_ref=v_ref, dst_ref=out_ref.at[_peer_index(k)],
                send_sem=send_sems.at[k - 1], recv_sem=recv_sems.at[k - 1],
                device_id=_peer(k), device_id_type=MESH).wait_recv()
        for cp in copies:
            cp.wait_send()

    return pl.pallas_call(
        body, name=name,
        out_shape=jax.ShapeDtypeStruct((N_DEV, r, c), v.dtype),
        in_specs=[pl.BlockSpec(memory_space=pltpu.VMEM)],
        out_specs=pl.BlockSpec(memory_space=pltpu.VMEM),
        scratch_shapes=[pltpu.SemaphoreType.DMA((N_DEV - 1,)), pltpu.SemaphoreType.DMA((N_DEV - 1,))],
        compiler_params=pltpu.CompilerParams(vmem_limit_bytes=VMEM_LIMIT),
    )(v)


class Piece:
    def __init__(self, name, buf, axis, base, size):
        self.name, self.buf, self.axis, self.base, self.size = name, buf, axis, base, size

    def window(self, ref, j):
        start = self.base + j * self.size
        if self.axis == 1:
            return ref.at[:, pl.ds(pl.multiple_of(start, LANES), self.size)]
        return ref.at[pl.ds(pl.multiple_of(start, BF16_ROWS), self.size), :]


class Exchange:
    def __init__(self, kind, pieces, ins, out_shapes, bufs):
        self.kind, self.pieces, self.ins, self.out_shapes = kind, pieces, list(ins), list(out_shapes)
        self.buf_of = {nm: i for i, nm in enumerate(bufs)}
        self.n_in, self.n_out = len(self.ins), len(self.out_shapes)
        n = len(pieces)
        self.scratch = [pltpu.SemaphoreType.DMA((n, N_DEV - 1)), pltpu.SemaphoreType.DMA((n, N_DEV - 1)),
                        pltpu.SemaphoreType.DMA((n,))]
        self.in_specs = [pl.BlockSpec(memory_space=pl.ANY)] * self.n_in
        self.out_specs = [pl.BlockSpec(memory_space=pl.ANY)] * self.n_out
        self.out_shape = [jax.ShapeDtypeStruct(s, BF16) for s in self.out_shapes]

    def _ends(self, pi, ins, outs, to):
        pc = self.pieces[pi]
        if self.kind == "gather":
            return ins[pi], pc.window(outs[self.buf_of[pc.buf]], _peer_index(0))
        return pc.window(ins[self.buf_of[pc.buf]], to), outs[pi].at[_peer_index(0)]

    def _landing(self, pi, outs, frm):
        pc = self.pieces[pi]
        if self.kind == "gather":
            return pc.window(outs[self.buf_of[pc.buf]], frm)
        return outs[pi].at[frm]

    def _remote(self, pi, k, src, dst, sems):
        return pltpu.make_async_remote_copy(
            src_ref=src, dst_ref=dst, send_sem=sems[0].at[pi, k - 1], recv_sem=sems[1].at[pi, k - 1],
            device_id=_peer(k), device_id_type=MESH)

    def _local(self, pi, ins, outs, sems):
        return pltpu.make_async_copy(*self._ends(pi, ins, outs, _peer_index(0)), sems[2].at[pi])

    def start(self, ins, outs, sems):
        for pi in range(len(self.pieces)):
            self._local(pi, ins, outs, sems).start()
            for k in range(1, N_DEV):
                self._remote(pi, k, *self._ends(pi, ins, outs, _peer_index(k)), sems).start()

    def finish(self, ins, outs, sems):
        for pi in range(len(self.pieces)):
            src_like = self._ends(pi, ins, outs, _peer_index(0))[0]
            for k in range(1, N_DEV):
                self._remote(pi, k, src_like, self._landing(pi, outs, _peer_index(k)), sems).wait_recv()
        for pi in range(len(self.pieces)):
            for k in range(1, N_DEV):
                self._remote(pi, k, *self._ends(pi, ins, outs, _peer_index(k)), sems).wait_send()
            self._local(pi, ins, outs, sems).wait()


def _hosted(ex, refs, n_in, n_out, first, last):
    if ex is None:
        return refs
    ins, rest = refs[:n_in], refs[n_in:]
    ex_ins, rest = rest[:ex.n_in], rest[ex.n_in:]
    outs, rest = rest[:n_out], rest[n_out:]
    ex_outs, rest = rest[:ex.n_out], rest[ex.n_out:]
    scr, sems = rest[:len(rest) - 3], rest[len(rest) - 3:]
    pl.when(first)(lambda: ex.start(ex_ins, ex_outs, sems))
    pl.when(last)(lambda: ex.finish(ex_ins, ex_outs, sems))
    return tuple(ins) + tuple(outs) + tuple(scr)


def _host_call(body, ex, *, name, grid, out_shape, in_specs, out_specs, scratch_shapes=(), sem=None, args):
    n_out = len(out_shape)
    if ex is not None:
        out_shape = list(out_shape) + ex.out_shape
        in_specs = list(in_specs) + ex.in_specs
        out_specs = list(out_specs) + ex.out_specs
        scratch_shapes = list(scratch_shapes) + ex.scratch
        args = list(args) + ex.ins
    res = pl.pallas_call(body, name=name, grid=grid, out_shape=out_shape, in_specs=in_specs, out_specs=out_specs,
                         scratch_shapes=scratch_shapes, compiler_params=_cp(*sem))(*args)
    return res[:n_out], res[n_out:]


def run_exchange(ex, name):
    def body(*refs):
        ins, outs, sems = refs[:ex.n_in], refs[ex.n_in:ex.n_in + ex.n_out], refs[ex.n_in + ex.n_out:]
        ex.start(ins, outs, sems)
        ex.finish(ins, outs, sems)

    return pl.pallas_call(body, name=name, out_shape=ex.out_shape, in_specs=ex.in_specs, out_specs=ex.out_specs,
                          scratch_shapes=ex.scratch)(*ex.ins)


def mod_partial(c_all, w_ada, b_loc):
    nl, dm, wc = w_ada.shape

    def body(c_ref, w_ref, b_ref, o_ref, sc_ref):
        cc = c_ref[...]
        sc = cc * _sigmoid(cc)
        sc_ref[...] = sc
        o_ref[...] = jnp.dot(sc, w_ref[...], preferred_element_type=F32,
                             precision=lax.Precision.HIGHEST) + b_ref[...]

    return pl.pallas_call(
        body, name="mod_partial", grid=(nl,),
        out_shape=[jax.ShapeDtypeStruct((nl, N_DEV, wc), F32), jax.ShapeDtypeStruct((N_DEV, dm), F32)],
        in_specs=[pl.BlockSpec((N_DEV, dm), lambda l: (0, 0)),
                  pl.BlockSpec((None, dm, wc), lambda l: (l, 0, 0)),
                  pl.BlockSpec((None, 1, wc), lambda l: (l, 0, 0))],
        out_specs=[pl.BlockSpec((None, N_DEV, wc), lambda l: (l, 0, 0)),
                   pl.BlockSpec((N_DEV, dm), lambda l: (0, 0))],
        compiler_params=_cp("arbitrary"),
    )(c_all, w_ada, b_loc)


def in_proj(x, s, sh, w, name, ex=None):
    t, dm = x.shape
    n = w.shape[1]
    tm = _row_tile(t)
    nsteps = t // tm
    ch = B_OUT_W
    dils = [dil for _, dil in B_GROUPS if dil > 1]

    def body(*refs):
        i = pl.program_id(0)
        x_ref, s_ref, sh_ref, w_ref, u_ref, *rest = _hosted(ex, refs, 4, 2 + 2 * len(dils), i == 0, i == nsteps - 1)
        uf_refs, o_ref, qf_refs = rest[:len(dils)], rest[len(dils)], rest[len(dils) + 1:len(dils) * 2 + 1]
        scrs = rest[len(dils) * 2 + 1:]
        uf = x_ref[...] * (1.0 + s_ref[...]) + sh_ref[...]
        u = uf.astype(BF16)
        u_ref[...] = u
        for d, uf_ref in zip(dils, uf_refs):
            _fold_to(uf_ref, uf, scrs, d)
        for c0 in range(0, n, ch):
            res = _dot(u, w_ref[:, c0:c0 + ch])
            o_ref[:, c0:c0 + ch] = res.astype(BF16)
            if A_W <= c0 < GATE_COL:
                part, g = divmod((c0 - A_W) // ch, N_GROUPS)
                d = B_GROUPS[g][1]
                if d > 1:
                    _fold_to(qf_refs[dils.index(d)], res, scrs, d, part * ch)

    vec = pl.BlockSpec((1, dm), lambda i: (0, 0))
    row = lambda w_: pl.BlockSpec((tm, w_), lambda i: (i, 0))
    return _host_call(
        body, ex, name=name, grid=(nsteps,),
        out_shape=[jax.ShapeDtypeStruct((t, dm), BF16)]
                  + [jax.ShapeDtypeStruct((d, t // d, dm), BF16) for d in dils]
                  + [jax.ShapeDtypeStruct((t, n), BF16)]
                  + [jax.ShapeDtypeStruct((d, t // d, B_GW), BF16) for d in dils],
        in_specs=[row(dm), vec, vec, pl.BlockSpec((dm, n), lambda i: (0, 0))],
        out_specs=[row(dm)] + [_folded_spec(d, tm, dm) for d in dils] + [row(n)]
                  + [_folded_spec(d, tm, B_GW) for d in dils],
        scratch_shapes=_fold_scratch(tm, dm), sem=("arbitrary",), args=[x, s, sh, w])


def modmm(x, s, sh, w, name, ex=None):
    t, dm = x.shape
    n = w.shape[1]
    tm = _row_tile(t)
    nsteps = t // tm
    ch = 512

    def body(*refs):
        i = pl.program_id(0)
        x_ref, s_ref, sh_ref, w_ref, u_ref, o_ref = _hosted(ex, refs, 4, 2, i == 0, i == nsteps - 1)
        u = (x_ref[...] * (1.0 + s_ref[...]) + sh_ref[...]).astype(BF16)
        u_ref[...] = u
        for c0 in range(0, n, ch):
            o_ref[:, c0:c0 + ch] = _dot(u, w_ref[:, c0:c0 + ch]).astype(BF16)

    vec = pl.BlockSpec((1, dm), lambda i: (0, 0))
    return _host_call(
        body, ex, name=name, grid=(nsteps,),
        out_shape=[jax.ShapeDtypeStruct((t, dm), BF16), jax.ShapeDtypeStruct((t, n), BF16)],
        in_specs=[pl.BlockSpec((tm, dm), lambda i: (i, 0)), vec, vec,
                  pl.BlockSpec((dm, n), lambda i: (0, 0))],
        out_specs=[pl.BlockSpec((tm, dm), lambda i: (i, 0)), pl.BlockSpec((tm, n), lambda i: (i, 0))],
        sem=("arbitrary",), args=[x, s, sh, w])


def _halves(tm):
    half = tm // 2 if tm % 32 == 0 else tm
    return [slice(r0, r0 + half) for r0 in range(0, tm, half)]


def _ln_store(y, rows, xres_ref, g_ref, lg_ref, lb_ref, y_ref, xo_ref, zh_ref, rs_ref):
    y_ref[rows, :] = y.astype(BF16)
    z = DN_ALPHA * xres_ref[rows, :] + g_ref[...] * y
    mu = jnp.mean(z, axis=1, keepdims=True)
    zc = z - mu
    var = jnp.mean(zc * zc, axis=1, keepdims=True)
    rstd = lax.rsqrt(var + LN_EPS)
    zhat = zc * rstd
    zh_ref[rows, :] = zhat
    xo_ref[rows, :] = zhat * lg_ref[...] + lb_ref[...]
    rs_ref[rows, :] = jnp.broadcast_to(rstd, (zhat.shape[0], rs_ref.shape[1]))


def _ln_out_shapes(t, dm):
    return [jax.ShapeDtypeStruct((t, dm), BF16), jax.ShapeDtypeStruct((t, dm), F32),
            jax.ShapeDtypeStruct((t, dm), F32), jax.ShapeDtypeStruct((t, LANES), F32)]


def _ln_out_specs(tm, dm):
    row = pl.BlockSpec((tm, dm), lambda i: (i, 0))
    return [row, row, row, pl.BlockSpec((tm, LANES), lambda i: (i, 0))]


def proj_ln(a, w, xres, gate, lg, lb, name):
    t, k = a.shape
    dm = w.shape[1]
    tm = _row_tile(t)

    def body(a_ref, w_ref, xres_ref, g_ref, lg_ref, lb_ref, y_ref, xo_ref, zh_ref, rs_ref):
        for rows in _halves(tm):
            y = _dot(a_ref[rows, :], w_ref[...])
            _ln_store(y, rows, xres_ref, g_ref, lg_ref, lb_ref, y_ref, xo_ref, zh_ref, rs_ref)

    vec = pl.BlockSpec((1, dm), lambda i: (0, 0))
    return pl.pallas_call(
        body, name=name, grid=(t // tm,),
        out_shape=_ln_out_shapes(t, dm),
        in_specs=[pl.BlockSpec((tm, k), lambda i: (i, 0)), pl.BlockSpec((k, dm), lambda i: (0, 0)),
                  pl.BlockSpec((tm, dm), lambda i: (i, 0)), vec, vec, vec],
        out_specs=_ln_out_specs(tm, dm),
        compiler_params=_cp("parallel"),
    )(a, w, xres, gate, lg, lb)


def swiglu_proj_ln(ab, w, xres, gate, lg, lb, name):
    t = ab.shape[0]
    f, dm = w.shape
    tm = _row_tile(t)

    def body(a_ref, b_ref, w_ref, xres_ref, g_ref, lg_ref, lb_ref, h_ref, y_ref, xo_ref, zh_ref, rs_ref):
        for rows in _halves(tm):
            y = None
            for c0 in range(0, f, FF_CHUNK):
                cols = slice(c0, c0 + FF_CHUNK)
                a = a_ref[rows, cols].astype(F32)
                h = (a * _sigmoid(a) * b_ref[rows, cols].astype(F32)).astype(BF16)
                h_ref[rows, cols] = h
                part = _dot(h, w_ref[cols, :])
                y = part if y is None else y + part
            _ln_store(y, rows, xres_ref, g_ref, lg_ref, lb_ref, y_ref, xo_ref, zh_ref, rs_ref)

    vec = pl.BlockSpec((1, dm), lambda i: (0, 0))
    return pl.pallas_call(
        body, name=name, grid=(t // tm,),
        out_shape=[jax.ShapeDtypeStruct((t, f), BF16)] + _ln_out_shapes(t, dm),
        in_specs=[pl.BlockSpec((tm, f), lambda i: (i, 0)), pl.BlockSpec((tm, f), lambda i: (i, 1)),
                  pl.BlockSpec((f, dm), lambda i: (0, 0)),
                  pl.BlockSpec((tm, dm), lambda i: (i, 0)), vec, vec, vec],
        out_specs=[pl.BlockSpec((tm, f), lambda i: (i, 0))] + _ln_out_specs(tm, dm),
        compiler_params=_cp("parallel"),
    )(ab, ab, w, xres, gate, lg, lb)


class AttnCfg:
    def __init__(self, dil, heads, kv_heads, qc, kc, vc, max_dist, head0, sinks):
        self.dil, self.heads, self.kv_heads = dil, heads, kv_heads
        self.qc, self.kc, self.vc = qc, kc, vc
        self.max_dist, self.head0, self.sinks = max_dist, head0, sinks
        self.wq = heads * HEAD_DIM
        self.wk = kv_heads * HEAD_DIM
        self.wout = self.wq + 2 * self.wk


ATTN_A = AttnCfg(1, A_Q_HEADS, A_KV_HEADS, 0, A_Q_HEADS * HEAD_DIM, (A_Q_HEADS + A_KV_HEADS) * HEAD_DIM,
                 A_WINDOW - 1, 0, True)


def _attn_b_cfg(g):
    win, dil = B_GROUPS[g]
    cols = ((A_W + g * B_OUT_W, A_W + B_ALL + g * B_OUT_W, A_W + 2 * B_ALL + g * B_OUT_W) if dil == 1
            else (0, B_OUT_W, 2 * B_OUT_W))
    return AttnCfg(dil, B_HEADS_PER_GROUP, B_HEADS_PER_GROUP, *cols, win // dil,
                   A_Q_HEADS + g * B_HEADS_PER_GROUP, False)


ATTN_B = [_attn_b_cfg(g) for g in range(N_GROUPS)]


SCALE = HEAD_DIM ** -0.5


def _head(h):
    return slice(h * HEAD_DIM, (h + 1) * HEAD_DIM)


def _stack(parts):
    return parts[0] if len(parts) == 1 else jnp.concatenate(parts, axis=0)


def _masked_bias(mask, distf, cfg, h, d):
    return jnp.where(mask, distf * (-(_slope(cfg.head0 + h) * d)), NEG_INF)


def _band(i, max_dist):
    qi = lax.broadcasted_iota(jnp.int32, (BLOCK, 2 * BLOCK), 0)
    sj = lax.broadcasted_iota(jnp.int32, (BLOCK, 2 * BLOCK), 1)
    dist = qi + BLOCK - sj
    valid = (dist >= 0) & (dist <= max_dist)
    first_key = jnp.where(i > 0, 0, BLOCK)
    valid_first = valid & (sj >= first_key)
    return dist, valid, valid_first


def _attn_geometry(cfg, n):
    tq = min(512, n)
    return tq, tq // BLOCK, n // tq


def attn_fwd(qkv, cfg, sinks, name):
    d, n, _ = qkv.shape
    tq, nsub, nqb = _attn_geometry(cfg, n)
    wq, wk = cfg.wq, cfg.wk
    grp = cfg.heads // cfg.kv_heads

    def body(sink_ref, q_ref, kc_ref, kp_ref, vc_ref, vp_ref, o_ref, l_ref, kf, vf):
        i = pl.program_id(1)
        kf[0:BLOCK, :] = kp_ref[...]
        kf[BLOCK:, :] = kc_ref[...]
        vf[0:BLOCK, :] = vp_ref[...]
        vf[BLOCK:, :] = vc_ref[...]
        dist, valid, valid_first = _band(i, cfg.max_dist)
        distf = dist.astype(F32)
        rows = [slice(a * BLOCK, (a + 1) * BLOCK) for a in range(nsub)]
        wins = [slice(a * BLOCK, (a + 2) * BLOCK) for a in range(nsub)]
        for head in range(cfg.heads):
            heads = [head]
            ks = _head(head // grp)
            b_reg = _stack([_masked_bias(valid, distf, cfg, h, d) for h in heads])
            b_first = _stack([_masked_bias(valid_first, distf, cfg, h, d) for h in heads])
            if cfg.sinks:
                sink = _stack([jnp.full((BLOCK, 1), sink_ref[h], F32) for h in heads])
            ss = [_dot_nt(_stack([q_ref[rows[a], _head(h)] for h in heads]) * SCALE, kf[wins[a], ks])
                  + (b_first if a == 0 else b_reg) for a in range(nsub)]
            es, invs = [], []
            for a in range(nsub):
                m = jnp.max(ss[a], axis=1, keepdims=True)
                if cfg.sinks:
                    m = jnp.maximum(m, sink)
                e = jnp.exp(ss[a] - m)
                den = jnp.sum(e, axis=1, keepdims=True)
                if cfg.sinks:
                    den = den + jnp.exp(sink - m)
                es.append(e.astype(BF16))
                invs.append(1.0 / den)
                lse = m + jnp.log(den)
                for g, h in enumerate(heads):
                    l_ref[rows[a], _head(h)] = jnp.broadcast_to(lse[g * BLOCK:(g + 1) * BLOCK], (BLOCK, HEAD_DIM))
            for a in range(nsub):
                o = _dot(es[a], vf[wins[a], ks]) * invs[a]
                for g, h in enumerate(heads):
                    o_ref[rows[a], _head(h)] = o[g * BLOCK:(g + 1) * BLOCK].astype(BF16)

    prev = lambda i: jnp.maximum(i * nsub - 1, 0)
    cur = lambda w, c: pl.BlockSpec((None, tq, w), lambda r, i: (r, i, c // w))
    prv = lambda w, c: pl.BlockSpec((None, BLOCK, w), lambda r, i: (r, prev(i), c // w))
    out = pl.BlockSpec((None, tq, wq), lambda r, i: (r, i, 0))
    return pl.pallas_call(
        body, name=name, grid=(d, nqb),
        out_shape=[jax.ShapeDtypeStruct((d, n, wq), BF16), jax.ShapeDtypeStruct((d, n, wq), F32)],
        in_specs=[pl.BlockSpec(memory_space=pltpu.SMEM),
                  cur(wq, cfg.qc), cur(wk, cfg.kc), prv(wk, cfg.kc), cur(wk, cfg.vc), prv(wk, cfg.vc)],
        out_specs=[out, out],
        scratch_shapes=[pltpu.VMEM((tq + BLOCK, wk), BF16), pltpu.VMEM((tq + BLOCK, wk), BF16)],
        compiler_params=_cp("parallel", "parallel"),
    )(sinks, qkv, qkv, qkv, qkv, qkv)


def mix_merge(ya, o_g, l_g, proj, w_a, w_b, name):
    t = ya.shape[0]
    dm = w_a.shape[1]
    tm = _row_tile(t)
    gcol = GATE_COL // dm
    dils = [o.shape[0] for o in o_g]

    def body(ya_ref, o0, o1, o2, l0, l1, l2, ga_ref, gb_ref, wa_ref, wb_ref, yb_ref, mg_ref, *scrs):
        ls = [_unfold_from(l, scrs, d) for l, d in zip((l0, l1, l2), dils)]
        m = jnp.maximum(jnp.maximum(ls[0], ls[1]), ls[2])
        es = [jnp.exp(l - m) for l in ls]
        inv = 1.0 / (es[0] + es[1] + es[2])
        yb = sum(_unfold_from(o, scrs, d) * (e * inv) for o, e, d in zip((o0, o1, o2), es, dils)).astype(BF16)
        yb_ref[...] = yb
        pa = _dot(ya_ref[...], wa_ref[...])
        pb = _dot(yb, wb_ref[...])
        mg = _sigmoid(ga_ref[...].astype(F32)) * pa + _sigmoid(gb_ref[...].astype(F32)) * pb
        mg_ref[...] = mg.astype(BF16)

    wide = lambda w: pl.BlockSpec((tm, w), lambda i: (i, 0))
    folded = [_folded_spec(d, tm, B_OUT_W) for d in dils]
    return pl.pallas_call(
        body, name=name, grid=(t // tm,),
        out_shape=[jax.ShapeDtypeStruct((t, B_OUT_W), BF16), jax.ShapeDtypeStruct((t, dm), BF16)],
        in_specs=[wide(ya.shape[1])] + folded + folded
                 + [pl.BlockSpec((tm, dm), lambda i: (i, gcol)), pl.BlockSpec((tm, dm), lambda i: (i, gcol + 1)),
                    pl.BlockSpec(w_a.shape, lambda i: (0, 0)), pl.BlockSpec(w_b.shape, lambda i: (0, 0))],
        out_specs=[wide(B_OUT_W), wide(dm)],
        scratch_shapes=_fold_scratch(tm, B_OUT_W),
        compiler_params=_cp("parallel"),
    )(ya, *o_g, *l_g, proj, proj, w_a, w_b)


def loss_head(y, target):
    t, dm = y.shape
    tm = _row_tile(t)

    def body(y_ref, t_ref, dy_ref, loss_ref):
        @pl.when(pl.program_id(0) == 0)
        def _():
            loss_ref[...] = jnp.zeros_like(loss_ref)
        err = y_ref[...] - t_ref[...]
        dy_ref[...] = err * (1.0 / dm)
        per_row = jnp.sum(err * err, axis=1, keepdims=True) * (1.0 / dm)
        loss_ref[...] += 0.5 * jnp.sum(per_row, axis=0, keepdims=True)

    row = pl.BlockSpec((tm, dm), lambda i: (i, 0))
    return pl.pallas_call(
        body, name="loss_head", grid=(t // tm,),
        out_shape=[jax.ShapeDtypeStruct((t, dm), F32), jax.ShapeDtypeStruct((8, LANES), F32)],
        in_specs=[row, row],
        out_specs=[row, pl.BlockSpec((8, LANES), lambda i: (0, 0))],
        compiler_params=_cp("arbitrary"),
    )(y, target)


def _fold_rows(v):
    tm, c = v.shape
    return jnp.sum(v.reshape(tm // 8, 8, c), axis=0)


def _finish_sums(refs, nsteps):
    @pl.when(pl.program_id(0) == nsteps - 1)
    def _():
        for r in refs:
            r[...] = jnp.broadcast_to(jnp.sum(r[...], axis=0, keepdims=True), r.shape)


def ln_bwd(dxo, zhat, rstd, ysub, lg, gate, name):
    t, dm = dxo.shape
    tm = _row_tile(t)

    def body(dxo_ref, zh_ref, rs_ref, y_ref, lg_ref, g_ref, dz_ref, dy_ref, sg_ref, sb_ref, sgate_ref):
        @pl.when(pl.program_id(0) == 0)
        def _():
            for r in (sg_ref, sb_ref, sgate_ref):
                r[...] = jnp.zeros_like(r)
        dxo_v = dxo_ref[...]
        zh = zh_ref[...]
        dxh = dxo_v * lg_ref[...]
        m1 = jnp.mean(dxh, axis=1, keepdims=True)
        m2 = jnp.mean(dxh * zh, axis=1, keepdims=True)
        dz = rs_ref[:, 0:1] * (dxh - m1 - zh * m2)
        dz_ref[...] = dz
        dy_ref[...] = (g_ref[...] * dz).astype(BF16)
        sg_ref[...] += _fold_rows(dxo_v * zh)
        sb_ref[...] += _fold_rows(dxo_v)
        sgate_ref[...] += _fold_rows(dz * y_ref[...].astype(F32))
        _finish_sums((sg_ref, sb_ref, sgate_ref), t // tm)

    row = pl.BlockSpec((tm, dm), lambda i: (i, 0))
    vec = pl.BlockSpec((1, dm), lambda i: (0, 0))
    acc = pl.BlockSpec((8, dm), lambda i: (0, 0))
    return pl.pallas_call(
        body, name=name, grid=(t // tm,),
        out_shape=[jax.ShapeDtypeStruct((t, dm), F32), jax.ShapeDtypeStruct((t, dm), BF16)]
                  + [jax.ShapeDtypeStruct((8, dm), F32)] * 3,
        in_specs=[row, row, pl.BlockSpec((tm, LANES), lambda i: (i, 0)), row, vec, vec],
        out_specs=[row, row, acc, acc, acc],
        compiler_params=_cp("arbitrary"),
    )(dxo, zhat, rstd, ysub, lg, gate)


def _mod_bwd_store(du_of, dz_ref, x_ref, s_ref, dx_ref, ss_ref, ssh_ref, nsteps):
    @pl.when(pl.program_id(0) == 0)
    def _():
        ss_ref[...] = jnp.zeros_like(ss_ref)
        ssh_ref[...] = jnp.zeros_like(ssh_ref)
    for c0 in range(0, dx_ref.shape[1], DGRAD_CHUNK):
        cols = slice(c0, c0 + DGRAD_CHUNK)
        du = du_of(cols)
        dx_ref[:, cols] = DN_ALPHA * dz_ref[:, cols] + du * (1.0 + s_ref[:, cols])
        ss_ref[:, cols] += _fold_rows(du * x_ref[:, cols])
        ssh_ref[:, cols] += _fold_rows(du)
    _finish_sums((ss_ref, ssh_ref), nsteps)


def dgrad_ffn(g, wt, dz, xin, s, name, ex=None):
    t, dm = dz.shape
    k = g.shape[1]
    tm = _row_tile(t)
    nsteps = t // tm

    def body(*refs):
        i = pl.program_id(0)
        g_ref, w_ref, dz_ref, x_ref, s_ref, dx_ref, ss_ref, ssh_ref = _hosted(ex, refs, 5, 3, i == 0, i == nsteps - 1)
        g_v = g_ref[...]
        _mod_bwd_store(lambda cols: _dot(g_v, w_ref[:, cols]), dz_ref, x_ref, s_ref, dx_ref, ss_ref, ssh_ref, nsteps)

    row = pl.BlockSpec((tm, dm), lambda i: (i, 0))
    acc = pl.BlockSpec((8, dm), lambda i: (0, 0))
    return _host_call(
        body, ex, name=name, grid=(nsteps,),
        out_shape=[jax.ShapeDtypeStruct((t, dm), F32)] + [jax.ShapeDtypeStruct((8, dm), F32)] * 2,
        in_specs=[pl.BlockSpec((tm, k), lambda i: (i, 0)), pl.BlockSpec((k, dm), lambda i: (0, 0)),
                  row, row, pl.BlockSpec((1, dm), lambda i: (0, 0))],
        out_specs=[row, acc, acc], sem=("arbitrary",), args=[g, wt, dz, xin, s])


def dgrad_in(d_a, d_b, dgab, wt, dz, xin, s, name, ex=None):
    t, dm = dz.shape
    tm = _row_tile(t)
    nsteps = t // tm
    dils = [a.shape[0] for a in d_b]

    def body(*refs):
        i = pl.program_id(0)
        (da_ref, b0, b1, b2, dg_ref, w_ref, dz_ref, x_ref, s_ref, dx_ref, ss_ref, ssh_ref,
         *scrs) = _hosted(ex, refs, 9, 3, i == 0, i == nsteps - 1)
        vs = [b_ref[...].reshape(tm, B_GW) for b_ref in (b0, b1, b2)]

        def du_of(cols):
            du = _dot(da_ref[0], w_ref[0:A_W, cols])
            for g, (v, d) in enumerate(zip(vs, dils)):
                part = None
                for p in range(3):
                    r0 = A_W + p * B_ALL + g * B_OUT_W
                    term = _dot(v[:, p * B_OUT_W:(p + 1) * B_OUT_W], w_ref[r0:r0 + B_OUT_W, cols])
                    part = term if part is None else part + term
                if d == 1:
                    du = du + part
                else:
                    n = tm // d
                    du = du + _unfold_rows(lambda r, cs: part[r * n:(r + 1) * n, cs], scrs, d, n, DGRAD_CHUNK)
            for j in range(2):
                du = du + _dot(dg_ref[:, j * dm:(j + 1) * dm], w_ref[GATE_COL + j * dm:GATE_COL + (j + 1) * dm, cols])
            return du

        _mod_bwd_store(du_of, dz_ref, x_ref, s_ref, dx_ref, ss_ref, ssh_ref, nsteps)

    row = pl.BlockSpec((tm, dm), lambda i: (i, 0))
    acc = pl.BlockSpec((8, dm), lambda i: (0, 0))
    return _host_call(
        body, ex, name=name, grid=(nsteps,),
        out_shape=[jax.ShapeDtypeStruct((t, dm), F32)] + [jax.ShapeDtypeStruct((8, dm), F32)] * 2,
        in_specs=[_folded_spec(1, tm, A_W)] + [_folded_spec(d, tm, B_GW) for d in dils]
                 + [pl.BlockSpec((tm, 2 * dm), lambda i: (i, 0)), pl.BlockSpec(wt.shape, lambda i: (0, 0)),
                    row, row, pl.BlockSpec((1, dm), lambda i: (0, 0))],
        out_specs=[row, acc, acc],
        scratch_shapes=_fold_scratch(tm, DGRAD_CHUNK), sem=("arbitrary",), args=[d_a, *d_b, dgab, wt, dz, xin, s])


def wgrad(a, b, buf, tn, nj, b0, o0, om, name):
    t, k = a.shape
    tt = ROW_TILE
    while tt * 2 * k <= WGRAD_TILE_ELEMS and tt * 2 <= t:
        tt *= 2
    nsteps = t // tt

    def body(a_ref, b_ref, buf_ref, o_ref, acc):
        s = pl.program_id(1)

        @pl.when(s == 0)
        def _():
            acc[...] = jnp.zeros_like(acc)
        acc[...] += _dot_tn(a_ref[...], b_ref[...])

        @pl.when(s == nsteps - 1)
        def _():
            o_ref[...] = acc[...].astype(BF16)

    return pl.pallas_call(
        body, name=name, grid=(nj, nsteps),
        out_shape=jax.ShapeDtypeStruct(buf.shape, buf.dtype),
        in_specs=[pl.BlockSpec((tt, k), lambda j, s: (s, 0)),
                  pl.BlockSpec((tt, tn), lambda j, s: (s, b0 + j)),
                  pl.BlockSpec(memory_space=pl.ANY)],
        out_specs=pl.BlockSpec((k, tn), lambda j, s: (0, o0 + om * j)),
        scratch_shapes=[pltpu.VMEM((k, tn), F32)],
        input_output_aliases={2: 0},
        compiler_params=_cp("parallel", "arbitrary"),
    )(a, b, buf)


def dswiglu(dy, wdt, ab, name, ex=None):
    t, dm = dy.shape
    f = wdt.shape[1]
    tm = _row_tile(t)
    nsteps = t // tm

    def body(*refs):
        i = pl.program_id(0)
        dy_ref, w_ref, a_ref, b_ref, o_ref = _hosted(ex, refs, 4, 1, i == 0, i == nsteps - 1)
        dy_v = dy_ref[...]
        for c0 in range(0, f, FF_CHUNK):
            cols = slice(c0, c0 + FF_CHUNK)
            dh = _dot(dy_v, w_ref[:, cols])
            a = a_ref[:, cols].astype(F32)
            sg = _sigmoid(a)
            o_ref[:, cols] = (dh * b_ref[:, cols].astype(F32) * (sg * (1.0 + a * (1.0 - sg)))).astype(BF16)
            o_ref[:, f + c0:f + c0 + FF_CHUNK] = (dh * (a * sg)).astype(BF16)

    return _host_call(
        body, ex, name=name, grid=(nsteps,),
        out_shape=[jax.ShapeDtypeStruct((t, 2 * f), BF16)],
        in_specs=[pl.BlockSpec((tm, dm), lambda i: (i, 0)), pl.BlockSpec((dm, f), lambda i: (0, 0)),
                  pl.BlockSpec((tm, f), lambda i: (i, 0)), pl.BlockSpec((tm, f), lambda i: (i, 1))],
        out_specs=[pl.BlockSpec((tm, 2 * f), lambda i: (i, 0))], sem=("arbitrary",), args=[dy, wdt, ab, ab])


def dmerge(do, wot, ya, yb, w_a, w_b, wat, wbt, proj, name):
    t, dm = do.shape
    tm = _row_tile(t)
    gcol = GATE_COL // dm

    def body(do_ref, wot_ref, ya_ref, yb_ref, wa_ref, wb_ref, wat_ref, wbt_ref, g_ref,
             dp_ref, dya_ref, dyb_ref, dg_ref, dm_scr):
        j = pl.program_id(1)

        ch = 256

        @pl.when(j == 0)
        def _():
            do_v = do_ref[...]
            for c0 in range(0, dm, ch):
                dm_scr[:, c0:c0 + ch] = _dot(do_v, wot_ref[:, c0:c0 + ch])

        def branch(y_ref, w_ref, wt_ref, dy_ref):
            y = y_ref[...]
            dy = None
            for c0 in range(0, dm, ch):
                cols = slice(c0, c0 + ch)
                p = _dot(y, w_ref[:, cols])
                sg = _sigmoid(g_ref[:, cols].astype(F32))
                dmg = dm_scr[:, cols]
                dp = (dmg * sg).astype(BF16)
                dp_ref[:, cols] = dp
                dg_ref[:, cols] = (dmg * p * (sg * (1.0 - sg))).astype(BF16)
                part = _dot(dp, wt_ref[cols, :])
                dy = part if dy is None else dy + part
            dy_ref[...] = dy.astype(dy_ref.dtype)

        pl.when(j == 0)(lambda: branch(ya_ref, wa_ref, wat_ref, dya_ref))
        pl.when(j == 1)(lambda: branch(yb_ref, wb_ref, wbt_ref, dyb_ref))

    full = lambda arr: pl.BlockSpec(arr.shape, lambda i, j: (0, 0))
    rowc = lambda w: pl.BlockSpec((tm, w), lambda i, j: (i, 0))
    return pl.pallas_call(
        body, name=name, grid=(t // tm, 2),
        out_shape=[jax.ShapeDtypeStruct((t, 2 * dm), BF16), jax.ShapeDtypeStruct((t, ya.shape[1]), BF16),
                   jax.ShapeDtypeStruct((t, yb.shape[1]), F32), jax.ShapeDtypeStruct((t, 2 * dm), BF16)],
        in_specs=[rowc(dm), full(wot), rowc(ya.shape[1]), rowc(yb.shape[1]), full(w_a), full(w_b),
                  full(wat), full(wbt), pl.BlockSpec((tm, dm), lambda i, j: (i, gcol + j))],
        out_specs=[pl.BlockSpec((tm, dm), lambda i, j: (i, j)), rowc(ya.shape[1]), rowc(yb.shape[1]),
                   pl.BlockSpec((tm, dm), lambda i, j: (i, j))],
        scratch_shapes=[pltpu.VMEM((tm, dm), F32)],
        compiler_params=_cp("parallel", "arbitrary"),
    )(do, wot, ya, yb, w_a, w_b, wat, wbt, proj)


def mix_bwd(dyb, o_g, l_g, name):
    t, w = dyb.shape
    tm = _row_tile(t)
    nh = w // HEAD_DIM
    dils = [o.shape[0] for o in o_g]

    def body(dyb_ref, o0, o1, o2, l0, l1, l2, do0, do1, do2, dl0, dl1, dl2, *scr):
        ls = [_unfold_from(l, scr, d) for l, d in zip((l0, l1, l2), dils)]
        m = jnp.maximum(jnp.maximum(ls[0], ls[1]), ls[2])
        es = [jnp.exp(l - m) for l in ls]
        inv = 1.0 / (es[0] + es[1] + es[2])
        wts = [e * inv for e in es]
        dyb_v = dyb_ref[...]
        dws = []
        for o_ref, do_ref, wt, d in zip((o0, o1, o2), (do0, do1, do2), wts, dils):
            prod = dyb_v * _unfold_from(o_ref, scr, d)
            _fold_to(do_ref, dyb_v * wt, scr, d)
            for h in range(nh):
                hs = slice(h * HEAD_DIM, (h + 1) * HEAD_DIM)
                dws.append(jnp.broadcast_to(jnp.sum(prod[:, hs], axis=1, keepdims=True), (tm, HEAD_DIM)))
        for g, (dl_ref, d) in enumerate(zip((dl0, dl1, dl2), dils)):
            cols = []
            for h in range(nh):
                hs = slice(h * HEAD_DIM, (h + 1) * HEAD_DIM)
                mean = sum(wts[g2][:, hs] * dws[g2 * nh + h] for g2 in range(N_GROUPS))
                cols.append(wts[g][:, hs] * (dws[g * nh + h] - mean))
            _fold_to(dl_ref, jnp.concatenate(cols, axis=1), scr, d)

    folded = [_folded_spec(d, tm, w) for d in dils]
    return pl.pallas_call(
        body, name=name, grid=(t // tm,),
        out_shape=[jax.ShapeDtypeStruct(o.shape, BF16) for o in o_g]
                  + [jax.ShapeDtypeStruct(o.shape, F32) for o in o_g],
        in_specs=[pl.BlockSpec((tm, w), lambda i: (i, 0))] + folded + folded,
        out_specs=folded + folded,
        scratch_shapes=_fold_scratch(tm, w),
        compiler_params=_cp("parallel"),
    )(dyb, *o_g, *l_g)


def attn_bwd(qkv, o, lse, do, dlse, cfg, sinks, name):
    d, n, _ = qkv.shape
    tq, nsub, nqb = _attn_geometry(cfg, n)
    wq, wk, wout = cfg.wq, cfg.wk, cfg.wout
    grp = cfg.heads // cfg.kv_heads
    has_dl = dlse is not None
    scale = HEAD_DIM ** -0.5

    def body(*refs):
        sink_ref, q_ref, qn_ref, kc_ref, kp_ref, vc_ref, vp_ref = refs[:7]
        o_ref, on_ref, do_ref, don_ref, l_ref, ln_ref = refs[7:13]
        rest = refs[13:]
        dl_ref = dln_ref = None
        if has_dl:
            dl_ref, dln_ref = rest[:2]
            rest = rest[2:]
        out_ref = rest[0]
        rest = rest[1:]
        if cfg.sinks:
            dsink_ref = rest[0]
            rest = rest[1:]
        kf, vf, dk_acc, dv_acc = rest
        r, i = pl.program_id(0), pl.program_id(1)
        kf[0:BLOCK, :] = kp_ref[...]
        kf[BLOCK:, :] = kc_ref[...]
        vf[0:BLOCK, :] = vp_ref[...]
        vf[BLOCK:, :] = vc_ref[...]
        dist, valid, valid_first = _band(i, cfg.max_dist)
        distf = dist.astype(F32)
        next_dist = jnp.where(i < nqb - 1, cfg.max_dist, -1)
        valid_next = (dist[:, 0:BLOCK] >= 0) & (dist[:, 0:BLOCK] <= next_dist)
        if cfg.sinks:
            @pl.when((r == 0) & (i == 0))
            def _():
                dsink_ref[...] = jnp.zeros_like(dsink_ref)

        def stacked(ref, rows, heads):
            return _stack([ref[rows, _head(h)] for h in heads])

        def per_row(ref, rows, heads):
            return _stack([jnp.max(ref[rows, _head(h)], axis=1, keepdims=True) for h in heads])

        for kv in range(cfg.kv_heads):
            heads = [kv * grp + g for g in range(grp)]
            ks = slice(kv * HEAD_DIM, (kv + 1) * HEAD_DIM)
            dk_acc[...] = jnp.zeros_like(dk_acc)
            dv_acc[...] = jnp.zeros_like(dv_acc)
            biases = [_stack([_masked_bias(m, dd, cfg, h, d) for h in heads])
                      for m, dd in ((valid_first, distf), (valid, distf), (valid_next, distf[:, 0:BLOCK]))]
            tiles = []
            for a in range(nsub + 1):
                if a < nsub:
                    rows, win = slice(a * BLOCK, (a + 1) * BLOCK), slice(a * BLOCK, (a + 2) * BLOCK)
                    src = (q_ref, o_ref, do_ref, l_ref, dl_ref)
                else:
                    rows, win = slice(0, BLOCK), slice(nsub * BLOCK, (nsub + 1) * BLOCK)
                    src = (qn_ref, on_ref, don_ref, ln_ref, dln_ref)
                qs = stacked(src[0], rows, heads) * SCALE
                do_v = stacked(src[2], rows, heads)
                delta = jnp.sum(do_v.astype(F32) * stacked(src[1], rows, heads).astype(F32), axis=1, keepdims=True)
                lse_v = per_row(src[3], rows, heads)
                shift = (per_row(src[4], rows, heads) - delta) if has_dl else -delta
                k = kf[win, ks]
                s = _dot_nt(qs, k) + biases[0 if a == 0 else (1 if a < nsub else 2)]
                dp = _dot_nt(do_v, vf[win, ks])
                tiles.append((qs, do_v, k, delta, lse_v, shift, s, dp))
            grads = []
            for qs, do_v, k, delta, lse_v, shift, s, dp in tiles:
                p = jnp.exp(s - lse_v)
                grads.append(((p * (dp + shift)).astype(BF16), p.astype(BF16)))
            for a, ((qs, do_v, k, delta, lse_v, shift, _, _), (dsb, pb)) in enumerate(zip(tiles, grads)):
                if a < nsub:
                    dq = _dot(dsb, k) * SCALE
                    for g, h in enumerate(heads):
                        out_ref[a * BLOCK:(a + 1) * BLOCK, _head(h)] = dq[g * BLOCK:(g + 1) * BLOCK].astype(BF16)
                if a == 0:
                    kcols = slice(0, BLOCK)
                    dsb, pb = dsb[:, BLOCK:], pb[:, BLOCK:]
                elif a < nsub:
                    kcols = slice((a - 1) * BLOCK, (a + 1) * BLOCK)
                else:
                    kcols = slice((nsub - 1) * BLOCK, nsub * BLOCK)
                dk_acc[:, kcols] += _dot_tn(qs, dsb)
                dv_acc[:, kcols] += _dot_tn(do_v, pb)
                if cfg.sinks and a < nsub:
                    for g, h in enumerate(heads):
                        part = slice(g * BLOCK, (g + 1) * BLOCK)
                        psink = jnp.exp(sink_ref[h] - lse_v[part])
                        tot = jnp.sum(psink * (-delta[part]), axis=0, keepdims=True)
                        dsink_ref[h:h + 1, :] += jnp.broadcast_to(tot, (1, LANES))
            out_ref[:, wq + kv * HEAD_DIM: wq + (kv + 1) * HEAD_DIM] = dk_acc[...].T.astype(BF16)
            out_ref[:, wq + wk + kv * HEAD_DIM: wq + wk + (kv + 1) * HEAD_DIM] = dv_acc[...].T.astype(BF16)

    prev = lambda i: jnp.maximum(i * nsub - 1, 0)
    nxt = lambda i: jnp.minimum((i + 1) * nsub, n // BLOCK - 1)
    cur = lambda w, c: pl.BlockSpec((None, tq, w), lambda r, i: (r, i, c // w))
    prv = lambda w, c: pl.BlockSpec((None, BLOCK, w), lambda r, i: (r, prev(i), c // w))
    o_cur = pl.BlockSpec((None, tq, wq), lambda r, i: (r, i, 0))
    o_nxt = pl.BlockSpec((None, BLOCK, wq), lambda r, i: (r, nxt(i), 0))
    in_specs = [pl.BlockSpec(memory_space=pltpu.SMEM),
                cur(wq, cfg.qc), pl.BlockSpec((None, BLOCK, wq), lambda r, i: (r, nxt(i), cfg.qc // wq)),
                cur(wk, cfg.kc), prv(wk, cfg.kc), cur(wk, cfg.vc), prv(wk, cfg.vc),
                o_cur, o_nxt, o_cur, o_nxt, o_cur, o_nxt]
    args = [sinks, qkv, qkv, qkv, qkv, qkv, qkv, o, o, do, do, lse, lse]
    if has_dl:
        in_specs += [o_cur, o_nxt]
        args += [dlse, dlse]
    out_shape = [jax.ShapeDtypeStruct((d, n, wout), BF16)]
    out_specs = [pl.BlockSpec((None, tq, wout), lambda r, i: (r, i, 0))]
    if cfg.sinks:
        out_shape.append(jax.ShapeDtypeStruct((8, LANES), F32))
        out_specs.append(pl.BlockSpec((8, LANES), lambda r, i: (0, 0)))
    return pl.pallas_call(
        body, name=name, grid=(d, nqb), out_shape=out_shape, in_specs=in_specs, out_specs=out_specs,
        scratch_shapes=[pltpu.VMEM((tq + BLOCK, wk), BF16), pltpu.VMEM((tq + BLOCK, wk), BF16),
                        pltpu.VMEM((HEAD_DIM, tq), F32), pltpu.VMEM((HEAD_DIM, tq), F32)],
        compiler_params=_cp("arbitrary", "arbitrary"),
    )(*args)


def _adamw(g, w, m, v):
    m = ADAM_B1 * m + (1.0 - ADAM_B1) * g
    v = ADAM_B2 * v + (1.0 - ADAM_B2) * (g * g)
    m_hat = m / (1.0 - ADAM_B1 ** ADAM_STEP)
    v_hat = v / (1.0 - ADAM_B2 ** ADAM_STEP)
    delta = -ADAM_LR * (m_hat / (jnp.sqrt(v_hat) + ADAM_EPS) + ADAM_WD * w)
    return delta, m, v


def adam_reduce(parts, w, m, v, name):
    r, c = w.shape
    tr = next(cand for cand in (256, 128, 64, 32, 16, 8) if r % cand == 0) if r > 256 else r

    def body(p_ref, w_ref, m_ref, v_ref, g_ref, d_ref, mo_ref, vo_ref):
        g = p_ref[0].astype(F32)
        for j in range(1, N_DEV):
            g = g + p_ref[j].astype(F32)
        g_ref[...] = g
        d_ref[...], mo_ref[...], vo_ref[...] = _adamw(g, w_ref[...], m_ref[...], v_ref[...])

    row = pl.BlockSpec((tr, c), lambda i: (i, 0))
    return pl.pallas_call(
        body, name=name, grid=(r // tr,),
        out_shape=[jax.ShapeDtypeStruct((r, c), F32)] * 4,
        in_specs=[pl.BlockSpec((N_DEV, tr, c), lambda i: (0, i, 0)), row, row, row],
        out_specs=[row] * 4,
        compiler_params=_cp("parallel"),
    )(parts, w, m, v)


def adam_layers(parts, w, m, v, name):
    nl, r, c = w.shape
    tr = next(cand for cand in (256, 128, 64, 32, 16, 8) if r % cand == 0)
    steps = r // tr

    def body(*refs):
        p_refs = refs[:nl]
        w_ref, m_ref, v_ref, g_ref, d_ref, mo_ref, vo_ref = refs[nl:]
        for k in range(nl):
            @pl.when(pl.program_id(0) == k)
            def _():
                g = p_refs[k][0].astype(F32)
                for j in range(1, N_DEV):
                    g = g + p_refs[k][j].astype(F32)
                g_ref[...] = g
                d_ref[...], mo_ref[...], vo_ref[...] = _adamw(g, w_ref[...], m_ref[...], v_ref[...])

    def part_spec(k):
        return pl.BlockSpec((N_DEV, tr, c), lambda l, i: (0, jnp.clip(i + (l - k) * steps, 0, steps - 1), 0))

    blk = pl.BlockSpec((None, tr, c), lambda l, i: (l, i, 0))
    return pl.pallas_call(
        body, name=name, grid=(nl, steps),
        out_shape=[jax.ShapeDtypeStruct((nl, r, c), F32)] * 4,
        in_specs=[part_spec(k) for k in range(nl)] + [blk, blk, blk],
        out_specs=[blk] * 4,
        compiler_params=_cp("arbitrary", "arbitrary"),
    )(*parts, w, m, v)


def adam_w_ada(sct, dm_loc, w, m, v):
    nl, dm, wc = w.shape
    tr = 512

    def body(s_ref, d_ref, w_ref, m_ref, v_ref, g_ref, dl_ref, mo_ref, vo_ref):
        g = jnp.dot(s_ref[...], d_ref[...], preferred_element_type=F32, precision=lax.Precision.HIGHEST)
        g_ref[...] = g
        dl_ref[...], mo_ref[...], vo_ref[...] = _adamw(g, w_ref[...], m_ref[...], v_ref[...])

    blk = pl.BlockSpec((None, tr, wc), lambda l, i: (l, i, 0))
    return pl.pallas_call(
        body, name="adam_w_ada", grid=(nl, dm // tr),
        out_shape=[jax.ShapeDtypeStruct(w.shape, F32)] * 4,
        in_specs=[pl.BlockSpec((tr, LANES), lambda l, i: (i, 0)),
                  pl.BlockSpec((None, LANES, wc), lambda l, i: (l, 0, 0)), blk, blk, blk],
        out_specs=[blk] * 4,
        compiler_params=_cp("parallel", "parallel"),
    )(sct, dm_loc, w, m, v)


TRANSPOSED = ("w_gate", "w_up")


def _pieces(dm):
    ncol = lambda n: n // N_DEV
    mixer = ([Piece("w_in", "w_in", 1, 0, ncol(GATE_COL + 2 * dm)),
              Piece("w_a", "w_a", 1, 0, ncol(dm)),
              Piece("w_b", "w_b", 1, 0, ncol(dm)),
              Piece("w_o", "w_o", 0, 0, ncol(dm))],
             {"w_in": (dm, GATE_COL + 2 * dm), "w_a": (A_Q_HEADS * HEAD_DIM, dm), "w_b": (B_OUT_W, dm),
              "w_o": (dm, dm)})
    ffn = ([Piece("w_gate", "w_ffn_t", 0, 0, ncol(D_FF)),
            Piece("w_up", "w_ffn_t", 0, D_FF, ncol(D_FF)),
            Piece("w_down", "w_down", 0, 0, ncol(D_FF))],
           {"w_ffn_t": (2 * D_FF, dm), "w_down": (D_FF, dm)})
    return mixer, ffn


def kernel(x, c, w_ada, b_ada, w_in, sinks, w_a, w_b, w_o, ln1_g, ln1_b, w_gate, w_up, w_down, ln2_g, ln2_b, loss_target, m_w_ada, m_b_ada, m_w_in, m_sinks, m_w_a, m_w_b, m_w_o, m_ln1_g, m_ln1_b, m_w_gate, m_w_up, m_w_down, m_ln2_g, m_ln2_b, v_w_ada, v_b_ada, v_w_in, v_sinks, v_w_a, v_w_b, v_w_o, v_ln1_g, v_ln1_b, v_w_gate, v_w_up, v_w_down, v_ln2_g, v_ln2_b):
    given = dict(locals())
    nl = w_in.shape[0]
    t, dm = x.shape[1], x.shape[2]
    me = 4 * lax.axis_index("x") + 2 * lax.axis_index("y") + lax.axis_index("c")
    x0 = x.reshape(t, dm)
    target = loss_target.reshape(t, dm)

    groups = dict(zip(("mixer", "ffn"), _pieces(dm)))
    local = lambda nm, pre="": (given[pre + nm].transpose(0, 2, 1) if nm in TRANSPOSED else given[pre + nm])
    shards = {pc.name: local(pc.name).astype(BF16) for pcs, _ in groups.values() for pc in pcs}

    def gather(group, l):
        pcs, bufs = groups[group]
        return Exchange("gather", pcs, [shards[pc.name][l] for pc in pcs], bufs.values(), bufs)

    def scatter(group, gbuf):
        pcs, bufs = groups[group]
        return Exchange("scatter", pcs, [gbuf[nm] for nm in bufs],
                        [(N_DEV,) + shards[pc.name].shape[1:] for pc in pcs], bufs)

    full = [dict() for _ in range(nl)]
    full[0].update(zip(groups["mixer"][1], run_exchange(gather("mixer", 0), "gather_mixer")))

    wc = w_ada.shape[2]
    c_all = all_gather_small(jnp.broadcast_to(c, (8, dm)), "gather_c")[:, 0, :]
    b_loc = lax.dynamic_slice_in_dim(b_ada, me * wc, wc, axis=1).reshape(nl, 1, wc)
    mp, sc_all = mod_partial(c_all, w_ada, b_loc)
    mp_all = all_gather_small(mp.reshape(nl * N_DEV, wc), "gather_mod").reshape(N_DEV, nl, N_DEV, wc)
    mod = lax.dynamic_index_in_dim(mp_all, me, axis=2, keepdims=False)
    mod = mod.transpose(1, 0, 2).reshape(nl, 6, 1, dm)

    vec = lambda a, l: a[l].reshape(1, dm)

    saved = []
    xl = x0
    for l in range(nl):
        sh1, s1, g1, sh2, s2, g2 = [mod[l, j] for j in range(6)]
        w = full[l]
        (u1, u1_f4, u1_f16, proj, qkv_f4, qkv_f16), got = in_proj(xl, s1, sh1, w["w_in"], "in_proj",
                                                                   gather("ffn", l))
        w.update(zip(groups["ffn"][1], got))
        proj3 = proj.reshape(1, t, proj.shape[1])
        qkv_b = [proj3, qkv_f4, qkv_f16]
        ya, lse_a = attn_fwd(proj3, ATTN_A, sinks[l], "attn_a_fwd")
        o_g, l_g = [], []
        for g, cfg in enumerate(ATTN_B):
            o, ls = attn_fwd(qkv_b[g], cfg, sinks[l], "attn_b%d_fwd" % g)
            o_g.append(o)
            l_g.append(ls)
        yb, merged = mix_merge(ya[0], o_g, l_g, proj, w["w_a"], w["w_b"], "mix_merge")
        y1, x1, zh1, rs1 = proj_ln(merged, w["w_o"], xl, g1, vec(ln1_g, l), vec(ln1_b, l), "out_proj_ln")
        (u2, ab), got = modmm(x1, s2, sh2, w["w_ffn_t"].T, "ffn_up", gather("mixer", l + 1) if l + 1 < nl else None)
        if l + 1 < nl:
            full[l + 1].update(zip(groups["mixer"][1], got))
        h, y2, x2, zh2, rs2 = swiglu_proj_ln(ab, w["w_down"], x1, g2, vec(ln2_g, l), vec(ln2_b, l), "ffn_down_ln")
        saved.append(dict(xin=xl, u1=[u1, u1_f4.reshape(t, dm), u1_f16.reshape(t, dm)], proj=proj, qkv_b=qkv_b,
                          ya=ya, lse_a=lse_a, o_g=o_g, l_g=l_g, yb=yb, merged=merged,
                          y1=y1, x1=x1, zh1=zh1, rs1=rs1, u2=u2, ab=ab, h=h, y2=y2, zh2=zh2, rs2=rs2))
        xl = x2

    dx, loss_part = loss_head(xl, target)

    small = {k: [None] * nl for k in ("dmod", "ln1_g", "ln1_b", "ln2_g", "ln2_b", "sinks")}
    recv = {nm: [None] * nl for grp in groups.values() for nm in (pc.name for pc in grp[0])}

    def keep(group, l, got):
        for pc, arr in zip(groups[group][0], got):
            recv[pc.name][l] = arr

    for l in reversed(range(nl)):
        sv, w = saved[l], full[l]
        sh1, s1, g1, sh2, s2, g2 = [mod[l, j] for j in range(6)]
        fresh = lambda nm: lax.empty({**groups["mixer"][1], **groups["ffn"][1]}[nm], BF16)
        gbuf = {}
        dz2, dy2, sg, sb, sgate2 = ln_bwd(dx, sv["zh2"], sv["rs2"], sv["y2"], vec(ln2_g, l), g2, "ln_bwd")
        small["ln2_g"][l], small["ln2_b"][l] = sg[0], sb[0]
        gbuf["w_down"] = wgrad(sv["h"], dy2, fresh("w_down"), 512, dm // 512, 0, 0, 1, "wgrad_down")
        (dab,), _ = dswiglu(dy2, w["w_down"].T, sv["ab"], "dswiglu")
        gbuf["w_ffn_t"] = wgrad(dab, sv["u2"], fresh("w_ffn_t"), 512, dm // 512, 0, 0, 1, "wgrad_ffn_up")
        (dx1, ss2, ssh2), got = dgrad_ffn(dab, w["w_ffn_t"], dz2, sv["x1"], s2, "dgrad_ffn", scatter("ffn", gbuf))
        keep("ffn", l, got)
        dz1, do1, sg, sb, sgate1 = ln_bwd(dx1, sv["zh1"], sv["rs1"], sv["y1"], vec(ln1_g, l), g1, "ln_bwd")
        small["ln1_g"][l], small["ln1_b"][l] = sg[0], sb[0]
        gbuf["w_o"] = wgrad(sv["merged"], do1, fresh("w_o"), 512, dm // 512, 0, 0, 1, "wgrad_o")
        dpab, dya, dyb, dgab = dmerge(do1, w["w_o"].T, sv["ya"][0], sv["yb"], w["w_a"], w["w_b"],
                                      w["w_a"].T, w["w_b"].T, sv["proj"], "dmerge")
        gbuf["w_a"] = wgrad(sv["ya"][0], dpab, fresh("w_a"), 512, dm // 512, 0, 0, 1, "wgrad_a")
        gbuf["w_b"] = wgrad(sv["yb"], dpab, fresh("w_b"), 512, dm // 512, dm // 512, 0, 1, "wgrad_b")
        mixed = mix_bwd(dyb, sv["o_g"], sv["l_g"], "mix_bwd")
        do_g, dl_g = mixed[:N_GROUPS], mixed[N_GROUPS:]
        d_a, dsink = attn_bwd(sv["qkv_b"][0], sv["ya"], sv["lse_a"], dya.reshape(1, t, -1), None, ATTN_A,
                              sinks[l], "attn_a_bwd")
        small["sinks"][l] = dsink[:, 0]
        d_b = [attn_bwd(sv["qkv_b"][g], sv["o_g"][g], sv["l_g"][g], do_g[g], dl_g[g], cfg, sinks[l],
                        "attn_b%d_bwd" % g)[0] for g, cfg in enumerate(ATTN_B)]
        gw = wgrad(sv["u1"][0], d_a.reshape(t, A_W), fresh("w_in"), A_W, 1, 0, 0, 1, "wgrad_in_a")
        for g in range(N_GROUPS):
            gw = wgrad(sv["u1"][g], d_b[g].reshape(t, B_GW), gw, B_OUT_W, 3, 0, A_W // B_OUT_W + g, N_GROUPS,
                       "wgrad_in_b%d" % g)
        gbuf["w_in"] = wgrad(sv["u1"][0], dgab, gw, 512, 2 * dm // 512, 0, GATE_COL // 512, 1, "wgrad_in_gate")
        (dx, ss1, ssh1), got = dgrad_in(d_a, d_b, dgab, w["w_in"].T, dz1, sv["xin"], s1, "dgrad_in",
                                        scatter("mixer", gbuf))
        keep("mixer", l, got)
        small["dmod"][l] = jnp.stack([ssh1[0], ss1[0], sgate1[0], ssh2[0], ss2[0], sgate2[0]])
    grad_x = dx.reshape(x.shape)

    big_out = {}
    for nm, parts in recv.items():
        outs = adam_layers(parts, local(nm), local(nm, "m_"), local(nm, "v_"), "adam_" + nm)
        big_out[nm] = [o.transpose(0, 2, 1) for o in outs] if nm in TRANSPOSED else outs

    rows = jnp.concatenate(
        [jnp.stack(small["dmod"]).reshape(nl * 6, dm)]
        + [jnp.stack(small[k]) for k in ("ln1_g", "ln1_b", "ln2_g", "ln2_b")]
        + [jnp.pad(jnp.stack(small["sinks"]).reshape(1, -1), ((0, 0), (0, dm - nl * A_Q_HEADS))),
           jnp.broadcast_to(loss_part[0:1, 0:1], (1, dm))])
    n_rows = rows.shape[0]
    rows = jnp.pad(rows, ((0, -n_rows % 8), (0, 0)))
    rows_all = all_gather_small(rows, "gather_small_grads")

    def pack_small(pre):
        parts = [given[pre + "b_ada"].reshape(nl * 6, dm)]
        parts += [given[pre + k] for k in ("ln1_g", "ln1_b", "ln2_g", "ln2_b")]
        parts.append(jnp.pad(given[pre + "sinks"].reshape(1, -1), ((0, 0), (0, dm - nl * A_Q_HEADS))))
        p = jnp.concatenate(parts)
        return jnp.pad(p, ((0, rows.shape[0] - p.shape[0]), (0, 0)))

    souts = adam_reduce(rows_all, pack_small(""), pack_small("m_"), pack_small("v_"), "adam_small")

    def unpack_small(o):
        r = {"b_ada": o[0:nl * 6].reshape(nl, 6 * dm)}
        for j, k in enumerate(("ln1_g", "ln1_b", "ln2_g", "ln2_b")):
            r[k] = o[nl * 6 + j * nl: nl * 6 + (j + 1) * nl]
        r["sinks"] = o[nl * 10, 0:nl * A_Q_HEADS].reshape(nl, A_Q_HEADS)
        return r

    small_out = [unpack_small(o) for o in souts]
    loss = souts[0][nl * 10 + 1, 0]

    dmod_all = rows_all[:, 0:nl * 6].reshape(N_DEV, nl, 6 * dm)
    dm_loc = lax.dynamic_slice_in_dim(dmod_all, me * wc, wc, axis=2).transpose(1, 0, 2)
    dm_loc = jnp.pad(dm_loc, ((0, 0), (0, LANES - N_DEV), (0, 0)))
    sct = jnp.pad(sc_all.T, ((0, 0), (0, LANES - N_DEV)))
    ada_out = adam_w_ada(sct, dm_loc, w_ada, m_w_ada, v_w_ada)

    names = ["w_ada", "b_ada", "w_in", "sinks", "w_a", "w_b", "w_o", "ln1_g", "ln1_b",
             "w_gate", "w_up", "w_down", "ln2_g", "ln2_b"]

    def pick(kind, nm):
        if nm == "w_ada":
            return ada_out[kind]
        if nm in small_out[kind]:
            return small_out[kind][nm]
        return big_out[nm][kind]

    result = [loss, grad_x]
    for kind in range(4):
        result += [pick(kind, nm) for nm in names]
    return tuple(result)
```

```python
import functools

import jax
import jax.numpy as jnp
from jax import lax
from jax.experimental import pallas as pl
from jax.experimental.pallas import tpu as pltpu

F32 = jnp.float32
BF16 = jnp.bfloat16

D_MODEL = 1024
HEAD_DIM = 64
A_Q_HEADS = 8
A_KV_HEADS = 2
A_WINDOW = 128
B_GROUPS = ((128, 1), (512, 4), (2048, 16))
N_GROUPS = len(B_GROUPS)
B_HEADS_PER_GROUP = 4
N_ATTN_HEADS = A_Q_HEADS + B_HEADS_PER_GROUP * N_GROUPS
BLOCK = 128
A_W = (A_Q_HEADS + 2 * A_KV_HEADS) * HEAD_DIM
B_OUT_W = B_HEADS_PER_GROUP * HEAD_DIM
B_GW = 3 * B_OUT_W
B_ALL = N_GROUPS * B_OUT_W
GATE_COL = A_W + 3 * B_ALL
D_FF = 2816
FF_CHUNK = 256
DGRAD_CHUNK = 256
DN_ALPHA = 8.0 ** 0.25
LN_EPS = 1e-5
NEG_INF = -1e30
ADAM_LR, ADAM_B1, ADAM_B2, ADAM_EPS, ADAM_WD, ADAM_STEP = 0.001, 0.9, 0.999, 1e-08, 0.01, 10

N_DEV = 8
MESH = pl.DeviceIdType.MESH
VMEM_LIMIT = 56 * 1024 * 1024
ROW_TILE = 512
WGRAD_TILE_ELEMS = 2 * 1024 * 1024
LANES = 128
BF16_ROWS = 16


def _cp(*sem):
    return pltpu.CompilerParams(dimension_semantics=sem, vmem_limit_bytes=VMEM_LIMIT)


def _row_tile(t):
    return min(ROW_TILE, t)


def _slope(head):
    return 2.0 ** (-8.0 * (head + 1) / N_ATTN_HEADS)


def _sigmoid(x):
    return 1.0 / (1.0 + jnp.exp(-x))


def _dot(a, b):
    return jnp.dot(a, b, preferred_element_type=F32)


def _dot_nt(a, b):
    return lax.dot_general(a, b, (((1,), (1,)), ((), ())), preferred_element_type=F32)


def _dot_tn(a, b):
    return lax.dot_general(a, b, (((0,), (0,)), ((), ())), preferred_element_type=F32)


def _fold_scratch(tm, w):
    return [pltpu.VMEM((tm, LANES), F32)] * (w // LANES)


def _fold_to(dst_ref, val, scrs, d, col0=0):
    tm, w = val.shape
    if d == 1:
        dst_ref[0, :, col0:col0 + w] = val.astype(dst_ref.dtype)
        return
    for cb in range(w // LANES):
        scrs[cb][...] = val[:, cb * LANES:(cb + 1) * LANES]
    for r in range(d):
        for cb in range(w // LANES):
            piece = scrs[cb][pl.ds(r, tm // d, stride=d), :]
            dst_ref[r, :, col0 + cb * LANES:col0 + (cb + 1) * LANES] = piece.astype(dst_ref.dtype)


def _unfold_rows(rows_of, scrs, d, n, w):
    for r in range(d):
        for cb in range(w // LANES):
            scrs[cb][pl.ds(r, n, stride=d), :] = rows_of(r, slice(cb * LANES, (cb + 1) * LANES)).astype(F32)
    return jnp.concatenate([scrs[cb][0:d * n, :] for cb in range(w // LANES)], axis=1)


def _unfold_from(src_ref, scrs, d):
    if d == 1:
        return src_ref[0].astype(F32)
    _, n, w = src_ref.shape
    return _unfold_rows(lambda r, cols: src_ref[r, :, cols], scrs, d, n, w)


def _folded_spec(d, tm, w):
    return pl.BlockSpec((d, tm // d, w), lambda i: (0, i, 0))


def _me():
    return lax.axis_index("x"), lax.axis_index("y"), lax.axis_index("c")


def _flip(v, bit):
    return 1 - v if bit else v


def _peer(k):
    x, y, c = _me()
    return (_flip(x, k & 4), _flip(y, k & 2), _flip(c, k & 1))


def _peer_index(k):
    px, py, pc = _peer(k)
    return 4 * px + 2 * py + pc


def all_gather_small(v, name):
    r, c = v.shape

    def body(v_ref, out_ref, send_sems, recv_sems):
        me = _peer_index(0)
        out_ref[me] = v_ref[...]
        copies = []
        for k in range(1, N_DEV):
            cp = pltpu.make_async_remote_copy(
                src_ref=v_ref, dst_ref=out_ref.at[me],
                send_sem=send_sems.at[k - 1], recv_sem=recv_sems.at[k - 1],
                device_id=_peer(k), device_id_type=MESH)
            cp.start()
            copies.append(cp)
        for k in range(1, N_DEV):
            pltpu.make_async_remote_copy(
                src_ref=v_ref, dst_ref=out_ref.at[_peer_index(k)],
                send_sem=send_sems.at[k - 1], recv_sem=recv_sems.at[k - 1],
                device_id=_peer(k), device_id_type=MESH).wait_recv()
        for cp in copies:
            cp.wait_send()

    return pl.pallas_call(
        body, name=name,
        out_shape=jax.ShapeDtypeStruct((N_DEV, r, c), v.dtype),
        in_specs=[pl.BlockSpec(memory_space=pltpu.VMEM)],
        out_specs=pl.BlockSpec(memory_space=pltpu.VMEM),
        scratch_shapes=[pltpu.SemaphoreType.DMA((N_DEV - 1,)), pltpu.SemaphoreType.DMA((N_DEV - 1,))],
        compiler_params=pltpu.CompilerParams(vmem_limit_bytes=VMEM_LIMIT),
    )(v)


class Piece:
    def __init__(self, name, buf, axis, base, size):
        self.name, self.buf, self.axis, self.base, self.size = name, buf, axis, base, size

    def window(self, ref, j):
        start = self.base + j * self.size
        if self.axis == 1:
            return ref.at[:, pl.ds(pl.multiple_of(start, LANES), self.size)]
        return ref.at[pl.ds(pl.multiple_of(start, BF16_ROWS), self.size), :]


class Exchange:
    def __init__(self, kind, pieces, ins, out_shapes, bufs):
        self.kind, self.pieces, self.ins, self.out_shapes = kind, pieces, list(ins), list(out_shapes)
        self.buf_of = {nm: i for i, nm in enumerate(bufs)}
        self.n_in, self.n_out = len(self.ins), len(self.out_shapes)
        n = len(pieces)
        self.scratch = [pltpu.SemaphoreType.DMA((n, N_DEV - 1)), pltpu.SemaphoreType.DMA((n, N_DEV - 1)),
                        pltpu.SemaphoreType.DMA((n,))]
        self.in_specs = [pl.BlockSpec(memory_space=pl.ANY)] * self.n_in
        self.out_specs = [pl.BlockSpec(memory_space=pl.ANY)] * self.n_out
        self.out_shape = [jax.ShapeDtypeStruct(s, BF16) for s in self.out_shapes]

    def _ends(self, pi, ins, outs, to):
        pc = self.pieces[pi]
        if self.kind == "gather":
            return ins[pi], pc.window(outs[self.buf_of[pc.buf]], _peer_index(0))
        return pc.window(ins[self.buf_of[pc.buf]], to), outs[pi].at[_peer_index(0)]

    def _landing(self, pi, outs, frm):
        pc = self.pieces[pi]
        if self.kind == "gather":
            return pc.window(outs[self.buf_of[pc.buf]], frm)
        return outs[pi].at[frm]

    def _remote(self, pi, k, src, dst, sems):
        return pltpu.make_async_remote_copy(
            src_ref=src, dst_ref=dst, send_sem=sems[0].at[pi, k - 1], recv_sem=sems[1].at[pi, k - 1],
            device_id=_peer(k), device_id_type=MESH)

    def _local(self, pi, ins, outs, sems):
        return pltpu.make_async_copy(*self._ends(pi, ins, outs, _peer_index(0)), sems[2].at[pi])

    def start(self, ins, outs, sems):
        for pi in range(len(self.pieces)):
            self._local(pi, ins, outs, sems).start()
            for k in range(1, N_DEV):
                self._remote(pi, k, *self._ends(pi, ins, outs, _peer_index(k)), sems).start()

    def finish(self, ins, outs, sems):
        for pi in range(len(self.pieces)):
            src_like = self._ends(pi, ins, outs, _peer_index(0))[0]
            for k in range(1, N_DEV):
                self._remote(pi, k, src_like, self._landing(pi, outs, _peer_index(k)), sems).wait_recv()
        for pi in range(len(self.pieces)):
            for k in range(1, N_DEV):
                self._remote(pi, k, *self._ends(pi, ins, outs, _peer_index(k)), sems).wait_send()
            self._local(pi, ins, outs, sems).wait()


def _hosted(ex, refs, n_in, n_out, first, last):
    if ex is None:
        return refs
    ins, rest = refs[:n_in], refs[n_in:]
    ex_ins, rest = rest[:ex.n_in], rest[ex.n_in:]
    outs, rest = rest[:n_out], rest[n_out:]
    ex_outs, rest = rest[:ex.n_out], rest[ex.n_out:]
    scr, sems = rest[:len(rest) - 3], rest[len(rest) - 3:]
    pl.when(first)(lambda: ex.start(ex_ins, ex_outs, sems))
    pl.when(last)(lambda: ex.finish(ex_ins, ex_outs, sems))
    return tuple(ins) + tuple(outs) + tuple(scr)


def _host_call(body, ex, *, name, grid, out_shape, in_specs, out_specs, scratch_shapes=(), sem=None, args):
    n_out = len(out_shape)
    if ex is not None:
        out_shape = list(out_shape) + ex.out_shape
        in_specs = list(in_specs) + ex.in_specs
        out_specs = list(out_specs) + ex.out_specs
        scratch_shapes = list(scratch_shapes) + ex.scratch
        args = list(args) + ex.ins
    res = pl.pallas_call(body, name=name, grid=grid, out_shape=out_shape, in_specs=in_specs, out_specs=out_specs,
                         scratch_shapes=scratch_shapes, compiler_params=_cp(*sem))(*args)
    return res[:n_out], res[n_out:]


def run_exchange(ex, name):
    def body(*refs):
        ins, outs, sems = refs[:ex.n_in], refs[ex.n_in:ex.n_in + ex.n_out], refs[ex.n_in + ex.n_out:]
        ex.start(ins, outs, sems)
        ex.finish(ins, outs, sems)

    return pl.pallas_call(body, name=name, out_shape=ex.out_shape, in_specs=ex.in_specs, out_specs=ex.out_specs,
                          scratch_shapes=ex.scratch)(*ex.ins)


def mod_partial(c_all, w_ada, b_loc):
    nl, dm, wc = w_ada.shape

    def body(c_ref, w_ref, b_ref, o_ref, sc_ref):
        cc = c_ref[...]
        sc = cc * _sigmoid(cc)
        sc_ref[...] = sc
        o_ref[...] = jnp.dot(sc, w_ref[...], preferred_element_type=F32,
                             precision=lax.Precision.HIGHEST) + b_ref[...]

    return pl.pallas_call(
        body, name="mod_partial", grid=(nl,),
        out_shape=[jax.ShapeDtypeStruct((nl, N_DEV, wc), F32), jax.ShapeDtypeStruct((N_DEV, dm), F32)],
        in_specs=[pl.BlockSpec((N_DEV, dm), lambda l: (0, 0)),
                  pl.BlockSpec((None, dm, wc), lambda l: (l, 0, 0)),
                  pl.BlockSpec((None, 1, wc), lambda l: (l, 0, 0))],
        out_specs=[pl.BlockSpec((None, N_DEV, wc), lambda l: (l, 0, 0)),
                   pl.BlockSpec((N_DEV, dm), lambda l: (0, 0))],
        compiler_params=_cp("arbitrary"),
    )(c_all, w_ada, b_loc)


def in_proj(x, s, sh, w, name, ex=None):
    t, dm = x.shape
    n = w.shape[1]
    tm = _row_tile(t)
    nsteps = t // tm
    ch = B_OUT_W
    dils = [dil for _, dil in B_GROUPS if dil > 1]

    def body(*refs):
        i = pl.program_id(0)
        x_ref, s_ref, sh_ref, w_ref, u_ref, *rest = _hosted(ex, refs, 4, 2 + 2 * len(dils), i == 0, i == nsteps - 1)
        uf_refs, o_ref, qf_refs = rest[:len(dils)], rest[len(dils)], rest[len(dils) + 1:len(dils) * 2 + 1]
        scrs = rest[len(dils) * 2 + 1:]
        uf = x_ref[...] * (1.0 + s_ref[...]) + sh_ref[...]
        u = uf.astype(BF16)
        u_ref[...] = u
        for d, uf_ref in zip(dils, uf_refs):
            _fold_to(uf_ref, uf, scrs, d)
        for c0 in range(0, n, ch):
            res = _dot(u, w_ref[:, c0:c0 + ch])
            o_ref[:, c0:c0 + ch] = res.astype(BF16)
            if A_W <= c0 < GATE_COL:
                part, g = divmod((c0 - A_W) // ch, N_GROUPS)
                d = B_GROUPS[g][1]
                if d > 1:
                    _fold_to(qf_refs[dils.index(d)], res, scrs, d, part * ch)

    vec = pl.BlockSpec((1, dm), lambda i: (0, 0))
    row = lambda w_: pl.BlockSpec((tm, w_), lambda i: (i, 0))
    return _host_call(
        body, ex, name=name, grid=(nsteps,),
        out_shape=[jax.ShapeDtypeStruct((t, dm), BF16)]
                  + [jax.ShapeDtypeStruct((d, t // d, dm), BF16) for d in dils]
                  + [jax.ShapeDtypeStruct((t, n), BF16)]
                  + [jax.ShapeDtypeStruct((d, t // d, B_GW), BF16) for d in dils],
        in_specs=[row(dm), vec, vec, pl.BlockSpec((dm, n), lambda i: (0, 0))],
        out_specs=[row(dm)] + [_folded_spec(d, tm, dm) for d in dils] + [row(n)]
                  + [_folded_spec(d, tm, B_GW) for d in dils],
        scratch_shapes=_fold_scratch(tm, dm), sem=("arbitrary",), args=[x, s, sh, w])


def modmm(x, s, sh, w, name, ex=None):
    t, dm = x.shape
    n = w.shape[1]
    tm = _row_tile(t)
    nsteps = t // tm
    ch = 512

    def body(*refs):
        i = pl.program_id(0)
        x_ref, s_ref, sh_ref, w_ref, u_ref, o_ref = _hosted(ex, refs, 4, 2, i == 0, i == nsteps - 1)
        u = (x_ref[...] * (1.0 + s_ref[...]) + sh_ref[...]).astype(BF16)
        u_ref[...] = u
        for c0 in range(0, n, ch):
            o_ref[:, c0:c0 + ch] = _dot(u, w_ref[:, c0:c0 + ch]).astype(BF16)

    vec = pl.BlockSpec((1, dm), lambda i: (0, 0))
    return _host_call(
        body, ex, name=name, grid=(nsteps,),
        out_shape=[jax.ShapeDtypeStruct((t, dm), BF16), jax.ShapeDtypeStruct((t, n), BF16)],
        in_specs=[pl.BlockSpec((tm, dm), lambda i: (i, 0)), vec, vec,
                  pl.BlockSpec((dm, n), lambda i: (0, 0))],
        out_specs=[pl.BlockSpec((tm, dm), lambda i: (i, 0)), pl.BlockSpec((tm, n), lambda i: (i, 0))],
        sem=("arbitrary",), args=[x, s, sh, w])


def _halves(tm):
    half = tm // 2 if tm % 32 == 0 else tm
    return [slice(r0, r0 + half) for r0 in range(0, tm, half)]


def _ln_store(y, rows, xres_ref, g_ref, lg_ref, lb_ref, y_ref, xo_ref, zh_ref, rs_ref):
    y_ref[rows, :] = y.astype(BF16)
    z = DN_ALPHA * xres_ref[rows, :] + g_ref[...] * y
    mu = jnp.mean(z, axis=1, keepdims=True)
    zc = z - mu
    var = jnp.mean(zc * zc, axis=1, keepdims=True)
    rstd = lax.rsqrt(var + LN_EPS)
    zhat = zc * rstd
    zh_ref[rows, :] = zhat.astype(zh_ref.dtype)
    xo_ref[rows, :] = zhat * lg_ref[...] + lb_ref[...]
    rs_ref[rows, :] = jnp.broadcast_to(rstd, (zhat.shape[0], rs_ref.shape[1]))


def _ln_out_shapes(t, dm):
    return [jax.ShapeDtypeStruct((t, dm), BF16), jax.ShapeDtypeStruct((t, dm), F32),
            jax.ShapeDtypeStruct((t, dm), F32), jax.ShapeDtypeStruct((t, LANES), F32)]


def _ln_out_specs(tm, dm):
    row = pl.BlockSpec((tm, dm), lambda i: (i, 0))
    return [row, row, row, pl.BlockSpec((tm, LANES), lambda i: (i, 0))]


def proj_ln(a, w, xres, gate, lg, lb, name):
    t, k = a.shape
    dm = w.shape[1]
    tm = _row_tile(t)

    def body(a_ref, w_ref, xres_ref, g_ref, lg_ref, lb_ref, y_ref, xo_ref, zh_ref, rs_ref):
        for rows in _halves(tm):
            y = _dot(a_ref[rows, :], w_ref[...])
            _ln_store(y, rows, xres_ref, g_ref, lg_ref, lb_ref, y_ref, xo_ref, zh_ref, rs_ref)

    vec = pl.BlockSpec((1, dm), lambda i: (0, 0))
    return pl.pallas_call(
        body, name=name, grid=(t // tm,),
        out_shape=_ln_out_shapes(t, dm),
        in_specs=[pl.BlockSpec((tm, k), lambda i: (i, 0)), pl.BlockSpec((k, dm), lambda i: (0, 0)),
                  pl.BlockSpec((tm, dm), lambda i: (i, 0)), vec, vec, vec],
        out_specs=_ln_out_specs(tm, dm),
        compiler_params=_cp("parallel"),
    )(a, w, xres, gate, lg, lb)


def swiglu_proj_ln(ab, w, xres, gate, lg, lb, name):
    t = ab.shape[0]
    f, dm = w.shape
    tm = _row_tile(t)

    def body(a_ref, b_ref, w_ref, xres_ref, g_ref, lg_ref, lb_ref, h_ref, y_ref, xo_ref, zh_ref, rs_ref):
        for rows in _halves(tm):
            y = None
            for c0 in range(0, f, FF_CHUNK):
                cols = slice(c0, c0 + FF_CHUNK)
                a = a_ref[rows, cols].astype(F32)
                h = (a * _sigmoid(a) * b_ref[rows, cols].astype(F32)).astype(BF16)
                h_ref[rows, cols] = h
                part = _dot(h, w_ref[cols, :])
                y = part if y is None else y + part
            _ln_store(y, rows, xres_ref, g_ref, lg_ref, lb_ref, y_ref, xo_ref, zh_ref, rs_ref)

    vec = pl.BlockSpec((1, dm), lambda i: (0, 0))
    return pl.pallas_call(
        body, name=name, grid=(t // tm,),
        out_shape=[jax.ShapeDtypeStruct((t, f), BF16)] + _ln_out_shapes(t, dm),
        in_specs=[pl.BlockSpec((tm, f), lambda i: (i, 0)), pl.BlockSpec((tm, f), lambda i: (i, 1)),
                  pl.BlockSpec((f, dm), lambda i: (0, 0)),
                  pl.BlockSpec((tm, dm), lambda i: (i, 0)), vec, vec, vec],
        out_specs=[pl.BlockSpec((tm, f), lambda i: (i, 0))] + _ln_out_specs(tm, dm),
        compiler_params=_cp("parallel"),
    )(ab, ab, w, xres, gate, lg, lb)


class AttnCfg:
    def __init__(self, dil, heads, kv_heads, qc, kc, vc, max_dist, head0, sinks):
        self.dil, self.heads, self.kv_heads = dil, heads, kv_heads
        self.qc, self.kc, self.vc = qc, kc, vc
        self.max_dist, self.head0, self.sinks = max_dist, head0, sinks
        self.wq = heads * HEAD_DIM
        self.wk = kv_heads * HEAD_DIM
        self.wout = self.wq + 2 * self.wk


ATTN_A = AttnCfg(1, A_Q_HEADS, A_KV_HEADS, 0, A_Q_HEADS * HEAD_DIM, (A_Q_HEADS + A_KV_HEADS) * HEAD_DIM,
                 A_WINDOW - 1, 0, True)


def _attn_b_cfg(g):
    win, dil = B_GROUPS[g]
    cols = ((A_W + g * B_OUT_W, A_W + B_ALL + g * B_OUT_W, A_W + 2 * B_ALL + g * B_OUT_W) if dil == 1
            else (0, B_OUT_W, 2 * B_OUT_W))
    return AttnCfg(dil, B_HEADS_PER_GROUP, B_HEADS_PER_GROUP, *cols, win // dil,
                   A_Q_HEADS + g * B_HEADS_PER_GROUP, False)


ATTN_B = [_attn_b_cfg(g) for g in range(N_GROUPS)]


SCALE = HEAD_DIM ** -0.5


def _head(h):
    return slice(h * HEAD_DIM, (h + 1) * HEAD_DIM)


def _stack(parts):
    return parts[0] if len(parts) == 1 else jnp.concatenate(parts, axis=0)


def _masked_bias(mask, distf, cfg, h, d):
    return jnp.where(mask, distf * (-(_slope(cfg.head0 + h) * d)), NEG_INF)


def _band(i, max_dist):
    qi = lax.broadcasted_iota(jnp.int32, (BLOCK, 2 * BLOCK), 0)
    sj = lax.broadcasted_iota(jnp.int32, (BLOCK, 2 * BLOCK), 1)
    dist = qi + BLOCK - sj
    valid = (dist >= 0) & (dist <= max_dist)
    first_key = jnp.where(i > 0, 0, BLOCK)
    valid_first = valid & (sj >= first_key)
    return dist, valid, valid_first


def _attn_geometry(cfg, n):
    tq = min(512, n)
    return tq, tq // BLOCK, n // tq


def attn_fwd(qkv, cfg, sinks, name):
    d, n, _ = qkv.shape
    tq, nsub, nqb = _attn_geometry(cfg, n)
    wq, wk = cfg.wq, cfg.wk
    grp = cfg.heads // cfg.kv_heads

    def body(sink_ref, q_ref, kc_ref, kp_ref, vc_ref, vp_ref, o_ref, l_ref, kf, vf):
        i = pl.program_id(1)
        kf[0:BLOCK, :] = kp_ref[...]
        kf[BLOCK:, :] = kc_ref[...]
        vf[0:BLOCK, :] = vp_ref[...]
        vf[BLOCK:, :] = vc_ref[...]
        dist, valid, valid_first = _band(i, cfg.max_dist)
        distf = dist.astype(F32)
        rows = [slice(a * BLOCK, (a + 1) * BLOCK) for a in range(nsub)]
        wins = [slice(a * BLOCK, (a + 2) * BLOCK) for a in range(nsub)]
        for head in range(cfg.heads):
            heads = [head]
            ks = _head(head // grp)
            b_reg = _stack([_masked_bias(valid, distf, cfg, h, d) for h in heads])
            b_first = _stack([_masked_bias(valid_first, distf, cfg, h, d) for h in heads])
            if cfg.sinks:
                sink = _stack([jnp.full((BLOCK, 1), sink_ref[h], F32) for h in heads])
            ss = [_dot_nt(_stack([q_ref[rows[a], _head(h)] for h in heads]) * SCALE, kf[wins[a], ks])
                  + (b_first if a == 0 else b_reg) for a in range(nsub)]
            es, invs = [], []
            for a in range(nsub):
                m = jnp.max(ss[a], axis=1, keepdims=True)
                if cfg.sinks:
                    m = jnp.maximum(m, sink)
                e = jnp.exp(ss[a] - m)
                den = jnp.sum(e, axis=1, keepdims=True)
                if cfg.sinks:
                    den = den + jnp.exp(sink - m)
                es.append(e.astype(BF16))
                invs.append(1.0 / den)
                lse = m + jnp.log(den)
                for g, h in enumerate(heads):
                    l_ref[rows[a], _head(h)] = jnp.broadcast_to(lse[g * BLOCK:(g + 1) * BLOCK], (BLOCK, HEAD_DIM))
            for a in range(nsub):
                o = _dot(es[a], vf[wins[a], ks]) * invs[a]
                for g, h in enumerate(heads):
                    o_ref[rows[a], _head(h)] = o[g * BLOCK:(g + 1) * BLOCK].astype(BF16)

    prev = lambda i: jnp.maximum(i * nsub - 1, 0)
    cur = lambda w, c: pl.BlockSpec((None, tq, w), lambda r, i: (r, i, c // w))
    prv = lambda w, c: pl.BlockSpec((None, BLOCK, w), lambda r, i: (r, prev(i), c // w))
    out = pl.BlockSpec((None, tq, wq), lambda r, i: (r, i, 0))
    return pl.pallas_call(
        body, name=name, grid=(d, nqb),
        out_shape=[jax.ShapeDtypeStruct((d, n, wq), BF16), jax.ShapeDtypeStruct((d, n, wq), F32)],
        in_specs=[pl.BlockSpec(memory_space=pltpu.SMEM),
                  cur(wq, cfg.qc), cur(wk, cfg.kc), prv(wk, cfg.kc), cur(wk, cfg.vc), prv(wk, cfg.vc)],
        out_specs=[out, out],
        scratch_shapes=[pltpu.VMEM((tq + BLOCK, wk), BF16), pltpu.VMEM((tq + BLOCK, wk), BF16)],
        compiler_params=_cp("parallel", "parallel"),
    )(sinks, qkv, qkv, qkv, qkv, qkv)


def mix_merge(ya, o_g, l_g, proj, w_a, w_b, name):
    t = ya.shape[0]
    dm = w_a.shape[1]
    tm = _row_tile(t)
    gcol = GATE_COL // dm
    dils = [o.shape[0] for o in o_g]

    def body(ya_ref, o0, o1, o2, l0, l1, l2, ga_ref, gb_ref, wa_ref, wb_ref, yb_ref, mg_ref, *scrs):
        ls = [_unfold_from(l, scrs, d) for l, d in zip((l0, l1, l2), dils)]
        m = jnp.maximum(jnp.maximum(ls[0], ls[1]), ls[2])
        es = [jnp.exp(l - m) for l in ls]
        inv = 1.0 / (es[0] + es[1] + es[2])
        yb = sum(_unfold_from(o, scrs, d) * (e * inv) for o, e, d in zip((o0, o1, o2), es, dils)).astype(BF16)
        yb_ref[...] = yb
        pa = _dot(ya_ref[...], wa_ref[...])
        pb = _dot(yb, wb_ref[...])
        mg = _sigmoid(ga_ref[...].astype(F32)) * pa + _sigmoid(gb_ref[...].astype(F32)) * pb
        mg_ref[...] = mg.astype(BF16)

    wide = lambda w: pl.BlockSpec((tm, w), lambda i: (i, 0))
    folded = [_folded_spec(d, tm, B_OUT_W) for d in dils]
    return pl.pallas_call(
        body, name=name, grid=(t // tm,),
        out_shape=[jax.ShapeDtypeStruct((t, B_OUT_W), BF16), jax.ShapeDtypeStruct((t, dm), BF16)],
        in_specs=[wide(ya.shape[1])] + folded + folded
                 + [pl.BlockSpec((tm, dm), lambda i: (i, gcol)), pl.BlockSpec((tm, dm), lambda i: (i, gcol + 1)),
                    pl.BlockSpec(w_a.shape, lambda i: (0, 0)), pl.BlockSpec(w_b.shape, lambda i: (0, 0))],
        out_specs=[wide(B_OUT_W), wide(dm)],
        scratch_shapes=_fold_scratch(tm, B_OUT_W),
        compiler_params=_cp("parallel"),
    )(ya, *o_g, *l_g, proj, proj, w_a, w_b)


def loss_head(y, target):
    t, dm = y.shape
    tm = _row_tile(t)

    def body(y_ref, t_ref, dy_ref, loss_ref):
        @pl.when(pl.program_id(0) == 0)
        def _():
            loss_ref[...] = jnp.zeros_like(loss_ref)
        err = y_ref[...] - t_ref[...]
        dy_ref[...] = err * (1.0 / dm)
        per_row = jnp.sum(err * err, axis=1, keepdims=True) * (1.0 / dm)
        loss_ref[...] += 0.5 * jnp.sum(per_row, axis=0, keepdims=True)

    row = pl.BlockSpec((tm, dm), lambda i: (i, 0))
    return pl.pallas_call(
        body, name="loss_head", grid=(t // tm,),
        out_shape=[jax.ShapeDtypeStruct((t, dm), F32), jax.ShapeDtypeStruct((8, LANES), F32)],
        in_specs=[row, row],
        out_specs=[row, pl.BlockSpec((8, LANES), lambda i: (0, 0))],
        compiler_params=_cp("arbitrary"),
    )(y, target)


def _fold_rows(v):
    tm, c = v.shape
    return jnp.sum(v.reshape(tm // 8, 8, c), axis=0)


def _finish_sums(refs, nsteps):
    @pl.when(pl.program_id(0) == nsteps - 1)
    def _():
        for r in refs:
            r[...] = jnp.broadcast_to(jnp.sum(r[...], axis=0, keepdims=True), r.shape)


def ln_bwd(dxo, zhat, rstd, ysub, lg, gate, act, name):
    t, dm = dxo.shape
    k = act.shape[1]
    tm = _row_tile(t)
    nsteps = t // tm
    ch = 256

    def body(dxo_ref, zh_ref, rs_ref, y_ref, lg_ref, g_ref, a_ref, dz_ref, dy_ref, sg_ref, sb_ref, sgate_ref,
             gw_ref, acc):
        @pl.when(pl.program_id(0) == 0)
        def _():
            for r in (sg_ref, sb_ref, sgate_ref, acc):
                r[...] = jnp.zeros_like(r)
        for rows in _halves(tm):
            dxo_v = dxo_ref[rows, :]
            zh = zh_ref[rows, :]
            dxh = dxo_v * lg_ref[...]
            m1 = jnp.mean(dxh, axis=1, keepdims=True)
            m2 = jnp.mean(dxh * zh, axis=1, keepdims=True)
            dz = rs_ref[rows, 0:1] * (dxh - m1 - zh * m2)
            dz_ref[rows, :] = dz
            dy = (g_ref[...] * dz).astype(BF16)
            dy_ref[rows, :] = dy
            sg_ref[...] += _fold_rows(dxo_v * zh)
            sb_ref[...] += _fold_rows(dxo_v)
            sgate_ref[...] += _fold_rows(dz * y_ref[rows, :].astype(F32))
            a = a_ref[rows, :]
            for c0 in range(0, dm, ch):
                acc[:, c0:c0 + ch] += _dot_tn(a, dy[:, c0:c0 + ch])
        _finish_sums((sg_ref, sb_ref, sgate_ref), nsteps)

        @pl.when(pl.program_id(0) == nsteps - 1)
        def _():
            gw_ref[...] = acc[...].astype(BF16)

    row = pl.BlockSpec((tm, dm), lambda i: (i, 0))
    vec = pl.BlockSpec((1, dm), lambda i: (0, 0))
    sums = pl.BlockSpec((8, dm), lambda i: (0, 0))
    return pl.pallas_call(
        body, name=name, grid=(nsteps,),
        out_shape=[jax.ShapeDtypeStruct((t, dm), F32), jax.ShapeDtypeStruct((t, dm), BF16)]
                  + [jax.ShapeDtypeStruct((8, dm), F32)] * 3 + [jax.ShapeDtypeStruct((k, dm), BF16)],
        in_specs=[row, row, pl.BlockSpec((tm, LANES), lambda i: (i, 0)), row, vec, vec,
                  pl.BlockSpec((tm, k), lambda i: (i, 0))],
        out_specs=[row, row, sums, sums, sums, pl.BlockSpec((k, dm), lambda i: (0, 0))],
        scratch_shapes=[pltpu.VMEM((k, dm), F32)],
        compiler_params=_cp("arbitrary"),
    )(dxo, zhat, rstd, ysub, lg, gate, act)


def _mod_bwd_store(du_of, dz_ref, x_ref, s_ref, dx_ref, ss_ref, ssh_ref, nsteps):
    @pl.when(pl.program_id(0) == 0)
    def _():
        ss_ref[...] = jnp.zeros_like(ss_ref)
        ssh_ref[...] = jnp.zeros_like(ssh_ref)
    for c0 in range(0, dx_ref.shape[1], DGRAD_CHUNK):
        cols = slice(c0, c0 + DGRAD_CHUNK)
        du = du_of(cols)
        dx_ref[:, cols] = DN_ALPHA * dz_ref[:, cols] + du * (1.0 + s_ref[:, cols])
        ss_ref[:, cols] += _fold_rows(du * x_ref[:, cols])
        ssh_ref[:, cols] += _fold_rows(du)
    _finish_sums((ss_ref, ssh_ref), nsteps)


def dgrad_ffn(g, wt, dz, xin, s, name, ex=None):
    t, dm = dz.shape
    k = g.shape[1]
    tm = _row_tile(t)
    nsteps = t // tm

    def body(*refs):
        i = pl.program_id(0)
        g_ref, w_ref, dz_ref, x_ref, s_ref, dx_ref, ss_ref, ssh_ref = _hosted(ex, refs, 5, 3, i == 0, i == nsteps - 1)
        g_v = g_ref[...]
        _mod_bwd_store(lambda cols: _dot(g_v, w_ref[:, cols]), dz_ref, x_ref, s_ref, dx_ref, ss_ref, ssh_ref, nsteps)

    row = pl.BlockSpec((tm, dm), lambda i: (i, 0))
    acc = pl.BlockSpec((8, dm), lambda i: (0, 0))
    return _host_call(
        body, ex, name=name, grid=(nsteps,),
        out_shape=[jax.ShapeDtypeStruct((t, dm), F32)] + [jax.ShapeDtypeStruct((8, dm), F32)] * 2,
        in_specs=[pl.BlockSpec((tm, k), lambda i: (i, 0)), pl.BlockSpec((k, dm), lambda i: (0, 0)),
                  row, row, pl.BlockSpec((1, dm), lambda i: (0, 0))],
        out_specs=[row, acc, acc], sem=("arbitrary",), args=[g, wt, dz, xin, s])


def dgrad_in(d_a, d_b, dgab, wt, dz, xin, s, name, ex=None):
    t, dm = dz.shape
    tm = _row_tile(t)
    nsteps = t // tm
    dils = [a.shape[0] for a in d_b]

    def body(*refs):
        i = pl.program_id(0)
        (da_ref, b0, b1, b2, dg_ref, w_ref, dz_ref, x_ref, s_ref, dx_ref, ss_ref, ssh_ref,
         *scrs) = _hosted(ex, refs, 9, 3, i == 0, i == nsteps - 1)
        vs = [b_ref[...].reshape(tm, B_GW) for b_ref in (b0, b1, b2)]

        def du_of(cols):
            du = _dot(da_ref[0], w_ref[0:A_W, cols])
            for g, (v, d) in enumerate(zip(vs, dils)):
                part = None
                for p in range(3):
                    r0 = A_W + p * B_ALL + g * B_OUT_W
                    term = _dot(v[:, p * B_OUT_W:(p + 1) * B_OUT_W], w_ref[r0:r0 + B_OUT_W, cols])
                    part = term if part is None else part + term
                if d == 1:
                    du = du + part
                else:
                    n = tm // d
                    du = du + _unfold_rows(lambda r, cs: part[r * n:(r + 1) * n, cs], scrs, d, n, DGRAD_CHUNK)
            for j in range(2):
                du = du + _dot(dg_ref[:, j * dm:(j + 1) * dm], w_ref[GATE_COL + j * dm:GATE_COL + (j + 1) * dm, cols])
            return du

        _mod_bwd_store(du_of, dz_ref, x_ref, s_ref, dx_ref, ss_ref, ssh_ref, nsteps)

    row = pl.BlockSpec((tm, dm), lambda i: (i, 0))
    acc = pl.BlockSpec((8, dm), lambda i: (0, 0))
    return _host_call(
        body, ex, name=name, grid=(nsteps,),
        out_shape=[jax.ShapeDtypeStruct((t, dm), F32)] + [jax.ShapeDtypeStruct((8, dm), F32)] * 2,
        in_specs=[_folded_spec(1, tm, A_W)] + [_folded_spec(d, tm, B_GW) for d in dils]
                 + [pl.BlockSpec((tm, 2 * dm), lambda i: (i, 0)), pl.BlockSpec(wt.shape, lambda i: (0, 0)),
                    row, row, pl.BlockSpec((1, dm), lambda i: (0, 0))],
        out_specs=[row, acc, acc],
        scratch_shapes=_fold_scratch(tm, DGRAD_CHUNK), sem=("arbitrary",), args=[d_a, *d_b, dgab, wt, dz, xin, s])


def wgrad(a, b, buf, tn, nj, b0, o0, om, name):
    t, k = a.shape
    tt = ROW_TILE
    while tt * 2 * k <= WGRAD_TILE_ELEMS and tt * 2 <= t:
        tt *= 2
    nsteps = t // tt
    last = nsteps - 1

    def body(a_ref, b_ref, buf_ref, o_ref, acc):
        s, j = pl.program_id(0), pl.program_id(1)

        @pl.when(s == 0)
        def _():
            acc[j] = jnp.zeros(acc.shape[1:], F32)
        acc[j] += _dot_tn(a_ref[...], b_ref[...])

        @pl.when(s == last)
        def _():
            o_ref[...] = acc[j].astype(BF16)

    return pl.pallas_call(
        body, name=name, grid=(nsteps, nj),
        out_shape=jax.ShapeDtypeStruct(buf.shape, buf.dtype),
        in_specs=[pl.BlockSpec((tt, k), lambda s, j: (s, 0)),
                  pl.BlockSpec((tt, tn), lambda s, j: (s, b0 + j)),
                  pl.BlockSpec(memory_space=pl.ANY)],
        out_specs=pl.BlockSpec((k, tn), lambda s, j: (0, o0 + om * jnp.where(s == last, j, 0))),
        scratch_shapes=[pltpu.VMEM((nj, k, tn), F32)],
        input_output_aliases={2: 0},
        compiler_params=_cp("arbitrary", "arbitrary"),
    )(a, b, buf)


def dswiglu(dy, wdt, ab, name, ex=None):
    t, dm = dy.shape
    f = wdt.shape[1]
    tm = _row_tile(t)
    nsteps = t // tm

    def body(*refs):
        i = pl.program_id(0)
        dy_ref, w_ref, a_ref, b_ref, o_ref = _hosted(ex, refs, 4, 1, i == 0, i == nsteps - 1)
        dy_v = dy_ref[...]
        for c0 in range(0, f, FF_CHUNK):
            cols = slice(c0, c0 + FF_CHUNK)
            dh = _dot(dy_v, w_ref[:, cols])
            a = a_ref[:, cols].astype(F32)
            sg = _sigmoid(a)
            o_ref[:, cols] = (dh * b_ref[:, cols].astype(F32) * (sg * (1.0 + a * (1.0 - sg)))).astype(BF16)
            o_ref[:, f + c0:f + c0 + FF_CHUNK] = (dh * (a * sg)).astype(BF16)

    return _host_call(
        body, ex, name=name, grid=(nsteps,),
        out_shape=[jax.ShapeDtypeStruct((t, 2 * f), BF16)],
        in_specs=[pl.BlockSpec((tm, dm), lambda i: (i, 0)), pl.BlockSpec((dm, f), lambda i: (0, 0)),
                  pl.BlockSpec((tm, f), lambda i: (i, 0)), pl.BlockSpec((tm, f), lambda i: (i, 1))],
        out_specs=[pl.BlockSpec((tm, 2 * f), lambda i: (i, 0))], sem=("arbitrary",), args=[dy, wdt, ab, ab])


def dmerge(do, wot, ya, yb, w_a, w_b, wat, wbt, proj, name):
    t, dm = do.shape
    tm = _row_tile(t)
    nsteps = t // tm
    gcol = GATE_COL // dm
    ch = 256

    def body(do_ref, wot_ref, ya_ref, yb_ref, wa_ref, wb_ref, wat_ref, wbt_ref, g_ref,
             dya_ref, dyb_ref, dg_ref, gwa_ref, gwb_ref, dm_scr, acc_a, acc_b):
        i, j = pl.program_id(0), pl.program_id(1)

        @pl.when((i == 0) & (j == 0))
        def _():
            acc_a[...] = jnp.zeros_like(acc_a)
            acc_b[...] = jnp.zeros_like(acc_b)

        @pl.when(j == 0)
        def _():
            do_v = do_ref[...]
            for c0 in range(0, dm, ch):
                dm_scr[:, c0:c0 + ch] = _dot(do_v, wot_ref[:, c0:c0 + ch])

        def branch(y_ref, w_ref, wt_ref, dy_ref, acc, gw_ref):
            y = y_ref[...]
            dy = None
            for c0 in range(0, dm, ch):
                cols = slice(c0, c0 + ch)
                p = _dot(y, w_ref[:, cols])
                sg = _sigmoid(g_ref[:, cols].astype(F32))
                dmg = dm_scr[:, cols]
                dp = (dmg * sg).astype(BF16)
                dg_ref[:, cols] = (dmg * p * (sg * (1.0 - sg))).astype(BF16)
                acc[:, cols] += _dot_tn(y, dp)
                part = _dot(dp, wt_ref[cols, :])
                dy = part if dy is None else dy + part
            dy_ref[...] = dy.astype(dy_ref.dtype)

            @pl.when(i == nsteps - 1)
            def _():
                gw_ref[...] = acc[...].astype(BF16)

        pl.when(j == 0)(lambda: branch(ya_ref, wa_ref, wat_ref, dya_ref, acc_a, gwa_ref))
        pl.when(j == 1)(lambda: branch(yb_ref, wb_ref, wbt_ref, dyb_ref, acc_b, gwb_ref))

    full = lambda arr: pl.BlockSpec(arr.shape, lambda i, j: (0, 0))
    rowc = lambda w: pl.BlockSpec((tm, w), lambda i, j: (i, 0))
    return pl.pallas_call(
        body, name=name, grid=(nsteps, 2),
        out_shape=[jax.ShapeDtypeStruct((t, ya.shape[1]), BF16), jax.ShapeDtypeStruct((t, yb.shape[1]), F32),
                   jax.ShapeDtypeStruct((t, 2 * dm), BF16),
                   jax.ShapeDtypeStruct(w_a.shape, BF16), jax.ShapeDtypeStruct(w_b.shape, BF16)],
        in_specs=[rowc(dm), full(wot), rowc(ya.shape[1]), rowc(yb.shape[1]), full(w_a), full(w_b),
                  full(wat), full(wbt), pl.BlockSpec((tm, dm), lambda i, j: (i, gcol + j))],
        out_specs=[rowc(ya.shape[1]), rowc(yb.shape[1]), pl.BlockSpec((tm, dm), lambda i, j: (i, j)),
                   full(w_a), full(w_b)],
        scratch_shapes=[pltpu.VMEM((tm, dm), F32), pltpu.VMEM(w_a.shape, F32), pltpu.VMEM(w_b.shape, F32)],
        compiler_params=_cp("arbitrary", "arbitrary"),
    )(do, wot, ya, yb, w_a, w_b, wat, wbt, proj)


def mix_bwd(dyb, o_g, l_g, name):
    t, w = dyb.shape
    tm = _row_tile(t)
    nh = w // HEAD_DIM
    dils = [o.shape[0] for o in o_g]

    def body(dyb_ref, o0, o1, o2, l0, l1, l2, do0, do1, do2, dl0, dl1, dl2, *scr):
        ls = [_unfold_from(l, scr, d) for l, d in zip((l0, l1, l2), dils)]
        m = jnp.maximum(jnp.maximum(ls[0], ls[1]), ls[2])
        es = [jnp.exp(l - m) for l in ls]
        inv = 1.0 / (es[0] + es[1] + es[2])
        wts = [e * inv for e in es]
        dyb_v = dyb_ref[...]
        dws = []
        for o_ref, do_ref, wt, d in zip((o0, o1, o2), (do0, do1, do2), wts, dils):
            prod = dyb_v * _unfold_from(o_ref, scr, d)
            _fold_to(do_ref, dyb_v * wt, scr, d)
            for h in range(nh):
                hs = slice(h * HEAD_DIM, (h + 1) * HEAD_DIM)
                dws.append(jnp.broadcast_to(jnp.sum(prod[:, hs], axis=1, keepdims=True), (tm, HEAD_DIM)))
        for g, (dl_ref, d) in enumerate(zip((dl0, dl1, dl2), dils)):
            cols = []
            for h in range(nh):
                hs = slice(h * HEAD_DIM, (h + 1) * HEAD_DIM)
                mean = sum(wts[g2][:, hs] * dws[g2 * nh + h] for g2 in range(N_GROUPS))
                cols.append(wts[g][:, hs] * (dws[g * nh + h] - mean))
            _fold_to(dl_ref, jnp.concatenate(cols, axis=1), scr, d)

    folded = [_folded_spec(d, tm, w) for d in dils]
    return pl.pallas_call(
        body, name=name, grid=(t // tm,),
        out_shape=[jax.ShapeDtypeStruct(o.shape, BF16) for o in o_g]
                  + [jax.ShapeDtypeStruct(o.shape, F32) for o in o_g],
        in_specs=[pl.BlockSpec((tm, w), lambda i: (i, 0))] + folded + folded,
        out_specs=folded + folded,
        scratch_shapes=_fold_scratch(tm, w),
        compiler_params=_cp("parallel"),
    )(dyb, *o_g, *l_g)


def attn_bwd(qkv, o, lse, do, dlse, cfg, sinks, name):
    d, n, _ = qkv.shape
    tq, nsub, nqb = _attn_geometry(cfg, n)
    wq, wk, wout = cfg.wq, cfg.wk, cfg.wout
    grp = cfg.heads // cfg.kv_heads
    has_dl = dlse is not None
    scale = HEAD_DIM ** -0.5

    def body(*refs):
        sink_ref, q_ref, qn_ref, kc_ref, kp_ref, vc_ref, vp_ref = refs[:7]
        o_ref, on_ref, do_ref, don_ref, l_ref, ln_ref = refs[7:13]
        rest = refs[13:]
        dl_ref = dln_ref = None
        if has_dl:
            dl_ref, dln_ref = rest[:2]
            rest = rest[2:]
        out_ref = rest[0]
        rest = rest[1:]
        if cfg.sinks:
            dsink_ref = rest[0]
            rest = rest[1:]
        kf, vf, dk_acc, dv_acc = rest
        r, i = pl.program_id(0), pl.program_id(1)
        kf[0:BLOCK, :] = kp_ref[...]
        kf[BLOCK:, :] = kc_ref[...]
        vf[0:BLOCK, :] = vp_ref[...]
        vf[BLOCK:, :] = vc_ref[...]
        dist, valid, valid_first = _band(i, cfg.max_dist)
        distf = dist.astype(F32)
        next_dist = jnp.where(i < nqb - 1, cfg.max_dist, -1)
        valid_next = (dist[:, 0:BLOCK] >= 0) & (dist[:, 0:BLOCK] <= next_dist)
        if cfg.sinks:
            @pl.when((r == 0) & (i == 0))
            def _():
                dsink_ref[...] = jnp.zeros_like(dsink_ref)

        def stacked(ref, rows, heads):
            return _stack([ref[rows, _head(h)] for h in heads])

        def per_row(ref, rows, heads):
            return _stack([jnp.max(ref[rows, _head(h)], axis=1, keepdims=True) for h in heads])

        for kv in range(cfg.kv_heads):
            heads = [kv * grp + g for g in range(grp)]
            ks = slice(kv * HEAD_DIM, (kv + 1) * HEAD_DIM)
            dk_acc[...] = jnp.zeros_like(dk_acc)
            dv_acc[...] = jnp.zeros_like(dv_acc)
            biases = [_stack([_masked_bias(m, dd, cfg, h, d) for h in heads])
                      for m, dd in ((valid_first, distf), (valid, distf), (valid_next, distf[:, 0:BLOCK]))]
            tiles = []
            for a in range(nsub + 1):
                if a < nsub:
                    rows, win = slice(a * BLOCK, (a + 1) * BLOCK), slice(a * BLOCK, (a + 2) * BLOCK)
                    src = (q_ref, o_ref, do_ref, l_ref, dl_ref)
                else:
                    rows, win = slice(0, BLOCK), slice(nsub * BLOCK, (nsub + 1) * BLOCK)
                    src = (qn_ref, on_ref, don_ref, ln_ref, dln_ref)
                qs = stacked(src[0], rows, heads) * SCALE
                do_v = stacked(src[2], rows, heads)
                delta = jnp.sum(do_v.astype(F32) * stacked(src[1], rows, heads).astype(F32), axis=1, keepdims=True)
                lse_v = per_row(src[3], rows, heads)
                shift = (per_row(src[4], rows, heads) - delta) if has_dl else -delta
                k = kf[win, ks]
                s = _dot_nt(qs, k) + biases[0 if a == 0 else (1 if a < nsub else 2)]
                dp = _dot_nt(do_v, vf[win, ks])
                tiles.append((qs, do_v, k, delta, lse_v, shift, s, dp))
            grads = []
            for qs, do_v, k, delta, lse_v, shift, s, dp in tiles:
                p = jnp.exp(s - lse_v)
                grads.append(((p * (dp + shift)).astype(BF16), p.astype(BF16)))
            for a, ((qs, do_v, k, delta, lse_v, shift, _, _), (dsb, pb)) in enumerate(zip(tiles, grads)):
                if a < nsub:
                    dq = _dot(dsb, k) * SCALE
                    for g, h in enumerate(heads):
                        out_ref[a * BLOCK:(a + 1) * BLOCK, _head(h)] = dq[g * BLOCK:(g + 1) * BLOCK].astype(BF16)
                if a == 0:
                    kcols = slice(0, BLOCK)
                    dsb, pb = dsb[:, BLOCK:], pb[:, BLOCK:]
                elif a < nsub:
                    kcols = slice((a - 1) * BLOCK, (a + 1) * BLOCK)
                else:
                    kcols = slice((nsub - 1) * BLOCK, nsub * BLOCK)
                dk_acc[:, kcols] += _dot_tn(qs, dsb)
                dv_acc[:, kcols] += _dot_tn(do_v, pb)
                if cfg.sinks and a < nsub:
                    for g, h in enumerate(heads):
                        part = slice(g * BLOCK, (g + 1) * BLOCK)
                        psink = jnp.exp(sink_ref[h] - lse_v[part])
                        tot = jnp.sum(psink * (-delta[part]), axis=0, keepdims=True)
                        dsink_ref[h:h + 1, :] += jnp.broadcast_to(tot, (1, LANES))
            out_ref[:, wq + kv * HEAD_DIM: wq + (kv + 1) * HEAD_DIM] = dk_acc[...].T.astype(BF16)
            out_ref[:, wq + wk + kv * HEAD_DIM: wq + wk + (kv + 1) * HEAD_DIM] = dv_acc[...].T.astype(BF16)

    prev = lambda i: jnp.maximum(i * nsub - 1, 0)
    nxt = lambda i: jnp.minimum((i + 1) * nsub, n // BLOCK - 1)
    cur = lambda w, c: pl.BlockSpec((None, tq, w), lambda r, i: (r, i, c // w))
    prv = lambda w, c: pl.BlockSpec((None, BLOCK, w), lambda r, i: (r, prev(i), c // w))
    o_cur = pl.BlockSpec((None, tq, wq), lambda r, i: (r, i, 0))
    o_nxt = pl.BlockSpec((None, BLOCK, wq), lambda r, i: (r, nxt(i), 0))
    in_specs = [pl.BlockSpec(memory_space=pltpu.SMEM),
                cur(wq, cfg.qc), pl.BlockSpec((None, BLOCK, wq), lambda r, i: (r, nxt(i), cfg.qc // wq)),
                cur(wk, cfg.kc), prv(wk, cfg.kc), cur(wk, cfg.vc), prv(wk, cfg.vc),
                o_cur, o_nxt, o_cur, o_nxt, o_cur, o_nxt]
    args = [sinks, qkv, qkv, qkv, qkv, qkv, qkv, o, o, do, do, lse, lse]
    if has_dl:
        in_specs += [o_cur, o_nxt]
        args += [dlse, dlse]
    out_shape = [jax.ShapeDtypeStruct((d, n, wout), BF16)]
    out_specs = [pl.BlockSpec((None, tq, wout), lambda r, i: (r, i, 0))]
    if cfg.sinks:
        out_shape.append(jax.ShapeDtypeStruct((8, LANES), F32))
        out_specs.append(pl.BlockSpec((8, LANES), lambda r, i: (0, 0)))
    return pl.pallas_call(
        body, name=name, grid=(d, nqb), out_shape=out_shape, in_specs=in_specs, out_specs=out_specs,
        scratch_shapes=[pltpu.VMEM((tq + BLOCK, wk), BF16), pltpu.VMEM((tq + BLOCK, wk), BF16),
                        pltpu.VMEM((HEAD_DIM, tq), F32), pltpu.VMEM((HEAD_DIM, tq), F32)],
        compiler_params=_cp("arbitrary", "arbitrary"),
    )(*args)


def _adamw(g, w, m, v):
    m = ADAM_B1 * m + (1.0 - ADAM_B1) * g
    v = ADAM_B2 * v + (1.0 - ADAM_B2) * (g * g)
    m_hat = m / (1.0 - ADAM_B1 ** ADAM_STEP)
    v_hat = v / (1.0 - ADAM_B2 ** ADAM_STEP)
    delta = -ADAM_LR * (m_hat / (jnp.sqrt(v_hat) + ADAM_EPS) + ADAM_WD * w)
    return delta, m, v


def adam_reduce(parts, w, m, v, name):
    r, c = w.shape
    tr = next(cand for cand in (256, 128, 64, 32, 16, 8) if r % cand == 0) if r > 256 else r

    def body(p_ref, w_ref, m_ref, v_ref, g_ref, d_ref, mo_ref, vo_ref):
        g = p_ref[0].astype(F32)
        for j in range(1, N_DEV):
            g = g + p_ref[j].astype(F32)
        g_ref[...] = g
        d_ref[...], mo_ref[...], vo_ref[...] = _adamw(g, w_ref[...], m_ref[...], v_ref[...])

    row = pl.BlockSpec((tr, c), lambda i: (i, 0))
    return pl.pallas_call(
        body, name=name, grid=(r // tr,),
        out_shape=[jax.ShapeDtypeStruct((r, c), F32)] * 4,
        in_specs=[pl.BlockSpec((N_DEV, tr, c), lambda i: (0, i, 0)), row, row, row],
        out_specs=[row] * 4,
        compiler_params=_cp("parallel"),
    )(parts, w, m, v)


def adam_layers(parts, w, m, v, name):
    nl, r, c = w.shape
    tr = next(cand for cand in (256, 128, 64, 32, 16, 8) if r % cand == 0)
    steps = r // tr

    def body(*refs):
        p_refs = refs[:nl]
        w_ref, m_ref, v_ref, g_ref, d_ref, mo_ref, vo_ref = refs[nl:]
        for k in range(nl):
            @pl.when(pl.program_id(0) == k)
            def _():
                g = p_refs[k][0].astype(F32)
                for j in range(1, N_DEV):
                    g = g + p_refs[k][j].astype(F32)
                g_ref[...] = g
                d_ref[...], mo_ref[...], vo_ref[...] = _adamw(g, w_ref[...], m_ref[...], v_ref[...])

    def part_spec(k):
        return pl.BlockSpec((N_DEV, tr, c), lambda l, i: (0, jnp.clip(i + (l - k) * steps, 0, steps - 1), 0))

    blk = pl.BlockSpec((None, tr, c), lambda l, i: (l, i, 0))
    return pl.pallas_call(
        body, name=name, grid=(nl, steps),
        out_shape=[jax.ShapeDtypeStruct((nl, r, c), F32)] * 4,
        in_specs=[part_spec(k) for k in range(nl)] + [blk, blk, blk],
        out_specs=[blk] * 4,
        compiler_params=_cp("arbitrary", "arbitrary"),
    )(*parts, w, m, v)


def adam_w_ada(sct, dm_loc, w, m, v):
    nl, dm, wc = w.shape
    tr = 512

    def body(s_ref, d_ref, w_ref, m_ref, v_ref, g_ref, dl_ref, mo_ref, vo_ref):
        g = jnp.dot(s_ref[...], d_ref[...], preferred_element_type=F32, precision=lax.Precision.HIGHEST)
        g_ref[...] = g
        dl_ref[...], mo_ref[...], vo_ref[...] = _adamw(g, w_ref[...], m_ref[...], v_ref[...])

    blk = pl.BlockSpec((None, tr, wc), lambda l, i: (l, i, 0))
    return pl.pallas_call(
        body, name="adam_w_ada", grid=(nl, dm // tr),
        out_shape=[jax.ShapeDtypeStruct(w.shape, F32)] * 4,
        in_specs=[pl.BlockSpec((tr, LANES), lambda l, i: (i, 0)),
                  pl.BlockSpec((None, LANES, wc), lambda l, i: (l, 0, 0)), blk, blk, blk],
        out_specs=[blk] * 4,
        compiler_params=_cp("parallel", "parallel"),
    )(sct, dm_loc, w, m, v)


TRANSPOSED = ("w_gate", "w_up")


def _pieces(dm):
    ncol = lambda n: n // N_DEV
    mixer = ([Piece("w_in", "w_in", 1, 0, ncol(GATE_COL + 2 * dm)),
              Piece("w_a", "w_a", 1, 0, ncol(dm)),
              Piece("w_b", "w_b", 1, 0, ncol(dm)),
              Piece("w_o", "w_o", 0, 0, ncol(dm))],
             {"w_in": (dm, GATE_COL + 2 * dm), "w_a": (A_Q_HEADS * HEAD_DIM, dm), "w_b": (B_OUT_W, dm),
              "w_o": (dm, dm)})
    ffn = ([Piece("w_gate", "w_ffn_t", 0, 0, ncol(D_FF)),
            Piece("w_up", "w_ffn_t", 0, D_FF, ncol(D_FF)),
            Piece("w_down", "w_down", 0, 0, ncol(D_FF))],
           {"w_ffn_t": (2 * D_FF, dm), "w_down": (D_FF, dm)})
    return mixer, ffn


def kernel(x, c, w_ada, b_ada, w_in, sinks, w_a, w_b, w_o, ln1_g, ln1_b, w_gate, w_up, w_down, ln2_g, ln2_b, loss_target, m_w_ada, m_b_ada, m_w_in, m_sinks, m_w_a, m_w_b, m_w_o, m_ln1_g, m_ln1_b, m_w_gate, m_w_up, m_w_down, m_ln2_g, m_ln2_b, v_w_ada, v_b_ada, v_w_in, v_sinks, v_w_a, v_w_b, v_w_o, v_ln1_g, v_ln1_b, v_w_gate, v_w_up, v_w_down, v_ln2_g, v_ln2_b):
    given = dict(locals())
    nl = w_in.shape[0]
    t, dm = x.shape[1], x.shape[2]
    me = 4 * lax.axis_index("x") + 2 * lax.axis_index("y") + lax.axis_index("c")
    x0 = x.reshape(t, dm)
    target = loss_target.reshape(t, dm)

    groups = dict(zip(("mixer", "ffn"), _pieces(dm)))
    local = lambda nm, pre="": (given[pre + nm].transpose(0, 2, 1) if nm in TRANSPOSED else given[pre + nm])
    shards = {pc.name: local(pc.name).astype(BF16) for pcs, _ in groups.values() for pc in pcs}

    def gather(group, l):
        pcs, bufs = groups[group]
        return Exchange("gather", pcs, [shards[pc.name][l] for pc in pcs], bufs.values(), bufs)

    def scatter(group, gbuf):
        pcs, bufs = groups[group]
        return Exchange("scatter", pcs, [gbuf[nm] for nm in bufs],
                        [(N_DEV,) + shards[pc.name].shape[1:] for pc in pcs], bufs)

    full = [dict() for _ in range(nl)]
    full[0].update(zip(groups["mixer"][1], run_exchange(gather("mixer", 0), "gather_mixer")))

    wc = w_ada.shape[2]
    c_all = all_gather_small(jnp.broadcast_to(c, (8, dm)), "gather_c")[:, 0, :]
    b_loc = lax.dynamic_slice_in_dim(b_ada, me * wc, wc, axis=1).reshape(nl, 1, wc)
    mp, sc_all = mod_partial(c_all, w_ada, b_loc)
    mp_all = all_gather_small(mp.reshape(nl * N_DEV, wc), "gather_mod").reshape(N_DEV, nl, N_DEV, wc)
    mod = lax.dynamic_index_in_dim(mp_all, me, axis=2, keepdims=False)
    mod = mod.transpose(1, 0, 2).reshape(nl, 6, 1, dm)

    vec = lambda a, l: a[l].reshape(1, dm)

    saved = []
    xl = x0
    for l in range(nl):
        sh1, s1, g1, sh2, s2, g2 = [mod[l, j] for j in range(6)]
        w = full[l]
        (u1, u1_f4, u1_f16, proj, qkv_f4, qkv_f16), got = in_proj(xl, s1, sh1, w["w_in"], "in_proj",
                                                                   gather("ffn", l))
        w.update(zip(groups["ffn"][1], got))
        proj3 = proj.reshape(1, t, proj.shape[1])
        qkv_b = [proj3, qkv_f4, qkv_f16]
        ya, lse_a = attn_fwd(proj3, ATTN_A, sinks[l], "attn_a_fwd")
        o_g, l_g = [], []
        for g, cfg in enumerate(ATTN_B):
            o, ls = attn_fwd(qkv_b[g], cfg, sinks[l], "attn_b%d_fwd" % g)
            o_g.append(o)
            l_g.append(ls)
        yb, merged = mix_merge(ya[0], o_g, l_g, proj, w["w_a"], w["w_b"], "mix_merge")
        y1, x1, zh1, rs1 = proj_ln(merged, w["w_o"], xl, g1, vec(ln1_g, l), vec(ln1_b, l), "out_proj_ln")
        (u2, ab), got = modmm(x1, s2, sh2, w["w_ffn_t"].T, "ffn_up", gather("mixer", l + 1) if l + 1 < nl else None)
        if l + 1 < nl:
            full[l + 1].update(zip(groups["mixer"][1], got))
        h, y2, x2, zh2, rs2 = swiglu_proj_ln(ab, w["w_down"], x1, g2, vec(ln2_g, l), vec(ln2_b, l), "ffn_down_ln")
        saved.append(dict(xin=xl, u1=[u1, u1_f4.reshape(t, dm), u1_f16.reshape(t, dm)], proj=proj, qkv_b=qkv_b,
                          ya=ya, lse_a=lse_a, o_g=o_g, l_g=l_g, yb=yb, merged=merged,
                          y1=y1, x1=x1, zh1=zh1, rs1=rs1, u2=u2, ab=ab, h=h, y2=y2, zh2=zh2, rs2=rs2))
        xl = x2

    dx, loss_part = loss_head(xl, target)

    small = {k: [None] * nl for k in ("dmod", "ln1_g", "ln1_b", "ln2_g", "ln2_b", "sinks")}
    recv = {nm: [None] * nl for grp in groups.values() for nm in (pc.name for pc in grp[0])}

    def keep(group, l, got):
        for pc, arr in zip(groups[group][0], got):
            recv[pc.name][l] = arr

    for l in reversed(range(nl)):
        sv, w = saved[l], full[l]
        sh1, s1, g1, sh2, s2, g2 = [mod[l, j] for j in range(6)]
        fresh = lambda nm: lax.empty({**groups["mixer"][1], **groups["ffn"][1]}[nm], BF16)
        gbuf = {}
        dz2, dy2, sg, sb, sgate2, gbuf["w_down"] = ln_bwd(dx, sv["zh2"], sv["rs2"], sv["y2"], vec(ln2_g, l), g2,
                                                          sv["h"], "ln_bwd_ffn")
        small["ln2_g"][l], small["ln2_b"][l] = sg[0], sb[0]
        (dab,), _ = dswiglu(dy2, w["w_down"].T, sv["ab"], "dswiglu")
        gbuf["w_ffn_t"] = wgrad(dab, sv["u2"], fresh("w_ffn_t"), 512, dm // 512, 0, 0, 1, "wgrad_ffn_up")
        (dx1, ss2, ssh2), got = dgrad_ffn(dab, w["w_ffn_t"], dz2, sv["x1"], s2, "dgrad_ffn", scatter("ffn", gbuf))
        keep("ffn", l, got)
        dz1, do1, sg, sb, sgate1, gbuf["w_o"] = ln_bwd(dx1, sv["zh1"], sv["rs1"], sv["y1"], vec(ln1_g, l), g1,
                                                       sv["merged"], "ln_bwd_mixer")
        small["ln1_g"][l], small["ln1_b"][l] = sg[0], sb[0]
        dya, dyb, dgab, gbuf["w_a"], gbuf["w_b"] = dmerge(do1, w["w_o"].T, sv["ya"][0], sv["yb"], w["w_a"],
                                                          w["w_b"], w["w_a"].T, w["w_b"].T, sv["proj"], "dmerge")
        mixed = mix_bwd(dyb, sv["o_g"], sv["l_g"], "mix_bwd")
        do_g, dl_g = mixed[:N_GROUPS], mixed[N_GROUPS:]
        d_a, dsink = attn_bwd(sv["qkv_b"][0], sv["ya"], sv["lse_a"], dya.reshape(1, t, -1), None, ATTN_A,
                              sinks[l], "attn_a_bwd")
        small["sinks"][l] = dsink[:, 0]
        d_b = [attn_bwd(sv["qkv_b"][g], sv["o_g"][g], sv["l_g"][g], do_g[g], dl_g[g], cfg, sinks[l],
                        "attn_b%d_bwd" % g)[0] for g, cfg in enumerate(ATTN_B)]
        gw = wgrad(sv["u1"][0], d_a.reshape(t, A_W), fresh("w_in"), A_W, 1, 0, 0, 1, "wgrad_in_a")
        for g in range(N_GROUPS):
            gw = wgrad(sv["u1"][g], d_b[g].reshape(t, B_GW), gw, B_OUT_W, 3, 0, A_W // B_OUT_W + g, N_GROUPS,
                       "wgrad_in_b%d" % g)
        gbuf["w_in"] = wgrad(sv["u1"][0], dgab, gw, 512, 2 * dm // 512, 0, GATE_COL // 512, 1, "wgrad_in_gate")
        (dx, ss1, ssh1), got = dgrad_in(d_a, d_b, dgab, w["w_in"].T, dz1, sv["xin"], s1, "dgrad_in",
                                        scatter("mixer", gbuf))
        keep("mixer", l, got)
        small["dmod"][l] = jnp.stack([ssh1[0], ss1[0], sgate1[0], ssh2[0], ss2[0], sgate2[0]])
    grad_x = dx.reshape(x.shape)

    big_out = {}
    for nm, parts in recv.items():
        outs = adam_layers(parts, local(nm), local(nm, "m_"), local(nm, "v_"), "adam_" + nm)
        big_out[nm] = [o.transpose(0, 2, 1) for o in outs] if nm in TRANSPOSED else outs

    rows = jnp.concatenate(
        [jnp.stack(small["dmod"]).reshape(nl * 6, dm)]
        + [jnp.stack(small[k]) for k in ("ln1_g", "ln1_b", "ln2_g", "ln2_b")]
        + [jnp.pad(jnp.stack(small["sinks"]).reshape(1, -1), ((0, 0), (0, dm - nl * A_Q_HEADS))),
           jnp.broadcast_to(loss_part[0:1, 0:1], (1, dm))])
    n_rows = rows.shape[0]
    rows = jnp.pad(rows, ((0, -n_rows % 8), (0, 0)))
    rows_all = all_gather_small(rows, "gather_small_grads")

    def pack_small(pre):
        parts = [given[pre + "b_ada"].reshape(nl * 6, dm)]
        parts += [given[pre + k] for k in ("ln1_g", "ln1_b", "ln2_g", "ln2_b")]
        parts.append(jnp.pad(given[pre + "sinks"].reshape(1, -1), ((0, 0), (0, dm - nl * A_Q_HEADS))))
        p = jnp.concatenate(parts)
        return jnp.pad(p, ((0, rows.shape[0] - p.shape[0]), (0, 0)))

    souts = adam_reduce(rows_all, pack_small(""), pack_small("m_"), pack_small("v_"), "adam_small")

    def unpack_small(o):
        r = {"b_ada": o[0:nl * 6].reshape(nl, 6 * dm)}
        for j, k in enumerate(("ln1_g", "ln1_b", "ln2_g", "ln2_b")):
            r[k] = o[nl * 6 + j * nl: nl * 6 + (j + 1) * nl]
        r["sinks"] = o[nl * 10, 0:nl * A_Q_HEADS].reshape(nl, A_Q_HEADS)
        return r

    small_out = [unpack_small(o) for o in souts]
    loss = souts[0][nl * 10 + 1, 0]

    dmod_all = rows_all[:, 0:nl * 6].reshape(N_DEV, nl, 6 * dm)
    dm_loc = lax.dynamic_slice_in_dim(dmod_all, me * wc, wc, axis=2).transpose(1, 0, 2)
    dm_loc = jnp.pad(dm_loc, ((0, 0), (0, LANES - N_DEV), (0, 0)))
    sct = jnp.pad(sc_all.T, ((0, 0), (0, LANES - N_DEV)))
    ada_out = adam_w_ada(sct, dm_loc, w_ada, m_w_ada, v_w_ada)

    names = ["w_ada", "b_ada", "w_in", "sinks", "w_a", "w_b", "w_o", "ln1_g", "ln1_b",
             "w_gate", "w_up", "w_down", "ln2_g", "ln2_b"]

    def pick(kind, nm):
        if nm == "w_ada":
            return ada_out[kind]
        if nm in small_out[kind]:
            return small_out[kind][nm]
        return big_out[nm][kind]

    result = [loss, grad_x]
    for kind in range(4):
        result += [pick(kind, nm) for nm in names]
    return tuple(result)
```

```python
import functools

import jax
import jax.numpy as jnp
from jax import lax
from jax.experimental import pallas as pl
from jax.experimental.pallas import tpu as pltpu

F32 = jnp.float32
BF16 = jnp.bfloat16

D_MODEL = 1024
HEAD_DIM = 64
A_Q_HEADS = 8
A_KV_HEADS = 2
A_WINDOW = 128
B_GROUPS = ((128, 1), (512, 4), (2048, 16))
N_GROUPS = len(B_GROUPS)
B_HEADS_PER_GROUP = 4
N_ATTN_HEADS = A_Q_HEADS + B_HEADS_PER_GROUP * N_GROUPS
BLOCK = 128
A_W = (A_Q_HEADS + 2 * A_KV_HEADS) * HEAD_DIM
B_OUT_W = B_HEADS_PER_GROUP * HEAD_DIM
B_GW = 3 * B_OUT_W
B_ALL = N_GROUPS * B_OUT_W
GATE_COL = A_W + 3 * B_ALL
D_FF = 2816
FF_CHUNK = 256
DGRAD_CHUNK = 256
DN_ALPHA = 8.0 ** 0.25
LN_EPS = 1e-5
NEG_INF = -1e30
ADAM_LR, ADAM_B1, ADAM_B2, ADAM_EPS, ADAM_WD, ADAM_STEP = 0.001, 0.9, 0.999, 1e-08, 0.01, 10

N_DEV = 8
MESH = pl.DeviceIdType.MESH
VMEM_LIMIT = 56 * 1024 * 1024
ROW_TILE = 512
WGRAD_TILE_ELEMS = 2 * 1024 * 1024
LANES = 128
BF16_ROWS = 16


def _cp(*sem):
    return pltpu.CompilerParams(dimension_semantics=sem, vmem_limit_bytes=VMEM_LIMIT)


def _row_tile(t):
    return min(ROW_TILE, t)


def _slope(head):
    return 2.0 ** (-8.0 * (head + 1) / N_ATTN_HEADS)


def _sigmoid(x):
    return 1.0 / (1.0 + jnp.exp(-x))


def _dot(a, b):
    return jnp.dot(a, b, preferred_element_type=F32)


def _dot_nt(a, b):
    return lax.dot_general(a, b, (((1,), (1,)), ((), ())), preferred_element_type=F32)


def _dot_tn(a, b):
    return lax.dot_general(a, b, (((0,), (0,)), ((), ())), preferred_element_type=F32)


def _fold_scratch(tm, w):
    return [pltpu.VMEM((tm, LANES), F32)] * (w // LANES)


def _fold_to(dst_ref, val, scrs, d, col0=0):
    tm, w = val.shape
    if d == 1:
        dst_ref[0, :, col0:col0 + w] = val.astype(dst_ref.dtype)
        return
    for cb in range(w // LANES):
        scrs[cb][...] = val[:, cb * LANES:(cb + 1) * LANES]
    for r in range(d):
        for cb in range(w // LANES):
            piece = scrs[cb][pl.ds(r, tm // d, stride=d), :]
            dst_ref[r, :, col0 + cb * LANES:col0 + (cb + 1) * LANES] = piece.astype(dst_ref.dtype)


def _unfold_rows(rows_of, scrs, d, n, w):
    for r in range(d):
        for cb in range(w // LANES):
            scrs[cb][pl.ds(r, n, stride=d), :] = rows_of(r, slice(cb * LANES, (cb + 1) * LANES)).astype(F32)
    return jnp.concatenate([scrs[cb][0:d * n, :] for cb in range(w // LANES)], axis=1)


def _unfold_from(src_ref, scrs, d):
    if d == 1:
        return src_ref[0].astype(F32)
    _, n, w = src_ref.shape
    return _unfold_rows(lambda r, cols: src_ref[r, :, cols], scrs, d, n, w)


def _folded_spec(d, tm, w):
    return pl.BlockSpec((d, tm // d, w), lambda i: (0, i, 0))


def _me():
    return lax.axis_index("x"), lax.axis_index("y"), lax.axis_index("c")


def _flip(v, bit):
    return 1 - v if bit else v


def _peer(k):
    x, y, c = _me()
    return (_flip(x, k & 4), _flip(y, k & 2), _flip(c, k & 1))


def _peer_index(k):
    px, py, pc = _peer(k)
    return 4 * px + 2 * py + pc


def all_gather_small(v, name):
    r, c = v.shape

    def body(v_ref, out_ref, send_sems, recv_sems):
        me = _peer_index(0)
        out_ref[me] = v_ref[...]
        copies = []
        for k in range(1, N_DEV):
            cp = pltpu.make_async_remote_copy(
                src_ref=v_ref, dst_ref=out_ref.at[me],
                send_sem=send_sems.at[k - 1], recv_sem=recv_sems.at[k - 1],
                device_id=_peer(k), device_id_type=MESH)
            cp.start()
            copies.append(cp)
        for k in range(1, N_DEV):
            pltpu.make_async_remote_copy(
                src_ref=v_ref, dst_ref=out_ref.at[_peer_index(k)],
                send_sem=send_sems.at[k - 1], recv_sem=recv_sems.at[k - 1],
                device_id=_peer(k), device_id_type=MESH).wait_recv()
        for cp in copies:
            cp.wait_send()

    return pl.pallas_call(
        body, name=name,
        out_shape=jax.ShapeDtypeStruct((N_DEV, r, c), v.dtype),
        in_specs=[pl.BlockSpec(memory_space=pltpu.VMEM)],
        out_specs=pl.BlockSpec(memory_space=pltpu.VMEM),
        scratch_shapes=[pltpu.SemaphoreType.DMA((N_DEV - 1,)), pltpu.SemaphoreType.DMA((N_DEV - 1,))],
        compiler_params=pltpu.CompilerParams(vmem_limit_bytes=VMEM_LIMIT),
    )(v)


class Piece:
    def __init__(self, name, buf, axis, base, size):
        self.name, self.buf, self.axis, self.base, self.size = name, buf, axis, base, size

    def window(self, ref, j):
        start = self.base + j * self.size
        if self.axis == 1:
            return ref.at[:, pl.ds(pl.multiple_of(start, LANES), self.size)]
        return ref.at[pl.ds(pl.multiple_of(start, BF16_ROWS), self.size), :]


class Exchange:
    def __init__(self, kind, pieces, ins, out_shapes, bufs):
        self.kind, self.pieces, self.ins, self.out_shapes = kind, pieces, list(ins), list(out_shapes)
        self.buf_of = {nm: i for i, nm in enumerate(bufs)}
        self.n_in, self.n_out = len(self.ins), len(self.out_shapes)
        n = len(pieces)
        self.scratch = [pltpu.SemaphoreType.DMA((n, N_DEV - 1)), pltpu.SemaphoreType.DMA((n, N_DEV - 1)),
                        pltpu.SemaphoreType.DMA((n,))]
        self.in_specs = [pl.BlockSpec(memory_space=pl.ANY)] * self.n_in
        self.out_specs = [pl.BlockSpec(memory_space=pl.ANY)] * self.n_out
        self.out_shape = [jax.ShapeDtypeStruct(s, BF16) for s in self.out_shapes]

    def _ends(self, pi, ins, outs, to):
        pc = self.pieces[pi]
        if self.kind == "gather":
            return ins[pi], pc.window(outs[self.buf_of[pc.buf]], _peer_index(0))
        return pc.window(ins[self.buf_of[pc.buf]], to), outs[pi].at[_peer_index(0)]

    def _landing(self, pi, outs, frm):
        pc = self.pieces[pi]
        if self.kind == "gather":
            return pc.window(outs[self.buf_of[pc.buf]], frm)
        return outs[pi].at[frm]

    def _remote(self, pi, k, src, dst, sems):
        return pltpu.make_async_remote_copy(
            src_ref=src, dst_ref=dst, send_sem=sems[0].at[pi, k - 1], recv_sem=sems[1].at[pi, k - 1],
            device_id=_peer(k), device_id_type=MESH)

    def _local(self, pi, ins, outs, sems):
        return pltpu.make_async_copy(*self._ends(pi, ins, outs, _peer_index(0)), sems[2].at[pi])

    def start(self, ins, outs, sems):
        for pi in range(len(self.pieces)):
            self._local(pi, ins, outs, sems).start()
            for k in range(1, N_DEV):
                self._remote(pi, k, *self._ends(pi, ins, outs, _peer_index(k)), sems).start()

    def finish(self, ins, outs, sems):
        for pi in range(len(self.pieces)):
            src_like = self._ends(pi, ins, outs, _peer_index(0))[0]
            for k in range(1, N_DEV):
                self._remote(pi, k, src_like, self._landing(pi, outs, _peer_index(k)), sems).wait_recv()
        for pi in range(len(self.pieces)):
            for k in range(1, N_DEV):
                self._remote(pi, k, *self._ends(pi, ins, outs, _peer_index(k)), sems).wait_send()
            self._local(pi, ins, outs, sems).wait()


def _hosted(ex, refs, n_in, n_out, first, last):
    if ex is None:
        return refs
    ins, rest = refs[:n_in], refs[n_in:]
    ex_ins, rest = rest[:ex.n_in], rest[ex.n_in:]
    outs, rest = rest[:n_out], rest[n_out:]
    ex_outs, rest = rest[:ex.n_out], rest[ex.n_out:]
    scr, sems = rest[:len(rest) - 3], rest[len(rest) - 3:]
    pl.when(first)(lambda: ex.start(ex_ins, ex_outs, sems))
    pl.when(last)(lambda: ex.finish(ex_ins, ex_outs, sems))
    return tuple(ins) + tuple(outs) + tuple(scr)


def _host_call(body, ex, *, name, grid, out_shape, in_specs, out_specs, scratch_shapes=(), sem=None, args):
    n_out = len(out_shape)
    if ex is not None:
        out_shape = list(out_shape) + ex.out_shape
        in_specs = list(in_specs) + ex.in_specs
        out_specs = list(out_specs) + ex.out_specs
        scratch_shapes = list(scratch_shapes) + ex.scratch
        args = list(args) + ex.ins
    res = pl.pallas_call(body, name=name, grid=grid, out_shape=out_shape, in_specs=in_specs, out_specs=out_specs,
                         scratch_shapes=scratch_shapes, compiler_params=_cp(*sem))(*args)
    return res[:n_out], res[n_out:]


def run_exchange(ex, name):
    def body(*refs):
        ins, outs, sems = refs[:ex.n_in], refs[ex.n_in:ex.n_in + ex.n_out], refs[ex.n_in + ex.n_out:]
        ex.start(ins, outs, sems)
        ex.finish(ins, outs, sems)

    return pl.pallas_call(body, name=name, out_shape=ex.out_shape, in_specs=ex.in_specs, out_specs=ex.out_specs,
                          scratch_shapes=ex.scratch)(*ex.ins)


def mod_partial(c_all, w_ada, b_loc):
    nl, dm, wc = w_ada.shape

    def body(c_ref, w_ref, b_ref, o_ref, sc_ref):
        cc = c_ref[...]
        sc = cc * _sigmoid(cc)
        sc_ref[...] = sc
        o_ref[...] = jnp.dot(sc, w_ref[...], preferred_element_type=F32,
                             precision=lax.Precision.HIGHEST) + b_ref[...]

    return pl.pallas_call(
        body, name="mod_partial", grid=(nl,),
        out_shape=[jax.ShapeDtypeStruct((nl, N_DEV, wc), F32), jax.ShapeDtypeStruct((N_DEV, dm), F32)],
        in_specs=[pl.BlockSpec((N_DEV, dm), lambda l: (0, 0)),
                  pl.BlockSpec((None, dm, wc), lambda l: (l, 0, 0)),
                  pl.BlockSpec((None, 1, wc), lambda l: (l, 0, 0))],
        out_specs=[pl.BlockSpec((None, N_DEV, wc), lambda l: (l, 0, 0)),
                   pl.BlockSpec((N_DEV, dm), lambda l: (0, 0))],
        compiler_params=_cp("arbitrary"),
    )(c_all, w_ada, b_loc)


def in_proj(x, s, sh, w, name, ex=None):
    t, dm = x.shape
    n = w.shape[1]
    tm = _row_tile(t)
    nsteps = t // tm
    ch = B_OUT_W
    dils = [dil for _, dil in B_GROUPS if dil > 1]

    def body(*refs):
        i = pl.program_id(0)
        x_ref, s_ref, sh_ref, w_ref, u_ref, *rest = _hosted(ex, refs, 4, 2 + 2 * len(dils), i == 0, i == nsteps - 1)
        uf_refs, o_ref, qf_refs = rest[:len(dils)], rest[len(dils)], rest[len(dils) + 1:len(dils) * 2 + 1]
        scrs = rest[len(dils) * 2 + 1:]
        uf = x_ref[...] * (1.0 + s_ref[...]) + sh_ref[...]
        u = uf.astype(BF16)
        u_ref[...] = u
        for d, uf_ref in zip(dils, uf_refs):
            _fold_to(uf_ref, uf, scrs, d)
        for c0 in range(0, n, ch):
            res = _dot(u, w_ref[:, c0:c0 + ch])
            o_ref[:, c0:c0 + ch] = res.astype(BF16)
            if A_W <= c0 < GATE_COL:
                part, g = divmod((c0 - A_W) // ch, N_GROUPS)
                d = B_GROUPS[g][1]
                if d > 1:
                    _fold_to(qf_refs[dils.index(d)], res, scrs, d, part * ch)

    vec = pl.BlockSpec((1, dm), lambda i: (0, 0))
    row = lambda w_: pl.BlockSpec((tm, w_), lambda i: (i, 0))
    return _host_call(
        body, ex, name=name, grid=(nsteps,),
        out_shape=[jax.ShapeDtypeStruct((t, dm), BF16)]
                  + [jax.ShapeDtypeStruct((d, t // d, dm), BF16) for d in dils]
                  + [jax.ShapeDtypeStruct((t, n), BF16)]
                  + [jax.ShapeDtypeStruct((d, t // d, B_GW), BF16) for d in dils],
        in_specs=[row(dm), vec, vec, pl.BlockSpec((dm, n), lambda i: (0, 0))],
        out_specs=[row(dm)] + [_folded_spec(d, tm, dm) for d in dils] + [row(n)]
                  + [_folded_spec(d, tm, B_GW) for d in dils],
        scratch_shapes=_fold_scratch(tm, dm), sem=("arbitrary",), args=[x, s, sh, w])


def modmm(x, s, sh, w, name, ex=None):
    t, dm = x.shape
    n = w.shape[1]
    tm = _row_tile(t)
    nsteps = t // tm
    ch = 512

    def body(*refs):
        i = pl.program_id(0)
        x_ref, s_ref, sh_ref, w_ref, u_ref, o_ref = _hosted(ex, refs, 4, 2, i == 0, i == nsteps - 1)
        u = (x_ref[...] * (1.0 + s_ref[...]) + sh_ref[...]).astype(BF16)
        u_ref[...] = u
        for c0 in range(0, n, ch):
            o_ref[:, c0:c0 + ch] = _dot(u, w_ref[:, c0:c0 + ch]).astype(BF16)

    vec = pl.BlockSpec((1, dm), lambda i: (0, 0))
    return _host_call(
        body, ex, name=name, grid=(nsteps,),
        out_shape=[jax.ShapeDtypeStruct((t, dm), BF16), jax.ShapeDtypeStruct((t, n), BF16)],
        in_specs=[pl.BlockSpec((tm, dm), lambda i: (i, 0)), vec, vec,
                  pl.BlockSpec((dm, n), lambda i: (0, 0))],
        out_specs=[pl.BlockSpec((tm, dm), lambda i: (i, 0)), pl.BlockSpec((tm, n), lambda i: (i, 0))],
        sem=("arbitrary",), args=[x, s, sh, w])


def _halves(tm):
    half = tm // 2 if tm % 32 == 0 else tm
    return [slice(r0, r0 + half) for r0 in range(0, tm, half)]


def _ln_store(y, rows, xres_ref, g_ref, lg_ref, lb_ref, y_ref, xo_ref, zh_ref, rs_ref):
    y_ref[rows, :] = y.astype(BF16)
    z = DN_ALPHA * xres_ref[rows, :] + g_ref[...] * y
    mu = jnp.mean(z, axis=1, keepdims=True)
    zc = z - mu
    var = jnp.mean(zc * zc, axis=1, keepdims=True)
    rstd = lax.rsqrt(var + LN_EPS)
    zhat = zc * rstd
    zh_ref[rows, :] = zhat.astype(zh_ref.dtype)
    xo_ref[rows, :] = zhat * lg_ref[...] + lb_ref[...]
    rs_ref[rows, :] = jnp.broadcast_to(rstd, (zhat.shape[0], rs_ref.shape[1]))


def _ln_out_shapes(t, dm):
    return [jax.ShapeDtypeStruct((t, dm), BF16), jax.ShapeDtypeStruct((t, dm), F32),
            jax.ShapeDtypeStruct((t, dm), F32), jax.ShapeDtypeStruct((t, LANES), F32)]


def _ln_out_specs(tm, dm):
    row = pl.BlockSpec((tm, dm), lambda i: (i, 0))
    return [row, row, row, pl.BlockSpec((tm, LANES), lambda i: (i, 0))]


def proj_ln(a, w, xres, gate, lg, lb, name):
    t, k = a.shape
    dm = w.shape[1]
    tm = _row_tile(t)

    def body(a_ref, w_ref, xres_ref, g_ref, lg_ref, lb_ref, y_ref, xo_ref, zh_ref, rs_ref):
        for rows in _halves(tm):
            y = _dot(a_ref[rows, :], w_ref[...])
            _ln_store(y, rows, xres_ref, g_ref, lg_ref, lb_ref, y_ref, xo_ref, zh_ref, rs_ref)

    vec = pl.BlockSpec((1, dm), lambda i: (0, 0))
    return pl.pallas_call(
        body, name=name, grid=(t // tm,),
        out_shape=_ln_out_shapes(t, dm),
        in_specs=[pl.BlockSpec((tm, k), lambda i: (i, 0)), pl.BlockSpec((k, dm), lambda i: (0, 0)),
                  pl.BlockSpec((tm, dm), lambda i: (i, 0)), vec, vec, vec],
        out_specs=_ln_out_specs(tm, dm),
        compiler_params=_cp("parallel"),
    )(a, w, xres, gate, lg, lb)


def swiglu_proj_ln(ab, w, xres, gate, lg, lb, name):
    t = ab.shape[0]
    f, dm = w.shape
    tm = _row_tile(t)

    def body(a_ref, b_ref, w_ref, xres_ref, g_ref, lg_ref, lb_ref, h_ref, y_ref, xo_ref, zh_ref, rs_ref):
        for rows in _halves(tm):
            y = None
            for c0 in range(0, f, FF_CHUNK):
                cols = slice(c0, c0 + FF_CHUNK)
                a = a_ref[rows, cols].astype(F32)
                h = (a * _sigmoid(a) * b_ref[rows, cols].astype(F32)).astype(BF16)
                h_ref[rows, cols] = h
                part = _dot(h, w_ref[cols, :])
                y = part if y is None else y + part
            _ln_store(y, rows, xres_ref, g_ref, lg_ref, lb_ref, y_ref, xo_ref, zh_ref, rs_ref)

    vec = pl.BlockSpec((1, dm), lambda i: (0, 0))
    return pl.pallas_call(
        body, name=name, grid=(t // tm,),
        out_shape=[jax.ShapeDtypeStruct((t, f), BF16)] + _ln_out_shapes(t, dm),
        in_specs=[pl.BlockSpec((tm, f), lambda i: (i, 0)), pl.BlockSpec((tm, f), lambda i: (i, 1)),
                  pl.BlockSpec((f, dm), lambda i: (0, 0)),
                  pl.BlockSpec((tm, dm), lambda i: (i, 0)), vec, vec, vec],
        out_specs=[pl.BlockSpec((tm, f), lambda i: (i, 0))] + _ln_out_specs(tm, dm),
        compiler_params=_cp("parallel"),
    )(ab, ab, w, xres, gate, lg, lb)


class AttnCfg:
    def __init__(self, dil, heads, kv_heads, qc, kc, vc, max_dist, head0, sinks):
        self.dil, self.heads, self.kv_heads = dil, heads, kv_heads
        self.qc, self.kc, self.vc = qc, kc, vc
        self.max_dist, self.head0, self.sinks = max_dist, head0, sinks
        self.wq = heads * HEAD_DIM
        self.wk = kv_heads * HEAD_DIM
        self.wout = self.wq + 2 * self.wk


ATTN_A = AttnCfg(1, A_Q_HEADS, A_KV_HEADS, 0, A_Q_HEADS * HEAD_DIM, (A_Q_HEADS + A_KV_HEADS) * HEAD_DIM,
                 A_WINDOW - 1, 0, True)


def _attn_b_cfg(g):
    win, dil = B_GROUPS[g]
    cols = ((A_W + g * B_OUT_W, A_W + B_ALL + g * B_OUT_W, A_W + 2 * B_ALL + g * B_OUT_W) if dil == 1
            else (0, B_OUT_W, 2 * B_OUT_W))
    return AttnCfg(dil, B_HEADS_PER_GROUP, B_HEADS_PER_GROUP, *cols, win // dil,
                   A_Q_HEADS + g * B_HEADS_PER_GROUP, False)


ATTN_B = [_attn_b_cfg(g) for g in range(N_GROUPS)]


SCALE = HEAD_DIM ** -0.5


def _head(h):
    return slice(h * HEAD_DIM, (h + 1) * HEAD_DIM)


def _stack(parts):
    return parts[0] if len(parts) == 1 else jnp.concatenate(parts, axis=0)


def _masked_bias(mask, distf, cfg, h, d):
    return jnp.where(mask, distf * (-(_slope(cfg.head0 + h) * d)), NEG_INF)


def _band(i, max_dist):
    qi = lax.broadcasted_iota(jnp.int32, (BLOCK, 2 * BLOCK), 0)
    sj = lax.broadcasted_iota(jnp.int32, (BLOCK, 2 * BLOCK), 1)
    dist = qi + BLOCK - sj
    valid = (dist >= 0) & (dist <= max_dist)
    first_key = jnp.where(i > 0, 0, BLOCK)
    valid_first = valid & (sj >= first_key)
    return dist, valid, valid_first


def _attn_geometry(cfg, n, tq_max=512):
    tq = min(tq_max, n)
    return tq, tq // BLOCK, n // tq


def attn_fwd(qkv, cfg, sinks, name):
    d, n, _ = qkv.shape
    tq, nsub, nqb = _attn_geometry(cfg, n)
    wq, wk = cfg.wq, cfg.wk
    grp = cfg.heads // cfg.kv_heads

    def body(sink_ref, q_ref, kc_ref, kp_ref, vc_ref, vp_ref, o_ref, l_ref, kf, vf):
        i = pl.program_id(1)
        kf[0:BLOCK, :] = kp_ref[...]
        kf[BLOCK:, :] = kc_ref[...]
        vf[0:BLOCK, :] = vp_ref[...]
        vf[BLOCK:, :] = vc_ref[...]
        dist, valid, valid_first = _band(i, cfg.max_dist)
        distf = dist.astype(F32)
        rows = [slice(a * BLOCK, (a + 1) * BLOCK) for a in range(nsub)]
        wins = [slice(a * BLOCK, (a + 2) * BLOCK) for a in range(nsub)]
        for head in range(cfg.heads):
            heads = [head]
            ks = _head(head // grp)
            b_reg = _stack([_masked_bias(valid, distf, cfg, h, d) for h in heads])
            b_first = _stack([_masked_bias(valid_first, distf, cfg, h, d) for h in heads])
            if cfg.sinks:
                sink = _stack([jnp.full((BLOCK, 1), sink_ref[h], F32) for h in heads])
            ss = [_dot_nt(_stack([q_ref[rows[a], _head(h)] for h in heads]) * SCALE, kf[wins[a], ks])
                  + (b_first if a == 0 else b_reg) for a in range(nsub)]
            es, invs = [], []
            for a in range(nsub):
                m = jnp.max(ss[a], axis=1, keepdims=True)
                if cfg.sinks:
                    m = jnp.maximum(m, sink)
                e = jnp.exp(ss[a] - m)
                den = jnp.sum(e, axis=1, keepdims=True)
                if cfg.sinks:
                    den = den + jnp.exp(sink - m)
                es.append(e.astype(BF16))
                invs.append(1.0 / den)
                lse = m + jnp.log(den)
                for g, h in enumerate(heads):
                    l_ref[rows[a], _head(h)] = jnp.broadcast_to(lse[g * BLOCK:(g + 1) * BLOCK], (BLOCK, HEAD_DIM))
            for a in range(nsub):
                o = _dot(es[a], vf[wins[a], ks]) * invs[a]
                for g, h in enumerate(heads):
                    o_ref[rows[a], _head(h)] = o[g * BLOCK:(g + 1) * BLOCK].astype(BF16)

    prev = lambda i: jnp.maximum(i * nsub - 1, 0)
    cur = lambda w, c: pl.BlockSpec((None, tq, w), lambda r, i: (r, i, c // w))
    prv = lambda w, c: pl.BlockSpec((None, BLOCK, w), lambda r, i: (r, prev(i), c // w))
    out = pl.BlockSpec((None, tq, wq), lambda r, i: (r, i, 0))
    return pl.pallas_call(
        body, name=name, grid=(d, nqb),
        out_shape=[jax.ShapeDtypeStruct((d, n, wq), BF16), jax.ShapeDtypeStruct((d, n, wq), F32)],
        in_specs=[pl.BlockSpec(memory_space=pltpu.SMEM),
                  cur(wq, cfg.qc), cur(wk, cfg.kc), prv(wk, cfg.kc), cur(wk, cfg.vc), prv(wk, cfg.vc)],
        out_specs=[out, out],
        scratch_shapes=[pltpu.VMEM((tq + BLOCK, wk), BF16), pltpu.VMEM((tq + BLOCK, wk), BF16)],
        compiler_params=_cp("parallel", "parallel"),
    )(sinks, qkv, qkv, qkv, qkv, qkv)


def mix_merge(ya, o_g, l_g, proj, w_a, w_b, name):
    t = ya.shape[0]
    dm = w_a.shape[1]
    tm = _row_tile(t)
    gcol = GATE_COL // dm
    dils = [o.shape[0] for o in o_g]

    def body(ya_ref, o0, o1, o2, l0, l1, l2, ga_ref, gb_ref, wa_ref, wb_ref, yb_ref, mg_ref, *scrs):
        ls = [_unfold_from(l, scrs, d) for l, d in zip((l0, l1, l2), dils)]
        m = jnp.maximum(jnp.maximum(ls[0], ls[1]), ls[2])
        es = [jnp.exp(l - m) for l in ls]
        inv = 1.0 / (es[0] + es[1] + es[2])
        yb = sum(_unfold_from(o, scrs, d) * (e * inv) for o, e, d in zip((o0, o1, o2), es, dils)).astype(BF16)
        yb_ref[...] = yb
        pa = _dot(ya_ref[...], wa_ref[...])
        pb = _dot(yb, wb_ref[...])
        mg = _sigmoid(ga_ref[...].astype(F32)) * pa + _sigmoid(gb_ref[...].astype(F32)) * pb
        mg_ref[...] = mg.astype(BF16)

    wide = lambda w: pl.BlockSpec((tm, w), lambda i: (i, 0))
    folded = [_folded_spec(d, tm, B_OUT_W) for d in dils]
    return pl.pallas_call(
        body, name=name, grid=(t // tm,),
        out_shape=[jax.ShapeDtypeStruct((t, B_OUT_W), BF16), jax.ShapeDtypeStruct((t, dm), BF16)],
        in_specs=[wide(ya.shape[1])] + folded + folded
                 + [pl.BlockSpec((tm, dm), lambda i: (i, gcol)), pl.BlockSpec((tm, dm), lambda i: (i, gcol + 1)),
                    pl.BlockSpec(w_a.shape, lambda i: (0, 0)), pl.BlockSpec(w_b.shape, lambda i: (0, 0))],
        out_specs=[wide(B_OUT_W), wide(dm)],
        scratch_shapes=_fold_scratch(tm, B_OUT_W),
        compiler_params=_cp("parallel"),
    )(ya, *o_g, *l_g, proj, proj, w_a, w_b)


def loss_head(y, target):
    t, dm = y.shape
    tm = _row_tile(t)

    def body(y_ref, t_ref, dy_ref, loss_ref):
        @pl.when(pl.program_id(0) == 0)
        def _():
            loss_ref[...] = jnp.zeros_like(loss_ref)
        err = y_ref[...] - t_ref[...]
        dy_ref[...] = err * (1.0 / dm)
        per_row = jnp.sum(err * err, axis=1, keepdims=True) * (1.0 / dm)
        loss_ref[...] += 0.5 * jnp.sum(per_row, axis=0, keepdims=True)

    row = pl.BlockSpec((tm, dm), lambda i: (i, 0))
    return pl.pallas_call(
        body, name="loss_head", grid=(t // tm,),
        out_shape=[jax.ShapeDtypeStruct((t, dm), F32), jax.ShapeDtypeStruct((8, LANES), F32)],
        in_specs=[row, row],
        out_specs=[row, pl.BlockSpec((8, LANES), lambda i: (0, 0))],
        compiler_params=_cp("arbitrary"),
    )(y, target)


def _fold_rows(v):
    tm, c = v.shape
    return jnp.sum(v.reshape(tm // 8, 8, c), axis=0)


def _finish_sums(refs, nsteps):
    @pl.when(pl.program_id(0) == nsteps - 1)
    def _():
        for r in refs:
            r[...] = jnp.broadcast_to(jnp.sum(r[...], axis=0, keepdims=True), r.shape)


def ln_bwd(dxo, zhat, rstd, ysub, lg, gate, act, name):
    t, dm = dxo.shape
    k = act.shape[1]
    tm = _row_tile(t)
    nsteps = t // tm
    ch = 256

    def body(dxo_ref, zh_ref, rs_ref, y_ref, lg_ref, g_ref, a_ref, dz_ref, dy_ref, sg_ref, sb_ref, sgate_ref,
             gw_ref, acc):
        @pl.when(pl.program_id(0) == 0)
        def _():
            for r in (sg_ref, sb_ref, sgate_ref, acc):
                r[...] = jnp.zeros_like(r)
        for rows in _halves(tm):
            dxo_v = dxo_ref[rows, :]
            zh = zh_ref[rows, :]
            dxh = dxo_v * lg_ref[...]
            m1 = jnp.mean(dxh, axis=1, keepdims=True)
            m2 = jnp.mean(dxh * zh, axis=1, keepdims=True)
            dz = rs_ref[rows, 0:1] * (dxh - m1 - zh * m2)
            dz_ref[rows, :] = dz
            dy = (g_ref[...] * dz).astype(BF16)
            dy_ref[rows, :] = dy
            sg_ref[...] += _fold_rows(dxo_v * zh)
            sb_ref[...] += _fold_rows(dxo_v)
            sgate_ref[...] += _fold_rows(dz * y_ref[rows, :].astype(F32))
            a = a_ref[rows, :]
            for c0 in range(0, dm, ch):
                acc[:, c0:c0 + ch] += _dot_tn(a, dy[:, c0:c0 + ch])
        _finish_sums((sg_ref, sb_ref, sgate_ref), nsteps)

        @pl.when(pl.program_id(0) == nsteps - 1)
        def _():
            gw_ref[...] = acc[...].astype(BF16)

    row = pl.BlockSpec((tm, dm), lambda i: (i, 0))
    vec = pl.BlockSpec((1, dm), lambda i: (0, 0))
    sums = pl.BlockSpec((8, dm), lambda i: (0, 0))
    return pl.pallas_call(
        body, name=name, grid=(nsteps,),
        out_shape=[jax.ShapeDtypeStruct((t, dm), F32), jax.ShapeDtypeStruct((t, dm), BF16)]
                  + [jax.ShapeDtypeStruct((8, dm), F32)] * 3 + [jax.ShapeDtypeStruct((k, dm), BF16)],
        in_specs=[row, row, pl.BlockSpec((tm, LANES), lambda i: (i, 0)), row, vec, vec,
                  pl.BlockSpec((tm, k), lambda i: (i, 0))],
        out_specs=[row, row, sums, sums, sums, pl.BlockSpec((k, dm), lambda i: (0, 0))],
        scratch_shapes=[pltpu.VMEM((k, dm), F32)],
        compiler_params=_cp("arbitrary"),
    )(dxo, zhat, rstd, ysub, lg, gate, act)


def _mod_bwd_store(du_of, dz_ref, x_ref, s_ref, dx_ref, ss_ref, ssh_ref, nsteps):
    @pl.when(pl.program_id(0) == 0)
    def _():
        ss_ref[...] = jnp.zeros_like(ss_ref)
        ssh_ref[...] = jnp.zeros_like(ssh_ref)
    for c0 in range(0, dx_ref.shape[1], DGRAD_CHUNK):
        cols = slice(c0, c0 + DGRAD_CHUNK)
        du = du_of(cols)
        dx_ref[:, cols] = DN_ALPHA * dz_ref[:, cols] + du * (1.0 + s_ref[:, cols])
        ss_ref[:, cols] += _fold_rows(du * x_ref[:, cols])
        ssh_ref[:, cols] += _fold_rows(du)
    _finish_sums((ss_ref, ssh_ref), nsteps)


def dgrad_ffn(g, wt, dz, xin, s, name, ex=None):
    t, dm = dz.shape
    k = g.shape[1]
    tm = _row_tile(t)
    nsteps = t // tm

    def body(*refs):
        i = pl.program_id(0)
        g_ref, w_ref, dz_ref, x_ref, s_ref, dx_ref, ss_ref, ssh_ref = _hosted(ex, refs, 5, 3, i == 0, i == nsteps - 1)
        g_v = g_ref[...]
        _mod_bwd_store(lambda cols: _dot(g_v, w_ref[:, cols]), dz_ref, x_ref, s_ref, dx_ref, ss_ref, ssh_ref, nsteps)

    row = pl.BlockSpec((tm, dm), lambda i: (i, 0))
    acc = pl.BlockSpec((8, dm), lambda i: (0, 0))
    return _host_call(
        body, ex, name=name, grid=(nsteps,),
        out_shape=[jax.ShapeDtypeStruct((t, dm), F32)] + [jax.ShapeDtypeStruct((8, dm), F32)] * 2,
        in_specs=[pl.BlockSpec((tm, k), lambda i: (i, 0)), pl.BlockSpec((k, dm), lambda i: (0, 0)),
                  row, row, pl.BlockSpec((1, dm), lambda i: (0, 0))],
        out_specs=[row, acc, acc], sem=("arbitrary",), args=[g, wt, dz, xin, s])


def dswiglu(dy, wd, ab, name):
    t, dm = dy.shape
    f = wd.shape[0]
    tm = _row_tile(t)

    def body(dy_ref, w_ref, a_ref, b_ref, o_ref):
        dy_v = dy_ref[...]
        for c0 in range(0, f, FF_CHUNK):
            cols = slice(c0, c0 + FF_CHUNK)
            dh = _dot_nt(dy_v, w_ref[cols, :])
            a = a_ref[:, cols].astype(F32)
            sg = _sigmoid(a)
            o_ref[:, cols] = (dh * b_ref[:, cols].astype(F32) * (sg * (1.0 + a * (1.0 - sg)))).astype(BF16)
            o_ref[:, f + c0:f + c0 + FF_CHUNK] = (dh * (a * sg)).astype(BF16)

    return pl.pallas_call(
        body, name=name, grid=(t // tm,),
        out_shape=jax.ShapeDtypeStruct((t, 2 * f), BF16),
        in_specs=[pl.BlockSpec((tm, dm), lambda i: (i, 0)), pl.BlockSpec((f, dm), lambda i: (0, 0)),
                  pl.BlockSpec((tm, f), lambda i: (i, 0)), pl.BlockSpec((tm, f), lambda i: (i, 1))],
        out_specs=pl.BlockSpec((tm, 2 * f), lambda i: (i, 0)),
        compiler_params=_cp("parallel"),
    )(dy, wd, ab, ab)


def dgrad_in(d_a, d_b, dgab, wt, dz, xin, s, name, ex=None):
    t, dm = dz.shape
    tm = _row_tile(t)
    nsteps = t // tm
    dils = [a.shape[0] for a in d_b]

    def body(*refs):
        i = pl.program_id(0)
        (da_ref, b0, b1, b2, dg_ref, w_ref, dz_ref, x_ref, s_ref, dx_ref, ss_ref, ssh_ref,
         *scrs) = _hosted(ex, refs, 9, 3, i == 0, i == nsteps - 1)
        vs = [b_ref[...].reshape(tm, B_GW) for b_ref in (b0, b1, b2)]

        def du_of(cols):
            du = _dot(da_ref[0], w_ref[0:A_W, cols])
            for g, (v, d) in enumerate(zip(vs, dils)):
                part = None
                for p in range(3):
                    r0 = A_W + p * B_ALL + g * B_OUT_W
                    term = _dot(v[:, p * B_OUT_W:(p + 1) * B_OUT_W], w_ref[r0:r0 + B_OUT_W, cols])
                    part = term if part is None else part + term
                if d == 1:
                    du = du + part
                else:
                    n = tm // d
                    du = du + _unfold_rows(lambda r, cs: part[r * n:(r + 1) * n, cs], scrs, d, n, DGRAD_CHUNK)
            for j in range(2):
                du = du + _dot(dg_ref[:, j * dm:(j + 1) * dm], w_ref[GATE_COL + j * dm:GATE_COL + (j + 1) * dm, cols])
            return du

        _mod_bwd_store(du_of, dz_ref, x_ref, s_ref, dx_ref, ss_ref, ssh_ref, nsteps)

    row = pl.BlockSpec((tm, dm), lambda i: (i, 0))
    acc = pl.BlockSpec((8, dm), lambda i: (0, 0))
    return _host_call(
        body, ex, name=name, grid=(nsteps,),
        out_shape=[jax.ShapeDtypeStruct((t, dm), F32)] + [jax.ShapeDtypeStruct((8, dm), F32)] * 2,
        in_specs=[_folded_spec(1, tm, A_W)] + [_folded_spec(d, tm, B_GW) for d in dils]
                 + [pl.BlockSpec((tm, 2 * dm), lambda i: (i, 0)), pl.BlockSpec(wt.shape, lambda i: (0, 0)),
                    row, row, pl.BlockSpec((1, dm), lambda i: (0, 0))],
        out_specs=[row, acc, acc],
        scratch_shapes=_fold_scratch(tm, DGRAD_CHUNK), sem=("arbitrary",), args=[d_a, *d_b, dgab, wt, dz, xin, s])


def wgrad(a, b, buf, tn, nj, b0, o0, om, name):
    t, k = a.shape
    tt = ROW_TILE
    while tt * 2 * k <= WGRAD_TILE_ELEMS and tt * 2 <= t:
        tt *= 2
    nsteps = t // tt
    last = nsteps - 1

    def body(a_ref, b_ref, buf_ref, o_ref, acc):
        s, j = pl.program_id(0), pl.program_id(1)

        @pl.when(s == 0)
        def _():
            acc[j] = jnp.zeros(acc.shape[1:], F32)
        acc[j] += _dot_tn(a_ref[...], b_ref[...])

        @pl.when(s == last)
        def _():
            o_ref[...] = acc[j].astype(BF16)

    return pl.pallas_call(
        body, name=name, grid=(nsteps, nj),
        out_shape=jax.ShapeDtypeStruct(buf.shape, buf.dtype),
        in_specs=[pl.BlockSpec((tt, k), lambda s, j: (s, 0)),
                  pl.BlockSpec((tt, tn), lambda s, j: (s, b0 + j)),
                  pl.BlockSpec(memory_space=pl.ANY)],
        out_specs=pl.BlockSpec((k, tn), lambda s, j: (0, o0 + om * jnp.where(s == last, j, 0))),
        scratch_shapes=[pltpu.VMEM((nj, k, tn), F32)],
        input_output_aliases={2: 0},
        compiler_params=_cp("arbitrary", "arbitrary"),
    )(a, b, buf)


def dmerge(do, wot, ya, yb, w_a, w_b, wat, wbt, proj, name):
    t, dm = do.shape
    tm = _row_tile(t)
    nsteps = t // tm
    gcol = GATE_COL // dm
    ch = 256

    def body(do_ref, wot_ref, ya_ref, yb_ref, wa_ref, wb_ref, wat_ref, wbt_ref, g_ref,
             dya_ref, dyb_ref, dg_ref, gwa_ref, gwb_ref, dm_scr, acc_a, acc_b):
        i, j = pl.program_id(0), pl.program_id(1)

        @pl.when((i == 0) & (j == 0))
        def _():
            acc_a[...] = jnp.zeros_like(acc_a)
            acc_b[...] = jnp.zeros_like(acc_b)

        @pl.when(j == 0)
        def _():
            do_v = do_ref[...]
            for c0 in range(0, dm, ch):
                dm_scr[:, c0:c0 + ch] = _dot(do_v, wot_ref[:, c0:c0 + ch])

        def branch(y_ref, w_ref, wt_ref, dy_ref, acc, gw_ref):
            y = y_ref[...]
            dy = None
            for c0 in range(0, dm, ch):
                cols = slice(c0, c0 + ch)
                p = _dot(y, w_ref[:, cols])
                sg = _sigmoid(g_ref[:, cols].astype(F32))
                dmg = dm_scr[:, cols]
                dp = (dmg * sg).astype(BF16)
                dg_ref[:, cols] = (dmg * p * (sg * (1.0 - sg))).astype(BF16)
                acc[:, cols] += _dot_tn(y, dp)
                part = _dot(dp, wt_ref[cols, :])
                dy = part if dy is None else dy + part
            dy_ref[...] = dy.astype(dy_ref.dtype)

            @pl.when(i == nsteps - 1)
            def _():
                gw_ref[...] = acc[...].astype(BF16)

        pl.when(j == 0)(lambda: branch(ya_ref, wa_ref, wat_ref, dya_ref, acc_a, gwa_ref))
        pl.when(j == 1)(lambda: branch(yb_ref, wb_ref, wbt_ref, dyb_ref, acc_b, gwb_ref))

    full = lambda arr: pl.BlockSpec(arr.shape, lambda i, j: (0, 0))
    rowc = lambda w: pl.BlockSpec((tm, w), lambda i, j: (i, 0))
    return pl.pallas_call(
        body, name=name, grid=(nsteps, 2),
        out_shape=[jax.ShapeDtypeStruct((t, ya.shape[1]), BF16), jax.ShapeDtypeStruct((t, yb.shape[1]), F32),
                   jax.ShapeDtypeStruct((t, 2 * dm), BF16),
                   jax.ShapeDtypeStruct(w_a.shape, BF16), jax.ShapeDtypeStruct(w_b.shape, BF16)],
        in_specs=[rowc(dm), full(wot), rowc(ya.shape[1]), rowc(yb.shape[1]), full(w_a), full(w_b),
                  full(wat), full(wbt), pl.BlockSpec((tm, dm), lambda i, j: (i, gcol + j))],
        out_specs=[rowc(ya.shape[1]), rowc(yb.shape[1]), pl.BlockSpec((tm, dm), lambda i, j: (i, j)),
                   full(w_a), full(w_b)],
        scratch_shapes=[pltpu.VMEM((tm, dm), F32), pltpu.VMEM(w_a.shape, F32), pltpu.VMEM(w_b.shape, F32)],
        compiler_params=_cp("arbitrary", "arbitrary"),
    )(do, wot, ya, yb, w_a, w_b, wat, wbt, proj)


def mix_bwd(dyb, o_g, l_g, name):
    t, w = dyb.shape
    tm = _row_tile(t)
    nh = w // HEAD_DIM
    dils = [o.shape[0] for o in o_g]

    def body(dyb_ref, o0, o1, o2, l0, l1, l2, do0, do1, do2, dl0, dl1, dl2, *scr):
        ls = [_unfold_from(l, scr, d) for l, d in zip((l0, l1, l2), dils)]
        m = jnp.maximum(jnp.maximum(ls[0], ls[1]), ls[2])
        es = [jnp.exp(l - m) for l in ls]
        inv = 1.0 / (es[0] + es[1] + es[2])
        wts = [e * inv for e in es]
        dyb_v = dyb_ref[...]
        dws = []
        for o_ref, do_ref, wt, d in zip((o0, o1, o2), (do0, do1, do2), wts, dils):
            prod = dyb_v * _unfold_from(o_ref, scr, d)
            _fold_to(do_ref, dyb_v * wt, scr, d)
            for h in range(nh):
                hs = slice(h * HEAD_DIM, (h + 1) * HEAD_DIM)
                dws.append(jnp.broadcast_to(jnp.sum(prod[:, hs], axis=1, keepdims=True), (tm, HEAD_DIM)))
        for g, (dl_ref, d) in enumerate(zip((dl0, dl1, dl2), dils)):
            cols = []
            for h in range(nh):
                hs = slice(h * HEAD_DIM, (h + 1) * HEAD_DIM)
                mean = sum(wts[g2][:, hs] * dws[g2 * nh + h] for g2 in range(N_GROUPS))
                cols.append(wts[g][:, hs] * (dws[g * nh + h] - mean))
            _fold_to(dl_ref, jnp.concatenate(cols, axis=1), scr, d)

    folded = [_folded_spec(d, tm, w) for d in dils]
    return pl.pallas_call(
        body, name=name, grid=(t // tm,),
        out_shape=[jax.ShapeDtypeStruct(o.shape, BF16) for o in o_g]
                  + [jax.ShapeDtypeStruct(o.shape, F32) for o in o_g],
        in_specs=[pl.BlockSpec((tm, w), lambda i: (i, 0))] + folded + folded,
        out_specs=folded + folded,
        scratch_shapes=_fold_scratch(tm, w),
        compiler_params=_cp("parallel"),
    )(dyb, *o_g, *l_g)


def attn_bwd(qkv, o, lse, do, dlse, cfg, sinks, name):
    d, n, _ = qkv.shape
    tq, nsub, nqb = _attn_geometry(cfg, n, 1024)
    wq, wk, wout = cfg.wq, cfg.wk, cfg.wout
    grp = cfg.heads // cfg.kv_heads
    has_dl = dlse is not None

    def body(*refs):
        sink_ref, q_ref, qn_ref, kc_ref, kp_ref, vc_ref, vp_ref = refs[:7]
        o_ref, on_ref, do_ref, don_ref, l_ref, ln_ref = refs[7:13]
        rest = refs[13:]
        dl_ref = dln_ref = None
        if has_dl:
            dl_ref, dln_ref = rest[:2]
            rest = rest[2:]
        out_ref = rest[0]
        rest = rest[1:]
        if cfg.sinks:
            dsink_ref = rest[0]
            rest = rest[1:]
        kf, vf, dk_acc, dv_acc = rest
        r, i = pl.program_id(0), pl.program_id(1)
        kf[0:BLOCK, :] = kp_ref[...]
        kf[BLOCK:, :] = kc_ref[...]
        vf[0:BLOCK, :] = vp_ref[...]
        vf[BLOCK:, :] = vc_ref[...]
        dist, valid, valid_first = _band(i, cfg.max_dist)
        distf = dist.astype(F32)
        next_dist = jnp.where(i < nqb - 1, cfg.max_dist, -1)
        valid_next = (dist[:, 0:BLOCK] >= 0) & (dist[:, 0:BLOCK] <= next_dist)
        if cfg.sinks:
            @pl.when((r == 0) & (i == 0))
            def _():
                dsink_ref[...] = jnp.zeros_like(dsink_ref)

        def stacked(ref, rows, heads):
            return _stack([ref[rows, _head(h)] for h in heads])

        def per_row(ref, rows, heads):
            return _stack([jnp.max(ref[rows, _head(h)], axis=1, keepdims=True) for h in heads])

        for kv in range(cfg.kv_heads):
            heads = [kv * grp + g for g in range(grp)]
            ks = slice(kv * HEAD_DIM, (kv + 1) * HEAD_DIM)
            dk_acc[...] = jnp.zeros_like(dk_acc)
            dv_acc[...] = jnp.zeros_like(dv_acc)
            biases = [_stack([_masked_bias(m, dd, cfg, h, d) for h in heads])
                      for m, dd in ((valid_first, distf), (valid, distf), (valid_next, distf[:, 0:BLOCK]))]
            tiles = []
            for a in range(nsub + 1):
                if a < nsub:
                    rows, win = slice(a * BLOCK, (a + 1) * BLOCK), slice(a * BLOCK, (a + 2) * BLOCK)
                    src = (q_ref, o_ref, do_ref, l_ref, dl_ref)
                else:
                    rows, win = slice(0, BLOCK), slice(nsub * BLOCK, (nsub + 1) * BLOCK)
                    src = (qn_ref, on_ref, don_ref, ln_ref, dln_ref)
                qs = stacked(src[0], rows, heads) * SCALE
                do_v = stacked(src[2], rows, heads)
                delta = jnp.sum(do_v.astype(F32) * stacked(src[1], rows, heads).astype(F32), axis=1, keepdims=True)
                lse_v = per_row(src[3], rows, heads)
                shift = (per_row(src[4], rows, heads) - delta) if has_dl else -delta
                k = kf[win, ks]
                s = _dot_nt(qs, k) + biases[0 if a == 0 else (1 if a < nsub else 2)]
                dp = _dot_nt(do_v, vf[win, ks])
                tiles.append((qs, do_v, k, delta, lse_v, shift, s, dp))
            grads = []
            for qs, do_v, k, delta, lse_v, shift, s, dp in tiles:
                p = jnp.exp(s - lse_v)
                grads.append(((p * (dp + shift)).astype(BF16), p.astype(BF16)))
            for a, ((qs, do_v, k, delta, lse_v, shift, _, _), (dsb, pb)) in enumerate(zip(tiles, grads)):
                if a < nsub:
                    dq = _dot(dsb, k) * SCALE
                    for g, h in enumerate(heads):
                        out_ref[a * BLOCK:(a + 1) * BLOCK, _head(h)] = dq[g * BLOCK:(g + 1) * BLOCK].astype(BF16)
                if a == 0:
                    kcols = slice(0, BLOCK)
                    dsb, pb = dsb[:, BLOCK:], pb[:, BLOCK:]
                elif a < nsub:
                    kcols = slice((a - 1) * BLOCK, (a + 1) * BLOCK)
                else:
                    kcols = slice((nsub - 1) * BLOCK, nsub * BLOCK)
                dk_acc[:, kcols] += _dot_tn(qs, dsb)
                dv_acc[:, kcols] += _dot_tn(do_v, pb)
                if cfg.sinks and a < nsub:
                    for g, h in enumerate(heads):
                        part = slice(g * BLOCK, (g + 1) * BLOCK)
                        psink = jnp.exp(sink_ref[h] - lse_v[part])
                        tot = jnp.sum(psink * (-delta[part]), axis=0, keepdims=True)
                        dsink_ref[h:h + 1, :] += jnp.broadcast_to(tot, (1, LANES))
            out_ref[:, wq + kv * HEAD_DIM: wq + (kv + 1) * HEAD_DIM] = dk_acc[...].T.astype(BF16)
            out_ref[:, wq + wk + kv * HEAD_DIM: wq + wk + (kv + 1) * HEAD_DIM] = dv_acc[...].T.astype(BF16)

    prev = lambda i: jnp.maximum(i * nsub - 1, 0)
    nxt = lambda i: jnp.minimum((i + 1) * nsub, n // BLOCK - 1)
    cur = lambda w, c: pl.BlockSpec((None, tq, w), lambda r, i: (r, i, c // w))
    prv = lambda w, c: pl.BlockSpec((None, BLOCK, w), lambda r, i: (r, prev(i), c // w))
    o_cur = pl.BlockSpec((None, tq, wq), lambda r, i: (r, i, 0))
    o_nxt = pl.BlockSpec((None, BLOCK, wq), lambda r, i: (r, nxt(i), 0))
    in_specs = [pl.BlockSpec(memory_space=pltpu.SMEM),
                cur(wq, cfg.qc), pl.BlockSpec((None, BLOCK, wq), lambda r, i: (r, nxt(i), cfg.qc // wq)),
                cur(wk, cfg.kc), prv(wk, cfg.kc), cur(wk, cfg.vc), prv(wk, cfg.vc),
                o_cur, o_nxt, o_cur, o_nxt, o_cur, o_nxt]
    args = [sinks, qkv, qkv, qkv, qkv, qkv, qkv, o, o, do, do, lse, lse]
    if has_dl:
        in_specs += [o_cur, o_nxt]
        args += [dlse, dlse]
    out_shape = [jax.ShapeDtypeStruct((d, n, wout), BF16)]
    out_specs = [pl.BlockSpec((None, tq, wout), lambda r, i: (r, i, 0))]
    if cfg.sinks:
        out_shape.append(jax.ShapeDtypeStruct((8, LANES), F32))
        out_specs.append(pl.BlockSpec((8, LANES), lambda r, i: (0, 0)))
    return pl.pallas_call(
        body, name=name, grid=(d, nqb), out_shape=out_shape, in_specs=in_specs, out_specs=out_specs,
        scratch_shapes=[pltpu.VMEM((tq + BLOCK, wk), BF16), pltpu.VMEM((tq + BLOCK, wk), BF16),
                        pltpu.VMEM((HEAD_DIM, tq), F32), pltpu.VMEM((HEAD_DIM, tq), F32)],
        compiler_params=_cp("arbitrary", "arbitrary"),
    )(*args)


def _adamw(g, w, m, v):
    m = ADAM_B1 * m + (1.0 - ADAM_B1) * g
    v = ADAM_B2 * v + (1.0 - ADAM_B2) * (g * g)
    m_hat = m / (1.0 - ADAM_B1 ** ADAM_STEP)
    v_hat = v / (1.0 - ADAM_B2 ** ADAM_STEP)
    delta = -ADAM_LR * (m_hat / (jnp.sqrt(v_hat) + ADAM_EPS) + ADAM_WD * w)
    return delta, m, v


def adam_reduce(parts, w, m, v, name):
    r, c = w.shape
    tr = next(cand for cand in (256, 128, 64, 32, 16, 8) if r % cand == 0) if r > 256 else r

    def body(p_ref, w_ref, m_ref, v_ref, g_ref, d_ref, mo_ref, vo_ref):
        g = p_ref[0].astype(F32)
        for j in range(1, N_DEV):
            g = g + p_ref[j].astype(F32)
        g_ref[...] = g
        d_ref[...], mo_ref[...], vo_ref[...] = _adamw(g, w_ref[...], m_ref[...], v_ref[...])

    row = pl.BlockSpec((tr, c), lambda i: (i, 0))
    return pl.pallas_call(
        body, name=name, grid=(r // tr,),
        out_shape=[jax.ShapeDtypeStruct((r, c), F32)] * 4,
        in_specs=[pl.BlockSpec((N_DEV, tr, c), lambda i: (0, i, 0)), row, row, row],
        out_specs=[row] * 4,
        compiler_params=_cp("parallel"),
    )(parts, w, m, v)


def adam_layers(parts, w, m, v, name):
    nl, r, c = w.shape
    tr = next(cand for cand in (256, 128, 64, 32, 16, 8) if r % cand == 0)
    steps = r // tr

    def body(*refs):
        p_refs = refs[:nl]
        w_ref, m_ref, v_ref, g_ref, d_ref, mo_ref, vo_ref = refs[nl:]
        for k in range(nl):
            @pl.when(pl.program_id(0) == k)
            def _():
                g = p_refs[k][0].astype(F32)
                for j in range(1, N_DEV):
                    g = g + p_refs[k][j].astype(F32)
                g_ref[...] = g
                d_ref[...], mo_ref[...], vo_ref[...] = _adamw(g, w_ref[...], m_ref[...], v_ref[...])

    def part_spec(k):
        return pl.BlockSpec((N_DEV, tr, c), lambda l, i: (0, jnp.clip(i + (l - k) * steps, 0, steps - 1), 0))

    blk = pl.BlockSpec((None, tr, c), lambda l, i: (l, i, 0))
    return pl.pallas_call(
        body, name=name, grid=(nl, steps),
        out_shape=[jax.ShapeDtypeStruct((nl, r, c), F32)] * 4,
        in_specs=[part_spec(k) for k in range(nl)] + [blk, blk, blk],
        out_specs=[blk] * 4,
        compiler_params=_cp("arbitrary", "arbitrary"),
    )(*parts, w, m, v)


def adam_w_ada(sct, dm_loc, w, m, v):
    nl, dm, wc = w.shape
    tr = 512

    def body(s_ref, d_ref, w_ref, m_ref, v_ref, g_ref, dl_ref, mo_ref, vo_ref):
        g = jnp.dot(s_ref[...], d_ref[...], preferred_element_type=F32, precision=lax.Precision.HIGHEST)
        g_ref[...] = g
        dl_ref[...], mo_ref[...], vo_ref[...] = _adamw(g, w_ref[...], m_ref[...], v_ref[...])

    blk = pl.BlockSpec((None, tr, wc), lambda l, i: (l, i, 0))
    return pl.pallas_call(
        body, name="adam_w_ada", grid=(nl, dm // tr),
        out_shape=[jax.ShapeDtypeStruct(w.shape, F32)] * 4,
        in_specs=[pl.BlockSpec((tr, LANES), lambda l, i: (i, 0)),
                  pl.BlockSpec((None, LANES, wc), lambda l, i: (l, 0, 0)), blk, blk, blk],
        out_specs=[blk] * 4,
        compiler_params=_cp("parallel", "parallel"),
    )(sct, dm_loc, w, m, v)


TRANSPOSED = ("w_gate", "w_up")


def _pieces(dm):
    ncol = lambda n: n // N_DEV
    mixer = ([Piece("w_in", "w_in", 1, 0, ncol(GATE_COL + 2 * dm)),
              Piece("w_a", "w_a", 1, 0, ncol(dm)),
              Piece("w_b", "w_b", 1, 0, ncol(dm)),
              Piece("w_o", "w_o", 0, 0, ncol(dm))],
             {"w_in": (dm, GATE_COL + 2 * dm), "w_a": (A_Q_HEADS * HEAD_DIM, dm), "w_b": (B_OUT_W, dm),
              "w_o": (dm, dm)})
    ffn = ([Piece("w_gate", "w_ffn_t", 0, 0, ncol(D_FF)),
            Piece("w_up", "w_ffn_t", 0, D_FF, ncol(D_FF)),
            Piece("w_down", "w_down", 0, 0, ncol(D_FF))],
           {"w_ffn_t": (2 * D_FF, dm), "w_down": (D_FF, dm)})
    return mixer, ffn


def kernel(x, c, w_ada, b_ada, w_in, sinks, w_a, w_b, w_o, ln1_g, ln1_b, w_gate, w_up, w_down, ln2_g, ln2_b, loss_target, m_w_ada, m_b_ada, m_w_in, m_sinks, m_w_a, m_w_b, m_w_o, m_ln1_g, m_ln1_b, m_w_gate, m_w_up, m_w_down, m_ln2_g, m_ln2_b, v_w_ada, v_b_ada, v_w_in, v_sinks, v_w_a, v_w_b, v_w_o, v_ln1_g, v_ln1_b, v_w_gate, v_w_up, v_w_down, v_ln2_g, v_ln2_b):
    given = dict(locals())
    nl = w_in.shape[0]
    t, dm = x.shape[1], x.shape[2]
    me = 4 * lax.axis_index("x") + 2 * lax.axis_index("y") + lax.axis_index("c")
    x0 = x.reshape(t, dm)
    target = loss_target.reshape(t, dm)

    groups = dict(zip(("mixer", "ffn"), _pieces(dm)))
    local = lambda nm, pre="": (given[pre + nm].transpose(0, 2, 1) if nm in TRANSPOSED else given[pre + nm])
    shards = {pc.name: local(pc.name).astype(BF16) for pcs, _ in groups.values() for pc in pcs}

    def gather(group, l):
        pcs, bufs = groups[group]
        return Exchange("gather", pcs, [shards[pc.name][l] for pc in pcs], bufs.values(), bufs)

    def scatter(group, gbuf):
        pcs, bufs = groups[group]
        return Exchange("scatter", pcs, [gbuf[nm] for nm in bufs],
                        [(N_DEV,) + shards[pc.name].shape[1:] for pc in pcs], bufs)

    full = [dict() for _ in range(nl)]
    full[0].update(zip(groups["mixer"][1], run_exchange(gather("mixer", 0), "gather_mixer")))

    wc = w_ada.shape[2]
    c_all = all_gather_small(jnp.broadcast_to(c, (8, dm)), "gather_c")[:, 0, :]
    b_loc = lax.dynamic_slice_in_dim(b_ada, me * wc, wc, axis=1).reshape(nl, 1, wc)
    mp, sc_all = mod_partial(c_all, w_ada, b_loc)
    mp_all = all_gather_small(mp.reshape(nl * N_DEV, wc), "gather_mod").reshape(N_DEV, nl, N_DEV, wc)
    mod = lax.dynamic_index_in_dim(mp_all, me, axis=2, keepdims=False)
    mod = mod.transpose(1, 0, 2).reshape(nl, 6, 1, dm)

    vec = lambda a, l: a[l].reshape(1, dm)

    saved = []
    xl = x0
    for l in range(nl):
        sh1, s1, g1, sh2, s2, g2 = [mod[l, j] for j in range(6)]
        w = full[l]
        (u1, u1_f4, u1_f16, proj, qkv_f4, qkv_f16), got = in_proj(xl, s1, sh1, w["w_in"], "in_proj",
                                                                   gather("ffn", l))
        w.update(zip(groups["ffn"][1], got))
        proj3 = proj.reshape(1, t, proj.shape[1])
        qkv_b = [proj3, qkv_f4, qkv_f16]
        ya, lse_a = attn_fwd(proj3, ATTN_A, sinks[l], "attn_a_fwd")
        o_g, l_g = [], []
        for g, cfg in enumerate(ATTN_B):
            o, ls = attn_fwd(qkv_b[g], cfg, sinks[l], "attn_b%d_fwd" % g)
            o_g.append(o)
            l_g.append(ls)
        yb, merged = mix_merge(ya[0], o_g, l_g, proj, w["w_a"], w["w_b"], "mix_merge")
        y1, x1, zh1, rs1 = proj_ln(merged, w["w_o"], xl, g1, vec(ln1_g, l), vec(ln1_b, l), "out_proj_ln")
        (u2, ab), got = modmm(x1, s2, sh2, w["w_ffn_t"].T, "ffn_up", gather("mixer", l + 1) if l + 1 < nl else None)
        if l + 1 < nl:
            full[l + 1].update(zip(groups["mixer"][1], got))
        h, y2, x2, zh2, rs2 = swiglu_proj_ln(ab, w["w_down"], x1, g2, vec(ln2_g, l), vec(ln2_b, l), "ffn_down_ln")
        saved.append(dict(xin=xl, u1=[u1, u1_f4.reshape(t, dm), u1_f16.reshape(t, dm)], proj=proj, qkv_b=qkv_b,
                          ya=ya, lse_a=lse_a, o_g=o_g, l_g=l_g, yb=yb, merged=merged,
                          y1=y1, x1=x1, zh1=zh1, rs1=rs1, u2=u2, ab=ab, h=h, y2=y2, zh2=zh2, rs2=rs2))
        xl = x2

    dx, loss_part = loss_head(xl, target)

    small = {k: [None] * nl for k in ("dmod", "ln1_g", "ln1_b", "ln2_g", "ln2_b", "sinks")}
    recv = {nm: [None] * nl for grp in groups.values() for nm in (pc.name for pc in grp[0])}

    def keep(group, l, got):
        for pc, arr in zip(groups[group][0], got):
            recv[pc.name][l] = arr

    for l in reversed(range(nl)):
        sv, w = saved[l], full[l]
        sh1, s1, g1, sh2, s2, g2 = [mod[l, j] for j in range(6)]
        fresh = lambda nm: lax.empty({**groups["mixer"][1], **groups["ffn"][1]}[nm], BF16)
        gbuf = {}
        dz2, dy2, sg, sb, sgate2, gbuf["w_down"] = ln_bwd(dx, sv["zh2"], sv["rs2"], sv["y2"], vec(ln2_g, l), g2,
                                                          sv["h"], "ln_bwd_ffn")
        small["ln2_g"][l], small["ln2_b"][l] = sg[0], sb[0]
        dab = dswiglu(dy2, w["w_down"], sv["ab"], "dswiglu")
        gbuf["w_ffn_t"] = wgrad(dab, sv["u2"], fresh("w_ffn_t"), 512, dm // 512, 0, 0, 1, "wgrad_ffn_up")
        (dx1, ss2, ssh2), got = dgrad_ffn(dab, w["w_ffn_t"], dz2, sv["x1"], s2, "dgrad_ffn", scatter("ffn", gbuf))
        keep("ffn", l, got)
        dz1, do1, sg, sb, sgate1, gbuf["w_o"] = ln_bwd(dx1, sv["zh1"], sv["rs1"], sv["y1"], vec(ln1_g, l), g1,
                                                       sv["merged"], "ln_bwd_mixer")
        small["ln1_g"][l], small["ln1_b"][l] = sg[0], sb[0]
        dya, dyb, dgab, gbuf["w_a"], gbuf["w_b"] = dmerge(
            do1, w["w_o"].T, sv["ya"][0], sv["yb"], w["w_a"], w["w_b"], w["w_a"].T, w["w_b"].T, sv["proj"], "dmerge")
        mixed = mix_bwd(dyb, sv["o_g"], sv["l_g"], "mix_bwd")
        do_g, dl_g = mixed[:N_GROUPS], mixed[N_GROUPS:]
        d_a, dsink = attn_bwd(sv["qkv_b"][0], sv["ya"], sv["lse_a"], dya.reshape(1, t, -1), None, ATTN_A,
                              sinks[l], "attn_a_bwd")
        small["sinks"][l] = dsink[:, 0]
        d_b = [attn_bwd(sv["qkv_b"][g], sv["o_g"][g], sv["l_g"][g], do_g[g], dl_g[g], cfg, sinks[l],
                        "attn_b%d_bwd" % g)[0] for g, cfg in enumerate(ATTN_B)]
        gw = wgrad(sv["u1"][0], d_a.reshape(t, A_W), fresh("w_in"), A_W, 1, 0, 0, 1, "wgrad_in_a")
        for g in range(N_GROUPS):
            gw = wgrad(sv["u1"][g], d_b[g].reshape(t, B_GW), gw, B_OUT_W, 3, 0, A_W // B_OUT_W + g, N_GROUPS,
                       "wgrad_in_b%d" % g)
        gbuf["w_in"] = wgrad(sv["u1"][0], dgab, gw, 512, 2 * dm // 512, 0, GATE_COL // 512, 1, "wgrad_in_gate")
        (dx, ss1, ssh1), got = dgrad_in(d_a, d_b, dgab, w["w_in"].T, dz1, sv["xin"], s1, "dgrad_in",
                                        scatter("mixer", gbuf))
        keep("mixer", l, got)
        small["dmod"][l] = jnp.stack([ssh1[0], ss1[0], sgate1[0], ssh2[0], ss2[0], sgate2[0]])
    grad_x = dx.reshape(x.shape)

    big_out = {}
    for nm, parts in recv.items():
        outs = adam_layers(parts, local(nm), local(nm, "m_"), local(nm, "v_"), "adam_" + nm)
        big_out[nm] = [o.transpose(0, 2, 1) for o in outs] if nm in TRANSPOSED else outs

    rows = jnp.concatenate(
        [jnp.stack(small["dmod"]).reshape(nl * 6, dm)]
        + [jnp.stack(small[k]) for k in ("ln1_g", "ln1_b", "ln2_g", "ln2_b")]
        + [jnp.pad(jnp.stack(small["sinks"]).reshape(1, -1), ((0, 0), (0, dm - nl * A_Q_HEADS))),
           jnp.broadcast_to(loss_part[0:1, 0:1], (1, dm))])
    n_rows = rows.shape[0]
    rows = jnp.pad(rows, ((0, -n_rows % 8), (0, 0)))
    rows_all = all_gather_small(rows, "gather_small_grads")

    def pack_small(pre):
        parts = [given[pre + "b_ada"].reshape(nl * 6, dm)]
        parts += [given[pre + k] for k in ("ln1_g", "ln1_b", "ln2_g", "ln2_b")]
        parts.append(jnp.pad(given[pre + "sinks"].reshape(1, -1), ((0, 0), (0, dm - nl * A_Q_HEADS))))
        p = jnp.concatenate(parts)
        return jnp.pad(p, ((0, rows.shape[0] - p.shape[0]), (0, 0)))

    souts = adam_reduce(rows_all, pack_small(""), pack_small("m_"), pack_small("v_"), "adam_small")

    def unpack_small(o):
        r = {"b_ada": o[0:nl * 6].reshape(nl, 6 * dm)}
        for j, k in enumerate(("ln1_g", "ln1_b", "ln2_g", "ln2_b")):
            r[k] = o[nl * 6 + j * nl: nl * 6 + (j + 1) * nl]
        r["sinks"] = o[nl * 10, 0:nl * A_Q_HEADS].reshape(nl, A_Q_HEADS)
        return r

    small_out = [unpack_small(o) for o in souts]
    loss = souts[0][nl * 10 + 1, 0]

    dmod_all = rows_all[:, 0:nl * 6].reshape(N_DEV, nl, 6 * dm)
    dm_loc = lax.dynamic_slice_in_dim(dmod_all, me * wc, wc, axis=2).transpose(1, 0, 2)
    dm_loc = jnp.pad(dm_loc, ((0, 0), (0, LANES - N_DEV), (0, 0)))
    sct = jnp.pad(sc_all.T, ((0, 0), (0, LANES - N_DEV)))
    ada_out = adam_w_ada(sct, dm_loc, w_ada, m_w_ada, v_w_ada)

    names = ["w_ada", "b_ada", "w_in", "sinks", "w_a", "w_b", "w_o", "ln1_g", "ln1_b",
             "w_gate", "w_up", "w_down", "ln2_g", "ln2_b"]

    def pick(kind, nm):
        if nm == "w_ada":
            return ada_out[kind]
        if nm in small_out[kind]:
            return small_out[kind][nm]
        return big_out[nm][kind]

    result = [loss, grad_x]
    for kind in range(4):
        result += [pick(kind, nm) for nm in names]
    return tuple(result)
```

```python
import functools

import jax
import jax.numpy as jnp
from jax import lax
from jax.experimental import pallas as pl
from jax.experimental.pallas import tpu as pltpu

F32 = jnp.float32
BF16 = jnp.bfloat16

D_MODEL = 1024
HEAD_DIM = 64
A_Q_HEADS = 8
A_KV_HEADS = 2
A_WINDOW = 128
B_GROUPS = ((128, 1), (512, 4), (2048, 16))
N_GROUPS = len(B_GROUPS)
B_HEADS_PER_GROUP = 4
N_ATTN_HEADS = A_Q_HEADS + B_HEADS_PER_GROUP * N_GROUPS
BLOCK = 128
A_W = (A_Q_HEADS + 2 * A_KV_HEADS) * HEAD_DIM
B_OUT_W = B_HEADS_PER_GROUP * HEAD_DIM
B_GW = 3 * B_OUT_W
B_ALL = N_GROUPS * B_OUT_W
GATE_COL = A_W + 3 * B_ALL
D_FF = 2816
FF_CHUNK = 256
DGRAD_CHUNK = 256
DN_ALPHA = 8.0 ** 0.25
LN_EPS = 1e-5
NEG_INF = -1e30
ADAM_LR, ADAM_B1, ADAM_B2, ADAM_EPS, ADAM_WD, ADAM_STEP = 0.001, 0.9, 0.999, 1e-08, 0.01, 10

N_DEV = 8
MESH = pl.DeviceIdType.MESH
VMEM_LIMIT = 56 * 1024 * 1024
ROW_TILE = 512
WGRAD_TILE_ELEMS = 2 * 1024 * 1024
LANES = 128
BF16_ROWS = 16


def _cp(*sem):
    return pltpu.CompilerParams(dimension_semantics=sem, vmem_limit_bytes=VMEM_LIMIT)


def _row_tile(t):
    return min(ROW_TILE, t)


def _slope(head):
    return 2.0 ** (-8.0 * (head + 1) / N_ATTN_HEADS)


def _sigmoid(x):
    return 1.0 / (1.0 + jnp.exp(-x))


def _dot(a, b):
    return jnp.dot(a, b, preferred_element_type=F32)


def _dot_nt(a, b):
    return lax.dot_general(a, b, (((1,), (1,)), ((), ())), preferred_element_type=F32)


def _dot_tn(a, b):
    return lax.dot_general(a, b, (((0,), (0,)), ((), ())), preferred_element_type=F32)


def _fold_scratch(tm, w):
    return [pltpu.VMEM((tm, LANES), F32)] * (w // LANES)


def _fold_to(dst_ref, val, scrs, d, col0=0):
    tm, w = val.shape
    if d == 1:
        dst_ref[0, :, col0:col0 + w] = val.astype(dst_ref.dtype)
        return
    for cb in range(w // LANES):
        scrs[cb][...] = val[:, cb * LANES:(cb + 1) * LANES]
    for r in range(d):
        for cb in range(w // LANES):
            piece = scrs[cb][pl.ds(r, tm // d, stride=d), :]
            dst_ref[r, :, col0 + cb * LANES:col0 + (cb + 1) * LANES] = piece.astype(dst_ref.dtype)


def _unfold_rows(rows_of, scrs, d, n, w):
    for r in range(d):
        for cb in range(w // LANES):
            scrs[cb][pl.ds(r, n, stride=d), :] = rows_of(r, slice(cb * LANES, (cb + 1) * LANES)).astype(F32)
    return jnp.concatenate([scrs[cb][0:d * n, :] for cb in range(w // LANES)], axis=1)


def _unfold_from(src_ref, scrs, d):
    if d == 1:
        return src_ref[0].astype(F32)
    _, n, w = src_ref.shape
    return _unfold_rows(lambda r, cols: src_ref[r, :, cols], scrs, d, n, w)


def _folded_spec(d, tm, w):
    return pl.BlockSpec((d, tm // d, w), lambda i: (0, i, 0))


def _me():
    return lax.axis_index("x"), lax.axis_index("y"), lax.axis_index("c")


def _flip(v, bit):
    return 1 - v if bit else v


def _peer(k):
    x, y, c = _me()
    return (_flip(x, k & 4), _flip(y, k & 2), _flip(c, k & 1))


def _peer_index(k):
    px, py, pc = _peer(k)
    return 4 * px + 2 * py + pc


def all_gather_small(v, name):
    r, c = v.shape

    def body(v_ref, out_ref, send_sems, recv_sems):
        me = _peer_index(0)
        out_ref[me] = v_ref[...]
        copies = []
        for k in range(1, N_DEV):
            cp = pltpu.make_async_remote_copy(
                src_ref=v_ref, dst_ref=out_ref.at[me],
                send_sem=send_sems.at[k - 1], recv_sem=recv_sems.at[k - 1],
                device_id=_peer(k), device_id_type=MESH)
            cp.start()
            copies.append(cp)
        for k in range(1, N_DEV):
            pltpu.make_async_remote_copy(
                src_ref=v_ref, dst_ref=out_ref.at[_peer_index(k)],
                send_sem=send_sems.at[k - 1], recv_sem=recv_sems.at[k - 1],
                device_id=_peer(k), device_id_type=MESH).wait_recv()
        for cp in copies:
            cp.wait_send()

    return pl.pallas_call(
        body, name=name,
        out_shape=jax.ShapeDtypeStruct((N_DEV, r, c), v.dtype),
        in_specs=[pl.BlockSpec(memory_space=pltpu.VMEM)],
        out_specs=pl.BlockSpec(memory_space=pltpu.VMEM),
        scratch_shapes=[pltpu.SemaphoreType.DMA((N_DEV - 1,)), pltpu.SemaphoreType.DMA((N_DEV - 1,))],
        compiler_params=pltpu.CompilerParams(vmem_limit_bytes=VMEM_LIMIT),
    )(v)


class Piece:
    def __init__(self, name, buf, axis, base, size):
        self.name, self.buf, self.axis, self.base, self.size = name, buf, axis, base, size

    def window(self, ref, j):
        start = self.base + j * self.size
        if self.axis == 1:
            return ref.at[:, pl.ds(pl.multiple_of(start, LANES), self.size)]
        return ref.at[pl.ds(pl.multiple_of(start, BF16_ROWS), self.size), :]


class Exchange:
    def __init__(self, kind, pieces, ins, out_shapes, bufs):
        self.kind, self.pieces, self.ins, self.out_shapes = kind, pieces, list(ins), list(out_shapes)
        self.buf_of = {nm: i for i, nm in enumerate(bufs)}
        self.n_in, self.n_out = len(self.ins), len(self.out_shapes)
        n = len(pieces)
        self.scratch = [pltpu.SemaphoreType.DMA((n, N_DEV - 1)), pltpu.SemaphoreType.DMA((n, N_DEV - 1)),
                        pltpu.SemaphoreType.DMA((n,))]
        self.in_specs = [pl.BlockSpec(memory_space=pl.ANY)] * self.n_in
        self.out_specs = [pl.BlockSpec(memory_space=pl.ANY)] * self.n_out
        self.out_shape = [jax.ShapeDtypeStruct(s, BF16) for s in self.out_shapes]

    def _ends(self, pi, ins, outs, to):
        pc = self.pieces[pi]
        if self.kind == "gather":
            return ins[pi], pc.window(outs[self.buf_of[pc.buf]], _peer_index(0))
        return pc.window(ins[self.buf_of[pc.buf]], to), outs[pi].at[_peer_index(0)]

    def _landing(self, pi, outs, frm):
        pc = self.pieces[pi]
        if self.kind == "gather":
            return pc.window(outs[self.buf_of[pc.buf]], frm)
        return outs[pi].at[frm]

    def _remote(self, pi, k, src, dst, sems):
        return pltpu.make_async_remote_copy(
            src_ref=src, dst_ref=dst, send_sem=sems[0].at[pi, k - 1], recv_sem=sems[1].at[pi, k - 1],
            device_id=_peer(k), device_id_type=MESH)

    def _local(self, pi, ins, outs, sems):
        return pltpu.make_async_copy(*self._ends(pi, ins, outs, _peer_index(0)), sems[2].at[pi])

    def start(self, ins, outs, sems):
        for pi in range(len(self.pieces)):
            self._local(pi, ins, outs, sems).start()
            for k in range(1, N_DEV):
                self._remote(pi, k, *self._ends(pi, ins, outs, _peer_index(k)), sems).start()

    def finish(self, ins, outs, sems):
        for pi in range(len(self.pieces)):
            src_like = self._ends(pi, ins, outs, _peer_index(0))[0]
            for k in range(1, N_DEV):
                self._remote(pi, k, src_like, self._landing(pi, outs, _peer_index(k)), sems).wait_recv()
        for pi in range(len(self.pieces)):
            for k in range(1, N_DEV):
                self._remote(pi, k, *self._ends(pi, ins, outs, _peer_index(k)), sems).wait_send()
            self._local(pi, ins, outs, sems).wait()


def _hosted(ex, refs, n_in, n_out, first, last):
    if ex is None:
        return refs
    ins, rest = refs[:n_in], refs[n_in:]
    ex_ins, rest = rest[:ex.n_in], rest[ex.n_in:]
    outs, rest = rest[:n_out], rest[n_out:]
    ex_outs, rest = rest[:ex.n_out], rest[ex.n_out:]
    scr, sems = rest[:len(rest) - 3], rest[len(rest) - 3:]
    pl.when(first)(lambda: ex.start(ex_ins, ex_outs, sems))
    pl.when(last)(lambda: ex.finish(ex_ins, ex_outs, sems))
    return tuple(ins) + tuple(outs) + tuple(scr)


def _host_call(body, ex, *, name, grid, out_shape, in_specs, out_specs, scratch_shapes=(), sem=None, args):
    n_out = len(out_shape)
    if ex is not None:
        out_shape = list(out_shape) + ex.out_shape
        in_specs = list(in_specs) + ex.in_specs
        out_specs = list(out_specs) + ex.out_specs
        scratch_shapes = list(scratch_shapes) + ex.scratch
        args = list(args) + ex.ins
    res = pl.pallas_call(body, name=name, grid=grid, out_shape=out_shape, in_specs=in_specs, out_specs=out_specs,
                         scratch_shapes=scratch_shapes, compiler_params=_cp(*sem))(*args)
    return res[:n_out], res[n_out:]


def run_exchange(ex, name):
    def body(*refs):
        ins, outs, sems = refs[:ex.n_in], refs[ex.n_in:ex.n_in + ex.n_out], refs[ex.n_in + ex.n_out:]
        ex.start(ins, outs, sems)
        ex.finish(ins, outs, sems)

    return pl.pallas_call(body, name=name, out_shape=ex.out_shape, in_specs=ex.in_specs, out_specs=ex.out_specs,
                          scratch_shapes=ex.scratch)(*ex.ins)


def mod_partial(c_all, w_ada, b_loc):
    nl, dm, wc = w_ada.shape

    def body(c_ref, w_ref, b_ref, o_ref, sc_ref):
        cc = c_ref[...]
        sc = cc * _sigmoid(cc)
        sc_ref[...] = sc
        o_ref[...] = jnp.dot(sc, w_ref[...], preferred_element_type=F32,
                             precision=lax.Precision.HIGHEST) + b_ref[...]

    return pl.pallas_call(
        body, name="mod_partial", grid=(nl,),
        out_shape=[jax.ShapeDtypeStruct((nl, N_DEV, wc), F32), jax.ShapeDtypeStruct((N_DEV, dm), F32)],
        in_specs=[pl.BlockSpec((N_DEV, dm), lambda l: (0, 0)),
                  pl.BlockSpec((None, dm, wc), lambda l: (l, 0, 0)),
                  pl.BlockSpec((None, 1, wc), lambda l: (l, 0, 0))],
        out_specs=[pl.BlockSpec((None, N_DEV, wc), lambda l: (l, 0, 0)),
                   pl.BlockSpec((N_DEV, dm), lambda l: (0, 0))],
        compiler_params=_cp("arbitrary"),
    )(c_all, w_ada, b_loc)


def in_proj(x, s, sh, w, name, ex=None):
    t, dm = x.shape
    n = w.shape[1]
    tm = _row_tile(t)
    nsteps = t // tm
    ch = B_OUT_W
    dils = [dil for _, dil in B_GROUPS if dil > 1]

    def body(*refs):
        i = pl.program_id(0)
        x_ref, s_ref, sh_ref, w_ref, u_ref, *rest = _hosted(ex, refs, 4, 2 + 2 * len(dils), i == 0, i == nsteps - 1)
        uf_refs, o_ref, qf_refs = rest[:len(dils)], rest[len(dils)], rest[len(dils) + 1:len(dils) * 2 + 1]
        scrs = rest[len(dils) * 2 + 1:]
        uf = x_ref[...] * (1.0 + s_ref[...]) + sh_ref[...]
        u = uf.astype(BF16)
        u_ref[...] = u
        for d, uf_ref in zip(dils, uf_refs):
            _fold_to(uf_ref, uf, scrs, d)
        for c0 in range(0, n, ch):
            res = _dot(u, w_ref[:, c0:c0 + ch])
            o_ref[:, c0:c0 + ch] = res.astype(BF16)
            if A_W <= c0 < GATE_COL:
                part, g = divmod((c0 - A_W) // ch, N_GROUPS)
                d = B_GROUPS[g][1]
                if d > 1:
                    _fold_to(qf_refs[dils.index(d)], res, scrs, d, part * ch)

    vec = pl.BlockSpec((1, dm), lambda i: (0, 0))
    row = lambda w_: pl.BlockSpec((tm, w_), lambda i: (i, 0))
    return _host_call(
        body, ex, name=name, grid=(nsteps,),
        out_shape=[jax.ShapeDtypeStruct((t, dm), BF16)]
                  + [jax.ShapeDtypeStruct((d, t // d, dm), BF16) for d in dils]
                  + [jax.ShapeDtypeStruct((t, n), BF16)]
                  + [jax.ShapeDtypeStruct((d, t // d, B_GW), BF16) for d in dils],
        in_specs=[row(dm), vec, vec, pl.BlockSpec((dm, n), lambda i: (0, 0))],
        out_specs=[row(dm)] + [_folded_spec(d, tm, dm) for d in dils] + [row(n)]
                  + [_folded_spec(d, tm, B_GW) for d in dils],
        scratch_shapes=_fold_scratch(tm, dm), sem=("arbitrary",), args=[x, s, sh, w])


def modmm(x, s, sh, w, name, ex=None):
    t, dm = x.shape
    n = w.shape[1]
    tm = _row_tile(t)
    nsteps = t // tm
    ch = 512

    def body(*refs):
        i = pl.program_id(0)
        x_ref, s_ref, sh_ref, w_ref, u_ref, o_ref = _hosted(ex, refs, 4, 2, i == 0, i == nsteps - 1)
        u = (x_ref[...] * (1.0 + s_ref[...]) + sh_ref[...]).astype(BF16)
        u_ref[...] = u
        for c0 in range(0, n, ch):
            o_ref[:, c0:c0 + ch] = _dot(u, w_ref[:, c0:c0 + ch]).astype(BF16)

    vec = pl.BlockSpec((1, dm), lambda i: (0, 0))
    return _host_call(
        body, ex, name=name, grid=(nsteps,),
        out_shape=[jax.ShapeDtypeStruct((t, dm), BF16), jax.ShapeDtypeStruct((t, n), BF16)],
        in_specs=[pl.BlockSpec((tm, dm), lambda i: (i, 0)), vec, vec,
                  pl.BlockSpec((dm, n), lambda i: (0, 0))],
        out_specs=[pl.BlockSpec((tm, dm), lambda i: (i, 0)), pl.BlockSpec((tm, n), lambda i: (i, 0))],
        sem=("arbitrary",), args=[x, s, sh, w])


def _halves(tm):
    half = tm // 2 if tm % 32 == 0 else tm
    return [slice(r0, r0 + half) for r0 in range(0, tm, half)]


def _ln_store(y, rows, xres_ref, g_ref, lg_ref, lb_ref, y_ref, xo_ref, zh_ref, rs_ref):
    y_ref[rows, :] = y.astype(BF16)
    z = DN_ALPHA * xres_ref[rows, :] + g_ref[...] * y
    mu = jnp.mean(z, axis=1, keepdims=True)
    zc = z - mu
    var = jnp.mean(zc * zc, axis=1, keepdims=True)
    rstd = lax.rsqrt(var + LN_EPS)
    zhat = zc * rstd
    zh_ref[rows, :] = zhat.astype(zh_ref.dtype)
    xo_ref[rows, :] = zhat * lg_ref[...] + lb_ref[...]
    rs_ref[rows, :] = jnp.broadcast_to(rstd, (zhat.shape[0], rs_ref.shape[1]))


def _ln_out_shapes(t, dm):
    return [jax.ShapeDtypeStruct((t, dm), BF16), jax.ShapeDtypeStruct((t, dm), F32),
            jax.ShapeDtypeStruct((t, dm), F32), jax.ShapeDtypeStruct((t, LANES), F32)]


def _ln_out_specs(tm, dm):
    row = pl.BlockSpec((tm, dm), lambda i: (i, 0))
    return [row, row, row, pl.BlockSpec((tm, LANES), lambda i: (i, 0))]


def proj_ln(a, w, xres, gate, lg, lb, name):
    t, k = a.shape
    dm = w.shape[1]
    tm = _row_tile(t)

    def body(a_ref, w_ref, xres_ref, g_ref, lg_ref, lb_ref, y_ref, xo_ref, zh_ref, rs_ref):
        for rows in _halves(tm):
            y = _dot(a_ref[rows, :], w_ref[...])
            _ln_store(y, rows, xres_ref, g_ref, lg_ref, lb_ref, y_ref, xo_ref, zh_ref, rs_ref)

    vec = pl.BlockSpec((1, dm), lambda i: (0, 0))
    return pl.pallas_call(
        body, name=name, grid=(t // tm,),
        out_shape=_ln_out_shapes(t, dm),
        in_specs=[pl.BlockSpec((tm, k), lambda i: (i, 0)), pl.BlockSpec((k, dm), lambda i: (0, 0)),
                  pl.BlockSpec((tm, dm), lambda i: (i, 0)), vec, vec, vec],
        out_specs=_ln_out_specs(tm, dm),
        compiler_params=_cp("parallel"),
    )(a, w, xres, gate, lg, lb)


def swiglu_proj_ln(ab, w, xres, gate, lg, lb, name):
    t = ab.shape[0]
    f, dm = w.shape
    tm = _row_tile(t)

    def body(a_ref, b_ref, w_ref, xres_ref, g_ref, lg_ref, lb_ref, h_ref, y_ref, xo_ref, zh_ref, rs_ref):
        for rows in _halves(tm):
            y = None
            for c0 in range(0, f, FF_CHUNK):
                cols = slice(c0, c0 + FF_CHUNK)
                a = a_ref[rows, cols].astype(F32)
                h = (a * _sigmoid(a) * b_ref[rows, cols].astype(F32)).astype(BF16)
                h_ref[rows, cols] = h
                part = _dot(h, w_ref[cols, :])
                y = part if y is None else y + part
            _ln_store(y, rows, xres_ref, g_ref, lg_ref, lb_ref, y_ref, xo_ref, zh_ref, rs_ref)

    vec = pl.BlockSpec((1, dm), lambda i: (0, 0))
    return pl.pallas_call(
        body, name=name, grid=(t // tm,),
        out_shape=[jax.ShapeDtypeStruct((t, f), BF16)] + _ln_out_shapes(t, dm),
        in_specs=[pl.BlockSpec((tm, f), lambda i: (i, 0)), pl.BlockSpec((tm, f), lambda i: (i, 1)),
                  pl.BlockSpec((f, dm), lambda i: (0, 0)),
                  pl.BlockSpec((tm, dm), lambda i: (i, 0)), vec, vec, vec],
        out_specs=[pl.BlockSpec((tm, f), lambda i: (i, 0))] + _ln_out_specs(tm, dm),
        compiler_params=_cp("parallel"),
    )(ab, ab, w, xres, gate, lg, lb)


class AttnCfg:
    def __init__(self, dil, heads, kv_heads, qc, kc, vc, max_dist, head0, sinks):
        self.dil, self.heads, self.kv_heads = dil, heads, kv_heads
        self.qc, self.kc, self.vc = qc, kc, vc
        self.max_dist, self.head0, self.sinks = max_dist, head0, sinks
        self.wq = heads * HEAD_DIM
        self.wk = kv_heads * HEAD_DIM
        self.wout = self.wq + 2 * self.wk


ATTN_A = AttnCfg(1, A_Q_HEADS, A_KV_HEADS, 0, A_Q_HEADS * HEAD_DIM, (A_Q_HEADS + A_KV_HEADS) * HEAD_DIM,
                 A_WINDOW - 1, 0, True)


def _attn_b_cfg(g):
    win, dil = B_GROUPS[g]
    cols = ((A_W + g * B_OUT_W, A_W + B_ALL + g * B_OUT_W, A_W + 2 * B_ALL + g * B_OUT_W) if dil == 1
            else (0, B_OUT_W, 2 * B_OUT_W))
    return AttnCfg(dil, B_HEADS_PER_GROUP, B_HEADS_PER_GROUP, *cols, win // dil,
                   A_Q_HEADS + g * B_HEADS_PER_GROUP, False)


ATTN_B = [_attn_b_cfg(g) for g in range(N_GROUPS)]


SCALE = HEAD_DIM ** -0.5


def _head(h):
    return slice(h * HEAD_DIM, (h + 1) * HEAD_DIM)


def _stack(parts):
    return parts[0] if len(parts) == 1 else jnp.concatenate(parts, axis=0)


def _masked_bias(mask, distf, cfg, h, d):
    return jnp.where(mask, distf * (-(_slope(cfg.head0 + h) * d)), NEG_INF)


def _lane_halves(rows):
    lane = lax.broadcasted_iota(jnp.int32, (rows, LANES), 1)
    return [lane < HEAD_DIM, lane >= HEAD_DIM]


def _n_pair_sources(cfg):
    return cfg.kv_heads if cfg.heads > cfg.kv_heads else cfg.heads // 2


def _pair_source(p, grp):
    return p if grp == 1 else (2 * p) // grp


def _fill_pairs(dst, prev_ref, cur_ref, grp):
    for j in range(dst.shape[0]):
        for ref, rows in ((prev_ref, slice(0, BLOCK)), (cur_ref, slice(BLOCK, dst.shape[1]))):
            if grp == 1:
                dst[j, rows, :] = ref[:, j * LANES:(j + 1) * LANES]
            else:
                one = ref[:, _head(j)]
                dst[j, rows, :] = jnp.concatenate([one, one], axis=1)


def _band(i, max_dist):
    qi = lax.broadcasted_iota(jnp.int32, (BLOCK, 2 * BLOCK), 0)
    sj = lax.broadcasted_iota(jnp.int32, (BLOCK, 2 * BLOCK), 1)
    dist = qi + BLOCK - sj
    valid = (dist >= 0) & (dist <= max_dist)
    first_key = jnp.where(i > 0, 0, BLOCK)
    valid_first = valid & (sj >= first_key)
    return dist, valid, valid_first


def _attn_geometry(cfg, n, tq_max=512):
    tq = min(tq_max, n)
    return tq, tq // BLOCK, n // tq


def attn_fwd(qkv, cfg, sinks, name):
    d, n, _ = qkv.shape
    tq, nsub, nqb = _attn_geometry(cfg, n)
    wq, wk = cfg.wq, cfg.wk
    grp = cfg.heads // cfg.kv_heads

    def body(sink_ref, q_ref, kc_ref, kp_ref, vc_ref, vp_ref, o_ref, l_ref, kf, vf):
        i = pl.program_id(1)
        _fill_pairs(kf, kp_ref, kc_ref, grp)
        _fill_pairs(vf, vp_ref, vc_ref, grp)
        dist, valid, valid_first = _band(i, cfg.max_dist)
        distf = dist.astype(F32)
        half_q, half_k = _lane_halves(BLOCK), _lane_halves(2 * BLOCK)
        rows = [slice(a * BLOCK, (a + 1) * BLOCK) for a in range(nsub)]
        wins = [slice(a * BLOCK, (a + 2) * BLOCK) for a in range(nsub)]
        for p in range(cfg.heads // 2):
            lanes = slice(p * LANES, (p + 1) * LANES)
            ki = _pair_source(p, grp)
            hs = (2 * p, 2 * p + 1)
            b_reg = [_masked_bias(valid, distf, cfg, h, d) for h in hs]
            b_first = [_masked_bias(valid_first, distf, cfg, h, d) for h in hs]
            ss = []
            for a in range(nsub):
                q2 = q_ref[rows[a], lanes] * SCALE
                k2 = kf[ki, wins[a], :]
                ss.append([_dot_nt(jnp.where(half_q[e], q2, 0), k2) + (b_first[e] if a == 0 else b_reg[e])
                           for e in range(2)])
            es, invs, lses = [], [], []
            for a in range(nsub):
                e_a, inv_a, lse_a = [], [], []
                for e in range(2):
                    m = jnp.max(ss[a][e], axis=1, keepdims=True)
                    if cfg.sinks:
                        m = jnp.maximum(m, sink_ref[hs[e]])
                    ex = jnp.exp(ss[a][e] - m)
                    den = jnp.sum(ex, axis=1, keepdims=True)
                    if cfg.sinks:
                        den = den + jnp.exp(sink_ref[hs[e]] - m)
                    e_a.append(ex.astype(BF16))
                    inv_a.append(1.0 / den)
                    lse_a.append(m + jnp.log(den))
                es.append(e_a)
                invs.append(inv_a)
                lses.append(lse_a)
            for a in range(nsub):
                v2 = vf[ki, wins[a], :]
                pcat = jnp.concatenate(es[a], axis=1)
                vcat = jnp.concatenate([jnp.where(half_k[e], v2, 0) for e in range(2)], axis=0)
                o = _dot(pcat, vcat) * jnp.where(half_q[0], invs[a][0], invs[a][1])
                o_ref[rows[a], lanes] = o.astype(BF16)
                l_ref[rows[a], lanes] = jnp.where(half_q[0], lses[a][0], lses[a][1])

    prev = lambda i: jnp.maximum(i * nsub - 1, 0)
    cur = lambda w, c: pl.BlockSpec((None, tq, w), lambda r, i: (r, i, c // w))
    prv = lambda w, c: pl.BlockSpec((None, BLOCK, w), lambda r, i: (r, prev(i), c // w))
    out = pl.BlockSpec((None, tq, wq), lambda r, i: (r, i, 0))
    pair_scratch = pltpu.VMEM((_n_pair_sources(cfg), tq + BLOCK, LANES), BF16)
    return pl.pallas_call(
        body, name=name, grid=(d, nqb),
        out_shape=[jax.ShapeDtypeStruct((d, n, wq), BF16), jax.ShapeDtypeStruct((d, n, wq), F32)],
        in_specs=[pl.BlockSpec(memory_space=pltpu.SMEM),
                  cur(wq, cfg.qc), cur(wk, cfg.kc), prv(wk, cfg.kc), cur(wk, cfg.vc), prv(wk, cfg.vc)],
        out_specs=[out, out],
        scratch_shapes=[pair_scratch, pair_scratch],
        compiler_params=_cp("parallel", "parallel"),
    )(sinks, qkv, qkv, qkv, qkv, qkv)


def mix_merge(ya, o_g, l_g, proj, w_a, w_b, name):
    t = ya.shape[0]
    dm = w_a.shape[1]
    tm = _row_tile(t)
    gcol = GATE_COL // dm
    dils = [o.shape[0] for o in o_g]

    def body(ya_ref, o0, o1, o2, l0, l1, l2, ga_ref, gb_ref, wa_ref, wb_ref, yb_ref, mg_ref, *scrs):
        ls = [_unfold_from(l, scrs, d) for l, d in zip((l0, l1, l2), dils)]
        m = jnp.maximum(jnp.maximum(ls[0], ls[1]), ls[2])
        es = [jnp.exp(l - m) for l in ls]
        inv = 1.0 / (es[0] + es[1] + es[2])
        yb = sum(_unfold_from(o, scrs, d) * (e * inv) for o, e, d in zip((o0, o1, o2), es, dils)).astype(BF16)
        yb_ref[...] = yb
        pa = _dot(ya_ref[...], wa_ref[...])
        pb = _dot(yb, wb_ref[...])
        mg = _sigmoid(ga_ref[...].astype(F32)) * pa + _sigmoid(gb_ref[...].astype(F32)) * pb
        mg_ref[...] = mg.astype(BF16)

    wide = lambda w: pl.BlockSpec((tm, w), lambda i: (i, 0))
    folded = [_folded_spec(d, tm, B_OUT_W) for d in dils]
    return pl.pallas_call(
        body, name=name, grid=(t // tm,),
        out_shape=[jax.ShapeDtypeStruct((t, B_OUT_W), BF16), jax.ShapeDtypeStruct((t, dm), BF16)],
        in_specs=[wide(ya.shape[1])] + folded + folded
                 + [pl.BlockSpec((tm, dm), lambda i: (i, gcol)), pl.BlockSpec((tm, dm), lambda i: (i, gcol + 1)),
                    pl.BlockSpec(w_a.shape, lambda i: (0, 0)), pl.BlockSpec(w_b.shape, lambda i: (0, 0))],
        out_specs=[wide(B_OUT_W), wide(dm)],
        scratch_shapes=_fold_scratch(tm, B_OUT_W),
        compiler_params=_cp("parallel"),
    )(ya, *o_g, *l_g, proj, proj, w_a, w_b)


def loss_head(y, target):
    t, dm = y.shape
    tm = _row_tile(t)

    def body(y_ref, t_ref, dy_ref, loss_ref):
        @pl.when(pl.program_id(0) == 0)
        def _():
            loss_ref[...] = jnp.zeros_like(loss_ref)
        err = y_ref[...] - t_ref[...]
        dy_ref[...] = err * (1.0 / dm)
        per_row = jnp.sum(err * err, axis=1, keepdims=True) * (1.0 / dm)
        loss_ref[...] += 0.5 * jnp.sum(per_row, axis=0, keepdims=True)

    row = pl.BlockSpec((tm, dm), lambda i: (i, 0))
    return pl.pallas_call(
        body, name="loss_head", grid=(t // tm,),
        out_shape=[jax.ShapeDtypeStruct((t, dm), F32), jax.ShapeDtypeStruct((8, LANES), F32)],
        in_specs=[row, row],
        out_specs=[row, pl.BlockSpec((8, LANES), lambda i: (0, 0))],
        compiler_params=_cp("arbitrary"),
    )(y, target)


def _fold_rows(v):
    tm, c = v.shape
    return jnp.sum(v.reshape(tm // 8, 8, c), axis=0)


def _finish_sums(refs, nsteps):
    @pl.when(pl.program_id(0) == nsteps - 1)
    def _():
        for r in refs:
            r[...] = jnp.broadcast_to(jnp.sum(r[...], axis=0, keepdims=True), r.shape)


def ln_bwd(dxo, zhat, rstd, ysub, lg, gate, act, name):
    t, dm = dxo.shape
    k = act.shape[1]
    tm = _row_tile(t)
    nsteps = t // tm
    ch = 256

    def body(dxo_ref, zh_ref, rs_ref, y_ref, lg_ref, g_ref, a_ref, dz_ref, dy_ref, sg_ref, sb_ref, sgate_ref,
             gw_ref, acc):
        @pl.when(pl.program_id(0) == 0)
        def _():
            for r in (sg_ref, sb_ref, sgate_ref, acc):
                r[...] = jnp.zeros_like(r)
        for rows in _halves(tm):
            dxo_v = dxo_ref[rows, :]
            zh = zh_ref[rows, :]
            dxh = dxo_v * lg_ref[...]
            m1 = jnp.mean(dxh, axis=1, keepdims=True)
            m2 = jnp.mean(dxh * zh, axis=1, keepdims=True)
            dz = rs_ref[rows, 0:1] * (dxh - m1 - zh * m2)
            dz_ref[rows, :] = dz
            dy = (g_ref[...] * dz).astype(BF16)
            dy_ref[rows, :] = dy
            sg_ref[...] += _fold_rows(dxo_v * zh)
            sb_ref[...] += _fold_rows(dxo_v)
            sgate_ref[...] += _fold_rows(dz * y_ref[rows, :].astype(F32))
            a = a_ref[rows, :]
            for c0 in range(0, dm, ch):
                acc[:, c0:c0 + ch] += _dot_tn(a, dy[:, c0:c0 + ch])
        _finish_sums((sg_ref, sb_ref, sgate_ref), nsteps)

        @pl.when(pl.program_id(0) == nsteps - 1)
        def _():
            gw_ref[...] = acc[...].astype(BF16)

    row = pl.BlockSpec((tm, dm), lambda i: (i, 0))
    vec = pl.BlockSpec((1, dm), lambda i: (0, 0))
    sums = pl.BlockSpec((8, dm), lambda i: (0, 0))
    return pl.pallas_call(
        body, name=name, grid=(nsteps,),
        out_shape=[jax.ShapeDtypeStruct((t, dm), F32), jax.ShapeDtypeStruct((t, dm), BF16)]
                  + [jax.ShapeDtypeStruct((8, dm), F32)] * 3 + [jax.ShapeDtypeStruct((k, dm), BF16)],
        in_specs=[row, row, pl.BlockSpec((tm, LANES), lambda i: (i, 0)), row, vec, vec,
                  pl.BlockSpec((tm, k), lambda i: (i, 0))],
        out_specs=[row, row, sums, sums, sums, pl.BlockSpec((k, dm), lambda i: (0, 0))],
        scratch_shapes=[pltpu.VMEM((k, dm), F32)],
        compiler_params=_cp("arbitrary"),
    )(dxo, zhat, rstd, ysub, lg, gate, act)


def _mod_bwd_store(du_of, dz_ref, x_ref, s_ref, dx_ref, ss_ref, ssh_ref, nsteps):
    @pl.when(pl.program_id(0) == 0)
    def _():
        ss_ref[...] = jnp.zeros_like(ss_ref)
        ssh_ref[...] = jnp.zeros_like(ssh_ref)
    for c0 in range(0, dx_ref.shape[1], DGRAD_CHUNK):
        cols = slice(c0, c0 + DGRAD_CHUNK)
        du = du_of(cols)
        dx_ref[:, cols] = DN_ALPHA * dz_ref[:, cols] + du * (1.0 + s_ref[:, cols])
        ss_ref[:, cols] += _fold_rows(du * x_ref[:, cols])
        ssh_ref[:, cols] += _fold_rows(du)
    _finish_sums((ss_ref, ssh_ref), nsteps)


def dgrad_ffn(g, wt, dz, xin, s, name, ex=None):
    t, dm = dz.shape
    k = g.shape[1]
    tm = _row_tile(t)
    nsteps = t // tm

    def body(*refs):
        i = pl.program_id(0)
        g_ref, w_ref, dz_ref, x_ref, s_ref, dx_ref, ss_ref, ssh_ref = _hosted(ex, refs, 5, 3, i == 0, i == nsteps - 1)
        g_v = g_ref[...]
        _mod_bwd_store(lambda cols: _dot(g_v, w_ref[:, cols]), dz_ref, x_ref, s_ref, dx_ref, ss_ref, ssh_ref, nsteps)

    row = pl.BlockSpec((tm, dm), lambda i: (i, 0))
    acc = pl.BlockSpec((8, dm), lambda i: (0, 0))
    return _host_call(
        body, ex, name=name, grid=(nsteps,),
        out_shape=[jax.ShapeDtypeStruct((t, dm), F32)] + [jax.ShapeDtypeStruct((8, dm), F32)] * 2,
        in_specs=[pl.BlockSpec((tm, k), lambda i: (i, 0)), pl.BlockSpec((k, dm), lambda i: (0, 0)),
                  row, row, pl.BlockSpec((1, dm), lambda i: (0, 0))],
        out_specs=[row, acc, acc], sem=("arbitrary",), args=[g, wt, dz, xin, s])


def dswiglu(dy, wdt, ab, name):
    t, dm = dy.shape
    f = wdt.shape[1]
    tm = _row_tile(t)

    def body(dy_ref, w_ref, a_ref, b_ref, o_ref):
        dy_v = dy_ref[...]
        for c0 in range(0, f, FF_CHUNK):
            cols = slice(c0, c0 + FF_CHUNK)
            dh = _dot(dy_v, w_ref[:, cols])
            a = a_ref[:, cols].astype(F32)
            sg = _sigmoid(a)
            o_ref[:, cols] = (dh * b_ref[:, cols].astype(F32) * (sg * (1.0 + a * (1.0 - sg)))).astype(BF16)
            o_ref[:, f + c0:f + c0 + FF_CHUNK] = (dh * (a * sg)).astype(BF16)

    return pl.pallas_call(
        body, name=name, grid=(t // tm,),
        out_shape=jax.ShapeDtypeStruct((t, 2 * f), BF16),
        in_specs=[pl.BlockSpec((tm, dm), lambda i: (i, 0)), pl.BlockSpec((dm, f), lambda i: (0, 0)),
                  pl.BlockSpec((tm, f), lambda i: (i, 0)), pl.BlockSpec((tm, f), lambda i: (i, 1))],
        out_specs=pl.BlockSpec((tm, 2 * f), lambda i: (i, 0)),
        compiler_params=_cp("parallel"),
    )(dy, wdt, ab, ab)


def dgrad_in(d_a, d_b, dgab, wt, dz, xin, s, name, ex=None):
    t, dm = dz.shape
    tm = _row_tile(t)
    nsteps = t // tm
    dils = [a.shape[0] for a in d_b]

    def body(*refs):
        i = pl.program_id(0)
        (da_ref, b0, b1, b2, dg_ref, w_ref, dz_ref, x_ref, s_ref, dx_ref, ss_ref, ssh_ref,
         *scrs) = _hosted(ex, refs, 9, 3, i == 0, i == nsteps - 1)
        vs = [b_ref[...].reshape(tm, B_GW) for b_ref in (b0, b1, b2)]

        def du_of(cols):
            du = _dot(da_ref[0], w_ref[0:A_W, cols])
            for g, (v, d) in enumerate(zip(vs, dils)):
                part = None
                for p in range(3):
                    r0 = A_W + p * B_ALL + g * B_OUT_W
                    term = _dot(v[:, p * B_OUT_W:(p + 1) * B_OUT_W], w_ref[r0:r0 + B_OUT_W, cols])
                    part = term if part is None else part + term
                if d == 1:
                    du = du + part
                else:
                    n = tm // d
                    du = du + _unfold_rows(lambda r, cs: part[r * n:(r + 1) * n, cs], scrs, d, n, DGRAD_CHUNK)
            for j in range(2):
                du = du + _dot(dg_ref[:, j * dm:(j + 1) * dm], w_ref[GATE_COL + j * dm:GATE_COL + (j + 1) * dm, cols])
            return du

        _mod_bwd_store(du_of, dz_ref, x_ref, s_ref, dx_ref, ss_ref, ssh_ref, nsteps)

    row = pl.BlockSpec((tm, dm), lambda i: (i, 0))
    acc = pl.BlockSpec((8, dm), lambda i: (0, 0))
    return _host_call(
        body, ex, name=name, grid=(nsteps,),
        out_shape=[jax.ShapeDtypeStruct((t, dm), F32)] + [jax.ShapeDtypeStruct((8, dm), F32)] * 2,
        in_specs=[_folded_spec(1, tm, A_W)] + [_folded_spec(d, tm, B_GW) for d in dils]
                 + [pl.BlockSpec((tm, 2 * dm), lambda i: (i, 0)), pl.BlockSpec(wt.shape, lambda i: (0, 0)),
                    row, row, pl.BlockSpec((1, dm), lambda i: (0, 0))],
        out_specs=[row, acc, acc],
        scratch_shapes=_fold_scratch(tm, DGRAD_CHUNK), sem=("arbitrary",), args=[d_a, *d_b, dgab, wt, dz, xin, s])


def wgrad(a, b, buf, tn, nj, b0, o0, om, name):
    t, k = a.shape
    tt = ROW_TILE
    while tt * 2 * k <= WGRAD_TILE_ELEMS and tt * 2 <= t:
        tt *= 2
    nsteps = t // tt
    last = nsteps - 1

    def body(a_ref, b_ref, buf_ref, o_ref, acc):
        s, j = pl.program_id(0), pl.program_id(1)

        @pl.when(s == 0)
        def _():
            acc[j] = jnp.zeros(acc.shape[1:], F32)
        acc[j] += _dot_tn(a_ref[...], b_ref[...])

        @pl.when(s == last)
        def _():
            o_ref[...] = acc[j].astype(BF16)

    return pl.pallas_call(
        body, name=name, grid=(nsteps, nj),
        out_shape=jax.ShapeDtypeStruct(buf.shape, buf.dtype),
        in_specs=[pl.BlockSpec((tt, k), lambda s, j: (s, 0)),
                  pl.BlockSpec((tt, tn), lambda s, j: (s, b0 + j)),
                  pl.BlockSpec(memory_space=pl.ANY)],
        out_specs=pl.BlockSpec((k, tn), lambda s, j: (0, o0 + om * jnp.where(s == last, j, 0))),
        scratch_shapes=[pltpu.VMEM((nj, k, tn), F32)],
        input_output_aliases={2: 0},
        compiler_params=_cp("arbitrary", "arbitrary"),
    )(a, b, buf)


def dmerge(do, wot, ya, yb, w_a, w_b, wat, wbt, proj, name):
    t, dm = do.shape
    tm = _row_tile(t)
    nsteps = t // tm
    gcol = GATE_COL // dm
    ch = 256

    def body(do_ref, wot_ref, ya_ref, yb_ref, wa_ref, wb_ref, wat_ref, wbt_ref, g_ref,
             dya_ref, dyb_ref, dg_ref, gwa_ref, gwb_ref, dm_scr, acc_a, acc_b):
        i, j = pl.program_id(0), pl.program_id(1)

        @pl.when((i == 0) & (j == 0))
        def _():
            acc_a[...] = jnp.zeros_like(acc_a)
            acc_b[...] = jnp.zeros_like(acc_b)

        @pl.when(j == 0)
        def _():
            do_v = do_ref[...]
            for c0 in range(0, dm, ch):
                dm_scr[:, c0:c0 + ch] = _dot(do_v, wot_ref[:, c0:c0 + ch])

        def branch(y_ref, w_ref, wt_ref, dy_ref, acc, gw_ref):
            y = y_ref[...]
            dy = None
            for c0 in range(0, dm, ch):
                cols = slice(c0, c0 + ch)
                p = _dot(y, w_ref[:, cols])
                sg = _sigmoid(g_ref[:, cols].astype(F32))
                dmg = dm_scr[:, cols]
                dp = (dmg * sg).astype(BF16)
                dg_ref[:, cols] = (dmg * p * (sg * (1.0 - sg))).astype(BF16)
                acc[:, cols] += _dot_tn(y, dp)
                part = _dot(dp, wt_ref[cols, :])
                dy = part if dy is None else dy + part
            dy_ref[...] = dy.astype(dy_ref.dtype)

            @pl.when(i == nsteps - 1)
            def _():
                gw_ref[...] = acc[...].astype(BF16)

        pl.when(j == 0)(lambda: branch(ya_ref, wa_ref, wat_ref, dya_ref, acc_a, gwa_ref))
        pl.when(j == 1)(lambda: branch(yb_ref, wb_ref, wbt_ref, dyb_ref, acc_b, gwb_ref))

    full = lambda arr: pl.BlockSpec(arr.shape, lambda i, j: (0, 0))
    rowc = lambda w: pl.BlockSpec((tm, w), lambda i, j: (i, 0))
    return pl.pallas_call(
        body, name=name, grid=(nsteps, 2),
        out_shape=[jax.ShapeDtypeStruct((t, ya.shape[1]), BF16), jax.ShapeDtypeStruct((t, yb.shape[1]), F32),
                   jax.ShapeDtypeStruct((t, 2 * dm), BF16),
                   jax.ShapeDtypeStruct(w_a.shape, BF16), jax.ShapeDtypeStruct(w_b.shape, BF16)],
        in_specs=[rowc(dm), full(wot), rowc(ya.shape[1]), rowc(yb.shape[1]), full(w_a), full(w_b),
                  full(wat), full(wbt), pl.BlockSpec((tm, dm), lambda i, j: (i, gcol + j))],
        out_specs=[rowc(ya.shape[1]), rowc(yb.shape[1]), pl.BlockSpec((tm, dm), lambda i, j: (i, j)),
                   full(w_a), full(w_b)],
        scratch_shapes=[pltpu.VMEM((tm, dm), F32), pltpu.VMEM(w_a.shape, F32), pltpu.VMEM(w_b.shape, F32)],
        compiler_params=_cp("arbitrary", "arbitrary"),
    )(do, wot, ya, yb, w_a, w_b, wat, wbt, proj)


def mix_bwd(dyb, o_g, l_g, name):
    t, w = dyb.shape
    tm = _row_tile(t)
    nh = w // HEAD_DIM
    dils = [o.shape[0] for o in o_g]

    def body(dyb_ref, o0, o1, o2, l0, l1, l2, do0, do1, do2, dl0, dl1, dl2, *scr):
        ls = [_unfold_from(l, scr, d) for l, d in zip((l0, l1, l2), dils)]
        m = jnp.maximum(jnp.maximum(ls[0], ls[1]), ls[2])
        es = [jnp.exp(l - m) for l in ls]
        inv = 1.0 / (es[0] + es[1] + es[2])
        wts = [e * inv for e in es]
        dyb_v = dyb_ref[...]
        dws = []
        for o_ref, do_ref, wt, d in zip((o0, o1, o2), (do0, do1, do2), wts, dils):
            prod = dyb_v * _unfold_from(o_ref, scr, d)
            _fold_to(do_ref, dyb_v * wt, scr, d)
            for h in range(nh):
                hs = slice(h * HEAD_DIM, (h + 1) * HEAD_DIM)
                dws.append(jnp.broadcast_to(jnp.sum(prod[:, hs], axis=1, keepdims=True), (tm, HEAD_DIM)))
        for g, (dl_ref, d) in enumerate(zip((dl0, dl1, dl2), dils)):
            cols = []
            for h in range(nh):
                hs = slice(h * HEAD_DIM, (h + 1) * HEAD_DIM)
                mean = sum(wts[g2][:, hs] * dws[g2 * nh + h] for g2 in range(N_GROUPS))
                cols.append(wts[g][:, hs] * (dws[g * nh + h] - mean))
            _fold_to(dl_ref, jnp.concatenate(cols, axis=1), scr, d)

    folded = [_folded_spec(d, tm, w) for d in dils]
    return pl.pallas_call(
        body, name=name, grid=(t // tm,),
        out_shape=[jax.ShapeDtypeStruct(o.shape, BF16) for o in o_g]
                  + [jax.ShapeDtypeStruct(o.shape, F32) for o in o_g],
        in_specs=[pl.BlockSpec((tm, w), lambda i: (i, 0))] + folded + folded,
        out_specs=folded + folded,
        scratch_shapes=_fold_scratch(tm, w),
        compiler_params=_cp("parallel"),
    )(dyb, *o_g, *l_g)


def attn_bwd(qkv, o, lse, do, dlse, cfg, sinks, name):
    d, n, _ = qkv.shape
    tq, nsub, nqb = _attn_geometry(cfg, n, 1024)
    wq, wk, wout = cfg.wq, cfg.wk, cfg.wout
    grp = cfg.heads // cfg.kv_heads
    has_dl = dlse is not None

    def body(*refs):
        sink_ref, q_ref, qn_ref, kc_ref, kp_ref, vc_ref, vp_ref = refs[:7]
        o_ref, on_ref, do_ref, don_ref, l_ref, ln_ref = refs[7:13]
        rest = refs[13:]
        dl_ref = dln_ref = None
        if has_dl:
            dl_ref, dln_ref = rest[:2]
            rest = rest[2:]
        out_ref = rest[0]
        rest = rest[1:]
        if cfg.sinks:
            dsink_ref = rest[0]
            rest = rest[1:]
        kf, vf = rest
        r, i = pl.program_id(0), pl.program_id(1)
        _fill_pairs(kf, kp_ref, kc_ref, grp)
        _fill_pairs(vf, vp_ref, vc_ref, grp)
        dist, valid, valid_first = _band(i, cfg.max_dist)
        distf = dist.astype(F32)
        next_dist = jnp.where(i < nqb - 1, cfg.max_dist, -1)
        valid_next = (dist[:, 0:BLOCK] >= 0) & (dist[:, 0:BLOCK] <= next_dist)
        half_q = _lane_halves(BLOCK)
        if cfg.sinks:
            @pl.when((r == 0) & (i == 0))
            def _():
                dsink_ref[...] = jnp.zeros_like(dsink_ref)

        shared = {}
        for p in range(cfg.heads // 2):
            lanes = slice(p * LANES, (p + 1) * LANES)
            ki = _pair_source(p, grp)
            hs = (2 * p, 2 * p + 1)
            biases = [[_masked_bias(m, dd, cfg, h, d) for h in hs]
                      for m, dd in ((valid_first, distf), (valid, distf), (valid_next, distf[:, 0:BLOCK]))]
            tiles = []
            for a in range(nsub + 1):
                if a < nsub:
                    rows, win = slice(a * BLOCK, (a + 1) * BLOCK), slice(a * BLOCK, (a + 2) * BLOCK)
                    src = (q_ref, o_ref, do_ref, l_ref, dl_ref)
                else:
                    rows, win = slice(0, BLOCK), slice(nsub * BLOCK, (nsub + 1) * BLOCK)
                    src = (qn_ref, on_ref, don_ref, ln_ref, dln_ref)
                q2 = src[0][rows, lanes] * SCALE
                do2 = src[2][rows, lanes]
                o2 = src[1][rows, lanes].astype(F32)
                l2 = src[3][rows, lanes]
                k2, v2 = kf[ki, win, :], vf[ki, win, :]
                per = []
                for e in range(2):
                    qe, doe = jnp.where(half_q[e], q2, 0), jnp.where(half_q[e], do2, 0)
                    delta = jnp.sum(doe.astype(F32) * o2, axis=1, keepdims=True)
                    lse_v = jnp.max(jnp.where(half_q[e], l2, NEG_INF), axis=1, keepdims=True)
                    shift = -delta
                    if has_dl:
                        shift = shift + jnp.max(jnp.where(half_q[e], src[4][rows, lanes], NEG_INF), axis=1,
                                                keepdims=True)
                    s = _dot_nt(qe, k2) + biases[0 if a == 0 else (1 if a < nsub else 2)][e]
                    per.append((qe, doe, delta, lse_v, shift, s, _dot_nt(doe, v2)))
                tiles.append((k2, per))
            grads = []
            for k2, per in tiles:
                both = []
                for qe, doe, delta, lse_v, shift, s, dp in per:
                    pr = jnp.exp(s - lse_v)
                    both.append(((pr * (dp + shift)).astype(BF16), pr.astype(BF16)))
                grads.append(both)
            dkt, dvt = [], []
            for a, ((k2, per), both) in enumerate(zip(tiles, grads)):
                if a < nsub:
                    half_k = _lane_halves(k2.shape[0])
                    ds_cat = jnp.concatenate([both[0][0], both[1][0]], axis=1)
                    k_cat = jnp.concatenate([jnp.where(half_k[e], k2, 0) for e in range(2)], axis=0)
                    out_ref[a * BLOCK:(a + 1) * BLOCK, lanes] = (_dot(ds_cat, k_cat) * SCALE).astype(BF16)
                cut = (lambda x: x[:, BLOCK:]) if a == 0 else (lambda x: x)
                q_cat = jnp.concatenate([per[0][0], per[1][0]], axis=0)
                do_cat = jnp.concatenate([per[0][1], per[1][1]], axis=0)
                dkt.append(_dot_tn(q_cat, jnp.concatenate([cut(both[0][0]), cut(both[1][0])], axis=0)))
                dvt.append(_dot_tn(do_cat, jnp.concatenate([cut(both[0][1]), cut(both[1][1])], axis=0)))
                if cfg.sinks and a < nsub:
                    for e in range(2):
                        psink = jnp.exp(sink_ref[hs[e]] - per[e][3])
                        tot = jnp.sum(psink * (-per[e][2]), axis=0, keepdims=True)
                        dsink_ref[hs[e]:hs[e] + 1, :] += jnp.broadcast_to(tot, (1, LANES))
            for m in range(nsub):
                rows = slice(m * BLOCK, (m + 1) * BLOCK)
                for which, (acc, col0) in enumerate(((dkt, wq), (dvt, wq + wk))):
                    own = acc[m] if m == 0 else acc[m][:, BLOCK:]
                    total = own + acc[m + 1][:, 0:BLOCK]
                    if grp == 1:
                        out_ref[rows, col0 + p * LANES:col0 + (p + 1) * LANES] = total.T.astype(BF16)
                    else:
                        t64 = total[0:HEAD_DIM] + total[HEAD_DIM:]
                        key = (ki, which, m)
                        shared[key] = t64 + shared[key] if key in shared else t64
        for (ki, which, m), t64 in shared.items():
            col0 = (wq, wq + wk)[which] + ki * HEAD_DIM
            out_ref[m * BLOCK:(m + 1) * BLOCK, col0:col0 + HEAD_DIM] = t64.T.astype(BF16)

    prev = lambda i: jnp.maximum(i * nsub - 1, 0)
    nxt = lambda i: jnp.minimum((i + 1) * nsub, n // BLOCK - 1)
    cur = lambda w, c: pl.BlockSpec((None, tq, w), lambda r, i: (r, i, c // w))
    prv = lambda w, c: pl.BlockSpec((None, BLOCK, w), lambda r, i: (r, prev(i), c // w))
    o_cur = pl.BlockSpec((None, tq, wq), lambda r, i: (r, i, 0))
    o_nxt = pl.BlockSpec((None, BLOCK, wq), lambda r, i: (r, nxt(i), 0))
    in_specs = [pl.BlockSpec(memory_space=pltpu.SMEM),
                cur(wq, cfg.qc), pl.BlockSpec((None, BLOCK, wq), lambda r, i: (r, nxt(i), cfg.qc // wq)),
                cur(wk, cfg.kc), prv(wk, cfg.kc), cur(wk, cfg.vc), prv(wk, cfg.vc),
                o_cur, o_nxt, o_cur, o_nxt, o_cur, o_nxt]
    args = [sinks, qkv, qkv, qkv, qkv, qkv, qkv, o, o, do, do, lse, lse]
    if has_dl:
        in_specs += [o_cur, o_nxt]
        args += [dlse, dlse]
    out_shape = [jax.ShapeDtypeStruct((d, n, wout), BF16)]
    out_specs = [pl.BlockSpec((None, tq, wout), lambda r, i: (r, i, 0))]
    if cfg.sinks:
        out_shape.append(jax.ShapeDtypeStruct((8, LANES), F32))
        out_specs.append(pl.BlockSpec((8, LANES), lambda r, i: (0, 0)))
    pair_scratch = pltpu.VMEM((_n_pair_sources(cfg), tq + BLOCK, LANES), BF16)
    return pl.pallas_call(
        body, name=name, grid=(d, nqb), out_shape=out_shape, in_specs=in_specs, out_specs=out_specs,
        scratch_shapes=[pair_scratch, pair_scratch],
        compiler_params=_cp("arbitrary", "arbitrary"),
    )(*args)


def _adamw(g, w, m, v):
    m = ADAM_B1 * m + (1.0 - ADAM_B1) * g
    v = ADAM_B2 * v + (1.0 - ADAM_B2) * (g * g)
    m_hat = m / (1.0 - ADAM_B1 ** ADAM_STEP)
    v_hat = v / (1.0 - ADAM_B2 ** ADAM_STEP)
    delta = -ADAM_LR * (m_hat / (jnp.sqrt(v_hat) + ADAM_EPS) + ADAM_WD * w)
    return delta, m, v


def adam_reduce(parts, w, m, v, name):
    r, c = w.shape
    tr = next(cand for cand in (256, 128, 64, 32, 16, 8) if r % cand == 0) if r > 256 else r

    def body(p_ref, w_ref, m_ref, v_ref, g_ref, d_ref, mo_ref, vo_ref):
        g = p_ref[0].astype(F32)
        for j in range(1, N_DEV):
            g = g + p_ref[j].astype(F32)
        g_ref[...] = g
        d_ref[...], mo_ref[...], vo_ref[...] = _adamw(g, w_ref[...], m_ref[...], v_ref[...])

    row = pl.BlockSpec((tr, c), lambda i: (i, 0))
    return pl.pallas_call(
        body, name=name, grid=(r // tr,),
        out_shape=[jax.ShapeDtypeStruct((r, c), F32)] * 4,
        in_specs=[pl.BlockSpec((N_DEV, tr, c), lambda i: (0, i, 0)), row, row, row],
        out_specs=[row] * 4,
        compiler_params=_cp("parallel"),
    )(parts, w, m, v)


def adam_layers(parts, w, m, v, name):
    nl, r, c = w.shape
    tr = next(cand for cand in (256, 128, 64, 32, 16, 8) if r % cand == 0)
    steps = r // tr

    def body(*refs):
        p_refs = refs[:nl]
        w_ref, m_ref, v_ref, g_ref, d_ref, mo_ref, vo_ref = refs[nl:]
        for k in range(nl):
            @pl.when(pl.program_id(0) == k)
            def _():
                g = p_refs[k][0].astype(F32)
                for j in range(1, N_DEV):
                    g = g + p_refs[k][j].astype(F32)
                g_ref[...] = g
                d_ref[...], mo_ref[...], vo_ref[...] = _adamw(g, w_ref[...], m_ref[...], v_ref[...])

    def part_spec(k):
        return pl.BlockSpec((N_DEV, tr, c), lambda l, i: (0, jnp.clip(i + (l - k) * steps, 0, steps - 1), 0))

    blk = pl.BlockSpec((None, tr, c), lambda l, i: (l, i, 0))
    return pl.pallas_call(
        body, name=name, grid=(nl, steps),
        out_shape=[jax.ShapeDtypeStruct((nl, r, c), F32)] * 4,
        in_specs=[part_spec(k) for k in range(nl)] + [blk, blk, blk],
        out_specs=[blk] * 4,
        compiler_params=_cp("arbitrary", "arbitrary"),
    )(*parts, w, m, v)


def adam_w_ada(sct, dm_loc, w, m, v):
    nl, dm, wc = w.shape
    tr = 512

    def body(s_ref, d_ref, w_ref, m_ref, v_ref, g_ref, dl_ref, mo_ref, vo_ref):
        g = jnp.dot(s_ref[...], d_ref[...], preferred_element_type=F32, precision=lax.Precision.HIGHEST)
        g_ref[...] = g
        dl_ref[...], mo_ref[...], vo_ref[...] = _adamw(g, w_ref[...], m_ref[...], v_ref[...])

    blk = pl.BlockSpec((None, tr, wc), lambda l, i: (l, i, 0))
    return pl.pallas_call(
        body, name="adam_w_ada", grid=(nl, dm // tr),
        out_shape=[jax.ShapeDtypeStruct(w.shape, F32)] * 4,
        in_specs=[pl.BlockSpec((tr, LANES), lambda l, i: (i, 0)),
                  pl.BlockSpec((None, LANES, wc), lambda l, i: (l, 0, 0)), blk, blk, blk],
        out_specs=[blk] * 4,
        compiler_params=_cp("parallel", "parallel"),
    )(sct, dm_loc, w, m, v)


TRANSPOSED = ("w_gate", "w_up")


def _pieces(dm):
    ncol = lambda n: n // N_DEV
    mixer = ([Piece("w_in", "w_in", 1, 0, ncol(GATE_COL + 2 * dm)),
              Piece("w_a", "w_a", 1, 0, ncol(dm)),
              Piece("w_b", "w_b", 1, 0, ncol(dm)),
              Piece("w_o", "w_o", 0, 0, ncol(dm))],
             {"w_in": (dm, GATE_COL + 2 * dm), "w_a": (A_Q_HEADS * HEAD_DIM, dm), "w_b": (B_OUT_W, dm),
              "w_o": (dm, dm)})
    ffn = ([Piece("w_gate", "w_ffn_t", 0, 0, ncol(D_FF)),
            Piece("w_up", "w_ffn_t", 0, D_FF, ncol(D_FF)),
            Piece("w_down", "w_down", 0, 0, ncol(D_FF))],
           {"w_ffn_t": (2 * D_FF, dm), "w_down": (D_FF, dm)})
    return mixer, ffn


def kernel(x, c, w_ada, b_ada, w_in, sinks, w_a, w_b, w_o, ln1_g, ln1_b, w_gate, w_up, w_down, ln2_g, ln2_b, loss_target, m_w_ada, m_b_ada, m_w_in, m_sinks, m_w_a, m_w_b, m_w_o, m_ln1_g, m_ln1_b, m_w_gate, m_w_up, m_w_down, m_ln2_g, m_ln2_b, v_w_ada, v_b_ada, v_w_in, v_sinks, v_w_a, v_w_b, v_w_o, v_ln1_g, v_ln1_b, v_w_gate, v_w_up, v_w_down, v_ln2_g, v_ln2_b):
    given = dict(locals())
    nl = w_in.shape[0]
    t, dm = x.shape[1], x.shape[2]
    me = 4 * lax.axis_index("x") + 2 * lax.axis_index("y") + lax.axis_index("c")
    x0 = x.reshape(t, dm)
    target = loss_target.reshape(t, dm)

    groups = dict(zip(("mixer", "ffn"), _pieces(dm)))
    local = lambda nm, pre="": (given[pre + nm].transpose(0, 2, 1) if nm in TRANSPOSED else given[pre + nm])
    shards = {pc.name: local(pc.name).astype(BF16) for pcs, _ in groups.values() for pc in pcs}

    def gather(group, l):
        pcs, bufs = groups[group]
        return Exchange("gather", pcs, [shards[pc.name][l] for pc in pcs], bufs.values(), bufs)

    def scatter(group, gbuf):
        pcs, bufs = groups[group]
        return Exchange("scatter", pcs, [gbuf[nm] for nm in bufs],
                        [(N_DEV,) + shards[pc.name].shape[1:] for pc in pcs], bufs)

    full = [dict() for _ in range(nl)]
    full[0].update(zip(groups["mixer"][1], run_exchange(gather("mixer", 0), "gather_mixer")))

    wc = w_ada.shape[2]
    c_all = all_gather_small(jnp.broadcast_to(c, (8, dm)), "gather_c")[:, 0, :]
    b_loc = lax.dynamic_slice_in_dim(b_ada, me * wc, wc, axis=1).reshape(nl, 1, wc)
    mp, sc_all = mod_partial(c_all, w_ada, b_loc)
    mp_all = all_gather_small(mp.reshape(nl * N_DEV, wc), "gather_mod").reshape(N_DEV, nl, N_DEV, wc)
    mod = lax.dynamic_index_in_dim(mp_all, me, axis=2, keepdims=False)
    mod = mod.transpose(1, 0, 2).reshape(nl, 6, 1, dm)

    vec = lambda a, l: a[l].reshape(1, dm)

    saved = []
    xl = x0
    for l in range(nl):
        sh1, s1, g1, sh2, s2, g2 = [mod[l, j] for j in range(6)]
        w = full[l]
        (u1, u1_f4, u1_f16, proj, qkv_f4, qkv_f16), got = in_proj(xl, s1, sh1, w["w_in"], "in_proj",
                                                                   gather("ffn", l))
        w.update(zip(groups["ffn"][1], got))
        proj3 = proj.reshape(1, t, proj.shape[1])
        qkv_b = [proj3, qkv_f4, qkv_f16]
        ya, lse_a = attn_fwd(proj3, ATTN_A, sinks[l], "attn_a_fwd")
        o_g, l_g = [], []
        for g, cfg in enumerate(ATTN_B):
            o, ls = attn_fwd(qkv_b[g], cfg, sinks[l], "attn_b%d_fwd" % g)
            o_g.append(o)
            l_g.append(ls)
        yb, merged = mix_merge(ya[0], o_g, l_g, proj, w["w_a"], w["w_b"], "mix_merge")
        y1, x1, zh1, rs1 = proj_ln(merged, w["w_o"], xl, g1, vec(ln1_g, l), vec(ln1_b, l), "out_proj_ln")
        (u2, ab), got = modmm(x1, s2, sh2, w["w_ffn_t"].T, "ffn_up", gather("mixer", l + 1) if l + 1 < nl else None)
        if l + 1 < nl:
            full[l + 1].update(zip(groups["mixer"][1], got))
        h, y2, x2, zh2, rs2 = swiglu_proj_ln(ab, w["w_down"], x1, g2, vec(ln2_g, l), vec(ln2_b, l), "ffn_down_ln")
        saved.append(dict(xin=xl, u1=[u1, u1_f4.reshape(t, dm), u1_f16.reshape(t, dm)], proj=proj, qkv_b=qkv_b,
                          ya=ya, lse_a=lse_a, o_g=o_g, l_g=l_g, yb=yb, merged=merged,
                          y1=y1, x1=x1, zh1=zh1, rs1=rs1, u2=u2, ab=ab, h=h, y2=y2, zh2=zh2, rs2=rs2))
        xl = x2

    dx, loss_part = loss_head(xl, target)

    small = {k: [None] * nl for k in ("dmod", "ln1_g", "ln1_b", "ln2_g", "ln2_b", "sinks")}
    recv = {nm: [None] * nl for grp in groups.values() for nm in (pc.name for pc in grp[0])}

    def keep(group, l, got):
        for pc, arr in zip(groups[group][0], got):
            recv[pc.name][l] = arr

    for l in reversed(range(nl)):
        sv, w = saved[l], full[l]
        sh1, s1, g1, sh2, s2, g2 = [mod[l, j] for j in range(6)]
        fresh = lambda nm: lax.empty({**groups["mixer"][1], **groups["ffn"][1]}[nm], BF16)
        gbuf = {}
        dz2, dy2, sg, sb, sgate2, gbuf["w_down"] = ln_bwd(dx, sv["zh2"], sv["rs2"], sv["y2"], vec(ln2_g, l), g2,
                                                          sv["h"], "ln_bwd_ffn")
        small["ln2_g"][l], small["ln2_b"][l] = sg[0], sb[0]
        dab = dswiglu(dy2, w["w_down"].T, sv["ab"], "dswiglu")
        gbuf["w_ffn_t"] = wgrad(dab, sv["u2"], fresh("w_ffn_t"), 512, dm // 512, 0, 0, 1, "wgrad_ffn_up")
        (dx1, ss2, ssh2), got = dgrad_ffn(dab, w["w_ffn_t"], dz2, sv["x1"], s2, "dgrad_ffn", scatter("ffn", gbuf))
        keep("ffn", l, got)
        dz1, do1, sg, sb, sgate1, gbuf["w_o"] = ln_bwd(dx1, sv["zh1"], sv["rs1"], sv["y1"], vec(ln1_g, l), g1,
                                                       sv["merged"], "ln_bwd_mixer")
        small["ln1_g"][l], small["ln1_b"][l] = sg[0], sb[0]
        dya, dyb, dgab, gbuf["w_a"], gbuf["w_b"] = dmerge(
            do1, w["w_o"].T, sv["ya"][0], sv["yb"], w["w_a"], w["w_b"], w["w_a"].T, w["w_b"].T, sv["proj"], "dmerge")
        mixed = mix_bwd(dyb, sv["o_g"], sv["l_g"], "mix_bwd")
        do_g, dl_g = mixed[:N_GROUPS], mixed[N_GROUPS:]
        d_a, dsink = attn_bwd(sv["qkv_b"][0], sv["ya"], sv["lse_a"], dya.reshape(1, t, -1), None, ATTN_A,
                              sinks[l], "attn_a_bwd")
        small["sinks"][l] = dsink[:, 0]
        d_b = [attn_bwd(sv["qkv_b"][g], sv["o_g"][g], sv["l_g"][g], do_g[g], dl_g[g], cfg, sinks[l],
                        "attn_b%d_bwd" % g)[0] for g, cfg in enumerate(ATTN_B)]
        gw = wgrad(sv["u1"][0], d_a.reshape(t, A_W), fresh("w_in"), A_W, 1, 0, 0, 1, "wgrad_in_a")
        for g in range(N_GROUPS):
            gw = wgrad(sv["u1"][g], d_b[g].reshape(t, B_GW), gw, B_OUT_W, 3, 0, A_W // B_OUT_W + g, N_GROUPS,
                       "wgrad_in_b%d" % g)
        gbuf["w_in"] = wgrad(sv["u1"][0], dgab, gw, 512, 2 * dm // 512, 0, GATE_COL // 512, 1, "wgrad_in_gate")
        (dx, ss1, ssh1), got = dgrad_in(d_a, d_b, dgab, w["w_in"].T, dz1, sv["xin"], s1, "dgrad_in",
                                        scatter("mixer", gbuf))
        keep("mixer", l, got)
        small["dmod"][l] = jnp.stack([ssh1[0], ss1[0], sgate1[0], ssh2[0], ss2[0], sgate2[0]])
    grad_x = dx.reshape(x.shape)

    big_out = {}
    for nm, parts in recv.items():
        outs = adam_layers(parts, local(nm), local(nm, "m_"), local(nm, "v_"), "adam_" + nm)
        big_out[nm] = [o.transpose(0, 2, 1) for o in outs] if nm in TRANSPOSED else outs

    rows = jnp.concatenate(
        [jnp.stack(small["dmod"]).reshape(nl * 6, dm)]
        + [jnp.stack(small[k]) for k in ("ln1_g", "ln1_b", "ln2_g", "ln2_b")]
        + [jnp.pad(jnp.stack(small["sinks"]).reshape(1, -1), ((0, 0), (0, dm - nl * A_Q_HEADS))),
           jnp.broadcast_to(loss_part[0:1, 0:1], (1, dm))])
    n_rows = rows.shape[0]
    rows = jnp.pad(rows, ((0, -n_rows % 8), (0, 0)))
    rows_all = all_gather_small(rows, "gather_small_grads")

    def pack_small(pre):
        parts = [given[pre + "b_ada"].reshape(nl * 6, dm)]
        parts += [given[pre + k] for k in ("ln1_g", "ln1_b", "ln2_g", "ln2_b")]
        parts.append(jnp.pad(given[pre + "sinks"].reshape(1, -1), ((0, 0), (0, dm - nl * A_Q_HEADS))))
        p = jnp.concatenate(parts)
        return jnp.pad(p, ((0, rows.shape[0] - p.shape[0]), (0, 0)))

    souts = adam_reduce(rows_all, pack_small(""), pack_small("m_"), pack_small("v_"), "adam_small")

    def unpack_small(o):
        r = {"b_ada": o[0:nl * 6].reshape(nl, 6 * dm)}
        for j, k in enumerate(("ln1_g", "ln1_b", "ln2_g", "ln2_b")):
            r[k] = o[nl * 6 + j * nl: nl * 6 + (j + 1) * nl]
        r["sinks"] = o[nl * 10, 0:nl * A_Q_HEADS].reshape(nl, A_Q_HEADS)
        return r

    small_out = [unpack_small(o) for o in souts]
    loss = souts[0][nl * 10 + 1, 0]

    dmod_all = rows_all[:, 0:nl * 6].reshape(N_DEV, nl, 6 * dm)
    dm_loc = lax.dynamic_slice_in_dim(dmod_all, me * wc, wc, axis=2).transpose(1, 0, 2)
    dm_loc = jnp.pad(dm_loc, ((0, 0), (0, LANES - N_DEV), (0, 0)))
    sct = jnp.pad(sc_all.T, ((0, 0), (0, LANES - N_DEV)))
    ada_out = adam_w_ada(sct, dm_loc, w_ada, m_w_ada, v_w_ada)

    names = ["w_ada", "b_ada", "w_in", "sinks", "w_a", "w_b", "w_o", "ln1_g", "ln1_b",
             "w_gate", "w_up", "w_down", "ln2_g", "ln2_b"]

    def pick(kind, nm):
        if nm == "w_ada":
            return ada_out[kind]
        if nm in small_out[kind]:
            return small_out[kind][nm]
        return big_out[nm][kind]

    result = [loss, grad_x]
    for kind in range(4):
        result += [pick(kind, nm) for nm in names]
    return tuple(result)
```

```python
import jax
import jax.numpy as jnp
from jax import lax
from jax.experimental import pallas as pl
from jax.experimental.pallas import tpu as pltpu

F32 = jnp.float32
BF16 = jnp.bfloat16

D_MODEL = 1024
HEAD_DIM = 64
A_Q_HEADS = 8
A_KV_HEADS = 2
A_WINDOW = 128
B_GROUPS = ((128, 1), (512, 4), (2048, 16))
N_GROUPS = len(B_GROUPS)
B_HEADS_PER_GROUP = 4
N_ATTN_HEADS = A_Q_HEADS + B_HEADS_PER_GROUP * N_GROUPS
BLOCK = 128
ATTN_QUERY_BLOCK = 1024
A_W =(A_Q_HEADS + 2 * A_KV_HEADS) * HEAD_DIM
B_OUT_W = B_HEADS_PER_GROUP * HEAD_DIM
B_GW = 3 * B_OUT_W
B_ALL = N_GROUPS * B_OUT_W
GATE_COL = A_W + 3 * B_ALL
D_FF = 2816
FF_CHUNK = 256
DGRAD_CHUNK = 256
DN_ALPHA = 8.0 ** 0.25
LN_EPS = 1e-5
NEG_INF = -1e30
ADAM_LR, ADAM_B1, ADAM_B2, ADAM_EPS, ADAM_WD, ADAM_STEP = 0.001, 0.9, 0.999, 1e-08, 0.01, 10

N_DEV = 8
MESH = pl.DeviceIdType.MESH
VMEM_LIMIT = 56 * 1024 * 1024
ROW_TILE = 512
WGRAD_TILE_ELEMS = 2 * 1024 * 1024
LANES = 128
BF16_ROWS = 16


def _cp(*sem):
    return pltpu.CompilerParams(dimension_semantics=sem, vmem_limit_bytes=VMEM_LIMIT)


def _row_tile(t):
    return min(ROW_TILE, t)


def _slope(head):
    return 2.0 ** (-8.0 * (head + 1) / N_ATTN_HEADS)


def _sigmoid(x):
    return 1.0 / (1.0 + jnp.exp(-x))


def _dot(a, b):
    return jnp.dot(a, b, preferred_element_type=F32)


def _dot_nt(a, b):
    return lax.dot_general(a, b, (((1,), (1,)), ((), ())), preferred_element_type=F32)


def _dot_tn(a, b):
    return lax.dot_general(a, b, (((0,), (0,)), ((), ())), preferred_element_type=F32)


def _fold_scratch(tm, w):
    return [pltpu.VMEM((tm, LANES), F32)] * (w // LANES)


def _fold_to(dst_ref, val, scrs, d, col0=0):
    tm, w = val.shape
    if d == 1:
        dst_ref[0, :, col0:col0 + w] = val.astype(dst_ref.dtype)
        return
    for cb in range(w // LANES):
        scrs[cb][...] = val[:, cb * LANES:(cb + 1) * LANES]
    for r in range(d):
        for cb in range(w // LANES):
            piece = scrs[cb][pl.ds(r, tm // d, stride=d), :]
            dst_ref[r, :, col0 + cb * LANES:col0 + (cb + 1) * LANES] = piece.astype(dst_ref.dtype)


def _unfold_rows(rows_of, scrs, d, n, w):
    for r in range(d):
        for cb in range(w // LANES):
            scrs[cb][pl.ds(r, n, stride=d), :] = rows_of(r, slice(cb * LANES, (cb + 1) * LANES)).astype(F32)
    return jnp.concatenate([scrs[cb][0:d * n, :] for cb in range(w // LANES)], axis=1)


def _unfold_from(src_ref, scrs, d):
    if d == 1:
        return src_ref[0].astype(F32)
    _, n, w = src_ref.shape
    return _unfold_rows(lambda r, cols: src_ref[r, :, cols], scrs, d, n, w)


def _folded_spec(d, tm, w):
    return pl.BlockSpec((d, tm // d, w), lambda i: (0, i, 0))


def _me():
    return lax.axis_index("x"), lax.axis_index("y"), lax.axis_index("c")


def _flip(v, bit):
    return 1 - v if bit else v


def _peer(k):
    x, y, c = _me()
    return (_flip(x, k & 4), _flip(y, k & 2), _flip(c, k & 1))


def _peer_index(k):
    px, py, pc = _peer(k)
    return 4 * px + 2 * py + pc


def all_gather_small(v, name):
    r, c = v.shape

    def body(v_ref, out_ref, send_sems, recv_sems):
        me = _peer_index(0)
        out_ref[me] = v_ref[...]
        copies = []
        for k in range(1, N_DEV):
            cp = pltpu.make_async_remote_copy(
                src_ref=v_ref, dst_ref=out_ref.at[me],
                send_sem=send_sems.at[k - 1], recv_sem=recv_sems.at[k - 1],
                device_id=_peer(k), device_id_type=MESH)
            cp.start()
            copies.append(cp)
        for k in range(1, N_DEV):
            pltpu.make_async_remote_copy(
                src_ref=v_ref, dst_ref=out_ref.at[_peer_index(k)],
                send_sem=send_sems.at[k - 1], recv_sem=recv_sems.at[k - 1],
                device_id=_peer(k), device_id_type=MESH).wait_recv()
        for cp in copies:
            cp.wait_send()

    return pl.pallas_call(
        body, name=name,
        out_shape=jax.ShapeDtypeStruct((N_DEV, r, c), v.dtype),
        in_specs=[pl.BlockSpec(memory_space=pltpu.VMEM)],
        out_specs=pl.BlockSpec(memory_space=pltpu.VMEM),
        scratch_shapes=[pltpu.SemaphoreType.DMA((N_DEV - 1,)), pltpu.SemaphoreType.DMA((N_DEV - 1,))],
        compiler_params=pltpu.CompilerParams(vmem_limit_bytes=VMEM_LIMIT),
    )(v)


class Piece:
    def __init__(self, name, buf, axis, base, size):
        self.name, self.buf, self.axis, self.base, self.size = name, buf, axis, base, size

    def window(self, ref, j):
        start = self.base + j * self.size
        if self.axis == 1:
            return ref.at[:, pl.ds(pl.multiple_of(start, LANES), self.size)]
        return ref.at[pl.ds(pl.multiple_of(start, BF16_ROWS), self.size), :]


class Exchange:
    def __init__(self, kind, pieces, ins, out_shapes, bufs):
        self.kind, self.pieces, self.ins, self.out_shapes = kind, pieces, list(ins), list(out_shapes)
        self.buf_of = {nm: i for i, nm in enumerate(bufs)}
        self.n_in, self.n_out = len(self.ins), len(self.out_shapes)
        n = len(pieces)
        self.scratch = [pltpu.SemaphoreType.DMA((n, N_DEV - 1)), pltpu.SemaphoreType.DMA((n, N_DEV - 1)),
                        pltpu.SemaphoreType.DMA((n,))]
        self.in_specs = [pl.BlockSpec(memory_space=pl.ANY)] * self.n_in
        self.out_specs = [pl.BlockSpec(memory_space=pl.ANY)] * self.n_out
        self.out_shape = [jax.ShapeDtypeStruct(s, BF16) for s in self.out_shapes]

    def _ends(self, pi, ins, outs, to):
        pc = self.pieces[pi]
        if self.kind == "gather":
            return ins[pi], pc.window(outs[self.buf_of[pc.buf]], _peer_index(0))
        return pc.window(ins[self.buf_of[pc.buf]], to), outs[pi].at[_peer_index(0)]

    def _landing(self, pi, outs, frm):
        pc = self.pieces[pi]
        if self.kind == "gather":
            return pc.window(outs[self.buf_of[pc.buf]], frm)
        return outs[pi].at[frm]

    def _remote(self, pi, k, src, dst, sems):
        return pltpu.make_async_remote_copy(
            src_ref=src, dst_ref=dst, send_sem=sems[0].at[pi, k - 1], recv_sem=sems[1].at[pi, k - 1],
            device_id=_peer(k), device_id_type=MESH)

    def _local(self, pi, ins, outs, sems):
        return pltpu.make_async_copy(*self._ends(pi, ins, outs, _peer_index(0)), sems[2].at[pi])

    def start(self, ins, outs, sems):
        for pi in range(len(self.pieces)):
            self._local(pi, ins, outs, sems).start()
            for k in range(1, N_DEV):
                self._remote(pi, k, *self._ends(pi, ins, outs, _peer_index(k)), sems).start()

    def finish(self, ins, outs, sems):
        for pi in range(len(self.pieces)):
            src_like = self._ends(pi, ins, outs, _peer_index(0))[0]
            for k in range(1, N_DEV):
                self._remote(pi, k, src_like, self._landing(pi, outs, _peer_index(k)), sems).wait_recv()
        for pi in range(len(self.pieces)):
            for k in range(1, N_DEV):
                self._remote(pi, k, *self._ends(pi, ins, outs, _peer_index(k)), sems).wait_send()
            self._local(pi, ins, outs, sems).wait()


def _hosted(ex, refs, n_in, n_out, first, last):
    if ex is None:
        return refs
    ins, rest = refs[:n_in], refs[n_in:]
    ex_ins, rest = rest[:ex.n_in], rest[ex.n_in:]
    outs, rest = rest[:n_out], rest[n_out:]
    ex_outs, rest = rest[:ex.n_out], rest[ex.n_out:]
    scr, sems = rest[:len(rest) - 3], rest[len(rest) - 3:]
    pl.when(first)(lambda: ex.start(ex_ins, ex_outs, sems))
    pl.when(last)(lambda: ex.finish(ex_ins, ex_outs, sems))
    return tuple(ins) + tuple(outs) + tuple(scr)


def _host_call(body, ex, *, name, grid, out_shape, in_specs, out_specs, scratch_shapes=(), sem=None, args):
    n_out = len(out_shape)
    if ex is not None:
        out_shape = list(out_shape) + ex.out_shape
        in_specs = list(in_specs) + ex.in_specs
        out_specs = list(out_specs) + ex.out_specs
        scratch_shapes = list(scratch_shapes) + ex.scratch
        args = list(args) + ex.ins
    res = pl.pallas_call(body, name=name, grid=grid, out_shape=out_shape, in_specs=in_specs, out_specs=out_specs,
                         scratch_shapes=scratch_shapes, compiler_params=_cp(*sem))(*args)
    return res[:n_out], res[n_out:]


def run_exchange(ex, name):
    def body(*refs):
        ins, outs, sems = refs[:ex.n_in], refs[ex.n_in:ex.n_in + ex.n_out], refs[ex.n_in + ex.n_out:]
        ex.start(ins, outs, sems)
        ex.finish(ins, outs, sems)

    return pl.pallas_call(body, name=name, out_shape=ex.out_shape, in_specs=ex.in_specs, out_specs=ex.out_specs,
                          scratch_shapes=ex.scratch)(*ex.ins)


def mod_partial(c_all, w_ada, b_loc):
    nl, dm, wc = w_ada.shape

    def body(c_ref, w_ref, b_ref, o_ref, sc_ref):
        cc = c_ref[...]
        sc = cc * _sigmoid(cc)
        sc_ref[...] = sc
        o_ref[...] = jnp.dot(sc, w_ref[...], preferred_element_type=F32,
                             precision=lax.Precision.HIGHEST) + b_ref[...]

    return pl.pallas_call(
        body, name="mod_partial", grid=(nl,),
        out_shape=[jax.ShapeDtypeStruct((nl, N_DEV, wc), F32), jax.ShapeDtypeStruct((N_DEV, dm), F32)],
        in_specs=[pl.BlockSpec((N_DEV, dm), lambda l: (0, 0)),
                  pl.BlockSpec((None, dm, wc), lambda l: (l, 0, 0)),
                  pl.BlockSpec((None, 1, wc), lambda l: (l, 0, 0))],
        out_specs=[pl.BlockSpec((None, N_DEV, wc), lambda l: (l, 0, 0)),
                   pl.BlockSpec((N_DEV, dm), lambda l: (0, 0))],
        compiler_params=_cp("arbitrary"),
    )(c_all, w_ada, b_loc)


def _stream_specs(tm, dm):
    vec = pl.BlockSpec((1, dm), lambda i: (0, 0))
    return [pl.BlockSpec((tm, dm), lambda i: (i, 0)), vec, vec]


def _stream(zh_ref, lg_ref, lb_ref, rows=slice(None), cols=slice(None)):
    return zh_ref[rows, cols] * lg_ref[:, cols] + lb_ref[:, cols]


def in_proj(x, s, sh, w, name, ex=None):
    t, dm = x[0].shape
    n = w.shape[1]
    tm = _row_tile(t)
    nsteps = t // tm
    ch = B_OUT_W
    dils = [dil for _, dil in B_GROUPS if dil > 1]

    def body(*refs):
        i = pl.program_id(0)
        zh_ref, lg_ref, lb_ref, s_ref, sh_ref, w_ref, u_ref, *rest = _hosted(
            ex, refs, 6, 2 + 2 * len(dils), i == 0, i == nsteps - 1)
        uf_refs, o_ref, qf_refs = rest[:len(dils)], rest[len(dils)], rest[len(dils) + 1:len(dils) * 2 + 1]
        scrs = rest[len(dils) * 2 + 1:]
        uf = _stream(zh_ref, lg_ref, lb_ref) * (1.0 + s_ref[...]) + sh_ref[...]
        u = uf.astype(BF16)
        u_ref[...] = u
        for d, uf_ref in zip(dils, uf_refs):
            _fold_to(uf_ref, uf, scrs, d)
        for c0 in range(0, n, ch):
            res = _dot(u, w_ref[:, c0:c0 + ch])
            o_ref[:, c0:c0 + ch] = res.astype(BF16)
            if A_W <= c0 < GATE_COL:
                part, g = divmod((c0 - A_W) // ch, N_GROUPS)
                d = B_GROUPS[g][1]
                if d > 1:
                    _fold_to(qf_refs[dils.index(d)], res, scrs, d, part * ch)

    vec = pl.BlockSpec((1, dm), lambda i: (0, 0))
    row = lambda w_: pl.BlockSpec((tm, w_), lambda i: (i, 0))
    return _host_call(
        body, ex, name=name, grid=(nsteps,),
        out_shape=[jax.ShapeDtypeStruct((t, dm), BF16)]
                  + [jax.ShapeDtypeStruct((d, t // d, dm), BF16) for d in dils]
                  + [jax.ShapeDtypeStruct((t, n), BF16)]
                  + [jax.ShapeDtypeStruct((d, t // d, B_GW), BF16) for d in dils],
        in_specs=_stream_specs(tm, dm) + [vec, vec, pl.BlockSpec((dm, n), lambda i: (0, 0))],
        out_specs=[row(dm)] + [_folded_spec(d, tm, dm) for d in dils] + [row(n)]
                  + [_folded_spec(d, tm, B_GW) for d in dils],
        scratch_shapes=_fold_scratch(tm, dm), sem=("arbitrary",), args=[*x, s, sh, w])


def modmm(x, s, sh, w, name, ex=None):
    t, dm = x[0].shape
    n = w.shape[1]
    tm = _row_tile(t)
    nsteps = t // tm
    ch = 512

    def body(*refs):
        i = pl.program_id(0)
        zh_ref, lg_ref, lb_ref, s_ref, sh_ref, w_ref, u_ref, o_ref = _hosted(ex, refs, 6, 2, i == 0, i == nsteps - 1)
        u = (_stream(zh_ref, lg_ref, lb_ref) * (1.0 + s_ref[...]) + sh_ref[...]).astype(BF16)
        u_ref[...] = u
        for c0 in range(0, n, ch):
            o_ref[:, c0:c0 + ch] = _dot(u, w_ref[:, c0:c0 + ch]).astype(BF16)

    vec = pl.BlockSpec((1, dm), lambda i: (0, 0))
    return _host_call(
        body, ex, name=name, grid=(nsteps,),
        out_shape=[jax.ShapeDtypeStruct((t, dm), BF16), jax.ShapeDtypeStruct((t, n), BF16)],
        in_specs=_stream_specs(tm, dm) + [vec, vec, pl.BlockSpec((dm, n), lambda i: (0, 0))],
        out_specs=[pl.BlockSpec((tm, dm), lambda i: (i, 0)), pl.BlockSpec((tm, n), lambda i: (i, 0))],
        sem=("arbitrary",), args=[*x, s, sh, w])


def _halves(tm):
    half = tm // 2 if tm % 32 == 0 else tm
    return [slice(r0, r0 + half) for r0 in range(0, tm, half)]


def _ln_store(y, rows, xres_refs, g_ref, y_ref, zh_ref, rs_ref):
    y_ref[rows, :] = y.astype(BF16)
    z = DN_ALPHA * _stream(*xres_refs, rows=rows) + g_ref[...] * y
    mu = jnp.mean(z, axis=1, keepdims=True)
    zc = z - mu
    var = jnp.mean(zc * zc, axis=1, keepdims=True)
    rstd = lax.rsqrt(var + LN_EPS)
    zh_ref[rows, :] = zc * rstd
    rs_ref[rows, :] = jnp.broadcast_to(rstd, (zc.shape[0], rs_ref.shape[1]))


def _ln_out_shapes(t, dm):
    return [jax.ShapeDtypeStruct((t, dm), BF16), jax.ShapeDtypeStruct((t, dm), F32),
            jax.ShapeDtypeStruct((t, LANES), F32)]


def _ln_out_specs(tm, dm):
    row = pl.BlockSpec((tm, dm), lambda i: (i, 0))
    return [row, row, pl.BlockSpec((tm, LANES), lambda i: (i, 0))]


def proj_ln(a, w, xres, gate, name):
    t, k = a.shape
    dm = w.shape[1]
    tm = _row_tile(t)

    def body(a_ref, w_ref, xz_ref, xg_ref, xb_ref, g_ref, y_ref, zh_ref, rs_ref):
        for rows in _halves(tm):
            y = _dot(a_ref[rows, :], w_ref[...])
            _ln_store(y, rows, (xz_ref, xg_ref, xb_ref), g_ref, y_ref, zh_ref, rs_ref)

    vec = pl.BlockSpec((1, dm), lambda i: (0, 0))
    return pl.pallas_call(
        body, name=name, grid=(t // tm,),
        out_shape=_ln_out_shapes(t, dm),
        in_specs=[pl.BlockSpec((tm, k), lambda i: (i, 0)), pl.BlockSpec((k, dm), lambda i: (0, 0))]
                 + _stream_specs(tm, dm) + [vec],
        out_specs=_ln_out_specs(tm, dm),
        compiler_params=_cp("parallel"),
    )(a, w, *xres, gate)


def swiglu_proj_ln(ab, w, xres, gate, name):
    t = ab.shape[0]
    f, dm = w.shape
    tm = _row_tile(t)

    def body(a_ref, b_ref, w_ref, xz_ref, xg_ref, xb_ref, g_ref, h_ref, y_ref, zh_ref, rs_ref):
        for rows in _halves(tm):
            y = None
            for c0 in range(0, f, FF_CHUNK):
                cols = slice(c0, c0 + FF_CHUNK)
                a = a_ref[rows, cols].astype(F32)
                h = (a * _sigmoid(a) * b_ref[rows, cols].astype(F32)).astype(BF16)
                h_ref[rows, cols] = h
                part = _dot(h, w_ref[cols, :])
                y = part if y is None else y + part
            _ln_store(y, rows, (xz_ref, xg_ref, xb_ref), g_ref, y_ref, zh_ref, rs_ref)

    vec = pl.BlockSpec((1, dm), lambda i: (0, 0))
    return pl.pallas_call(
        body, name=name, grid=(t // tm,),
        out_shape=[jax.ShapeDtypeStruct((t, f), BF16)] + _ln_out_shapes(t, dm),
        in_specs=[pl.BlockSpec((tm, f), lambda i: (i, 0)), pl.BlockSpec((tm, f), lambda i: (i, 1)),
                  pl.BlockSpec((f, dm), lambda i: (0, 0))] + _stream_specs(tm, dm) + [vec],
        out_specs=[pl.BlockSpec((tm, f), lambda i: (i, 0))] + _ln_out_specs(tm, dm),
        compiler_params=_cp("parallel"),
    )(ab, ab, w, *xres, gate)


class AttnCfg:
    def __init__(self, dil, heads, kv_heads, qc, kc, vc, max_dist, head0, sinks):
        self.dil, self.heads, self.kv_heads = dil, heads, kv_heads
        self.qc, self.kc, self.vc = qc, kc, vc
        self.max_dist, self.head0, self.sinks = max_dist, head0, sinks
        self.wq = heads * HEAD_DIM
        self.wk = kv_heads * HEAD_DIM
        self.wout = self.wq + 2 * self.wk


ATTN_A = AttnCfg(1, A_Q_HEADS, A_KV_HEADS, 0, A_Q_HEADS * HEAD_DIM, (A_Q_HEADS + A_KV_HEADS) * HEAD_DIM,
                 A_WINDOW - 1, 0, True)


def _attn_b_cfg(g):
    win, dil = B_GROUPS[g]
    cols = ((A_W + g * B_OUT_W, A_W + B_ALL + g * B_OUT_W, A_W + 2 * B_ALL + g * B_OUT_W) if dil == 1
            else (0, B_OUT_W, 2 * B_OUT_W))
    return AttnCfg(dil, B_HEADS_PER_GROUP, B_HEADS_PER_GROUP, *cols, win // dil,
                   A_Q_HEADS + g * B_HEADS_PER_GROUP, False)


ATTN_B = [_attn_b_cfg(g) for g in range(N_GROUPS)]


SCALE = HEAD_DIM ** -0.5


def _head(h):
    return slice(h * HEAD_DIM, (h + 1) * HEAD_DIM)


def _masked_bias(mask, distf, cfg, h, d):
    return jnp.where(mask, distf * (-(_slope(cfg.head0 + h) * d)), NEG_INF)


def _lane_halves(rows):
    lane = lax.broadcasted_iota(jnp.int32, (rows, LANES), 1)
    return [lane < HEAD_DIM, lane >= HEAD_DIM]


def _n_pair_sources(cfg):
    return cfg.kv_heads if cfg.heads > cfg.kv_heads else cfg.heads // 2


def _pair_source(p, grp):
    return p if grp == 1 else (2 * p) // grp


def _fill_pairs(dst, prev_ref, cur_ref, grp):
    for j in range(dst.shape[0]):
        for ref, rows in ((prev_ref, slice(0, BLOCK)), (cur_ref, slice(BLOCK, dst.shape[1]))):
            if grp == 1:
                dst[j, rows, :] = ref[:, j * LANES:(j + 1) * LANES]
            else:
                one = ref[:, _head(j)]
                dst[j, rows, :] = jnp.concatenate([one, one], axis=1)


def _band(i, max_dist):
    qi = lax.broadcasted_iota(jnp.int32, (BLOCK, 2 * BLOCK), 0)
    sj = lax.broadcasted_iota(jnp.int32, (BLOCK, 2 * BLOCK), 1)
    dist = qi + BLOCK - sj
    valid = (dist >= 0) & (dist <= max_dist)
    first_key = jnp.where(i > 0, 0, BLOCK)
    valid_first = valid & (sj >= first_key)
    return dist, valid, valid_first


def _attn_geometry(n):
    tq = min(ATTN_QUERY_BLOCK, n)
    return tq, tq // BLOCK, n // tq


def attn_fwd(qkv, cfg, sinks, name):
    d, n, _ = qkv.shape
    tq, nsub, nqb = _attn_geometry(n)
    wq, wk = cfg.wq, cfg.wk
    grp = cfg.heads // cfg.kv_heads

    def body(sink_ref, q_ref, kc_ref, kp_ref, vc_ref, vp_ref, o_ref, l_ref, kf, vf):
        i = pl.program_id(1)
        _fill_pairs(kf, kp_ref, kc_ref, grp)
        _fill_pairs(vf, vp_ref, vc_ref, grp)
        dist, valid, valid_first = _band(i, cfg.max_dist)
        distf = dist.astype(F32)
        half_q, half_k = _lane_halves(BLOCK), _lane_halves(2 * BLOCK)
        rows = [slice(a * BLOCK, (a + 1) * BLOCK) for a in range(nsub)]
        wins = [slice(a * BLOCK, (a + 2) * BLOCK) for a in range(nsub)]
        for p in range(cfg.heads // 2):
            lanes = slice(p * LANES, (p + 1) * LANES)
            ki = _pair_source(p, grp)
            hs = (2 * p, 2 * p + 1)
            b_reg = [_masked_bias(valid, distf, cfg, h, d) for h in hs]
            b_first = [_masked_bias(valid_first, distf, cfg, h, d) for h in hs]
            ss = []
            for a in range(nsub):
                q2 = q_ref[rows[a], lanes] * SCALE
                k2 = kf[ki, wins[a], :]
                ss.append([_dot_nt(jnp.where(half_q[e], q2, 0), k2) + (b_first[e] if a == 0 else b_reg[e])
                           for e in range(2)])
            es, invs, lses = [], [], []
            for a in range(nsub):
                e_a, inv_a, lse_a = [], [], []
                for e in range(2):
                    m = jnp.max(ss[a][e], axis=1, keepdims=True)
                    if cfg.sinks:
                        m = jnp.maximum(m, sink_ref[hs[e]])
                    ex = jnp.exp(ss[a][e] - m)
                    den = jnp.sum(ex, axis=1, keepdims=True)
                    if cfg.sinks:
                        den = den + jnp.exp(sink_ref[hs[e]] - m)
                    e_a.append(ex.astype(BF16))
                    inv_a.append(1.0 / den)
                    lse_a.append(m + jnp.log(den))
                es.append(e_a)
                invs.append(inv_a)
                lses.append(lse_a)
            for a in range(nsub):
                v2 = vf[ki, wins[a], :]
                pcat = jnp.concatenate(es[a], axis=1)
                vcat = jnp.concatenate([jnp.where(half_k[e], v2, 0) for e in range(2)], axis=0)
                o = _dot(pcat, vcat) * jnp.where(half_q[0], invs[a][0], invs[a][1])
                o_ref[rows[a], lanes] = o.astype(BF16)
                l_ref[rows[a], lanes] = jnp.where(half_q[0], lses[a][0], lses[a][1])

    prev = lambda i: jnp.maximum(i * nsub - 1, 0)
    cur = lambda w, c: pl.BlockSpec((None, tq, w), lambda r, i: (r, i, c // w))
    prv = lambda w, c: pl.BlockSpec((None, BLOCK, w), lambda r, i: (r, prev(i), c // w))
    out = pl.BlockSpec((None, tq, wq), lambda r, i: (r, i, 0))
    pair_scratch = pltpu.VMEM((_n_pair_sources(cfg), tq + BLOCK, LANES), BF16)
    return pl.pallas_call(
        body, name=name, grid=(d, nqb),
        out_shape=[jax.ShapeDtypeStruct((d, n, wq), BF16), jax.ShapeDtypeStruct((d, n, wq), F32)],
        in_specs=[pl.BlockSpec(memory_space=pltpu.SMEM),
                  cur(wq, cfg.qc), cur(wk, cfg.kc), prv(wk, cfg.kc), cur(wk, cfg.vc), prv(wk, cfg.vc)],
        out_specs=[out, out],
        scratch_shapes=[pair_scratch, pair_scratch],
        compiler_params=_cp("parallel", "parallel"),
    )(sinks, qkv, qkv, qkv, qkv, qkv)


def mix_merge(ya, o_g, l_g, proj, w_a, w_b, name):
    t = ya.shape[0]
    dm = w_a.shape[1]
    tm = _row_tile(t)
    gcol = GATE_COL // dm
    dils = [o.shape[0] for o in o_g]

    def body(ya_ref, o0, o1, o2, l0, l1, l2, ga_ref, gb_ref, wa_ref, wb_ref, yb_ref, mg_ref, *scrs):
        ls = [_unfold_from(l, scrs, d) for l, d in zip((l0, l1, l2), dils)]
        m = jnp.maximum(jnp.maximum(ls[0], ls[1]), ls[2])
        es = [jnp.exp(l - m) for l in ls]
        inv = 1.0 / (es[0] + es[1] + es[2])
        yb = sum(_unfold_from(o, scrs, d) * (e * inv) for o, e, d in zip((o0, o1, o2), es, dils)).astype(BF16)
        yb_ref[...] = yb
        pa = _dot(ya_ref[...], wa_ref[...])
        pb = _dot(yb, wb_ref[...])
        mg = _sigmoid(ga_ref[...].astype(F32)) * pa + _sigmoid(gb_ref[...].astype(F32)) * pb
        mg_ref[...] = mg.astype(BF16)

    wide = lambda w: pl.BlockSpec((tm, w), lambda i: (i, 0))
    folded = [_folded_spec(d, tm, B_OUT_W) for d in dils]
    return pl.pallas_call(
        body, name=name, grid=(t // tm,),
        out_shape=[jax.ShapeDtypeStruct((t, B_OUT_W), BF16), jax.ShapeDtypeStruct((t, dm), BF16)],
        in_specs=[wide(ya.shape[1])] + folded + folded
                 + [pl.BlockSpec((tm, dm), lambda i: (i, gcol)), pl.BlockSpec((tm, dm), lambda i: (i, gcol + 1)),
                    pl.BlockSpec(w_a.shape, lambda i: (0, 0)), pl.BlockSpec(w_b.shape, lambda i: (0, 0))],
        out_specs=[wide(B_OUT_W), wide(dm)],
        scratch_shapes=_fold_scratch(tm, B_OUT_W),
        compiler_params=_cp("parallel"),
    )(ya, *o_g, *l_g, proj, proj, w_a, w_b)


def loss_head(y, target):
    t, dm = y[0].shape
    tm = _row_tile(t)

    def body(zh_ref, lg_ref, lb_ref, t_ref, dy_ref, loss_ref):
        @pl.when(pl.program_id(0) == 0)
        def _():
            loss_ref[...] = jnp.zeros_like(loss_ref)
        err = _stream(zh_ref, lg_ref, lb_ref) - t_ref[...]
        dy_ref[...] = err * (1.0 / dm)
        per_row = jnp.sum(err * err, axis=1, keepdims=True) * (1.0 / dm)
        loss_ref[...] += 0.5 * jnp.sum(per_row, axis=0, keepdims=True)

    row = pl.BlockSpec((tm, dm), lambda i: (i, 0))
    return pl.pallas_call(
        body, name="loss_head", grid=(t // tm,),
        out_shape=[jax.ShapeDtypeStruct((t, dm), F32), jax.ShapeDtypeStruct((8, LANES), F32)],
        in_specs=_stream_specs(tm, dm) + [row],
        out_specs=[row, pl.BlockSpec((8, LANES), lambda i: (0, 0))],
        compiler_params=_cp("arbitrary"),
    )(*y, target)


def _fold_rows(v):
    tm, c = v.shape
    return jnp.sum(v.reshape(tm // 8, 8, c), axis=0)


def _finish_sums(refs, nsteps):
    @pl.when(pl.program_id(0) == nsteps - 1)
    def _():
        for r in refs:
            r[...] = jnp.broadcast_to(jnp.sum(r[...], axis=0, keepdims=True), r.shape)


def ln_bwd(dxo, zhat, rstd, ysub, lg, gate, act, name):
    t, dm = dxo.shape
    k = act.shape[1]
    tm = _row_tile(t)
    nsteps = t // tm
    ch = 256

    def body(dxo_ref, zh_ref, rs_ref, y_ref, lg_ref, g_ref, a_ref, dz_ref, dy_ref, sg_ref, sb_ref, sgate_ref,
             gw_ref, acc):
        @pl.when(pl.program_id(0) == 0)
        def _():
            for r in (sg_ref, sb_ref, sgate_ref, acc):
                r[...] = jnp.zeros_like(r)
        for rows in _halves(tm):
            dxo_v = dxo_ref[rows, :]
            zh = zh_ref[rows, :]
            dxh = dxo_v * lg_ref[...]
            m1 = jnp.mean(dxh, axis=1, keepdims=True)
            m2 = jnp.mean(dxh * zh, axis=1, keepdims=True)
            dz = rs_ref[rows, 0:1] * (dxh - m1 - zh * m2)
            dz_ref[rows, :] = dz
            dy = (g_ref[...] * dz).astype(BF16)
            dy_ref[rows, :] = dy
            sg_ref[...] += _fold_rows(dxo_v * zh)
            sb_ref[...] += _fold_rows(dxo_v)
            sgate_ref[...] += _fold_rows(dz * y_ref[rows, :].astype(F32))
            a = a_ref[rows, :]
            for c0 in range(0, dm, ch):
                acc[:, c0:c0 + ch] += _dot_tn(a, dy[:, c0:c0 + ch])
        _finish_sums((sg_ref, sb_ref, sgate_ref), nsteps)

        @pl.when(pl.program_id(0) == nsteps - 1)
        def _():
            gw_ref[...] = acc[...].astype(BF16)

    row = pl.BlockSpec((tm, dm), lambda i: (i, 0))
    vec = pl.BlockSpec((1, dm), lambda i: (0, 0))
    sums = pl.BlockSpec((8, dm), lambda i: (0, 0))
    return pl.pallas_call(
        body, name=name, grid=(nsteps,),
        out_shape=[jax.ShapeDtypeStruct((t, dm), F32), jax.ShapeDtypeStruct((t, dm), BF16)]
                  + [jax.ShapeDtypeStruct((8, dm), F32)] * 3 + [jax.ShapeDtypeStruct((k, dm), BF16)],
        in_specs=[row, row, pl.BlockSpec((tm, LANES), lambda i: (i, 0)), row, vec, vec,
                  pl.BlockSpec((tm, k), lambda i: (i, 0))],
        out_specs=[row, row, sums, sums, sums, pl.BlockSpec((k, dm), lambda i: (0, 0))],
        scratch_shapes=[pltpu.VMEM((k, dm), F32)],
        compiler_params=_cp("arbitrary"),
    )(dxo, zhat, rstd, ysub, lg, gate, act)


def _mod_bwd_store(du_of, dz_ref, x_refs, s_ref, dx_ref, ss_ref, ssh_ref, nsteps):
    @pl.when(pl.program_id(0) == 0)
    def _():
        ss_ref[...] = jnp.zeros_like(ss_ref)
        ssh_ref[...] = jnp.zeros_like(ssh_ref)
    for c0 in range(0, dx_ref.shape[1], DGRAD_CHUNK):
        cols = slice(c0, c0 + DGRAD_CHUNK)
        du = du_of(cols)
        dx_ref[:, cols] = DN_ALPHA * dz_ref[:, cols] + du * (1.0 + s_ref[:, cols])
        ss_ref[:, cols] += _fold_rows(du * _stream(*x_refs, cols=cols))
        ssh_ref[:, cols] += _fold_rows(du)
    _finish_sums((ss_ref, ssh_ref), nsteps)


def dgrad_ffn(g, wt, dz, xin, s, name, ex=None):
    t, dm = dz.shape
    k = g.shape[1]
    tm = _row_tile(t)
    nsteps = t // tm

    def body(*refs):
        i = pl.program_id(0)
        (g_ref, w_ref, dz_ref, xz_ref, xg_ref, xb_ref, s_ref,
         dx_ref, ss_ref, ssh_ref) = _hosted(ex, refs, 7, 3, i == 0, i == nsteps - 1)
        g_v = g_ref[...]
        _mod_bwd_store(lambda cols: _dot(g_v, w_ref[:, cols]), dz_ref, (xz_ref, xg_ref, xb_ref), s_ref,
                       dx_ref, ss_ref, ssh_ref, nsteps)

    row = pl.BlockSpec((tm, dm), lambda i: (i, 0))
    acc = pl.BlockSpec((8, dm), lambda i: (0, 0))
    return _host_call(
        body, ex, name=name, grid=(nsteps,),
        out_shape=[jax.ShapeDtypeStruct((t, dm), F32)] + [jax.ShapeDtypeStruct((8, dm), F32)] * 2,
        in_specs=[pl.BlockSpec((tm, k), lambda i: (i, 0)), pl.BlockSpec((k, dm), lambda i: (0, 0)), row]
                 + _stream_specs(tm, dm) + [pl.BlockSpec((1, dm), lambda i: (0, 0))],
        out_specs=[row, acc, acc], sem=("arbitrary",), args=[g, wt, dz, *xin, s])


def dswiglu(dy, wdt, ab, name):
    t, dm = dy.shape
    f = wdt.shape[1]
    tm = _row_tile(t)

    def body(dy_ref, w_ref, a_ref, b_ref, o_ref):
        dy_v = dy_ref[...]
        for c0 in range(0, f, FF_CHUNK):
            cols = slice(c0, c0 + FF_CHUNK)
            dh = _dot(dy_v, w_ref[:, cols])
            a = a_ref[:, cols].astype(F32)
            sg = _sigmoid(a)
            o_ref[:, cols] = (dh * b_ref[:, cols].astype(F32) * (sg * (1.0 + a * (1.0 - sg)))).astype(BF16)
            o_ref[:, f + c0:f + c0 + FF_CHUNK] = (dh * (a * sg)).astype(BF16)

    return pl.pallas_call(
        body, name=name, grid=(t // tm,),
        out_shape=jax.ShapeDtypeStruct((t, 2 * f), BF16),
        in_specs=[pl.BlockSpec((tm, dm), lambda i: (i, 0)), pl.BlockSpec((dm, f), lambda i: (0, 0)),
                  pl.BlockSpec((tm, f), lambda i: (i, 0)), pl.BlockSpec((tm, f), lambda i: (i, 1))],
        out_specs=pl.BlockSpec((tm, 2 * f), lambda i: (i, 0)),
        compiler_params=_cp("parallel"),
    )(dy, wdt, ab, ab)


def dgrad_in(d_a, d_b, dgab, wt, dz, xin, s, name, ex=None):
    t, dm = dz.shape
    tm = _row_tile(t)
    nsteps = t // tm
    dils = [a.shape[0] for a in d_b]

    def body(*refs):
        i = pl.program_id(0)
        (da_ref, b0, b1, b2, dg_ref, w_ref, dz_ref, xz_ref, xg_ref, xb_ref, s_ref, dx_ref, ss_ref, ssh_ref,
         *scrs) = _hosted(ex, refs, 11, 3, i == 0, i == nsteps - 1)
        vs = [b_ref[...].reshape(tm, B_GW) for b_ref in (b0, b1, b2)]

        def du_of(cols):
            du = _dot(da_ref[0], w_ref[0:A_W, cols])
            for g, (v, d) in enumerate(zip(vs, dils)):
                part = None
                for p in range(3):
                    r0 = A_W + p * B_ALL + g * B_OUT_W
                    term = _dot(v[:, p * B_OUT_W:(p + 1) * B_OUT_W], w_ref[r0:r0 + B_OUT_W, cols])
                    part = term if part is None else part + term
                if d == 1:
                    du = du + part
                else:
                    n = tm // d
                    du = du + _unfold_rows(lambda r, cs: part[r * n:(r + 1) * n, cs], scrs, d, n, DGRAD_CHUNK)
            for j in range(2):
                du = du + _dot(dg_ref[:, j * dm:(j + 1) * dm], w_ref[GATE_COL + j * dm:GATE_COL + (j + 1) * dm, cols])
            return du

        _mod_bwd_store(du_of, dz_ref, (xz_ref, xg_ref, xb_ref), s_ref, dx_ref, ss_ref, ssh_ref, nsteps)

    row = pl.BlockSpec((tm, dm), lambda i: (i, 0))
    acc = pl.BlockSpec((8, dm), lambda i: (0, 0))
    return _host_call(
        body, ex, name=name, grid=(nsteps,),
        out_shape=[jax.ShapeDtypeStruct((t, dm), F32)] + [jax.ShapeDtypeStruct((8, dm), F32)] * 2,
        in_specs=[_folded_spec(1, tm, A_W)] + [_folded_spec(d, tm, B_GW) for d in dils]
                 + [pl.BlockSpec((tm, 2 * dm), lambda i: (i, 0)), pl.BlockSpec(wt.shape, lambda i: (0, 0)), row]
                 + _stream_specs(tm, dm) + [pl.BlockSpec((1, dm), lambda i: (0, 0))],
        out_specs=[row, acc, acc],
        scratch_shapes=_fold_scratch(tm, DGRAD_CHUNK), sem=("arbitrary",), args=[d_a, *d_b, dgab, wt, dz, *xin, s])


def wgrad(a, b, buf, tn, nj, b0, o0, om, name):
    t, k = a.shape
    tt = ROW_TILE
    while tt * 2 * k <= WGRAD_TILE_ELEMS and tt * 2 <= t:
        tt *= 2
    nsteps = t // tt
    last = nsteps - 1

    def body(a_ref, b_ref, buf_ref, o_ref, acc):
        s, j = pl.program_id(0), pl.program_id(1)

        @pl.when(s == 0)
        def _():
            acc[j] = jnp.zeros(acc.shape[1:], F32)
        acc[j] += _dot_tn(a_ref[...], b_ref[...])

        @pl.when(s == last)
        def _():
            o_ref[...] = acc[j].astype(BF16)

    return pl.pallas_call(
        body, name=name, grid=(nsteps, nj),
        out_shape=jax.ShapeDtypeStruct(buf.shape, buf.dtype),
        in_specs=[pl.BlockSpec((tt, k), lambda s, j: (s, 0)),
                  pl.BlockSpec((tt, tn), lambda s, j: (s, b0 + j)),
                  pl.BlockSpec(memory_space=pl.ANY)],
        out_specs=pl.BlockSpec((k, tn), lambda s, j: (0, o0 + om * jnp.where(s == last, j, 0))),
        scratch_shapes=[pltpu.VMEM((nj, k, tn), F32)],
        input_output_aliases={2: 0},
        compiler_params=_cp("arbitrary", "arbitrary"),
    )(a, b, buf)


def dmerge(do, wot, ya, yb, w_a, w_b, wat, wbt, proj, name):
    t, dm = do.shape
    tm = _row_tile(t)
    nsteps = t // tm
    gcol = GATE_COL // dm
    ch = 256

    def body(do_ref, wot_ref, ya_ref, yb_ref, wa_ref, wb_ref, wat_ref, wbt_ref, g_ref,
             dya_ref, dyb_ref, dg_ref, gwa_ref, gwb_ref, dm_scr, acc_a, acc_b):
        i, j = pl.program_id(0), pl.program_id(1)

        @pl.when((i == 0) & (j == 0))
        def _():
            acc_a[...] = jnp.zeros_like(acc_a)
            acc_b[...] = jnp.zeros_like(acc_b)

        @pl.when(j == 0)
        def _():
            do_v = do_ref[...]
            for c0 in range(0, dm, ch):
                dm_scr[:, c0:c0 + ch] = _dot(do_v, wot_ref[:, c0:c0 + ch])

        def branch(y_ref, w_ref, wt_ref, dy_ref, acc, gw_ref):
            y = y_ref[...]
            dy = None
            for c0 in range(0, dm, ch):
                cols = slice(c0, c0 + ch)
                p = _dot(y, w_ref[:, cols])
                sg = _sigmoid(g_ref[:, cols].astype(F32))
                dmg = dm_scr[:, cols]
                dp = (dmg * sg).astype(BF16)
                dg_ref[:, cols] = (dmg * p * (sg * (1.0 - sg))).astype(BF16)
                acc[:, cols] += _dot_tn(y, dp)
                part = _dot(dp, wt_ref[cols, :])
                dy = part if dy is None else dy + part
            dy_ref[...] = dy.astype(dy_ref.dtype)

            @pl.when(i == nsteps - 1)
            def _():
                gw_ref[...] = acc[...].astype(BF16)

        pl.when(j == 0)(lambda: branch(ya_ref, wa_ref, wat_ref, dya_ref, acc_a, gwa_ref))
        pl.when(j == 1)(lambda: branch(yb_ref, wb_ref, wbt_ref, dyb_ref, acc_b, gwb_ref))

    full = lambda arr: pl.BlockSpec(arr.shape, lambda i, j: (0, 0))
    rowc = lambda w: pl.BlockSpec((tm, w), lambda i, j: (i, 0))
    return pl.pallas_call(
        body, name=name, grid=(nsteps, 2),
        out_shape=[jax.ShapeDtypeStruct((t, ya.shape[1]), BF16), jax.ShapeDtypeStruct((t, yb.shape[1]), F32),
                   jax.ShapeDtypeStruct((t, 2 * dm), BF16),
                   jax.ShapeDtypeStruct(w_a.shape, BF16), jax.ShapeDtypeStruct(w_b.shape, BF16)],
        in_specs=[rowc(dm), full(wot), rowc(ya.shape[1]), rowc(yb.shape[1]), full(w_a), full(w_b),
                  full(wat), full(wbt), pl.BlockSpec((tm, dm), lambda i, j: (i, gcol + j))],
        out_specs=[rowc(ya.shape[1]), rowc(yb.shape[1]), pl.BlockSpec((tm, dm), lambda i, j: (i, j)),
                   full(w_a), full(w_b)],
        scratch_shapes=[pltpu.VMEM((tm, dm), F32), pltpu.VMEM(w_a.shape, F32), pltpu.VMEM(w_b.shape, F32)],
        compiler_params=_cp("arbitrary", "arbitrary"),
    )(do, wot, ya, yb, w_a, w_b, wat, wbt, proj)


def mix_bwd(dyb, o_g, l_g, name):
    t, w = dyb.shape
    tm = _row_tile(t)
    nh = w // HEAD_DIM
    dils = [o.shape[0] for o in o_g]

    def body(dyb_ref, o0, o1, o2, l0, l1, l2, do0, do1, do2, dl0, dl1, dl2, *scr):
        ls = [_unfold_from(l, scr, d) for l, d in zip((l0, l1, l2), dils)]
        m = jnp.maximum(jnp.maximum(ls[0], ls[1]), ls[2])
        es = [jnp.exp(l - m) for l in ls]
        inv = 1.0 / (es[0] + es[1] + es[2])
        wts = [e * inv for e in es]
        dyb_v = dyb_ref[...]
        dws = []
        for o_ref, do_ref, wt, d in zip((o0, o1, o2), (do0, do1, do2), wts, dils):
            prod = dyb_v * _unfold_from(o_ref, scr, d)
            _fold_to(do_ref, dyb_v * wt, scr, d)
            for h in range(nh):
                hs = slice(h * HEAD_DIM, (h + 1) * HEAD_DIM)
                dws.append(jnp.broadcast_to(jnp.sum(prod[:, hs], axis=1, keepdims=True), (tm, HEAD_DIM)))
        for g, (dl_ref, d) in enumerate(zip((dl0, dl1, dl2), dils)):
            cols = []
            for h in range(nh):
                hs = slice(h * HEAD_DIM, (h + 1) * HEAD_DIM)
                mean = sum(wts[g2][:, hs] * dws[g2 * nh + h] for g2 in range(N_GROUPS))
                cols.append(wts[g][:, hs] * (dws[g * nh + h] - mean))
            _fold_to(dl_ref, jnp.concatenate(cols, axis=1), scr, d)

    folded = [_folded_spec(d, tm, w) for d in dils]
    return pl.pallas_call(
        body, name=name, grid=(t // tm,),
        out_shape=[jax.ShapeDtypeStruct(o.shape, BF16) for o in o_g]
                  + [jax.ShapeDtypeStruct(o.shape, F32) for o in o_g],
        in_specs=[pl.BlockSpec((tm, w), lambda i: (i, 0))] + folded + folded,
        out_specs=folded + folded,
        scratch_shapes=_fold_scratch(tm, w),
        compiler_params=_cp("parallel"),
    )(dyb, *o_g, *l_g)


def attn_bwd(qkv, o, lse, do, dlse, cfg, sinks, name):
    d, n, _ = qkv.shape
    tq, nsub, nqb = _attn_geometry(n)
    wq, wk, wout = cfg.wq, cfg.wk, cfg.wout
    grp = cfg.heads // cfg.kv_heads
    has_dl = dlse is not None

    def body(*refs):
        sink_ref, q_ref, qn_ref, kc_ref, kp_ref, vc_ref, vp_ref = refs[:7]
        o_ref, on_ref, do_ref, don_ref, l_ref, ln_ref = refs[7:13]
        rest = refs[13:]
        dl_ref = dln_ref = None
        if has_dl:
            dl_ref, dln_ref = rest[:2]
            rest = rest[2:]
        out_ref = rest[0]
        rest = rest[1:]
        if cfg.sinks:
            dsink_ref = rest[0]
            rest = rest[1:]
        kf, vf = rest
        r, i = pl.program_id(0), pl.program_id(1)
        _fill_pairs(kf, kp_ref, kc_ref, grp)
        _fill_pairs(vf, vp_ref, vc_ref, grp)
        dist, valid, valid_first = _band(i, cfg.max_dist)
        distf = dist.astype(F32)
        next_dist = jnp.where(i < nqb - 1, cfg.max_dist, -1)
        valid_next = (dist[:, 0:BLOCK] >= 0) & (dist[:, 0:BLOCK] <= next_dist)
        half_q = _lane_halves(BLOCK)
        if cfg.sinks:
            @pl.when((r == 0) & (i == 0))
            def _():
                dsink_ref[...] = jnp.zeros_like(dsink_ref)

        shared = {}
        for p in range(cfg.heads // 2):
            lanes = slice(p * LANES, (p + 1) * LANES)
            ki = _pair_source(p, grp)
            hs = (2 * p, 2 * p + 1)
            biases = [[_masked_bias(m, dd, cfg, h, d) for h in hs]
                      for m, dd in ((valid_first, distf), (valid, distf), (valid_next, distf[:, 0:BLOCK]))]
            tiles = []
            for a in range(nsub + 1):
                if a < nsub:
                    rows, win = slice(a * BLOCK, (a + 1) * BLOCK), slice(a * BLOCK, (a + 2) * BLOCK)
                    src = (q_ref, o_ref, do_ref, l_ref, dl_ref)
                else:
                    rows, win = slice(0, BLOCK), slice(nsub * BLOCK, (nsub + 1) * BLOCK)
                    src = (qn_ref, on_ref, don_ref, ln_ref, dln_ref)
                q2 = src[0][rows, lanes] * SCALE
                do2 = src[2][rows, lanes]
                o2 = src[1][rows, lanes].astype(F32)
                l2 = src[3][rows, lanes]
                k2, v2 = kf[ki, win, :], vf[ki, win, :]
                per = []
                for e in range(2):
                    qe, doe = jnp.where(half_q[e], q2, 0), jnp.where(half_q[e], do2, 0)
                    delta = jnp.sum(doe.astype(F32) * o2, axis=1, keepdims=True)
                    lse_v = jnp.max(jnp.where(half_q[e], l2, NEG_INF), axis=1, keepdims=True)
                    shift = -delta
                    if has_dl:
                        shift = shift + jnp.max(jnp.where(half_q[e], src[4][rows, lanes], NEG_INF), axis=1,
                                                keepdims=True)
                    s = _dot_nt(qe, k2) + biases[0 if a == 0 else (1 if a < nsub else 2)][e]
                    per.append((qe, doe, delta, lse_v, shift, s, _dot_nt(doe, v2)))
                tiles.append((k2, per))
            grads = []
            for k2, per in tiles:
                both = []
                for qe, doe, delta, lse_v, shift, s, dp in per:
                    pr = jnp.exp(s - lse_v)
                    both.append(((pr * (dp + shift)).astype(BF16), pr.astype(BF16)))
                grads.append(both)
            dkt, dvt = [], []
            for a, ((k2, per), both) in enumerate(zip(tiles, grads)):
                if a < nsub:
                    half_k = _lane_halves(k2.shape[0])
                    ds_cat = jnp.concatenate([both[0][0], both[1][0]], axis=1)
                    k_cat = jnp.concatenate([jnp.where(half_k[e], k2, 0) for e in range(2)], axis=0)
                    out_ref[a * BLOCK:(a + 1) * BLOCK, lanes] = (_dot(ds_cat, k_cat) * SCALE).astype(BF16)
                cut = (lambda x: x[:, BLOCK:]) if a == 0 else (lambda x: x)
                q_cat = jnp.concatenate([per[0][0], per[1][0]], axis=0)
                do_cat = jnp.concatenate([per[0][1], per[1][1]], axis=0)
                dkt.append(_dot_tn(q_cat, jnp.concatenate([cut(both[0][0]), cut(both[1][0])], axis=0)))
                dvt.append(_dot_tn(do_cat, jnp.concatenate([cut(both[0][1]), cut(both[1][1])], axis=0)))
                if cfg.sinks and a < nsub:
                    for e in range(2):
                        psink = jnp.exp(sink_ref[hs[e]] - per[e][3])
                        tot = jnp.sum(psink * (-per[e][2]), axis=0, keepdims=True)
                        dsink_ref[hs[e]:hs[e] + 1, :] += jnp.broadcast_to(tot, (1, LANES))
            for m in range(nsub):
                rows = slice(m * BLOCK, (m + 1) * BLOCK)
                for which, (acc, col0) in enumerate(((dkt, wq), (dvt, wq + wk))):
                    own = acc[m] if m == 0 else acc[m][:, BLOCK:]
                    total = own + acc[m + 1][:, 0:BLOCK]
                    if grp == 1:
                        out_ref[rows, col0 + p * LANES:col0 + (p + 1) * LANES] = total.T.astype(BF16)
                    else:
                        t64 = total[0:HEAD_DIM] + total[HEAD_DIM:]
                        key = (ki, which, m)
                        shared[key] = t64 + shared[key] if key in shared else t64
        for (ki, which, m), t64 in shared.items():
            col0 = (wq, wq + wk)[which] + ki * HEAD_DIM
            out_ref[m * BLOCK:(m + 1) * BLOCK, col0:col0 + HEAD_DIM] = t64.T.astype(BF16)

    prev = lambda i: jnp.maximum(i * nsub - 1, 0)
    nxt = lambda i: jnp.minimum((i + 1) * nsub, n // BLOCK - 1)
    cur = lambda w, c: pl.BlockSpec((None, tq, w), lambda r, i: (r, i, c // w))
    prv = lambda w, c: pl.BlockSpec((None, BLOCK, w), lambda r, i: (r, prev(i), c // w))
    o_cur = pl.BlockSpec((None, tq, wq), lambda r, i: (r, i, 0))
    o_nxt = pl.BlockSpec((None, BLOCK, wq), lambda r, i: (r, nxt(i), 0))
    in_specs = [pl.BlockSpec(memory_space=pltpu.SMEM),
                cur(wq, cfg.qc), pl.BlockSpec((None, BLOCK, wq), lambda r, i: (r, nxt(i), cfg.qc // wq)),
                cur(wk, cfg.kc), prv(wk, cfg.kc), cur(wk, cfg.vc), prv(wk, cfg.vc),
                o_cur, o_nxt, o_cur, o_nxt, o_cur, o_nxt]
    args = [sinks, qkv, qkv, qkv, qkv, qkv, qkv, o, o, do, do, lse, lse]
    if has_dl:
        in_specs += [o_cur, o_nxt]
        args += [dlse, dlse]
    out_shape = [jax.ShapeDtypeStruct((d, n, wout), BF16)]
    out_specs = [pl.BlockSpec((None, tq, wout), lambda r, i: (r, i, 0))]
    if cfg.sinks:
        out_shape.append(jax.ShapeDtypeStruct((8, LANES), F32))
        out_specs.append(pl.BlockSpec((8, LANES), lambda r, i: (0, 0)))
    pair_scratch = pltpu.VMEM((_n_pair_sources(cfg), tq + BLOCK, LANES), BF16)
    return pl.pallas_call(
        body, name=name, grid=(d, nqb), out_shape=out_shape, in_specs=in_specs, out_specs=out_specs,
        scratch_shapes=[pair_scratch, pair_scratch],
        compiler_params=_cp("arbitrary", "arbitrary"),
    )(*args)


def _adamw(g, w, m, v):
    m = ADAM_B1 * m + (1.0 - ADAM_B1) * g
    v = ADAM_B2 * v + (1.0 - ADAM_B2) * (g * g)
    m_hat = m / (1.0 - ADAM_B1 ** ADAM_STEP)
    v_hat = v / (1.0 - ADAM_B2 ** ADAM_STEP)
    delta = -ADAM_LR * (m_hat / (jnp.sqrt(v_hat) + ADAM_EPS) + ADAM_WD * w)
    return delta, m, v


def adam_reduce(parts, w, m, v, name):
    r, c = w.shape
    tr = next(cand for cand in (256, 128, 64, 32, 16, 8) if r % cand == 0) if r > 256 else r

    def body(p_ref, w_ref, m_ref, v_ref, g_ref, d_ref, mo_ref, vo_ref):
        g = p_ref[0].astype(F32)
        for j in range(1, N_DEV):
            g = g + p_ref[j].astype(F32)
        g_ref[...] = g
        d_ref[...], mo_ref[...], vo_ref[...] = _adamw(g, w_ref[...], m_ref[...], v_ref[...])

    row = pl.BlockSpec((tr, c), lambda i: (i, 0))
    return pl.pallas_call(
        body, name=name, grid=(r // tr,),
        out_shape=[jax.ShapeDtypeStruct((r, c), F32)] * 4,
        in_specs=[pl.BlockSpec((N_DEV, tr, c), lambda i: (0, i, 0)), row, row, row],
        out_specs=[row] * 4,
        compiler_params=_cp("parallel"),
    )(parts, w, m, v)


def adam_layers(parts, w, m, v, name):
    nl, r, c = w.shape
    tr = next(cand for cand in (256, 128, 64, 32, 16, 8) if r % cand == 0)
    steps = r // tr

    def body(*refs):
        p_refs = refs[:nl]
        w_ref, m_ref, v_ref, g_ref, d_ref, mo_ref, vo_ref = refs[nl:]
        for k in range(nl):
            @pl.when(pl.program_id(0) == k)
            def _():
                g = p_refs[k][0].astype(F32)
                for j in range(1, N_DEV):
                    g = g + p_refs[k][j].astype(F32)
                g_ref[...] = g
                d_ref[...], mo_ref[...], vo_ref[...] = _adamw(g, w_ref[...], m_ref[...], v_ref[...])

    def part_spec(k):
        return pl.BlockSpec((N_DEV, tr, c), lambda l, i: (0, jnp.clip(i + (l - k) * steps, 0, steps - 1), 0))

    blk = pl.BlockSpec((None, tr, c), lambda l, i: (l, i, 0))
    return pl.pallas_call(
        body, name=name, grid=(nl, steps),
        out_shape=[jax.ShapeDtypeStruct((nl, r, c), F32)] * 4,
        in_specs=[part_spec(k) for k in range(nl)] + [blk, blk, blk],
        out_specs=[blk] * 4,
        compiler_params=_cp("arbitrary", "arbitrary"),
    )(*parts, w, m, v)


def adam_w_ada(sct, dm_loc, w, m, v):
    nl, dm, wc = w.shape
    tr = 512

    def body(s_ref, d_ref, w_ref, m_ref, v_ref, g_ref, dl_ref, mo_ref, vo_ref):
        g = jnp.dot(s_ref[...], d_ref[...], preferred_element_type=F32, precision=lax.Precision.HIGHEST)
        g_ref[...] = g
        dl_ref[...], mo_ref[...], vo_ref[...] = _adamw(g, w_ref[...], m_ref[...], v_ref[...])

    blk = pl.BlockSpec((None, tr, wc), lambda l, i: (l, i, 0))
    return pl.pallas_call(
        body, name="adam_w_ada", grid=(nl, dm // tr),
        out_shape=[jax.ShapeDtypeStruct(w.shape, F32)] * 4,
        in_specs=[pl.BlockSpec((tr, LANES), lambda l, i: (i, 0)),
                  pl.BlockSpec((None, LANES, wc), lambda l, i: (l, 0, 0)), blk, blk, blk],
        out_specs=[blk] * 4,
        compiler_params=_cp("parallel", "parallel"),
    )(sct, dm_loc, w, m, v)


TRANSPOSED = ("w_gate", "w_up")


def _pieces(dm):
    ncol = lambda n: n // N_DEV
    mixer = ([Piece("w_in", "w_in", 1, 0, ncol(GATE_COL + 2 * dm)),
              Piece("w_a", "w_a", 1, 0, ncol(dm)),
              Piece("w_b", "w_b", 1, 0, ncol(dm)),
              Piece("w_o", "w_o", 0, 0, ncol(dm))],
             {"w_in": (dm, GATE_COL + 2 * dm), "w_a": (A_Q_HEADS * HEAD_DIM, dm), "w_b": (B_OUT_W, dm),
              "w_o": (dm, dm)})
    ffn = ([Piece("w_gate", "w_ffn_t", 0, 0, ncol(D_FF)),
            Piece("w_up", "w_ffn_t", 0, D_FF, ncol(D_FF)),
            Piece("w_down", "w_down", 0, 0, ncol(D_FF))],
           {"w_ffn_t": (2 * D_FF, dm), "w_down": (D_FF, dm)})
    return mixer, ffn


def kernel(x, c, w_ada, b_ada, w_in, sinks, w_a, w_b, w_o, ln1_g, ln1_b, w_gate, w_up, w_down, ln2_g, ln2_b, loss_target, m_w_ada, m_b_ada, m_w_in, m_sinks, m_w_a, m_w_b, m_w_o, m_ln1_g, m_ln1_b, m_w_gate, m_w_up, m_w_down, m_ln2_g, m_ln2_b, v_w_ada, v_b_ada, v_w_in, v_sinks, v_w_a, v_w_b, v_w_o, v_ln1_g, v_ln1_b, v_w_gate, v_w_up, v_w_down, v_ln2_g, v_ln2_b):
    given = dict(locals())
    nl = w_in.shape[0]
    t, dm = x.shape[1], x.shape[2]
    me = 4 * lax.axis_index("x") + 2 * lax.axis_index("y") + lax.axis_index("c")
    x0 = x.reshape(t, dm)
    target = loss_target.reshape(t, dm)

    groups = dict(zip(("mixer", "ffn"), _pieces(dm)))
    local = lambda nm, pre="": (given[pre + nm].transpose(0, 2, 1) if nm in TRANSPOSED else given[pre + nm])
    shards = {pc.name: local(pc.name).astype(BF16) for pcs, _ in groups.values() for pc in pcs}

    def gather(group, l):
        pcs, bufs = groups[group]
        return Exchange("gather", pcs, [shards[pc.name][l] for pc in pcs], bufs.values(), bufs)

    def scatter(group, gbuf):
        pcs, bufs = groups[group]
        return Exchange("scatter", pcs, [gbuf[nm] for nm in bufs],
                        [(N_DEV,) + shards[pc.name].shape[1:] for pc in pcs], bufs)

    full = [dict() for _ in range(nl)]
    full[0].update(zip(groups["mixer"][1], run_exchange(gather("mixer", 0), "gather_mixer")))

    wc = w_ada.shape[2]
    c_all = all_gather_small(jnp.broadcast_to(c, (8, dm)), "gather_c")[:, 0, :]
    b_loc = lax.dynamic_slice_in_dim(b_ada, me * wc, wc, axis=1).reshape(nl, 1, wc)
    mp, sc_all = mod_partial(c_all, w_ada, b_loc)
    mp_all = all_gather_small(mp.reshape(nl * N_DEV, wc), "gather_mod").reshape(N_DEV, nl, N_DEV, wc)
    mod = lax.dynamic_index_in_dim(mp_all, me, axis=2, keepdims=False)
    mod = mod.transpose(1, 0, 2).reshape(nl, 6, 1, dm)

    vec = lambda a, l: a[l].reshape(1, dm)

    saved = []
    xl = (x0, jnp.ones((1, dm), F32), jnp.zeros((1, dm), F32))
    for l in range(nl):
        sh1, s1, g1, sh2, s2, g2 = [mod[l, j] for j in range(6)]
        w = full[l]
        (u1, u1_f4, u1_f16, proj, qkv_f4, qkv_f16), got = in_proj(xl, s1, sh1, w["w_in"], "in_proj",
                                                                   gather("ffn", l))
        w.update(zip(groups["ffn"][1], got))
        proj3 = proj.reshape(1, t, proj.shape[1])
        qkv_b = [proj3, qkv_f4, qkv_f16]
        ya, lse_a = attn_fwd(proj3, ATTN_A, sinks[l], "attn_a_fwd")
        o_g, l_g = [], []
        for g, cfg in enumerate(ATTN_B):
            o, ls = attn_fwd(qkv_b[g], cfg, sinks[l], "attn_b%d_fwd" % g)
            o_g.append(o)
            l_g.append(ls)
        yb, merged = mix_merge(ya[0], o_g, l_g, proj, w["w_a"], w["w_b"], "mix_merge")
        y1, zh1, rs1 = proj_ln(merged, w["w_o"], xl, g1, "out_proj_ln")
        x1 = (zh1, vec(ln1_g, l), vec(ln1_b, l))
        (u2, ab), got = modmm(x1, s2, sh2, w["w_ffn_t"].T, "ffn_up", gather("mixer", l + 1) if l + 1 < nl else None)
        if l + 1 < nl:
            full[l + 1].update(zip(groups["mixer"][1], got))
        h, y2, zh2, rs2 = swiglu_proj_ln(ab, w["w_down"], x1, g2, "ffn_down_ln")
        x2 = (zh2, vec(ln2_g, l), vec(ln2_b, l))
        saved.append(dict(xin=xl, u1=[u1, u1_f4.reshape(t, dm), u1_f16.reshape(t, dm)], proj=proj, qkv_b=qkv_b,
                          ya=ya, lse_a=lse_a, o_g=o_g, l_g=l_g, yb=yb, merged=merged,
                          y1=y1, x1=x1, zh1=zh1, rs1=rs1, u2=u2, ab=ab, h=h, y2=y2, zh2=zh2, rs2=rs2))
        xl = x2

    dx, loss_part = loss_head(xl, target)

    small = {k: [None] * nl for k in ("dmod", "ln1_g", "ln1_b", "ln2_g", "ln2_b", "sinks")}
    recv = {nm: [None] * nl for grp in groups.values() for nm in (pc.name for pc in grp[0])}

    def keep(group, l, got):
        for pc, arr in zip(groups[group][0], got):
            recv[pc.name][l] = arr

    for l in reversed(range(nl)):
        sv, w = saved[l], full[l]
        sh1, s1, g1, sh2, s2, g2 = [mod[l, j] for j in range(6)]
        fresh = lambda nm: lax.empty({**groups["mixer"][1], **groups["ffn"][1]}[nm], BF16)
        gbuf = {}
        dz2, dy2, sg, sb, sgate2, gbuf["w_down"] = ln_bwd(dx, sv["zh2"], sv["rs2"], sv["y2"], vec(ln2_g, l), g2,
                                                          sv["h"], "ln_bwd_ffn")
        small["ln2_g"][l], small["ln2_b"][l] = sg[0], sb[0]
        dab = dswiglu(dy2, w["w_down"].T, sv["ab"], "dswiglu")
        gbuf["w_ffn_t"] = wgrad(dab, sv["u2"], fresh("w_ffn_t"), 512, dm // 512, 0, 0, 1, "wgrad_ffn_up")
        (dx1, ss2, ssh2), got = dgrad_ffn(dab, w["w_ffn_t"], dz2, sv["x1"], s2, "dgrad_ffn", scatter("ffn", gbuf))
        keep("ffn", l, got)
        dz1, do1, sg, sb, sgate1, gbuf["w_o"] = ln_bwd(dx1, sv["zh1"], sv["rs1"], sv["y1"], vec(ln1_g, l), g1,
                                                       sv["merged"], "ln_bwd_mixer")
        small["ln1_g"][l], small["ln1_b"][l] = sg[0], sb[0]
        dya, dyb, dgab, gbuf["w_a"], gbuf["w_b"] = dmerge(
            do1, w["w_o"].T, sv["ya"][0], sv["yb"], w["w_a"], w["w_b"], w["w_a"].T, w["w_b"].T, sv["proj"], "dmerge")
        mixed = mix_bwd(dyb, sv["o_g"], sv["l_g"], "mix_bwd")
        do_g, dl_g = mixed[:N_GROUPS], mixed[N_GROUPS:]
        d_a, dsink = attn_bwd(sv["qkv_b"][0], sv["ya"], sv["lse_a"], dya.reshape(1, t, -1), None, ATTN_A,
                              sinks[l], "attn_a_bwd")
        small["sinks"][l] = dsink[:, 0]
        d_b = [attn_bwd(sv["qkv_b"][g], sv["o_g"][g], sv["l_g"][g], do_g[g], dl_g[g], cfg, sinks[l],
                        "attn_b%d_bwd" % g)[0] for g, cfg in enumerate(ATTN_B)]
        gw = wgrad(sv["u1"][0], d_a.reshape(t, A_W), fresh("w_in"), A_W, 1, 0, 0, 1, "wgrad_in_a")
        for g in range(N_GROUPS):
            gw = wgrad(sv["u1"][g], d_b[g].reshape(t, B_GW), gw, B_OUT_W, 3, 0, A_W // B_OUT_W + g, N_GROUPS,
                       "wgrad_in_b%d" % g)
        gbuf["w_in"] = wgrad(sv["u1"][0], dgab, gw, 512, 2 * dm // 512, 0, GATE_COL // 512, 1, "wgrad_in_gate")
        (dx, ss1, ssh1), got = dgrad_in(d_a, d_b, dgab, w["w_in"].T, dz1, sv["xin"], s1, "dgrad_in",
                                        scatter("mixer", gbuf))
        keep("mixer", l, got)
        small["dmod"][l] = jnp.stack([ssh1[0], ss1[0], sgate1[0], ssh2[0], ss2[0], sgate2[0]])
    grad_x = dx.reshape(x.shape)

    big_out = {}
    for nm, parts in recv.items():
        outs = adam_layers(parts, local(nm), local(nm, "m_"), local(nm, "v_"), "adam_" + nm)
        big_out[nm] = [o.transpose(0, 2, 1) for o in outs] if nm in TRANSPOSED else outs

    rows = jnp.concatenate(
        [jnp.stack(small["dmod"]).reshape(nl * 6, dm)]
        + [jnp.stack(small[k]) for k in ("ln1_g", "ln1_b", "ln2_g", "ln2_b")]
        + [jnp.pad(jnp.stack(small["sinks"]).reshape(1, -1), ((0, 0), (0, dm - nl * A_Q_HEADS))),
           jnp.broadcast_to(loss_part[0:1, 0:1], (1, dm))])
    n_rows = rows.shape[0]
    rows = jnp.pad(rows, ((0, -n_rows % 8), (0, 0)))
    rows_all = all_gather_small(rows, "gather_small_grads")

    def pack_small(pre):
        parts = [given[pre + "b_ada"].reshape(nl * 6, dm)]
        parts += [given[pre + k] for k in ("ln1_g", "ln1_b", "ln2_g", "ln2_b")]
        parts.append(jnp.pad(given[pre + "sinks"].reshape(1, -1), ((0, 0), (0, dm - nl * A_Q_HEADS))))
        p = jnp.concatenate(parts)
        return jnp.pad(p, ((0, rows.shape[0] - p.shape[0]), (0, 0)))

    souts = adam_reduce(rows_all, pack_small(""), pack_small("m_"), pack_small("v_"), "adam_small")

    def unpack_small(o):
        r = {"b_ada": o[0:nl * 6].reshape(nl, 6 * dm)}
        for j, k in enumerate(("ln1_g", "ln1_b", "ln2_g", "ln2_b")):
            r[k] = o[nl * 6 + j * nl: nl * 6 + (j + 1) * nl]
        r["sinks"] = o[nl * 10, 0:nl * A_Q_HEADS].reshape(nl, A_Q_HEADS)
        return r

    small_out = [unpack_small(o) for o in souts]
    loss = souts[0][nl * 10 + 1, 0]

    dmod_all = rows_all[:, 0:nl * 6].reshape(N_DEV, nl, 6 * dm)
    dm_loc = lax.dynamic_slice_in_dim(dmod_all, me * wc, wc, axis=2).transpose(1, 0, 2)
    dm_loc = jnp.pad(dm_loc, ((0, 0), (0, LANES - N_DEV), (0, 0)))
    sct = jnp.pad(sc_all.T, ((0, 0), (0, LANES - N_DEV)))
    ada_out = adam_w_ada(sct, dm_loc, w_ada, m_w_ada, v_w_ada)

    names = ["w_ada", "b_ada", "w_in", "sinks", "w_a", "w_b", "w_o", "ln1_g", "ln1_b",
             "w_gate", "w_up", "w_down", "ln2_g", "ln2_b"]

    def pick(kind, nm):
        if nm == "w_ada":
            return ada_out[kind]
        if nm in small_out[kind]:
            return small_out[kind][nm]
        return big_out[nm][kind]

    result = [loss, grad_x]
    for kind in range(4):
        result += [pick(kind, nm) for nm in names]
    return tuple(result)
```

```python
import jax
import jax.numpy as jnp
from jax import lax
from jax.experimental import pallas as pl
from jax.experimental.pallas import tpu as pltpu

F32 = jnp.float32
BF16 = jnp.bfloat16

D_MODEL = 1024
HEAD_DIM = 64
A_Q_HEADS = 8
A_KV_HEADS = 2
A_WINDOW = 128
B_GROUPS = ((128, 1), (512, 4), (2048, 16))
N_GROUPS = len(B_GROUPS)
B_HEADS_PER_GROUP = 4
N_ATTN_HEADS = A_Q_HEADS + B_HEADS_PER_GROUP * N_GROUPS
BLOCK = 128
ATTN_QUERY_BLOCK = 1024
A_W =(A_Q_HEADS + 2 * A_KV_HEADS) * HEAD_DIM
B_OUT_W = B_HEADS_PER_GROUP * HEAD_DIM
B_GW = 3 * B_OUT_W
B_ALL = N_GROUPS * B_OUT_W
GATE_COL = A_W + 3 * B_ALL
D_FF = 2816
COL_CHUNK = 256
FF_CHUNK = COL_CHUNK
DGRAD_CHUNK = COL_CHUNK
MM_CHUNK = 512
WGRAD_COLS = 512
ADAM_ROW_TILE = 256
DN_ALPHA = 8.0 ** 0.25
LN_EPS = 1e-5
NEG_INF = -1e30
ADAM_LR, ADAM_B1, ADAM_B2, ADAM_EPS, ADAM_WD, ADAM_STEP = 0.001, 0.9, 0.999, 1e-08, 0.01, 10

N_DEV = 8
MESH = pl.DeviceIdType.MESH
VMEM_LIMIT = 56 * 1024 * 1024
ROW_TILE = 512
WGRAD_TILE_ELEMS = 2 * 1024 * 1024
LANES = 128
BF16_ROWS = 16


def _cp(*sem):
    return pltpu.CompilerParams(dimension_semantics=sem, vmem_limit_bytes=VMEM_LIMIT)


def _row_tile(t):
    return min(ROW_TILE, t)


def _slope(head):
    return 2.0 ** (-8.0 * (head + 1) / N_ATTN_HEADS)


def _sigmoid(x):
    return 1.0 / (1.0 + jnp.exp(-x))


def _dot(a, b):
    return jnp.dot(a, b, preferred_element_type=F32)


def _dot_nt(a, b):
    return lax.dot_general(a, b, (((1,), (1,)), ((), ())), preferred_element_type=F32)


def _dot_tn(a, b):
    return lax.dot_general(a, b, (((0,), (0,)), ((), ())), preferred_element_type=F32)


def _fold_scratch(tm, w):
    return [pltpu.VMEM((tm, LANES), F32)] * (w // LANES)


def _fold_to(dst_ref, val, scrs, d, col0=0):
    tm, w = val.shape
    if d == 1:
        dst_ref[0, :, col0:col0 + w] = val.astype(dst_ref.dtype)
        return
    for cb in range(w // LANES):
        scrs[cb][...] = val[:, cb * LANES:(cb + 1) * LANES]
    for r in range(d):
        for cb in range(w // LANES):
            piece = scrs[cb][pl.ds(r, tm // d, stride=d), :]
            dst_ref[r, :, col0 + cb * LANES:col0 + (cb + 1) * LANES] = piece.astype(dst_ref.dtype)


def _unfold_rows(rows_of, scrs, d, n, w):
    for r in range(d):
        for cb in range(w // LANES):
            scrs[cb][pl.ds(r, n, stride=d), :] = rows_of(r, slice(cb * LANES, (cb + 1) * LANES)).astype(F32)
    return jnp.concatenate([scrs[cb][0:d * n, :] for cb in range(w // LANES)], axis=1)


def _unfold_from(src_ref, scrs, d):
    if d == 1:
        return src_ref[0].astype(F32)
    _, n, w = src_ref.shape
    return _unfold_rows(lambda r, cols: src_ref[r, :, cols], scrs, d, n, w)


def _folded_spec(d, tm, w):
    return pl.BlockSpec((d, tm // d, w), lambda i: (0, i, 0))


def _me():
    return lax.axis_index("x"), lax.axis_index("y"), lax.axis_index("c")


def _flip(v, bit):
    return 1 - v if bit else v


def _peer(k):
    x, y, c = _me()
    return (_flip(x, k & 4), _flip(y, k & 2), _flip(c, k & 1))


def _peer_index(k):
    px, py, pc = _peer(k)
    return 4 * px + 2 * py + pc


def all_gather_small(v, name):
    r, c = v.shape

    def body(v_ref, out_ref, send_sems, recv_sems):
        me = _peer_index(0)
        out_ref[me] = v_ref[...]
        copies = []
        for k in range(1, N_DEV):
            cp = pltpu.make_async_remote_copy(
                src_ref=v_ref, dst_ref=out_ref.at[me],
                send_sem=send_sems.at[k - 1], recv_sem=recv_sems.at[k - 1],
                device_id=_peer(k), device_id_type=MESH)
            cp.start()
            copies.append(cp)
        for k in range(1, N_DEV):
            pltpu.make_async_remote_copy(
                src_ref=v_ref, dst_ref=out_ref.at[_peer_index(k)],
                send_sem=send_sems.at[k - 1], recv_sem=recv_sems.at[k - 1],
                device_id=_peer(k), device_id_type=MESH).wait_recv()
        for cp in copies:
            cp.wait_send()

    return pl.pallas_call(
        body, name=name,
        out_shape=jax.ShapeDtypeStruct((N_DEV, r, c), v.dtype),
        in_specs=[pl.BlockSpec(memory_space=pltpu.VMEM)],
        out_specs=pl.BlockSpec(memory_space=pltpu.VMEM),
        scratch_shapes=[pltpu.SemaphoreType.DMA((N_DEV - 1,)), pltpu.SemaphoreType.DMA((N_DEV - 1,))],
        compiler_params=pltpu.CompilerParams(vmem_limit_bytes=VMEM_LIMIT),
    )(v)


class Piece:
    def __init__(self, name, buf, axis, base, size):
        self.name, self.buf, self.axis, self.base, self.size = name, buf, axis, base, size

    def window(self, ref, j):
        start = self.base + j * self.size
        if self.axis == 1:
            return ref.at[:, pl.ds(pl.multiple_of(start, LANES), self.size)]
        return ref.at[pl.ds(pl.multiple_of(start, BF16_ROWS), self.size), :]


class Exchange:
    def __init__(self, kind, pieces, ins, out_shapes, bufs):
        self.kind, self.pieces, self.ins, self.out_shapes = kind, pieces, list(ins), list(out_shapes)
        self.buf_of = {nm: i for i, nm in enumerate(bufs)}
        self.n_in, self.n_out = len(self.ins), len(self.out_shapes)
        n = len(pieces)
        self.scratch = [pltpu.SemaphoreType.DMA((n, N_DEV - 1)), pltpu.SemaphoreType.DMA((n, N_DEV - 1)),
                        pltpu.SemaphoreType.DMA((n,))]
        self.in_specs = [pl.BlockSpec(memory_space=pl.ANY)] * self.n_in
        self.out_specs = [pl.BlockSpec(memory_space=pl.ANY)] * self.n_out
        self.out_shape = [jax.ShapeDtypeStruct(s, BF16) for s in self.out_shapes]

    def _ends(self, pi, ins, outs, to):
        pc = self.pieces[pi]
        if self.kind == "gather":
            return ins[pi], pc.window(outs[self.buf_of[pc.buf]], _peer_index(0))
        return pc.window(ins[self.buf_of[pc.buf]], to), outs[pi].at[_peer_index(0)]

    def _landing(self, pi, outs, frm):
        pc = self.pieces[pi]
        if self.kind == "gather":
            return pc.window(outs[self.buf_of[pc.buf]], frm)
        return outs[pi].at[frm]

    def _remote(self, pi, k, src, dst, sems):
        return pltpu.make_async_remote_copy(
            src_ref=src, dst_ref=dst, send_sem=sems[0].at[pi, k - 1], recv_sem=sems[1].at[pi, k - 1],
            device_id=_peer(k), device_id_type=MESH)

    def _local(self, pi, ins, outs, sems):
        return pltpu.make_async_copy(*self._ends(pi, ins, outs, _peer_index(0)), sems[2].at[pi])

    def start(self, ins, outs, sems):
        for pi in range(len(self.pieces)):
            self._local(pi, ins, outs, sems).start()
            for k in range(1, N_DEV):
                self._remote(pi, k, *self._ends(pi, ins, outs, _peer_index(k)), sems).start()

    def finish(self, ins, outs, sems):
        for pi in range(len(self.pieces)):
            src_like = self._ends(pi, ins, outs, _peer_index(0))[0]
            for k in range(1, N_DEV):
                self._remote(pi, k, src_like, self._landing(pi, outs, _peer_index(k)), sems).wait_recv()
        for pi in range(len(self.pieces)):
            for k in range(1, N_DEV):
                self._remote(pi, k, *self._ends(pi, ins, outs, _peer_index(k)), sems).wait_send()
            self._local(pi, ins, outs, sems).wait()


def _hosted(ex, refs, n_in, n_out, first, last):
    if ex is None:
        return refs
    ins, rest = refs[:n_in], refs[n_in:]
    ex_ins, rest = rest[:ex.n_in], rest[ex.n_in:]
    outs, rest = rest[:n_out], rest[n_out:]
    ex_outs, rest = rest[:ex.n_out], rest[ex.n_out:]
    scr, sems = rest[:len(rest) - 3], rest[len(rest) - 3:]
    pl.when(first)(lambda: ex.start(ex_ins, ex_outs, sems))
    pl.when(last)(lambda: ex.finish(ex_ins, ex_outs, sems))
    return tuple(ins) + tuple(outs) + tuple(scr)


def _host_call(body, ex, *, name, grid, out_shape, in_specs, out_specs, scratch_shapes=(), sem=None, args):
    n_out = len(out_shape)
    if ex is not None:
        out_shape = list(out_shape) + ex.out_shape
        in_specs = list(in_specs) + ex.in_specs
        out_specs = list(out_specs) + ex.out_specs
        scratch_shapes = list(scratch_shapes) + ex.scratch
        args = list(args) + ex.ins
    res = pl.pallas_call(body, name=name, grid=grid, out_shape=out_shape, in_specs=in_specs, out_specs=out_specs,
                         scratch_shapes=scratch_shapes, compiler_params=_cp(*sem))(*args)
    return res[:n_out], res[n_out:]


def run_exchange(ex, name):
    def body(*refs):
        ins, outs, sems = refs[:ex.n_in], refs[ex.n_in:ex.n_in + ex.n_out], refs[ex.n_in + ex.n_out:]
        ex.start(ins, outs, sems)
        ex.finish(ins, outs, sems)

    return pl.pallas_call(body, name=name, out_shape=ex.out_shape, in_specs=ex.in_specs, out_specs=ex.out_specs,
                          scratch_shapes=ex.scratch)(*ex.ins)


def mod_partial(c_all, w_ada, b_loc):
    nl, dm, wc = w_ada.shape

    def body(c_ref, w_ref, b_ref, o_ref, sc_ref):
        cc = c_ref[...]
        sc = cc * _sigmoid(cc)
        sc_ref[...] = sc
        o_ref[...] = jnp.dot(sc, w_ref[...], preferred_element_type=F32,
                             precision=lax.Precision.HIGHEST) + b_ref[...]

    return pl.pallas_call(
        body, name="mod_partial", grid=(nl,),
        out_shape=[jax.ShapeDtypeStruct((nl, N_DEV, wc), F32), jax.ShapeDtypeStruct((N_DEV, dm), F32)],
        in_specs=[pl.BlockSpec((N_DEV, dm), lambda l: (0, 0)),
                  pl.BlockSpec((None, dm, wc), lambda l: (l, 0, 0)),
                  pl.BlockSpec((None, 1, wc), lambda l: (l, 0, 0))],
        out_specs=[pl.BlockSpec((None, N_DEV, wc), lambda l: (l, 0, 0)),
                   pl.BlockSpec((N_DEV, dm), lambda l: (0, 0))],
        compiler_params=_cp("arbitrary"),
    )(c_all, w_ada, b_loc)


def _stream_specs(tm, dm):
    vec = pl.BlockSpec((1, dm), lambda i: (0, 0))
    return [pl.BlockSpec((tm, dm), lambda i: (i, 0)), vec, vec]


def _stream(zh_ref, lg_ref, lb_ref, rows=slice(None), cols=slice(None)):
    return zh_ref[rows, cols] * lg_ref[:, cols] + lb_ref[:, cols]


def in_proj(x, s, sh, w, name, ex=None):
    t, dm = x[0].shape
    n = w.shape[1]
    tm = _row_tile(t)
    nsteps = t // tm
    ch = B_OUT_W
    dils = [dil for _, dil in B_GROUPS if dil > 1]

    def body(*refs):
        i = pl.program_id(0)
        zh_ref, lg_ref, lb_ref, s_ref, sh_ref, w_ref, u_ref, *rest = _hosted(
            ex, refs, 6, 2 + 2 * len(dils), i == 0, i == nsteps - 1)
        uf_refs, o_ref, qf_refs = rest[:len(dils)], rest[len(dils)], rest[len(dils) + 1:len(dils) * 2 + 1]
        scrs = rest[len(dils) * 2 + 1:]
        uf = _stream(zh_ref, lg_ref, lb_ref) * (1.0 + s_ref[...]) + sh_ref[...]
        u = uf.astype(BF16)
        u_ref[...] = u
        for d, uf_ref in zip(dils, uf_refs):
            _fold_to(uf_ref, uf, scrs, d)
        for c0 in range(0, n, ch):
            res = _dot(u, w_ref[:, c0:c0 + ch])
            o_ref[:, c0:c0 + ch] = res.astype(BF16)
            if A_W <= c0 < GATE_COL:
                part, g = divmod((c0 - A_W) // ch, N_GROUPS)
                d = B_GROUPS[g][1]
                if d > 1:
                    _fold_to(qf_refs[dils.index(d)], res, scrs, d, part * ch)

    vec = pl.BlockSpec((1, dm), lambda i: (0, 0))
    row = lambda w_: pl.BlockSpec((tm, w_), lambda i: (i, 0))
    return _host_call(
        body, ex, name=name, grid=(nsteps,),
        out_shape=[jax.ShapeDtypeStruct((t, dm), BF16)]
                  + [jax.ShapeDtypeStruct((d, t // d, dm), BF16) for d in dils]
                  + [jax.ShapeDtypeStruct((t, n), BF16)]
                  + [jax.ShapeDtypeStruct((d, t // d, B_GW), BF16) for d in dils],
        in_specs=_stream_specs(tm, dm) + [vec, vec, pl.BlockSpec((dm, n), lambda i: (0, 0))],
        out_specs=[row(dm)] + [_folded_spec(d, tm, dm) for d in dils] + [row(n)]
                  + [_folded_spec(d, tm, B_GW) for d in dils],
        scratch_shapes=_fold_scratch(tm, dm), sem=("arbitrary",), args=[*x, s, sh, w])


def modmm(x, s, sh, w, name, ex=None):
    t, dm = x[0].shape
    n = w.shape[1]
    tm = _row_tile(t)
    nsteps = t // tm
    ch = MM_CHUNK

    def body(*refs):
        i = pl.program_id(0)
        zh_ref, lg_ref, lb_ref, s_ref, sh_ref, w_ref, u_ref, o_ref = _hosted(ex, refs, 6, 2, i == 0, i == nsteps - 1)
        u = (_stream(zh_ref, lg_ref, lb_ref) * (1.0 + s_ref[...]) + sh_ref[...]).astype(BF16)
        u_ref[...] = u
        for c0 in range(0, n, ch):
            o_ref[:, c0:c0 + ch] = _dot(u, w_ref[:, c0:c0 + ch]).astype(BF16)

    vec = pl.BlockSpec((1, dm), lambda i: (0, 0))
    return _host_call(
        body, ex, name=name, grid=(nsteps,),
        out_shape=[jax.ShapeDtypeStruct((t, dm), BF16), jax.ShapeDtypeStruct((t, n), BF16)],
        in_specs=_stream_specs(tm, dm) + [vec, vec, pl.BlockSpec((dm, n), lambda i: (0, 0))],
        out_specs=[pl.BlockSpec((tm, dm), lambda i: (i, 0)), pl.BlockSpec((tm, n), lambda i: (i, 0))],
        sem=("arbitrary",), args=[*x, s, sh, w])


def _halves(tm):
    half = tm // 2 if tm % 32 == 0 else tm
    return [slice(r0, r0 + half) for r0 in range(0, tm, half)]


def _ln_store(y, rows, xres_refs, g_ref, y_ref, zh_ref, rs_ref):
    y_ref[rows, :] = y.astype(BF16)
    z = DN_ALPHA * _stream(*xres_refs, rows=rows) + g_ref[...] * y
    mu = jnp.mean(z, axis=1, keepdims=True)
    zc = z - mu
    var = jnp.mean(zc * zc, axis=1, keepdims=True)
    rstd = lax.rsqrt(var + LN_EPS)
    zh_ref[rows, :] = zc * rstd
    rs_ref[rows, :] = jnp.broadcast_to(rstd, (zc.shape[0], rs_ref.shape[1]))


def _ln_out_shapes(t, dm):
    return [jax.ShapeDtypeStruct((t, dm), BF16), jax.ShapeDtypeStruct((t, dm), F32),
            jax.ShapeDtypeStruct((t, LANES), F32)]


def _ln_out_specs(tm, dm):
    row = pl.BlockSpec((tm, dm), lambda i: (i, 0))
    return [row, row, pl.BlockSpec((tm, LANES), lambda i: (i, 0))]


def proj_ln(a, w, xres, gate, name):
    t, k = a.shape
    dm = w.shape[1]
    tm = _row_tile(t)

    def body(a_ref, w_ref, xz_ref, xg_ref, xb_ref, g_ref, y_ref, zh_ref, rs_ref):
        for rows in _halves(tm):
            y = _dot(a_ref[rows, :], w_ref[...])
            _ln_store(y, rows, (xz_ref, xg_ref, xb_ref), g_ref, y_ref, zh_ref, rs_ref)

    vec = pl.BlockSpec((1, dm), lambda i: (0, 0))
    return pl.pallas_call(
        body, name=name, grid=(t // tm,),
        out_shape=_ln_out_shapes(t, dm),
        in_specs=[pl.BlockSpec((tm, k), lambda i: (i, 0)), pl.BlockSpec((k, dm), lambda i: (0, 0))]
                 + _stream_specs(tm, dm) + [vec],
        out_specs=_ln_out_specs(tm, dm),
        compiler_params=_cp("parallel"),
    )(a, w, *xres, gate)


def swiglu_proj_ln(ab, w, xres, gate, name):
    t = ab.shape[0]
    f, dm = w.shape
    tm = _row_tile(t)

    def body(a_ref, b_ref, w_ref, xz_ref, xg_ref, xb_ref, g_ref, h_ref, y_ref, zh_ref, rs_ref):
        for rows in _halves(tm):
            y = None
            for c0 in range(0, f, FF_CHUNK):
                cols = slice(c0, c0 + FF_CHUNK)
                a = a_ref[rows, cols].astype(F32)
                h = (a * _sigmoid(a) * b_ref[rows, cols].astype(F32)).astype(BF16)
                h_ref[rows, cols] = h
                part = _dot(h, w_ref[cols, :])
                y = part if y is None else y + part
            _ln_store(y, rows, (xz_ref, xg_ref, xb_ref), g_ref, y_ref, zh_ref, rs_ref)

    vec = pl.BlockSpec((1, dm), lambda i: (0, 0))
    return pl.pallas_call(
        body, name=name, grid=(t // tm,),
        out_shape=[jax.ShapeDtypeStruct((t, f), BF16)] + _ln_out_shapes(t, dm),
        in_specs=[pl.BlockSpec((tm, f), lambda i: (i, 0)), pl.BlockSpec((tm, f), lambda i: (i, 1)),
                  pl.BlockSpec((f, dm), lambda i: (0, 0))] + _stream_specs(tm, dm) + [vec],
        out_specs=[pl.BlockSpec((tm, f), lambda i: (i, 0))] + _ln_out_specs(tm, dm),
        compiler_params=_cp("parallel"),
    )(ab, ab, w, *xres, gate)


class AttnCfg:
    def __init__(self, dil, heads, kv_heads, qc, kc, vc, max_dist, head0, sinks):
        self.dil, self.heads, self.kv_heads = dil, heads, kv_heads
        self.qc, self.kc, self.vc = qc, kc, vc
        self.max_dist, self.head0, self.sinks = max_dist, head0, sinks
        self.wq = heads * HEAD_DIM
        self.wk = kv_heads * HEAD_DIM
        self.wout = self.wq + 2 * self.wk


ATTN_A = AttnCfg(1, A_Q_HEADS, A_KV_HEADS, 0, A_Q_HEADS * HEAD_DIM, (A_Q_HEADS + A_KV_HEADS) * HEAD_DIM,
                 A_WINDOW - 1, 0, True)


def _attn_b_cfg(g):
    win, dil = B_GROUPS[g]
    cols = ((A_W + g * B_OUT_W, A_W + B_ALL + g * B_OUT_W, A_W + 2 * B_ALL + g * B_OUT_W) if dil == 1
            else (0, B_OUT_W, 2 * B_OUT_W))
    return AttnCfg(dil, B_HEADS_PER_GROUP, B_HEADS_PER_GROUP, *cols, win // dil,
                   A_Q_HEADS + g * B_HEADS_PER_GROUP, False)


ATTN_B = [_attn_b_cfg(g) for g in range(N_GROUPS)]


SCALE = HEAD_DIM ** -0.5


def _head(h):
    return slice(h * HEAD_DIM, (h + 1) * HEAD_DIM)


def _masked_bias(mask, distf, cfg, h, d):
    return jnp.where(mask, distf * (-(_slope(cfg.head0 + h) * d)), NEG_INF)


def _lane_halves(rows):
    lane = lax.broadcasted_iota(jnp.int32, (rows, LANES), 1)
    return [lane < HEAD_DIM, lane >= HEAD_DIM]


def _n_pair_sources(cfg):
    return cfg.kv_heads if cfg.heads > cfg.kv_heads else cfg.heads // 2


def _pair_source(p, grp):
    return p if grp == 1 else (2 * p) // grp


def _fill_pairs(dst, prev_ref, cur_ref, grp):
    for j in range(dst.shape[0]):
        for ref, rows in ((prev_ref, slice(0, BLOCK)), (cur_ref, slice(BLOCK, dst.shape[1]))):
            if grp == 1:
                dst[j, rows, :] = ref[:, j * LANES:(j + 1) * LANES]
            else:
                one = ref[:, _head(j)]
                dst[j, rows, :] = jnp.concatenate([one, one], axis=1)


def _band(i, max_dist):
    qi = lax.broadcasted_iota(jnp.int32, (BLOCK, 2 * BLOCK), 0)
    sj = lax.broadcasted_iota(jnp.int32, (BLOCK, 2 * BLOCK), 1)
    dist = qi + BLOCK - sj
    valid = (dist >= 0) & (dist <= max_dist)
    first_key = jnp.where(i > 0, 0, BLOCK)
    valid_first = valid & (sj >= first_key)
    return dist, valid, valid_first


def _attn_geometry(n):
    tq = min(ATTN_QUERY_BLOCK, n)
    return tq, tq // BLOCK, n // tq


def attn_fwd(qkv, cfg, sinks, name):
    d, n, _ = qkv.shape
    tq, nsub, nqb = _attn_geometry(n)
    wq, wk = cfg.wq, cfg.wk
    grp = cfg.heads // cfg.kv_heads

    def body(sink_ref, q_ref, kc_ref, kp_ref, vc_ref, vp_ref, o_ref, l_ref, kf, vf):
        i = pl.program_id(1)
        _fill_pairs(kf, kp_ref, kc_ref, grp)
        _fill_pairs(vf, vp_ref, vc_ref, grp)
        dist, valid, valid_first = _band(i, cfg.max_dist)
        distf = dist.astype(F32)
        half_q, half_k = _lane_halves(BLOCK), _lane_halves(2 * BLOCK)
        rows = [slice(a * BLOCK, (a + 1) * BLOCK) for a in range(nsub)]
        wins = [slice(a * BLOCK, (a + 2) * BLOCK) for a in range(nsub)]
        for p in range(cfg.heads // 2):
            lanes = slice(p * LANES, (p + 1) * LANES)
            ki = _pair_source(p, grp)
            hs = (2 * p, 2 * p + 1)
            b_reg = [_masked_bias(valid, distf, cfg, h, d) for h in hs]
            b_first = [_masked_bias(valid_first, distf, cfg, h, d) for h in hs]
            ss = []
            for a in range(nsub):
                q2 = q_ref[rows[a], lanes] * SCALE
                k2 = kf[ki, wins[a], :]
                ss.append([_dot_nt(jnp.where(half_q[e], q2, 0), k2) + (b_first[e] if a == 0 else b_reg[e])
                           for e in range(2)])
            es, invs, lses = [], [], []
            for a in range(nsub):
                e_a, inv_a, lse_a = [], [], []
                for e in range(2):
                    m = jnp.max(ss[a][e], axis=1, keepdims=True)
                    if cfg.sinks:
                        m = jnp.maximum(m, sink_ref[hs[e]])
                    ex = jnp.exp(ss[a][e] - m)
                    den = jnp.sum(ex, axis=1, keepdims=True)
                    if cfg.sinks:
                        den = den + jnp.exp(sink_ref[hs[e]] - m)
                    e_a.append(ex.astype(BF16))
                    inv_a.append(1.0 / den)
                    lse_a.append(m + jnp.log(den))
                es.append(e_a)
                invs.append(inv_a)
                lses.append(lse_a)
            for a in range(nsub):
                v2 = vf[ki, wins[a], :]
                pcat = jnp.concatenate(es[a], axis=1)
                vcat = jnp.concatenate([jnp.where(half_k[e], v2, 0) for e in range(2)], axis=0)
                o = _dot(pcat, vcat) * jnp.where(half_q[0], invs[a][0], invs[a][1])
                o_ref[rows[a], lanes] = o.astype(BF16)
                l_ref[rows[a], lanes] = jnp.where(half_q[0], lses[a][0], lses[a][1])

    prev = lambda i: jnp.maximum(i * nsub - 1, 0)
    cur = lambda w, c: pl.BlockSpec((None, tq, w), lambda r, i: (r, i, c // w))
    prv = lambda w, c: pl.BlockSpec((None, BLOCK, w), lambda r, i: (r, prev(i), c // w))
    out = pl.BlockSpec((None, tq, wq), lambda r, i: (r, i, 0))
    pair_scratch = pltpu.VMEM((_n_pair_sources(cfg), tq + BLOCK, LANES), BF16)
    return pl.pallas_call(
        body, name=name, grid=(d, nqb),
        out_shape=[jax.ShapeDtypeStruct((d, n, wq), BF16), jax.ShapeDtypeStruct((d, n, wq), F32)],
        in_specs=[pl.BlockSpec(memory_space=pltpu.SMEM),
                  cur(wq, cfg.qc), cur(wk, cfg.kc), prv(wk, cfg.kc), cur(wk, cfg.vc), prv(wk, cfg.vc)],
        out_specs=[out, out],
        scratch_shapes=[pair_scratch, pair_scratch],
        compiler_params=_cp("parallel", "parallel"),
    )(sinks, qkv, qkv, qkv, qkv, qkv)


def mix_merge(ya, o_g, l_g, proj, w_a, w_b, name):
    t = ya.shape[0]
    dm = w_a.shape[1]
    tm = _row_tile(t)
    gcol = GATE_COL // dm
    dils = [o.shape[0] for o in o_g]

    def body(ya_ref, o0, o1, o2, l0, l1, l2, ga_ref, gb_ref, wa_ref, wb_ref, yb_ref, mg_ref, *scrs):
        ls = [_unfold_from(l, scrs, d) for l, d in zip((l0, l1, l2), dils)]
        m = jnp.maximum(jnp.maximum(ls[0], ls[1]), ls[2])
        es = [jnp.exp(l - m) for l in ls]
        inv = 1.0 / (es[0] + es[1] + es[2])
        yb = sum(_unfold_from(o, scrs, d) * (e * inv) for o, e, d in zip((o0, o1, o2), es, dils)).astype(BF16)
        yb_ref[...] = yb
        pa = _dot(ya_ref[...], wa_ref[...])
        pb = _dot(yb, wb_ref[...])
        mg = _sigmoid(ga_ref[...].astype(F32)) * pa + _sigmoid(gb_ref[...].astype(F32)) * pb
        mg_ref[...] = mg.astype(BF16)

    wide = lambda w: pl.BlockSpec((tm, w), lambda i: (i, 0))
    folded = [_folded_spec(d, tm, B_OUT_W) for d in dils]
    return pl.pallas_call(
        body, name=name, grid=(t // tm,),
        out_shape=[jax.ShapeDtypeStruct((t, B_OUT_W), BF16), jax.ShapeDtypeStruct((t, dm), BF16)],
        in_specs=[wide(ya.shape[1])] + folded + folded
                 + [pl.BlockSpec((tm, dm), lambda i: (i, gcol)), pl.BlockSpec((tm, dm), lambda i: (i, gcol + 1)),
                    pl.BlockSpec(w_a.shape, lambda i: (0, 0)), pl.BlockSpec(w_b.shape, lambda i: (0, 0))],
        out_specs=[wide(B_OUT_W), wide(dm)],
        scratch_shapes=_fold_scratch(tm, B_OUT_W),
        compiler_params=_cp("parallel"),
    )(ya, *o_g, *l_g, proj, proj, w_a, w_b)


def loss_head(y, target):
    t, dm = y[0].shape
    tm = _row_tile(t)

    def body(zh_ref, lg_ref, lb_ref, t_ref, dy_ref, loss_ref):
        @pl.when(pl.program_id(0) == 0)
        def _():
            loss_ref[...] = jnp.zeros_like(loss_ref)
        err = _stream(zh_ref, lg_ref, lb_ref) - t_ref[...]
        dy_ref[...] = err * (1.0 / dm)
        per_row = jnp.sum(err * err, axis=1, keepdims=True) * (1.0 / dm)
        loss_ref[...] += 0.5 * jnp.sum(per_row, axis=0, keepdims=True)

    row = pl.BlockSpec((tm, dm), lambda i: (i, 0))
    return pl.pallas_call(
        body, name="loss_head", grid=(t // tm,),
        out_shape=[jax.ShapeDtypeStruct((t, dm), F32), jax.ShapeDtypeStruct((8, LANES), F32)],
        in_specs=_stream_specs(tm, dm) + [row],
        out_specs=[row, pl.BlockSpec((8, LANES), lambda i: (0, 0))],
        compiler_params=_cp("arbitrary"),
    )(*y, target)


def _fold_rows(v):
    tm, c = v.shape
    return jnp.sum(v.reshape(tm // 8, 8, c), axis=0)


def _finish_sums(refs, nsteps):
    @pl.when(pl.program_id(0) == nsteps - 1)
    def _():
        for r in refs:
            r[...] = jnp.broadcast_to(jnp.sum(r[...], axis=0, keepdims=True), r.shape)


def ln_bwd(dxo, zhat, rstd, ysub, lg, gate, act, name):
    t, dm = dxo.shape
    k = act.shape[1]
    tm = _row_tile(t)
    nsteps = t // tm
    ch = COL_CHUNK

    def body(dxo_ref, zh_ref, rs_ref, y_ref, lg_ref, g_ref, a_ref, dz_ref, dy_ref, sg_ref, sb_ref, sgate_ref,
             gw_ref, acc):
        @pl.when(pl.program_id(0) == 0)
        def _():
            for r in (sg_ref, sb_ref, sgate_ref, acc):
                r[...] = jnp.zeros_like(r)
        for rows in _halves(tm):
            dxo_v = dxo_ref[rows, :]
            zh = zh_ref[rows, :]
            dxh = dxo_v * lg_ref[...]
            m1 = jnp.mean(dxh, axis=1, keepdims=True)
            m2 = jnp.mean(dxh * zh, axis=1, keepdims=True)
            dz = rs_ref[rows, 0:1] * (dxh - m1 - zh * m2)
            dz_ref[rows, :] = dz
            dy = (g_ref[...] * dz).astype(BF16)
            dy_ref[rows, :] = dy
            sg_ref[...] += _fold_rows(dxo_v * zh)
            sb_ref[...] += _fold_rows(dxo_v)
            sgate_ref[...] += _fold_rows(dz * y_ref[rows, :].astype(F32))
            a = a_ref[rows, :]
            for c0 in range(0, dm, ch):
                acc[:, c0:c0 + ch] += _dot_tn(a, dy[:, c0:c0 + ch])
        _finish_sums((sg_ref, sb_ref, sgate_ref), nsteps)

        @pl.when(pl.program_id(0) == nsteps - 1)
        def _():
            gw_ref[...] = acc[...].astype(BF16)

    row = pl.BlockSpec((tm, dm), lambda i: (i, 0))
    vec = pl.BlockSpec((1, dm), lambda i: (0, 0))
    sums = pl.BlockSpec((8, dm), lambda i: (0, 0))
    return pl.pallas_call(
        body, name=name, grid=(nsteps,),
        out_shape=[jax.ShapeDtypeStruct((t, dm), F32), jax.ShapeDtypeStruct((t, dm), BF16)]
                  + [jax.ShapeDtypeStruct((8, dm), F32)] * 3 + [jax.ShapeDtypeStruct((k, dm), BF16)],
        in_specs=[row, row, pl.BlockSpec((tm, LANES), lambda i: (i, 0)), row, vec, vec,
                  pl.BlockSpec((tm, k), lambda i: (i, 0))],
        out_specs=[row, row, sums, sums, sums, pl.BlockSpec((k, dm), lambda i: (0, 0))],
        scratch_shapes=[pltpu.VMEM((k, dm), F32)],
        compiler_params=_cp("arbitrary"),
    )(dxo, zhat, rstd, ysub, lg, gate, act)


def _mod_bwd_store(du_of, dz_ref, x_refs, s_ref, dx_ref, ss_ref, ssh_ref, nsteps):
    @pl.when(pl.program_id(0) == 0)
    def _():
        ss_ref[...] = jnp.zeros_like(ss_ref)
        ssh_ref[...] = jnp.zeros_like(ssh_ref)
    for c0 in range(0, dx_ref.shape[1], DGRAD_CHUNK):
        cols = slice(c0, c0 + DGRAD_CHUNK)
        du = du_of(cols)
        dx_ref[:, cols] = DN_ALPHA * dz_ref[:, cols] + du * (1.0 + s_ref[:, cols])
        ss_ref[:, cols] += _fold_rows(du * _stream(*x_refs, cols=cols))
        ssh_ref[:, cols] += _fold_rows(du)
    _finish_sums((ss_ref, ssh_ref), nsteps)


def dgrad_ffn(g, wt, dz, xin, s, name, ex=None):
    t, dm = dz.shape
    k = g.shape[1]
    tm = _row_tile(t)
    nsteps = t // tm

    def body(*refs):
        i = pl.program_id(0)
        (g_ref, w_ref, dz_ref, xz_ref, xg_ref, xb_ref, s_ref,
         dx_ref, ss_ref, ssh_ref) = _hosted(ex, refs, 7, 3, i == 0, i == nsteps - 1)
        g_v = g_ref[...]
        _mod_bwd_store(lambda cols: _dot(g_v, w_ref[:, cols]), dz_ref, (xz_ref, xg_ref, xb_ref), s_ref,
                       dx_ref, ss_ref, ssh_ref, nsteps)

    row = pl.BlockSpec((tm, dm), lambda i: (i, 0))
    acc = pl.BlockSpec((8, dm), lambda i: (0, 0))
    return _host_call(
        body, ex, name=name, grid=(nsteps,),
        out_shape=[jax.ShapeDtypeStruct((t, dm), F32)] + [jax.ShapeDtypeStruct((8, dm), F32)] * 2,
        in_specs=[pl.BlockSpec((tm, k), lambda i: (i, 0)), pl.BlockSpec((k, dm), lambda i: (0, 0)), row]
                 + _stream_specs(tm, dm) + [pl.BlockSpec((1, dm), lambda i: (0, 0))],
        out_specs=[row, acc, acc], sem=("arbitrary",), args=[g, wt, dz, *xin, s])


def dswiglu(dy, wdt, ab, name):
    t, dm = dy.shape
    f = wdt.shape[1]
    tm = _row_tile(t)

    def body(dy_ref, w_ref, a_ref, b_ref, o_ref):
        dy_v = dy_ref[...]
        for c0 in range(0, f, FF_CHUNK):
            cols = slice(c0, c0 + FF_CHUNK)
            dh = _dot(dy_v, w_ref[:, cols])
            a = a_ref[:, cols].astype(F32)
            sg = _sigmoid(a)
            o_ref[:, cols] = (dh * b_ref[:, cols].astype(F32) * (sg * (1.0 + a * (1.0 - sg)))).astype(BF16)
            o_ref[:, f + c0:f + c0 + FF_CHUNK] = (dh * (a * sg)).astype(BF16)

    return pl.pallas_call(
        body, name=name, grid=(t // tm,),
        out_shape=jax.ShapeDtypeStruct((t, 2 * f), BF16),
        in_specs=[pl.BlockSpec((tm, dm), lambda i: (i, 0)), pl.BlockSpec((dm, f), lambda i: (0, 0)),
                  pl.BlockSpec((tm, f), lambda i: (i, 0)), pl.BlockSpec((tm, f), lambda i: (i, 1))],
        out_specs=pl.BlockSpec((tm, 2 * f), lambda i: (i, 0)),
        compiler_params=_cp("parallel"),
    )(dy, wdt, ab, ab)


def dgrad_in(d_a, d_b, dgab, wt, dz, xin, s, name, ex=None):
    t, dm = dz.shape
    tm = _row_tile(t)
    nsteps = t // tm
    dils = [a.shape[0] for a in d_b]

    def body(*refs):
        i = pl.program_id(0)
        (da_ref, b0, b1, b2, dg_ref, w_ref, dz_ref, xz_ref, xg_ref, xb_ref, s_ref, dx_ref, ss_ref, ssh_ref,
         *scrs) = _hosted(ex, refs, 11, 3, i == 0, i == nsteps - 1)
        vs = [b_ref[...].reshape(tm, B_GW) for b_ref in (b0, b1, b2)]

        def du_of(cols):
            du = _dot(da_ref[0], w_ref[0:A_W, cols])
            for g, (v, d) in enumerate(zip(vs, dils)):
                part = None
                for p in range(3):
                    r0 = A_W + p * B_ALL + g * B_OUT_W
                    term = _dot(v[:, p * B_OUT_W:(p + 1) * B_OUT_W], w_ref[r0:r0 + B_OUT_W, cols])
                    part = term if part is None else part + term
                if d == 1:
                    du = du + part
                else:
                    n = tm // d
                    du = du + _unfold_rows(lambda r, cs: part[r * n:(r + 1) * n, cs], scrs, d, n, DGRAD_CHUNK)
            for j in range(2):
                du = du + _dot(dg_ref[:, j * dm:(j + 1) * dm], w_ref[GATE_COL + j * dm:GATE_COL + (j + 1) * dm, cols])
            return du

        _mod_bwd_store(du_of, dz_ref, (xz_ref, xg_ref, xb_ref), s_ref, dx_ref, ss_ref, ssh_ref, nsteps)

    row = pl.BlockSpec((tm, dm), lambda i: (i, 0))
    acc = pl.BlockSpec((8, dm), lambda i: (0, 0))
    return _host_call(
        body, ex, name=name, grid=(nsteps,),
        out_shape=[jax.ShapeDtypeStruct((t, dm), F32)] + [jax.ShapeDtypeStruct((8, dm), F32)] * 2,
        in_specs=[_folded_spec(1, tm, A_W)] + [_folded_spec(d, tm, B_GW) for d in dils]
                 + [pl.BlockSpec((tm, 2 * dm), lambda i: (i, 0)), pl.BlockSpec(wt.shape, lambda i: (0, 0)), row]
                 + _stream_specs(tm, dm) + [pl.BlockSpec((1, dm), lambda i: (0, 0))],
        out_specs=[row, acc, acc],
        scratch_shapes=_fold_scratch(tm, DGRAD_CHUNK), sem=("arbitrary",), args=[d_a, *d_b, dgab, wt, dz, *xin, s])


def wgrad(a, b, buf, tn, nj, b0, o0, om, name):
    t, k = a.shape
    tt = ROW_TILE
    while tt * 2 * k <= WGRAD_TILE_ELEMS and tt * 2 <= t:
        tt *= 2
    nsteps = t // tt
    last = nsteps - 1

    def body(a_ref, b_ref, buf_ref, o_ref, acc):
        s, j = pl.program_id(0), pl.program_id(1)

        @pl.when(s == 0)
        def _():
            acc[j] = jnp.zeros(acc.shape[1:], F32)
        acc[j] += _dot_tn(a_ref[...], b_ref[...])

        @pl.when(s == last)
        def _():
            o_ref[...] = acc[j].astype(BF16)

    return pl.pallas_call(
        body, name=name, grid=(nsteps, nj),
        out_shape=jax.ShapeDtypeStruct(buf.shape, buf.dtype),
        in_specs=[pl.BlockSpec((tt, k), lambda s, j: (s, 0)),
                  pl.BlockSpec((tt, tn), lambda s, j: (s, b0 + j)),
                  pl.BlockSpec(memory_space=pl.ANY)],
        out_specs=pl.BlockSpec((k, tn), lambda s, j: (0, o0 + om * jnp.where(s == last, j, 0))),
        scratch_shapes=[pltpu.VMEM((nj, k, tn), F32)],
        input_output_aliases={2: 0},
        compiler_params=_cp("arbitrary", "arbitrary"),
    )(a, b, buf)


def dmerge(do, wot, ya, yb, w_a, w_b, wat, wbt, proj, name):
    t, dm = do.shape
    tm = _row_tile(t)
    nsteps = t // tm
    gcol = GATE_COL // dm
    ch = COL_CHUNK

    def body(do_ref, wot_ref, ya_ref, yb_ref, wa_ref, wb_ref, wat_ref, wbt_ref, g_ref,
             dya_ref, dyb_ref, dg_ref, gwa_ref, gwb_ref, dm_scr, acc_a, acc_b):
        i, j = pl.program_id(0), pl.program_id(1)

        @pl.when((i == 0) & (j == 0))
        def _():
            acc_a[...] = jnp.zeros_like(acc_a)
            acc_b[...] = jnp.zeros_like(acc_b)

        @pl.when(j == 0)
        def _():
            do_v = do_ref[...]
            for c0 in range(0, dm, ch):
                dm_scr[:, c0:c0 + ch] = _dot(do_v, wot_ref[:, c0:c0 + ch])

        def branch(y_ref, w_ref, wt_ref, dy_ref, acc, gw_ref):
            y = y_ref[...]
            dy = None
            for c0 in range(0, dm, ch):
                cols = slice(c0, c0 + ch)
                p = _dot(y, w_ref[:, cols])
                sg = _sigmoid(g_ref[:, cols].astype(F32))
                dmg = dm_scr[:, cols]
                dp = (dmg * sg).astype(BF16)
                dg_ref[:, cols] = (dmg * p * (sg * (1.0 - sg))).astype(BF16)
                acc[:, cols] += _dot_tn(y, dp)
                part = _dot(dp, wt_ref[cols, :])
                dy = part if dy is None else dy + part
            dy_ref[...] = dy.astype(dy_ref.dtype)

            @pl.when(i == nsteps - 1)
            def _():
                gw_ref[...] = acc[...].astype(BF16)

        pl.when(j == 0)(lambda: branch(ya_ref, wa_ref, wat_ref, dya_ref, acc_a, gwa_ref))
        pl.when(j == 1)(lambda: branch(yb_ref, wb_ref, wbt_ref, dyb_ref, acc_b, gwb_ref))

    full = lambda arr: pl.BlockSpec(arr.shape, lambda i, j: (0, 0))
    rowc = lambda w: pl.BlockSpec((tm, w), lambda i, j: (i, 0))
    return pl.pallas_call(
        body, name=name, grid=(nsteps, 2),
        out_shape=[jax.ShapeDtypeStruct((t, ya.shape[1]), BF16), jax.ShapeDtypeStruct((t, yb.shape[1]), F32),
                   jax.ShapeDtypeStruct((t, 2 * dm), BF16),
                   jax.ShapeDtypeStruct(w_a.shape, BF16), jax.ShapeDtypeStruct(w_b.shape, BF16)],
        in_specs=[rowc(dm), full(wot), rowc(ya.shape[1]), rowc(yb.shape[1]), full(w_a), full(w_b),
                  full(wat), full(wbt), pl.BlockSpec((tm, dm), lambda i, j: (i, gcol + j))],
        out_specs=[rowc(ya.shape[1]), rowc(yb.shape[1]), pl.BlockSpec((tm, dm), lambda i, j: (i, j)),
                   full(w_a), full(w_b)],
        scratch_shapes=[pltpu.VMEM((tm, dm), F32), pltpu.VMEM(w_a.shape, F32), pltpu.VMEM(w_b.shape, F32)],
        compiler_params=_cp("arbitrary", "arbitrary"),
    )(do, wot, ya, yb, w_a, w_b, wat, wbt, proj)


def mix_bwd(dyb, o_g, l_g, name):
    t, w = dyb.shape
    tm = _row_tile(t)
    nh = w // HEAD_DIM
    dils = [o.shape[0] for o in o_g]

    def body(dyb_ref, o0, o1, o2, l0, l1, l2, do0, do1, do2, dl0, dl1, dl2, *scr):
        ls = [_unfold_from(l, scr, d) for l, d in zip((l0, l1, l2), dils)]
        m = jnp.maximum(jnp.maximum(ls[0], ls[1]), ls[2])
        es = [jnp.exp(l - m) for l in ls]
        inv = 1.0 / (es[0] + es[1] + es[2])
        wts = [e * inv for e in es]
        dyb_v = dyb_ref[...]
        dws = []
        for o_ref, do_ref, wt, d in zip((o0, o1, o2), (do0, do1, do2), wts, dils):
            prod = dyb_v * _unfold_from(o_ref, scr, d)
            _fold_to(do_ref, dyb_v * wt, scr, d)
            for h in range(nh):
                hs = slice(h * HEAD_DIM, (h + 1) * HEAD_DIM)
                dws.append(jnp.broadcast_to(jnp.sum(prod[:, hs], axis=1, keepdims=True), (tm, HEAD_DIM)))
        for g, (dl_ref, d) in enumerate(zip((dl0, dl1, dl2), dils)):
            cols = []
            for h in range(nh):
                hs = slice(h * HEAD_DIM, (h + 1) * HEAD_DIM)
                mean = sum(wts[g2][:, hs] * dws[g2 * nh + h] for g2 in range(N_GROUPS))
                cols.append(wts[g][:, hs] * (dws[g * nh + h] - mean))
            _fold_to(dl_ref, jnp.concatenate(cols, axis=1), scr, d)

    folded = [_folded_spec(d, tm, w) for d in dils]
    return pl.pallas_call(
        body, name=name, grid=(t // tm,),
        out_shape=[jax.ShapeDtypeStruct(o.shape, BF16) for o in o_g]
                  + [jax.ShapeDtypeStruct(o.shape, F32) for o in o_g],
        in_specs=[pl.BlockSpec((tm, w), lambda i: (i, 0))] + folded + folded,
        out_specs=folded + folded,
        scratch_shapes=_fold_scratch(tm, w),
        compiler_params=_cp("parallel"),
    )(dyb, *o_g, *l_g)


def attn_bwd(qkv, o, lse, do, dlse, cfg, sinks, name):
    d, n, _ = qkv.shape
    tq, nsub, nqb = _attn_geometry(n)
    wq, wk, wout = cfg.wq, cfg.wk, cfg.wout
    grp = cfg.heads // cfg.kv_heads
    has_dl = dlse is not None

    def body(*refs):
        sink_ref, q_ref, qn_ref, kc_ref, kp_ref, vc_ref, vp_ref = refs[:7]
        o_ref, on_ref, do_ref, don_ref, l_ref, ln_ref = refs[7:13]
        rest = refs[13:]
        dl_ref = dln_ref = None
        if has_dl:
            dl_ref, dln_ref = rest[:2]
            rest = rest[2:]
        out_ref = rest[0]
        rest = rest[1:]
        if cfg.sinks:
            dsink_ref = rest[0]
            rest = rest[1:]
        kf, vf = rest
        r, i = pl.program_id(0), pl.program_id(1)
        _fill_pairs(kf, kp_ref, kc_ref, grp)
        _fill_pairs(vf, vp_ref, vc_ref, grp)
        dist, valid, valid_first = _band(i, cfg.max_dist)
        distf = dist.astype(F32)
        next_dist = jnp.where(i < nqb - 1, cfg.max_dist, -1)
        valid_next = (dist[:, 0:BLOCK] >= 0) & (dist[:, 0:BLOCK] <= next_dist)
        half_q = _lane_halves(BLOCK)
        if cfg.sinks:
            @pl.when((r == 0) & (i == 0))
            def _():
                dsink_ref[...] = jnp.zeros_like(dsink_ref)

        shared = {}
        for p in range(cfg.heads // 2):
            lanes = slice(p * LANES, (p + 1) * LANES)
            ki = _pair_source(p, grp)
            hs = (2 * p, 2 * p + 1)
            biases = [[_masked_bias(m, dd, cfg, h, d) for h in hs]
                      for m, dd in ((valid_first, distf), (valid, distf), (valid_next, distf[:, 0:BLOCK]))]
            tiles = []
            for a in range(nsub + 1):
                if a < nsub:
                    rows, win = slice(a * BLOCK, (a + 1) * BLOCK), slice(a * BLOCK, (a + 2) * BLOCK)
                    src = (q_ref, o_ref, do_ref, l_ref, dl_ref)
                else:
                    rows, win = slice(0, BLOCK), slice(nsub * BLOCK, (nsub + 1) * BLOCK)
                    src = (qn_ref, on_ref, don_ref, ln_ref, dln_ref)
                q2 = src[0][rows, lanes] * SCALE
                do2 = src[2][rows, lanes]
                o2 = src[1][rows, lanes].astype(F32)
                l2 = src[3][rows, lanes]
                k2, v2 = kf[ki, win, :], vf[ki, win, :]
                per = []
                for e in range(2):
                    qe, doe = jnp.where(half_q[e], q2, 0), jnp.where(half_q[e], do2, 0)
                    delta = jnp.sum(doe.astype(F32) * o2, axis=1, keepdims=True)
                    lse_v = jnp.max(jnp.where(half_q[e], l2, NEG_INF), axis=1, keepdims=True)
                    shift = -delta
                    if has_dl:
                        shift = shift + jnp.max(jnp.where(half_q[e], src[4][rows, lanes], NEG_INF), axis=1,
                                                keepdims=True)
                    s = _dot_nt(qe, k2) + biases[0 if a == 0 else (1 if a < nsub else 2)][e]
                    per.append((qe, doe, delta, lse_v, shift, s, _dot_nt(doe, v2)))
                tiles.append((k2, per))
            grads = []
            for k2, per in tiles:
                both = []
                for qe, doe, delta, lse_v, shift, s, dp in per:
                    pr = jnp.exp(s - lse_v)
                    both.append(((pr * (dp + shift)).astype(BF16), pr.astype(BF16)))
                grads.append(both)
            dkt, dvt = [], []
            for a, ((k2, per), both) in enumerate(zip(tiles, grads)):
                if a < nsub:
                    half_k = _lane_halves(k2.shape[0])
                    ds_cat = jnp.concatenate([both[0][0], both[1][0]], axis=1)
                    k_cat = jnp.concatenate([jnp.where(half_k[e], k2, 0) for e in range(2)], axis=0)
                    out_ref[a * BLOCK:(a + 1) * BLOCK, lanes] = (_dot(ds_cat, k_cat) * SCALE).astype(BF16)
                cut = (lambda x: x[:, BLOCK:]) if a == 0 else (lambda x: x)
                q_cat = jnp.concatenate([per[0][0], per[1][0]], axis=0)
                do_cat = jnp.concatenate([per[0][1], per[1][1]], axis=0)
                dkt.append(_dot_tn(q_cat, jnp.concatenate([cut(both[0][0]), cut(both[1][0])], axis=0)))
                dvt.append(_dot_tn(do_cat, jnp.concatenate([cut(both[0][1]), cut(both[1][1])], axis=0)))
                if cfg.sinks and a < nsub:
                    for e in range(2):
                        psink = jnp.exp(sink_ref[hs[e]] - per[e][3])
                        tot = jnp.sum(psink * (-per[e][2]), axis=0, keepdims=True)
                        dsink_ref[hs[e]:hs[e] + 1, :] += jnp.broadcast_to(tot, (1, LANES))
            for m in range(nsub):
                rows = slice(m * BLOCK, (m + 1) * BLOCK)
                for which, (acc, col0) in enumerate(((dkt, wq), (dvt, wq + wk))):
                    own = acc[m] if m == 0 else acc[m][:, BLOCK:]
                    total = own + acc[m + 1][:, 0:BLOCK]
                    if grp == 1:
                        out_ref[rows, col0 + p * LANES:col0 + (p + 1) * LANES] = total.T.astype(BF16)
                    else:
                        t64 = total[0:HEAD_DIM] + total[HEAD_DIM:]
                        key = (ki, which, m)
                        shared[key] = t64 + shared[key] if key in shared else t64
        for (ki, which, m), t64 in shared.items():
            col0 = (wq, wq + wk)[which] + ki * HEAD_DIM
            out_ref[m * BLOCK:(m + 1) * BLOCK, col0:col0 + HEAD_DIM] = t64.T.astype(BF16)

    prev = lambda i: jnp.maximum(i * nsub - 1, 0)
    nxt = lambda i: jnp.minimum((i + 1) * nsub, n // BLOCK - 1)
    cur = lambda w, c: pl.BlockSpec((None, tq, w), lambda r, i: (r, i, c // w))
    prv = lambda w, c: pl.BlockSpec((None, BLOCK, w), lambda r, i: (r, prev(i), c // w))
    o_cur = pl.BlockSpec((None, tq, wq), lambda r, i: (r, i, 0))
    o_nxt = pl.BlockSpec((None, BLOCK, wq), lambda r, i: (r, nxt(i), 0))
    in_specs = [pl.BlockSpec(memory_space=pltpu.SMEM),
                cur(wq, cfg.qc), pl.BlockSpec((None, BLOCK, wq), lambda r, i: (r, nxt(i), cfg.qc // wq)),
                cur(wk, cfg.kc), prv(wk, cfg.kc), cur(wk, cfg.vc), prv(wk, cfg.vc),
                o_cur, o_nxt, o_cur, o_nxt, o_cur, o_nxt]
    args = [sinks, qkv, qkv, qkv, qkv, qkv, qkv, o, o, do, do, lse, lse]
    if has_dl:
        in_specs += [o_cur, o_nxt]
        args += [dlse, dlse]
    out_shape = [jax.ShapeDtypeStruct((d, n, wout), BF16)]
    out_specs = [pl.BlockSpec((None, tq, wout), lambda r, i: (r, i, 0))]
    if cfg.sinks:
        out_shape.append(jax.ShapeDtypeStruct((8, LANES), F32))
        out_specs.append(pl.BlockSpec((8, LANES), lambda r, i: (0, 0)))
    pair_scratch = pltpu.VMEM((_n_pair_sources(cfg), tq + BLOCK, LANES), BF16)
    return pl.pallas_call(
        body, name=name, grid=(d, nqb), out_shape=out_shape, in_specs=in_specs, out_specs=out_specs,
        scratch_shapes=[pair_scratch, pair_scratch],
        compiler_params=_cp("arbitrary", "arbitrary"),
    )(*args)


def _adamw(g, w, m, v):
    m = ADAM_B1 * m + (1.0 - ADAM_B1) * g
    v = ADAM_B2 * v + (1.0 - ADAM_B2) * (g * g)
    m_hat = m / (1.0 - ADAM_B1 ** ADAM_STEP)
    v_hat = v / (1.0 - ADAM_B2 ** ADAM_STEP)
    delta = -ADAM_LR * (m_hat / (jnp.sqrt(v_hat) + ADAM_EPS) + ADAM_WD * w)
    return delta, m, v


def _adam_rows(r):
    if r <= ADAM_ROW_TILE:
        return r
    return next(rows for rows in range(ADAM_ROW_TILE, 0, -8) if r % rows == 0)


def adam_reduce(parts, w, m, v, name):
    r, c = w.shape
    tr = _adam_rows(r)

    def body(p_ref, w_ref, m_ref, v_ref, g_ref, d_ref, mo_ref, vo_ref):
        g = p_ref[0].astype(F32)
        for j in range(1, N_DEV):
            g = g + p_ref[j].astype(F32)
        g_ref[...] = g
        d_ref[...], mo_ref[...], vo_ref[...] = _adamw(g, w_ref[...], m_ref[...], v_ref[...])

    row = pl.BlockSpec((tr, c), lambda i: (i, 0))
    return pl.pallas_call(
        body, name=name, grid=(r // tr,),
        out_shape=[jax.ShapeDtypeStruct((r, c), F32)] * 4,
        in_specs=[pl.BlockSpec((N_DEV, tr, c), lambda i: (0, i, 0)), row, row, row],
        out_specs=[row] * 4,
        compiler_params=_cp("parallel"),
    )(parts, w, m, v)


def adam_layers(parts, w, m, v, name):
    nl, r, c = w.shape
    tr = _adam_rows(r)
    steps = r // tr

    def body(*refs):
        p_refs = refs[:nl]
        w_ref, m_ref, v_ref, g_ref, d_ref, mo_ref, vo_ref = refs[nl:]
        for k in range(nl):
            @pl.when(pl.program_id(0) == k)
            def _():
                g = p_refs[k][0].astype(F32)
                for j in range(1, N_DEV):
                    g = g + p_refs[k][j].astype(F32)
                g_ref[...] = g
                d_ref[...], mo_ref[...], vo_ref[...] = _adamw(g, w_ref[...], m_ref[...], v_ref[...])

    def part_spec(k):
        return pl.BlockSpec((N_DEV, tr, c), lambda l, i: (0, jnp.clip(i + (l - k) * steps, 0, steps - 1), 0))

    blk = pl.BlockSpec((None, tr, c), lambda l, i: (l, i, 0))
    return pl.pallas_call(
        body, name=name, grid=(nl, steps),
        out_shape=[jax.ShapeDtypeStruct((nl, r, c), F32)] * 4,
        in_specs=[part_spec(k) for k in range(nl)] + [blk, blk, blk],
        out_specs=[blk] * 4,
        compiler_params=_cp("arbitrary", "arbitrary"),
    )(*parts, w, m, v)


def adam_w_ada(sct, dm_loc, w, m, v):
    nl, dm, wc = w.shape
    tr = _row_tile(dm)

    def body(s_ref, d_ref, w_ref, m_ref, v_ref, g_ref, dl_ref, mo_ref, vo_ref):
        g = jnp.dot(s_ref[...], d_ref[...], preferred_element_type=F32, precision=lax.Precision.HIGHEST)
        g_ref[...] = g
        dl_ref[...], mo_ref[...], vo_ref[...] = _adamw(g, w_ref[...], m_ref[...], v_ref[...])

    blk = pl.BlockSpec((None, tr, wc), lambda l, i: (l, i, 0))
    return pl.pallas_call(
        body, name="adam_w_ada", grid=(nl, dm // tr),
        out_shape=[jax.ShapeDtypeStruct(w.shape, F32)] * 4,
        in_specs=[pl.BlockSpec((tr, LANES), lambda l, i: (i, 0)),
                  pl.BlockSpec((None, LANES, wc), lambda l, i: (l, 0, 0)), blk, blk, blk],
        out_specs=[blk] * 4,
        compiler_params=_cp("parallel", "parallel"),
    )(sct, dm_loc, w, m, v)


TRANSPOSED = ("w_gate", "w_up")


def _pieces(dm):
    ncol = lambda n: n // N_DEV
    mixer = ([Piece("w_in", "w_in", 1, 0, ncol(GATE_COL + 2 * dm)),
              Piece("w_a", "w_a", 1, 0, ncol(dm)),
              Piece("w_b", "w_b", 1, 0, ncol(dm)),
              Piece("w_o", "w_o", 0, 0, ncol(dm))],
             {"w_in": (dm, GATE_COL + 2 * dm), "w_a": (A_Q_HEADS * HEAD_DIM, dm), "w_b": (B_OUT_W, dm),
              "w_o": (dm, dm)})
    ffn = ([Piece("w_gate", "w_ffn_t", 0, 0, ncol(D_FF)),
            Piece("w_up", "w_ffn_t", 0, D_FF, ncol(D_FF)),
            Piece("w_down", "w_down", 0, 0, ncol(D_FF))],
           {"w_ffn_t": (2 * D_FF, dm), "w_down": (D_FF, dm)})
    return mixer, ffn


def kernel(x, c, w_ada, b_ada, w_in, sinks, w_a, w_b, w_o, ln1_g, ln1_b, w_gate, w_up, w_down, ln2_g, ln2_b, loss_target, m_w_ada, m_b_ada, m_w_in, m_sinks, m_w_a, m_w_b, m_w_o, m_ln1_g, m_ln1_b, m_w_gate, m_w_up, m_w_down, m_ln2_g, m_ln2_b, v_w_ada, v_b_ada, v_w_in, v_sinks, v_w_a, v_w_b, v_w_o, v_ln1_g, v_ln1_b, v_w_gate, v_w_up, v_w_down, v_ln2_g, v_ln2_b):
    given = dict(locals())
    nl = w_in.shape[0]
    t, dm = x.shape[1], x.shape[2]
    me = 4 * lax.axis_index("x") + 2 * lax.axis_index("y") + lax.axis_index("c")
    x0 = x.reshape(t, dm)
    target = loss_target.reshape(t, dm)

    groups = dict(zip(("mixer", "ffn"), _pieces(dm)))
    local = lambda nm, pre="": (given[pre + nm].transpose(0, 2, 1) if nm in TRANSPOSED else given[pre + nm])
    shards = {pc.name: local(pc.name).astype(BF16) for pcs, _ in groups.values() for pc in pcs}

    def gather(group, l):
        pcs, bufs = groups[group]
        return Exchange("gather", pcs, [shards[pc.name][l] for pc in pcs], bufs.values(), bufs)

    def scatter(group, gbuf):
        pcs, bufs = groups[group]
        return Exchange("scatter", pcs, [gbuf[nm] for nm in bufs],
                        [(N_DEV,) + shards[pc.name].shape[1:] for pc in pcs], bufs)

    mixer_pcs, mixer_bufs = groups["mixer"]
    groups["first"] = (mixer_pcs[:1], {"w_in": mixer_bufs["w_in"]})
    groups["rest"] = (mixer_pcs[1:] + groups["ffn"][0],
                      {**{k: v for k, v in mixer_bufs.items() if k != "w_in"}, **groups["ffn"][1]})
    full = [dict() for _ in range(nl)]
    full[0].update(zip(groups["first"][1], run_exchange(gather("first", 0), "gather_w_in")))

    wc = w_ada.shape[2]
    c_all = all_gather_small(jnp.broadcast_to(c, (8, dm)), "gather_c")[:, 0, :]
    b_loc = lax.dynamic_slice_in_dim(b_ada, me * wc, wc, axis=1).reshape(nl, 1, wc)
    mp, sc_all = mod_partial(c_all, w_ada, b_loc)
    mp_all = all_gather_small(mp.reshape(nl * N_DEV, wc), "gather_mod").reshape(N_DEV, nl, N_DEV, wc)
    mod = lax.dynamic_index_in_dim(mp_all, me, axis=2, keepdims=False)
    mod = mod.transpose(1, 0, 2).reshape(nl, 6, 1, dm)

    vec = lambda a, l: a[l].reshape(1, dm)

    saved = []
    xl = (x0, jnp.ones((1, dm), F32), jnp.zeros((1, dm), F32))
    for l in range(nl):
        sh1, s1, g1, sh2, s2, g2 = [mod[l, j] for j in range(6)]
        w = full[l]
        hosted = "rest" if l == 0 else "ffn"
        (u1, u1_f4, u1_f16, proj, qkv_f4, qkv_f16), got = in_proj(xl, s1, sh1, w["w_in"], "in_proj",
                                                                   gather(hosted, l))
        w.update(zip(groups[hosted][1], got))
        proj3 = proj.reshape(1, t, proj.shape[1])
        qkv_b = [proj3, qkv_f4, qkv_f16]
        ya, lse_a = attn_fwd(proj3, ATTN_A, sinks[l], "attn_a_fwd")
        o_g, l_g = [], []
        for g, cfg in enumerate(ATTN_B):
            o, ls = attn_fwd(qkv_b[g], cfg, sinks[l], "attn_b%d_fwd" % g)
            o_g.append(o)
            l_g.append(ls)
        yb, merged = mix_merge(ya[0], o_g, l_g, proj, w["w_a"], w["w_b"], "mix_merge")
        y1, zh1, rs1 = proj_ln(merged, w["w_o"], xl, g1, "out_proj_ln")
        x1 = (zh1, vec(ln1_g, l), vec(ln1_b, l))
        (u2, ab), got = modmm(x1, s2, sh2, w["w_ffn_t"].T, "ffn_up", gather("mixer", l + 1) if l + 1 < nl else None)
        if l + 1 < nl:
            full[l + 1].update(zip(groups["mixer"][1], got))
        h, y2, zh2, rs2 = swiglu_proj_ln(ab, w["w_down"], x1, g2, "ffn_down_ln")
        x2 = (zh2, vec(ln2_g, l), vec(ln2_b, l))
        saved.append(dict(xin=xl, u1=[u1, u1_f4.reshape(t, dm), u1_f16.reshape(t, dm)], proj=proj, qkv_b=qkv_b,
                          ya=ya, lse_a=lse_a, o_g=o_g, l_g=l_g, yb=yb, merged=merged,
                          y1=y1, x1=x1, zh1=zh1, rs1=rs1, u2=u2, ab=ab, h=h, y2=y2, zh2=zh2, rs2=rs2))
        xl = x2

    dx, loss_part = loss_head(xl, target)

    small = {k: [None] * nl for k in ("dmod", "ln1_g", "ln1_b", "ln2_g", "ln2_b", "sinks")}
    recv = {nm: [None] * nl for grp in groups.values() for nm in (pc.name for pc in grp[0])}

    def keep(group, l, got):
        for pc, arr in zip(groups[group][0], got):
            recv[pc.name][l] = arr

    for l in reversed(range(nl)):
        sv, w = saved[l], full[l]
        sh1, s1, g1, sh2, s2, g2 = [mod[l, j] for j in range(6)]
        fresh = lambda nm: lax.empty({**groups["mixer"][1], **groups["ffn"][1]}[nm], BF16)
        gbuf = {}
        dz2, dy2, sg, sb, sgate2, gbuf["w_down"] = ln_bwd(dx, sv["zh2"], sv["rs2"], sv["y2"], vec(ln2_g, l), g2,
                                                          sv["h"], "ln_bwd_ffn")
        small["ln2_g"][l], small["ln2_b"][l] = sg[0], sb[0]
        dab = dswiglu(dy2, w["w_down"].T, sv["ab"], "dswiglu")
        gbuf["w_ffn_t"] = wgrad(dab, sv["u2"], fresh("w_ffn_t"), WGRAD_COLS, dm // WGRAD_COLS, 0, 0, 1,
                                "wgrad_ffn_up")
        (dx1, ss2, ssh2), got = dgrad_ffn(dab, w["w_ffn_t"], dz2, sv["x1"], s2, "dgrad_ffn", scatter("ffn", gbuf))
        keep("ffn", l, got)
        dz1, do1, sg, sb, sgate1, gbuf["w_o"] = ln_bwd(dx1, sv["zh1"], sv["rs1"], sv["y1"], vec(ln1_g, l), g1,
                                                       sv["merged"], "ln_bwd_mixer")
        small["ln1_g"][l], small["ln1_b"][l] = sg[0], sb[0]
        dya, dyb, dgab, gbuf["w_a"], gbuf["w_b"] = dmerge(
            do1, w["w_o"].T, sv["ya"][0], sv["yb"], w["w_a"], w["w_b"], w["w_a"].T, w["w_b"].T, sv["proj"], "dmerge")
        mixed = mix_bwd(dyb, sv["o_g"], sv["l_g"], "mix_bwd")
        do_g, dl_g = mixed[:N_GROUPS], mixed[N_GROUPS:]
        d_a, dsink = attn_bwd(sv["qkv_b"][0], sv["ya"], sv["lse_a"], dya.reshape(1, t, -1), None, ATTN_A,
                              sinks[l], "attn_a_bwd")
        small["sinks"][l] = dsink[:, 0]
        d_b = [attn_bwd(sv["qkv_b"][g], sv["o_g"][g], sv["l_g"][g], do_g[g], dl_g[g], cfg, sinks[l],
                        "attn_b%d_bwd" % g)[0] for g, cfg in enumerate(ATTN_B)]
        gw = wgrad(sv["u1"][0], d_a.reshape(t, A_W), fresh("w_in"), A_W, 1, 0, 0, 1, "wgrad_in_a")
        for g in range(N_GROUPS):
            gw = wgrad(sv["u1"][g], d_b[g].reshape(t, B_GW), gw, B_OUT_W, 3, 0, A_W // B_OUT_W + g, N_GROUPS,
                       "wgrad_in_b%d" % g)
        gbuf["w_in"] = wgrad(sv["u1"][0], dgab, gw, WGRAD_COLS, 2 * dm // WGRAD_COLS, 0, GATE_COL // WGRAD_COLS, 1,
                             "wgrad_in_gate")
        (dx, ss1, ssh1), got = dgrad_in(d_a, d_b, dgab, w["w_in"].T, dz1, sv["xin"], s1, "dgrad_in",
                                        scatter("mixer", gbuf))
        keep("mixer", l, got)
        small["dmod"][l] = jnp.stack([ssh1[0], ss1[0], sgate1[0], ssh2[0], ss2[0], sgate2[0]])
    grad_x = dx.reshape(x.shape)

    big_out = {}
    for nm, parts in recv.items():
        outs = adam_layers(parts, local(nm), local(nm, "m_"), local(nm, "v_"), "adam_" + nm)
        big_out[nm] = [o.transpose(0, 2, 1) for o in outs] if nm in TRANSPOSED else outs

    rows = jnp.concatenate(
        [jnp.stack(small["dmod"]).reshape(nl * 6, dm)]
        + [jnp.stack(small[k]) for k in ("ln1_g", "ln1_b", "ln2_g", "ln2_b")]
        + [jnp.pad(jnp.stack(small["sinks"]).reshape(1, -1), ((0, 0), (0, dm - nl * A_Q_HEADS))),
           jnp.broadcast_to(loss_part[0:1, 0:1], (1, dm))])
    n_rows = rows.shape[0]
    rows = jnp.pad(rows, ((0, -n_rows % 8), (0, 0)))
    rows_all = all_gather_small(rows, "gather_small_grads")

    def pack_small(pre):
        parts = [given[pre + "b_ada"].reshape(nl * 6, dm)]
        parts += [given[pre + k] for k in ("ln1_g", "ln1_b", "ln2_g", "ln2_b")]
        parts.append(jnp.pad(given[pre + "sinks"].reshape(1, -1), ((0, 0), (0, dm - nl * A_Q_HEADS))))
        p = jnp.concatenate(parts)
        return jnp.pad(p, ((0, rows.shape[0] - p.shape[0]), (0, 0)))

    souts = adam_reduce(rows_all, pack_small(""), pack_small("m_"), pack_small("v_"), "adam_small")

    def unpack_small(o):
        r = {"b_ada": o[0:nl * 6].reshape(nl, 6 * dm)}
        for j, k in enumerate(("ln1_g", "ln1_b", "ln2_g", "ln2_b")):
            r[k] = o[nl * 6 + j * nl: nl * 6 + (j + 1) * nl]
        r["sinks"] = o[nl * 10, 0:nl * A_Q_HEADS].reshape(nl, A_Q_HEADS)
        return r

    small_out = [unpack_small(o) for o in souts]
    loss = souts[0][nl * 10 + 1, 0]

    dmod_all = rows_all[:, 0:nl * 6].reshape(N_DEV, nl, 6 * dm)
    dm_loc = lax.dynamic_slice_in_dim(dmod_all, me * wc, wc, axis=2).transpose(1, 0, 2)
    dm_loc = jnp.pad(dm_loc, ((0, 0), (0, LANES - N_DEV), (0, 0)))
    sct = jnp.pad(sc_all.T, ((0, 0), (0, LANES - N_DEV)))
    ada_out = adam_w_ada(sct, dm_loc, w_ada, m_w_ada, v_w_ada)

    names = ["w_ada", "b_ada", "w_in", "sinks", "w_a", "w_b", "w_o", "ln1_g", "ln1_b",
             "w_gate", "w_up", "w_down", "ln2_g", "ln2_b"]

    def pick(kind, nm):
        if nm == "w_ada":
            return ada_out[kind]
        if nm in small_out[kind]:
            return small_out[kind][nm]
        return big_out[nm][kind]

    result = [loss, grad_x]
    for kind in range(4):
        result += [pick(kind, nm) for nm in names]
    return tuple(result)
```

```python
import jax
import jax.numpy as jnp
from jax import lax
from jax.experimental import pallas as pl
from jax.experimental.pallas import tpu as pltpu

F32 = jnp.float32
BF16 = jnp.bfloat16

D_MODEL = 1024
HEAD_DIM = 64
A_Q_HEADS = 8
A_KV_HEADS = 2
A_WINDOW = 128
B_GROUPS = ((128, 1), (512, 4), (2048, 16))
N_GROUPS = len(B_GROUPS)
B_HEADS_PER_GROUP = 4
N_ATTN_HEADS = A_Q_HEADS + B_HEADS_PER_GROUP * N_GROUPS
BLOCK = 128
ATTN_QUERY_BLOCK = 1024
A_W =(A_Q_HEADS + 2 * A_KV_HEADS) * HEAD_DIM
B_OUT_W = B_HEADS_PER_GROUP * HEAD_DIM
B_GW = 3 * B_OUT_W
B_ALL = N_GROUPS * B_OUT_W
GATE_COL = A_W + 3 * B_ALL
D_FF = 2816
COL_CHUNK = 256
FF_CHUNK = COL_CHUNK
DGRAD_CHUNK = COL_CHUNK
MM_CHUNK = 512
WGRAD_COLS = 512
ADAM_ROW_TILE = 256
DN_ALPHA = 8.0 ** 0.25
LN_EPS = 1e-5
NEG_INF = -1e30
ADAM_LR, ADAM_B1, ADAM_B2, ADAM_EPS, ADAM_WD, ADAM_STEP = 0.001, 0.9, 0.999, 1e-08, 0.01, 10

N_DEV = 8
MESH = pl.DeviceIdType.MESH
VMEM_LIMIT = 56 * 1024 * 1024
ROW_TILE = 512
WIDE_ROW_TILE = 1024
WGRAD_TILE_ELEMS = 2 * 1024 * 1024
LANES = 128
BF16_ROWS = 16


def _cp(*sem):
    return pltpu.CompilerParams(dimension_semantics=sem, vmem_limit_bytes=VMEM_LIMIT)


def _row_tile(t, rows=ROW_TILE):
    return min(rows, t)


def _slope(head):
    return 2.0 ** (-8.0 * (head + 1) / N_ATTN_HEADS)


def _sigmoid(x):
    return 1.0 / (1.0 + jnp.exp(-x))


def _dot(a, b):
    return jnp.dot(a, b, preferred_element_type=F32)


def _dot_nt(a, b):
    return lax.dot_general(a, b, (((1,), (1,)), ((), ())), preferred_element_type=F32)


def _dot_tn(a, b):
    return lax.dot_general(a, b, (((0,), (0,)), ((), ())), preferred_element_type=F32)


def _fold_scratch(tm, w):
    return [pltpu.VMEM((tm, LANES), F32)] * (w // LANES)


def _fold_to(dst_ref, val, scrs, d, col0=0):
    tm, w = val.shape
    if d == 1:
        dst_ref[0, :, col0:col0 + w] = val.astype(dst_ref.dtype)
        return
    for cb in range(w // LANES):
        scrs[cb][...] = val[:, cb * LANES:(cb + 1) * LANES]
    for r in range(d):
        for cb in range(w // LANES):
            piece = scrs[cb][pl.ds(r, tm // d, stride=d), :]
            dst_ref[r, :, col0 + cb * LANES:col0 + (cb + 1) * LANES] = piece.astype(dst_ref.dtype)


def _unfold_rows(rows_of, scrs, d, n, w):
    for r in range(d):
        for cb in range(w // LANES):
            scrs[cb][pl.ds(r, n, stride=d), :] = rows_of(r, slice(cb * LANES, (cb + 1) * LANES)).astype(F32)
    return jnp.concatenate([scrs[cb][0:d * n, :] for cb in range(w // LANES)], axis=1)


def _unfold_from(src_ref, scrs, d):
    if d == 1:
        return src_ref[0].astype(F32)
    _, n, w = src_ref.shape
    return _unfold_rows(lambda r, cols: src_ref[r, :, cols], scrs, d, n, w)


def _folded_spec(d, tm, w):
    return pl.BlockSpec((d, tm // d, w), lambda i: (0, i, 0))


def _me():
    return lax.axis_index("x"), lax.axis_index("y"), lax.axis_index("c")


def _flip(v, bit):
    return 1 - v if bit else v


def _peer(k):
    x, y, c = _me()
    return (_flip(x, k & 4), _flip(y, k & 2), _flip(c, k & 1))


def _peer_index(k):
    px, py, pc = _peer(k)
    return 4 * px + 2 * py + pc


def all_gather_small(v, name):
    r, c = v.shape

    def body(v_ref, out_ref, send_sems, recv_sems):
        me = _peer_index(0)
        out_ref[me] = v_ref[...]
        copies = []
        for k in range(1, N_DEV):
            cp = pltpu.make_async_remote_copy(
                src_ref=v_ref, dst_ref=out_ref.at[me],
                send_sem=send_sems.at[k - 1], recv_sem=recv_sems.at[k - 1],
                device_id=_peer(k), device_id_type=MESH)
            cp.start()
            copies.append(cp)
        for k in range(1, N_DEV):
            pltpu.make_async_remote_copy(
                src_ref=v_ref, dst_ref=out_ref.at[_peer_index(k)],
                send_sem=send_sems.at[k - 1], recv_sem=recv_sems.at[k - 1],
                device_id=_peer(k), device_id_type=MESH).wait_recv()
        for cp in copies:
            cp.wait_send()

    return pl.pallas_call(
        body, name=name,
        out_shape=jax.ShapeDtypeStruct((N_DEV, r, c), v.dtype),
        in_specs=[pl.BlockSpec(memory_space=pltpu.VMEM)],
        out_specs=pl.BlockSpec(memory_space=pltpu.VMEM),
        scratch_shapes=[pltpu.SemaphoreType.DMA((N_DEV - 1,)), pltpu.SemaphoreType.DMA((N_DEV - 1,))],
        compiler_params=pltpu.CompilerParams(vmem_limit_bytes=VMEM_LIMIT),
    )(v)


class Piece:
    def __init__(self, name, buf, axis, base, size):
        self.name, self.buf, self.axis, self.base, self.size = name, buf, axis, base, size

    def window(self, ref, j):
        start = self.base + j * self.size
        if self.axis == 1:
            return ref.at[:, pl.ds(pl.multiple_of(start, LANES), self.size)]
        return ref.at[pl.ds(pl.multiple_of(start, BF16_ROWS), self.size), :]


class Exchange:
    def __init__(self, kind, pieces, ins, out_shapes, bufs):
        self.kind, self.pieces, self.ins, self.out_shapes = kind, pieces, list(ins), list(out_shapes)
        self.buf_of = {nm: i for i, nm in enumerate(bufs)}
        self.n_in, self.n_out = len(self.ins), len(self.out_shapes)
        n = len(pieces)
        self.scratch = [pltpu.SemaphoreType.DMA((n, N_DEV - 1)), pltpu.SemaphoreType.DMA((n, N_DEV - 1)),
                        pltpu.SemaphoreType.DMA((n,))]
        self.in_specs = [pl.BlockSpec(memory_space=pl.ANY)] * self.n_in
        self.out_specs = [pl.BlockSpec(memory_space=pl.ANY)] * self.n_out
        self.out_shape = [jax.ShapeDtypeStruct(s, BF16) for s in self.out_shapes]

    def _ends(self, pi, ins, outs, to):
        pc = self.pieces[pi]
        if self.kind == "gather":
            return ins[pi], pc.window(outs[self.buf_of[pc.buf]], _peer_index(0))
        return pc.window(ins[self.buf_of[pc.buf]], to), outs[pi].at[_peer_index(0)]

    def _landing(self, pi, outs, frm):
        pc = self.pieces[pi]
        if self.kind == "gather":
            return pc.window(outs[self.buf_of[pc.buf]], frm)
        return outs[pi].at[frm]

    def _remote(self, pi, k, src, dst, sems):
        return pltpu.make_async_remote_copy(
            src_ref=src, dst_ref=dst, send_sem=sems[0].at[pi, k - 1], recv_sem=sems[1].at[pi, k - 1],
            device_id=_peer(k), device_id_type=MESH)

    def _local(self, pi, ins, outs, sems):
        return pltpu.make_async_copy(*self._ends(pi, ins, outs, _peer_index(0)), sems[2].at[pi])

    def start(self, ins, outs, sems):
        for pi in range(len(self.pieces)):
            self._local(pi, ins, outs, sems).start()
            for k in range(1, N_DEV):
                self._remote(pi, k, *self._ends(pi, ins, outs, _peer_index(k)), sems).start()

    def finish(self, ins, outs, sems):
        for pi in range(len(self.pieces)):
            src_like = self._ends(pi, ins, outs, _peer_index(0))[0]
            for k in range(1, N_DEV):
                self._remote(pi, k, src_like, self._landing(pi, outs, _peer_index(k)), sems).wait_recv()
        for pi in range(len(self.pieces)):
            for k in range(1, N_DEV):
                self._remote(pi, k, *self._ends(pi, ins, outs, _peer_index(k)), sems).wait_send()
            self._local(pi, ins, outs, sems).wait()


def _hosted(ex, refs, n_in, n_out, first, last):
    if ex is None:
        return refs
    ins, rest = refs[:n_in], refs[n_in:]
    ex_ins, rest = rest[:ex.n_in], rest[ex.n_in:]
    outs, rest = rest[:n_out], rest[n_out:]
    ex_outs, rest = rest[:ex.n_out], rest[ex.n_out:]
    scr, sems = rest[:len(rest) - 3], rest[len(rest) - 3:]
    pl.when(first)(lambda: ex.start(ex_ins, ex_outs, sems))
    pl.when(last)(lambda: ex.finish(ex_ins, ex_outs, sems))
    return tuple(ins) + tuple(outs) + tuple(scr)


def _host_call(body, ex, *, name, grid, out_shape, in_specs, out_specs, scratch_shapes=(), sem=None, args):
    n_out = len(out_shape)
    if ex is not None:
        out_shape = list(out_shape) + ex.out_shape
        in_specs = list(in_specs) + ex.in_specs
        out_specs = list(out_specs) + ex.out_specs
        scratch_shapes = list(scratch_shapes) + ex.scratch
        args = list(args) + ex.ins
    res = pl.pallas_call(body, name=name, grid=grid, out_shape=out_shape, in_specs=in_specs, out_specs=out_specs,
                         scratch_shapes=scratch_shapes, compiler_params=_cp(*sem))(*args)
    return res[:n_out], res[n_out:]


def run_exchange(ex, name):
    def body(*refs):
        ins, outs, sems = refs[:ex.n_in], refs[ex.n_in:ex.n_in + ex.n_out], refs[ex.n_in + ex.n_out:]
        ex.start(ins, outs, sems)
        ex.finish(ins, outs, sems)

    return pl.pallas_call(body, name=name, out_shape=ex.out_shape, in_specs=ex.in_specs, out_specs=ex.out_specs,
                          scratch_shapes=ex.scratch)(*ex.ins)


def mod_partial(c_all, w_ada, b_loc):
    nl, dm, wc = w_ada.shape

    def body(c_ref, w_ref, b_ref, o_ref, sc_ref):
        cc = c_ref[...]
        sc = cc * _sigmoid(cc)
        sc_ref[...] = sc
        o_ref[...] = jnp.dot(sc, w_ref[...], preferred_element_type=F32,
                             precision=lax.Precision.HIGHEST) + b_ref[...]

    return pl.pallas_call(
        body, name="mod_partial", grid=(nl,),
        out_shape=[jax.ShapeDtypeStruct((nl, N_DEV, wc), F32), jax.ShapeDtypeStruct((N_DEV, dm), F32)],
        in_specs=[pl.BlockSpec((N_DEV, dm), lambda l: (0, 0)),
                  pl.BlockSpec((None, dm, wc), lambda l: (l, 0, 0)),
                  pl.BlockSpec((None, 1, wc), lambda l: (l, 0, 0))],
        out_specs=[pl.BlockSpec((None, N_DEV, wc), lambda l: (l, 0, 0)),
                   pl.BlockSpec((N_DEV, dm), lambda l: (0, 0))],
        compiler_params=_cp("arbitrary"),
    )(c_all, w_ada, b_loc)


def _stream_specs(tm, dm):
    vec = pl.BlockSpec((1, dm), lambda i: (0, 0))
    return [pl.BlockSpec((tm, dm), lambda i: (i, 0)), vec, vec]


def _stream(zh_ref, lg_ref, lb_ref, rows=slice(None), cols=slice(None)):
    return zh_ref[rows, cols] * lg_ref[:, cols] + lb_ref[:, cols]


def in_proj(x, s, sh, w, name, ex=None):
    t, dm = x[0].shape
    n = w.shape[1]
    tm = _row_tile(t)
    nsteps = t // tm
    ch = B_OUT_W
    dils = [dil for _, dil in B_GROUPS if dil > 1]

    def body(*refs):
        i = pl.program_id(0)
        zh_ref, lg_ref, lb_ref, s_ref, sh_ref, w_ref, u_ref, *rest = _hosted(
            ex, refs, 6, 2 + 2 * len(dils), i == 0, i == nsteps - 1)
        uf_refs, o_ref, qf_refs = rest[:len(dils)], rest[len(dils)], rest[len(dils) + 1:len(dils) * 2 + 1]
        scrs = rest[len(dils) * 2 + 1:]
        uf = _stream(zh_ref, lg_ref, lb_ref) * (1.0 + s_ref[...]) + sh_ref[...]
        u = uf.astype(BF16)
        u_ref[...] = u
        for d, uf_ref in zip(dils, uf_refs):
            _fold_to(uf_ref, uf, scrs, d)
        for c0 in range(0, n, ch):
            res = _dot(u, w_ref[:, c0:c0 + ch])
            o_ref[:, c0:c0 + ch] = res.astype(BF16)
            if A_W <= c0 < GATE_COL:
                part, g = divmod((c0 - A_W) // ch, N_GROUPS)
                d = B_GROUPS[g][1]
                if d > 1:
                    _fold_to(qf_refs[dils.index(d)], res, scrs, d, part * ch)

    vec = pl.BlockSpec((1, dm), lambda i: (0, 0))
    row = lambda w_: pl.BlockSpec((tm, w_), lambda i: (i, 0))
    return _host_call(
        body, ex, name=name, grid=(nsteps,),
        out_shape=[jax.ShapeDtypeStruct((t, dm), BF16)]
                  + [jax.ShapeDtypeStruct((d, t // d, dm), BF16) for d in dils]
                  + [jax.ShapeDtypeStruct((t, n), BF16)]
                  + [jax.ShapeDtypeStruct((d, t // d, B_GW), BF16) for d in dils],
        in_specs=_stream_specs(tm, dm) + [vec, vec, pl.BlockSpec((dm, n), lambda i: (0, 0))],
        out_specs=[row(dm)] + [_folded_spec(d, tm, dm) for d in dils] + [row(n)]
                  + [_folded_spec(d, tm, B_GW) for d in dils],
        scratch_shapes=_fold_scratch(tm, dm), sem=("arbitrary",), args=[*x, s, sh, w])


def modmm(x, s, sh, w, name, ex=None):
    t, dm = x[0].shape
    n = w.shape[1]
    tm = _row_tile(t)
    nsteps = t // tm
    ch = MM_CHUNK

    def body(*refs):
        i = pl.program_id(0)
        zh_ref, lg_ref, lb_ref, s_ref, sh_ref, w_ref, u_ref, o_ref = _hosted(ex, refs, 6, 2, i == 0, i == nsteps - 1)
        u = (_stream(zh_ref, lg_ref, lb_ref) * (1.0 + s_ref[...]) + sh_ref[...]).astype(BF16)
        u_ref[...] = u
        for c0 in range(0, n, ch):
            o_ref[:, c0:c0 + ch] = _dot(u, w_ref[:, c0:c0 + ch]).astype(BF16)

    vec = pl.BlockSpec((1, dm), lambda i: (0, 0))
    return _host_call(
        body, ex, name=name, grid=(nsteps,),
        out_shape=[jax.ShapeDtypeStruct((t, dm), BF16), jax.ShapeDtypeStruct((t, n), BF16)],
        in_specs=_stream_specs(tm, dm) + [vec, vec, pl.BlockSpec((dm, n), lambda i: (0, 0))],
        out_specs=[pl.BlockSpec((tm, dm), lambda i: (i, 0)), pl.BlockSpec((tm, n), lambda i: (i, 0))],
        sem=("arbitrary",), args=[*x, s, sh, w])


def _halves(tm):
    half = tm // 2 if tm % 32 == 0 else tm
    return [slice(r0, r0 + half) for r0 in range(0, tm, half)]


def _ln_store(y, rows, xres_refs, g_ref, y_ref, zh_ref, rs_ref):
    y_ref[rows, :] = y.astype(BF16)
    z = DN_ALPHA * _stream(*xres_refs, rows=rows) + g_ref[...] * y
    mu = jnp.mean(z, axis=1, keepdims=True)
    zc = z - mu
    var = jnp.mean(zc * zc, axis=1, keepdims=True)
    rstd = lax.rsqrt(var + LN_EPS)
    zh_ref[rows, :] = zc * rstd
    rs_ref[rows, :] = jnp.broadcast_to(rstd, (zc.shape[0], rs_ref.shape[1]))


def _ln_out_shapes(t, dm):
    return [jax.ShapeDtypeStruct((t, dm), BF16), jax.ShapeDtypeStruct((t, dm), F32),
            jax.ShapeDtypeStruct((t, LANES), F32)]


def _ln_out_specs(tm, dm):
    row = pl.BlockSpec((tm, dm), lambda i: (i, 0))
    return [row, row, pl.BlockSpec((tm, LANES), lambda i: (i, 0))]


def proj_ln(a, w, xres, gate, name):
    t, k = a.shape
    dm = w.shape[1]
    tm = _row_tile(t, WIDE_ROW_TILE)

    def body(a_ref, w_ref, xz_ref, xg_ref, xb_ref, g_ref, y_ref, zh_ref, rs_ref):
        for rows in _halves(tm):
            y = _dot(a_ref[rows, :], w_ref[...])
            _ln_store(y, rows, (xz_ref, xg_ref, xb_ref), g_ref, y_ref, zh_ref, rs_ref)

    vec = pl.BlockSpec((1, dm), lambda i: (0, 0))
    return pl.pallas_call(
        body, name=name, grid=(t // tm,),
        out_shape=_ln_out_shapes(t, dm),
        in_specs=[pl.BlockSpec((tm, k), lambda i: (i, 0)), pl.BlockSpec((k, dm), lambda i: (0, 0))]
                 + _stream_specs(tm, dm) + [vec],
        out_specs=_ln_out_specs(tm, dm),
        compiler_params=_cp("parallel"),
    )(a, w, *xres, gate)


def swiglu_proj_ln(ab, w, xres, gate, name):
    t = ab.shape[0]
    f, dm = w.shape
    tm = _row_tile(t)

    def body(a_ref, b_ref, w_ref, xz_ref, xg_ref, xb_ref, g_ref, h_ref, y_ref, zh_ref, rs_ref):
        for rows in _halves(tm):
            y = None
            for c0 in range(0, f, FF_CHUNK):
                cols = slice(c0, c0 + FF_CHUNK)
                a = a_ref[rows, cols].astype(F32)
                h = (a * _sigmoid(a) * b_ref[rows, cols].astype(F32)).astype(BF16)
                h_ref[rows, cols] = h
                part = _dot(h, w_ref[cols, :])
                y = part if y is None else y + part
            _ln_store(y, rows, (xz_ref, xg_ref, xb_ref), g_ref, y_ref, zh_ref, rs_ref)

    vec = pl.BlockSpec((1, dm), lambda i: (0, 0))
    return pl.pallas_call(
        body, name=name, grid=(t // tm,),
        out_shape=[jax.ShapeDtypeStruct((t, f), BF16)] + _ln_out_shapes(t, dm),
        in_specs=[pl.BlockSpec((tm, f), lambda i: (i, 0)), pl.BlockSpec((tm, f), lambda i: (i, 1)),
                  pl.BlockSpec((f, dm), lambda i: (0, 0))] + _stream_specs(tm, dm) + [vec],
        out_specs=[pl.BlockSpec((tm, f), lambda i: (i, 0))] + _ln_out_specs(tm, dm),
        compiler_params=_cp("parallel"),
    )(ab, ab, w, *xres, gate)


class AttnCfg:
    def __init__(self, dil, heads, kv_heads, qc, kc, vc, max_dist, head0, sinks):
        self.dil, self.heads, self.kv_heads = dil, heads, kv_heads
        self.qc, self.kc, self.vc = qc, kc, vc
        self.max_dist, self.head0, self.sinks = max_dist, head0, sinks
        self.wq = heads * HEAD_DIM
        self.wk = kv_heads * HEAD_DIM
        self.wout = self.wq + 2 * self.wk


ATTN_A = AttnCfg(1, A_Q_HEADS, A_KV_HEADS, 0, A_Q_HEADS * HEAD_DIM, (A_Q_HEADS + A_KV_HEADS) * HEAD_DIM,
                 A_WINDOW - 1, 0, True)


def _attn_b_cfg(g):
    win, dil = B_GROUPS[g]
    cols = ((A_W + g * B_OUT_W, A_W + B_ALL + g * B_OUT_W, A_W + 2 * B_ALL + g * B_OUT_W) if dil == 1
            else (0, B_OUT_W, 2 * B_OUT_W))
    return AttnCfg(dil, B_HEADS_PER_GROUP, B_HEADS_PER_GROUP, *cols, win // dil,
                   A_Q_HEADS + g * B_HEADS_PER_GROUP, False)


ATTN_B = [_attn_b_cfg(g) for g in range(N_GROUPS)]


SCALE = HEAD_DIM ** -0.5


def _head(h):
    return slice(h * HEAD_DIM, (h + 1) * HEAD_DIM)


def _masked_bias(mask, distf, cfg, h, d):
    return jnp.where(mask, distf * (-(_slope(cfg.head0 + h) * d)), NEG_INF)


def _lane_halves(rows):
    lane = lax.broadcasted_iota(jnp.int32, (rows, LANES), 1)
    return [lane < HEAD_DIM, lane >= HEAD_DIM]


def _n_pair_sources(cfg):
    return cfg.kv_heads if cfg.heads > cfg.kv_heads else cfg.heads // 2


def _pair_source(p, grp):
    return p if grp == 1 else (2 * p) // grp


def _fill_pairs(dst, prev_ref, cur_ref, grp):
    for j in range(dst.shape[0]):
        for ref, rows in ((prev_ref, slice(0, BLOCK)), (cur_ref, slice(BLOCK, dst.shape[1]))):
            if grp == 1:
                dst[j, rows, :] = ref[:, j * LANES:(j + 1) * LANES]
            else:
                one = ref[:, _head(j)]
                dst[j, rows, :] = jnp.concatenate([one, one], axis=1)


def _band(i, max_dist):
    qi = lax.broadcasted_iota(jnp.int32, (BLOCK, 2 * BLOCK), 0)
    sj = lax.broadcasted_iota(jnp.int32, (BLOCK, 2 * BLOCK), 1)
    dist = qi + BLOCK - sj
    valid = (dist >= 0) & (dist <= max_dist)
    first_key = jnp.where(i > 0, 0, BLOCK)
    valid_first = valid & (sj >= first_key)
    return dist, valid, valid_first


def _attn_geometry(n):
    tq = min(ATTN_QUERY_BLOCK, n)
    return tq, tq // BLOCK, n // tq


def attn_fwd(qkv, cfg, sinks, name):
    d, n, _ = qkv.shape
    tq, nsub, nqb = _attn_geometry(n)
    wq, wk = cfg.wq, cfg.wk
    grp = cfg.heads // cfg.kv_heads

    def body(sink_ref, q_ref, kc_ref, kp_ref, vc_ref, vp_ref, o_ref, l_ref, kf, vf):
        i = pl.program_id(1)
        _fill_pairs(kf, kp_ref, kc_ref, grp)
        _fill_pairs(vf, vp_ref, vc_ref, grp)
        dist, valid, valid_first = _band(i, cfg.max_dist)
        distf = dist.astype(F32)
        half_q, half_k = _lane_halves(BLOCK), _lane_halves(2 * BLOCK)
        rows = [slice(a * BLOCK, (a + 1) * BLOCK) for a in range(nsub)]
        wins = [slice(a * BLOCK, (a + 2) * BLOCK) for a in range(nsub)]
        for p in range(cfg.heads // 2):
            lanes = slice(p * LANES, (p + 1) * LANES)
            ki = _pair_source(p, grp)
            hs = (2 * p, 2 * p + 1)
            b_reg = [_masked_bias(valid, distf, cfg, h, d) for h in hs]
            b_first = [_masked_bias(valid_first, distf, cfg, h, d) for h in hs]
            ss = []
            for a in range(nsub):
                q2 = q_ref[rows[a], lanes] * SCALE
                k2 = kf[ki, wins[a], :]
                ss.append([_dot_nt(jnp.where(half_q[e], q2, 0), k2) + (b_first[e] if a == 0 else b_reg[e])
                           for e in range(2)])
            es, invs, lses = [], [], []
            for a in range(nsub):
                e_a, inv_a, lse_a = [], [], []
                for e in range(2):
                    m = jnp.max(ss[a][e], axis=1, keepdims=True)
                    if cfg.sinks:
                        m = jnp.maximum(m, sink_ref[hs[e]])
                    ex = jnp.exp(ss[a][e] - m)
                    den = jnp.sum(ex, axis=1, keepdims=True)
                    if cfg.sinks:
                        den = den + jnp.exp(sink_ref[hs[e]] - m)
                    e_a.append(ex.astype(BF16))
                    inv_a.append(1.0 / den)
                    lse_a.append(m + jnp.log(den))
                es.append(e_a)
                invs.append(inv_a)
                lses.append(lse_a)
            for a in range(nsub):
                v2 = vf[ki, wins[a], :]
                pcat = jnp.concatenate(es[a], axis=1)
                vcat = jnp.concatenate([jnp.where(half_k[e], v2, 0) for e in range(2)], axis=0)
                o = _dot(pcat, vcat) * jnp.where(half_q[0], invs[a][0], invs[a][1])
                o_ref[rows[a], lanes] = o.astype(BF16)
                l_ref[rows[a], lanes] = jnp.where(half_q[0], lses[a][0], lses[a][1])

    prev = lambda i: jnp.maximum(i * nsub - 1, 0)
    cur = lambda w, c: pl.BlockSpec((None, tq, w), lambda r, i: (r, i, c // w))
    prv = lambda w, c: pl.BlockSpec((None, BLOCK, w), lambda r, i: (r, prev(i), c // w))
    out = pl.BlockSpec((None, tq, wq), lambda r, i: (r, i, 0))
    pair_scratch = pltpu.VMEM((_n_pair_sources(cfg), tq + BLOCK, LANES), BF16)
    return pl.pallas_call(
        body, name=name, grid=(d, nqb),
        out_shape=[jax.ShapeDtypeStruct((d, n, wq), BF16), jax.ShapeDtypeStruct((d, n, wq), F32)],
        in_specs=[pl.BlockSpec(memory_space=pltpu.SMEM),
                  cur(wq, cfg.qc), cur(wk, cfg.kc), prv(wk, cfg.kc), cur(wk, cfg.vc), prv(wk, cfg.vc)],
        out_specs=[out, out],
        scratch_shapes=[pair_scratch, pair_scratch],
        compiler_params=_cp("parallel", "parallel"),
    )(sinks, qkv, qkv, qkv, qkv, qkv)


def mix_merge(ya, o_g, l_g, proj, w_a, w_b, name):
    t = ya.shape[0]
    dm = w_a.shape[1]
    tm = _row_tile(t, WIDE_ROW_TILE)
    gcol = GATE_COL // dm
    dils = [o.shape[0] for o in o_g]

    def body(ya_ref, o0, o1, o2, l0, l1, l2, ga_ref, gb_ref, wa_ref, wb_ref, yb_ref, mg_ref, *scrs):
        ls = [_unfold_from(l, scrs, d) for l, d in zip((l0, l1, l2), dils)]
        m = jnp.maximum(jnp.maximum(ls[0], ls[1]), ls[2])
        es = [jnp.exp(l - m) for l in ls]
        inv = 1.0 / (es[0] + es[1] + es[2])
        yb = sum(_unfold_from(o, scrs, d) * (e * inv) for o, e, d in zip((o0, o1, o2), es, dils)).astype(BF16)
        yb_ref[...] = yb
        pa = _dot(ya_ref[...], wa_ref[...])
        pb = _dot(yb, wb_ref[...])
        mg = _sigmoid(ga_ref[...].astype(F32)) * pa + _sigmoid(gb_ref[...].astype(F32)) * pb
        mg_ref[...] = mg.astype(BF16)

    wide = lambda w: pl.BlockSpec((tm, w), lambda i: (i, 0))
    folded = [_folded_spec(d, tm, B_OUT_W) for d in dils]
    return pl.pallas_call(
        body, name=name, grid=(t // tm,),
        out_shape=[jax.ShapeDtypeStruct((t, B_OUT_W), BF16), jax.ShapeDtypeStruct((t, dm), BF16)],
        in_specs=[wide(ya.shape[1])] + folded + folded
                 + [pl.BlockSpec((tm, dm), lambda i: (i, gcol)), pl.BlockSpec((tm, dm), lambda i: (i, gcol + 1)),
                    pl.BlockSpec(w_a.shape, lambda i: (0, 0)), pl.BlockSpec(w_b.shape, lambda i: (0, 0))],
        out_specs=[wide(B_OUT_W), wide(dm)],
        scratch_shapes=_fold_scratch(tm, B_OUT_W),
        compiler_params=_cp("parallel"),
    )(ya, *o_g, *l_g, proj, proj, w_a, w_b)


def loss_head(y, target):
    t, dm = y[0].shape
    tm = _row_tile(t, WIDE_ROW_TILE)

    def body(zh_ref, lg_ref, lb_ref, t_ref, dy_ref, loss_ref):
        @pl.when(pl.program_id(0) == 0)
        def _():
            loss_ref[...] = jnp.zeros_like(loss_ref)
        err = _stream(zh_ref, lg_ref, lb_ref) - t_ref[...]
        dy_ref[...] = err * (1.0 / dm)
        per_row = jnp.sum(err * err, axis=1, keepdims=True) * (1.0 / dm)
        loss_ref[...] += 0.5 * jnp.sum(per_row, axis=0, keepdims=True)

    row = pl.BlockSpec((tm, dm), lambda i: (i, 0))
    return pl.pallas_call(
        body, name="loss_head", grid=(t // tm,),
        out_shape=[jax.ShapeDtypeStruct((t, dm), F32), jax.ShapeDtypeStruct((8, LANES), F32)],
        in_specs=_stream_specs(tm, dm) + [row],
        out_specs=[row, pl.BlockSpec((8, LANES), lambda i: (0, 0))],
        compiler_params=_cp("arbitrary"),
    )(*y, target)


def _fold_rows(v):
    tm, c = v.shape
    return jnp.sum(v.reshape(tm // 8, 8, c), axis=0)


def _finish_sums(refs, nsteps):
    @pl.when(pl.program_id(0) == nsteps - 1)
    def _():
        for r in refs:
            r[...] = jnp.broadcast_to(jnp.sum(r[...], axis=0, keepdims=True), r.shape)


def ln_bwd(dxo, zhat, rstd, ysub, lg, gate, act, name):
    t, dm = dxo.shape
    k = act.shape[1]
    tm = _row_tile(t, WIDE_ROW_TILE if k <= dm else ROW_TILE)
    nsteps = t // tm
    ch = COL_CHUNK

    def body(dxo_ref, zh_ref, rs_ref, y_ref, lg_ref, g_ref, a_ref, dz_ref, dy_ref, sg_ref, sb_ref, sgate_ref,
             gw_ref, acc):
        @pl.when(pl.program_id(0) == 0)
        def _():
            for r in (sg_ref, sb_ref, sgate_ref, acc):
                r[...] = jnp.zeros_like(r)
        for rows in _halves(tm):
            dxo_v = dxo_ref[rows, :]
            zh = zh_ref[rows, :]
            dxh = dxo_v * lg_ref[...]
            m1 = jnp.mean(dxh, axis=1, keepdims=True)
            m2 = jnp.mean(dxh * zh, axis=1, keepdims=True)
            dz = rs_ref[rows, 0:1] * (dxh - m1 - zh * m2)
            dz_ref[rows, :] = dz
            dy = (g_ref[...] * dz).astype(BF16)
            dy_ref[rows, :] = dy
            sg_ref[...] += _fold_rows(dxo_v * zh)
            sb_ref[...] += _fold_rows(dxo_v)
            sgate_ref[...] += _fold_rows(dz * y_ref[rows, :].astype(F32))
            a = a_ref[rows, :]
            for c0 in range(0, dm, ch):
                acc[:, c0:c0 + ch] += _dot_tn(a, dy[:, c0:c0 + ch])
        _finish_sums((sg_ref, sb_ref, sgate_ref), nsteps)

        @pl.when(pl.program_id(0) == nsteps - 1)
        def _():
            gw_ref[...] = acc[...].astype(BF16)

    row = pl.BlockSpec((tm, dm), lambda i: (i, 0))
    vec = pl.BlockSpec((1, dm), lambda i: (0, 0))
    sums = pl.BlockSpec((8, dm), lambda i: (0, 0))
    return pl.pallas_call(
        body, name=name, grid=(nsteps,),
        out_shape=[jax.ShapeDtypeStruct((t, dm), F32), jax.ShapeDtypeStruct((t, dm), BF16)]
                  + [jax.ShapeDtypeStruct((8, dm), F32)] * 3 + [jax.ShapeDtypeStruct((k, dm), BF16)],
        in_specs=[row, row, pl.BlockSpec((tm, LANES), lambda i: (i, 0)), row, vec, vec,
                  pl.BlockSpec((tm, k), lambda i: (i, 0))],
        out_specs=[row, row, sums, sums, sums, pl.BlockSpec((k, dm), lambda i: (0, 0))],
        scratch_shapes=[pltpu.VMEM((k, dm), F32)],
        compiler_params=_cp("arbitrary"),
    )(dxo, zhat, rstd, ysub, lg, gate, act)


def _mod_bwd_store(du_of, dz_ref, x_refs, s_ref, dx_ref, ss_ref, ssh_ref, nsteps):
    @pl.when(pl.program_id(0) == 0)
    def _():
        ss_ref[...] = jnp.zeros_like(ss_ref)
        ssh_ref[...] = jnp.zeros_like(ssh_ref)
    for c0 in range(0, dx_ref.shape[1], DGRAD_CHUNK):
        cols = slice(c0, c0 + DGRAD_CHUNK)
        du = du_of(cols)
        dx_ref[:, cols] = DN_ALPHA * dz_ref[:, cols] + du * (1.0 + s_ref[:, cols])
        ss_ref[:, cols] += _fold_rows(du * _stream(*x_refs, cols=cols))
        ssh_ref[:, cols] += _fold_rows(du)
    _finish_sums((ss_ref, ssh_ref), nsteps)


def dgrad_ffn(g, wt, dz, xin, s, name, ex=None):
    t, dm = dz.shape
    k = g.shape[1]
    tm = _row_tile(t)
    nsteps = t // tm

    def body(*refs):
        i = pl.program_id(0)
        (g_ref, w_ref, dz_ref, xz_ref, xg_ref, xb_ref, s_ref,
         dx_ref, ss_ref, ssh_ref) = _hosted(ex, refs, 7, 3, i == 0, i == nsteps - 1)
        g_v = g_ref[...]
        _mod_bwd_store(lambda cols: _dot(g_v, w_ref[:, cols]), dz_ref, (xz_ref, xg_ref, xb_ref), s_ref,
                       dx_ref, ss_ref, ssh_ref, nsteps)

    row = pl.BlockSpec((tm, dm), lambda i: (i, 0))
    acc = pl.BlockSpec((8, dm), lambda i: (0, 0))
    return _host_call(
        body, ex, name=name, grid=(nsteps,),
        out_shape=[jax.ShapeDtypeStruct((t, dm), F32)] + [jax.ShapeDtypeStruct((8, dm), F32)] * 2,
        in_specs=[pl.BlockSpec((tm, k), lambda i: (i, 0)), pl.BlockSpec((k, dm), lambda i: (0, 0)), row]
                 + _stream_specs(tm, dm) + [pl.BlockSpec((1, dm), lambda i: (0, 0))],
        out_specs=[row, acc, acc], sem=("arbitrary",), args=[g, wt, dz, *xin, s])


def dswiglu(dy, wdt, ab, name):
    t, dm = dy.shape
    f = wdt.shape[1]
    tm = _row_tile(t)

    def body(dy_ref, w_ref, a_ref, b_ref, o_ref):
        dy_v = dy_ref[...]
        for c0 in range(0, f, FF_CHUNK):
            cols = slice(c0, c0 + FF_CHUNK)
            dh = _dot(dy_v, w_ref[:, cols])
            a = a_ref[:, cols].astype(F32)
            sg = _sigmoid(a)
            o_ref[:, cols] = (dh * b_ref[:, cols].astype(F32) * (sg * (1.0 + a * (1.0 - sg)))).astype(BF16)
            o_ref[:, f + c0:f + c0 + FF_CHUNK] = (dh * (a * sg)).astype(BF16)

    return pl.pallas_call(
        body, name=name, grid=(t // tm,),
        out_shape=jax.ShapeDtypeStruct((t, 2 * f), BF16),
        in_specs=[pl.BlockSpec((tm, dm), lambda i: (i, 0)), pl.BlockSpec((dm, f), lambda i: (0, 0)),
                  pl.BlockSpec((tm, f), lambda i: (i, 0)), pl.BlockSpec((tm, f), lambda i: (i, 1))],
        out_specs=pl.BlockSpec((tm, 2 * f), lambda i: (i, 0)),
        compiler_params=_cp("parallel"),
    )(dy, wdt, ab, ab)


def dgrad_in(d_a, d_b, dgab, wt, dz, xin, s, name, ex=None):
    t, dm = dz.shape
    tm = _row_tile(t)
    nsteps = t // tm
    dils = [a.shape[0] for a in d_b]

    def body(*refs):
        i = pl.program_id(0)
        (da_ref, b0, b1, b2, dg_ref, w_ref, dz_ref, xz_ref, xg_ref, xb_ref, s_ref, dx_ref, ss_ref, ssh_ref,
         *scrs) = _hosted(ex, refs, 11, 3, i == 0, i == nsteps - 1)
        vs = [b_ref[...].reshape(tm, B_GW) for b_ref in (b0, b1, b2)]

        def du_of(cols):
            du = _dot(da_ref[0], w_ref[0:A_W, cols])
            for g, (v, d) in enumerate(zip(vs, dils)):
                part = None
                for p in range(3):
                    r0 = A_W + p * B_ALL + g * B_OUT_W
                    term = _dot(v[:, p * B_OUT_W:(p + 1) * B_OUT_W], w_ref[r0:r0 + B_OUT_W, cols])
                    part = term if part is None else part + term
                if d == 1:
                    du = du + part
                else:
                    n = tm // d
                    du = du + _unfold_rows(lambda r, cs: part[r * n:(r + 1) * n, cs], scrs, d, n, DGRAD_CHUNK)
            for j in range(2):
                du = du + _dot(dg_ref[:, j * dm:(j + 1) * dm], w_ref[GATE_COL + j * dm:GATE_COL + (j + 1) * dm, cols])
            return du

        _mod_bwd_store(du_of, dz_ref, (xz_ref, xg_ref, xb_ref), s_ref, dx_ref, ss_ref, ssh_ref, nsteps)

    row = pl.BlockSpec((tm, dm), lambda i: (i, 0))
    acc = pl.BlockSpec((8, dm), lambda i: (0, 0))
    return _host_call(
        body, ex, name=name, grid=(nsteps,),
        out_shape=[jax.ShapeDtypeStruct((t, dm), F32)] + [jax.ShapeDtypeStruct((8, dm), F32)] * 2,
        in_specs=[_folded_spec(1, tm, A_W)] + [_folded_spec(d, tm, B_GW) for d in dils]
                 + [pl.BlockSpec((tm, 2 * dm), lambda i: (i, 0)), pl.BlockSpec(wt.shape, lambda i: (0, 0)), row]
                 + _stream_specs(tm, dm) + [pl.BlockSpec((1, dm), lambda i: (0, 0))],
        out_specs=[row, acc, acc],
        scratch_shapes=_fold_scratch(tm, DGRAD_CHUNK), sem=("arbitrary",), args=[d_a, *d_b, dgab, wt, dz, *xin, s])


def wgrad(a, b, buf, tn, nj, b0, o0, om, name):
    t, k = a.shape
    tt = ROW_TILE
    while tt * 2 * k <= WGRAD_TILE_ELEMS and tt * 2 <= t:
        tt *= 2
    nsteps = t // tt
    last = nsteps - 1

    def body(a_ref, b_ref, buf_ref, o_ref, acc):
        s, j = pl.program_id(0), pl.program_id(1)

        @pl.when(s == 0)
        def _():
            acc[j] = jnp.zeros(acc.shape[1:], F32)
        acc[j] += _dot_tn(a_ref[...], b_ref[...])

        @pl.when(s == last)
        def _():
            o_ref[...] = acc[j].astype(BF16)

    return pl.pallas_call(
        body, name=name, grid=(nsteps, nj),
        out_shape=jax.ShapeDtypeStruct(buf.shape, buf.dtype),
        in_specs=[pl.BlockSpec((tt, k), lambda s, j: (s, 0)),
                  pl.BlockSpec((tt, tn), lambda s, j: (s, b0 + j)),
                  pl.BlockSpec(memory_space=pl.ANY)],
        out_specs=pl.BlockSpec((k, tn), lambda s, j: (0, o0 + om * jnp.where(s == last, j, 0))),
        scratch_shapes=[pltpu.VMEM((nj, k, tn), F32)],
        input_output_aliases={2: 0},
        compiler_params=_cp("arbitrary", "arbitrary"),
    )(a, b, buf)


def dmerge(do, wot, ya, yb, w_a, w_b, wat, wbt, proj, name):
    t, dm = do.shape
    tm = _row_tile(t, WIDE_ROW_TILE)
    nsteps = t // tm
    gcol = GATE_COL // dm
    ch = COL_CHUNK

    def body(do_ref, wot_ref, ya_ref, yb_ref, wa_ref, wb_ref, wat_ref, wbt_ref, g_ref,
             dya_ref, dyb_ref, dg_ref, gwa_ref, gwb_ref, dm_scr, acc_a, acc_b):
        i, j = pl.program_id(0), pl.program_id(1)

        @pl.when((i == 0) & (j == 0))
        def _():
            acc_a[...] = jnp.zeros_like(acc_a)
            acc_b[...] = jnp.zeros_like(acc_b)

        @pl.when(j == 0)
        def _():
            do_v = do_ref[...]
            for c0 in range(0, dm, ch):
                dm_scr[:, c0:c0 + ch] = _dot(do_v, wot_ref[:, c0:c0 + ch])

        def branch(y_ref, w_ref, wt_ref, dy_ref, acc, gw_ref):
            y = y_ref[...]
            dy = None
            for c0 in range(0, dm, ch):
                cols = slice(c0, c0 + ch)
                p = _dot(y, w_ref[:, cols])
                sg = _sigmoid(g_ref[:, cols].astype(F32))
                dmg = dm_scr[:, cols]
                dp = (dmg * sg).astype(BF16)
                dg_ref[:, cols] = (dmg * p * (sg * (1.0 - sg))).astype(BF16)
                acc[:, cols] += _dot_tn(y, dp)
                part = _dot(dp, wt_ref[cols, :])
                dy = part if dy is None else dy + part
            dy_ref[...] = dy.astype(dy_ref.dtype)

            @pl.when(i == nsteps - 1)
            def _():
                gw_ref[...] = acc[...].astype(BF16)

        pl.when(j == 0)(lambda: branch(ya_ref, wa_ref, wat_ref, dya_ref, acc_a, gwa_ref))
        pl.when(j == 1)(lambda: branch(yb_ref, wb_ref, wbt_ref, dyb_ref, acc_b, gwb_ref))

    full = lambda arr: pl.BlockSpec(arr.shape, lambda i, j: (0, 0))
    rowc = lambda w: pl.BlockSpec((tm, w), lambda i, j: (i, 0))
    return pl.pallas_call(
        body, name=name, grid=(nsteps, 2),
        out_shape=[jax.ShapeDtypeStruct((t, ya.shape[1]), BF16), jax.ShapeDtypeStruct((t, yb.shape[1]), F32),
                   jax.ShapeDtypeStruct((t, 2 * dm), BF16),
                   jax.ShapeDtypeStruct(w_a.shape, BF16), jax.ShapeDtypeStruct(w_b.shape, BF16)],
        in_specs=[rowc(dm), full(wot), rowc(ya.shape[1]), rowc(yb.shape[1]), full(w_a), full(w_b),
                  full(wat), full(wbt), pl.BlockSpec((tm, dm), lambda i, j: (i, gcol + j))],
        out_specs=[rowc(ya.shape[1]), rowc(yb.shape[1]), pl.BlockSpec((tm, dm), lambda i, j: (i, j)),
                   full(w_a), full(w_b)],
        scratch_shapes=[pltpu.VMEM((tm, dm), F32), pltpu.VMEM(w_a.shape, F32), pltpu.VMEM(w_b.shape, F32)],
        compiler_params=_cp("arbitrary", "arbitrary"),
    )(do, wot, ya, yb, w_a, w_b, wat, wbt, proj)


def mix_bwd(dyb, o_g, l_g, name):
    t, w = dyb.shape
    tm = _row_tile(t, WIDE_ROW_TILE)
    nh = w // HEAD_DIM
    dils = [o.shape[0] for o in o_g]

    def body(dyb_ref, o0, o1, o2, l0, l1, l2, do0, do1, do2, dl0, dl1, dl2, *scr):
        ls = [_unfold_from(l, scr, d) for l, d in zip((l0, l1, l2), dils)]
        m = jnp.maximum(jnp.maximum(ls[0], ls[1]), ls[2])
        es = [jnp.exp(l - m) for l in ls]
        inv = 1.0 / (es[0] + es[1] + es[2])
        wts = [e * inv for e in es]
        dyb_v = dyb_ref[...]
        dws = []
        for o_ref, do_ref, wt, d in zip((o0, o1, o2), (do0, do1, do2), wts, dils):
            prod = dyb_v * _unfold_from(o_ref, scr, d)
            _fold_to(do_ref, dyb_v * wt, scr, d)
            for h in range(nh):
                hs = slice(h * HEAD_DIM, (h + 1) * HEAD_DIM)
                dws.append(jnp.broadcast_to(jnp.sum(prod[:, hs], axis=1, keepdims=True), (tm, HEAD_DIM)))
        for g, (dl_ref, d) in enumerate(zip((dl0, dl1, dl2), dils)):
            cols = []
            for h in range(nh):
                hs = slice(h * HEAD_DIM, (h + 1) * HEAD_DIM)
                mean = sum(wts[g2][:, hs] * dws[g2 * nh + h] for g2 in range(N_GROUPS))
                cols.append(wts[g][:, hs] * (dws[g * nh + h] - mean))
            _fold_to(dl_ref, jnp.concatenate(cols, axis=1), scr, d)

    folded = [_folded_spec(d, tm, w) for d in dils]
    return pl.pallas_call(
        body, name=name, grid=(t // tm,),
        out_shape=[jax.ShapeDtypeStruct(o.shape, BF16) for o in o_g]
                  + [jax.ShapeDtypeStruct(o.shape, F32) for o in o_g],
        in_specs=[pl.BlockSpec((tm, w), lambda i: (i, 0))] + folded + folded,
        out_specs=folded + folded,
        scratch_shapes=_fold_scratch(tm, w),
        compiler_params=_cp("parallel"),
    )(dyb, *o_g, *l_g)


def attn_bwd(qkv, o, lse, do, dlse, cfg, sinks, name):
    d, n, _ = qkv.shape
    tq, nsub, nqb = _attn_geometry(n)
    wq, wk, wout = cfg.wq, cfg.wk, cfg.wout
    grp = cfg.heads // cfg.kv_heads
    has_dl = dlse is not None

    def body(*refs):
        sink_ref, q_ref, qn_ref, kc_ref, kp_ref, vc_ref, vp_ref = refs[:7]
        o_ref, on_ref, do_ref, don_ref, l_ref, ln_ref = refs[7:13]
        rest = refs[13:]
        dl_ref = dln_ref = None
        if has_dl:
            dl_ref, dln_ref = rest[:2]
            rest = rest[2:]
        out_ref = rest[0]
        rest = rest[1:]
        if cfg.sinks:
            dsink_ref = rest[0]
            rest = rest[1:]
        kf, vf = rest
        r, i = pl.program_id(0), pl.program_id(1)
        _fill_pairs(kf, kp_ref, kc_ref, grp)
        _fill_pairs(vf, vp_ref, vc_ref, grp)
        dist, valid, valid_first = _band(i, cfg.max_dist)
        distf = dist.astype(F32)
        next_dist = jnp.where(i < nqb - 1, cfg.max_dist, -1)
        valid_next = (dist[:, 0:BLOCK] >= 0) & (dist[:, 0:BLOCK] <= next_dist)
        half_q = _lane_halves(BLOCK)
        if cfg.sinks:
            @pl.when((r == 0) & (i == 0))
            def _():
                dsink_ref[...] = jnp.zeros_like(dsink_ref)

        shared = {}
        for p in range(cfg.heads // 2):
            lanes = slice(p * LANES, (p + 1) * LANES)
            ki = _pair_source(p, grp)
            hs = (2 * p, 2 * p + 1)
            biases = [[_masked_bias(m, dd, cfg, h, d) for h in hs]
                      for m, dd in ((valid_first, distf), (valid, distf), (valid_next, distf[:, 0:BLOCK]))]
            tiles = []
            for a in range(nsub + 1):
                if a < nsub:
                    rows, win = slice(a * BLOCK, (a + 1) * BLOCK), slice(a * BLOCK, (a + 2) * BLOCK)
                    src = (q_ref, o_ref, do_ref, l_ref, dl_ref)
                else:
                    rows, win = slice(0, BLOCK), slice(nsub * BLOCK, (nsub + 1) * BLOCK)
                    src = (qn_ref, on_ref, don_ref, ln_ref, dln_ref)
                q2 = src[0][rows, lanes] * SCALE
                do2 = src[2][rows, lanes]
                o2 = src[1][rows, lanes].astype(F32)
                l2 = src[3][rows, lanes]
                k2, v2 = kf[ki, win, :], vf[ki, win, :]
                per = []
                for e in range(2):
                    qe, doe = jnp.where(half_q[e], q2, 0), jnp.where(half_q[e], do2, 0)
                    delta = jnp.sum(doe.astype(F32) * o2, axis=1, keepdims=True)
                    lse_v = jnp.max(jnp.where(half_q[e], l2, NEG_INF), axis=1, keepdims=True)
                    shift = -delta
                    if has_dl:
                        shift = shift + jnp.max(jnp.where(half_q[e], src[4][rows, lanes], NEG_INF), axis=1,
                                                keepdims=True)
                    s = _dot_nt(qe, k2) + biases[0 if a == 0 else (1 if a < nsub else 2)][e]
                    per.append((qe, doe, delta, lse_v, shift, s, _dot_nt(doe, v2)))
                tiles.append((k2, per))
            grads = []
            for k2, per in tiles:
                both = []
                for qe, doe, delta, lse_v, shift, s, dp in per:
                    pr = jnp.exp(s - lse_v)
                    both.append(((pr * (dp + shift)).astype(BF16), pr.astype(BF16)))
                grads.append(both)
            dkt, dvt = [], []
            for a, ((k2, per), both) in enumerate(zip(tiles, grads)):
                if a < nsub:
                    half_k = _lane_halves(k2.shape[0])
                    ds_cat = jnp.concatenate([both[0][0], both[1][0]], axis=1)
                    k_cat = jnp.concatenate([jnp.where(half_k[e], k2, 0) for e in range(2)], axis=0)
                    out_ref[a * BLOCK:(a + 1) * BLOCK, lanes] = (_dot(ds_cat, k_cat) * SCALE).astype(BF16)
                cut = (lambda x: x[:, BLOCK:]) if a == 0 else (lambda x: x)
                q_cat = jnp.concatenate([per[0][0], per[1][0]], axis=0)
                do_cat = jnp.concatenate([per[0][1], per[1][1]], axis=0)
                dkt.append(_dot_tn(q_cat, jnp.concatenate([cut(both[0][0]), cut(both[1][0])], axis=0)))
                dvt.append(_dot_tn(do_cat, jnp.concatenate([cut(both[0][1]), cut(both[1][1])], axis=0)))
                if cfg.sinks and a < nsub:
                    for e in range(2):
                        psink = jnp.exp(sink_ref[hs[e]] - per[e][3])
                        tot = jnp.sum(psink * (-per[e][2]), axis=0, keepdims=True)
                        dsink_ref[hs[e]:hs[e] + 1, :] += jnp.broadcast_to(tot, (1, LANES))
            for m in range(nsub):
                rows = slice(m * BLOCK, (m + 1) * BLOCK)
                for which, (acc, col0) in enumerate(((dkt, wq), (dvt, wq + wk))):
                    own = acc[m] if m == 0 else acc[m][:, BLOCK:]
                    total = own + acc[m + 1][:, 0:BLOCK]
                    if grp == 1:
                        out_ref[rows, col0 + p * LANES:col0 + (p + 1) * LANES] = total.T.astype(BF16)
                    else:
                        t64 = total[0:HEAD_DIM] + total[HEAD_DIM:]
                        key = (ki, which, m)
                        shared[key] = t64 + shared[key] if key in shared else t64
        for (ki, which, m), t64 in shared.items():
            col0 = (wq, wq + wk)[which] + ki * HEAD_DIM
            out_ref[m * BLOCK:(m + 1) * BLOCK, col0:col0 + HEAD_DIM] = t64.T.astype(BF16)

    prev = lambda i: jnp.maximum(i * nsub - 1, 0)
    nxt = lambda i: jnp.minimum((i + 1) * nsub, n // BLOCK - 1)
    cur = lambda w, c: pl.BlockSpec((None, tq, w), lambda r, i: (r, i, c // w))
    prv = lambda w, c: pl.BlockSpec((None, BLOCK, w), lambda r, i: (r, prev(i), c // w))
    o_cur = pl.BlockSpec((None, tq, wq), lambda r, i: (r, i, 0))
    o_nxt = pl.BlockSpec((None, BLOCK, wq), lambda r, i: (r, nxt(i), 0))
    in_specs = [pl.BlockSpec(memory_space=pltpu.SMEM),
                cur(wq, cfg.qc), pl.BlockSpec((None, BLOCK, wq), lambda r, i: (r, nxt(i), cfg.qc // wq)),
                cur(wk, cfg.kc), prv(wk, cfg.kc), cur(wk, cfg.vc), prv(wk, cfg.vc),
                o_cur, o_nxt, o_cur, o_nxt, o_cur, o_nxt]
    args = [sinks, qkv, qkv, qkv, qkv, qkv, qkv, o, o, do, do, lse, lse]
    if has_dl:
        in_specs += [o_cur, o_nxt]
        args += [dlse, dlse]
    out_shape = [jax.ShapeDtypeStruct((d, n, wout), BF16)]
    out_specs = [pl.BlockSpec((None, tq, wout), lambda r, i: (r, i, 0))]
    if cfg.sinks:
        out_shape.append(jax.ShapeDtypeStruct((8, LANES), F32))
        out_specs.append(pl.BlockSpec((8, LANES), lambda r, i: (0, 0)))
    pair_scratch = pltpu.VMEM((_n_pair_sources(cfg), tq + BLOCK, LANES), BF16)
    return pl.pallas_call(
        body, name=name, grid=(d, nqb), out_shape=out_shape, in_specs=in_specs, out_specs=out_specs,
        scratch_shapes=[pair_scratch, pair_scratch],
        compiler_params=_cp("arbitrary", "arbitrary"),
    )(*args)


def _adamw(g, w, m, v):
    m = ADAM_B1 * m + (1.0 - ADAM_B1) * g
    v = ADAM_B2 * v + (1.0 - ADAM_B2) * (g * g)
    m_hat = m / (1.0 - ADAM_B1 ** ADAM_STEP)
    v_hat = v / (1.0 - ADAM_B2 ** ADAM_STEP)
    delta = -ADAM_LR * (m_hat / (jnp.sqrt(v_hat) + ADAM_EPS) + ADAM_WD * w)
    return delta, m, v


def _adam_rows(r):
    if r <= ADAM_ROW_TILE:
        return r
    return next(rows for rows in range(ADAM_ROW_TILE, 0, -8) if r % rows == 0)


def adam_reduce(parts, w, m, v, name):
    r, c = w.shape
    tr = _adam_rows(r)

    def body(p_ref, w_ref, m_ref, v_ref, g_ref, d_ref, mo_ref, vo_ref):
        g = p_ref[0].astype(F32)
        for j in range(1, N_DEV):
            g = g + p_ref[j].astype(F32)
        g_ref[...] = g
        d_ref[...], mo_ref[...], vo_ref[...] = _adamw(g, w_ref[...], m_ref[...], v_ref[...])

    row = pl.BlockSpec((tr, c), lambda i: (i, 0))
    return pl.pallas_call(
        body, name=name, grid=(r // tr,),
        out_shape=[jax.ShapeDtypeStruct((r, c), F32)] * 4,
        in_specs=[pl.BlockSpec((N_DEV, tr, c), lambda i: (0, i, 0)), row, row, row],
        out_specs=[row] * 4,
        compiler_params=_cp("parallel"),
    )(parts, w, m, v)


def adam_layers(parts, w, m, v, name):
    nl, r, c = w.shape
    tr = _adam_rows(r)
    steps = r // tr

    def body(*refs):
        p_refs = refs[:nl]
        w_ref, m_ref, v_ref, g_ref, d_ref, mo_ref, vo_ref = refs[nl:]
        for k in range(nl):
            @pl.when(pl.program_id(0) == k)
            def _():
                g = p_refs[k][0].astype(F32)
                for j in range(1, N_DEV):
                    g = g + p_refs[k][j].astype(F32)
                g_ref[...] = g
                d_ref[...], mo_ref[...], vo_ref[...] = _adamw(g, w_ref[...], m_ref[...], v_ref[...])

    def part_spec(k):
        return pl.BlockSpec((N_DEV, tr, c), lambda l, i: (0, jnp.clip(i + (l - k) * steps, 0, steps - 1), 0))

    blk = pl.BlockSpec((None, tr, c), lambda l, i: (l, i, 0))
    return pl.pallas_call(
        body, name=name, grid=(nl, steps),
        out_shape=[jax.ShapeDtypeStruct((nl, r, c), F32)] * 4,
        in_specs=[part_spec(k) for k in range(nl)] + [blk, blk, blk],
        out_specs=[blk] * 4,
        compiler_params=_cp("arbitrary", "arbitrary"),
    )(*parts, w, m, v)


def adam_w_ada(sct, dm_loc, w, m, v):
    nl, dm, wc = w.shape
    tr = _row_tile(dm)

    def body(s_ref, d_ref, w_ref, m_ref, v_ref, g_ref, dl_ref, mo_ref, vo_ref):
        g = jnp.dot(s_ref[...], d_ref[...], preferred_element_type=F32, precision=lax.Precision.HIGHEST)
        g_ref[...] = g
        dl_ref[...], mo_ref[...], vo_ref[...] = _adamw(g, w_ref[...], m_ref[...], v_ref[...])

    blk = pl.BlockSpec((None, tr, wc), lambda l, i: (l, i, 0))
    return pl.pallas_call(
        body, name="adam_w_ada", grid=(nl, dm // tr),
        out_shape=[jax.ShapeDtypeStruct(w.shape, F32)] * 4,
        in_specs=[pl.BlockSpec((tr, LANES), lambda l, i: (i, 0)),
                  pl.BlockSpec((None, LANES, wc), lambda l, i: (l, 0, 0)), blk, blk, blk],
        out_specs=[blk] * 4,
        compiler_params=_cp("parallel", "parallel"),
    )(sct, dm_loc, w, m, v)


TRANSPOSED = ("w_gate", "w_up")


def _pieces(dm):
    ncol = lambda n: n // N_DEV
    mixer = ([Piece("w_in", "w_in", 1, 0, ncol(GATE_COL + 2 * dm)),
              Piece("w_a", "w_a", 1, 0, ncol(dm)),
              Piece("w_b", "w_b", 1, 0, ncol(dm)),
              Piece("w_o", "w_o", 0, 0, ncol(dm))],
             {"w_in": (dm, GATE_COL + 2 * dm), "w_a": (A_Q_HEADS * HEAD_DIM, dm), "w_b": (B_OUT_W, dm),
              "w_o": (dm, dm)})
    ffn = ([Piece("w_gate", "w_ffn_t", 0, 0, ncol(D_FF)),
            Piece("w_up", "w_ffn_t", 0, D_FF, ncol(D_FF)),
            Piece("w_down", "w_down", 0, 0, ncol(D_FF))],
           {"w_ffn_t": (2 * D_FF, dm), "w_down": (D_FF, dm)})
    return mixer, ffn


def kernel(x, c, w_ada, b_ada, w_in, sinks, w_a, w_b, w_o, ln1_g, ln1_b, w_gate, w_up, w_down, ln2_g, ln2_b, loss_target, m_w_ada, m_b_ada, m_w_in, m_sinks, m_w_a, m_w_b, m_w_o, m_ln1_g, m_ln1_b, m_w_gate, m_w_up, m_w_down, m_ln2_g, m_ln2_b, v_w_ada, v_b_ada, v_w_in, v_sinks, v_w_a, v_w_b, v_w_o, v_ln1_g, v_ln1_b, v_w_gate, v_w_up, v_w_down, v_ln2_g, v_ln2_b):
    given = dict(locals())
    nl = w_in.shape[0]
    t, dm = x.shape[1], x.shape[2]
    me = 4 * lax.axis_index("x") + 2 * lax.axis_index("y") + lax.axis_index("c")
    x0 = x.reshape(t, dm)
    target = loss_target.reshape(t, dm)

    groups = dict(zip(("mixer", "ffn"), _pieces(dm)))
    local = lambda nm, pre="": (given[pre + nm].transpose(0, 2, 1) if nm in TRANSPOSED else given[pre + nm])
    shards = {pc.name: local(pc.name).astype(BF16) for pcs, _ in groups.values() for pc in pcs}

    def gather(group, l):
        pcs, bufs = groups[group]
        return Exchange("gather", pcs, [shards[pc.name][l] for pc in pcs], bufs.values(), bufs)

    def scatter(group, gbuf):
        pcs, bufs = groups[group]
        return Exchange("scatter", pcs, [gbuf[nm] for nm in bufs],
                        [(N_DEV,) + shards[pc.name].shape[1:] for pc in pcs], bufs)

    mixer_pcs, mixer_bufs = groups["mixer"]
    groups["first"] = (mixer_pcs[:1], {"w_in": mixer_bufs["w_in"]})
    groups["rest"] = (mixer_pcs[1:] + groups["ffn"][0],
                      {**{k: v for k, v in mixer_bufs.items() if k != "w_in"}, **groups["ffn"][1]})
    full = [dict() for _ in range(nl)]
    full[0].update(zip(groups["first"][1], run_exchange(gather("first", 0), "gather_w_in")))

    wc = w_ada.shape[2]
    c_all = all_gather_small(jnp.broadcast_to(c, (8, dm)), "gather_c")[:, 0, :]
    b_loc = lax.dynamic_slice_in_dim(b_ada, me * wc, wc, axis=1).reshape(nl, 1, wc)
    mp, sc_all = mod_partial(c_all, w_ada, b_loc)
    mp_all = all_gather_small(mp.reshape(nl * N_DEV, wc), "gather_mod").reshape(N_DEV, nl, N_DEV, wc)
    mod = lax.dynamic_index_in_dim(mp_all, me, axis=2, keepdims=False)
    mod = mod.transpose(1, 0, 2).reshape(nl, 6, 1, dm)

    vec = lambda a, l: a[l].reshape(1, dm)

    saved = []
    xl = (x0, jnp.ones((1, dm), F32), jnp.zeros((1, dm), F32))
    for l in range(nl):
        sh1, s1, g1, sh2, s2, g2 = [mod[l, j] for j in range(6)]
        w = full[l]
        hosted = "rest" if l == 0 else "ffn"
        (u1, u1_f4, u1_f16, proj, qkv_f4, qkv_f16), got = in_proj(xl, s1, sh1, w["w_in"], "in_proj",
                                                                   gather(hosted, l))
        w.update(zip(groups[hosted][1], got))
        proj3 = proj.reshape(1, t, proj.shape[1])
        qkv_b = [proj3, qkv_f4, qkv_f16]
        ya, lse_a = attn_fwd(proj3, ATTN_A, sinks[l], "attn_a_fwd")
        o_g, l_g = [], []
        for g, cfg in enumerate(ATTN_B):
            o, ls = attn_fwd(qkv_b[g], cfg, sinks[l], "attn_b%d_fwd" % g)
            o_g.append(o)
            l_g.append(ls)
        yb, merged = mix_merge(ya[0], o_g, l_g, proj, w["w_a"], w["w_b"], "mix_merge")
        y1, zh1, rs1 = proj_ln(merged, w["w_o"], xl, g1, "out_proj_ln")
        x1 = (zh1, vec(ln1_g, l), vec(ln1_b, l))
        (u2, ab), got = modmm(x1, s2, sh2, w["w_ffn_t"].T, "ffn_up", gather("mixer", l + 1) if l + 1 < nl else None)
        if l + 1 < nl:
            full[l + 1].update(zip(groups["mixer"][1], got))
        h, y2, zh2, rs2 = swiglu_proj_ln(ab, w["w_down"], x1, g2, "ffn_down_ln")
        x2 = (zh2, vec(ln2_g, l), vec(ln2_b, l))
        saved.append(dict(xin=xl, u1=[u1, u1_f4.reshape(t, dm), u1_f16.reshape(t, dm)], proj=proj, qkv_b=qkv_b,
                          ya=ya, lse_a=lse_a, o_g=o_g, l_g=l_g, yb=yb, merged=merged,
                          y1=y1, x1=x1, zh1=zh1, rs1=rs1, u2=u2, ab=ab, h=h, y2=y2, zh2=zh2, rs2=rs2))
        xl = x2

    dx, loss_part = loss_head(xl, target)

    small = {k: [None] * nl for k in ("dmod", "ln1_g", "ln1_b", "ln2_g", "ln2_b", "sinks")}
    recv = {nm: [None] * nl for grp in groups.values() for nm in (pc.name for pc in grp[0])}

    def keep(group, l, got):
        for pc, arr in zip(groups[group][0], got):
            recv[pc.name][l] = arr

    for l in reversed(range(nl)):
        sv, w = saved[l], full[l]
        sh1, s1, g1, sh2, s2, g2 = [mod[l, j] for j in range(6)]
        fresh = lambda nm: lax.empty({**groups["mixer"][1], **groups["ffn"][1]}[nm], BF16)
        gbuf = {}
        dz2, dy2, sg, sb, sgate2, gbuf["w_down"] = ln_bwd(dx, sv["zh2"], sv["rs2"], sv["y2"], vec(ln2_g, l), g2,
                                                          sv["h"], "ln_bwd_ffn")
        small["ln2_g"][l], small["ln2_b"][l] = sg[0], sb[0]
        dab = dswiglu(dy2, w["w_down"].T, sv["ab"], "dswiglu")
        gbuf["w_ffn_t"] = wgrad(dab, sv["u2"], fresh("w_ffn_t"), WGRAD_COLS, dm // WGRAD_COLS, 0, 0, 1,
                                "wgrad_ffn_up")
        (dx1, ss2, ssh2), got = dgrad_ffn(dab, w["w_ffn_t"], dz2, sv["x1"], s2, "dgrad_ffn", scatter("ffn", gbuf))
        keep("ffn", l, got)
        dz1, do1, sg, sb, sgate1, gbuf["w_o"] = ln_bwd(dx1, sv["zh1"], sv["rs1"], sv["y1"], vec(ln1_g, l), g1,
                                                       sv["merged"], "ln_bwd_mixer")
        small["ln1_g"][l], small["ln1_b"][l] = sg[0], sb[0]
        dya, dyb, dgab, gbuf["w_a"], gbuf["w_b"] = dmerge(
            do1, w["w_o"].T, sv["ya"][0], sv["yb"], w["w_a"], w["w_b"], w["w_a"].T, w["w_b"].T, sv["proj"], "dmerge")
        mixed = mix_bwd(dyb, sv["o_g"], sv["l_g"], "mix_bwd")
        do_g, dl_g = mixed[:N_GROUPS], mixed[N_GROUPS:]
        d_a, dsink = attn_bwd(sv["qkv_b"][0], sv["ya"], sv["lse_a"], dya.reshape(1, t, -1), None, ATTN_A,
                              sinks[l], "attn_a_bwd")
        small["sinks"][l] = dsink[:, 0]
        d_b = [attn_bwd(sv["qkv_b"][g], sv["o_g"][g], sv["l_g"][g], do_g[g], dl_g[g], cfg, sinks[l],
                        "attn_b%d_bwd" % g)[0] for g, cfg in enumerate(ATTN_B)]
        gw = wgrad(sv["u1"][0], d_a.reshape(t, A_W), fresh("w_in"), A_W, 1, 0, 0, 1, "wgrad_in_a")
        for g in range(N_GROUPS):
            gw = wgrad(sv["u1"][g], d_b[g].reshape(t, B_GW), gw, B_OUT_W, 3, 0, A_W // B_OUT_W + g, N_GROUPS,
                       "wgrad_in_b%d" % g)
        gbuf["w_in"] = wgrad(sv["u1"][0], dgab, gw, WGRAD_COLS, 2 * dm // WGRAD_COLS, 0, GATE_COL // WGRAD_COLS, 1,
                             "wgrad_in_gate")
        (dx, ss1, ssh1), got = dgrad_in(d_a, d_b, dgab, w["w_in"].T, dz1, sv["xin"], s1, "dgrad_in",
                                        scatter("mixer", gbuf))
        keep("mixer", l, got)
        small["dmod"][l] = jnp.stack([ssh1[0], ss1[0], sgate1[0], ssh2[0], ss2[0], sgate2[0]])
    grad_x = dx.reshape(x.shape)

    big_out = {}
    for nm, parts in recv.items():
        outs = adam_layers(parts, local(nm), local(nm, "m_"), local(nm, "v_"), "adam_" + nm)
        big_out[nm] = [o.transpose(0, 2, 1) for o in outs] if nm in TRANSPOSED else outs

    rows = jnp.concatenate(
        [jnp.stack(small["dmod"]).reshape(nl * 6, dm)]
        + [jnp.stack(small[k]) for k in ("ln1_g", "ln1_b", "ln2_g", "ln2_b")]
        + [jnp.pad(jnp.stack(small["sinks"]).reshape(1, -1), ((0, 0), (0, dm - nl * A_Q_HEADS))),
           jnp.broadcast_to(loss_part[0:1, 0:1], (1, dm))])
    n_rows = rows.shape[0]
    rows = jnp.pad(rows, ((0, -n_rows % 8), (0, 0)))
    rows_all = all_gather_small(rows, "gather_small_grads")

    def pack_small(pre):
        parts = [given[pre + "b_ada"].reshape(nl * 6, dm)]
        parts += [given[pre + k] for k in ("ln1_g", "ln1_b", "ln2_g", "ln2_b")]
        parts.append(jnp.pad(given[pre + "sinks"].reshape(1, -1), ((0, 0), (0, dm - nl * A_Q_HEADS))))
        p = jnp.concatenate(parts)
        return jnp.pad(p, ((0, rows.shape[0] - p.shape[0]), (0, 0)))

    souts = adam_reduce(rows_all, pack_small(""), pack_small("m_"), pack_small("v_"), "adam_small")

    def unpack_small(o):
        r = {"b_ada": o[0:nl * 6].reshape(nl, 6 * dm)}
        for j, k in enumerate(("ln1_g", "ln1_b", "ln2_g", "ln2_b")):
            r[k] = o[nl * 6 + j * nl: nl * 6 + (j + 1) * nl]
        r["sinks"] = o[nl * 10, 0:nl * A_Q_HEADS].reshape(nl, A_Q_HEADS)
        return r

    small_out = [unpack_small(o) for o in souts]
    loss = souts[0][nl * 10 + 1, 0]

    dmod_all = rows_all[:, 0:nl * 6].reshape(N_DEV, nl, 6 * dm)
    dm_loc = lax.dynamic_slice_in_dim(dmod_all, me * wc, wc, axis=2).transpose(1, 0, 2)
    dm_loc = jnp.pad(dm_loc, ((0, 0), (0, LANES - N_DEV), (0, 0)))
    sct = jnp.pad(sc_all.T, ((0, 0), (0, LANES - N_DEV)))
    ada_out = adam_w_ada(sct, dm_loc, w_ada, m_w_ada, v_w_ada)

    names = ["w_ada", "b_ada", "w_in", "sinks", "w_a", "w_b", "w_o", "ln1_g", "ln1_b",
             "w_gate", "w_up", "w_down", "ln2_g", "ln2_b"]

    def pick(kind, nm):
        if nm == "w_ada":
            return ada_out[kind]
        if nm in small_out[kind]:
            return small_out[kind][nm]
        return big_out[nm][kind]

    result = [loss, grad_x]
    for kind in range(4):
        result += [pick(kind, nm) for nm in names]
    return tuple(result)
```

```python
import jax
import jax.numpy as jnp
from jax import lax
from jax.experimental import pallas as pl
from jax.experimental.pallas import tpu as pltpu

F32 = jnp.float32
BF16 = jnp.bfloat16

D_MODEL = 1024
HEAD_DIM = 64
A_Q_HEADS = 8
A_KV_HEADS = 2
A_WINDOW = 128
B_GROUPS = ((128, 1), (512, 4), (2048, 16))
N_GROUPS = len(B_GROUPS)
B_HEADS_PER_GROUP = 4
N_ATTN_HEADS = A_Q_HEADS + B_HEADS_PER_GROUP * N_GROUPS
BLOCK = 128
ATTN_FWD_BLOCK = 1024
ATTN_BWD_BLOCK = 2048
A_W =(A_Q_HEADS + 2 * A_KV_HEADS) * HEAD_DIM
B_OUT_W = B_HEADS_PER_GROUP * HEAD_DIM
B_GW = 3 * B_OUT_W
B_ALL = N_GROUPS * B_OUT_W
GATE_COL = A_W + 3 * B_ALL
D_FF = 2816
COL_CHUNK = 256
FF_CHUNK = COL_CHUNK
DGRAD_CHUNK = COL_CHUNK
MM_CHUNK = 512
WGRAD_COLS = 512
ADAM_ROW_TILE = 256
DN_ALPHA = 8.0 ** 0.25
LN_EPS = 1e-5
NEG_INF = -1e30
ADAM_LR, ADAM_B1, ADAM_B2, ADAM_EPS, ADAM_WD, ADAM_STEP = 0.001, 0.9, 0.999, 1e-08, 0.01, 10

N_DEV = 8
MESH = pl.DeviceIdType.MESH
VMEM_LIMIT = 56 * 1024 * 1024
ROW_TILE = 512
WIDE_ROW_TILE = 1024
WGRAD_TILE_ELEMS = 2 * 1024 * 1024
LANES = 128
BF16_ROWS = 16


def _cp(*sem):
    return pltpu.CompilerParams(dimension_semantics=sem, vmem_limit_bytes=VMEM_LIMIT)


def _row_tile(t, rows=ROW_TILE):
    return min(rows, t)


def _slope(head):
    return 2.0 ** (-8.0 * (head + 1) / N_ATTN_HEADS)


def _sigmoid(x):
    return 1.0 / (1.0 + jnp.exp(-x))


def _dot(a, b):
    return jnp.dot(a, b, preferred_element_type=F32)


def _dot_nt(a, b):
    return lax.dot_general(a, b, (((1,), (1,)), ((), ())), preferred_element_type=F32)


def _dot_tn(a, b):
    return lax.dot_general(a, b, (((0,), (0,)), ((), ())), preferred_element_type=F32)


def _fold_scratch(tm, w):
    return [pltpu.VMEM((tm, LANES), F32)] * (w // LANES)


def _fold_to(dst_ref, val, scrs, d, col0=0):
    tm, w = val.shape
    if d == 1:
        dst_ref[0, :, col0:col0 + w] = val.astype(dst_ref.dtype)
        return
    for cb in range(w // LANES):
        scrs[cb][...] = val[:, cb * LANES:(cb + 1) * LANES]
    for r in range(d):
        for cb in range(w // LANES):
            piece = scrs[cb][pl.ds(r, tm // d, stride=d), :]
            dst_ref[r, :, col0 + cb * LANES:col0 + (cb + 1) * LANES] = piece.astype(dst_ref.dtype)


def _unfold_rows(rows_of, scrs, d, n, w):
    for r in range(d):
        for cb in range(w // LANES):
            scrs[cb][pl.ds(r, n, stride=d), :] = rows_of(r, slice(cb * LANES, (cb + 1) * LANES)).astype(F32)
    return jnp.concatenate([scrs[cb][0:d * n, :] for cb in range(w // LANES)], axis=1)


def _unfold_from(src_ref, scrs, d):
    if d == 1:
        return src_ref[0].astype(F32)
    _, n, w = src_ref.shape
    return _unfold_rows(lambda r, cols: src_ref[r, :, cols], scrs, d, n, w)


def _folded_spec(d, tm, w):
    return pl.BlockSpec((d, tm // d, w), lambda i: (0, i, 0))


def _me():
    return lax.axis_index("x"), lax.axis_index("y"), lax.axis_index("c")


def _flip(v, bit):
    return 1 - v if bit else v


def _peer(k):
    x, y, c = _me()
    return (_flip(x, k & 4), _flip(y, k & 2), _flip(c, k & 1))


def _peer_index(k):
    px, py, pc = _peer(k)
    return 4 * px + 2 * py + pc


def all_gather_small(v, name):
    r, c = v.shape

    def body(v_ref, out_ref, send_sems, recv_sems):
        me = _peer_index(0)
        out_ref[me] = v_ref[...]
        copies = []
        for k in range(1, N_DEV):
            cp = pltpu.make_async_remote_copy(
                src_ref=v_ref, dst_ref=out_ref.at[me],
                send_sem=send_sems.at[k - 1], recv_sem=recv_sems.at[k - 1],
                device_id=_peer(k), device_id_type=MESH)
            cp.start()
            copies.append(cp)
        for k in range(1, N_DEV):
            pltpu.make_async_remote_copy(
                src_ref=v_ref, dst_ref=out_ref.at[_peer_index(k)],
                send_sem=send_sems.at[k - 1], recv_sem=recv_sems.at[k - 1],
                device_id=_peer(k), device_id_type=MESH).wait_recv()
        for cp in copies:
            cp.wait_send()

    return pl.pallas_call(
        body, name=name,
        out_shape=jax.ShapeDtypeStruct((N_DEV, r, c), v.dtype),
        in_specs=[pl.BlockSpec(memory_space=pltpu.VMEM)],
        out_specs=pl.BlockSpec(memory_space=pltpu.VMEM),
        scratch_shapes=[pltpu.SemaphoreType.DMA((N_DEV - 1,)), pltpu.SemaphoreType.DMA((N_DEV - 1,))],
        compiler_params=pltpu.CompilerParams(vmem_limit_bytes=VMEM_LIMIT),
    )(v)


class Piece:
    def __init__(self, name, buf, axis, base, size):
        self.name, self.buf, self.axis, self.base, self.size = name, buf, axis, base, size

    def window(self, ref, j):
        start = self.base + j * self.size
        if self.axis == 1:
            return ref.at[:, pl.ds(pl.multiple_of(start, LANES), self.size)]
        return ref.at[pl.ds(pl.multiple_of(start, BF16_ROWS), self.size), :]


class Exchange:
    def __init__(self, kind, pieces, ins, out_shapes, bufs):
        self.kind, self.pieces, self.ins, self.out_shapes = kind, pieces, list(ins), list(out_shapes)
        self.buf_of = {nm: i for i, nm in enumerate(bufs)}
        self.n_in, self.n_out = len(self.ins), len(self.out_shapes)
        n = len(pieces)
        self.scratch = [pltpu.SemaphoreType.DMA((n, N_DEV - 1)), pltpu.SemaphoreType.DMA((n, N_DEV - 1)),
                        pltpu.SemaphoreType.DMA((n,))]
        self.in_specs = [pl.BlockSpec(memory_space=pl.ANY)] * self.n_in
        self.out_specs = [pl.BlockSpec(memory_space=pl.ANY)] * self.n_out
        self.out_shape = [jax.ShapeDtypeStruct(s, BF16) for s in self.out_shapes]

    def _ends(self, pi, ins, outs, to):
        pc = self.pieces[pi]
        if self.kind == "gather":
            return ins[pi], pc.window(outs[self.buf_of[pc.buf]], _peer_index(0))
        return pc.window(ins[self.buf_of[pc.buf]], to), outs[pi].at[_peer_index(0)]

    def _landing(self, pi, outs, frm):
        pc = self.pieces[pi]
        if self.kind == "gather":
            return pc.window(outs[self.buf_of[pc.buf]], frm)
        return outs[pi].at[frm]

    def _remote(self, pi, k, src, dst, sems):
        return pltpu.make_async_remote_copy(
            src_ref=src, dst_ref=dst, send_sem=sems[0].at[pi, k - 1], recv_sem=sems[1].at[pi, k - 1],
            device_id=_peer(k), device_id_type=MESH)

    def _local(self, pi, ins, outs, sems):
        return pltpu.make_async_copy(*self._ends(pi, ins, outs, _peer_index(0)), sems[2].at[pi])

    def start(self, ins, outs, sems):
        for pi in range(len(self.pieces)):
            self._local(pi, ins, outs, sems).start()
            for k in range(1, N_DEV):
                self._remote(pi, k, *self._ends(pi, ins, outs, _peer_index(k)), sems).start()

    def finish(self, ins, outs, sems):
        for pi in range(len(self.pieces)):
            src_like = self._ends(pi, ins, outs, _peer_index(0))[0]
            for k in range(1, N_DEV):
                self._remote(pi, k, src_like, self._landing(pi, outs, _peer_index(k)), sems).wait_recv()
        for pi in range(len(self.pieces)):
            for k in range(1, N_DEV):
                self._remote(pi, k, *self._ends(pi, ins, outs, _peer_index(k)), sems).wait_send()
            self._local(pi, ins, outs, sems).wait()


def _hosted(ex, refs, n_in, n_out, first, last):
    if ex is None:
        return refs
    ins, rest = refs[:n_in], refs[n_in:]
    ex_ins, rest = rest[:ex.n_in], rest[ex.n_in:]
    outs, rest = rest[:n_out], rest[n_out:]
    ex_outs, rest = rest[:ex.n_out], rest[ex.n_out:]
    scr, sems = rest[:len(rest) - 3], rest[len(rest) - 3:]
    pl.when(first)(lambda: ex.start(ex_ins, ex_outs, sems))
    pl.when(last)(lambda: ex.finish(ex_ins, ex_outs, sems))
    return tuple(ins) + tuple(outs) + tuple(scr)


def _host_call(body, ex, *, name, grid, out_shape, in_specs, out_specs, scratch_shapes=(), sem=None, args):
    n_out = len(out_shape)
    if ex is not None:
        out_shape = list(out_shape) + ex.out_shape
        in_specs = list(in_specs) + ex.in_specs
        out_specs = list(out_specs) + ex.out_specs
        scratch_shapes = list(scratch_shapes) + ex.scratch
        args = list(args) + ex.ins
    res = pl.pallas_call(body, name=name, grid=grid, out_shape=out_shape, in_specs=in_specs, out_specs=out_specs,
                         scratch_shapes=scratch_shapes, compiler_params=_cp(*sem))(*args)
    return res[:n_out], res[n_out:]


def run_exchange(ex, name):
    def body(*refs):
        ins, outs, sems = refs[:ex.n_in], refs[ex.n_in:ex.n_in + ex.n_out], refs[ex.n_in + ex.n_out:]
        ex.start(ins, outs, sems)
        ex.finish(ins, outs, sems)

    return pl.pallas_call(body, name=name, out_shape=ex.out_shape, in_specs=ex.in_specs, out_specs=ex.out_specs,
                          scratch_shapes=ex.scratch)(*ex.ins)


def mod_partial(c_all, w_ada, b_loc):
    nl, dm, wc = w_ada.shape

    def body(c_ref, w_ref, b_ref, o_ref, sc_ref):
        cc = c_ref[...]
        sc = cc * _sigmoid(cc)
        sc_ref[...] = sc
        o_ref[...] = jnp.dot(sc, w_ref[...], preferred_element_type=F32,
                             precision=lax.Precision.HIGHEST) + b_ref[...]

    return pl.pallas_call(
        body, name="mod_partial", grid=(nl,),
        out_shape=[jax.ShapeDtypeStruct((nl, N_DEV, wc), F32), jax.ShapeDtypeStruct((N_DEV, dm), F32)],
        in_specs=[pl.BlockSpec((N_DEV, dm), lambda l: (0, 0)),
                  pl.BlockSpec((None, dm, wc), lambda l: (l, 0, 0)),
                  pl.BlockSpec((None, 1, wc), lambda l: (l, 0, 0))],
        out_specs=[pl.BlockSpec((None, N_DEV, wc), lambda l: (l, 0, 0)),
                   pl.BlockSpec((N_DEV, dm), lambda l: (0, 0))],
        compiler_params=_cp("arbitrary"),
    )(c_all, w_ada, b_loc)


def _stream_specs(tm, dm):
    vec = pl.BlockSpec((1, dm), lambda i: (0, 0))
    return [pl.BlockSpec((tm, dm), lambda i: (i, 0)), vec, vec]


def _stream(zh_ref, lg_ref, lb_ref, rows=slice(None), cols=slice(None)):
    return zh_ref[rows, cols] * lg_ref[:, cols] + lb_ref[:, cols]


def in_proj(x, s, sh, w, name, ex=None):
    t, dm = x[0].shape
    n = w.shape[1]
    tm = _row_tile(t)
    nsteps = t // tm
    ch = B_OUT_W
    dils = [dil for _, dil in B_GROUPS if dil > 1]

    def body(*refs):
        i = pl.program_id(0)
        zh_ref, lg_ref, lb_ref, s_ref, sh_ref, w_ref, u_ref, *rest = _hosted(
            ex, refs, 6, 2 + 2 * len(dils), i == 0, i == nsteps - 1)
        uf_refs, o_ref, qf_refs = rest[:len(dils)], rest[len(dils)], rest[len(dils) + 1:len(dils) * 2 + 1]
        scrs = rest[len(dils) * 2 + 1:]
        uf = _stream(zh_ref, lg_ref, lb_ref) * (1.0 + s_ref[...]) + sh_ref[...]
        u = uf.astype(BF16)
        u_ref[...] = u
        for d, uf_ref in zip(dils, uf_refs):
            _fold_to(uf_ref, uf, scrs, d)
        for c0 in range(0, n, ch):
            res = _dot(u, w_ref[:, c0:c0 + ch])
            o_ref[:, c0:c0 + ch] = res.astype(BF16)
            if A_W <= c0 < GATE_COL:
                part, g = divmod((c0 - A_W) // ch, N_GROUPS)
                d = B_GROUPS[g][1]
                if d > 1:
                    _fold_to(qf_refs[dils.index(d)], res, scrs, d, part * ch)

    vec = pl.BlockSpec((1, dm), lambda i: (0, 0))
    row = lambda w_: pl.BlockSpec((tm, w_), lambda i: (i, 0))
    return _host_call(
        body, ex, name=name, grid=(nsteps,),
        out_shape=[jax.ShapeDtypeStruct((t, dm), BF16)]
                  + [jax.ShapeDtypeStruct((d, t // d, dm), BF16) for d in dils]
                  + [jax.ShapeDtypeStruct((t, n), BF16)]
                  + [jax.ShapeDtypeStruct((d, t // d, B_GW), BF16) for d in dils],
        in_specs=_stream_specs(tm, dm) + [vec, vec, pl.BlockSpec((dm, n), lambda i: (0, 0))],
        out_specs=[row(dm)] + [_folded_spec(d, tm, dm) for d in dils] + [row(n)]
                  + [_folded_spec(d, tm, B_GW) for d in dils],
        scratch_shapes=_fold_scratch(tm, dm), sem=("arbitrary",), args=[*x, s, sh, w])


def modmm(x, s, sh, w, name, ex=None):
    t, dm = x[0].shape
    n = w.shape[1]
    tm = _row_tile(t)
    nsteps = t // tm
    ch = MM_CHUNK

    def body(*refs):
        i = pl.program_id(0)
        zh_ref, lg_ref, lb_ref, s_ref, sh_ref, w_ref, u_ref, o_ref = _hosted(ex, refs, 6, 2, i == 0, i == nsteps - 1)
        u = (_stream(zh_ref, lg_ref, lb_ref) * (1.0 + s_ref[...]) + sh_ref[...]).astype(BF16)
        u_ref[...] = u
        for c0 in range(0, n, ch):
            o_ref[:, c0:c0 + ch] = _dot(u, w_ref[:, c0:c0 + ch]).astype(BF16)

    vec = pl.BlockSpec((1, dm), lambda i: (0, 0))
    return _host_call(
        body, ex, name=name, grid=(nsteps,),
        out_shape=[jax.ShapeDtypeStruct((t, dm), BF16), jax.ShapeDtypeStruct((t, n), BF16)],
        in_specs=_stream_specs(tm, dm) + [vec, vec, pl.BlockSpec((dm, n), lambda i: (0, 0))],
        out_specs=[pl.BlockSpec((tm, dm), lambda i: (i, 0)), pl.BlockSpec((tm, n), lambda i: (i, 0))],
        sem=("arbitrary",), args=[*x, s, sh, w])


def _halves(tm):
    half = tm // 2 if tm % 32 == 0 else tm
    return [slice(r0, r0 + half) for r0 in range(0, tm, half)]


def _ln_store(y, rows, xres_refs, g_ref, y_ref, zh_ref, rs_ref):
    y_ref[rows, :] = y.astype(BF16)
    z = DN_ALPHA * _stream(*xres_refs, rows=rows) + g_ref[...] * y
    mu = jnp.mean(z, axis=1, keepdims=True)
    zc = z - mu
    var = jnp.mean(zc * zc, axis=1, keepdims=True)
    rstd = lax.rsqrt(var + LN_EPS)
    zh_ref[rows, :] = zc * rstd
    rs_ref[rows, :] = jnp.broadcast_to(rstd, (zc.shape[0], rs_ref.shape[1]))


def _ln_out_shapes(t, dm):
    return [jax.ShapeDtypeStruct((t, dm), BF16), jax.ShapeDtypeStruct((t, dm), F32),
            jax.ShapeDtypeStruct((t, LANES), F32)]


def _ln_out_specs(tm, dm):
    row = pl.BlockSpec((tm, dm), lambda i: (i, 0))
    return [row, row, pl.BlockSpec((tm, LANES), lambda i: (i, 0))]


def proj_ln(a, w, xres, gate, name):
    t, k = a.shape
    dm = w.shape[1]
    tm = _row_tile(t, WIDE_ROW_TILE)

    def body(a_ref, w_ref, xz_ref, xg_ref, xb_ref, g_ref, y_ref, zh_ref, rs_ref):
        for rows in _halves(tm):
            y = _dot(a_ref[rows, :], w_ref[...])
            _ln_store(y, rows, (xz_ref, xg_ref, xb_ref), g_ref, y_ref, zh_ref, rs_ref)

    vec = pl.BlockSpec((1, dm), lambda i: (0, 0))
    return pl.pallas_call(
        body, name=name, grid=(t // tm,),
        out_shape=_ln_out_shapes(t, dm),
        in_specs=[pl.BlockSpec((tm, k), lambda i: (i, 0)), pl.BlockSpec((k, dm), lambda i: (0, 0))]
                 + _stream_specs(tm, dm) + [vec],
        out_specs=_ln_out_specs(tm, dm),
        compiler_params=_cp("parallel"),
    )(a, w, *xres, gate)


def swiglu_proj_ln(ab, w, xres, gate, name):
    t = ab.shape[0]
    f, dm = w.shape
    tm = _row_tile(t)

    def body(a_ref, b_ref, w_ref, xz_ref, xg_ref, xb_ref, g_ref, h_ref, y_ref, zh_ref, rs_ref):
        for rows in _halves(tm):
            y = None
            for c0 in range(0, f, FF_CHUNK):
                cols = slice(c0, c0 + FF_CHUNK)
                a = a_ref[rows, cols].astype(F32)
                h = (a * _sigmoid(a) * b_ref[rows, cols].astype(F32)).astype(BF16)
                h_ref[rows, cols] = h
                part = _dot(h, w_ref[cols, :])
                y = part if y is None else y + part
            _ln_store(y, rows, (xz_ref, xg_ref, xb_ref), g_ref, y_ref, zh_ref, rs_ref)

    vec = pl.BlockSpec((1, dm), lambda i: (0, 0))
    return pl.pallas_call(
        body, name=name, grid=(t // tm,),
        out_shape=[jax.ShapeDtypeStruct((t, f), BF16)] + _ln_out_shapes(t, dm),
        in_specs=[pl.BlockSpec((tm, f), lambda i: (i, 0)), pl.BlockSpec((tm, f), lambda i: (i, 1)),
                  pl.BlockSpec((f, dm), lambda i: (0, 0))] + _stream_specs(tm, dm) + [vec],
        out_specs=[pl.BlockSpec((tm, f), lambda i: (i, 0))] + _ln_out_specs(tm, dm),
        compiler_params=_cp("parallel"),
    )(ab, ab, w, *xres, gate)


class AttnCfg:
    def __init__(self, dil, heads, kv_heads, qc, kc, vc, max_dist, head0, sinks):
        self.dil, self.heads, self.kv_heads = dil, heads, kv_heads
        self.qc, self.kc, self.vc = qc, kc, vc
        self.max_dist, self.head0, self.sinks = max_dist, head0, sinks
        self.wq = heads * HEAD_DIM
        self.wk = kv_heads * HEAD_DIM
        self.wout = self.wq + 2 * self.wk


ATTN_A = AttnCfg(1, A_Q_HEADS, A_KV_HEADS, 0, A_Q_HEADS * HEAD_DIM, (A_Q_HEADS + A_KV_HEADS) * HEAD_DIM,
                 A_WINDOW - 1, 0, True)


def _attn_b_cfg(g):
    win, dil = B_GROUPS[g]
    cols = ((A_W + g * B_OUT_W, A_W + B_ALL + g * B_OUT_W, A_W + 2 * B_ALL + g * B_OUT_W) if dil == 1
            else (0, B_OUT_W, 2 * B_OUT_W))
    return AttnCfg(dil, B_HEADS_PER_GROUP, B_HEADS_PER_GROUP, *cols, win // dil,
                   A_Q_HEADS + g * B_HEADS_PER_GROUP, False)


ATTN_B = [_attn_b_cfg(g) for g in range(N_GROUPS)]


SCALE = HEAD_DIM ** -0.5


def _head(h):
    return slice(h * HEAD_DIM, (h + 1) * HEAD_DIM)


def _masked_bias(mask, distf, cfg, h, d):
    return jnp.where(mask, distf * (-(_slope(cfg.head0 + h) * d)), NEG_INF)


def _lane_halves(rows):
    lane = lax.broadcasted_iota(jnp.int32, (rows, LANES), 1)
    return [lane < HEAD_DIM, lane >= HEAD_DIM]


def _n_pair_sources(cfg):
    return cfg.kv_heads if cfg.heads > cfg.kv_heads else cfg.heads // 2


def _pair_source(p, grp):
    return p if grp == 1 else (2 * p) // grp


def _fill_pairs(dst, prev_ref, cur_ref, grp):
    for j in range(dst.shape[0]):
        for ref, rows in ((prev_ref, slice(0, BLOCK)), (cur_ref, slice(BLOCK, dst.shape[1]))):
            if grp == 1:
                dst[j, rows, :] = ref[:, j * LANES:(j + 1) * LANES]
            else:
                one = ref[:, _head(j)]
                dst[j, rows, :] = jnp.concatenate([one, one], axis=1)


def _band(i, max_dist):
    qi = lax.broadcasted_iota(jnp.int32, (BLOCK, 2 * BLOCK), 0)
    sj = lax.broadcasted_iota(jnp.int32, (BLOCK, 2 * BLOCK), 1)
    dist = qi + BLOCK - sj
    valid = (dist >= 0) & (dist <= max_dist)
    first_key = jnp.where(i > 0, 0, BLOCK)
    valid_first = valid & (sj >= first_key)
    return dist, valid, valid_first


def _attn_geometry(n, block):
    tq = min(block, n)
    return tq, tq // BLOCK, n // tq


def attn_fwd(qkv, cfg, sinks, name):
    d, n, _ = qkv.shape
    tq, nsub, nqb = _attn_geometry(n, ATTN_FWD_BLOCK)
    wq, wk = cfg.wq, cfg.wk
    grp = cfg.heads // cfg.kv_heads

    def body(sink_ref, q_ref, kc_ref, kp_ref, vc_ref, vp_ref, o_ref, l_ref, kf, vf):
        i = pl.program_id(1)
        _fill_pairs(kf, kp_ref, kc_ref, grp)
        _fill_pairs(vf, vp_ref, vc_ref, grp)
        dist, valid, valid_first = _band(i, cfg.max_dist)
        distf = dist.astype(F32)
        half_q, half_k = _lane_halves(BLOCK), _lane_halves(2 * BLOCK)
        rows = [slice(a * BLOCK, (a + 1) * BLOCK) for a in range(nsub)]
        wins = [slice(a * BLOCK, (a + 2) * BLOCK) for a in range(nsub)]
        for p in range(cfg.heads // 2):
            lanes = slice(p * LANES, (p + 1) * LANES)
            ki = _pair_source(p, grp)
            hs = (2 * p, 2 * p + 1)
            b_reg = [_masked_bias(valid, distf, cfg, h, d) for h in hs]
            b_first = [_masked_bias(valid_first, distf, cfg, h, d) for h in hs]
            ss = []
            for a in range(nsub):
                q2 = q_ref[rows[a], lanes] * SCALE
                k2 = kf[ki, wins[a], :]
                ss.append([_dot_nt(jnp.where(half_q[e], q2, 0), k2) + (b_first[e] if a == 0 else b_reg[e])
                           for e in range(2)])
            es, invs, lses = [], [], []
            for a in range(nsub):
                e_a, inv_a, lse_a = [], [], []
                for e in range(2):
                    m = jnp.max(ss[a][e], axis=1, keepdims=True)
                    if cfg.sinks:
                        m = jnp.maximum(m, sink_ref[hs[e]])
                    ex = jnp.exp(ss[a][e] - m)
                    den = jnp.sum(ex, axis=1, keepdims=True)
                    if cfg.sinks:
                        den = den + jnp.exp(sink_ref[hs[e]] - m)
                    e_a.append(ex.astype(BF16))
                    inv_a.append(1.0 / den)
                    lse_a.append(m + jnp.log(den))
                es.append(e_a)
                invs.append(inv_a)
                lses.append(lse_a)
            for a in range(nsub):
                v2 = vf[ki, wins[a], :]
                pcat = jnp.concatenate(es[a], axis=1)
                vcat = jnp.concatenate([jnp.where(half_k[e], v2, 0) for e in range(2)], axis=0)
                o = _dot(pcat, vcat) * jnp.where(half_q[0], invs[a][0], invs[a][1])
                o_ref[rows[a], lanes] = o.astype(BF16)
                l_ref[rows[a], lanes] = jnp.where(half_q[0], lses[a][0], lses[a][1])

    prev = lambda i: jnp.maximum(i * nsub - 1, 0)
    cur = lambda w, c: pl.BlockSpec((None, tq, w), lambda r, i: (r, i, c // w))
    prv = lambda w, c: pl.BlockSpec((None, BLOCK, w), lambda r, i: (r, prev(i), c // w))
    out = pl.BlockSpec((None, tq, wq), lambda r, i: (r, i, 0))
    pair_scratch = pltpu.VMEM((_n_pair_sources(cfg), tq + BLOCK, LANES), BF16)
    return pl.pallas_call(
        body, name=name, grid=(d, nqb),
        out_shape=[jax.ShapeDtypeStruct((d, n, wq), BF16), jax.ShapeDtypeStruct((d, n, wq), F32)],
        in_specs=[pl.BlockSpec(memory_space=pltpu.SMEM),
                  cur(wq, cfg.qc), cur(wk, cfg.kc), prv(wk, cfg.kc), cur(wk, cfg.vc), prv(wk, cfg.vc)],
        out_specs=[out, out],
        scratch_shapes=[pair_scratch, pair_scratch],
        compiler_params=_cp("parallel", "parallel"),
    )(sinks, qkv, qkv, qkv, qkv, qkv)


def mix_merge(ya, o_g, l_g, proj, w_a, w_b, name):
    t = ya.shape[0]
    dm = w_a.shape[1]
    tm = _row_tile(t, WIDE_ROW_TILE)
    gcol = GATE_COL // dm
    dils = [o.shape[0] for o in o_g]

    def body(ya_ref, o0, o1, o2, l0, l1, l2, ga_ref, gb_ref, wa_ref, wb_ref, yb_ref, mg_ref, *scrs):
        ls = [_unfold_from(l, scrs, d) for l, d in zip((l0, l1, l2), dils)]
        m = jnp.maximum(jnp.maximum(ls[0], ls[1]), ls[2])
        es = [jnp.exp(l - m) for l in ls]
        inv = 1.0 / (es[0] + es[1] + es[2])
        yb = sum(_unfold_from(o, scrs, d) * (e * inv) for o, e, d in zip((o0, o1, o2), es, dils)).astype(BF16)
        yb_ref[...] = yb
        pa = _dot(ya_ref[...], wa_ref[...])
        pb = _dot(yb, wb_ref[...])
        mg = _sigmoid(ga_ref[...].astype(F32)) * pa + _sigmoid(gb_ref[...].astype(F32)) * pb
        mg_ref[...] = mg.astype(BF16)

    wide = lambda w: pl.BlockSpec((tm, w), lambda i: (i, 0))
    folded = [_folded_spec(d, tm, B_OUT_W) for d in dils]
    return pl.pallas_call(
        body, name=name, grid=(t // tm,),
        out_shape=[jax.ShapeDtypeStruct((t, B_OUT_W), BF16), jax.ShapeDtypeStruct((t, dm), BF16)],
        in_specs=[wide(ya.shape[1])] + folded + folded
                 + [pl.BlockSpec((tm, dm), lambda i: (i, gcol)), pl.BlockSpec((tm, dm), lambda i: (i, gcol + 1)),
                    pl.BlockSpec(w_a.shape, lambda i: (0, 0)), pl.BlockSpec(w_b.shape, lambda i: (0, 0))],
        out_specs=[wide(B_OUT_W), wide(dm)],
        scratch_shapes=_fold_scratch(tm, B_OUT_W),
        compiler_params=_cp("parallel"),
    )(ya, *o_g, *l_g, proj, proj, w_a, w_b)


def loss_head(y, target):
    t, dm = y[0].shape
    tm = _row_tile(t, WIDE_ROW_TILE)

    def body(zh_ref, lg_ref, lb_ref, t_ref, dy_ref, loss_ref):
        @pl.when(pl.program_id(0) == 0)
        def _():
            loss_ref[...] = jnp.zeros_like(loss_ref)
        err = _stream(zh_ref, lg_ref, lb_ref) - t_ref[...]
        dy_ref[...] = err * (1.0 / dm)
        per_row = jnp.sum(err * err, axis=1, keepdims=True) * (1.0 / dm)
        loss_ref[...] += 0.5 * jnp.sum(per_row, axis=0, keepdims=True)

    row = pl.BlockSpec((tm, dm), lambda i: (i, 0))
    return pl.pallas_call(
        body, name="loss_head", grid=(t // tm,),
        out_shape=[jax.ShapeDtypeStruct((t, dm), F32), jax.ShapeDtypeStruct((8, LANES), F32)],
        in_specs=_stream_specs(tm, dm) + [row],
        out_specs=[row, pl.BlockSpec((8, LANES), lambda i: (0, 0))],
        compiler_params=_cp("arbitrary"),
    )(*y, target)


def _fold_rows(v):
    tm, c = v.shape
    return jnp.sum(v.reshape(tm // 8, 8, c), axis=0)


def _finish_sums(refs, nsteps):
    @pl.when(pl.program_id(0) == nsteps - 1)
    def _():
        for r in refs:
            r[...] = jnp.broadcast_to(jnp.sum(r[...], axis=0, keepdims=True), r.shape)


def ln_bwd(dxo, zhat, rstd, ysub, lg, gate, act, name):
    t, dm = dxo.shape
    k = act.shape[1]
    tm = _row_tile(t, WIDE_ROW_TILE if k <= dm else ROW_TILE)
    nsteps = t // tm
    ch = COL_CHUNK

    def body(dxo_ref, zh_ref, rs_ref, y_ref, lg_ref, g_ref, a_ref, dz_ref, dy_ref, sg_ref, sb_ref, sgate_ref,
             gw_ref, acc):
        @pl.when(pl.program_id(0) == 0)
        def _():
            for r in (sg_ref, sb_ref, sgate_ref, acc):
                r[...] = jnp.zeros_like(r)
        for rows in _halves(tm):
            dxo_v = dxo_ref[rows, :]
            zh = zh_ref[rows, :]
            dxh = dxo_v * lg_ref[...]
            m1 = jnp.mean(dxh, axis=1, keepdims=True)
            m2 = jnp.mean(dxh * zh, axis=1, keepdims=True)
            dz = rs_ref[rows, 0:1] * (dxh - m1 - zh * m2)
            dz_ref[rows, :] = dz
            dy = (g_ref[...] * dz).astype(BF16)
            dy_ref[rows, :] = dy
            sg_ref[...] += _fold_rows(dxo_v * zh)
            sb_ref[...] += _fold_rows(dxo_v)
            sgate_ref[...] += _fold_rows(dz * y_ref[rows, :].astype(F32))
            a = a_ref[rows, :]
            for c0 in range(0, dm, ch):
                acc[:, c0:c0 + ch] += _dot_tn(a, dy[:, c0:c0 + ch])
        _finish_sums((sg_ref, sb_ref, sgate_ref), nsteps)

        @pl.when(pl.program_id(0) == nsteps - 1)
        def _():
            gw_ref[...] = acc[...].astype(BF16)

    row = pl.BlockSpec((tm, dm), lambda i: (i, 0))
    vec = pl.BlockSpec((1, dm), lambda i: (0, 0))
    sums = pl.BlockSpec((8, dm), lambda i: (0, 0))
    return pl.pallas_call(
        body, name=name, grid=(nsteps,),
        out_shape=[jax.ShapeDtypeStruct((t, dm), F32), jax.ShapeDtypeStruct((t, dm), BF16)]
                  + [jax.ShapeDtypeStruct((8, dm), F32)] * 3 + [jax.ShapeDtypeStruct((k, dm), BF16)],
        in_specs=[row, row, pl.BlockSpec((tm, LANES), lambda i: (i, 0)), row, vec, vec,
                  pl.BlockSpec((tm, k), lambda i: (i, 0))],
        out_specs=[row, row, sums, sums, sums, pl.BlockSpec((k, dm), lambda i: (0, 0))],
        scratch_shapes=[pltpu.VMEM((k, dm), F32)],
        compiler_params=_cp("arbitrary"),
    )(dxo, zhat, rstd, ysub, lg, gate, act)


def _mod_bwd_store(du_of, dz_ref, x_refs, s_ref, dx_ref, ss_ref, ssh_ref, nsteps):
    @pl.when(pl.program_id(0) == 0)
    def _():
        ss_ref[...] = jnp.zeros_like(ss_ref)
        ssh_ref[...] = jnp.zeros_like(ssh_ref)
    for c0 in range(0, dx_ref.shape[1], DGRAD_CHUNK):
        cols = slice(c0, c0 + DGRAD_CHUNK)
        du = du_of(cols)
        dx_ref[:, cols] = DN_ALPHA * dz_ref[:, cols] + du * (1.0 + s_ref[:, cols])
        ss_ref[:, cols] += _fold_rows(du * _stream(*x_refs, cols=cols))
        ssh_ref[:, cols] += _fold_rows(du)
    _finish_sums((ss_ref, ssh_ref), nsteps)


def dgrad_ffn(g, wt, dz, xin, s, name, ex=None):
    t, dm = dz.shape
    k = g.shape[1]
    tm = _row_tile(t)
    nsteps = t // tm

    def body(*refs):
        i = pl.program_id(0)
        (g_ref, w_ref, dz_ref, xz_ref, xg_ref, xb_ref, s_ref,
         dx_ref, ss_ref, ssh_ref) = _hosted(ex, refs, 7, 3, i == 0, i == nsteps - 1)
        g_v = g_ref[...]
        _mod_bwd_store(lambda cols: _dot(g_v, w_ref[:, cols]), dz_ref, (xz_ref, xg_ref, xb_ref), s_ref,
                       dx_ref, ss_ref, ssh_ref, nsteps)

    row = pl.BlockSpec((tm, dm), lambda i: (i, 0))
    acc = pl.BlockSpec((8, dm), lambda i: (0, 0))
    return _host_call(
        body, ex, name=name, grid=(nsteps,),
        out_shape=[jax.ShapeDtypeStruct((t, dm), F32)] + [jax.ShapeDtypeStruct((8, dm), F32)] * 2,
        in_specs=[pl.BlockSpec((tm, k), lambda i: (i, 0)), pl.BlockSpec((k, dm), lambda i: (0, 0)), row]
                 + _stream_specs(tm, dm) + [pl.BlockSpec((1, dm), lambda i: (0, 0))],
        out_specs=[row, acc, acc], sem=("arbitrary",), args=[g, wt, dz, *xin, s])


def dswiglu(dy, wdt, ab, name):
    t, dm = dy.shape
    f = wdt.shape[1]
    tm = _row_tile(t)

    def body(dy_ref, w_ref, a_ref, b_ref, o_ref):
        dy_v = dy_ref[...]
        for c0 in range(0, f, FF_CHUNK):
            cols = slice(c0, c0 + FF_CHUNK)
            dh = _dot(dy_v, w_ref[:, cols])
            a = a_ref[:, cols].astype(F32)
            sg = _sigmoid(a)
            o_ref[:, cols] = (dh * b_ref[:, cols].astype(F32) * (sg * (1.0 + a * (1.0 - sg)))).astype(BF16)
            o_ref[:, f + c0:f + c0 + FF_CHUNK] = (dh * (a * sg)).astype(BF16)

    return pl.pallas_call(
        body, name=name, grid=(t // tm,),
        out_shape=jax.ShapeDtypeStruct((t, 2 * f), BF16),
        in_specs=[pl.BlockSpec((tm, dm), lambda i: (i, 0)), pl.BlockSpec((dm, f), lambda i: (0, 0)),
                  pl.BlockSpec((tm, f), lambda i: (i, 0)), pl.BlockSpec((tm, f), lambda i: (i, 1))],
        out_specs=pl.BlockSpec((tm, 2 * f), lambda i: (i, 0)),
        compiler_params=_cp("parallel"),
    )(dy, wdt, ab, ab)


def dgrad_in(d_a, d_b, dgab, wt, dz, xin, s, name, ex=None):
    t, dm = dz.shape
    tm = _row_tile(t)
    nsteps = t // tm
    dils = [a.shape[0] for a in d_b]

    def body(*refs):
        i = pl.program_id(0)
        (da_ref, b0, b1, b2, dg_ref, w_ref, dz_ref, xz_ref, xg_ref, xb_ref, s_ref, dx_ref, ss_ref, ssh_ref,
         *scrs) = _hosted(ex, refs, 11, 3, i == 0, i == nsteps - 1)
        vs = [b_ref[...].reshape(tm, B_GW) for b_ref in (b0, b1, b2)]

        def du_of(cols):
            du = _dot(da_ref[0], w_ref[0:A_W, cols])
            for g, (v, d) in enumerate(zip(vs, dils)):
                part = None
                for p in range(3):
                    r0 = A_W + p * B_ALL + g * B_OUT_W
                    term = _dot(v[:, p * B_OUT_W:(p + 1) * B_OUT_W], w_ref[r0:r0 + B_OUT_W, cols])
                    part = term if part is None else part + term
                if d == 1:
                    du = du + part
                else:
                    n = tm // d
                    du = du + _unfold_rows(lambda r, cs: part[r * n:(r + 1) * n, cs], scrs, d, n, DGRAD_CHUNK)
            for j in range(2):
                du = du + _dot(dg_ref[:, j * dm:(j + 1) * dm], w_ref[GATE_COL + j * dm:GATE_COL + (j + 1) * dm, cols])
            return du

        _mod_bwd_store(du_of, dz_ref, (xz_ref, xg_ref, xb_ref), s_ref, dx_ref, ss_ref, ssh_ref, nsteps)

    row = pl.BlockSpec((tm, dm), lambda i: (i, 0))
    acc = pl.BlockSpec((8, dm), lambda i: (0, 0))
    return _host_call(
        body, ex, name=name, grid=(nsteps,),
        out_shape=[jax.ShapeDtypeStruct((t, dm), F32)] + [jax.ShapeDtypeStruct((8, dm), F32)] * 2,
        in_specs=[_folded_spec(1, tm, A_W)] + [_folded_spec(d, tm, B_GW) for d in dils]
                 + [pl.BlockSpec((tm, 2 * dm), lambda i: (i, 0)), pl.BlockSpec(wt.shape, lambda i: (0, 0)), row]
                 + _stream_specs(tm, dm) + [pl.BlockSpec((1, dm), lambda i: (0, 0))],
        out_specs=[row, acc, acc],
        scratch_shapes=_fold_scratch(tm, DGRAD_CHUNK), sem=("arbitrary",), args=[d_a, *d_b, dgab, wt, dz, *xin, s])


def wgrad(a, b, buf, tn, nj, b0, o0, om, name):
    t, k = a.shape
    tt = ROW_TILE
    while tt * 2 * k <= WGRAD_TILE_ELEMS and tt * 2 <= t:
        tt *= 2
    nsteps = t // tt
    last = nsteps - 1

    def body(a_ref, b_ref, buf_ref, o_ref, acc):
        s, j = pl.program_id(0), pl.program_id(1)

        @pl.when(s == 0)
        def _():
            acc[j] = jnp.zeros(acc.shape[1:], F32)
        acc[j] += _dot_tn(a_ref[...], b_ref[...])

        @pl.when(s == last)
        def _():
            o_ref[...] = acc[j].astype(BF16)

    return pl.pallas_call(
        body, name=name, grid=(nsteps, nj),
        out_shape=jax.ShapeDtypeStruct(buf.shape, buf.dtype),
        in_specs=[pl.BlockSpec((tt, k), lambda s, j: (s, 0)),
                  pl.BlockSpec((tt, tn), lambda s, j: (s, b0 + j)),
                  pl.BlockSpec(memory_space=pl.ANY)],
        out_specs=pl.BlockSpec((k, tn), lambda s, j: (0, o0 + om * jnp.where(s == last, j, 0))),
        scratch_shapes=[pltpu.VMEM((nj, k, tn), F32)],
        input_output_aliases={2: 0},
        compiler_params=_cp("arbitrary", "arbitrary"),
    )(a, b, buf)


def dmerge(do, wot, ya, yb, w_a, w_b, wat, wbt, proj, name):
    t, dm = do.shape
    tm = _row_tile(t, WIDE_ROW_TILE)
    nsteps = t // tm
    gcol = GATE_COL // dm
    ch = COL_CHUNK

    def body(do_ref, wot_ref, ya_ref, yb_ref, wa_ref, wb_ref, wat_ref, wbt_ref, g_ref,
             dya_ref, dyb_ref, dg_ref, gwa_ref, gwb_ref, dm_scr, acc_a, acc_b):
        i, j = pl.program_id(0), pl.program_id(1)

        @pl.when((i == 0) & (j == 0))
        def _():
            acc_a[...] = jnp.zeros_like(acc_a)
            acc_b[...] = jnp.zeros_like(acc_b)

        @pl.when(j == 0)
        def _():
            do_v = do_ref[...]
            for c0 in range(0, dm, ch):
                dm_scr[:, c0:c0 + ch] = _dot(do_v, wot_ref[:, c0:c0 + ch])

        def branch(y_ref, w_ref, wt_ref, dy_ref, acc, gw_ref):
            y = y_ref[...]
            dy = None
            for c0 in range(0, dm, ch):
                cols = slice(c0, c0 + ch)
                p = _dot(y, w_ref[:, cols])
                sg = _sigmoid(g_ref[:, cols].astype(F32))
                dmg = dm_scr[:, cols]
                dp = (dmg * sg).astype(BF16)
                dg_ref[:, cols] = (dmg * p * (sg * (1.0 - sg))).astype(BF16)
                acc[:, cols] += _dot_tn(y, dp)
                part = _dot(dp, wt_ref[cols, :])
                dy = part if dy is None else dy + part
            dy_ref[...] = dy.astype(dy_ref.dtype)

            @pl.when(i == nsteps - 1)
            def _():
                gw_ref[...] = acc[...].astype(BF16)

        pl.when(j == 0)(lambda: branch(ya_ref, wa_ref, wat_ref, dya_ref, acc_a, gwa_ref))
        pl.when(j == 1)(lambda: branch(yb_ref, wb_ref, wbt_ref, dyb_ref, acc_b, gwb_ref))

    full = lambda arr: pl.BlockSpec(arr.shape, lambda i, j: (0, 0))
    rowc = lambda w: pl.BlockSpec((tm, w), lambda i, j: (i, 0))
    return pl.pallas_call(
        body, name=name, grid=(nsteps, 2),
        out_shape=[jax.ShapeDtypeStruct((t, ya.shape[1]), BF16), jax.ShapeDtypeStruct((t, yb.shape[1]), F32),
                   jax.ShapeDtypeStruct((t, 2 * dm), BF16),
                   jax.ShapeDtypeStruct(w_a.shape, BF16), jax.ShapeDtypeStruct(w_b.shape, BF16)],
        in_specs=[rowc(dm), full(wot), rowc(ya.shape[1]), rowc(yb.shape[1]), full(w_a), full(w_b),
                  full(wat), full(wbt), pl.BlockSpec((tm, dm), lambda i, j: (i, gcol + j))],
        out_specs=[rowc(ya.shape[1]), rowc(yb.shape[1]), pl.BlockSpec((tm, dm), lambda i, j: (i, j)),
                   full(w_a), full(w_b)],
        scratch_shapes=[pltpu.VMEM((tm, dm), F32), pltpu.VMEM(w_a.shape, F32), pltpu.VMEM(w_b.shape, F32)],
        compiler_params=_cp("arbitrary", "arbitrary"),
    )(do, wot, ya, yb, w_a, w_b, wat, wbt, proj)


def mix_bwd(dyb, o_g, l_g, name):
    t, w = dyb.shape
    tm = _row_tile(t, WIDE_ROW_TILE)
    nh = w // HEAD_DIM
    dils = [o.shape[0] for o in o_g]

    def body(dyb_ref, o0, o1, o2, l0, l1, l2, do0, do1, do2, dl0, dl1, dl2, *scr):
        ls = [_unfold_from(l, scr, d) for l, d in zip((l0, l1, l2), dils)]
        m = jnp.maximum(jnp.maximum(ls[0], ls[1]), ls[2])
        es = [jnp.exp(l - m) for l in ls]
        inv = 1.0 / (es[0] + es[1] + es[2])
        wts = [e * inv for e in es]
        dyb_v = dyb_ref[...]
        dws = []
        for o_ref, do_ref, wt, d in zip((o0, o1, o2), (do0, do1, do2), wts, dils):
            prod = dyb_v * _unfold_from(o_ref, scr, d)
            _fold_to(do_ref, dyb_v * wt, scr, d)
            for h in range(nh):
                hs = slice(h * HEAD_DIM, (h + 1) * HEAD_DIM)
                dws.append(jnp.broadcast_to(jnp.sum(prod[:, hs], axis=1, keepdims=True), (tm, HEAD_DIM)))
        for g, (dl_ref, d) in enumerate(zip((dl0, dl1, dl2), dils)):
            cols = []
            for h in range(nh):
                hs = slice(h * HEAD_DIM, (h + 1) * HEAD_DIM)
                mean = sum(wts[g2][:, hs] * dws[g2 * nh + h] for g2 in range(N_GROUPS))
                cols.append(wts[g][:, hs] * (dws[g * nh + h] - mean))
            _fold_to(dl_ref, jnp.concatenate(cols, axis=1), scr, d)

    folded = [_folded_spec(d, tm, w) for d in dils]
    return pl.pallas_call(
        body, name=name, grid=(t // tm,),
        out_shape=[jax.ShapeDtypeStruct(o.shape, BF16) for o in o_g]
                  + [jax.ShapeDtypeStruct(o.shape, F32) for o in o_g],
        in_specs=[pl.BlockSpec((tm, w), lambda i: (i, 0))] + folded + folded,
        out_specs=folded + folded,
        scratch_shapes=_fold_scratch(tm, w),
        compiler_params=_cp("parallel"),
    )(dyb, *o_g, *l_g)


def attn_bwd(qkv, o, lse, do, dlse, cfg, sinks, name):
    d, n, _ = qkv.shape
    tq, nsub, nqb = _attn_geometry(n, ATTN_BWD_BLOCK)
    wq, wk, wout = cfg.wq, cfg.wk, cfg.wout
    grp = cfg.heads // cfg.kv_heads
    has_dl = dlse is not None

    def body(*refs):
        sink_ref, q_ref, qn_ref, kc_ref, kp_ref, vc_ref, vp_ref = refs[:7]
        o_ref, on_ref, do_ref, don_ref, l_ref, ln_ref = refs[7:13]
        rest = refs[13:]
        dl_ref = dln_ref = None
        if has_dl:
            dl_ref, dln_ref = rest[:2]
            rest = rest[2:]
        out_ref = rest[0]
        rest = rest[1:]
        if cfg.sinks:
            dsink_ref = rest[0]
            rest = rest[1:]
        kf, vf = rest
        r, i = pl.program_id(0), pl.program_id(1)
        _fill_pairs(kf, kp_ref, kc_ref, grp)
        _fill_pairs(vf, vp_ref, vc_ref, grp)
        dist, valid, valid_first = _band(i, cfg.max_dist)
        distf = dist.astype(F32)
        next_dist = jnp.where(i < nqb - 1, cfg.max_dist, -1)
        valid_next = (dist[:, 0:BLOCK] >= 0) & (dist[:, 0:BLOCK] <= next_dist)
        half_q = _lane_halves(BLOCK)
        if cfg.sinks:
            @pl.when((r == 0) & (i == 0))
            def _():
                dsink_ref[...] = jnp.zeros_like(dsink_ref)

        shared = {}
        for p in range(cfg.heads // 2):
            lanes = slice(p * LANES, (p + 1) * LANES)
            ki = _pair_source(p, grp)
            hs = (2 * p, 2 * p + 1)
            biases = [[_masked_bias(m, dd, cfg, h, d) for h in hs]
                      for m, dd in ((valid_first, distf), (valid, distf), (valid_next, distf[:, 0:BLOCK]))]
            tiles = []
            for a in range(nsub + 1):
                if a < nsub:
                    rows, win = slice(a * BLOCK, (a + 1) * BLOCK), slice(a * BLOCK, (a + 2) * BLOCK)
                    src = (q_ref, o_ref, do_ref, l_ref, dl_ref)
                else:
                    rows, win = slice(0, BLOCK), slice(nsub * BLOCK, (nsub + 1) * BLOCK)
                    src = (qn_ref, on_ref, don_ref, ln_ref, dln_ref)
                q2 = src[0][rows, lanes] * SCALE
                do2 = src[2][rows, lanes]
                o2 = src[1][rows, lanes].astype(F32)
                l2 = src[3][rows, lanes]
                k2, v2 = kf[ki, win, :], vf[ki, win, :]
                per = []
                for e in range(2):
                    qe, doe = jnp.where(half_q[e], q2, 0), jnp.where(half_q[e], do2, 0)
                    delta = jnp.sum(doe.astype(F32) * o2, axis=1, keepdims=True)
                    lse_v = jnp.max(jnp.where(half_q[e], l2, NEG_INF), axis=1, keepdims=True)
                    shift = -delta
                    if has_dl:
                        shift = shift + jnp.max(jnp.where(half_q[e], src[4][rows, lanes], NEG_INF), axis=1,
                                                keepdims=True)
                    s = _dot_nt(qe, k2) + biases[0 if a == 0 else (1 if a < nsub else 2)][e]
                    per.append((qe, doe, delta, lse_v, shift, s, _dot_nt(doe, v2)))
                tiles.append((k2, per))
            grads = []
            for k2, per in tiles:
                both = []
                for qe, doe, delta, lse_v, shift, s, dp in per:
                    pr = jnp.exp(s - lse_v)
                    both.append(((pr * (dp + shift)).astype(BF16), pr.astype(BF16)))
                grads.append(both)
            dkt, dvt = [], []
            for a, ((k2, per), both) in enumerate(zip(tiles, grads)):
                if a < nsub:
                    half_k = _lane_halves(k2.shape[0])
                    ds_cat = jnp.concatenate([both[0][0], both[1][0]], axis=1)
                    k_cat = jnp.concatenate([jnp.where(half_k[e], k2, 0) for e in range(2)], axis=0)
                    out_ref[a * BLOCK:(a + 1) * BLOCK, lanes] = (_dot(ds_cat, k_cat) * SCALE).astype(BF16)
                cut = (lambda x: x[:, BLOCK:]) if a == 0 else (lambda x: x)
                q_cat = jnp.concatenate([per[0][0], per[1][0]], axis=0)
                do_cat = jnp.concatenate([per[0][1], per[1][1]], axis=0)
                dkt.append(_dot_tn(q_cat, jnp.concatenate([cut(both[0][0]), cut(both[1][0])], axis=0)))
                dvt.append(_dot_tn(do_cat, jnp.concatenate([cut(both[0][1]), cut(both[1][1])], axis=0)))
                if cfg.sinks and a < nsub:
                    for e in range(2):
                        psink = jnp.exp(sink_ref[hs[e]] - per[e][3])
                        tot = jnp.sum(psink * (-per[e][2]), axis=0, keepdims=True)
                        dsink_ref[hs[e]:hs[e] + 1, :] += jnp.broadcast_to(tot, (1, LANES))
            for m in range(nsub):
                rows = slice(m * BLOCK, (m + 1) * BLOCK)
                for which, (acc, col0) in enumerate(((dkt, wq), (dvt, wq + wk))):
                    own = acc[m] if m == 0 else acc[m][:, BLOCK:]
                    total = own + acc[m + 1][:, 0:BLOCK]
                    if grp == 1:
                        out_ref[rows, col0 + p * LANES:col0 + (p + 1) * LANES] = total.T.astype(BF16)
                    else:
                        t64 = total[0:HEAD_DIM] + total[HEAD_DIM:]
                        key = (ki, which, m)
                        shared[key] = t64 + shared[key] if key in shared else t64
        for (ki, which, m), t64 in shared.items():
            col0 = (wq, wq + wk)[which] + ki * HEAD_DIM
            out_ref[m * BLOCK:(m + 1) * BLOCK, col0:col0 + HEAD_DIM] = t64.T.astype(BF16)

    prev = lambda i: jnp.maximum(i * nsub - 1, 0)
    nxt = lambda i: jnp.minimum((i + 1) * nsub, n // BLOCK - 1)
    cur = lambda w, c: pl.BlockSpec((None, tq, w), lambda r, i: (r, i, c // w))
    prv = lambda w, c: pl.BlockSpec((None, BLOCK, w), lambda r, i: (r, prev(i), c // w))
    o_cur = pl.BlockSpec((None, tq, wq), lambda r, i: (r, i, 0))
    o_nxt = pl.BlockSpec((None, BLOCK, wq), lambda r, i: (r, nxt(i), 0))
    in_specs = [pl.BlockSpec(memory_space=pltpu.SMEM),
                cur(wq, cfg.qc), pl.BlockSpec((None, BLOCK, wq), lambda r, i: (r, nxt(i), cfg.qc // wq)),
                cur(wk, cfg.kc), prv(wk, cfg.kc), cur(wk, cfg.vc), prv(wk, cfg.vc),
                o_cur, o_nxt, o_cur, o_nxt, o_cur, o_nxt]
    args = [sinks, qkv, qkv, qkv, qkv, qkv, qkv, o, o, do, do, lse, lse]
    if has_dl:
        in_specs += [o_cur, o_nxt]
        args += [dlse, dlse]
    out_shape = [jax.ShapeDtypeStruct((d, n, wout), BF16)]
    out_specs = [pl.BlockSpec((None, tq, wout), lambda r, i: (r, i, 0))]
    if cfg.sinks:
        out_shape.append(jax.ShapeDtypeStruct((8, LANES), F32))
        out_specs.append(pl.BlockSpec((8, LANES), lambda r, i: (0, 0)))
    pair_scratch = pltpu.VMEM((_n_pair_sources(cfg), tq + BLOCK, LANES), BF16)
    return pl.pallas_call(
        body, name=name, grid=(d, nqb), out_shape=out_shape, in_specs=in_specs, out_specs=out_specs,
        scratch_shapes=[pair_scratch, pair_scratch],
        compiler_params=_cp("arbitrary", "arbitrary"),
    )(*args)


def _adamw(g, w, m, v):
    m = ADAM_B1 * m + (1.0 - ADAM_B1) * g
    v = ADAM_B2 * v + (1.0 - ADAM_B2) * (g * g)
    m_hat = m / (1.0 - ADAM_B1 ** ADAM_STEP)
    v_hat = v / (1.0 - ADAM_B2 ** ADAM_STEP)
    delta = -ADAM_LR * (m_hat / (jnp.sqrt(v_hat) + ADAM_EPS) + ADAM_WD * w)
    return delta, m, v


def _adam_rows(r):
    if r <= ADAM_ROW_TILE:
        return r
    return next(rows for rows in range(ADAM_ROW_TILE, 0, -8) if r % rows == 0)


def adam_reduce(parts, w, m, v, name):
    r, c = w.shape
    tr = _adam_rows(r)

    def body(p_ref, w_ref, m_ref, v_ref, g_ref, d_ref, mo_ref, vo_ref):
        g = p_ref[0].astype(F32)
        for j in range(1, N_DEV):
            g = g + p_ref[j].astype(F32)
        g_ref[...] = g
        d_ref[...], mo_ref[...], vo_ref[...] = _adamw(g, w_ref[...], m_ref[...], v_ref[...])

    row = pl.BlockSpec((tr, c), lambda i: (i, 0))
    return pl.pallas_call(
        body, name=name, grid=(r // tr,),
        out_shape=[jax.ShapeDtypeStruct((r, c), F32)] * 4,
        in_specs=[pl.BlockSpec((N_DEV, tr, c), lambda i: (0, i, 0)), row, row, row],
        out_specs=[row] * 4,
        compiler_params=_cp("parallel"),
    )(parts, w, m, v)


def adam_layers(parts, w, m, v, name):
    nl, r, c = w.shape
    tr = _adam_rows(r)
    steps = r // tr

    def body(*refs):
        p_refs = refs[:nl]
        w_ref, m_ref, v_ref, g_ref, d_ref, mo_ref, vo_ref = refs[nl:]
        for k in range(nl):
            @pl.when(pl.program_id(0) == k)
            def _():
                g = p_refs[k][0].astype(F32)
                for j in range(1, N_DEV):
                    g = g + p_refs[k][j].astype(F32)
                g_ref[...] = g
                d_ref[...], mo_ref[...], vo_ref[...] = _adamw(g, w_ref[...], m_ref[...], v_ref[...])

    def part_spec(k):
        return pl.BlockSpec((N_DEV, tr, c), lambda l, i: (0, jnp.clip(i + (l - k) * steps, 0, steps - 1), 0))

    blk = pl.BlockSpec((None, tr, c), lambda l, i: (l, i, 0))
    return pl.pallas_call(
        body, name=name, grid=(nl, steps),
        out_shape=[jax.ShapeDtypeStruct((nl, r, c), F32)] * 4,
        in_specs=[part_spec(k) for k in range(nl)] + [blk, blk, blk],
        out_specs=[blk] * 4,
        compiler_params=_cp("arbitrary", "arbitrary"),
    )(*parts, w, m, v)


def adam_w_ada(sct, dm_loc, w, m, v):
    nl, dm, wc = w.shape
    tr = _row_tile(dm)

    def body(s_ref, d_ref, w_ref, m_ref, v_ref, g_ref, dl_ref, mo_ref, vo_ref):
        g = jnp.dot(s_ref[...], d_ref[...], preferred_element_type=F32, precision=lax.Precision.HIGHEST)
        g_ref[...] = g
        dl_ref[...], mo_ref[...], vo_ref[...] = _adamw(g, w_ref[...], m_ref[...], v_ref[...])

    blk = pl.BlockSpec((None, tr, wc), lambda l, i: (l, i, 0))
    return pl.pallas_call(
        body, name="adam_w_ada", grid=(nl, dm // tr),
        out_shape=[jax.ShapeDtypeStruct(w.shape, F32)] * 4,
        in_specs=[pl.BlockSpec((tr, LANES), lambda l, i: (i, 0)),
                  pl.BlockSpec((None, LANES, wc), lambda l, i: (l, 0, 0)), blk, blk, blk],
        out_specs=[blk] * 4,
        compiler_params=_cp("parallel", "parallel"),
    )(sct, dm_loc, w, m, v)


TRANSPOSED = ("w_gate", "w_up")


def _pieces(dm):
    ncol = lambda n: n // N_DEV
    mixer = ([Piece("w_in", "w_in", 1, 0, ncol(GATE_COL + 2 * dm)),
              Piece("w_a", "w_a", 1, 0, ncol(dm)),
              Piece("w_b", "w_b", 1, 0, ncol(dm)),
              Piece("w_o", "w_o", 0, 0, ncol(dm))],
             {"w_in": (dm, GATE_COL + 2 * dm), "w_a": (A_Q_HEADS * HEAD_DIM, dm), "w_b": (B_OUT_W, dm),
              "w_o": (dm, dm)})
    ffn = ([Piece("w_gate", "w_ffn_t", 0, 0, ncol(D_FF)),
            Piece("w_up", "w_ffn_t", 0, D_FF, ncol(D_FF)),
            Piece("w_down", "w_down", 0, 0, ncol(D_FF))],
           {"w_ffn_t": (2 * D_FF, dm), "w_down": (D_FF, dm)})
    return mixer, ffn


def kernel(x, c, w_ada, b_ada, w_in, sinks, w_a, w_b, w_o, ln1_g, ln1_b, w_gate, w_up, w_down, ln2_g, ln2_b, loss_target, m_w_ada, m_b_ada, m_w_in, m_sinks, m_w_a, m_w_b, m_w_o, m_ln1_g, m_ln1_b, m_w_gate, m_w_up, m_w_down, m_ln2_g, m_ln2_b, v_w_ada, v_b_ada, v_w_in, v_sinks, v_w_a, v_w_b, v_w_o, v_ln1_g, v_ln1_b, v_w_gate, v_w_up, v_w_down, v_ln2_g, v_ln2_b):
    given = dict(locals())
    nl = w_in.shape[0]
    t, dm = x.shape[1], x.shape[2]
    me = 4 * lax.axis_index("x") + 2 * lax.axis_index("y") + lax.axis_index("c")
    x0 = x.reshape(t, dm)
    target = loss_target.reshape(t, dm)

    groups = dict(zip(("mixer", "ffn"), _pieces(dm)))
    local = lambda nm, pre="": (given[pre + nm].transpose(0, 2, 1) if nm in TRANSPOSED else given[pre + nm])
    shards = {pc.name: local(pc.name).astype(BF16) for pcs, _ in groups.values() for pc in pcs}

    def gather(group, l):
        pcs, bufs = groups[group]
        return Exchange("gather", pcs, [shards[pc.name][l] for pc in pcs], bufs.values(), bufs)

    def scatter(group, gbuf):
        pcs, bufs = groups[group]
        return Exchange("scatter", pcs, [gbuf[nm] for nm in bufs],
                        [(N_DEV,) + shards[pc.name].shape[1:] for pc in pcs], bufs)

    mixer_pcs, mixer_bufs = groups["mixer"]
    groups["first"] = (mixer_pcs[:1], {"w_in": mixer_bufs["w_in"]})
    groups["rest"] = (mixer_pcs[1:] + groups["ffn"][0],
                      {**{k: v for k, v in mixer_bufs.items() if k != "w_in"}, **groups["ffn"][1]})
    full = [dict() for _ in range(nl)]
    full[0].update(zip(groups["first"][1], run_exchange(gather("first", 0), "gather_w_in")))

    wc = w_ada.shape[2]
    c_all = all_gather_small(jnp.broadcast_to(c, (8, dm)), "gather_c")[:, 0, :]
    b_loc = lax.dynamic_slice_in_dim(b_ada, me * wc, wc, axis=1).reshape(nl, 1, wc)
    mp, sc_all = mod_partial(c_all, w_ada, b_loc)
    mp_all = all_gather_small(mp.reshape(nl * N_DEV, wc), "gather_mod").reshape(N_DEV, nl, N_DEV, wc)
    mod = lax.dynamic_index_in_dim(mp_all, me, axis=2, keepdims=False)
    mod = mod.transpose(1, 0, 2).reshape(nl, 6, 1, dm)

    vec = lambda a, l: a[l].reshape(1, dm)

    saved = []
    xl = (x0, jnp.ones((1, dm), F32), jnp.zeros((1, dm), F32))
    for l in range(nl):
        sh1, s1, g1, sh2, s2, g2 = [mod[l, j] for j in range(6)]
        w = full[l]
        hosted = "rest" if l == 0 else "ffn"
        (u1, u1_f4, u1_f16, proj, qkv_f4, qkv_f16), got = in_proj(xl, s1, sh1, w["w_in"], "in_proj",
                                                                   gather(hosted, l))
        w.update(zip(groups[hosted][1], got))
        proj3 = proj.reshape(1, t, proj.shape[1])
        qkv_b = [proj3, qkv_f4, qkv_f16]
        ya, lse_a = attn_fwd(proj3, ATTN_A, sinks[l], "attn_a_fwd")
        o_g, l_g = [], []
        for g, cfg in enumerate(ATTN_B):
            o, ls = attn_fwd(qkv_b[g], cfg, sinks[l], "attn_b%d_fwd" % g)
            o_g.append(o)
            l_g.append(ls)
        yb, merged = mix_merge(ya[0], o_g, l_g, proj, w["w_a"], w["w_b"], "mix_merge")
        y1, zh1, rs1 = proj_ln(merged, w["w_o"], xl, g1, "out_proj_ln")
        x1 = (zh1, vec(ln1_g, l), vec(ln1_b, l))
        (u2, ab), got = modmm(x1, s2, sh2, w["w_ffn_t"].T, "ffn_up", gather("mixer", l + 1) if l + 1 < nl else None)
        if l + 1 < nl:
            full[l + 1].update(zip(groups["mixer"][1], got))
        h, y2, zh2, rs2 = swiglu_proj_ln(ab, w["w_down"], x1, g2, "ffn_down_ln")
        x2 = (zh2, vec(ln2_g, l), vec(ln2_b, l))
        saved.append(dict(xin=xl, u1=[u1, u1_f4.reshape(t, dm), u1_f16.reshape(t, dm)], proj=proj, qkv_b=qkv_b,
                          ya=ya, lse_a=lse_a, o_g=o_g, l_g=l_g, yb=yb, merged=merged,
                          y1=y1, x1=x1, zh1=zh1, rs1=rs1, u2=u2, ab=ab, h=h, y2=y2, zh2=zh2, rs2=rs2))
        xl = x2

    dx, loss_part = loss_head(xl, target)

    small = {k: [None] * nl for k in ("dmod", "ln1_g", "ln1_b", "ln2_g", "ln2_b", "sinks")}
    recv = {nm: [None] * nl for grp in groups.values() for nm in (pc.name for pc in grp[0])}

    def keep(group, l, got):
        for pc, arr in zip(groups[group][0], got):
            recv[pc.name][l] = arr

    for l in reversed(range(nl)):
        sv, w = saved[l], full[l]
        sh1, s1, g1, sh2, s2, g2 = [mod[l, j] for j in range(6)]
        fresh = lambda nm: lax.empty({**groups["mixer"][1], **groups["ffn"][1]}[nm], BF16)
        gbuf = {}
        dz2, dy2, sg, sb, sgate2, gbuf["w_down"] = ln_bwd(dx, sv["zh2"], sv["rs2"], sv["y2"], vec(ln2_g, l), g2,
                                                          sv["h"], "ln_bwd_ffn")
        small["ln2_g"][l], small["ln2_b"][l] = sg[0], sb[0]
        dab = dswiglu(dy2, w["w_down"].T, sv["ab"], "dswiglu")
        gbuf["w_ffn_t"] = wgrad(dab, sv["u2"], fresh("w_ffn_t"), WGRAD_COLS, dm // WGRAD_COLS, 0, 0, 1,
                                "wgrad_ffn_up")
        (dx1, ss2, ssh2), got = dgrad_ffn(dab, w["w_ffn_t"], dz2, sv["x1"], s2, "dgrad_ffn", scatter("ffn", gbuf))
        keep("ffn", l, got)
        dz1, do1, sg, sb, sgate1, gbuf["w_o"] = ln_bwd(dx1, sv["zh1"], sv["rs1"], sv["y1"], vec(ln1_g, l), g1,
                                                       sv["merged"], "ln_bwd_mixer")
        small["ln1_g"][l], small["ln1_b"][l] = sg[0], sb[0]
        dya, dyb, dgab, gbuf["w_a"], gbuf["w_b"] = dmerge(
            do1, w["w_o"].T, sv["ya"][0], sv["yb"], w["w_a"], w["w_b"], w["w_a"].T, w["w_b"].T, sv["proj"], "dmerge")
        mixed = mix_bwd(dyb, sv["o_g"], sv["l_g"], "mix_bwd")
        do_g, dl_g = mixed[:N_GROUPS], mixed[N_GROUPS:]
        d_a, dsink = attn_bwd(sv["qkv_b"][0], sv["ya"], sv["lse_a"], dya.reshape(1, t, -1), None, ATTN_A,
                              sinks[l], "attn_a_bwd")
        small["sinks"][l] = dsink[:, 0]
        d_b = [attn_bwd(sv["qkv_b"][g], sv["o_g"][g], sv["l_g"][g], do_g[g], dl_g[g], cfg, sinks[l],
                        "attn_b%d_bwd" % g)[0] for g, cfg in enumerate(ATTN_B)]
        gw = wgrad(sv["u1"][0], d_a.reshape(t, A_W), fresh("w_in"), A_W, 1, 0, 0, 1, "wgrad_in_a")
        for g in range(N_GROUPS):
            gw = wgrad(sv["u1"][g], d_b[g].reshape(t, B_GW), gw, B_OUT_W, 3, 0, A_W // B_OUT_W + g, N_GROUPS,
                       "wgrad_in_b%d" % g)
        gbuf["w_in"] = wgrad(sv["u1"][0], dgab, gw, WGRAD_COLS, 2 * dm // WGRAD_COLS, 0, GATE_COL // WGRAD_COLS, 1,
                             "wgrad_in_gate")
        (dx, ss1, ssh1), got = dgrad_in(d_a, d_b, dgab, w["w_in"].T, dz1, sv["xin"], s1, "dgrad_in",
                                        scatter("mixer", gbuf))
        keep("mixer", l, got)
        small["dmod"][l] = jnp.stack([ssh1[0], ss1[0], sgate1[0], ssh2[0], ss2[0], sgate2[0]])
    grad_x = dx.reshape(x.shape)

    big_out = {}
    for nm, parts in recv.items():
        outs = adam_layers(parts, local(nm), local(nm, "m_"), local(nm, "v_"), "adam_" + nm)
        big_out[nm] = [o.transpose(0, 2, 1) for o in outs] if nm in TRANSPOSED else outs

    rows = jnp.concatenate(
        [jnp.stack(small["dmod"]).reshape(nl * 6, dm)]
        + [jnp.stack(small[k]) for k in ("ln1_g", "ln1_b", "ln2_g", "ln2_b")]
        + [jnp.pad(jnp.stack(small["sinks"]).reshape(1, -1), ((0, 0), (0, dm - nl * A_Q_HEADS))),
           jnp.broadcast_to(loss_part[0:1, 0:1], (1, dm))])
    n_rows = rows.shape[0]
    rows = jnp.pad(rows, ((0, -n_rows % 8), (0, 0)))
    rows_all = all_gather_small(rows, "gather_small_grads")

    def pack_small(pre):
        parts = [given[pre + "b_ada"].reshape(nl * 6, dm)]
        parts += [given[pre + k] for k in ("ln1_g", "ln1_b", "ln2_g", "ln2_b")]
        parts.append(jnp.pad(given[pre + "sinks"].reshape(1, -1), ((0, 0), (0, dm - nl * A_Q_HEADS))))
        p = jnp.concatenate(parts)
        return jnp.pad(p, ((0, rows.shape[0] - p.shape[0]), (0, 0)))

    souts = adam_reduce(rows_all, pack_small(""), pack_small("m_"), pack_small("v_"), "adam_small")

    def unpack_small(o):
        r = {"b_ada": o[0:nl * 6].reshape(nl, 6 * dm)}
        for j, k in enumerate(("ln1_g", "ln1_b", "ln2_g", "ln2_b")):
            r[k] = o[nl * 6 + j * nl: nl * 6 + (j + 1) * nl]
        r["sinks"] = o[nl * 10, 0:nl * A_Q_HEADS].reshape(nl, A_Q_HEADS)
        return r

    small_out = [unpack_small(o) for o in souts]
    loss = souts[0][nl * 10 + 1, 0]

    dmod_all = rows_all[:, 0:nl * 6].reshape(N_DEV, nl, 6 * dm)
    dm_loc = lax.dynamic_slice_in_dim(dmod_all, me * wc, wc, axis=2).transpose(1, 0, 2)
    dm_loc = jnp.pad(dm_loc, ((0, 0), (0, LANES - N_DEV), (0, 0)))
    sct = jnp.pad(sc_all.T, ((0, 0), (0, LANES - N_DEV)))
    ada_out = adam_w_ada(sct, dm_loc, w_ada, m_w_ada, v_w_ada)

    names = ["w_ada", "b_ada", "w_in", "sinks", "w_a", "w_b", "w_o", "ln1_g", "ln1_b",
             "w_gate", "w_up", "w_down", "ln2_g", "ln2_b"]

    def pick(kind, nm):
        if nm == "w_ada":
            return ada_out[kind]
        if nm in small_out[kind]:
            return small_out[kind][nm]
        return big_out[nm][kind]

    result = [loss, grad_x]
    for kind in range(4):
        result += [pick(kind, nm) for nm in names]
    return tuple(result)
```

```python
import jax
import jax.numpy as jnp
from jax import lax
from jax.experimental import pallas as pl
from jax.experimental.pallas import tpu as pltpu

F32 = jnp.float32
BF16 = jnp.bfloat16

D_MODEL = 1024
HEAD_DIM = 64
A_Q_HEADS = 8
A_KV_HEADS = 2
A_WINDOW = 128
B_GROUPS = ((128, 1), (512, 4), (2048, 16))
N_GROUPS = len(B_GROUPS)
B_HEADS_PER_GROUP = 4
N_ATTN_HEADS = A_Q_HEADS + B_HEADS_PER_GROUP * N_GROUPS
BLOCK = 128
ATTN_FWD_BLOCK = 1024
ATTN_BWD_BLOCK = 2048
A_W =(A_Q_HEADS + 2 * A_KV_HEADS) * HEAD_DIM
B_OUT_W = B_HEADS_PER_GROUP * HEAD_DIM
B_GW = 3 * B_OUT_W
B_ALL = N_GROUPS * B_OUT_W
GATE_COL = A_W + 3 * B_ALL
D_FF = 2816
COL_CHUNK = 256
FF_CHUNK = COL_CHUNK
DGRAD_CHUNK = COL_CHUNK
MM_CHUNK = 512
WGRAD_COLS = 512
ADAM_ROW_TILE = 256
DN_ALPHA = 8.0 ** 0.25
LN_EPS = 1e-5
NEG_INF = -1e30
ADAM_LR, ADAM_B1, ADAM_B2, ADAM_EPS, ADAM_WD, ADAM_STEP = 0.001, 0.9, 0.999, 1e-08, 0.01, 10

N_DEV = 8
MESH = pl.DeviceIdType.MESH
VMEM_LIMIT = 56 * 1024 * 1024
ROW_TILE = 512
WIDE_ROW_TILE = 1024
WGRAD_TILE_ELEMS = 2 * 1024 * 1024
LANES = 128
BF16_ROWS = 16


def _cp(*sem):
    return pltpu.CompilerParams(dimension_semantics=sem, vmem_limit_bytes=VMEM_LIMIT)


def _row_tile(t, rows=ROW_TILE):
    return min(rows, t)


def _slope(head):
    return 2.0 ** (-8.0 * (head + 1) / N_ATTN_HEADS)


def _sigmoid(x):
    return 1.0 / (1.0 + jnp.exp(-x))


def _dot(a, b):
    return jnp.dot(a, b, preferred_element_type=F32)


def _dot_nt(a, b):
    return lax.dot_general(a, b, (((1,), (1,)), ((), ())), preferred_element_type=F32)


def _dot_tn(a, b):
    return lax.dot_general(a, b, (((0,), (0,)), ((), ())), preferred_element_type=F32)


def _fold_scratch(tm, w):
    return [pltpu.VMEM((tm, LANES), F32)] * (w // LANES)


def _fold_to(dst_ref, val, scrs, d, col0=0):
    tm, w = val.shape
    if d == 1:
        dst_ref[0, :, col0:col0 + w] = val.astype(dst_ref.dtype)
        return
    for cb in range(w // LANES):
        scrs[cb][...] = val[:, cb * LANES:(cb + 1) * LANES]
    for r in range(d):
        for cb in range(w // LANES):
            piece = scrs[cb][pl.ds(r, tm // d, stride=d), :]
            dst_ref[r, :, col0 + cb * LANES:col0 + (cb + 1) * LANES] = piece.astype(dst_ref.dtype)


def _unfold_rows(rows_of, scrs, d, n, w):
    for r in range(d):
        for cb in range(w // LANES):
            scrs[cb][pl.ds(r, n, stride=d), :] = rows_of(r, slice(cb * LANES, (cb + 1) * LANES)).astype(F32)
    return jnp.concatenate([scrs[cb][0:d * n, :] for cb in range(w // LANES)], axis=1)


def _unfold_from(src_ref, scrs, d):
    if d == 1:
        return src_ref[0].astype(F32)
    _, n, w = src_ref.shape
    return _unfold_rows(lambda r, cols: src_ref[r, :, cols], scrs, d, n, w)


def _folded_spec(d, tm, w):
    return pl.BlockSpec((d, tm // d, w), lambda i: (0, i, 0))


def _me():
    return lax.axis_index("x"), lax.axis_index("y"), lax.axis_index("c")


def _flip(v, bit):
    return 1 - v if bit else v


def _peer(k):
    x, y, c = _me()
    return (_flip(x, k & 4), _flip(y, k & 2), _flip(c, k & 1))


def _peer_index(k):
    px, py, pc = _peer(k)
    return 4 * px + 2 * py + pc


def all_gather_small(v, name):
    r, c = v.shape

    def body(v_ref, out_ref, send_sems, recv_sems):
        me = _peer_index(0)
        out_ref[me] = v_ref[...]
        copies = []
        for k in range(1, N_DEV):
            cp = pltpu.make_async_remote_copy(
                src_ref=v_ref, dst_ref=out_ref.at[me],
                send_sem=send_sems.at[k - 1], recv_sem=recv_sems.at[k - 1],
                device_id=_peer(k), device_id_type=MESH)
            cp.start()
            copies.append(cp)
        for k in range(1, N_DEV):
            pltpu.make_async_remote_copy(
                src_ref=v_ref, dst_ref=out_ref.at[_peer_index(k)],
                send_sem=send_sems.at[k - 1], recv_sem=recv_sems.at[k - 1],
                device_id=_peer(k), device_id_type=MESH).wait_recv()
        for cp in copies:
            cp.wait_send()

    return pl.pallas_call(
        body, name=name,
        out_shape=jax.ShapeDtypeStruct((N_DEV, r, c), v.dtype),
        in_specs=[pl.BlockSpec(memory_space=pltpu.VMEM)],
        out_specs=pl.BlockSpec(memory_space=pltpu.VMEM),
        scratch_shapes=[pltpu.SemaphoreType.DMA((N_DEV - 1,)), pltpu.SemaphoreType.DMA((N_DEV - 1,))],
        compiler_params=pltpu.CompilerParams(vmem_limit_bytes=VMEM_LIMIT),
    )(v)


class Piece:
    def __init__(self, name, buf, axis, base, size):
        self.name, self.buf, self.axis, self.base, self.size = name, buf, axis, base, size

    def window(self, ref, j):
        start = self.base + j * self.size
        if self.axis == 1:
            return ref.at[:, pl.ds(pl.multiple_of(start, LANES), self.size)]
        return ref.at[pl.ds(pl.multiple_of(start, BF16_ROWS), self.size), :]


class Exchange:
    def __init__(self, kind, pieces, ins, out_shapes, bufs):
        self.kind, self.pieces, self.ins, self.out_shapes = kind, pieces, list(ins), list(out_shapes)
        self.buf_of = {nm: i for i, nm in enumerate(bufs)}
        self.n_in, self.n_out = len(self.ins), len(self.out_shapes)
        n = len(pieces)
        self.scratch = [pltpu.SemaphoreType.DMA((n, N_DEV - 1)), pltpu.SemaphoreType.DMA((n, N_DEV - 1)),
                        pltpu.SemaphoreType.DMA((n,))]
        self.in_specs = [pl.BlockSpec(memory_space=pl.ANY)] * self.n_in
        self.out_specs = [pl.BlockSpec(memory_space=pl.ANY)] * self.n_out
        self.out_shape = [jax.ShapeDtypeStruct(s, BF16) for s in self.out_shapes]

    def _ends(self, pi, ins, outs, to):
        pc = self.pieces[pi]
        if self.kind == "gather":
            return ins[pi], pc.window(outs[self.buf_of[pc.buf]], _peer_index(0))
        return pc.window(ins[self.buf_of[pc.buf]], to), outs[pi].at[_peer_index(0)]

    def _landing(self, pi, outs, frm):
        pc = self.pieces[pi]
        if self.kind == "gather":
            return pc.window(outs[self.buf_of[pc.buf]], frm)
        return outs[pi].at[frm]

    def _remote(self, pi, k, src, dst, sems):
        return pltpu.make_async_remote_copy(
            src_ref=src, dst_ref=dst, send_sem=sems[0].at[pi, k - 1], recv_sem=sems[1].at[pi, k - 1],
            device_id=_peer(k), device_id_type=MESH)

    def _local(self, pi, ins, outs, sems):
        return pltpu.make_async_copy(*self._ends(pi, ins, outs, _peer_index(0)), sems[2].at[pi])

    def start(self, ins, outs, sems):
        for pi in range(len(self.pieces)):
            self._local(pi, ins, outs, sems).start()
            for k in range(1, N_DEV):
                self._remote(pi, k, *self._ends(pi, ins, outs, _peer_index(k)), sems).start()

    def finish(self, ins, outs, sems):
        for pi in range(len(self.pieces)):
            src_like = self._ends(pi, ins, outs, _peer_index(0))[0]
            for k in range(1, N_DEV):
                self._remote(pi, k, src_like, self._landing(pi, outs, _peer_index(k)), sems).wait_recv()
        for pi in range(len(self.pieces)):
            for k in range(1, N_DEV):
                self._remote(pi, k, *self._ends(pi, ins, outs, _peer_index(k)), sems).wait_send()
            self._local(pi, ins, outs, sems).wait()


def _hosted(ex, refs, n_in, n_out, first, last):
    if ex is None:
        return refs
    ins, rest = refs[:n_in], refs[n_in:]
    ex_ins, rest = rest[:ex.n_in], rest[ex.n_in:]
    outs, rest = rest[:n_out], rest[n_out:]
    ex_outs, rest = rest[:ex.n_out], rest[ex.n_out:]
    scr, sems = rest[:len(rest) - 3], rest[len(rest) - 3:]
    pl.when(first)(lambda: ex.start(ex_ins, ex_outs, sems))
    pl.when(last)(lambda: ex.finish(ex_ins, ex_outs, sems))
    return tuple(ins) + tuple(outs) + tuple(scr)


def _host_call(body, ex, *, name, grid, out_shape, in_specs, out_specs, scratch_shapes=(), sem=None, args):
    n_out = len(out_shape)
    if ex is not None:
        out_shape = list(out_shape) + ex.out_shape
        in_specs = list(in_specs) + ex.in_specs
        out_specs = list(out_specs) + ex.out_specs
        scratch_shapes = list(scratch_shapes) + ex.scratch
        args = list(args) + ex.ins
    res = pl.pallas_call(body, name=name, grid=grid, out_shape=out_shape, in_specs=in_specs, out_specs=out_specs,
                         scratch_shapes=scratch_shapes, compiler_params=_cp(*sem))(*args)
    return res[:n_out], res[n_out:]


def run_exchange(ex, name):
    def body(*refs):
        ins, outs, sems = refs[:ex.n_in], refs[ex.n_in:ex.n_in + ex.n_out], refs[ex.n_in + ex.n_out:]
        ex.start(ins, outs, sems)
        ex.finish(ins, outs, sems)

    return pl.pallas_call(body, name=name, out_shape=ex.out_shape, in_specs=ex.in_specs, out_specs=ex.out_specs,
                          scratch_shapes=ex.scratch)(*ex.ins)


def mod_partial(c_all, w_ada, b_loc):
    nl, dm, wc = w_ada.shape

    def body(c_ref, w_ref, b_ref, o_ref, sc_ref):
        cc = c_ref[...]
        sc = cc * _sigmoid(cc)
        sc_ref[...] = sc
        o_ref[...] = jnp.dot(sc, w_ref[...], preferred_element_type=F32,
                             precision=lax.Precision.HIGHEST) + b_ref[...]

    return pl.pallas_call(
        body, name="mod_partial", grid=(nl,),
        out_shape=[jax.ShapeDtypeStruct((nl, N_DEV, wc), F32), jax.ShapeDtypeStruct((N_DEV, dm), F32)],
        in_specs=[pl.BlockSpec((N_DEV, dm), lambda l: (0, 0)),
                  pl.BlockSpec((None, dm, wc), lambda l: (l, 0, 0)),
                  pl.BlockSpec((None, 1, wc), lambda l: (l, 0, 0))],
        out_specs=[pl.BlockSpec((None, N_DEV, wc), lambda l: (l, 0, 0)),
                   pl.BlockSpec((N_DEV, dm), lambda l: (0, 0))],
        compiler_params=_cp("arbitrary"),
    )(c_all, w_ada, b_loc)


def _stream_specs(tm, dm):
    vec = pl.BlockSpec((1, dm), lambda i: (0, 0))
    return [pl.BlockSpec((tm, dm), lambda i: (i, 0)), vec, vec]


def _stream(zh_ref, lg_ref, lb_ref, rows=slice(None), cols=slice(None)):
    return zh_ref[rows, cols] * lg_ref[:, cols] + lb_ref[:, cols]


def in_proj(x, s, sh, w, name, ex=None):
    t, dm = x[0].shape
    n = w.shape[1]
    tm = _row_tile(t)
    nsteps = t // tm
    ch = B_OUT_W
    dils = [dil for _, dil in B_GROUPS if dil > 1]

    def body(*refs):
        i = pl.program_id(0)
        zh_ref, lg_ref, lb_ref, s_ref, sh_ref, w_ref, u_ref, *rest = _hosted(
            ex, refs, 6, 2 + 2 * len(dils), i == 0, i == nsteps - 1)
        uf_refs, o_ref, qf_refs = rest[:len(dils)], rest[len(dils)], rest[len(dils) + 1:len(dils) * 2 + 1]
        scrs = rest[len(dils) * 2 + 1:]
        uf = _stream(zh_ref, lg_ref, lb_ref) * (1.0 + s_ref[...]) + sh_ref[...]
        u = uf.astype(BF16)
        u_ref[...] = u
        for d, uf_ref in zip(dils, uf_refs):
            _fold_to(uf_ref, uf, scrs, d)
        for c0 in range(0, n, ch):
            res = _dot(u, w_ref[:, c0:c0 + ch])
            o_ref[:, c0:c0 + ch] = res.astype(BF16)
            if A_W <= c0 < GATE_COL:
                part, g = divmod((c0 - A_W) // ch, N_GROUPS)
                d = B_GROUPS[g][1]
                if d > 1:
                    _fold_to(qf_refs[dils.index(d)], res, scrs, d, part * ch)

    vec = pl.BlockSpec((1, dm), lambda i: (0, 0))
    row = lambda w_: pl.BlockSpec((tm, w_), lambda i: (i, 0))
    return _host_call(
        body, ex, name=name, grid=(nsteps,),
        out_shape=[jax.ShapeDtypeStruct((t, dm), BF16)]
                  + [jax.ShapeDtypeStruct((d, t // d, dm), BF16) for d in dils]
                  + [jax.ShapeDtypeStruct((t, n), BF16)]
                  + [jax.ShapeDtypeStruct((d, t // d, B_GW), BF16) for d in dils],
        in_specs=_stream_specs(tm, dm) + [vec, vec, pl.BlockSpec((dm, n), lambda i: (0, 0))],
        out_specs=[row(dm)] + [_folded_spec(d, tm, dm) for d in dils] + [row(n)]
                  + [_folded_spec(d, tm, B_GW) for d in dils],
        scratch_shapes=_fold_scratch(tm, dm), sem=("arbitrary",), args=[*x, s, sh, w])


def modmm(x, s, sh, w, name, ex=None):
    t, dm = x[0].shape
    n = w.shape[1]
    tm = _row_tile(t)
    nsteps = t // tm
    ch = MM_CHUNK

    def body(*refs):
        i = pl.program_id(0)
        zh_ref, lg_ref, lb_ref, s_ref, sh_ref, w_ref, u_ref, o_ref = _hosted(ex, refs, 6, 2, i == 0, i == nsteps - 1)
        u = (_stream(zh_ref, lg_ref, lb_ref) * (1.0 + s_ref[...]) + sh_ref[...]).astype(BF16)
        u_ref[...] = u
        for c0 in range(0, n, ch):
            o_ref[:, c0:c0 + ch] = _dot(u, w_ref[:, c0:c0 + ch]).astype(BF16)

    vec = pl.BlockSpec((1, dm), lambda i: (0, 0))
    return _host_call(
        body, ex, name=name, grid=(nsteps,),
        out_shape=[jax.ShapeDtypeStruct((t, dm), BF16), jax.ShapeDtypeStruct((t, n), BF16)],
        in_specs=_stream_specs(tm, dm) + [vec, vec, pl.BlockSpec((dm, n), lambda i: (0, 0))],
        out_specs=[pl.BlockSpec((tm, dm), lambda i: (i, 0)), pl.BlockSpec((tm, n), lambda i: (i, 0))],
        sem=("arbitrary",), args=[*x, s, sh, w])


def _halves(tm):
    half = tm // 2 if tm % 32 == 0 else tm
    return [slice(r0, r0 + half) for r0 in range(0, tm, half)]


def _ln_store(y, rows, xres_refs, g_ref, y_ref, zh_ref, rs_ref):
    y_ref[rows, :] = y.astype(BF16)
    z = DN_ALPHA * _stream(*xres_refs, rows=rows) + g_ref[...] * y
    mu = jnp.mean(z, axis=1, keepdims=True)
    zc = z - mu
    var = jnp.mean(zc * zc, axis=1, keepdims=True)
    rstd = lax.rsqrt(var + LN_EPS)
    zh_ref[rows, :] = zc * rstd
    rs_ref[rows, :] = jnp.broadcast_to(rstd, (zc.shape[0], rs_ref.shape[1]))


def _ln_out_shapes(t, dm):
    return [jax.ShapeDtypeStruct((t, dm), BF16), jax.ShapeDtypeStruct((t, dm), F32),
            jax.ShapeDtypeStruct((t, LANES), F32)]


def _ln_out_specs(tm, dm):
    row = pl.BlockSpec((tm, dm), lambda i: (i, 0))
    return [row, row, pl.BlockSpec((tm, LANES), lambda i: (i, 0))]


def proj_ln(a, w, xres, gate, name):
    t, k = a.shape
    dm = w.shape[1]
    tm = _row_tile(t, WIDE_ROW_TILE)

    def body(a_ref, w_ref, xz_ref, xg_ref, xb_ref, g_ref, y_ref, zh_ref, rs_ref):
        for rows in _halves(tm):
            y = _dot(a_ref[rows, :], w_ref[...])
            _ln_store(y, rows, (xz_ref, xg_ref, xb_ref), g_ref, y_ref, zh_ref, rs_ref)

    vec = pl.BlockSpec((1, dm), lambda i: (0, 0))
    return pl.pallas_call(
        body, name=name, grid=(t // tm,),
        out_shape=_ln_out_shapes(t, dm),
        in_specs=[pl.BlockSpec((tm, k), lambda i: (i, 0)), pl.BlockSpec((k, dm), lambda i: (0, 0))]
                 + _stream_specs(tm, dm) + [vec],
        out_specs=_ln_out_specs(tm, dm),
        compiler_params=_cp("parallel"),
    )(a, w, *xres, gate)


def swiglu_proj_ln(ab, w, xres, gate, name):
    t = ab.shape[0]
    f, dm = w.shape
    tm = _row_tile(t)

    def body(a_ref, b_ref, w_ref, xz_ref, xg_ref, xb_ref, g_ref, h_ref, y_ref, zh_ref, rs_ref):
        for rows in _halves(tm):
            y = None
            for c0 in range(0, f, FF_CHUNK):
                cols = slice(c0, c0 + FF_CHUNK)
                a = a_ref[rows, cols].astype(F32)
                h = (a * _sigmoid(a) * b_ref[rows, cols].astype(F32)).astype(BF16)
                h_ref[rows, cols] = h
                part = _dot(h, w_ref[cols, :])
                y = part if y is None else y + part
            _ln_store(y, rows, (xz_ref, xg_ref, xb_ref), g_ref, y_ref, zh_ref, rs_ref)

    vec = pl.BlockSpec((1, dm), lambda i: (0, 0))
    return pl.pallas_call(
        body, name=name, grid=(t // tm,),
        out_shape=[jax.ShapeDtypeStruct((t, f), BF16)] + _ln_out_shapes(t, dm),
        in_specs=[pl.BlockSpec((tm, f), lambda i: (i, 0)), pl.BlockSpec((tm, f), lambda i: (i, 1)),
                  pl.BlockSpec((f, dm), lambda i: (0, 0))] + _stream_specs(tm, dm) + [vec],
        out_specs=[pl.BlockSpec((tm, f), lambda i: (i, 0))] + _ln_out_specs(tm, dm),
        compiler_params=_cp("parallel"),
    )(ab, ab, w, *xres, gate)


class AttnCfg:
    def __init__(self, dil, heads, kv_heads, qc, kc, vc, max_dist, head0, sinks):
        self.dil, self.heads, self.kv_heads = dil, heads, kv_heads
        self.qc, self.kc, self.vc = qc, kc, vc
        self.max_dist, self.head0, self.sinks = max_dist, head0, sinks
        self.wq = heads * HEAD_DIM
        self.wk = kv_heads * HEAD_DIM
        self.wout = self.wq + 2 * self.wk


ATTN_A = AttnCfg(1, A_Q_HEADS, A_KV_HEADS, 0, A_Q_HEADS * HEAD_DIM, (A_Q_HEADS + A_KV_HEADS) * HEAD_DIM,
                 A_WINDOW - 1, 0, True)


def _attn_b_cfg(g):
    win, dil = B_GROUPS[g]
    cols = ((A_W + g * B_OUT_W, A_W + B_ALL + g * B_OUT_W, A_W + 2 * B_ALL + g * B_OUT_W) if dil == 1
            else (0, B_OUT_W, 2 * B_OUT_W))
    return AttnCfg(dil, B_HEADS_PER_GROUP, B_HEADS_PER_GROUP, *cols, win // dil,
                   A_Q_HEADS + g * B_HEADS_PER_GROUP, False)


ATTN_B = [_attn_b_cfg(g) for g in range(N_GROUPS)]


SCALE = HEAD_DIM ** -0.5


def _head(h):
    return slice(h * HEAD_DIM, (h + 1) * HEAD_DIM)


def _masked_bias(mask, distf, cfg, h, d):
    return jnp.where(mask, distf * (-(_slope(cfg.head0 + h) * d)), NEG_INF)


def _lane_halves(rows):
    lane = lax.broadcasted_iota(jnp.int32, (rows, LANES), 1)
    return [lane < HEAD_DIM, lane >= HEAD_DIM]


def _n_pair_sources(cfg):
    return cfg.kv_heads if cfg.heads > cfg.kv_heads else cfg.heads // 2


def _pair_source(p, grp):
    return p if grp == 1 else (2 * p) // grp


def _fill_pairs(dst, prev_ref, cur_ref, grp):
    for j in range(dst.shape[0]):
        for ref, rows in ((prev_ref, slice(0, BLOCK)), (cur_ref, slice(BLOCK, dst.shape[1]))):
            if grp == 1:
                dst[j, rows, :] = ref[:, j * LANES:(j + 1) * LANES]
            else:
                one = ref[:, _head(j)]
                dst[j, rows, :] = jnp.concatenate([one, one], axis=1)


def _band(i, max_dist):
    qi = lax.broadcasted_iota(jnp.int32, (BLOCK, 2 * BLOCK), 0)
    sj = lax.broadcasted_iota(jnp.int32, (BLOCK, 2 * BLOCK), 1)
    dist = qi + BLOCK - sj
    valid = (dist >= 0) & (dist <= max_dist)
    first_key = jnp.where(i > 0, 0, BLOCK)
    valid_first = valid & (sj >= first_key)
    return dist, valid, valid_first


def _attn_geometry(n, block):
    tq = min(block, n)
    return tq, tq // BLOCK, n // tq


def attn_fwd(qkv, cfg, sinks, name):
    d, n, _ = qkv.shape
    tq, nsub, nqb = _attn_geometry(n, ATTN_FWD_BLOCK)
    wq, wk = cfg.wq, cfg.wk
    grp = cfg.heads // cfg.kv_heads

    def body(sink_ref, q_ref, kc_ref, kp_ref, vc_ref, vp_ref, o_ref, l_ref, kf, vf):
        i = pl.program_id(1)
        _fill_pairs(kf, kp_ref, kc_ref, grp)
        _fill_pairs(vf, vp_ref, vc_ref, grp)
        dist, valid, valid_first = _band(i, cfg.max_dist)
        distf = dist.astype(F32)
        half_q, half_k = _lane_halves(BLOCK), _lane_halves(2 * BLOCK)
        rows = [slice(a * BLOCK, (a + 1) * BLOCK) for a in range(nsub)]
        wins = [slice(a * BLOCK, (a + 2) * BLOCK) for a in range(nsub)]
        for p in range(cfg.heads // 2):
            lanes = slice(p * LANES, (p + 1) * LANES)
            ki = _pair_source(p, grp)
            hs = (2 * p, 2 * p + 1)
            b_reg = [_masked_bias(valid, distf, cfg, h, d) for h in hs]
            b_first = [_masked_bias(valid_first, distf, cfg, h, d) for h in hs]
            ss = []
            for a in range(nsub):
                q2 = q_ref[rows[a], lanes] * SCALE
                k2 = kf[ki, wins[a], :]
                ss.append([_dot_nt(jnp.where(half_q[e], q2, 0), k2) + (b_first[e] if a == 0 else b_reg[e])
                           for e in range(2)])
            es, invs, lses = [], [], []
            for a in range(nsub):
                e_a, inv_a, lse_a = [], [], []
                for e in range(2):
                    m = jnp.max(ss[a][e], axis=1, keepdims=True)
                    if cfg.sinks:
                        m = jnp.maximum(m, sink_ref[hs[e]])
                    ex = jnp.exp(ss[a][e] - m)
                    den = jnp.sum(ex, axis=1, keepdims=True)
                    if cfg.sinks:
                        den = den + jnp.exp(sink_ref[hs[e]] - m)
                    e_a.append(ex.astype(BF16))
                    inv_a.append(1.0 / den)
                    lse_a.append(m + jnp.log(den))
                es.append(e_a)
                invs.append(inv_a)
                lses.append(lse_a)
            for a in range(nsub):
                v2 = vf[ki, wins[a], :]
                pcat = jnp.concatenate(es[a], axis=1)
                vcat = jnp.concatenate([jnp.where(half_k[e], v2, 0) for e in range(2)], axis=0)
                o = _dot(pcat, vcat) * jnp.where(half_q[0], invs[a][0], invs[a][1])
                o_ref[rows[a], lanes] = o.astype(BF16)
                l_ref[rows[a], lanes] = jnp.where(half_q[0], lses[a][0], lses[a][1])

    prev = lambda i: jnp.maximum(i * nsub - 1, 0)
    cur = lambda w, c: pl.BlockSpec((None, tq, w), lambda r, i: (r, i, c // w))
    prv = lambda w, c: pl.BlockSpec((None, BLOCK, w), lambda r, i: (r, prev(i), c // w))
    out = pl.BlockSpec((None, tq, wq), lambda r, i: (r, i, 0))
    pair_scratch = pltpu.VMEM((_n_pair_sources(cfg), tq + BLOCK, LANES), BF16)
    return pl.pallas_call(
        body, name=name, grid=(d, nqb),
        out_shape=[jax.ShapeDtypeStruct((d, n, wq), BF16), jax.ShapeDtypeStruct((d, n, wq), F32)],
        in_specs=[pl.BlockSpec(memory_space=pltpu.SMEM),
                  cur(wq, cfg.qc), cur(wk, cfg.kc), prv(wk, cfg.kc), cur(wk, cfg.vc), prv(wk, cfg.vc)],
        out_specs=[out, out],
        scratch_shapes=[pair_scratch, pair_scratch],
        compiler_params=_cp("parallel", "parallel"),
    )(sinks, qkv, qkv, qkv, qkv, qkv)


def mix_merge(ya, o_g, l_g, proj, w_a, w_b, name):
    t = ya.shape[0]
    dm = w_a.shape[1]
    tm = _row_tile(t, WIDE_ROW_TILE)
    gcol = GATE_COL // dm
    dils = [o.shape[0] for o in o_g]

    def body(ya_ref, o0, o1, o2, l0, l1, l2, ga_ref, gb_ref, wa_ref, wb_ref, yb_ref, mg_ref, *scrs):
        ls = [_unfold_from(l, scrs, d) for l, d in zip((l0, l1, l2), dils)]
        m = jnp.maximum(jnp.maximum(ls[0], ls[1]), ls[2])
        es = [jnp.exp(l - m) for l in ls]
        inv = 1.0 / (es[0] + es[1] + es[2])
        yb = sum(_unfold_from(o, scrs, d) * (e * inv) for o, e, d in zip((o0, o1, o2), es, dils)).astype(BF16)
        yb_ref[...] = yb
        pa = _dot(ya_ref[...], wa_ref[...])
        pb = _dot(yb, wb_ref[...])
        mg = _sigmoid(ga_ref[...].astype(F32)) * pa + _sigmoid(gb_ref[...].astype(F32)) * pb
        mg_ref[...] = mg.astype(BF16)

    wide = lambda w: pl.BlockSpec((tm, w), lambda i: (i, 0))
    folded = [_folded_spec(d, tm, B_OUT_W) for d in dils]
    return pl.pallas_call(
        body, name=name, grid=(t // tm,),
        out_shape=[jax.ShapeDtypeStruct((t, B_OUT_W), BF16), jax.ShapeDtypeStruct((t, dm), BF16)],
        in_specs=[wide(ya.shape[1])] + folded + folded
                 + [pl.BlockSpec((tm, dm), lambda i: (i, gcol)), pl.BlockSpec((tm, dm), lambda i: (i, gcol + 1)),
                    pl.BlockSpec(w_a.shape, lambda i: (0, 0)), pl.BlockSpec(w_b.shape, lambda i: (0, 0))],
        out_specs=[wide(B_OUT_W), wide(dm)],
        scratch_shapes=_fold_scratch(tm, B_OUT_W),
        compiler_params=_cp("parallel"),
    )(ya, *o_g, *l_g, proj, proj, w_a, w_b)


def loss_head(y, target):
    t, dm = y[0].shape
    tm = _row_tile(t, WIDE_ROW_TILE)

    def body(zh_ref, lg_ref, lb_ref, t_ref, dy_ref, loss_ref):
        @pl.when(pl.program_id(0) == 0)
        def _():
            loss_ref[...] = jnp.zeros_like(loss_ref)
        err = _stream(zh_ref, lg_ref, lb_ref) - t_ref[...]
        dy_ref[...] = err * (1.0 / dm)
        per_row = jnp.sum(err * err, axis=1, keepdims=True) * (1.0 / dm)
        loss_ref[...] += 0.5 * jnp.sum(per_row, axis=0, keepdims=True)

    row = pl.BlockSpec((tm, dm), lambda i: (i, 0))
    return pl.pallas_call(
        body, name="loss_head", grid=(t // tm,),
        out_shape=[jax.ShapeDtypeStruct((t, dm), F32), jax.ShapeDtypeStruct((8, LANES), F32)],
        in_specs=_stream_specs(tm, dm) + [row],
        out_specs=[row, pl.BlockSpec((8, LANES), lambda i: (0, 0))],
        compiler_params=_cp("arbitrary"),
    )(*y, target)


def _fold_rows(v):
    tm, c = v.shape
    return jnp.sum(v.reshape(tm // 8, 8, c), axis=0)


def _finish_sums(refs, nsteps):
    @pl.when(pl.program_id(0) == nsteps - 1)
    def _():
        for r in refs:
            r[...] = jnp.broadcast_to(jnp.sum(r[...], axis=0, keepdims=True), r.shape)


def ln_bwd(dxo, zhat, rstd, ysub, lg, gate, act, name):
    t, dm = dxo.shape
    k = act.shape[1]
    tm = _row_tile(t, WIDE_ROW_TILE if k <= dm else ROW_TILE)
    nsteps = t // tm
    ch = COL_CHUNK

    def body(dxo_ref, zh_ref, rs_ref, y_ref, lg_ref, g_ref, a_ref, dz_ref, dy_ref, sg_ref, sb_ref, sgate_ref,
             gw_ref, acc):
        @pl.when(pl.program_id(0) == 0)
        def _():
            for r in (sg_ref, sb_ref, sgate_ref, acc):
                r[...] = jnp.zeros_like(r)
        for rows in _halves(tm):
            dxo_v = dxo_ref[rows, :]
            zh = zh_ref[rows, :]
            dxh = dxo_v * lg_ref[...]
            m1 = jnp.mean(dxh, axis=1, keepdims=True)
            m2 = jnp.mean(dxh * zh, axis=1, keepdims=True)
            dz = rs_ref[rows, 0:1] * (dxh - m1 - zh * m2)
            dz_ref[rows, :] = dz
            dy = (g_ref[...] * dz).astype(BF16)
            dy_ref[rows, :] = dy
            sg_ref[...] += _fold_rows(dxo_v * zh)
            sb_ref[...] += _fold_rows(dxo_v)
            sgate_ref[...] += _fold_rows(dz * y_ref[rows, :].astype(F32))
            a = a_ref[rows, :]
            for c0 in range(0, dm, ch):
                acc[:, c0:c0 + ch] += _dot_tn(a, dy[:, c0:c0 + ch])
        _finish_sums((sg_ref, sb_ref, sgate_ref), nsteps)

        @pl.when(pl.program_id(0) == nsteps - 1)
        def _():
            gw_ref[...] = acc[...].astype(BF16)

    row = pl.BlockSpec((tm, dm), lambda i: (i, 0))
    vec = pl.BlockSpec((1, dm), lambda i: (0, 0))
    sums = pl.BlockSpec((8, dm), lambda i: (0, 0))
    return pl.pallas_call(
        body, name=name, grid=(nsteps,),
        out_shape=[jax.ShapeDtypeStruct((t, dm), F32), jax.ShapeDtypeStruct((t, dm), BF16)]
                  + [jax.ShapeDtypeStruct((8, dm), F32)] * 3 + [jax.ShapeDtypeStruct((k, dm), BF16)],
        in_specs=[row, row, pl.BlockSpec((tm, LANES), lambda i: (i, 0)), row, vec, vec,
                  pl.BlockSpec((tm, k), lambda i: (i, 0))],
        out_specs=[row, row, sums, sums, sums, pl.BlockSpec((k, dm), lambda i: (0, 0))],
        scratch_shapes=[pltpu.VMEM((k, dm), F32)],
        compiler_params=_cp("arbitrary"),
    )(dxo, zhat, rstd, ysub, lg, gate, act)


def _mod_bwd_store(du_of, dz_ref, x_refs, s_ref, dx_ref, ss_ref, ssh_ref, nsteps):
    @pl.when(pl.program_id(0) == 0)
    def _():
        ss_ref[...] = jnp.zeros_like(ss_ref)
        ssh_ref[...] = jnp.zeros_like(ssh_ref)
    for c0 in range(0, dx_ref.shape[1], DGRAD_CHUNK):
        cols = slice(c0, c0 + DGRAD_CHUNK)
        du = du_of(cols)
        dx_ref[:, cols] = DN_ALPHA * dz_ref[:, cols] + du * (1.0 + s_ref[:, cols])
        ss_ref[:, cols] += _fold_rows(du * _stream(*x_refs, cols=cols))
        ssh_ref[:, cols] += _fold_rows(du)
    _finish_sums((ss_ref, ssh_ref), nsteps)


def dgrad_ffn(g, wt, dz, xin, s, name, ex=None):
    t, dm = dz.shape
    k = g.shape[1]
    tm = _row_tile(t)
    nsteps = t // tm

    def body(*refs):
        i = pl.program_id(0)
        (g_ref, w_ref, dz_ref, xz_ref, xg_ref, xb_ref, s_ref,
         dx_ref, ss_ref, ssh_ref) = _hosted(ex, refs, 7, 3, i == 0, i == nsteps - 1)
        g_v = g_ref[...]
        _mod_bwd_store(lambda cols: _dot(g_v, w_ref[:, cols]), dz_ref, (xz_ref, xg_ref, xb_ref), s_ref,
                       dx_ref, ss_ref, ssh_ref, nsteps)

    row = pl.BlockSpec((tm, dm), lambda i: (i, 0))
    acc = pl.BlockSpec((8, dm), lambda i: (0, 0))
    return _host_call(
        body, ex, name=name, grid=(nsteps,),
        out_shape=[jax.ShapeDtypeStruct((t, dm), F32)] + [jax.ShapeDtypeStruct((8, dm), F32)] * 2,
        in_specs=[pl.BlockSpec((tm, k), lambda i: (i, 0)), pl.BlockSpec((k, dm), lambda i: (0, 0)), row]
                 + _stream_specs(tm, dm) + [pl.BlockSpec((1, dm), lambda i: (0, 0))],
        out_specs=[row, acc, acc], sem=("arbitrary",), args=[g, wt, dz, *xin, s])


def dswiglu(dy, wdt, ab, name):
    t, dm = dy.shape
    f = wdt.shape[1]
    tm = _row_tile(t)
    nsteps = t // tm
    nbuf = 3

    def body(dy_ref, w_ref, ab_hbm, o_ref, abuf, sems):
        i = pl.program_id(0)

        def fetch(tile):
            slot = tile % nbuf
            return pltpu.make_async_copy(ab_hbm.at[pl.ds(pl.multiple_of(tile * tm, tm), tm), :], abuf.at[slot],
                                         sems.at[slot])

        @pl.when(i == 0)
        def _():
            for tile in range(min(nbuf - 1, nsteps)):
                fetch(tile).start()

        @pl.when(i + nbuf - 1 < nsteps)
        def _():
            fetch(i + nbuf - 1).start()
        fetch(i).wait()
        ab_ref = abuf.at[i % nbuf]
        dy_v = dy_ref[...]
        for c0 in range(0, f, FF_CHUNK):
            cols = slice(c0, c0 + FF_CHUNK)
            dh = _dot(dy_v, w_ref[:, cols])
            a = ab_ref[:, cols].astype(F32)
            b = ab_ref[:, f + c0:f + c0 + FF_CHUNK].astype(F32)
            sg = _sigmoid(a)
            o_ref[:, cols] = (dh * b * (sg * (1.0 + a * (1.0 - sg)))).astype(BF16)
            o_ref[:, f + c0:f + c0 + FF_CHUNK] = (dh * (a * sg)).astype(BF16)

    return pl.pallas_call(
        body, name=name, grid=(nsteps,),
        out_shape=jax.ShapeDtypeStruct((t, 2 * f), BF16),
        in_specs=[pl.BlockSpec((tm, dm), lambda i: (i, 0)),
                  pl.BlockSpec((dm, f), lambda i: (0, 0), pipeline_mode=pl.Buffered(1)),
                  pl.BlockSpec(memory_space=pl.ANY)],
        out_specs=pl.BlockSpec((tm, 2 * f), lambda i: (i, 0)),
        scratch_shapes=[pltpu.VMEM((nbuf, tm, 2 * f), BF16), pltpu.SemaphoreType.DMA((nbuf,))],
        compiler_params=_cp("arbitrary"),
    )(dy, wdt, ab)


def dgrad_in(d_a, d_b, dgab, wt, dz, xin, s, name, ex=None):
    t, dm = dz.shape
    tm = _row_tile(t)
    nsteps = t // tm
    dils = [a.shape[0] for a in d_b]

    def body(*refs):
        i = pl.program_id(0)
        (da_ref, b0, b1, b2, dg_ref, w_ref, dz_ref, xz_ref, xg_ref, xb_ref, s_ref, dx_ref, ss_ref, ssh_ref,
         *scrs) = _hosted(ex, refs, 11, 3, i == 0, i == nsteps - 1)
        vs = [b_ref[...].reshape(tm, B_GW) for b_ref in (b0, b1, b2)]

        def du_of(cols):
            du = _dot(da_ref[0], w_ref[0:A_W, cols])
            for g, (v, d) in enumerate(zip(vs, dils)):
                part = None
                for p in range(3):
                    r0 = A_W + p * B_ALL + g * B_OUT_W
                    term = _dot(v[:, p * B_OUT_W:(p + 1) * B_OUT_W], w_ref[r0:r0 + B_OUT_W, cols])
                    part = term if part is None else part + term
                if d == 1:
                    du = du + part
                else:
                    n = tm // d
                    du = du + _unfold_rows(lambda r, cs: part[r * n:(r + 1) * n, cs], scrs, d, n, DGRAD_CHUNK)
            for j in range(2):
                du = du + _dot(dg_ref[:, j * dm:(j + 1) * dm], w_ref[GATE_COL + j * dm:GATE_COL + (j + 1) * dm, cols])
            return du

        _mod_bwd_store(du_of, dz_ref, (xz_ref, xg_ref, xb_ref), s_ref, dx_ref, ss_ref, ssh_ref, nsteps)

    row = pl.BlockSpec((tm, dm), lambda i: (i, 0))
    acc = pl.BlockSpec((8, dm), lambda i: (0, 0))
    return _host_call(
        body, ex, name=name, grid=(nsteps,),
        out_shape=[jax.ShapeDtypeStruct((t, dm), F32)] + [jax.ShapeDtypeStruct((8, dm), F32)] * 2,
        in_specs=[_folded_spec(1, tm, A_W)] + [_folded_spec(d, tm, B_GW) for d in dils]
                 + [pl.BlockSpec((tm, 2 * dm), lambda i: (i, 0)), pl.BlockSpec(wt.shape, lambda i: (0, 0)), row]
                 + _stream_specs(tm, dm) + [pl.BlockSpec((1, dm), lambda i: (0, 0))],
        out_specs=[row, acc, acc],
        scratch_shapes=_fold_scratch(tm, DGRAD_CHUNK), sem=("arbitrary",), args=[d_a, *d_b, dgab, wt, dz, *xin, s])


def wgrad(a, b, buf, tn, nj, b0, o0, om, name):
    t, k = a.shape
    tt = ROW_TILE
    while tt * 2 * k <= WGRAD_TILE_ELEMS and tt * 2 <= t:
        tt *= 2
    nsteps = t // tt
    last = nsteps - 1

    def body(a_ref, b_ref, buf_ref, o_ref, acc):
        s, j = pl.program_id(0), pl.program_id(1)

        @pl.when(s == 0)
        def _():
            acc[j] = jnp.zeros(acc.shape[1:], F32)
        acc[j] += _dot_tn(a_ref[...], b_ref[...])

        @pl.when(s == last)
        def _():
            o_ref[...] = acc[j].astype(BF16)

    return pl.pallas_call(
        body, name=name, grid=(nsteps, nj),
        out_shape=jax.ShapeDtypeStruct(buf.shape, buf.dtype),
        in_specs=[pl.BlockSpec((tt, k), lambda s, j: (s, 0)),
                  pl.BlockSpec((tt, tn), lambda s, j: (s, b0 + j)),
                  pl.BlockSpec(memory_space=pl.ANY)],
        out_specs=pl.BlockSpec((k, tn), lambda s, j: (0, o0 + om * jnp.where(s == last, j, 0))),
        scratch_shapes=[pltpu.VMEM((nj, k, tn), F32)],
        input_output_aliases={2: 0},
        compiler_params=_cp("arbitrary", "arbitrary"),
    )(a, b, buf)


def dmerge(do, wot, ya, yb, w_a, w_b, wat, wbt, proj, name):
    t, dm = do.shape
    tm = _row_tile(t, WIDE_ROW_TILE)
    nsteps = t // tm
    gcol = GATE_COL // dm
    ch = COL_CHUNK

    def body(do_ref, wot_ref, ya_ref, yb_ref, wa_ref, wb_ref, wat_ref, wbt_ref, g_ref,
             dya_ref, dyb_ref, dg_ref, gwa_ref, gwb_ref, dm_scr, acc_a, acc_b):
        i, j = pl.program_id(0), pl.program_id(1)

        @pl.when((i == 0) & (j == 0))
        def _():
            acc_a[...] = jnp.zeros_like(acc_a)
            acc_b[...] = jnp.zeros_like(acc_b)

        @pl.when(j == 0)
        def _():
            do_v = do_ref[...]
            for c0 in range(0, dm, ch):
                dm_scr[:, c0:c0 + ch] = _dot(do_v, wot_ref[:, c0:c0 + ch])

        def branch(y_ref, w_ref, wt_ref, dy_ref, acc, gw_ref):
            y = y_ref[...]
            dy = None
            for c0 in range(0, dm, ch):
                cols = slice(c0, c0 + ch)
                p = _dot(y, w_ref[:, cols])
                sg = _sigmoid(g_ref[:, cols].astype(F32))
                dmg = dm_scr[:, cols]
                dp = (dmg * sg).astype(BF16)
                dg_ref[:, cols] = (dmg * p * (sg * (1.0 - sg))).astype(BF16)
                acc[:, cols] += _dot_tn(y, dp)
                part = _dot(dp, wt_ref[cols, :])
                dy = part if dy is None else dy + part
            dy_ref[...] = dy.astype(dy_ref.dtype)

            @pl.when(i == nsteps - 1)
            def _():
                gw_ref[...] = acc[...].astype(BF16)

        pl.when(j == 0)(lambda: branch(ya_ref, wa_ref, wat_ref, dya_ref, acc_a, gwa_ref))
        pl.when(j == 1)(lambda: branch(yb_ref, wb_ref, wbt_ref, dyb_ref, acc_b, gwb_ref))

    full = lambda arr: pl.BlockSpec(arr.shape, lambda i, j: (0, 0))
    rowc = lambda w: pl.BlockSpec((tm, w), lambda i, j: (i, 0))
    return pl.pallas_call(
        body, name=name, grid=(nsteps, 2),
        out_shape=[jax.ShapeDtypeStruct((t, ya.shape[1]), BF16), jax.ShapeDtypeStruct((t, yb.shape[1]), F32),
                   jax.ShapeDtypeStruct((t, 2 * dm), BF16),
                   jax.ShapeDtypeStruct(w_a.shape, BF16), jax.ShapeDtypeStruct(w_b.shape, BF16)],
        in_specs=[rowc(dm), full(wot), rowc(ya.shape[1]), rowc(yb.shape[1]), full(w_a), full(w_b),
                  full(wat), full(wbt), pl.BlockSpec((tm, dm), lambda i, j: (i, gcol + j))],
        out_specs=[rowc(ya.shape[1]), rowc(yb.shape[1]), pl.BlockSpec((tm, dm), lambda i, j: (i, j)),
                   full(w_a), full(w_b)],
        scratch_shapes=[pltpu.VMEM((tm, dm), F32), pltpu.VMEM(w_a.shape, F32), pltpu.VMEM(w_b.shape, F32)],
        compiler_params=_cp("arbitrary", "arbitrary"),
    )(do, wot, ya, yb, w_a, w_b, wat, wbt, proj)


def mix_bwd(dyb, o_g, l_g, name):
    t, w = dyb.shape
    tm = _row_tile(t, WIDE_ROW_TILE)
    nh = w // HEAD_DIM
    dils = [o.shape[0] for o in o_g]

    def body(dyb_ref, o0, o1, o2, l0, l1, l2, do0, do1, do2, dl0, dl1, dl2, *scr):
        ls = [_unfold_from(l, scr, d) for l, d in zip((l0, l1, l2), dils)]
        m = jnp.maximum(jnp.maximum(ls[0], ls[1]), ls[2])
        es = [jnp.exp(l - m) for l in ls]
        inv = 1.0 / (es[0] + es[1] + es[2])
        wts = [e * inv for e in es]
        dyb_v = dyb_ref[...]
        dws = []
        for o_ref, do_ref, wt, d in zip((o0, o1, o2), (do0, do1, do2), wts, dils):
            prod = dyb_v * _unfold_from(o_ref, scr, d)
            _fold_to(do_ref, dyb_v * wt, scr, d)
            for h in range(nh):
                hs = slice(h * HEAD_DIM, (h + 1) * HEAD_DIM)
                dws.append(jnp.broadcast_to(jnp.sum(prod[:, hs], axis=1, keepdims=True), (tm, HEAD_DIM)))
        for g, (dl_ref, d) in enumerate(zip((dl0, dl1, dl2), dils)):
            cols = []
            for h in range(nh):
                hs = slice(h * HEAD_DIM, (h + 1) * HEAD_DIM)
                mean = sum(wts[g2][:, hs] * dws[g2 * nh + h] for g2 in range(N_GROUPS))
                cols.append(wts[g][:, hs] * (dws[g * nh + h] - mean))
            _fold_to(dl_ref, jnp.concatenate(cols, axis=1), scr, d)

    folded = [_folded_spec(d, tm, w) for d in dils]
    return pl.pallas_call(
        body, name=name, grid=(t // tm,),
        out_shape=[jax.ShapeDtypeStruct(o.shape, BF16) for o in o_g]
                  + [jax.ShapeDtypeStruct(o.shape, F32) for o in o_g],
        in_specs=[pl.BlockSpec((tm, w), lambda i: (i, 0))] + folded + folded,
        out_specs=folded + folded,
        scratch_shapes=_fold_scratch(tm, w),
        compiler_params=_cp("parallel"),
    )(dyb, *o_g, *l_g)


def attn_bwd(qkv, o, lse, do, dlse, cfg, sinks, name):
    d, n, _ = qkv.shape
    tq, nsub, nqb = _attn_geometry(n, ATTN_BWD_BLOCK)
    wq, wk, wout = cfg.wq, cfg.wk, cfg.wout
    grp = cfg.heads // cfg.kv_heads
    has_dl = dlse is not None

    def body(*refs):
        sink_ref, q_ref, qn_ref, kc_ref, kp_ref, vc_ref, vp_ref = refs[:7]
        o_ref, on_ref, do_ref, don_ref, l_ref, ln_ref = refs[7:13]
        rest = refs[13:]
        dl_ref = dln_ref = None
        if has_dl:
            dl_ref, dln_ref = rest[:2]
            rest = rest[2:]
        out_ref = rest[0]
        rest = rest[1:]
        if cfg.sinks:
            dsink_ref = rest[0]
            rest = rest[1:]
        kf, vf = rest
        r, i = pl.program_id(0), pl.program_id(1)
        _fill_pairs(kf, kp_ref, kc_ref, grp)
        _fill_pairs(vf, vp_ref, vc_ref, grp)
        dist, valid, valid_first = _band(i, cfg.max_dist)
        distf = dist.astype(F32)
        next_dist = jnp.where(i < nqb - 1, cfg.max_dist, -1)
        valid_next = (dist[:, 0:BLOCK] >= 0) & (dist[:, 0:BLOCK] <= next_dist)
        half_q = _lane_halves(BLOCK)
        if cfg.sinks:
            @pl.when((r == 0) & (i == 0))
            def _():
                dsink_ref[...] = jnp.zeros_like(dsink_ref)

        shared = {}
        for p in range(cfg.heads // 2):
            lanes = slice(p * LANES, (p + 1) * LANES)
            ki = _pair_source(p, grp)
            hs = (2 * p, 2 * p + 1)
            biases = [[_masked_bias(m, dd, cfg, h, d) for h in hs]
                      for m, dd in ((valid_first, distf), (valid, distf), (valid_next, distf[:, 0:BLOCK]))]
            tiles = []
            for a in range(nsub + 1):
                if a < nsub:
                    rows, win = slice(a * BLOCK, (a + 1) * BLOCK), slice(a * BLOCK, (a + 2) * BLOCK)
                    src = (q_ref, o_ref, do_ref, l_ref, dl_ref)
                else:
                    rows, win = slice(0, BLOCK), slice(nsub * BLOCK, (nsub + 1) * BLOCK)
                    src = (qn_ref, on_ref, don_ref, ln_ref, dln_ref)
                q2 = src[0][rows, lanes] * SCALE
                do2 = src[2][rows, lanes]
                o2 = src[1][rows, lanes].astype(F32)
                l2 = src[3][rows, lanes]
                k2, v2 = kf[ki, win, :], vf[ki, win, :]
                per = []
                for e in range(2):
                    qe, doe = jnp.where(half_q[e], q2, 0), jnp.where(half_q[e], do2, 0)
                    delta = jnp.sum(doe.astype(F32) * o2, axis=1, keepdims=True)
                    lse_v = jnp.max(jnp.where(half_q[e], l2, NEG_INF), axis=1, keepdims=True)
                    shift = -delta
                    if has_dl:
                        shift = shift + jnp.max(jnp.where(half_q[e], src[4][rows, lanes], NEG_INF), axis=1,
                                                keepdims=True)
                    s = _dot_nt(qe, k2) + biases[0 if a == 0 else (1 if a < nsub else 2)][e]
                    per.append((qe, doe, delta, lse_v, shift, s, _dot_nt(doe, v2)))
                tiles.append((k2, per))
            grads = []
            for k2, per in tiles:
                both = []
                for qe, doe, delta, lse_v, shift, s, dp in per:
                    pr = jnp.exp(s - lse_v)
                    both.append(((pr * (dp + shift)).astype(BF16), pr.astype(BF16)))
                grads.append(both)
            dkt, dvt = [], []
            for a, ((k2, per), both) in enumerate(zip(tiles, grads)):
                if a < nsub:
                    half_k = _lane_halves(k2.shape[0])
                    ds_cat = jnp.concatenate([both[0][0], both[1][0]], axis=1)
                    k_cat = jnp.concatenate([jnp.where(half_k[e], k2, 0) for e in range(2)], axis=0)
                    out_ref[a * BLOCK:(a + 1) * BLOCK, lanes] = (_dot(ds_cat, k_cat) * SCALE).astype(BF16)
                cut = (lambda x: x[:, BLOCK:]) if a == 0 else (lambda x: x)
                q_cat = jnp.concatenate([per[0][0], per[1][0]], axis=0)
                do_cat = jnp.concatenate([per[0][1], per[1][1]], axis=0)
                dkt.append(_dot_tn(q_cat, jnp.concatenate([cut(both[0][0]), cut(both[1][0])], axis=0)))
                dvt.append(_dot_tn(do_cat, jnp.concatenate([cut(both[0][1]), cut(both[1][1])], axis=0)))
                if cfg.sinks and a < nsub:
                    for e in range(2):
                        psink = jnp.exp(sink_ref[hs[e]] - per[e][3])
                        tot = jnp.sum(psink * (-per[e][2]), axis=0, keepdims=True)
                        dsink_ref[hs[e]:hs[e] + 1, :] += jnp.broadcast_to(tot, (1, LANES))
            for m in range(nsub):
                rows = slice(m * BLOCK, (m + 1) * BLOCK)
                for which, (acc, col0) in enumerate(((dkt, wq), (dvt, wq + wk))):
                    own = acc[m] if m == 0 else acc[m][:, BLOCK:]
                    total = own + acc[m + 1][:, 0:BLOCK]
                    if grp == 1:
                        out_ref[rows, col0 + p * LANES:col0 + (p + 1) * LANES] = total.T.astype(BF16)
                    else:
                        t64 = total[0:HEAD_DIM] + total[HEAD_DIM:]
                        key = (ki, which, m)
                        shared[key] = t64 + shared[key] if key in shared else t64
        for (ki, which, m), t64 in shared.items():
            col0 = (wq, wq + wk)[which] + ki * HEAD_DIM
            out_ref[m * BLOCK:(m + 1) * BLOCK, col0:col0 + HEAD_DIM] = t64.T.astype(BF16)

    prev = lambda i: jnp.maximum(i * nsub - 1, 0)
    nxt = lambda i: jnp.minimum((i + 1) * nsub, n // BLOCK - 1)
    cur = lambda w, c: pl.BlockSpec((None, tq, w), lambda r, i: (r, i, c // w))
    prv = lambda w, c: pl.BlockSpec((None, BLOCK, w), lambda r, i: (r, prev(i), c // w))
    o_cur = pl.BlockSpec((None, tq, wq), lambda r, i: (r, i, 0))
    o_nxt = pl.BlockSpec((None, BLOCK, wq), lambda r, i: (r, nxt(i), 0))
    in_specs = [pl.BlockSpec(memory_space=pltpu.SMEM),
                cur(wq, cfg.qc), pl.BlockSpec((None, BLOCK, wq), lambda r, i: (r, nxt(i), cfg.qc // wq)),
                cur(wk, cfg.kc), prv(wk, cfg.kc), cur(wk, cfg.vc), prv(wk, cfg.vc),
                o_cur, o_nxt, o_cur, o_nxt, o_cur, o_nxt]
    args = [sinks, qkv, qkv, qkv, qkv, qkv, qkv, o, o, do, do, lse, lse]
    if has_dl:
        in_specs += [o_cur, o_nxt]
        args += [dlse, dlse]
    out_shape = [jax.ShapeDtypeStruct((d, n, wout), BF16)]
    out_specs = [pl.BlockSpec((None, tq, wout), lambda r, i: (r, i, 0))]
    if cfg.sinks:
        out_shape.append(jax.ShapeDtypeStruct((8, LANES), F32))
        out_specs.append(pl.BlockSpec((8, LANES), lambda r, i: (0, 0)))
    pair_scratch = pltpu.VMEM((_n_pair_sources(cfg), tq + BLOCK, LANES), BF16)
    return pl.pallas_call(
        body, name=name, grid=(d, nqb), out_shape=out_shape, in_specs=in_specs, out_specs=out_specs,
        scratch_shapes=[pair_scratch, pair_scratch],
        compiler_params=_cp("arbitrary", "arbitrary"),
    )(*args)


def _adamw(g, w, m, v):
    m = ADAM_B1 * m + (1.0 - ADAM_B1) * g
    v = ADAM_B2 * v + (1.0 - ADAM_B2) * (g * g)
    m_hat = m / (1.0 - ADAM_B1 ** ADAM_STEP)
    v_hat = v / (1.0 - ADAM_B2 ** ADAM_STEP)
    delta = -ADAM_LR * (m_hat / (jnp.sqrt(v_hat) + ADAM_EPS) + ADAM_WD * w)
    return delta, m, v


def _adam_rows(r):
    if r <= ADAM_ROW_TILE:
        return r
    return next(rows for rows in range(ADAM_ROW_TILE, 0, -8) if r % rows == 0)


def adam_reduce(parts, w, m, v, name):
    r, c = w.shape
    tr = _adam_rows(r)

    def body(p_ref, w_ref, m_ref, v_ref, g_ref, d_ref, mo_ref, vo_ref):
        g = p_ref[0].astype(F32)
        for j in range(1, N_DEV):
            g = g + p_ref[j].astype(F32)
        g_ref[...] = g
        d_ref[...], mo_ref[...], vo_ref[...] = _adamw(g, w_ref[...], m_ref[...], v_ref[...])

    row = pl.BlockSpec((tr, c), lambda i: (i, 0))
    return pl.pallas_call(
        body, name=name, grid=(r // tr,),
        out_shape=[jax.ShapeDtypeStruct((r, c), F32)] * 4,
        in_specs=[pl.BlockSpec((N_DEV, tr, c), lambda i: (0, i, 0)), row, row, row],
        out_specs=[row] * 4,
        compiler_params=_cp("parallel"),
    )(parts, w, m, v)


def adam_layers(parts, w, m, v, name):
    nl, r, c = w.shape
    tr = _adam_rows(r)
    steps = r // tr

    def body(*refs):
        p_refs = refs[:nl]
        w_ref, m_ref, v_ref, g_ref, d_ref, mo_ref, vo_ref = refs[nl:]
        for k in range(nl):
            @pl.when(pl.program_id(0) == k)
            def _():
                g = p_refs[k][0].astype(F32)
                for j in range(1, N_DEV):
                    g = g + p_refs[k][j].astype(F32)
                g_ref[...] = g
                d_ref[...], mo_ref[...], vo_ref[...] = _adamw(g, w_ref[...], m_ref[...], v_ref[...])

    def part_spec(k):
        return pl.BlockSpec((N_DEV, tr, c), lambda l, i: (0, jnp.clip(i + (l - k) * steps, 0, steps - 1), 0))

    blk = pl.BlockSpec((None, tr, c), lambda l, i: (l, i, 0))
    return pl.pallas_call(
        body, name=name, grid=(nl, steps),
        out_shape=[jax.ShapeDtypeStruct((nl, r, c), F32)] * 4,
        in_specs=[part_spec(k) for k in range(nl)] + [blk, blk, blk],
        out_specs=[blk] * 4,
        compiler_params=_cp("arbitrary", "arbitrary"),
    )(*parts, w, m, v)


def adam_w_ada(sct, dm_loc, w, m, v):
    nl, dm, wc = w.shape
    tr = _row_tile(dm)

    def body(s_ref, d_ref, w_ref, m_ref, v_ref, g_ref, dl_ref, mo_ref, vo_ref):
        g = jnp.dot(s_ref[...], d_ref[...], preferred_element_type=F32, precision=lax.Precision.HIGHEST)
        g_ref[...] = g
        dl_ref[...], mo_ref[...], vo_ref[...] = _adamw(g, w_ref[...], m_ref[...], v_ref[...])

    blk = pl.BlockSpec((None, tr, wc), lambda l, i: (l, i, 0))
    return pl.pallas_call(
        body, name="adam_w_ada", grid=(nl, dm // tr),
        out_shape=[jax.ShapeDtypeStruct(w.shape, F32)] * 4,
        in_specs=[pl.BlockSpec((tr, LANES), lambda l, i: (i, 0)),
                  pl.BlockSpec((None, LANES, wc), lambda l, i: (l, 0, 0)), blk, blk, blk],
        out_specs=[blk] * 4,
        compiler_params=_cp("parallel", "parallel"),
    )(sct, dm_loc, w, m, v)


TRANSPOSED = ("w_gate", "w_up")


def _pieces(dm):
    ncol = lambda n: n // N_DEV
    mixer = ([Piece("w_in", "w_in", 1, 0, ncol(GATE_COL + 2 * dm)),
              Piece("w_a", "w_a", 1, 0, ncol(dm)),
              Piece("w_b", "w_b", 1, 0, ncol(dm)),
              Piece("w_o", "w_o", 0, 0, ncol(dm))],
             {"w_in": (dm, GATE_COL + 2 * dm), "w_a": (A_Q_HEADS * HEAD_DIM, dm), "w_b": (B_OUT_W, dm),
              "w_o": (dm, dm)})
    ffn = ([Piece("w_gate", "w_ffn_t", 0, 0, ncol(D_FF)),
            Piece("w_up", "w_ffn_t", 0, D_FF, ncol(D_FF)),
            Piece("w_down", "w_down", 0, 0, ncol(D_FF))],
           {"w_ffn_t": (2 * D_FF, dm), "w_down": (D_FF, dm)})
    return mixer, ffn


def kernel(x, c, w_ada, b_ada, w_in, sinks, w_a, w_b, w_o, ln1_g, ln1_b, w_gate, w_up, w_down, ln2_g, ln2_b, loss_target, m_w_ada, m_b_ada, m_w_in, m_sinks, m_w_a, m_w_b, m_w_o, m_ln1_g, m_ln1_b, m_w_gate, m_w_up, m_w_down, m_ln2_g, m_ln2_b, v_w_ada, v_b_ada, v_w_in, v_sinks, v_w_a, v_w_b, v_w_o, v_ln1_g, v_ln1_b, v_w_gate, v_w_up, v_w_down, v_ln2_g, v_ln2_b):
    given = dict(locals())
    nl = w_in.shape[0]
    t, dm = x.shape[1], x.shape[2]
    me = 4 * lax.axis_index("x") + 2 * lax.axis_index("y") + lax.axis_index("c")
    x0 = x.reshape(t, dm)
    target = loss_target.reshape(t, dm)

    groups = dict(zip(("mixer", "ffn"), _pieces(dm)))
    local = lambda nm, pre="": (given[pre + nm].transpose(0, 2, 1) if nm in TRANSPOSED else given[pre + nm])
    shards = {pc.name: local(pc.name).astype(BF16) for pcs, _ in groups.values() for pc in pcs}

    def gather(group, l):
        pcs, bufs = groups[group]
        return Exchange("gather", pcs, [shards[pc.name][l] for pc in pcs], bufs.values(), bufs)

    def scatter(group, gbuf):
        pcs, bufs = groups[group]
        return Exchange("scatter", pcs, [gbuf[nm] for nm in bufs],
                        [(N_DEV,) + shards[pc.name].shape[1:] for pc in pcs], bufs)

    mixer_pcs, mixer_bufs = groups["mixer"]
    groups["first"] = (mixer_pcs[:1], {"w_in": mixer_bufs["w_in"]})
    groups["rest"] = (mixer_pcs[1:] + groups["ffn"][0],
                      {**{k: v for k, v in mixer_bufs.items() if k != "w_in"}, **groups["ffn"][1]})
    full = [dict() for _ in range(nl)]
    full[0].update(zip(groups["first"][1], run_exchange(gather("first", 0), "gather_w_in")))

    wc = w_ada.shape[2]
    c_all = all_gather_small(jnp.broadcast_to(c, (8, dm)), "gather_c")[:, 0, :]
    b_loc = lax.dynamic_slice_in_dim(b_ada, me * wc, wc, axis=1).reshape(nl, 1, wc)
    mp, sc_all = mod_partial(c_all, w_ada, b_loc)
    mp_all = all_gather_small(mp.reshape(nl * N_DEV, wc), "gather_mod").reshape(N_DEV, nl, N_DEV, wc)
    mod = lax.dynamic_index_in_dim(mp_all, me, axis=2, keepdims=False)
    mod = mod.transpose(1, 0, 2).reshape(nl, 6, 1, dm)

    vec = lambda a, l: a[l].reshape(1, dm)

    saved = []
    xl = (x0, jnp.ones((1, dm), F32), jnp.zeros((1, dm), F32))
    for l in range(nl):
        sh1, s1, g1, sh2, s2, g2 = [mod[l, j] for j in range(6)]
        w = full[l]
        hosted = "rest" if l == 0 else "ffn"
        (u1, u1_f4, u1_f16, proj, qkv_f4, qkv_f16), got = in_proj(xl, s1, sh1, w["w_in"], "in_proj",
                                                                   gather(hosted, l))
        w.update(zip(groups[hosted][1], got))
        proj3 = proj.reshape(1, t, proj.shape[1])
        qkv_b = [proj3, qkv_f4, qkv_f16]
        ya, lse_a = attn_fwd(proj3, ATTN_A, sinks[l], "attn_a_fwd")
        o_g, l_g = [], []
        for g, cfg in enumerate(ATTN_B):
            o, ls = attn_fwd(qkv_b[g], cfg, sinks[l], "attn_b%d_fwd" % g)
            o_g.append(o)
            l_g.append(ls)
        yb, merged = mix_merge(ya[0], o_g, l_g, proj, w["w_a"], w["w_b"], "mix_merge")
        y1, zh1, rs1 = proj_ln(merged, w["w_o"], xl, g1, "out_proj_ln")
        x1 = (zh1, vec(ln1_g, l), vec(ln1_b, l))
        (u2, ab), got = modmm(x1, s2, sh2, w["w_ffn_t"].T, "ffn_up", gather("mixer", l + 1) if l + 1 < nl else None)
        if l + 1 < nl:
            full[l + 1].update(zip(groups["mixer"][1], got))
        h, y2, zh2, rs2 = swiglu_proj_ln(ab, w["w_down"], x1, g2, "ffn_down_ln")
        x2 = (zh2, vec(ln2_g, l), vec(ln2_b, l))
        saved.append(dict(xin=xl, u1=[u1, u1_f4.reshape(t, dm), u1_f16.reshape(t, dm)], proj=proj, qkv_b=qkv_b,
                          ya=ya, lse_a=lse_a, o_g=o_g, l_g=l_g, yb=yb, merged=merged,
                          y1=y1, x1=x1, zh1=zh1, rs1=rs1, u2=u2, ab=ab, h=h, y2=y2, zh2=zh2, rs2=rs2))
        xl = x2

    dx, loss_part = loss_head(xl, target)

    small = {k: [None] * nl for k in ("dmod", "ln1_g", "ln1_b", "ln2_g", "ln2_b", "sinks")}
    recv = {nm: [None] * nl for grp in groups.values() for nm in (pc.name for pc in grp[0])}

    def keep(group, l, got):
        for pc, arr in zip(groups[group][0], got):
            recv[pc.name][l] = arr

    for l in reversed(range(nl)):
        sv, w = saved[l], full[l]
        sh1, s1, g1, sh2, s2, g2 = [mod[l, j] for j in range(6)]
        fresh = lambda nm: lax.empty({**groups["mixer"][1], **groups["ffn"][1]}[nm], BF16)
        gbuf = {}
        dz2, dy2, sg, sb, sgate2, gbuf["w_down"] = ln_bwd(dx, sv["zh2"], sv["rs2"], sv["y2"], vec(ln2_g, l), g2,
                                                          sv["h"], "ln_bwd_ffn")
        small["ln2_g"][l], small["ln2_b"][l] = sg[0], sb[0]
        dab = dswiglu(dy2, w["w_down"].T, sv["ab"], "dswiglu")
        gbuf["w_ffn_t"] = wgrad(dab, sv["u2"], fresh("w_ffn_t"), WGRAD_COLS, dm // WGRAD_COLS, 0, 0, 1,
                                "wgrad_ffn_up")
        (dx1, ss2, ssh2), got = dgrad_ffn(dab, w["w_ffn_t"], dz2, sv["x1"], s2, "dgrad_ffn", scatter("ffn", gbuf))
        keep("ffn", l, got)
        dz1, do1, sg, sb, sgate1, gbuf["w_o"] = ln_bwd(dx1, sv["zh1"], sv["rs1"], sv["y1"], vec(ln1_g, l), g1,
                                                       sv["merged"], "ln_bwd_mixer")
        small["ln1_g"][l], small["ln1_b"][l] = sg[0], sb[0]
        dya, dyb, dgab, gbuf["w_a"], gbuf["w_b"] = dmerge(
            do1, w["w_o"].T, sv["ya"][0], sv["yb"], w["w_a"], w["w_b"], w["w_a"].T, w["w_b"].T, sv["proj"], "dmerge")
        mixed = mix_bwd(dyb, sv["o_g"], sv["l_g"], "mix_bwd")
        do_g, dl_g = mixed[:N_GROUPS], mixed[N_GROUPS:]
        d_a, dsink = attn_bwd(sv["qkv_b"][0], sv["ya"], sv["lse_a"], dya.reshape(1, t, -1), None, ATTN_A,
                              sinks[l], "attn_a_bwd")
        small["sinks"][l] = dsink[:, 0]
        d_b = [attn_bwd(sv["qkv_b"][g], sv["o_g"][g], sv["l_g"][g], do_g[g], dl_g[g], cfg, sinks[l],
                        "attn_b%d_bwd" % g)[0] for g, cfg in enumerate(ATTN_B)]
        gw = wgrad(sv["u1"][0], d_a.reshape(t, A_W), fresh("w_in"), A_W, 1, 0, 0, 1, "wgrad_in_a")
        for g in range(N_GROUPS):
            gw = wgrad(sv["u1"][g], d_b[g].reshape(t, B_GW), gw, B_OUT_W, 3, 0, A_W // B_OUT_W + g, N_GROUPS,
                       "wgrad_in_b%d" % g)
        gbuf["w_in"] = wgrad(sv["u1"][0], dgab, gw, WGRAD_COLS, 2 * dm // WGRAD_COLS, 0, GATE_COL // WGRAD_COLS, 1,
                             "wgrad_in_gate")
        (dx, ss1, ssh1), got = dgrad_in(d_a, d_b, dgab, w["w_in"].T, dz1, sv["xin"], s1, "dgrad_in",
                                        scatter("mixer", gbuf))
        keep("mixer", l, got)
        small["dmod"][l] = jnp.stack([ssh1[0], ss1[0], sgate1[0], ssh2[0], ss2[0], sgate2[0]])
    grad_x = dx.reshape(x.shape)

    big_out = {}
    for nm, parts in recv.items():
        outs = adam_layers(parts, local(nm), local(nm, "m_"), local(nm, "v_"), "adam_" + nm)
        big_out[nm] = [o.transpose(0, 2, 1) for o in outs] if nm in TRANSPOSED else outs

    rows = jnp.concatenate(
        [jnp.stack(small["dmod"]).reshape(nl * 6, dm)]
        + [jnp.stack(small[k]) for k in ("ln1_g", "ln1_b", "ln2_g", "ln2_b")]
        + [jnp.pad(jnp.stack(small["sinks"]).reshape(1, -1), ((0, 0), (0, dm - nl * A_Q_HEADS))),
           jnp.broadcast_to(loss_part[0:1, 0:1], (1, dm))])
    n_rows = rows.shape[0]
    rows = jnp.pad(rows, ((0, -n_rows % 8), (0, 0)))
    rows_all = all_gather_small(rows, "gather_small_grads")

    def pack_small(pre):
        parts = [given[pre + "b_ada"].reshape(nl * 6, dm)]
        parts += [given[pre + k] for k in ("ln1_g", "ln1_b", "ln2_g", "ln2_b")]
        parts.append(jnp.pad(given[pre + "sinks"].reshape(1, -1), ((0, 0), (0, dm - nl * A_Q_HEADS))))
        p = jnp.concatenate(parts)
        return jnp.pad(p, ((0, rows.shape[0] - p.shape[0]), (0, 0)))

    souts = adam_reduce(rows_all, pack_small(""), pack_small("m_"), pack_small("v_"), "adam_small")

    def unpack_small(o):
        r = {"b_ada": o[0:nl * 6].reshape(nl, 6 * dm)}
        for j, k in enumerate(("ln1_g", "ln1_b", "ln2_g", "ln2_b")):
            r[k] = o[nl * 6 + j * nl: nl * 6 + (j + 1) * nl]
        r["sinks"] = o[nl * 10, 0:nl * A_Q_HEADS].reshape(nl, A_Q_HEADS)
        return r

    small_out = [unpack_small(o) for o in souts]
    loss = souts[0][nl * 10 + 1, 0]

    dmod_all = rows_all[:, 0:nl * 6].reshape(N_DEV, nl, 6 * dm)
    dm_loc = lax.dynamic_slice_in_dim(dmod_all, me * wc, wc, axis=2).transpose(1, 0, 2)
    dm_loc = jnp.pad(dm_loc, ((0, 0), (0, LANES - N_DEV), (0, 0)))
    sct = jnp.pad(sc_all.T, ((0, 0), (0, LANES - N_DEV)))
    ada_out = adam_w_ada(sct, dm_loc, w_ada, m_w_ada, v_w_ada)

    names = ["w_ada", "b_ada", "w_in", "sinks", "w_a", "w_b", "w_o", "ln1_g", "ln1_b",
             "w_gate", "w_up", "w_down", "ln2_g", "ln2_b"]

    def pick(kind, nm):
        if nm == "w_ada":
            return ada_out[kind]
        if nm in small_out[kind]:
            return small_out[kind][nm]
        return big_out[nm][kind]

    result = [loss, grad_x]
    for kind in range(4):
        result += [pick(kind, nm) for nm in names]
    return tuple(result)
```
